```python
import math
import jax, jax.numpy as jnp
from jax import lax
import numpy as np

D_MODEL = 1024
BATCH = 8
SEQ = 4096
DEPTH = 2

N_MEM = 256
MIX_WIDTH = D_MODEL
ML_HEADS = 4
ML_WIDTH = MIX_WIDTH // 2
ML_HEAD_DIM = ML_WIDTH // ML_HEADS
ML_CHUNK = 64
ML_CONV = 4
SWA_HEAD_DIM = 64
SWA_WIDTH = MIX_WIDTH - ML_WIDTH
SWA_HEADS = SWA_WIDTH // SWA_HEAD_DIM
SWA_KV_HEADS = SWA_HEADS // 4
SWA_GROUP = SWA_HEADS // SWA_KV_HEADS
SWA_KV_WIDTH = SWA_KV_HEADS * SWA_HEAD_DIM
WINDOW = 128
BLOCK = 128
REL_BUCKETS = 32
REL_MAX_DIST = 128
XA_HEADS = 4
XA_HEAD_DIM = D_MODEL // XA_HEADS
D_FF = 256 * ((8 * D_MODEL // 3 + 255) // 256)
FFN_CONV = 3
ALPHA = (2.0 * DEPTH) ** 0.25
BETA = (8.0 * DEPTH) ** -0.25
EPS = 1e-5
IN_SPLITS = (2 * ML_WIDTH, 3 * ML_WIDTH, 4 * ML_WIDTH, 4 * ML_WIDTH + ML_HEADS, 4 * ML_WIDTH + 2 * ML_HEADS, 4 * ML_WIDTH + 2 * ML_HEADS + SWA_WIDTH, 4 * ML_WIDTH + 2 * ML_HEADS + SWA_WIDTH + SWA_KV_WIDTH)
N_IN = IN_SPLITS[-1] + SWA_KV_WIDTH

kernel_name = "hybrid_mlstm_swa_deepnorm_block"

f32 = jnp.float32


def layer_norm(x, g, b):
    xf = x.astype(f32)
    mu = xf.mean(-1, keepdims=True)
    var = jnp.square(xf - mu).mean(-1, keepdims=True)
    return ((xf - mu) * lax.rsqrt(var + EPS) * g.astype(f32) + b.astype(f32)).astype(x.dtype)


def causal_dwconv(x, w, b):
    K = w.shape[0]
    S = x.shape[1]
    xp = jnp.pad(x, ((0, 0), (K - 1, 0), (0, 0)))
    y = b + xp[:, 0:S] * w[0]
    for j in range(1, K):
        y = y + xp[:, j:j + S] * w[j]
    return y


def t5_bucket(dist):
    n = jnp.maximum(dist, 0)
    max_exact = REL_BUCKETS // 2
    nf = jnp.maximum(n, 1).astype(f32)
    large = max_exact + (jnp.log(nf / max_exact) / math.log(REL_MAX_DIST / max_exact) * (REL_BUCKETS - max_exact)).astype(jnp.int32)
    large = jnp.minimum(large, REL_BUCKETS - 1)
    return jnp.where(n < max_exact, n, large)


def mlstm(q, k, v, o_pre, i_pre, f_pre, norm_g):
    B, S, _ = q.shape
    nc = S // ML_CHUNK
    L = ML_CHUNK

    def heads(t):
        return t.astype(f32).reshape(B, nc, L, ML_HEADS, ML_HEAD_DIM).transpose(1, 0, 3, 2, 4)

    def gates(t):
        return t.astype(f32).reshape(B, nc, L, ML_HEADS).transpose(1, 0, 3, 2)

    qc = heads(q)
    kc = heads(k) * (ML_HEAD_DIM ** -0.5)
    vc = heads(v)
    ic = gates(i_pre)
    lfc = jax.nn.log_sigmoid(gates(f_pre))
    causal = jnp.tril(jnp.ones((L, L), dtype=bool))

    def step(carry, inp):
        C, n, m = carry
        qb, kb, vb, ig, lf = inp
        b = jnp.cumsum(lf, axis=-1)
        Dm = jnp.where(causal, b[..., :, None] - b[..., None, :] + ig[..., None, :], -jnp.inf)
        inter = b + m[..., None]
        m_t = jnp.maximum(inter, Dm.max(-1))
        w_inter = jnp.exp(inter - m_t)
        s = jnp.einsum('bhtd,bhsd->bhts', qb, kb) * jnp.exp(Dm - m_t[..., None])
        num = w_inter[..., None] * jnp.einsum('bhtd,bhde->bhte', qb, C) + jnp.einsum('bhts,bhse->bhte', s, vb)
        den = w_inter * jnp.einsum('bhtd,bhd->bht', qb, n) + s.sum(-1)
        h = num / jnp.maximum(jnp.abs(den), jnp.exp(-m_t))[..., None]
        g = b[..., -1]
        a = g[..., None] - b + ig
        m_new = jnp.maximum(g + m, a.max(-1))
        decay = jnp.exp(g + m - m_new)
        wk = jnp.exp(a - m_new[..., None])
        C_new = decay[..., None, None] * C + jnp.einsum('bhsd,bhse->bhde', kb * wk[..., None], vb)
        n_new = decay[..., None] * n + jnp.einsum('bhs,bhsd->bhd', wk, kb)
        return (C_new, n_new, m_new), h

    init = (jnp.zeros((B, ML_HEADS, ML_HEAD_DIM, ML_HEAD_DIM), f32),
            jnp.zeros((B, ML_HEADS, ML_HEAD_DIM), f32),
            jnp.zeros((B, ML_HEADS), f32))
    _, h = lax.scan(step, init, (qc, kc, vc, ic, lfc))
    mu = h.mean(-1, keepdims=True)
    var = jnp.square(h - mu).mean(-1, keepdims=True)
    hn = ((h - mu) * lax.rsqrt(var + EPS)).transpose(1, 0, 3, 2, 4).reshape(B, S, ML_WIDTH)
    hn = hn * norm_g.astype(f32)
    return (jax.nn.sigmoid(o_pre.astype(f32)) * hn).astype(q.dtype)


def sliding_window_attention(q, k, v, sinks, rel_bias):
    B, S = q.shape[:2]
    nb = S // BLOCK
    qb = q.reshape(B, nb, BLOCK, SWA_KV_HEADS, SWA_GROUP, SWA_HEAD_DIM)

    def band(t):
        tb = t.reshape(B, nb, BLOCK, SWA_KV_HEADS, SWA_HEAD_DIM)
        prev = jnp.pad(tb, ((0, 0), (1, 0), (0, 0), (0, 0), (0, 0)))[:, :-1]
        return jnp.concatenate([prev, tb], axis=2)

    kb, vb = band(k), band(v)
    logits = jnp.einsum('bnqhgd,bnkhd->bnhgqk', qb, kb).astype(f32) * (SWA_HEAD_DIM ** -0.5)
    r = jnp.arange(BLOCK)[:, None]
    c = jnp.arange(2 * BLOCK)[None, :]
    dist = BLOCK + r - c
    bias = rel_bias.astype(f32)[t5_bucket(dist)]
    bias = bias.transpose(2, 0, 1).reshape(SWA_KV_HEADS, SWA_GROUP, BLOCK, 2 * BLOCK)
    kpos = jnp.arange(nb)[:, None, None] * BLOCK - BLOCK + c[None]
    valid = (dist >= 0) & (dist < WINDOW) & (kpos >= 0)
    logits = jnp.where(valid[None, :, None, None], logits + bias, -jnp.inf)
    sink = sinks.astype(f32).reshape(SWA_KV_HEADS, SWA_GROUP)[None, None, :, :, None, None]
    mx = jnp.maximum(logits.max(-1, keepdims=True), sink)
    p = jnp.exp(logits - mx)
    probs = (p / (p.sum(-1, keepdims=True) + jnp.exp(sink - mx))).astype(v.dtype)
    out = jnp.einsum('bnhgqk,bnkhd->bnqhgd', probs, vb)
    return out.reshape(B, S, SWA_WIDTH)


def hybrid_mixer(x, w_in, ml_conv_w, ml_conv_b, ml_i_bias, ml_f_bias, ml_norm_g, swa_sinks, rel_bias, w_out):
    B, S, _ = x.shape
    proj = x @ w_in
    ml_qk, ml_v, ml_o, ml_i, ml_f, sw_q, sw_k, sw_v = jnp.split(proj, IN_SPLITS, axis=-1)
    ml_qk = jax.nn.silu(causal_dwconv(ml_qk, ml_conv_w, ml_conv_b))
    ml_q, ml_k = jnp.split(ml_qk, 2, axis=-1)
    h_ml = mlstm(ml_q, ml_k, ml_v, ml_o, ml_i + ml_i_bias, ml_f + ml_f_bias, ml_norm_g)
    h_sw = sliding_window_attention(sw_q.reshape(B, S, SWA_HEADS, SWA_HEAD_DIM),
                                    sw_k.reshape(B, S, SWA_KV_HEADS, SWA_HEAD_DIM),
                                    sw_v.reshape(B, S, SWA_KV_HEADS, SWA_HEAD_DIM),
                                    swa_sinks, rel_bias)
    return jnp.concatenate([h_ml, h_sw], axis=-1) @ w_out


def memory_cross_attention(x, mem, wq, wkv, wo):
    B, S, _ = x.shape
    M = mem.shape[1]
    q = (x @ wq).reshape(B, S, XA_HEADS, XA_HEAD_DIM)
    k, v = jnp.split(mem @ wkv, 2, axis=-1)
    k = k.reshape(B, M, XA_HEADS, XA_HEAD_DIM)
    v = v.reshape(B, M, XA_HEADS, XA_HEAD_DIM)
    logits = jnp.einsum('bshd,bmhd->bhsm', q, k).astype(f32) * (XA_HEAD_DIM ** -0.5)
    p = jax.nn.softmax(logits, axis=-1).astype(x.dtype)
    o = jnp.einsum('bhsm,bmhd->bshd', p, v).reshape(B, S, D_MODEL)
    return o @ wo


def conv_ffn(x, w_up, conv_w, conv_b, w_down):
    u = causal_dwconv(x @ w_up, conv_w, conv_b)
    g, val = jnp.split(u, 2, axis=-1)
    return (jax.nn.gelu(g) * val) @ w_down


def _fwd_setup_inputs(seed: int = 0) -> dict:
    key = jax.random.key(seed)
    ks = jax.random.split(key, 24)
    nrm = lambda k, shape, s: jax.random.normal(k, shape, f32) * s
    L = DEPTH
    return {
        'x': nrm(ks[0], (BATCH, SEQ, D_MODEL), 1.0),
        'mem': nrm(ks[1], (BATCH, N_MEM, D_MODEL), 1.0),
        'rel_bias': nrm(ks[2], (REL_BUCKETS, SWA_HEADS), 0.5),
        'w_in': nrm(ks[3], (L, D_MODEL, N_IN), D_MODEL ** -0.5),
        'ml_conv_w': nrm(ks[4], (L, ML_CONV, 2 * ML_WIDTH), ML_CONV ** -0.5),
        'ml_conv_b': nrm(ks[5], (L, 2 * ML_WIDTH), 0.02),
        'ml_i_bias': nrm(ks[6], (L, ML_HEADS), 0.1),
        'ml_f_bias': jnp.linspace(3.0, 6.0, ML_HEADS, dtype=f32)[None, :] + nrm(ks[7], (L, ML_HEADS), 0.1),
        'ml_norm_g': 1.0 + nrm(ks[8], (L, ML_WIDTH), 0.05),
        'swa_sinks': nrm(ks[9], (L, SWA_HEADS), 0.5),
        'w_out': nrm(ks[10], (L, MIX_WIDTH, D_MODEL), BETA * MIX_WIDTH ** -0.5),
        'ln1_g': 1.0 + nrm(ks[11], (L, D_MODEL), 0.05),
        'ln1_b': nrm(ks[12], (L, D_MODEL), 0.02),
        'xa_wq': nrm(ks[13], (L, D_MODEL, D_MODEL), D_MODEL ** -0.5),
        'xa_wkv': nrm(ks[14], (L, D_MODEL, 2 * D_MODEL), D_MODEL ** -0.5),
        'xa_wo': nrm(ks[15], (L, D_MODEL, D_MODEL), BETA * D_MODEL ** -0.5),
        'ln2_g': 1.0 + nrm(ks[16], (L, D_MODEL), 0.05),
        'ln2_b': nrm(ks[17], (L, D_MODEL), 0.02),
        'ffn_w_up': nrm(ks[18], (L, D_MODEL, 2 * D_FF), D_MODEL ** -0.5),
        'ffn_conv_w': nrm(ks[19], (L, FFN_CONV, 2 * D_FF), FFN_CONV ** -0.5),
        'ffn_conv_b': nrm(ks[20], (L, 2 * D_FF), 0.02),
        'ffn_w_down': nrm(ks[21], (L, D_FF, D_MODEL), BETA * D_FF ** -0.5),
        'ln3_g': 1.0 + nrm(ks[22], (L, D_MODEL), 0.05),
        'ln3_b': nrm(ks[23], (L, D_MODEL), 0.02),
    }


def _fwd_reference(x, mem, rel_bias, w_in, ml_conv_w, ml_conv_b, ml_i_bias, ml_f_bias, ml_norm_g, swa_sinks, w_out, ln1_g, ln1_b, xa_wq, xa_wkv, xa_wo, ln2_g, ln2_b, ffn_w_up, ffn_conv_w, ffn_conv_b, ffn_w_down, ln3_g, ln3_b):
    for l in range(DEPTH):
        h = hybrid_mixer(x, w_in[l], ml_conv_w[l], ml_conv_b[l], ml_i_bias[l], ml_f_bias[l], ml_norm_g[l], swa_sinks[l], rel_bias, w_out[l])
        x = layer_norm(ALPHA * x + h, ln1_g[l], ln1_b[l])
        h = memory_cross_attention(x, mem, xa_wq[l], xa_wkv[l], xa_wo[l])
        x = layer_norm(ALPHA * x + h, ln2_g[l], ln2_b[l])
        h = conv_ffn(x, ffn_w_up[l], ffn_conv_w[l], ffn_conv_b[l], ffn_w_down[l])
        x = layer_norm(ALPHA * x + h, ln3_g[l], ln3_b[l])
    return x


import jax as _jax
import jax.numpy as _jnp

TWIN_FORMAT = 'train_step'
FWD_PARAMS = ['x', 'mem', 'rel_bias', 'w_in', 'ml_conv_w', 'ml_conv_b', 'ml_i_bias', 'ml_f_bias', 'ml_norm_g', 'swa_sinks', 'w_out', 'ln1_g', 'ln1_b', 'xa_wq', 'xa_wkv', 'xa_wo', 'ln2_g', 'ln2_b', 'ffn_w_up', 'ffn_conv_w', 'ffn_conv_b', 'ffn_w_down', 'ln3_g', 'ln3_b']
TWIN_WEIGHTS = ['rel_bias', 'w_in', 'ml_conv_w', 'ml_conv_b', 'ml_i_bias', 'ml_f_bias', 'ml_norm_g', 'swa_sinks', 'w_out', 'ln1_g', 'ln1_b', 'xa_wq', 'xa_wkv', 'xa_wo', 'ln2_g', 'ln2_b', 'ffn_w_up', 'ffn_conv_w', 'ffn_conv_b', 'ffn_w_down', 'ln3_g', 'ln3_b']
TWIN_DIFF_INPUT = 'x'
TWIN_INPUTS = ['x', 'mem', 'rel_bias', 'w_in', 'ml_conv_w', 'ml_conv_b', 'ml_i_bias', 'ml_f_bias', 'ml_norm_g', 'swa_sinks', 'w_out', 'ln1_g', 'ln1_b', 'xa_wq', 'xa_wkv', 'xa_wo', 'ln2_g', 'ln2_b', 'ffn_w_up', 'ffn_conv_w', 'ffn_conv_b', 'ffn_w_down', 'ln3_g', 'ln3_b', 'loss_target', 'm_rel_bias', 'm_w_in', 'm_ml_conv_w', 'm_ml_conv_b', 'm_ml_i_bias', 'm_ml_f_bias', 'm_ml_norm_g', 'm_swa_sinks', 'm_w_out', 'm_ln1_g', 'm_ln1_b', 'm_xa_wq', 'm_xa_wkv', 'm_xa_wo', 'm_ln2_g', 'm_ln2_b', 'm_ffn_w_up', 'm_ffn_conv_w', 'm_ffn_conv_b', 'm_ffn_w_down', 'm_ln3_g', 'm_ln3_b', 'v_rel_bias', 'v_w_in', 'v_ml_conv_w', 'v_ml_conv_b', 'v_ml_i_bias', 'v_ml_f_bias', 'v_ml_norm_g', 'v_swa_sinks', 'v_w_out', 'v_ln1_g', 'v_ln1_b', 'v_xa_wq', 'v_xa_wkv', 'v_xa_wo', 'v_ln2_g', 'v_ln2_b', 'v_ffn_w_up', 'v_ffn_conv_w', 'v_ffn_conv_b', 'v_ffn_w_down', 'v_ln3_g', 'v_ln3_b']
TWIN_OUTPUTS = ['loss', 'grad_x', 'grad_rel_bias', 'grad_w_in', 'grad_ml_conv_w', 'grad_ml_conv_b', 'grad_ml_i_bias', 'grad_ml_f_bias', 'grad_ml_norm_g', 'grad_swa_sinks', 'grad_w_out', 'grad_ln1_g', 'grad_ln1_b', 'grad_xa_wq', 'grad_xa_wkv', 'grad_xa_wo', 'grad_ln2_g', 'grad_ln2_b', 'grad_ffn_w_up', 'grad_ffn_conv_w', 'grad_ffn_conv_b', 'grad_ffn_w_down', 'grad_ln3_g', 'grad_ln3_b', 'delta_rel_bias', 'delta_w_in', 'delta_ml_conv_w', 'delta_ml_conv_b', 'delta_ml_i_bias', 'delta_ml_f_bias', 'delta_ml_norm_g', 'delta_swa_sinks', 'delta_w_out', 'delta_ln1_g', 'delta_ln1_b', 'delta_xa_wq', 'delta_xa_wkv', 'delta_xa_wo', 'delta_ln2_g', 'delta_ln2_b', 'delta_ffn_w_up', 'delta_ffn_conv_w', 'delta_ffn_conv_b', 'delta_ffn_w_down', 'delta_ln3_g', 'delta_ln3_b', 'new_m_rel_bias', 'new_m_w_in', 'new_m_ml_conv_w', 'new_m_ml_conv_b', 'new_m_ml_i_bias', 'new_m_ml_f_bias', 'new_m_ml_norm_g', 'new_m_swa_sinks', 'new_m_w_out', 'new_m_ln1_g', 'new_m_ln1_b', 'new_m_xa_wq', 'new_m_xa_wkv', 'new_m_xa_wo', 'new_m_ln2_g', 'new_m_ln2_b', 'new_m_ffn_w_up', 'new_m_ffn_conv_w', 'new_m_ffn_conv_b', 'new_m_ffn_w_down', 'new_m_ln3_g', 'new_m_ln3_b', 'new_v_rel_bias', 'new_v_w_in', 'new_v_ml_conv_w', 'new_v_ml_conv_b', 'new_v_ml_i_bias', 'new_v_ml_f_bias', 'new_v_ml_norm_g', 'new_v_swa_sinks', 'new_v_w_out', 'new_v_ln1_g', 'new_v_ln1_b', 'new_v_xa_wq', 'new_v_xa_wkv', 'new_v_xa_wo', 'new_v_ln2_g', 'new_v_ln2_b', 'new_v_ffn_w_up', 'new_v_ffn_conv_w', 'new_v_ffn_conv_b', 'new_v_ffn_w_down', 'new_v_ln3_g', 'new_v_ln3_b']
TWIN_LEAF_KINDS = {'loss': 'loss', 'grad_x': 'grad_x', 'grad_rel_bias': 'grad_w', 'grad_w_in': 'grad_w', 'grad_ml_conv_w': 'grad_w', 'grad_ml_conv_b': 'grad_w', 'grad_ml_i_bias': 'grad_w', 'grad_ml_f_bias': 'grad_w', 'grad_ml_norm_g': 'grad_w', 'grad_swa_sinks': 'grad_w', 'grad_w_out': 'grad_w', 'grad_ln1_g': 'grad_w', 'grad_ln1_b': 'grad_w', 'grad_xa_wq': 'grad_w', 'grad_xa_wkv': 'grad_w', 'grad_xa_wo': 'grad_w', 'grad_ln2_g': 'grad_w', 'grad_ln2_b': 'grad_w', 'grad_ffn_w_up': 'grad_w', 'grad_ffn_conv_w': 'grad_w', 'grad_ffn_conv_b': 'grad_w', 'grad_ffn_w_down': 'grad_w', 'grad_ln3_g': 'grad_w', 'grad_ln3_b': 'grad_w', 'delta_rel_bias': 'delta_w', 'delta_w_in': 'delta_w', 'delta_ml_conv_w': 'delta_w', 'delta_ml_conv_b': 'delta_w', 'delta_ml_i_bias': 'delta_w', 'delta_ml_f_bias': 'delta_w', 'delta_ml_norm_g': 'delta_w', 'delta_swa_sinks': 'delta_w', 'delta_w_out': 'delta_w', 'delta_ln1_g': 'delta_w', 'delta_ln1_b': 'delta_w', 'delta_xa_wq': 'delta_w', 'delta_xa_wkv': 'delta_w', 'delta_xa_wo': 'delta_w', 'delta_ln2_g': 'delta_w', 'delta_ln2_b': 'delta_w', 'delta_ffn_w_up': 'delta_w', 'delta_ffn_conv_w': 'delta_w', 'delta_ffn_conv_b': 'delta_w', 'delta_ffn_w_down': 'delta_w', 'delta_ln3_g': 'delta_w', 'delta_ln3_b': 'delta_w', 'new_m_rel_bias': 'new_m', 'new_m_w_in': 'new_m', 'new_m_ml_conv_w': 'new_m', 'new_m_ml_conv_b': 'new_m', 'new_m_ml_i_bias': 'new_m', 'new_m_ml_f_bias': 'new_m', 'new_m_ml_norm_g': 'new_m', 'new_m_swa_sinks': 'new_m', 'new_m_w_out': 'new_m', 'new_m_ln1_g': 'new_m', 'new_m_ln1_b': 'new_m', 'new_m_xa_wq': 'new_m', 'new_m_xa_wkv': 'new_m', 'new_m_xa_wo': 'new_m', 'new_m_ln2_g': 'new_m', 'new_m_ln2_b': 'new_m', 'new_m_ffn_w_up': 'new_m', 'new_m_ffn_conv_w': 'new_m', 'new_m_ffn_conv_b': 'new_m', 'new_m_ffn_w_down': 'new_m', 'new_m_ln3_g': 'new_m', 'new_m_ln3_b': 'new_m', 'new_v_rel_bias': 'new_v', 'new_v_w_in': 'new_v', 'new_v_ml_conv_w': 'new_v', 'new_v_ml_conv_b': 'new_v', 'new_v_ml_i_bias': 'new_v', 'new_v_ml_f_bias': 'new_v', 'new_v_ml_norm_g': 'new_v', 'new_v_swa_sinks': 'new_v', 'new_v_w_out': 'new_v', 'new_v_ln1_g': 'new_v', 'new_v_ln1_b': 'new_v', 'new_v_xa_wq': 'new_v', 'new_v_xa_wkv': 'new_v', 'new_v_xa_wo': 'new_v', 'new_v_ln2_g': 'new_v', 'new_v_ln2_b': 'new_v', 'new_v_ffn_w_up': 'new_v', 'new_v_ffn_conv_w': 'new_v', 'new_v_ffn_conv_b': 'new_v', 'new_v_ffn_w_down': 'new_v', 'new_v_ln3_g': 'new_v', 'new_v_ln3_b': 'new_v'}


def _forward(args):
    return _fwd_reference(*[args[k] for k in FWD_PARAMS])


def _output_shape():
    out = _jax.eval_shape(lambda: _forward(_fwd_setup_inputs(0)))
    return out.shape, out.dtype

N_MICROBATCH = 1
ADAM_LR = 0.001
ADAM_B1 = 0.9
ADAM_B2 = 0.999
ADAM_EPS = 1e-08
ADAM_WD = 0.01
ADAM_STEP = 10
PER_EXAMPLE_BATCH_AXIS = {'x': 0, 'mem': 0, 'loss_target': 0}
SHARED_INPUTS = []
_WEIGHT_DTYPES = {'rel_bias': _jnp.float32, 'w_in': _jnp.float32, 'ml_conv_w': _jnp.float32, 'ml_conv_b': _jnp.float32, 'ml_i_bias': _jnp.float32, 'ml_f_bias': _jnp.float32, 'ml_norm_g': _jnp.float32, 'swa_sinks': _jnp.float32, 'w_out': _jnp.float32, 'ln1_g': _jnp.float32, 'ln1_b': _jnp.float32, 'xa_wq': _jnp.float32, 'xa_wkv': _jnp.float32, 'xa_wo': _jnp.float32, 'ln2_g': _jnp.float32, 'ln2_b': _jnp.float32, 'ffn_w_up': _jnp.float32, 'ffn_conv_w': _jnp.float32, 'ffn_conv_b': _jnp.float32, 'ffn_w_down': _jnp.float32, 'ln3_g': _jnp.float32, 'ln3_b': _jnp.float32}
MOMENT_SCALE = {'rel_bias': 2.362767e-02, 'w_in': 2.783191e-02, 'ml_conv_w': 2.482122e-02, 'ml_conv_b': 2.452302e-02, 'ml_i_bias': 3.442370e-03, 'ml_f_bias': 4.913987e-01, 'ml_norm_g': 4.439554e-02, 'swa_sinks': 8.244860e-03, 'w_out': 5.842309e-02, 'ln1_g': 2.756807e+00, 'ln1_b': 5.305192e-01, 'xa_wq': 6.964450e-03, 'xa_wkv': 7.764892e-03, 'xa_wo': 1.685543e-02, 'ln2_g': 2.749593e+00, 'ln2_b': 5.296631e-01, 'ffn_w_up': 2.529355e-02, 'ffn_conv_w': 2.535726e-02, 'ffn_conv_b': 3.099251e-02, 'ffn_w_down': 8.337725e-02, 'ln3_g': 2.311562e+01, 'ln3_b': 1.244798e+00}


def _to_microbatches(a, axis):
    t = _jnp.moveaxis(a, axis, 0)
    t = t.reshape((N_MICROBATCH, t.shape[0] // N_MICROBATCH) + t.shape[1:])
    return _jnp.moveaxis(t, 1, axis + 1)


def setup_inputs(seed: int = 0) -> dict:
    inp = _fwd_setup_inputs(seed)
    key = _jax.random.fold_in(_jax.random.key(seed), 7919)
    shape, _ = _output_shape()
    out = dict(inp)
    out["loss_target"] = _jax.random.normal(_jax.random.fold_in(key, 0), shape, _jnp.float32)
    for i, name in enumerate(TWIN_WEIGHTS):
        w = inp[name].astype(_jnp.float32)
        if MOMENT_SCALE is None:
            s = _jnp.sqrt(_jnp.mean(_jnp.square(w)) + 1e-30)
        else:
            s = MOMENT_SCALE[name]
        km, kv = _jax.random.split(_jax.random.fold_in(key, i + 1))
        out[name] = w
        out["m_" + name] = s * _jax.random.normal(km, w.shape, _jnp.float32)
        out["v_" + name] = (s * s) * _jax.random.uniform(kv, w.shape, _jnp.float32, 0.5, 1.5)
    if N_MICROBATCH > 1:
        for name, axis in PER_EXAMPLE_BATCH_AXIS.items():
            out[name] = _to_microbatches(out[name], axis)
    return {'x': out['x'], 'mem': out['mem'], 'rel_bias': out['rel_bias'], 'w_in': out['w_in'], 'ml_conv_w': out['ml_conv_w'], 'ml_conv_b': out['ml_conv_b'], 'ml_i_bias': out['ml_i_bias'], 'ml_f_bias': out['ml_f_bias'], 'ml_norm_g': out['ml_norm_g'], 'swa_sinks': out['swa_sinks'], 'w_out': out['w_out'], 'ln1_g': out['ln1_g'], 'ln1_b': out['ln1_b'], 'xa_wq': out['xa_wq'], 'xa_wkv': out['xa_wkv'], 'xa_wo': out['xa_wo'], 'ln2_g': out['ln2_g'], 'ln2_b': out['ln2_b'], 'ffn_w_up': out['ffn_w_up'], 'ffn_conv_w': out['ffn_conv_w'], 'ffn_conv_b': out['ffn_conv_b'], 'ffn_w_down': out['ffn_w_down'], 'ln3_g': out['ln3_g'], 'ln3_b': out['ln3_b'], 'loss_target': out['loss_target'], 'm_rel_bias': out['m_rel_bias'], 'm_w_in': out['m_w_in'], 'm_ml_conv_w': out['m_ml_conv_w'], 'm_ml_conv_b': out['m_ml_conv_b'], 'm_ml_i_bias': out['m_ml_i_bias'], 'm_ml_f_bias': out['m_ml_f_bias'], 'm_ml_norm_g': out['m_ml_norm_g'], 'm_swa_sinks': out['m_swa_sinks'], 'm_w_out': out['m_w_out'], 'm_ln1_g': out['m_ln1_g'], 'm_ln1_b': out['m_ln1_b'], 'm_xa_wq': out['m_xa_wq'], 'm_xa_wkv': out['m_xa_wkv'], 'm_xa_wo': out['m_xa_wo'], 'm_ln2_g': out['m_ln2_g'], 'm_ln2_b': out['m_ln2_b'], 'm_ffn_w_up': out['m_ffn_w_up'], 'm_ffn_conv_w': out['m_ffn_conv_w'], 'm_ffn_conv_b': out['m_ffn_conv_b'], 'm_ffn_w_down': out['m_ffn_w_down'], 'm_ln3_g': out['m_ln3_g'], 'm_ln3_b': out['m_ln3_b'], 'v_rel_bias': out['v_rel_bias'], 'v_w_in': out['v_w_in'], 'v_ml_conv_w': out['v_ml_conv_w'], 'v_ml_conv_b': out['v_ml_conv_b'], 'v_ml_i_bias': out['v_ml_i_bias'], 'v_ml_f_bias': out['v_ml_f_bias'], 'v_ml_norm_g': out['v_ml_norm_g'], 'v_swa_sinks': out['v_swa_sinks'], 'v_w_out': out['v_w_out'], 'v_ln1_g': out['v_ln1_g'], 'v_ln1_b': out['v_ln1_b'], 'v_xa_wq': out['v_xa_wq'], 'v_xa_wkv': out['v_xa_wkv'], 'v_xa_wo': out['v_xa_wo'], 'v_ln2_g': out['v_ln2_g'], 'v_ln2_b': out['v_ln2_b'], 'v_ffn_w_up': out['v_ffn_w_up'], 'v_ffn_conv_w': out['v_ffn_conv_w'], 'v_ffn_conv_b': out['v_ffn_conv_b'], 'v_ffn_w_down': out['v_ffn_w_down'], 'v_ln3_g': out['v_ln3_g'], 'v_ln3_b': out['v_ln3_b']}


def _loss(weights, diff, rest, loss_target):
    with _jax.named_scope("forward"):
        args = {**rest, TWIN_DIFF_INPUT: diff, **{k: w.astype(_WEIGHT_DTYPES[k]) for k, w in weights.items()}}
        y = _forward(args)
    with _jax.named_scope("loss_head"):
        err = _jnp.square(y.astype(_jnp.float32) - loss_target)
        return 0.5 * _jnp.sum(_jnp.mean(err, axis=-1)) if err.ndim else 0.5 * err


def _adamw(w, g, m, v):
    m = ADAM_B1 * m + (1.0 - ADAM_B1) * g
    v = ADAM_B2 * v + (1.0 - ADAM_B2) * _jnp.square(g)
    m_hat = m / (1.0 - ADAM_B1 ** ADAM_STEP)
    v_hat = v / (1.0 - ADAM_B2 ** ADAM_STEP)
    delta = -ADAM_LR * (m_hat / (_jnp.sqrt(v_hat) + ADAM_EPS) + ADAM_WD * w)
    return delta, m, v


def reference(x, mem, rel_bias, w_in, ml_conv_w, ml_conv_b, ml_i_bias, ml_f_bias, ml_norm_g, swa_sinks, w_out, ln1_g, ln1_b, xa_wq, xa_wkv, xa_wo, ln2_g, ln2_b, ffn_w_up, ffn_conv_w, ffn_conv_b, ffn_w_down, ln3_g, ln3_b, loss_target, m_rel_bias, m_w_in, m_ml_conv_w, m_ml_conv_b, m_ml_i_bias, m_ml_f_bias, m_ml_norm_g, m_swa_sinks, m_w_out, m_ln1_g, m_ln1_b, m_xa_wq, m_xa_wkv, m_xa_wo, m_ln2_g, m_ln2_b, m_ffn_w_up, m_ffn_conv_w, m_ffn_conv_b, m_ffn_w_down, m_ln3_g, m_ln3_b, v_rel_bias, v_w_in, v_ml_conv_w, v_ml_conv_b, v_ml_i_bias, v_ml_f_bias, v_ml_norm_g, v_swa_sinks, v_w_out, v_ln1_g, v_ln1_b, v_xa_wq, v_xa_wkv, v_xa_wo, v_ln2_g, v_ln2_b, v_ffn_w_up, v_ffn_conv_w, v_ffn_conv_b, v_ffn_w_down, v_ln3_g, v_ln3_b):
    given = dict(x=x, mem=mem, rel_bias=rel_bias, w_in=w_in, ml_conv_w=ml_conv_w, ml_conv_b=ml_conv_b, ml_i_bias=ml_i_bias, ml_f_bias=ml_f_bias, ml_norm_g=ml_norm_g, swa_sinks=swa_sinks, w_out=w_out, ln1_g=ln1_g, ln1_b=ln1_b, xa_wq=xa_wq, xa_wkv=xa_wkv, xa_wo=xa_wo, ln2_g=ln2_g, ln2_b=ln2_b, ffn_w_up=ffn_w_up, ffn_conv_w=ffn_conv_w, ffn_conv_b=ffn_conv_b, ffn_w_down=ffn_w_down, ln3_g=ln3_g, ln3_b=ln3_b, loss_target=loss_target, m_rel_bias=m_rel_bias, m_w_in=m_w_in, m_ml_conv_w=m_ml_conv_w, m_ml_conv_b=m_ml_conv_b, m_ml_i_bias=m_ml_i_bias, m_ml_f_bias=m_ml_f_bias, m_ml_norm_g=m_ml_norm_g, m_swa_sinks=m_swa_sinks, m_w_out=m_w_out, m_ln1_g=m_ln1_g, m_ln1_b=m_ln1_b, m_xa_wq=m_xa_wq, m_xa_wkv=m_xa_wkv, m_xa_wo=m_xa_wo, m_ln2_g=m_ln2_g, m_ln2_b=m_ln2_b, m_ffn_w_up=m_ffn_w_up, m_ffn_conv_w=m_ffn_conv_w, m_ffn_conv_b=m_ffn_conv_b, m_ffn_w_down=m_ffn_w_down, m_ln3_g=m_ln3_g, m_ln3_b=m_ln3_b, v_rel_bias=v_rel_bias, v_w_in=v_w_in, v_ml_conv_w=v_ml_conv_w, v_ml_conv_b=v_ml_conv_b, v_ml_i_bias=v_ml_i_bias, v_ml_f_bias=v_ml_f_bias, v_ml_norm_g=v_ml_norm_g, v_swa_sinks=v_swa_sinks, v_w_out=v_w_out, v_ln1_g=v_ln1_g, v_ln1_b=v_ln1_b, v_xa_wq=v_xa_wq, v_xa_wkv=v_xa_wkv, v_xa_wo=v_xa_wo, v_ln2_g=v_ln2_g, v_ln2_b=v_ln2_b, v_ffn_w_up=v_ffn_w_up, v_ffn_conv_w=v_ffn_conv_w, v_ffn_conv_b=v_ffn_conv_b, v_ffn_w_down=v_ffn_w_down, v_ln3_g=v_ln3_g, v_ln3_b=v_ln3_b)
    weights = {n: given[n] for n in TWIN_WEIGHTS}
    shared = {n: given[n] for n in SHARED_INPUTS}
    per_example = {n: given[n] for n in ['x', 'mem']}
    grad_fn = _jax.value_and_grad(_loss, argnums=(0, 1))

    def one_microbatch(ex, loss_target):
        ex = dict(ex)
        diff = ex.pop(TWIN_DIFF_INPUT)
        return grad_fn(weights, diff, {**shared, **ex}, loss_target)

    if N_MICROBATCH == 1:
        loss, (grad_w, grad_x) = one_microbatch(per_example, given["loss_target"])
    else:
        def body(carry, xs):
            loss_sum, grad_sum = carry
            l_k, (gw_k, gx_k) = one_microbatch(xs[0], xs[1])
            with _jax.named_scope("update"):
                return (loss_sum + l_k, _jax.tree.map(_jnp.add, grad_sum, gw_k)), gx_k

        init = (_jnp.zeros((), _jnp.float32), _jax.tree.map(_jnp.zeros_like, weights))
        (loss, grad_w), grad_x = _jax.lax.scan(body, init, (per_example, given["loss_target"]))
    with _jax.named_scope("update"):
        delta_w, new_m, new_v = {}, {}, {}
        for n in TWIN_WEIGHTS:
            delta_w[n], new_m[n], new_v[n] = _adamw(weights[n], grad_w[n], given["m_" + n], given["v_" + n])
    return (loss, grad_x, *[grad_w[n] for n in TWIN_WEIGHTS], *[delta_w[n] for n in TWIN_WEIGHTS],
            *[new_m[n] for n in TWIN_WEIGHTS], *[new_v[n] for n in TWIN_WEIGHTS])
```

```python
import functools
import math

import jax
import jax.numpy as jnp
import numpy as np
from jax import lax
from jax.experimental import pallas as pl
from jax.experimental.pallas import tpu as pltpu

F32 = jnp.float32
BF16 = jnp.bfloat16

N_DEV = 8
D = 1024
DEPTH = 2
ML_H = 4
ML_W = 512
ML_DH = 128
ML_L = 64
ML_CONV = 4
SW_DH = 64
SW_W = 512
SW_H = 8
SW_G = 4
SW_KVW = 128
BLK = 128
REL_B = 32
REL_MAXD = 128
XA_H = 4
XA_DH = 256
DFF = 2816
FFN_CONV = 3
ALPHA = (2.0 * DEPTH) ** 0.25
EPS = 1e-5
N_IN = 2824
NP_IN = 3072
ADAM_LR = 0.001
ADAM_B1 = 0.9
ADAM_B2 = 0.999
ADAM_EPS = 1e-08
ADAM_WD = 0.01
ADAM_STEP = 10
VMEM_LIMIT = 56 * 1024 * 1024

NN = ((1,), (0,))
NT = ((1,), (1,))
TN = ((0,), (0,))


def _dg(a, b, dn):
    return lax.dot_general(a.astype(BF16), b.astype(BF16), (dn, ((), ())), preferred_element_type=F32)


@jax.custom_vjp
def dot_nn(a, b):
    return _dg(a, b, NN)


dot_nn.defvjp(lambda a, b: (_dg(a, b, NN), (a, b)), lambda r, g: (_dg(g, r[1], NT), _dg(r[0], g, TN)))


@jax.custom_vjp
def dot_nt(a, b):
    return _dg(a, b, NT)


dot_nt.defvjp(lambda a, b: (_dg(a, b, NT), (a, b)), lambda r, g: (_dg(g, r[1], NN), _dg(g, r[0], TN)))


@jax.custom_vjp
def dot_tn(a, b):
    return _dg(a, b, TN)


dot_tn.defvjp(lambda a, b: (_dg(a, b, TN), (a, b)), lambda r, g: (_dg(r[1], g, NT), _dg(r[0], g, NN)))


def _params(sem=None):
    return pltpu.CompilerParams(dimension_semantics=sem, vmem_limit_bytes=VMEM_LIMIT)


def _sds(shape, dtype):
    return jax.ShapeDtypeStruct(tuple(shape), dtype)


def _mm(a, b, *, trans_b=False, out_dtype=F32, add=None, add_scale=1.0, tm=512, tn=512, name):
    M, K = a.shape
    N = b.shape[0] if trans_b else b.shape[1]
    tm = min(tm, M)
    tn = next(t for t in (tn, 256, 128) if N % t == 0)
    assert M % tm == 0
    tk = K
    for cand in (K, 1408, 1024):
        if K % cand == 0 and cand <= 1408:
            tk = cand
            break
    nk = K // tk
    has_add = add is not None

    def body(*refs):
        if has_add:
            a_ref, b_ref, add_ref, o_ref, acc_ref = refs
        else:
            a_ref, b_ref, o_ref, acc_ref = refs
        k = pl.program_id(2)
        p = _dg(a_ref[...], b_ref[...], NT if trans_b else NN)

        @pl.when(k == 0)
        def _():
            acc_ref[...] = p

        @pl.when(k > 0)
        def _():
            acc_ref[...] += p

        @pl.when(k == nk - 1)
        def _():
            r = acc_ref[...]
            if has_add:
                r = r + add_scale * add_ref[...].astype(F32)
            o_ref[...] = r.astype(out_dtype)

    in_specs = [pl.BlockSpec((tm, tk), lambda i, j, k: (i, k)),
                pl.BlockSpec((tn, tk), lambda i, j, k: (j, k)) if trans_b else pl.BlockSpec((tk, tn), lambda i, j, k: (k, j))]
    args = [a, b]
    if has_add:
        in_specs.append(pl.BlockSpec((tm, tn), lambda i, j, k: (i, j)))
        args.append(add)
    return pl.pallas_call(
        body, name=name, grid=(M // tm, N // tn, nk), in_specs=in_specs,
        out_specs=pl.BlockSpec((tm, tn), lambda i, j, k: (i, j)), out_shape=_sds((M, N), out_dtype),
        scratch_shapes=[pltpu.VMEM((tm, tn), F32)],
        compiler_params=_params(("parallel", "parallel", "arbitrary")))(*args)


def _mm_tn(a, g, *, a_cols=None, name):
    S = a.shape[0]
    K = a.shape[1] if a_cols is None else a_cols[0]
    N = g.shape[1]
    a_off = 0 if a_cols is None else a_cols[1]
    tk = K if K <= 1024 else K // 2
    tn = min(N, 512)
    if N % tn:
        tn = 128
    ts = min(512, S)
    ns = S // ts
    assert K % tk == 0 and N % tn == 0 and S % ts == 0
    kb = K // tk

    def body(a_ref, g_ref, o_ref):
        s = pl.program_id(2)
        p = _dg(a_ref[...], g_ref[...], TN)

        @pl.when(s == 0)
        def _():
            o_ref[...] = p

        @pl.when(s > 0)
        def _():
            o_ref[...] += p

    return pl.pallas_call(
        body, name=name, grid=(kb, N // tn, ns),
        in_specs=[pl.BlockSpec((ts, tk), lambda i, j, s: (s, i + a_off * kb)), pl.BlockSpec((ts, tn), lambda i, j, s: (s, j))],
        out_specs=pl.BlockSpec((tk, tn), lambda i, j, s: (i, j)), out_shape=_sds((K, N), F32),
        compiler_params=_params(("parallel", "parallel", "arbitrary")))(a, g)


def _mm_res_ln(a, w, resid, gam, bet, *, name):
    M, K = a.shape
    tm = 256

    def body(a_ref, w_ref, r_ref, g_ref, b_ref, y_ref, z_ref):
        z = ALPHA * r_ref[...] + _dg(a_ref[...], w_ref[...], NN)
        mu = jnp.mean(z, axis=1, keepdims=True)
        zc = z - mu
        var = jnp.mean(zc * zc, axis=1, keepdims=True)
        y_ref[...] = zc * lax.rsqrt(var + EPS) * g_ref[...] + b_ref[...]
        z_ref[...] = z

    row = pl.BlockSpec((tm, D), lambda i: (i, 0))
    vec = pl.BlockSpec((1, D), lambda i: (0, 0))
    return pl.pallas_call(
        body, name=name, grid=(M // tm,),
        in_specs=[pl.BlockSpec((tm, K), lambda i: (i, 0)), pl.BlockSpec((K, D), lambda i: (0, 0)), row, vec, vec],
        out_specs=[row, row], out_shape=[_sds((M, D), F32), _sds((M, D), F32)],
        compiler_params=_params(("parallel",)))(a, w, resid, gam, bet)


def _ln_bwd(dy, z, gam, *, name):
    M = dy.shape[0]
    tm = 512

    def body(dy_ref, z_ref, g_ref, dz_ref, dg_ref, db_ref):
        i = pl.program_id(0)
        z = z_ref[...]
        dy_ = dy_ref[...]
        mu = jnp.mean(z, axis=1, keepdims=True)
        zc = z - mu
        var = jnp.mean(zc * zc, axis=1, keepdims=True)
        rstd = lax.rsqrt(var + EPS)
        xh = zc * rstd
        dxh = dy_ * g_ref[...]
        m1 = jnp.mean(dxh, axis=1, keepdims=True)
        m2 = jnp.mean(dxh * xh, axis=1, keepdims=True)
        dz_ref[...] = rstd * (dxh - m1 - xh * m2)
        pg = jnp.sum(dy_ * xh, axis=0, keepdims=True)
        pb = jnp.sum(dy_, axis=0, keepdims=True)

        @pl.when(i == 0)
        def _():
            dg_ref[...] = pg
            db_ref[...] = pb

        @pl.when(i > 0)
        def _():
            dg_ref[...] += pg
            db_ref[...] += pb

    row = pl.BlockSpec((tm, D), lambda i: (i, 0))
    vec = pl.BlockSpec((1, D), lambda i: (0, 0))
    return pl.pallas_call(
        body, name=name, grid=(M // tm,), in_specs=[row, row, vec], out_specs=[row, vec, vec],
        out_shape=[_sds((M, D), F32), _sds((1, D), F32), _sds((1, D), F32)],
        compiler_params=_params(("arbitrary",)))(dy, z, gam)


def _shift_down(x, d):
    if d == 0:
        return x
    rows = lax.broadcasted_iota(jnp.int32, x.shape, 0)
    return jnp.where(rows >= d, pltpu.roll(x, d, 0), 0.0)


def _shift_up(x, d):
    if d == 0:
        return x
    S = x.shape[0]
    rows = lax.broadcasted_iota(jnp.int32, x.shape, 0)
    return jnp.where(rows < S - d, pltpu.roll(x, S - d, 0), 0.0)


def _conv(x, w_ref, b, K):
    y = b
    for j in range(K):
        y = y + _shift_down(x, K - 1 - j) * w_ref[j:j + 1, :]
    return y


def _conv_bwd(dy, x, w_ref, dw_ref, db_ref, K):
    dx = jnp.zeros_like(x)
    for j in range(K):
        dx = dx + _shift_up(dy, K - 1 - j) * w_ref[j:j + 1, :]
        dw_ref[j:j + 1, :] = jnp.sum(dy * _shift_down(x, K - 1 - j), axis=0, keepdims=True)
    db_ref[...] = jnp.sum(dy, axis=0, keepdims=True)
    return dx


def _silu_conv_fwd(proj, cw, cb, *, name):
    S = proj.shape[0]

    def body(x_ref, w_ref, b_ref, o_ref):
        o_ref[...] = jax.nn.silu(_conv(x_ref[...], w_ref, b_ref[...], ML_CONV))

    col = pl.BlockSpec((S, 128), lambda j: (0, j))
    return pl.pallas_call(
        body, name=name, grid=(8,),
        in_specs=[col, pl.BlockSpec((ML_CONV, 128), lambda j: (0, j)), pl.BlockSpec((1, 128), lambda j: (0, j))],
        out_specs=col, out_shape=_sds((S, 2 * ML_W), F32), compiler_params=_params(("parallel",)))(proj, cw, cb)


def _silu_conv_bwd(dqk, proj, cw, cb, *, name):
    S = proj.shape[0]

    def body(d_ref, x_ref, w_ref, b_ref, dx_ref, dw_ref, db_ref):
        x = x_ref[...]
        y = _conv(x, w_ref, b_ref[...], ML_CONV)
        dy = jax.vjp(jax.nn.silu, y)[1](d_ref[...])[0]
        dx_ref[...] = _conv_bwd(dy, x, w_ref, dw_ref, db_ref, ML_CONV).astype(BF16)

    col = pl.BlockSpec((S, 128), lambda j: (0, j))
    wsp = pl.BlockSpec((ML_CONV, 128), lambda j: (0, j))
    bsp = pl.BlockSpec((1, 128), lambda j: (0, j))
    return pl.pallas_call(
        body, name=name, grid=(8,), in_specs=[col, col, wsp, bsp], out_specs=[col, wsp, bsp],
        out_shape=[_sds((S, 2 * ML_W), BF16), _sds((ML_CONV, 2 * ML_W), F32), _sds((1, 2 * ML_W), F32)],
        compiler_params=_params(("parallel",)))(dqk, proj, cw, cb)


NB_FF = DFF // 128


def _gate(ug, uv):
    return jax.nn.gelu(ug) * uv


def _ffn_gate_fwd(up, cw, cb, *, name):
    S = up.shape[0]

    def body(g_ref, v_ref, wg_ref, wv_ref, bg_ref, bv_ref, o_ref):
        ug = _conv(g_ref[...], wg_ref, bg_ref[...], FFN_CONV)
        uv = _conv(v_ref[...], wv_ref, bv_ref[...], FFN_CONV)
        o_ref[...] = _gate(ug, uv).astype(BF16)

    cg = pl.BlockSpec((S, 128), lambda j: (0, j))
    cv = pl.BlockSpec((S, 128), lambda j: (0, j + NB_FF))
    wg = pl.BlockSpec((FFN_CONV, 128), lambda j: (0, j))
    wv = pl.BlockSpec((FFN_CONV, 128), lambda j: (0, j + NB_FF))
    bg = pl.BlockSpec((1, 128), lambda j: (0, j))
    bv = pl.BlockSpec((1, 128), lambda j: (0, j + NB_FF))
    return pl.pallas_call(
        body, name=name, grid=(NB_FF,), in_specs=[cg, cv, wg, wv, bg, bv], out_specs=cg,
        out_shape=_sds((S, DFF), BF16), compiler_params=_params(("parallel",)))(up, up, cw, cw, cb, cb)


def _ffn_gate_bwd(da, up, cw, cb, *, name):
    S = up.shape[0]

    def body(da_ref, g_ref, v_ref, wg_ref, wv_ref, bg_ref, bv_ref, dxg_ref, dxv_ref, dwg_ref, dwv_ref, dbg_ref, dbv_ref):
        xg, xv = g_ref[...], v_ref[...]
        ug = _conv(xg, wg_ref, bg_ref[...], FFN_CONV)
        uv = _conv(xv, wv_ref, bv_ref[...], FFN_CONV)
        dug, duv = jax.vjp(_gate, ug, uv)[1](da_ref[...])
        dxg_ref[...] = _conv_bwd(dug, xg, wg_ref, dwg_ref, dbg_ref, FFN_CONV).astype(BF16)
        dxv_ref[...] = _conv_bwd(duv, xv, wv_ref, dwv_ref, dbv_ref, FFN_CONV).astype(BF16)

    cg = pl.BlockSpec((S, 128), lambda j: (0, j))
    cv = pl.BlockSpec((S, 128), lambda j: (0, j + NB_FF))
    wg = pl.BlockSpec((FFN_CONV, 128), lambda j: (0, j))
    wv = pl.BlockSpec((FFN_CONV, 128), lambda j: (0, j + NB_FF))
    bg = pl.BlockSpec((1, 128), lambda j: (0, j))
    bv = pl.BlockSpec((1, 128), lambda j: (0, j + NB_FF))
    half = _sds((S, DFF), BF16)
    dxg, dxv, dwg, dwv, dbg, dbv = pl.pallas_call(
        body, name=name, grid=(NB_FF,), in_specs=[cg, cg, cv, wg, wv, bg, bv],
        out_specs=[cg, cg, wg, wg, bg, bg],
        out_shape=[half, half, _sds((FFN_CONV, DFF), F32), _sds((FFN_CONV, DFF), F32), _sds((1, DFF), F32), _sds((1, DFF), F32)],
        compiler_params=_params(("parallel",)))(da, up, up, cw, cw, cb, cb)
    return dxg, dxv, jnp.concatenate([dwg, dwv], axis=1), jnp.concatenate([dbg, dbv], axis=1)


def _log_sigmoid(x):
    return jnp.minimum(x, 0.0) - jnp.log1p(jnp.exp(-jnp.abs(x)))


@jax.custom_vjp
def _clamp_div(num, den, floor, shift):
    return num / jnp.maximum(jnp.abs(den), floor)


def _clamp_div_fwd(num, den, floor, shift):
    out = num / jnp.maximum(jnp.abs(den), floor)
    return out, (den, floor, out)


def _clamp_div_bwd(res, g):
    den, floor, out = res
    active = jnp.abs(den) < floor
    dinv = jnp.maximum(jnp.abs(den), floor)
    go = jnp.sum(g * out, axis=1, keepdims=True)
    ddiv = -go / dinv
    return (g / dinv, jnp.where(active, 0.0, ddiv * jnp.sign(den)), jnp.where(active, ddiv, 0.0),
            jnp.sum(jnp.where(active, go, 0.0), axis=0, keepdims=True))


_clamp_div.defvjp(_clamp_div_fwd, _clamp_div_bwd)


def _ml_head(q, k, v, o_pre, gates, gbias, C, n, ng, m, shift, hidx):
    L = q.shape[0]
    lane = lax.broadcasted_iota(jnp.int32, (L, 128), 1)
    lane1 = lax.broadcasted_iota(jnp.int32, (1, 128), 1)
    gz = gates + jnp.where(lane1 < ML_H, lax.stop_gradient(gbias), gbias)
    ig = jnp.sum(jnp.where(lane == hidx, gz, 0.0), axis=1, keepdims=True)
    lf = _log_sigmoid(jnp.sum(jnp.where(lane == ML_H + hidx, gz, 0.0), axis=1, keepdims=True))
    r = lax.broadcasted_iota(jnp.int32, (L, L), 0)
    c = lax.broadcasted_iota(jnp.int32, (L, L), 1)
    eye, tril = r == c, r >= c

    def to_row(col):
        return jnp.sum(jnp.where(eye, col, 0.0), axis=0, keepdims=True)

    b_col = jnp.sum(jnp.where(tril, to_row(lf), 0.0), axis=1, keepdims=True)
    Dm = jnp.where(tril, b_col - to_row(b_col) + to_row(ig), -jnp.inf)
    inter = b_col + m
    m_t = jnp.maximum(inter, jnp.max(Dm, axis=1, keepdims=True))
    w_inter = jnp.exp(inter - m_t)
    ks = k * (ML_DH ** -0.5)
    s = dot_nt(q, ks) * jnp.exp(Dm - m_t)
    num = w_inter * dot_nn(q, C) + dot_nn(s, v)
    den = w_inter * jnp.sum(q * n, axis=1, keepdims=True) + jnp.sum(s, axis=1, keepdims=True)
    h = _clamp_div(num, den, jnp.exp(-m_t), shift)
    g = jnp.sum(lf, axis=0, keepdims=True)
    a = g - b_col + ig
    m_new = jnp.maximum(g + m, jnp.max(a, axis=0, keepdims=True))
    decay = jnp.exp(g + m - m_new)
    wk = jnp.exp(a - m_new)
    C_new = decay * C + dot_tn(ks * wk, v)
    n_new = decay * n + jnp.sum(wk * ks, axis=0, keepdims=True)
    mu = jnp.mean(h, axis=1, keepdims=True)
    hc = h - mu
    var = jnp.mean(hc * hc, axis=1, keepdims=True)
    out = jax.nn.sigmoid(o_pre) * (hc * lax.rsqrt(var + EPS) * ng)
    return out, C_new, n_new, m_new


def _hs(h):
    return slice(h * ML_DH, (h + 1) * ML_DH)


def _mlstm_fwd(qk, proj, gbias, ng, *, name):
    S = qk.shape[0]
    nc = S // ML_L

    def body(q_ref, k_ref, v_ref, o_ref, g_ref, gb_ref, ng_ref, h_ref, cs_ref, ns_ref, ms_ref, c_s, n_s, m_s):
        @pl.when(pl.program_id(0) == 0)
        def _():
            c_s[...] = jnp.zeros_like(c_s)
            n_s[...] = jnp.zeros_like(n_s)
            m_s[...] = jnp.zeros_like(m_s)

        gates = g_ref[...]
        for h in range(ML_H):
            C, n, m = c_s[h], n_s[h:h + 1, :], m_s[h:h + 1, 0:1]
            cs_ref[0, h] = C
            ns_ref[0, h] = n
            ms_ref[0, h] = m_s[h:h + 1, :]
            out, C2, n2, m2 = _ml_head(q_ref[:, _hs(h)], k_ref[:, _hs(h)], v_ref[:, _hs(h)], o_ref[:, _hs(h)], gates,
                                       gb_ref[...], C, n, ng_ref[:, _hs(h)], m, jnp.zeros((1, 1), F32), h)
            h_ref[:, _hs(h)] = out.astype(BF16)
            c_s[h] = C2
            n_s[h:h + 1, :] = n2
            m_s[h:h + 1, :] = jnp.broadcast_to(m2, (1, 128))

    def w(j):
        return pl.BlockSpec((ML_L, ML_W), lambda c, j=j: (c, j))

    return pl.pallas_call(
        body, name=name, grid=(nc,),
        in_specs=[w(0), w(1), w(2), w(3), pl.BlockSpec((ML_L, 128), lambda c: (c, 22)),
                  pl.BlockSpec((1, 128), lambda c: (0, 0)), pl.BlockSpec((1, ML_W), lambda c: (0, 0))],
        out_specs=[w(0), pl.BlockSpec((1, ML_H, ML_DH, ML_DH), lambda c: (c, 0, 0, 0)),
                   pl.BlockSpec((1, ML_H, 1, 128), lambda c: (c, 0, 0, 0)), pl.BlockSpec((1, ML_H, 1, 128), lambda c: (c, 0, 0, 0))],
        out_shape=[_sds((S, ML_W), BF16), _sds((nc, ML_H, ML_DH, ML_DH), F32), _sds((nc, ML_H, 1, 128), F32),
                   _sds((nc, ML_H, 1, 128), F32)],
        scratch_shapes=[pltpu.VMEM((ML_H, ML_DH, ML_DH), F32), pltpu.VMEM((8, 128), F32), pltpu.VMEM((8, 128), F32)],
        compiler_params=_params(("arbitrary",)))(qk, qk, proj, proj, proj, gbias, ng)


def _mlstm_bwd(dh, qk, proj, gbias, ng, cs, ns, ms, *, name):
    S = qk.shape[0]
    nc = S // ML_L

    def body(dh_ref, q_ref, k_ref, v_ref, o_ref, g_ref, gb_ref, ng_ref, cs_ref, ns_ref, ms_ref,
             dqk_ref, dv_ref, do_ref, dg_ref, dgb_ref, dng_ref, dc_s, dn_s, dm_s):
        @pl.when(pl.program_id(0) == 0)
        def _():
            dc_s[...] = jnp.zeros_like(dc_s)
            dn_s[...] = jnp.zeros_like(dn_s)
            dm_s[...] = jnp.zeros_like(dm_s)
            dgb_ref[...] = jnp.zeros_like(dgb_ref)
            dng_ref[...] = jnp.zeros_like(dng_ref)

        gates = g_ref[...]
        dgates = jnp.zeros_like(gates)
        dgb = jnp.zeros((1, 128), F32)
        for h in range(ML_H):
            f = functools.partial(_ml_head, hidx=h)
            _, vjp = jax.vjp(f, q_ref[:, _hs(h)], k_ref[:, _hs(h)], v_ref[:, _hs(h)], o_ref[:, _hs(h)], gates, gb_ref[...],
                             cs_ref[0, h], ns_ref[0, h], ng_ref[:, _hs(h)], ms_ref[0, h][:, 0:1], jnp.zeros((1, 1), F32))
            dq, dk, dv, do, dg_h, dgb_h, dC, dn, dng, dm, dshift = vjp(
                (dh_ref[:, _hs(h)], dc_s[h], dn_s[h:h + 1, :], dm_s[h:h + 1, 0:1]))
            dm_s[h:h + 1, :] = jnp.broadcast_to(dm, (1, 128))
            dgb_h = jnp.where(lax.broadcasted_iota(jnp.int32, (1, 128), 1) == h, dshift, dgb_h)
            dqk_ref[:, _hs(h)] = dq
            dqk_ref[:, ML_W + h * ML_DH:ML_W + (h + 1) * ML_DH] = dk
            dv_ref[:, _hs(h)] = dv.astype(BF16)
            do_ref[:, _hs(h)] = do.astype(BF16)
            dng_ref[:, _hs(h)] += dng
            dgates = dgates + dg_h
            dgb = dgb + dgb_h
            dc_s[h] = dC
            dn_s[h:h + 1, :] = dn
        dg_ref[...] = dgates.astype(BF16)
        dgb_ref[...] += dgb

    def w(j):
        return pl.BlockSpec((ML_L, ML_W), lambda c, j=j: (nc - 1 - c, j))

    vec = pl.BlockSpec((1, 128), lambda c: (0, 0))
    vecw = pl.BlockSpec((1, ML_W), lambda c: (0, 0))
    gsp = pl.BlockSpec((ML_L, 128), lambda c: (nc - 1 - c, 22))
    st = pl.BlockSpec((1, ML_H, 1, 128), lambda c: (nc - 1 - c, 0, 0, 0))
    return pl.pallas_call(
        body, name=name, grid=(nc,),
        in_specs=[w(0), w(0), w(1), w(2), w(3), gsp, vec, vecw,
                  pl.BlockSpec((1, ML_H, ML_DH, ML_DH), lambda c: (nc - 1 - c, 0, 0, 0)), st, st],
        out_specs=[pl.BlockSpec((ML_L, 2 * ML_W), lambda c: (nc - 1 - c, 0)), w(0), w(0),
                   pl.BlockSpec((ML_L, 128), lambda c: (nc - 1 - c, 0)), vec, vecw],
        out_shape=[_sds((S, 2 * ML_W), F32), _sds((S, ML_W), BF16), _sds((S, ML_W), BF16), _sds((S, 128), BF16),
                   _sds((1, 128), F32), _sds((1, ML_W), F32)],
        scratch_shapes=[pltpu.VMEM((ML_H, ML_DH, ML_DH), F32), pltpu.VMEM((8, 128), F32), pltpu.VMEM((8, 128), F32)],
        compiler_params=_params(("arbitrary",)))(dh, qk, qk, proj, proj, proj, gbias, ng, cs, ns, ms)


def _t5_buckets():
    r = np.arange(BLK)[:, None]
    c = np.arange(2 * BLK)[None, :]
    n = np.maximum(BLK + r - c, 0)
    max_exact = REL_B // 2
    nf = np.maximum(n, 1).astype(np.float32)
    large = max_exact + (np.log(nf / np.float32(max_exact)) / np.float32(math.log(REL_MAXD / max_exact))
                         * np.float32(REL_B - max_exact)).astype(np.int32)
    large = np.minimum(large, REL_B - 1)
    return np.where(n < max_exact, n, large).astype(np.int32)


def _bias_table(rel_bias, bucket, *, name):
    def body(rb_ref, bk_ref, o_ref):
        bk = bk_ref[...]
        for h in range(SW_H):
            acc = jnp.zeros((BLK, 2 * BLK), F32)
            for b in range(REL_B):
                acc = jnp.where(bk == b, rb_ref[b, h], acc)
            o_ref[h] = acc

    return pl.pallas_call(
        body, name=name, in_specs=[pl.BlockSpec(memory_space=pltpu.SMEM), pl.BlockSpec(memory_space=pltpu.VMEM)],
        out_specs=pl.BlockSpec(memory_space=pltpu.VMEM), out_shape=_sds((SW_H, BLK, 2 * BLK), F32),
        compiler_params=_params())(rel_bias, bucket)


def _bias_table_bwd(dbias, bucket, *, name):
    def body(d_ref, bk_ref, o_ref):
        bk = bk_ref[...]
        rows = lax.broadcasted_iota(jnp.int32, (REL_B, 128), 0)
        lanes = lax.broadcasted_iota(jnp.int32, (REL_B, 128), 1)
        acc = jnp.zeros((REL_B, 128), F32)
        for h in range(SW_H):
            d = d_ref[h]
            for b in range(REL_B):
                t = jnp.sum(jnp.sum(jnp.where(bk == b, d, 0.0), axis=0, keepdims=True), axis=1, keepdims=True)
                acc = jnp.where((rows == b) & (lanes == h), t, acc)
        o_ref[...] = acc

    return pl.pallas_call(
        body, name=name, in_specs=[pl.BlockSpec(memory_space=pltpu.VMEM), pl.BlockSpec(memory_space=pltpu.VMEM)],
        out_specs=pl.BlockSpec(memory_space=pltpu.VMEM), out_shape=_sds((REL_B, 128), F32),
        compiler_params=_params())(dbias, bucket)


def _swa_head(qh, kph, kch, vph, vch, bp, bc, sinks, has_prev, h):
    r = lax.broadcasted_iota(jnp.int32, (BLK, BLK), 0)
    c = lax.broadcasted_iota(jnp.int32, (BLK, BLK), 1)
    lane = lax.broadcasted_iota(jnp.int32, (1, 128), 1)
    sink = jnp.sum(jnp.where(lane == h, sinks, 0.0), axis=1, keepdims=True)
    lp = jnp.where((c > r) & has_prev, dot_nt(qh, kph) * (SW_DH ** -0.5) + bp, -jnp.inf)
    lc = jnp.where(c <= r, dot_nt(qh, kch) * (SW_DH ** -0.5) + bc, -jnp.inf)
    mx = jnp.maximum(jnp.maximum(jnp.max(lp, axis=1, keepdims=True), jnp.max(lc, axis=1, keepdims=True)), sink)
    mx = lax.stop_gradient(mx)
    pp, pc = jnp.exp(lp - mx), jnp.exp(lc - mx)
    den = jnp.sum(pp, axis=1, keepdims=True) + jnp.sum(pc, axis=1, keepdims=True) + jnp.exp(sink - mx)
    return dot_nn(pp / den, vph) + dot_nn(pc / den, vch)


def _qs(h):
    return slice(h * SW_DH, (h + 1) * SW_DH)


def _kvs(h):
    return slice((h // SW_G) * SW_DH, (h // SW_G + 1) * SW_DH)


def _swa_fwd(proj, bias, sinks, *, name):
    S = proj.shape[0]
    nb = S // BLK

    def body(q_ref, kp_ref, kc_ref, vp_ref, vc_ref, b_ref, s_ref, o_ref):
        has_prev = pl.program_id(0) > 0
        for h in range(SW_H):
            o_ref[:, _qs(h)] = _swa_head(q_ref[:, _qs(h)], kp_ref[:, _kvs(h)], kc_ref[:, _kvs(h)], vp_ref[:, _kvs(h)],
                                         vc_ref[:, _kvs(h)], b_ref[h, :, :BLK], b_ref[h, :, BLK:], s_ref[...], has_prev,
                                         h).astype(BF16)

    def cur(j):
        return pl.BlockSpec((BLK, 128), lambda n, j=j: (n, j))

    def prev(j):
        return pl.BlockSpec((BLK, 128), lambda n, j=j: (jnp.maximum(n - 1, 0), j))

    return pl.pallas_call(
        body, name=name, grid=(nb,),
        in_specs=[pl.BlockSpec((BLK, SW_W), lambda n: (n, 4)), prev(20), cur(20), prev(21), cur(21),
                  pl.BlockSpec((SW_H, BLK, 2 * BLK), lambda n: (0, 0, 0)), pl.BlockSpec((1, 128), lambda n: (0, 0))],
        out_specs=pl.BlockSpec((BLK, SW_W), lambda n: (n, 0)), out_shape=_sds((S, SW_W), BF16),
        compiler_params=_params(("parallel",)))(proj, proj, proj, proj, proj, bias, sinks)


def _swa_bwd(dh, proj, bias, sinks, *, name):
    S = proj.shape[0]
    nb = S // BLK

    def body(dh_ref, q_ref, kp_ref, kc_ref, vp_ref, vc_ref, b_ref, s_ref, dq_ref, dk_ref, dv_ref, db_ref, ds_ref,
             ck_s, cv_s, nk_s, nv_s):
        i = pl.program_id(0)

        @pl.when(i == 0)
        def _():
            ck_s[...] = jnp.zeros_like(ck_s)
            cv_s[...] = jnp.zeros_like(cv_s)
            db_ref[...] = jnp.zeros_like(db_ref)
            ds_ref[...] = jnp.zeros_like(ds_ref)

        has_prev = i < nb - 1
        ds = jnp.zeros((1, 128), F32)
        for g in range(SW_H // SW_G):
            dkp = dkc = dvp = dvc = jnp.zeros((BLK, SW_DH), F32)
            for h in range(g * SW_G, (g + 1) * SW_G):
                f = functools.partial(_swa_head, has_prev=has_prev, h=h)
                _, vjp = jax.vjp(f, q_ref[:, _qs(h)], kp_ref[:, _kvs(h)], kc_ref[:, _kvs(h)], vp_ref[:, _kvs(h)],
                                 vc_ref[:, _kvs(h)], b_ref[h, :, :BLK], b_ref[h, :, BLK:], s_ref[...])
                dq, a, b, c, d, dbp, dbc, ds_h = vjp(dh_ref[:, _qs(h)])
                dkp, dkc, dvp, dvc, ds = dkp + a, dkc + b, dvp + c, dvc + d, ds + ds_h
                dq_ref[:, _qs(h)] = dq.astype(BF16)
                db_ref[h, :, :BLK] += dbp
                db_ref[h, :, BLK:] += dbc
            kv = slice(g * SW_DH, (g + 1) * SW_DH)
            dk_ref[:, kv] = (dkc + ck_s[:, kv]).astype(BF16)
            dv_ref[:, kv] = (dvc + cv_s[:, kv]).astype(BF16)
            nk_s[:, kv] = dkp
            nv_s[:, kv] = dvp
        ck_s[...] = nk_s[...]
        cv_s[...] = nv_s[...]
        ds_ref[...] += ds

    def cur(j):
        return pl.BlockSpec((BLK, 128), lambda i, j=j: (nb - 1 - i, j))

    def prev(j):
        return pl.BlockSpec((BLK, 128), lambda i, j=j: (jnp.maximum(nb - 2 - i, 0), j))

    wide = pl.BlockSpec((BLK, SW_W), lambda i: (nb - 1 - i, 0))
    bsp = pl.BlockSpec((SW_H, BLK, 2 * BLK), lambda i: (0, 0, 0))
    vec = pl.BlockSpec((1, 128), lambda i: (0, 0))
    return pl.pallas_call(
        body, name=name, grid=(nb,),
        in_specs=[pl.BlockSpec((BLK, SW_W), lambda i: (nb - 1 - i, 1)), pl.BlockSpec((BLK, SW_W), lambda i: (nb - 1 - i, 4)),
                  prev(20), cur(20), prev(21), cur(21), bsp, vec],
        out_specs=[wide, cur(0), cur(0), bsp, vec],
        out_shape=[_sds((S, SW_W), BF16), _sds((S, 128), BF16), _sds((S, 128), BF16), _sds((SW_H, BLK, 2 * BLK), F32),
                   _sds((1, 128), F32)],
        scratch_shapes=[pltpu.VMEM((BLK, 128), F32)] * 4,
        compiler_params=_params(("arbitrary",)))(dh, proj, proj, proj, proj, proj, bias, sinks)


XA_TM = 512


def _xa_head(qh, kh, vh):
    logits = dot_nt(qh, kh) * (XA_DH ** -0.5)
    mx = lax.stop_gradient(jnp.max(logits, axis=1, keepdims=True))
    e = jnp.exp(logits - mx)
    return dot_nn(e / jnp.sum(e, axis=1, keepdims=True), vh)


def _xs(h, off=0):
    return slice(off + h * XA_DH, off + (h + 1) * XA_DH)


def _xattn_fwd(q, kv, *, name):
    S = q.shape[0]
    M = kv.shape[0]

    def body(q_ref, kv_ref, o_ref):
        for h in range(XA_H):
            o_ref[:, _xs(h)] = _xa_head(q_ref[:, _xs(h)], kv_ref[:, _xs(h)], kv_ref[:, _xs(h, D)]).astype(BF16)

    row = pl.BlockSpec((XA_TM, D), lambda i: (i, 0))
    return pl.pallas_call(
        body, name=name, grid=(S // XA_TM,), in_specs=[row, pl.BlockSpec((M, 2 * D), lambda i: (0, 0))], out_specs=row,
        out_shape=_sds((S, D), BF16), compiler_params=_params(("parallel",)))(q, kv)


def _xattn_bwd(do, q, kv, *, name):
    S = q.shape[0]
    M = kv.shape[0]

    def body(do_ref, q_ref, kv_ref, dq_ref, dkv_ref):
        @pl.when(pl.program_id(0) == 0)
        def _():
            dkv_ref[...] = jnp.zeros_like(dkv_ref)

        for h in range(XA_H):
            _, vjp = jax.vjp(_xa_head, q_ref[:, _xs(h)], kv_ref[:, _xs(h)], kv_ref[:, _xs(h, D)])
            dq, dk, dv = vjp(do_ref[:, _xs(h)])
            dq_ref[:, _xs(h)] = dq.astype(BF16)
            dkv_ref[:, _xs(h)] += dk
            dkv_ref[:, _xs(h, D)] += dv

    row = pl.BlockSpec((XA_TM, D), lambda i: (i, 0))
    full = pl.BlockSpec((M, 2 * D), lambda i: (0, 0))
    return pl.pallas_call(
        body, name=name, grid=(S // XA_TM,), in_specs=[row, row, full], out_specs=[row, full],
        out_shape=[_sds((S, D), BF16), _sds((M, 2 * D), F32)], compiler_params=_params(("arbitrary",)))(do, q, kv)


def _loss_head(y, tgt, *, name):
    S = y.shape[0]
    tm = 512

    def body(y_ref, t_ref, l_ref, dy_ref):
        e = y_ref[...] - t_ref[...]
        dy_ref[...] = e * (1.0 / D)
        part = 0.5 * jnp.sum(jnp.sum(e * e, axis=1, keepdims=True) * (1.0 / D), axis=0, keepdims=True)

        @pl.when(pl.program_id(0) == 0)
        def _():
            l_ref[...] = jnp.broadcast_to(part, (8, 128))

        @pl.when(pl.program_id(0) > 0)
        def _():
            l_ref[...] += jnp.broadcast_to(part, (8, 128))

    row = pl.BlockSpec((tm, D), lambda i: (i, 0))
    return pl.pallas_call(
        body, name=name, grid=(S // tm,), in_specs=[row, row], out_specs=[pl.BlockSpec((8, 128), lambda i: (0, 0)), row],
        out_shape=[_sds((8, 128), F32), _sds((S, D), F32)], compiler_params=_params(("arbitrary",)))(y, tgt)


def _exchange(src, *, gather, name):
    shape = (N_DEV,) + tuple(src.shape[-2:])

    def body(src_ref, out_ref, send_sems, recv_sems, local_sem):
        x, y, c = lax.axis_index("x"), lax.axis_index("y"), lax.axis_index("c")
        me = 4 * x + 2 * y + c

        def mine(p):
            return src_ref if gather else src_ref.at[p]

        copies = []
        for j in range(1, N_DEV):
            px, py, pc = x ^ ((j >> 2) & 1), y ^ ((j >> 1) & 1), c ^ (j & 1)
            p = 4 * px + 2 * py + pc
            send = pltpu.make_async_remote_copy(src_ref=mine(p), dst_ref=out_ref.at[me], send_sem=send_sems.at[j - 1],
                                                recv_sem=recv_sems.at[j - 1], device_id=(px, py, pc),
                                                device_id_type=pl.DeviceIdType.MESH)
            recv = pltpu.make_async_remote_copy(src_ref=mine(p), dst_ref=out_ref.at[p], send_sem=send_sems.at[j - 1],
                                                recv_sem=recv_sems.at[j - 1], device_id=(px, py, pc),
                                                device_id_type=pl.DeviceIdType.MESH)
            send.start()
            copies.append((send, recv))
        local = pltpu.make_async_copy(mine(me), out_ref.at[me], local_sem)
        local.start()
        for send, recv in copies:
            recv.wait_recv()
        for send, recv in copies:
            send.wait_send()
        local.wait()

    return pl.pallas_call(
        body, name=name, in_specs=[pl.BlockSpec(memory_space=pl.ANY)], out_specs=pl.BlockSpec(memory_space=pl.ANY),
        out_shape=_sds(shape, src.dtype),
        scratch_shapes=[pltpu.SemaphoreType.DMA((N_DEV - 1,)), pltpu.SemaphoreType.DMA((N_DEV - 1,)), pltpu.SemaphoreType.DMA],
        compiler_params=pltpu.CompilerParams(has_side_effects=True))(src)


def _adamw(parts, w, m, v, *, name):
    R = w.shape[0]
    tr = R
    for cand in (1024, 512, 256, 128, 64, 48, 32, 16, 8):
        if R % cand == 0:
            tr = cand
            break
    c1 = 1.0 / (1.0 - ADAM_B1 ** ADAM_STEP)
    c2 = 1.0 / (1.0 - ADAM_B2 ** ADAM_STEP)

    def body(p_ref, w_ref, m_ref, v_ref, g_ref, d_ref, nm_ref, nv_ref):
        g = p_ref[0].astype(F32)
        for j in range(1, N_DEV):
            g = g + p_ref[j].astype(F32)
        nm = ADAM_B1 * m_ref[...] + (1.0 - ADAM_B1) * g
        nv = ADAM_B2 * v_ref[...] + (1.0 - ADAM_B2) * (g * g)
        g_ref[...] = g
        nm_ref[...] = nm
        nv_ref[...] = nv
        d_ref[...] = -ADAM_LR * ((nm * c1) / (jnp.sqrt(nv * c2) + ADAM_EPS) + ADAM_WD * w_ref[...])

    row = pl.BlockSpec((tr, 128), lambda i: (i, 0))
    out = _sds((R, 128), F32)
    return pl.pallas_call(
        body, name=name, grid=(R // tr,), in_specs=[pl.BlockSpec((N_DEV, tr, 128), lambda i: (0, i, 0)), row, row, row],
        out_specs=[row, row, row, row], out_shape=[out, out, out, out], compiler_params=_params(("parallel",)))(parts, w, m, v)


BIG = ("w_in", "w_out", "xa_wq", "xa_wkv", "xa_wo", "ffn_w_up", "ffn_w_down")
COL_SHARDED = ("w_in", "xa_wkv", "ffn_w_up", "ml_conv_w", "ffn_conv_w")
SHARDED_SMALL = ("ml_conv_w", "ffn_conv_w")
REPLICATED = ("rel_bias", "ml_conv_b", "ml_i_bias", "ml_f_bias", "ml_norm_g", "swa_sinks", "ln1_g", "ln1_b", "ln2_g", "ln2_b",
              "ffn_conv_b", "ln3_g", "ln3_b")
NAMES = ("rel_bias", "w_in", "ml_conv_w", "ml_conv_b", "ml_i_bias", "ml_f_bias", "ml_norm_g", "swa_sinks", "w_out", "ln1_g", "ln1_b",
         "xa_wq", "xa_wkv", "xa_wo", "ln2_g", "ln2_b", "ffn_w_up", "ffn_conv_w", "ffn_conv_b", "ffn_w_down", "ln3_g", "ln3_b")


def _flat_rows(a, mult):
    f = a.reshape(-1)
    n = -(-f.shape[0] // (128 * mult)) * (128 * mult)
    if n != f.shape[0]:
        f = jnp.pad(f, (0, n - f.shape[0]))
    return f.reshape(-1, 128)


def _pack(arrs, mult):
    parts = [_flat_rows(a, mult) for a in arrs]
    return jnp.concatenate(parts, axis=0), [p.shape[0] for p in parts]


def _unpack(flat, rows, shapes):
    out, off = [], 0
    lead = flat.shape[:-2]
    for r, shp in zip(rows, shapes):
        n = int(np.prod(shp))
        piece = flat[..., off:off + r, :].reshape(lead + (r * 128,))[..., :n]
        out.append(piece.reshape(lead + tuple(shp)))
        off += r
    return out


def _full_from_shards(stacked, name):
    if name in COL_SHARDED:
        return jnp.moveaxis(stacked, 0, 2).reshape(stacked.shape[1], stacked.shape[2], N_DEV * stacked.shape[3])
    return jnp.moveaxis(stacked, 0, 1).reshape(stacked.shape[1], N_DEV * stacked.shape[2], stacked.shape[3])


def _shards_from_full(full, name):
    L, A, B = full.shape
    if name in COL_SHARDED:
        return jnp.moveaxis(full.reshape(L, A, N_DEV, B // N_DEV), 2, 0)
    return jnp.moveaxis(full.reshape(L, N_DEV, A // N_DEV, B), 1, 0)


def _pad_win(w):
    z = jnp.zeros(w.shape[:-1] + (NP_IN - N_IN,), w.dtype)
    return jnp.concatenate([w[..., :2048], w[..., 2056:], w[..., 2048:2056], z], axis=-1)


def _unpad_win(w):
    return jnp.concatenate([w[..., :2048], w[..., 2816:2824], w[..., 2048:2816]], axis=-1)


def _row128(v):
    return jnp.pad(v, (0, 128 - v.shape[0])).reshape(1, 128)


def kernel(x, mem, rel_bias, w_in, ml_conv_w, ml_conv_b, ml_i_bias, ml_f_bias, ml_norm_g, swa_sinks, w_out, ln1_g, ln1_b, xa_wq, xa_wkv, xa_wo, ln2_g, ln2_b, ffn_w_up, ffn_conv_w, ffn_conv_b, ffn_w_down, ln3_g, ln3_b, loss_target, m_rel_bias, m_w_in, m_ml_conv_w, m_ml_conv_b, m_ml_i_bias, m_ml_f_bias, m_ml_norm_g, m_swa_sinks, m_w_out, m_ln1_g, m_ln1_b, m_xa_wq, m_xa_wkv, m_xa_wo, m_ln2_g, m_ln2_b, m_ffn_w_up, m_ffn_conv_w, m_ffn_conv_b, m_ffn_w_down, m_ln3_g, m_ln3_b, v_rel_bias, v_w_in, v_ml_conv_w, v_ml_conv_b, v_ml_i_bias, v_ml_f_bias, v_ml_norm_g, v_swa_sinks, v_w_out, v_ln1_g, v_ln1_b, v_xa_wq, v_xa_wkv, v_xa_wo, v_ln2_g, v_ln2_b, v_ffn_w_up, v_ffn_conv_w, v_ffn_conv_b, v_ffn_w_down, v_ln3_g, v_ln3_b):
    W = dict(rel_bias=rel_bias, w_in=w_in, ml_conv_w=ml_conv_w, ml_conv_b=ml_conv_b, ml_i_bias=ml_i_bias, ml_f_bias=ml_f_bias,
             ml_norm_g=ml_norm_g, swa_sinks=swa_sinks, w_out=w_out, ln1_g=ln1_g, ln1_b=ln1_b, xa_wq=xa_wq, xa_wkv=xa_wkv,
             xa_wo=xa_wo, ln2_g=ln2_g, ln2_b=ln2_b, ffn_w_up=ffn_w_up, ffn_conv_w=ffn_conv_w, ffn_conv_b=ffn_conv_b,
             ffn_w_down=ffn_w_down, ln3_g=ln3_g, ln3_b=ln3_b)
    Mo = dict(rel_bias=m_rel_bias, w_in=m_w_in, ml_conv_w=m_ml_conv_w, ml_conv_b=m_ml_conv_b, ml_i_bias=m_ml_i_bias,
              ml_f_bias=m_ml_f_bias, ml_norm_g=m_ml_norm_g, swa_sinks=m_swa_sinks, w_out=m_w_out, ln1_g=m_ln1_g, ln1_b=m_ln1_b,
              xa_wq=m_xa_wq, xa_wkv=m_xa_wkv, xa_wo=m_xa_wo, ln2_g=m_ln2_g, ln2_b=m_ln2_b, ffn_w_up=m_ffn_w_up,
              ffn_conv_w=m_ffn_conv_w, ffn_conv_b=m_ffn_conv_b, ffn_w_down=m_ffn_w_down, ln3_g=m_ln3_g, ln3_b=m_ln3_b)
    Vo = dict(rel_bias=v_rel_bias, w_in=v_w_in, ml_conv_w=v_ml_conv_w, ml_conv_b=v_ml_conv_b, ml_i_bias=v_ml_i_bias,
              ml_f_bias=v_ml_f_bias, ml_norm_g=v_ml_norm_g, swa_sinks=v_swa_sinks, w_out=v_w_out, ln1_g=v_ln1_g, ln1_b=v_ln1_b,
              xa_wq=v_xa_wq, xa_wkv=v_xa_wkv, xa_wo=v_xa_wo, ln2_g=v_ln2_g, ln2_b=v_ln2_b, ffn_w_up=v_ffn_w_up,
              ffn_conv_w=v_ffn_conv_w, ffn_conv_b=v_ffn_conv_b, ffn_w_down=v_ffn_w_down, ln3_g=v_ln3_g, ln3_b=v_ln3_b)
    S = x.shape[1]
    me = 4 * lax.axis_index("x") + 2 * lax.axis_index("y") + lax.axis_index("c")
    xs = x.reshape(S, D)
    mems = mem.reshape(mem.shape[1], D)
    tgt = loss_target.reshape(S, D)

    big_flat, big_rows = _pack([W[n].astype(BF16) for n in BIG], 16)
    big_all = _exchange(big_flat, gather=True, name="gather_weights")
    full = {n: _full_from_shards(s, n) for n, s in zip(BIG, _unpack(big_all, big_rows, [W[n].shape for n in BIG]))}
    sm_flat, sm_rows = _pack([W[n] for n in SHARDED_SMALL], 8)
    sm_all = _exchange(sm_flat, gather=True, name="gather_conv_weights")
    full.update({n: _full_from_shards(s, n)
                 for n, s in zip(SHARDED_SMALL, _unpack(sm_all, sm_rows, [W[n].shape for n in SHARDED_SMALL]))})
    win_p = _pad_win(full["w_in"])

    bucket = jnp.asarray(_t5_buckets())
    bias = _bias_table(rel_bias, bucket, name="bias_table")

    saved = []
    h0 = xs
    for l in range(DEPTH):
        gbias = _row128(jnp.concatenate([ml_i_bias[l], ml_f_bias[l]]))
        sinks = _row128(swa_sinks[l])
        ng = ml_norm_g[l].reshape(1, ML_W)
        proj = _mm(h0, win_p[l], name=f"proj{l}")
        qk = _silu_conv_fwd(proj, full["ml_conv_w"][l], ml_conv_b[l].reshape(1, -1), name=f"mlconv{l}")
        h_ml, cs, ns, ms = _mlstm_fwd(qk, proj, gbias, ng, name=f"mlstm{l}")
        h_sw = _swa_fwd(proj, bias, sinks, name=f"swa{l}")
        hcat = jnp.concatenate([h_ml, h_sw], axis=1)
        h1, z1 = _mm_res_ln(hcat, full["w_out"][l], h0, ln1_g[l].reshape(1, D), ln1_b[l].reshape(1, D), name=f"mix_out{l}")
        q = _mm(h1, full["xa_wq"][l], name=f"xa_q{l}")
        kv = _mm(mems, full["xa_wkv"][l], tm=256, name=f"xa_kv{l}")
        o = _xattn_fwd(q, kv, name=f"xattn{l}")
        h2, z2 = _mm_res_ln(o, full["xa_wo"][l], h1, ln2_g[l].reshape(1, D), ln2_b[l].reshape(1, D), name=f"xa_out{l}")
        up = _mm(h2, full["ffn_w_up"][l], name=f"ffn_up{l}")
        cwf, cbf = full["ffn_conv_w"][l], ffn_conv_b[l].reshape(1, -1)
        act = _ffn_gate_fwd(up, cwf, cbf, name=f"ffn_gate{l}")
        h3, z3 = _mm_res_ln(act, full["ffn_w_down"][l], h2, ln3_g[l].reshape(1, D), ln3_b[l].reshape(1, D), name=f"ffn_out{l}")
        saved.append(dict(h0=h0, proj=proj, qk=qk, cs=cs, ns=ns, ms=ms, hcat=hcat, z1=z1, h1=h1, q=q, kv=kv, o=o, z2=z2, h2=h2,
                          up=up, act=act, z3=z3, gbias=gbias, sinks=sinks, ng=ng))
        h0 = h3

    loss_part, dh = _loss_head(h0, tgt, name="loss_head")

    G = {n: [None] * DEPTH for n in NAMES if n != "rel_bias"}
    dbias_tot = None
    for l in reversed(range(DEPTH)):
        sv = saved[l]
        dz3, G["ln3_g"][l], G["ln3_b"][l] = _ln_bwd(dh, sv["z3"], ln3_g[l].reshape(1, D), name=f"ln3_bwd{l}")
        G["ffn_w_down"][l] = _mm_tn(sv["act"], dz3, name=f"d_w_down{l}")
        dact = _mm(dz3, full["ffn_w_down"][l], trans_b=True, name=f"d_act{l}")
        dupg, dupv, G["ffn_conv_w"][l], G["ffn_conv_b"][l] = _ffn_gate_bwd(
            dact, sv["up"], full["ffn_conv_w"][l], ffn_conv_b[l].reshape(1, -1), name=f"ffn_gate_bwd{l}")
        dup = jnp.concatenate([dupg, dupv], axis=1)
        G["ffn_w_up"][l] = _mm_tn(sv["h2"], dup, name=f"d_w_up{l}")
        dh2 = _mm(dup, full["ffn_w_up"][l], trans_b=True, add=dz3, add_scale=ALPHA, name=f"d_h2_{l}")

        dz2, G["ln2_g"][l], G["ln2_b"][l] = _ln_bwd(dh2, sv["z2"], ln2_g[l].reshape(1, D), name=f"ln2_bwd{l}")
        G["xa_wo"][l] = _mm_tn(sv["o"], dz2, name=f"d_xa_wo{l}")
        do = _mm(dz2, full["xa_wo"][l], trans_b=True, name=f"d_xa_o{l}")
        dq, dkv = _xattn_bwd(do, sv["q"], sv["kv"], name=f"xattn_bwd{l}")
        G["xa_wkv"][l] = _mm_tn(mems, dkv, name=f"d_xa_wkv{l}")
        G["xa_wq"][l] = _mm_tn(sv["h1"], dq, name=f"d_xa_wq{l}")
        dh1 = _mm(dq, full["xa_wq"][l], trans_b=True, add=dz2, add_scale=ALPHA, name=f"d_h1_{l}")

        dz1, G["ln1_g"][l], G["ln1_b"][l] = _ln_bwd(dh1, sv["z1"], ln1_g[l].reshape(1, D), name=f"ln1_bwd{l}")
        G["w_out"][l] = _mm_tn(sv["hcat"], dz1, name=f"d_w_out{l}")
        dhcat = _mm(dz1, full["w_out"][l], trans_b=True, name=f"d_hcat{l}")
        dswq, dswk, dswv, dbias_l, dsinks = _swa_bwd(dhcat, sv["proj"], bias, sv["sinks"], name=f"swa_bwd{l}")
        dbias_tot = dbias_l if dbias_tot is None else dbias_tot + dbias_l
        dqk, dv, dop, dgates, dgb, dng = _mlstm_bwd(dhcat, sv["qk"], sv["proj"], sv["gbias"], sv["ng"], sv["cs"], sv["ns"], sv["ms"],
                                                    name=f"mlstm_bwd{l}")
        dqk_pre, G["ml_conv_w"][l], G["ml_conv_b"][l] = _silu_conv_bwd(
            dqk, sv["proj"], full["ml_conv_w"][l], ml_conv_b[l].reshape(1, -1), name=f"mlconv_bwd{l}")
        dproj = jnp.concatenate([dqk_pre, dv, dop, dswq, dswk, dswv, dgates, jnp.zeros((S, NP_IN - 2944), BF16)], axis=1)
        G["w_in"][l] = _unpad_win(_mm_tn(sv["h0"], dproj, name=f"d_w_in{l}"))
        dh = _mm(dproj, win_p[l], trans_b=True, add=dz1, add_scale=ALPHA, name=f"d_h0_{l}")
        G["ml_i_bias"][l] = dgb[0, :ML_H]
        G["ml_f_bias"][l] = dgb[0, ML_H:2 * ML_H]
        G["ml_norm_g"][l] = dng
        G["swa_sinks"][l] = dsinks[0, :SW_H]
    grad_x = dh.reshape(x.shape)
    d_rel = _bias_table_bwd(dbias_tot, bucket, name="bias_table_bwd")[:, :SW_H]

    Gf = {n: jnp.stack([g.reshape(W[n].shape[1:]) if n in REPLICATED else g for g in G[n]]) for n in G}
    Gf["rel_bias"] = d_rel

    send_flat = jnp.concatenate([_shards_from_full(Gf[n], n).reshape(N_DEV, -1, 128) for n in BIG], axis=1)
    recv = _exchange(send_flat, gather=False, name="exchange_grads")
    rows_f32 = [int(np.prod(W[n].shape)) // 128 for n in BIG]
    wf = jnp.concatenate([W[n].reshape(-1, 128) for n in BIG], axis=0)
    mf = jnp.concatenate([Mo[n].reshape(-1, 128) for n in BIG], axis=0)
    vf = jnp.concatenate([Vo[n].reshape(-1, 128) for n in BIG], axis=0)
    outs_big = [_unpack(o_, rows_f32, [W[n].shape for n in BIG]) for o_ in _adamw(recv, wf, mf, vf, name="adamw_big")]

    small = REPLICATED + SHARDED_SMALL
    sp_flat, sp_rows = _pack([Gf[n] for n in small] + [loss_part], 8)
    sp_all = _exchange(sp_flat, gather=True, name="gather_small_grads")

    def widen(n, t):
        if n not in SHARDED_SMALL:
            return t[n]
        fullw = jnp.zeros(Gf[n].shape, F32)
        return lax.dynamic_update_slice(fullw, t[n], (0, 0, me * t[n].shape[2]))

    zl = jnp.zeros((8, 128), F32)
    wsm, _ = _pack([widen(n, W) for n in small] + [zl], 8)
    msm, _ = _pack([widen(n, Mo) for n in small] + [zl], 8)
    vsm, _ = _pack([widen(n, Vo) for n in small] + [zl], 8)
    outs_small = [_unpack(o_, sp_rows, [Gf[n].shape for n in small] + [(8, 128)]) for o_ in _adamw(sp_all, wsm, msm, vsm, name="adamw_small")]
    loss = outs_small[0][-1][0, 0]

    res = {}
    for kind, ob, os_ in zip(("g", "d", "m", "v"), outs_big, outs_small):
        for n, a in zip(BIG, ob):
            res[kind, n] = a
        for n, a in zip(small, os_[:-1]):
            if n in SHARDED_SMALL:
                a = lax.dynamic_slice(a, (0, 0, me * W[n].shape[2]), W[n].shape)
            res[kind, n] = a
    return (loss, grad_x, *[res["g", n] for n in NAMES], *[res["d", n] for n in NAMES], *[res["m", n] for n in NAMES],
            *[res["v", n] for n in NAMES])
```

```python
import functools
import math

import jax
import jax.numpy as jnp
import numpy as np
from jax import lax
from jax.experimental import pallas as pl
from jax.experimental.pallas import tpu as pltpu

F32 = jnp.float32
BF16 = jnp.bfloat16

N_DEV = 8
N_CHIP = 4
D = 1024
DEPTH = 2
ML_H = 4
ML_W = 512
ML_DH = 128
ML_L = 64
ML_CONV = 4
SW_DH = 64
SW_W = 512
SW_H = 8
SW_G = 4
SW_KVW = 128
BLK = 128
REL_B = 32
REL_MAXD = 128
XA_H = 4
XA_DH = 256
DFF = 2816
NB_FF = DFF // 128
FFN_CONV = 3
ALPHA = (2.0 * DEPTH) ** 0.25
EPS = 1e-5
N_IN = 2824
NP_IN = 3072
ML_GW = 2 * ML_W + 128
SW_GW = SW_W + 2 * SW_KVW
ADAM_LR = 0.001
ADAM_B1 = 0.9
ADAM_B2 = 0.999
ADAM_EPS = 1e-08
ADAM_WD = 0.01
ADAM_STEP = 10
VMEM_LIMIT = 56 * 1024 * 1024
MESH = pl.DeviceIdType.MESH

NN = ((1,), (0,))
NT = ((1,), (1,))
TN = ((0,), (0,))


def _dg(a, b, dn):
    return lax.dot_general(a.astype(BF16), b.astype(BF16), (dn, ((), ())), preferred_element_type=F32)


@jax.custom_vjp
def dot_nn(a, b):
    return _dg(a, b, NN)


dot_nn.defvjp(lambda a, b: (_dg(a, b, NN), (a, b)), lambda r, g: (_dg(g, r[1], NT), _dg(r[0], g, TN)))


@jax.custom_vjp
def dot_nt(a, b):
    return _dg(a, b, NT)


dot_nt.defvjp(lambda a, b: (_dg(a, b, NT), (a, b)), lambda r, g: (_dg(g, r[1], NN), _dg(g, r[0], TN)))


@jax.custom_vjp
def dot_tn(a, b):
    return _dg(a, b, TN)


dot_tn.defvjp(lambda a, b: (_dg(a, b, TN), (a, b)), lambda r, g: (_dg(r[1], g, NT), _dg(r[0], g, NN)))


def _params(sem=None):
    return pltpu.CompilerParams(dimension_semantics=sem, vmem_limit_bytes=VMEM_LIMIT)


def _sds(shape, dtype):
    return jax.ShapeDtypeStruct(tuple(shape), dtype)


def _mm(a, b, *, trans_b=False, out_dtype=F32, add=None, add_scale=1.0, tm=512, tn=512, name):
    M, K = a.shape
    N = b.shape[0] if trans_b else b.shape[1]
    tm = min(tm, M)
    tn = next(t for t in (tn, 384, 256, 128) if N % t == 0)
    assert M % tm == 0
    tk = next(t for t in (K, 1408, 1024) if K % t == 0 and t <= 1408)
    nk = K // tk
    has_add = add is not None

    def body(*refs):
        if has_add:
            a_ref, b_ref, add_ref, o_ref, acc_ref = refs
        else:
            a_ref, b_ref, o_ref, acc_ref = refs
        k = pl.program_id(2)
        p = _dg(a_ref[...], b_ref[...], NT if trans_b else NN)

        @pl.when(k == 0)
        def _():
            acc_ref[...] = p

        @pl.when(k > 0)
        def _():
            acc_ref[...] += p

        @pl.when(k == nk - 1)
        def _():
            r = acc_ref[...]
            if has_add:
                r = r + add_scale * add_ref[...].astype(F32)
            o_ref[...] = r.astype(out_dtype)

    in_specs = [pl.BlockSpec((tm, tk), lambda i, j, k: (i, k)),
                pl.BlockSpec((tn, tk), lambda i, j, k: (j, k)) if trans_b else pl.BlockSpec((tk, tn), lambda i, j, k: (k, j))]
    args = [a, b]
    if has_add:
        in_specs.append(pl.BlockSpec((tm, tn), lambda i, j, k: (i, j)))
        args.append(add)
    return pl.pallas_call(
        body, name=name, grid=(M // tm, N // tn, nk), in_specs=in_specs,
        out_specs=pl.BlockSpec((tm, tn), lambda i, j, k: (i, j)), out_shape=_sds((M, N), out_dtype),
        scratch_shapes=[pltpu.VMEM((tm, tn), F32)],
        compiler_params=_params(("parallel", "parallel", "arbitrary")))(*args)


def _mm_tn(a, g, *, name):
    S, K = a.shape
    N = g.shape[1]
    tk = K if K <= 1024 else K // 2
    tn = next(t for t in (512, 384, 256, 128) if N % t == 0)
    ts = min(512, S)
    ns = S // ts
    assert K % tk == 0 and S % ts == 0

    def body(a_ref, g_ref, o_ref):
        s = pl.program_id(2)
        p = _dg(a_ref[...], g_ref[...], TN)

        @pl.when(s == 0)
        def _():
            o_ref[...] = p

        @pl.when(s > 0)
        def _():
            o_ref[...] += p

    return pl.pallas_call(
        body, name=name, grid=(K // tk, N // tn, ns),
        in_specs=[pl.BlockSpec((ts, tk), lambda i, j, s: (s, i)), pl.BlockSpec((ts, tn), lambda i, j, s: (s, j))],
        out_specs=pl.BlockSpec((tk, tn), lambda i, j, s: (i, j)), out_shape=_sds((K, N), F32),
        compiler_params=_params(("parallel", "parallel", "arbitrary")))(a, g)


def _mm_res_ln(a_list, w, resid, gam, bet, *, name):
    M, Ka = a_list[0].shape
    na = len(a_list)
    assert w.shape[0] == na * Ka
    tm = 256

    def body(*refs):
        a_refs, w_refs = refs[:na], refs[na:2 * na]
        r_ref, g_ref, b_ref, y_ref, z_ref = refs[2 * na:]
        z = ALPHA * r_ref[...]
        for a_ref, w_ref in zip(a_refs, w_refs):
            z = z + _dg(a_ref[...], w_ref[...], NN)
        mu = jnp.mean(z, axis=1, keepdims=True)
        zc = z - mu
        var = jnp.mean(zc * zc, axis=1, keepdims=True)
        y_ref[...] = zc * lax.rsqrt(var + EPS) * g_ref[...] + b_ref[...]
        z_ref[...] = z

    row = pl.BlockSpec((tm, D), lambda i: (i, 0))
    vec = pl.BlockSpec((1, D), lambda i: (0, 0))
    a_specs = [pl.BlockSpec((tm, Ka), lambda i: (i, 0)) for _ in a_list]
    w_specs = [pl.BlockSpec((Ka, D), lambda i, t=t: (t, 0)) for t in range(na)]
    return pl.pallas_call(
        body, name=name, grid=(M // tm,), in_specs=a_specs + w_specs + [row, vec, vec],
        out_specs=[row, row], out_shape=[_sds((M, D), F32), _sds((M, D), F32)],
        compiler_params=_params(("parallel",)))(*a_list, *([w] * na), resid, gam, bet)


def _ln_bwd(dy, z, gam, *, name):
    M = dy.shape[0]
    tm = 512

    def body(dy_ref, z_ref, g_ref, dz_ref, dg_ref, db_ref):
        i = pl.program_id(0)
        z = z_ref[...]
        dy_ = dy_ref[...]
        mu = jnp.mean(z, axis=1, keepdims=True)
        zc = z - mu
        var = jnp.mean(zc * zc, axis=1, keepdims=True)
        rstd = lax.rsqrt(var + EPS)
        xh = zc * rstd
        dxh = dy_ * g_ref[...]
        m1 = jnp.mean(dxh, axis=1, keepdims=True)
        m2 = jnp.mean(dxh * xh, axis=1, keepdims=True)
        dz_ref[...] = rstd * (dxh - m1 - xh * m2)
        pg = jnp.sum(dy_ * xh, axis=0, keepdims=True)
        pb = jnp.sum(dy_, axis=0, keepdims=True)

        @pl.when(i == 0)
        def _():
            dg_ref[...] = pg
            db_ref[...] = pb

        @pl.when(i > 0)
        def _():
            dg_ref[...] += pg
            db_ref[...] += pb

    row = pl.BlockSpec((tm, D), lambda i: (i, 0))
    vec = pl.BlockSpec((1, D), lambda i: (0, 0))
    return pl.pallas_call(
        body, name=name, grid=(M // tm,), in_specs=[row, row, vec], out_specs=[row, vec, vec],
        out_shape=[_sds((M, D), F32), _sds((1, D), F32), _sds((1, D), F32)],
        compiler_params=_params(("arbitrary",)))(dy, z, gam)


def _shift_down(x, d):
    if d == 0:
        return x
    rows = lax.broadcasted_iota(jnp.int32, x.shape, 0)
    return jnp.where(rows >= d, pltpu.roll(x, d, 0), 0.0)


def _shift_up(x, d):
    if d == 0:
        return x
    S = x.shape[0]
    rows = lax.broadcasted_iota(jnp.int32, x.shape, 0)
    return jnp.where(rows < S - d, pltpu.roll(x, S - d, 0), 0.0)


def _conv(x, w_ref, b_ref, cs, K):
    y = b_ref[:, cs]
    for j in range(K):
        y = y + _shift_down(x, K - 1 - j) * w_ref[j:j + 1, cs]
    return y


def _conv_bwd(dy, x, w_ref, dw_ref, db_ref, cs, K):
    dx = jnp.zeros_like(x)
    for j in range(K):
        dx = dx + _shift_up(dy, K - 1 - j) * w_ref[j:j + 1, cs]
        dw_ref[j:j + 1, cs] = jnp.sum(dy * _shift_down(x, K - 1 - j), axis=0, keepdims=True)
    db_ref[:, cs] = jnp.sum(dy, axis=0, keepdims=True)
    return dx


ALL = slice(None)


def _silu_conv_fwd(proj, cw, cb, *, name):
    S = proj.shape[0]

    def body(x_ref, w_ref, b_ref, o_ref):
        o_ref[...] = jax.nn.silu(_conv(x_ref[...], w_ref, b_ref, ALL, ML_CONV))

    col = pl.BlockSpec((S, 128), lambda j: (0, j))
    return pl.pallas_call(
        body, name=name, grid=(8,),
        in_specs=[col, pl.BlockSpec((ML_CONV, 128), lambda j: (0, j)), pl.BlockSpec((1, 128), lambda j: (0, j))],
        out_specs=col, out_shape=_sds((S, 2 * ML_W), F32), compiler_params=_params(("parallel",)))(proj, cw, cb)


def _silu_conv_bwd(dqk, proj, cw, cb, *, name):
    S = proj.shape[0]

    def body(d_ref, x_ref, w_ref, b_ref, dx_ref, dw_ref, db_ref):
        x = x_ref[...]
        y = _conv(x, w_ref, b_ref, ALL, ML_CONV)
        dy = jax.vjp(jax.nn.silu, y)[1](d_ref[...])[0]
        dx_ref[...] = _conv_bwd(dy, x, w_ref, dw_ref, db_ref, ALL, ML_CONV).astype(BF16)

    col = pl.BlockSpec((S, 128), lambda j: (0, j))
    wsp = pl.BlockSpec((ML_CONV, 128), lambda j: (0, j))
    bsp = pl.BlockSpec((1, 128), lambda j: (0, j))
    return pl.pallas_call(
        body, name=name, grid=(8,), in_specs=[col, col, wsp, bsp], out_specs=[col, wsp, bsp],
        out_shape=[_sds((S, 2 * ML_W), BF16), _sds((ML_CONV, 2 * ML_W), F32), _sds((1, 2 * ML_W), F32)],
        compiler_params=_params(("parallel",)))(dqk, proj, cw, cb)


def _gate(ug, uv):
    return jax.nn.gelu(ug) * uv


GCOL = slice(0, 128)
VCOL = slice(128, 256)


def _ffn_gate_fwd(up, cw, cb, *, name):
    S = up.shape[0]

    def body(x_ref, w_ref, b_ref, o_ref):
        ug = _conv(x_ref[:, GCOL], w_ref, b_ref, GCOL, FFN_CONV)
        uv = _conv(x_ref[:, VCOL], w_ref, b_ref, VCOL, FFN_CONV)
        o_ref[...] = _gate(ug, uv).astype(BF16)

    return pl.pallas_call(
        body, name=name, grid=(NB_FF,),
        in_specs=[pl.BlockSpec((S, 256), lambda j: (0, j)), pl.BlockSpec((FFN_CONV, 256), lambda j: (0, j)),
                  pl.BlockSpec((1, 256), lambda j: (0, j))],
        out_specs=pl.BlockSpec((S, 128), lambda j: (0, j)), out_shape=_sds((S, DFF), BF16),
        compiler_params=_params(("parallel",)))(up, cw, cb)


def _ffn_gate_bwd(da, up, cw, cb, *, name):
    S = up.shape[0]

    def body(da_ref, x_ref, w_ref, b_ref, dx_ref, dw_ref, db_ref):
        xg, xv = x_ref[:, GCOL], x_ref[:, VCOL]
        ug = _conv(xg, w_ref, b_ref, GCOL, FFN_CONV)
        uv = _conv(xv, w_ref, b_ref, VCOL, FFN_CONV)
        dug, duv = jax.vjp(_gate, ug, uv)[1](da_ref[...])
        dx_ref[:, GCOL] = _conv_bwd(dug, xg, w_ref, dw_ref, db_ref, GCOL, FFN_CONV).astype(BF16)
        dx_ref[:, VCOL] = _conv_bwd(duv, xv, w_ref, dw_ref, db_ref, VCOL, FFN_CONV).astype(BF16)

    xsp = pl.BlockSpec((S, 256), lambda j: (0, j))
    wsp = pl.BlockSpec((FFN_CONV, 256), lambda j: (0, j))
    bsp = pl.BlockSpec((1, 256), lambda j: (0, j))
    return pl.pallas_call(
        body, name=name, grid=(NB_FF,), in_specs=[pl.BlockSpec((S, 128), lambda j: (0, j)), xsp, wsp, bsp],
        out_specs=[xsp, wsp, bsp],
        out_shape=[_sds((S, 2 * DFF), BF16), _sds((FFN_CONV, 2 * DFF), F32), _sds((1, 2 * DFF), F32)],
        compiler_params=_params(("parallel",)))(da, up, cw, cb)


def _log_sigmoid(x):
    return jnp.minimum(x, 0.0) - jnp.log1p(jnp.exp(-jnp.abs(x)))


@jax.custom_vjp
def _clamp_div(num, den, floor, shift):
    return num / jnp.maximum(jnp.abs(den), floor)


def _clamp_div_fwd(num, den, floor, shift):
    out = num / jnp.maximum(jnp.abs(den), floor)
    return out, (den, floor, out)


def _clamp_div_bwd(res, g):
    den, floor, out = res
    active = jnp.abs(den) < floor
    dinv = jnp.maximum(jnp.abs(den), floor)
    go = jnp.sum(g * out, axis=1, keepdims=True)
    ddiv = -go / dinv
    return (g / dinv, jnp.where(active, 0.0, ddiv * jnp.sign(den)), jnp.where(active, ddiv, 0.0),
            jnp.sum(jnp.where(active, go, 0.0), axis=0, keepdims=True))


_clamp_div.defvjp(_clamp_div_fwd, _clamp_div_bwd)


def _ml_head(q, k, v, o_pre, gates, gbias, C, n, ng, m, shift, hidx):
    L = q.shape[0]
    lane = lax.broadcasted_iota(jnp.int32, (L, 128), 1)
    lane1 = lax.broadcasted_iota(jnp.int32, (1, 128), 1)
    gz = gates + jnp.where(lane1 < ML_H, lax.stop_gradient(gbias), gbias)
    ig = jnp.sum(jnp.where(lane == hidx, gz, 0.0), axis=1, keepdims=True)
    lf = _log_sigmoid(jnp.sum(jnp.where(lane == ML_H + hidx, gz, 0.0), axis=1, keepdims=True))
    r = lax.broadcasted_iota(jnp.int32, (L, L), 0)
    c = lax.broadcasted_iota(jnp.int32, (L, L), 1)
    eye, tril = r == c, r >= c

    def to_row(col):
        return jnp.sum(jnp.where(eye, col, 0.0), axis=0, keepdims=True)

    b_col = jnp.sum(jnp.where(tril, to_row(lf), 0.0), axis=1, keepdims=True)
    Dm = jnp.where(tril, b_col - to_row(b_col) + to_row(ig), -jnp.inf)
    inter = b_col + m
    m_t = jnp.maximum(inter, jnp.max(Dm, axis=1, keepdims=True))
    w_inter = jnp.exp(inter - m_t)
    ks = k * (ML_DH ** -0.5)
    s = dot_nt(q, ks) * jnp.exp(Dm - m_t)
    num = w_inter * dot_nn(q, C) + dot_nn(s, v)
    den = w_inter * jnp.sum(q * n, axis=1, keepdims=True) + jnp.sum(s, axis=1, keepdims=True)
    h = _clamp_div(num, den, jnp.exp(-m_t), shift)
    g = jnp.sum(lf, axis=0, keepdims=True)
    a = g - b_col + ig
    m_new = jnp.maximum(g + m, jnp.max(a, axis=0, keepdims=True))
    decay = jnp.exp(g + m - m_new)
    wk = jnp.exp(a - m_new)
    C_new = decay * C + dot_tn(ks * wk, v)
    n_new = decay * n + jnp.sum(wk * ks, axis=0, keepdims=True)
    mu = jnp.mean(h, axis=1, keepdims=True)
    hc = h - mu
    var = jnp.mean(hc * hc, axis=1, keepdims=True)
    out = jax.nn.sigmoid(o_pre) * (hc * lax.rsqrt(var + EPS) * ng)
    return out, C_new, n_new, m_new


def _hs(h, off=0):
    return slice(off + h * ML_DH, off + (h + 1) * ML_DH)


def _mlstm_fwd(qk, proj, gbias, ng, *, name):
    S = qk.shape[0]
    nc = S // ML_L

    def body(q_ref, k_ref, v_ref, o_ref, g_ref, gb_ref, ng_ref, h_ref, cs_ref, ns_ref, ms_ref, c_s, n_s, m_s):
        @pl.when(pl.program_id(0) == 0)
        def _():
            c_s[...] = jnp.zeros_like(c_s)
            n_s[...] = jnp.zeros_like(n_s)
            m_s[...] = jnp.zeros_like(m_s)

        gates = g_ref[...]
        for h in range(ML_H):
            C, n, m = c_s[h], n_s[h:h + 1, :], m_s[h:h + 1, 0:1]
            cs_ref[0, h] = C
            ns_ref[0, h] = n
            ms_ref[0, h] = m_s[h:h + 1, :]
            out, C2, n2, m2 = _ml_head(q_ref[:, _hs(h)], k_ref[:, _hs(h)], v_ref[:, _hs(h)], o_ref[:, _hs(h)], gates,
                                       gb_ref[...], C, n, ng_ref[:, _hs(h)], m, jnp.zeros((1, 1), F32), h)
            h_ref[:, _hs(h)] = out.astype(BF16)
            c_s[h] = C2
            n_s[h:h + 1, :] = n2
            m_s[h:h + 1, :] = jnp.broadcast_to(m2, (1, 128))

    def w(j):
        return pl.BlockSpec((ML_L, ML_W), lambda c, j=j: (c, j))

    return pl.pallas_call(
        body, name=name, grid=(nc,),
        in_specs=[w(0), w(1), w(2), w(3), pl.BlockSpec((ML_L, 128), lambda c: (c, 22)),
                  pl.BlockSpec((1, 128), lambda c: (0, 0)), pl.BlockSpec((1, ML_W), lambda c: (0, 0))],
        out_specs=[w(0), pl.BlockSpec((1, ML_H, ML_DH, ML_DH), lambda c: (c, 0, 0, 0)),
                   pl.BlockSpec((1, ML_H, 1, 128), lambda c: (c, 0, 0, 0)), pl.BlockSpec((1, ML_H, 1, 128), lambda c: (c, 0, 0, 0))],
        out_shape=[_sds((S, ML_W), BF16), _sds((nc, ML_H, ML_DH, ML_DH), F32), _sds((nc, ML_H, 1, 128), F32),
                   _sds((nc, ML_H, 1, 128), F32)],
        scratch_shapes=[pltpu.VMEM((ML_H, ML_DH, ML_DH), F32), pltpu.VMEM((8, 128), F32), pltpu.VMEM((8, 128), F32)],
        compiler_params=_params(("arbitrary",)))(qk, qk, proj, proj, proj, gbias, ng)


def _mlstm_bwd(dh, qk, proj, gbias, ng, cs, ns, ms, *, name):
    S = qk.shape[0]
    nc = S // ML_L

    def body(dh_ref, q_ref, k_ref, v_ref, o_ref, g_ref, gb_ref, ng_ref, cs_ref, ns_ref, ms_ref,
             dqk_ref, dml_ref, dgb_ref, dng_ref, dc_s, dn_s, dm_s):
        @pl.when(pl.program_id(0) == 0)
        def _():
            dc_s[...] = jnp.zeros_like(dc_s)
            dn_s[...] = jnp.zeros_like(dn_s)
            dm_s[...] = jnp.zeros_like(dm_s)
            dgb_ref[...] = jnp.zeros_like(dgb_ref)
            dng_ref[...] = jnp.zeros_like(dng_ref)

        gates = g_ref[...]
        dgates = jnp.zeros_like(gates)
        dgb = jnp.zeros((1, 128), F32)
        for h in range(ML_H):
            f = functools.partial(_ml_head, hidx=h)
            _, vjp = jax.vjp(f, q_ref[:, _hs(h)], k_ref[:, _hs(h)], v_ref[:, _hs(h)], o_ref[:, _hs(h)], gates, gb_ref[...],
                             cs_ref[0, h], ns_ref[0, h], ng_ref[:, _hs(h)], ms_ref[0, h][:, 0:1], jnp.zeros((1, 1), F32))
            dq, dk, dv, do, dg_h, dgb_h, dC, dn, dng, dm, dshift = vjp(
                (dh_ref[:, _hs(h)], dc_s[h], dn_s[h:h + 1, :], dm_s[h:h + 1, 0:1]))
            dm_s[h:h + 1, :] = jnp.broadcast_to(dm, (1, 128))
            dgb_h = jnp.where(lax.broadcasted_iota(jnp.int32, (1, 128), 1) == h, dshift, dgb_h)
            dqk_ref[:, _hs(h)] = dq
            dqk_ref[:, _hs(h, ML_W)] = dk
            dml_ref[:, _hs(h)] = dv.astype(BF16)
            dml_ref[:, _hs(h, ML_W)] = do.astype(BF16)
            dng_ref[:, _hs(h)] += dng
            dgates = dgates + dg_h
            dgb = dgb + dgb_h
            dc_s[h] = dC
            dn_s[h:h + 1, :] = dn
        dml_ref[:, 2 * ML_W:] = dgates.astype(BF16)
        dgb_ref[...] += dgb

    def w(j):
        return pl.BlockSpec((ML_L, ML_W), lambda c, j=j: (nc - 1 - c, j))

    vec = pl.BlockSpec((1, 128), lambda c: (0, 0))
    vecw = pl.BlockSpec((1, ML_W), lambda c: (0, 0))
    gsp = pl.BlockSpec((ML_L, 128), lambda c: (nc - 1 - c, 22))
    st = pl.BlockSpec((1, ML_H, 1, 128), lambda c: (nc - 1 - c, 0, 0, 0))
    return pl.pallas_call(
        body, name=name, grid=(nc,),
        in_specs=[w(0), w(0), w(1), w(2), w(3), gsp, vec, vecw,
                  pl.BlockSpec((1, ML_H, ML_DH, ML_DH), lambda c: (nc - 1 - c, 0, 0, 0)), st, st],
        out_specs=[pl.BlockSpec((ML_L, 2 * ML_W), lambda c: (nc - 1 - c, 0)), pl.BlockSpec((ML_L, ML_GW), lambda c: (nc - 1 - c, 0)),
                   vec, vecw],
        out_shape=[_sds((S, 2 * ML_W), F32), _sds((S, ML_GW), BF16), _sds((1, 128), F32), _sds((1, ML_W), F32)],
        scratch_shapes=[pltpu.VMEM((ML_H, ML_DH, ML_DH), F32), pltpu.VMEM((8, 128), F32), pltpu.VMEM((8, 128), F32)],
        compiler_params=_params(("arbitrary",)))(dh, qk, qk, proj, proj, proj, gbias, ng, cs, ns, ms)


def _t5_buckets():
    r = np.arange(BLK)[:, None]
    c = np.arange(2 * BLK)[None, :]
    n = np.maximum(BLK + r - c, 0)
    max_exact = REL_B // 2
    nf = np.maximum(n, 1).astype(np.float32)
    large = max_exact + (np.log(nf / np.float32(max_exact)) / np.float32(math.log(REL_MAXD / max_exact))
                         * np.float32(REL_B - max_exact)).astype(np.int32)
    large = np.minimum(large, REL_B - 1)
    return np.where(n < max_exact, n, large).astype(np.int32)


def _bias_table(rel_bias, bucket, *, name):
    def body(rb_ref, bk_ref, o_ref):
        bk = bk_ref[...]
        for h in range(SW_H):
            acc = jnp.zeros((BLK, 2 * BLK), F32)
            for b in range(REL_B):
                acc = jnp.where(bk == b, rb_ref[b, h], acc)
            o_ref[h] = acc

    return pl.pallas_call(
        body, name=name, in_specs=[pl.BlockSpec(memory_space=pltpu.SMEM), pl.BlockSpec(memory_space=pltpu.VMEM)],
        out_specs=pl.BlockSpec(memory_space=pltpu.VMEM), out_shape=_sds((SW_H, BLK, 2 * BLK), F32),
        compiler_params=_params())(rel_bias, bucket)


def _bias_table_bwd(dbias_list, bucket, *, name):
    nl = len(dbias_list)

    def body(*refs):
        d_refs, bk_ref, o_ref = refs[:nl], refs[nl], refs[nl + 1]
        bk = bk_ref[...]
        rows = lax.broadcasted_iota(jnp.int32, (REL_B, 128), 0)
        lanes = lax.broadcasted_iota(jnp.int32, (REL_B, 128), 1)
        acc = jnp.zeros((REL_B, 128), F32)
        for h in range(SW_H):
            d = d_refs[0][h]
            for d_ref in d_refs[1:]:
                d = d + d_ref[h]
            for b in range(REL_B):
                t = jnp.sum(jnp.sum(jnp.where(bk == b, d, 0.0), axis=0, keepdims=True), axis=1, keepdims=True)
                acc = jnp.where((rows == b) & (lanes == h), t, acc)
        o_ref[...] = acc

    vm = pl.BlockSpec(memory_space=pltpu.VMEM)
    return pl.pallas_call(
        body, name=name, in_specs=[vm] * (nl + 1), out_specs=vm, out_shape=_sds((REL_B, 128), F32),
        compiler_params=_params())(*dbias_list, bucket)


def _swa_head(qh, kph, kch, vph, vch, bp, bc, sinks, has_prev, h):
    r = lax.broadcasted_iota(jnp.int32, (BLK, BLK), 0)
    c = lax.broadcasted_iota(jnp.int32, (BLK, BLK), 1)
    lane = lax.broadcasted_iota(jnp.int32, (1, 128), 1)
    sink = jnp.sum(jnp.where(lane == h, sinks, 0.0), axis=1, keepdims=True)
    lp = jnp.where((c > r) & has_prev, dot_nt(qh, kph) * (SW_DH ** -0.5) + bp, -jnp.inf)
    lc = jnp.where(c <= r, dot_nt(qh, kch) * (SW_DH ** -0.5) + bc, -jnp.inf)
    mx = jnp.maximum(jnp.maximum(jnp.max(lp, axis=1, keepdims=True), jnp.max(lc, axis=1, keepdims=True)), sink)
    mx = lax.stop_gradient(mx)
    pp, pc = jnp.exp(lp - mx), jnp.exp(lc - mx)
    den = jnp.sum(pp, axis=1, keepdims=True) + jnp.sum(pc, axis=1, keepdims=True) + jnp.exp(sink - mx)
    return dot_nn(pp / den, vph) + dot_nn(pc / den, vch)


def _qs(h):
    return slice(h * SW_DH, (h + 1) * SW_DH)


def _kvs(h):
    return slice((h // SW_G) * SW_DH, (h // SW_G + 1) * SW_DH)


def _swa_fwd(proj, bias, sinks, *, name):
    S = proj.shape[0]
    nb = S // BLK

    def body(q_ref, kp_ref, kc_ref, vp_ref, vc_ref, b_ref, s_ref, o_ref):
        has_prev = pl.program_id(0) > 0
        for h in range(SW_H):
            o_ref[:, _qs(h)] = _swa_head(q_ref[:, _qs(h)], kp_ref[:, _kvs(h)], kc_ref[:, _kvs(h)], vp_ref[:, _kvs(h)],
                                         vc_ref[:, _kvs(h)], b_ref[h, :, :BLK], b_ref[h, :, BLK:], s_ref[...], has_prev,
                                         h).astype(BF16)

    def cur(j):
        return pl.BlockSpec((BLK, 128), lambda n, j=j: (n, j))

    def prev(j):
        return pl.BlockSpec((BLK, 128), lambda n, j=j: (jnp.maximum(n - 1, 0), j))

    return pl.pallas_call(
        body, name=name, grid=(nb,),
        in_specs=[pl.BlockSpec((BLK, SW_W), lambda n: (n, 4)), prev(20), cur(20), prev(21), cur(21),
                  pl.BlockSpec((SW_H, BLK, 2 * BLK), lambda n: (0, 0, 0)), pl.BlockSpec((1, 128), lambda n: (0, 0))],
        out_specs=pl.BlockSpec((BLK, SW_W), lambda n: (n, 0)), out_shape=_sds((S, SW_W), BF16),
        compiler_params=_params(("parallel",)))(proj, proj, proj, proj, proj, bias, sinks)


def _swa_bwd(dh, proj, bias, sinks, *, name):
    S = proj.shape[0]
    nb = S // BLK

    def body(dh_ref, q_ref, kp_ref, kc_ref, vp_ref, vc_ref, b_ref, s_ref, dsw_ref, db_ref, ds_ref, ck_s, cv_s, nk_s, nv_s):
        i = pl.program_id(0)

        @pl.when(i == 0)
        def _():
            ck_s[...] = jnp.zeros_like(ck_s)
            cv_s[...] = jnp.zeros_like(cv_s)
            db_ref[...] = jnp.zeros_like(db_ref)
            ds_ref[...] = jnp.zeros_like(ds_ref)

        has_prev = i < nb - 1
        ds = jnp.zeros((1, 128), F32)
        for g in range(SW_H // SW_G):
            dkp = dkc = dvp = dvc = jnp.zeros((BLK, SW_DH), F32)
            for h in range(g * SW_G, (g + 1) * SW_G):
                f = functools.partial(_swa_head, has_prev=has_prev, h=h)
                _, vjp = jax.vjp(f, q_ref[:, _qs(h)], kp_ref[:, _kvs(h)], kc_ref[:, _kvs(h)], vp_ref[:, _kvs(h)],
                                 vc_ref[:, _kvs(h)], b_ref[h, :, :BLK], b_ref[h, :, BLK:], s_ref[...])
                dq, a, b, c, d, dbp, dbc, ds_h = vjp(dh_ref[:, _qs(h)])
                dkp, dkc, dvp, dvc, ds = dkp + a, dkc + b, dvp + c, dvc + d, ds + ds_h
                dsw_ref[:, _qs(h)] = dq.astype(BF16)
                db_ref[h, :, :BLK] += dbp
                db_ref[h, :, BLK:] += dbc
            kv = slice(g * SW_DH, (g + 1) * SW_DH)
            dsw_ref[:, SW_W + g * SW_DH:SW_W + (g + 1) * SW_DH] = (dkc + ck_s[:, kv]).astype(BF16)
            dsw_ref[:, SW_W + SW_KVW + g * SW_DH:SW_W + SW_KVW + (g + 1) * SW_DH] = (dvc + cv_s[:, kv]).astype(BF16)
            nk_s[:, kv] = dkp
            nv_s[:, kv] = dvp
        ck_s[...] = nk_s[...]
        cv_s[...] = nv_s[...]
        ds_ref[...] += ds

    def cur(j):
        return pl.BlockSpec((BLK, 128), lambda i, j=j: (nb - 1 - i, j))

    def prev(j):
        return pl.BlockSpec((BLK, 128), lambda i, j=j: (jnp.maximum(nb - 2 - i, 0), j))

    bsp = pl.BlockSpec((SW_H, BLK, 2 * BLK), lambda i: (0, 0, 0))
    vec = pl.BlockSpec((1, 128), lambda i: (0, 0))
    return pl.pallas_call(
        body, name=name, grid=(nb,),
        in_specs=[pl.BlockSpec((BLK, SW_W), lambda i: (nb - 1 - i, 1)), pl.BlockSpec((BLK, SW_W), lambda i: (nb - 1 - i, 4)),
                  prev(20), cur(20), prev(21), cur(21), bsp, vec],
        out_specs=[pl.BlockSpec((BLK, SW_GW), lambda i: (nb - 1 - i, 0)), bsp, vec],
        out_shape=[_sds((S, SW_GW), BF16), _sds((SW_H, BLK, 2 * BLK), F32), _sds((1, 128), F32)],
        scratch_shapes=[pltpu.VMEM((BLK, 128), F32)] * 4,
        compiler_params=_params(("arbitrary",)))(dh, proj, proj, proj, proj, proj, bias, sinks)


XA_TM = 512


def _xa_head(qh, kh, vh):
    logits = dot_nt(qh, kh) * (XA_DH ** -0.5)
    mx = lax.stop_gradient(jnp.max(logits, axis=1, keepdims=True))
    e = jnp.exp(logits - mx)
    return dot_nn(e / jnp.sum(e, axis=1, keepdims=True), vh)


def _xs(h, off=0):
    return slice(off + h * XA_DH, off + (h + 1) * XA_DH)


def _xattn_fwd(q, kv, *, name):
    S = q.shape[0]
    M = kv.shape[0]

    def body(q_ref, kv_ref, o_ref):
        for h in range(XA_H):
            o_ref[:, _xs(h)] = _xa_head(q_ref[:, _xs(h)], kv_ref[:, _xs(h)], kv_ref[:, _xs(h, D)]).astype(BF16)

    row = pl.BlockSpec((XA_TM, D), lambda i: (i, 0))
    return pl.pallas_call(
        body, name=name, grid=(S // XA_TM,), in_specs=[row, pl.BlockSpec((M, 2 * D), lambda i: (0, 0))], out_specs=row,
        out_shape=_sds((S, D), BF16), compiler_params=_params(("parallel",)))(q, kv)


def _xattn_bwd(do, q, kv, *, name):
    S = q.shape[0]
    M = kv.shape[0]

    def body(do_ref, q_ref, kv_ref, dq_ref, dkv_ref):
        @pl.when(pl.program_id(0) == 0)
        def _():
            dkv_ref[...] = jnp.zeros_like(dkv_ref)

        for h in range(XA_H):
            _, vjp = jax.vjp(_xa_head, q_ref[:, _xs(h)], kv_ref[:, _xs(h)], kv_ref[:, _xs(h, D)])
            dq, dk, dv = vjp(do_ref[:, _xs(h)])
            dq_ref[:, _xs(h)] = dq.astype(BF16)
            dkv_ref[:, _xs(h)] += dk
            dkv_ref[:, _xs(h, D)] += dv

    row = pl.BlockSpec((XA_TM, D), lambda i: (i, 0))
    full = pl.BlockSpec((M, 2 * D), lambda i: (0, 0))
    return pl.pallas_call(
        body, name=name, grid=(S // XA_TM,), in_specs=[row, row, full], out_specs=[row, full],
        out_shape=[_sds((S, D), BF16), _sds((M, 2 * D), F32)], compiler_params=_params(("arbitrary",)))(do, q, kv)


def _loss_head(y, tgt, *, name):
    S = y.shape[0]
    tm = 512

    def body(y_ref, t_ref, l_ref, dy_ref):
        e = y_ref[...] - t_ref[...]
        dy_ref[...] = e * (1.0 / D)
        part = 0.5 * jnp.sum(jnp.sum(e * e, axis=1, keepdims=True) * (1.0 / D), axis=0, keepdims=True)

        @pl.when(pl.program_id(0) == 0)
        def _():
            l_ref[...] = jnp.broadcast_to(part, (8, 128))

        @pl.when(pl.program_id(0) > 0)
        def _():
            l_ref[...] += jnp.broadcast_to(part, (8, 128))

    row = pl.BlockSpec((tm, D), lambda i: (i, 0))
    return pl.pallas_call(
        body, name=name, grid=(S // tm,), in_specs=[row, row], out_specs=[pl.BlockSpec((8, 128), lambda i: (0, 0)), row],
        out_shape=[_sds((8, 128), F32), _sds((S, D), F32)], compiler_params=_params(("arbitrary",)))(y, tgt)


ANY = pl.BlockSpec(memory_space=pl.ANY)


def _place():
    x, y, c = lax.axis_index("x"), lax.axis_index("y"), lax.axis_index("c")
    chips = [(1 - x, y), (x, 1 - y), (1 - x, 1 - y)]
    return x, y, c, chips


def _gather(arrs, *, name):
    n = len(arrs)

    def body(*refs):
        srcs, outs = refs[:n], refs[n:2 * n]
        send_sems, recv_sems, local_sems = refs[2 * n:]
        x, y, c, chips = _place()
        me, sib = (x, y, c), (x, y, 1 - c)

        def idx(p):
            return 4 * p[0] + 2 * p[1] + p[2]

        def copy(i, k, block, to, from_src=False):
            return pltpu.make_async_remote_copy(
                src_ref=srcs[i] if from_src else outs[i].at[idx(block)], dst_ref=outs[i].at[idx(block)],
                send_sem=send_sems.at[7 * i + k], recv_sem=recv_sems.at[7 * i + k], device_id=to, device_id_type=MESH)

        local = [pltpu.make_async_copy(srcs[i], outs[i].at[idx(me)], local_sems.at[i]) for i in range(n)]
        for cp in local:
            cp.start()
        first = []
        for i in range(n):
            first.append(copy(i, 0, me, sib, True))
            first += [copy(i, 1 + j, me, (*chip, c), True) for j, chip in enumerate(chips)]
        for cp in first:
            cp.start()
        passed = []
        for j, chip in enumerate(chips):
            for i in range(n):
                copy(i, 1 + j, (*chip, c), me).wait_recv()
                cp = copy(i, 4 + j, (*chip, c), sib)
                cp.start()
                passed.append(cp)
        for i in range(n):
            copy(i, 0, sib, me).wait_recv()
        for j, chip in enumerate(chips):
            for i in range(n):
                copy(i, 4 + j, (*chip, 1 - c), me).wait_recv()
        for cp in first + passed:
            cp.wait_send()
        for cp in local:
            cp.wait()

    return pl.pallas_call(
        body, name=name, in_specs=[ANY] * n, out_specs=[ANY] * n,
        out_shape=[_sds((N_DEV,) + a.shape, a.dtype) for a in arrs],
        scratch_shapes=[pltpu.SemaphoreType.DMA((7 * n,)), pltpu.SemaphoreType.DMA((7 * n,)), pltpu.SemaphoreType.DMA((n,))],
        compiler_params=pltpu.CompilerParams(has_side_effects=True))(*arrs)


def _swap_sibling(arrs, *, name):
    n = len(arrs)

    def body(*refs):
        srcs, outs = refs[:n], refs[n:2 * n]
        send_sems, recv_sems = refs[2 * n:]
        x, y, c, _ = _place()
        copies = [pltpu.make_async_remote_copy(
            src_ref=srcs[i].at[2 * j + (1 - c)], dst_ref=outs[i].at[j], send_sem=send_sems.at[N_CHIP * i + j],
            recv_sem=recv_sems.at[N_CHIP * i + j], device_id=(x, y, 1 - c), device_id_type=MESH)
            for i in range(n) for j in range(N_CHIP)]
        for cp in copies:
            cp.start()
        for cp in copies:
            cp.wait()

    return pl.pallas_call(
        body, name=name, in_specs=[ANY] * n, out_specs=[ANY] * n,
        out_shape=[_sds((N_CHIP,) + a.shape[1:], a.dtype) for a in arrs],
        scratch_shapes=[pltpu.SemaphoreType.DMA((N_CHIP * n,)), pltpu.SemaphoreType.DMA((N_CHIP * n,))],
        compiler_params=pltpu.CompilerParams(has_side_effects=True))(*arrs)


def _pair_sum(part, got, c_idx, *, name):
    _, R, C = part.shape
    tr = next(t for t in (512, 256, 128, 64, 32, 16) if R % t == 0)

    def body(c_ref, p_ref, g_ref, o_ref):
        o_ref[...] = (p_ref[0].astype(F32) + g_ref[...].astype(F32)).astype(o_ref.dtype)

    return pl.pallas_call(
        body, name=name,
        grid_spec=pltpu.PrefetchScalarGridSpec(
            num_scalar_prefetch=1, grid=(N_CHIP, R // tr),
            in_specs=[pl.BlockSpec((1, 1, tr, C), lambda j, r, c_ref: (j, c_ref[0], r, 0)),
                      pl.BlockSpec((1, tr, C), lambda j, r, c_ref: (j, r, 0))],
            out_specs=pl.BlockSpec((1, tr, C), lambda j, r, c_ref: (j, r, 0))),
        out_shape=_sds((N_CHIP, R, C), part.dtype),
        compiler_params=_params(("parallel", "parallel")))(c_idx, part.reshape(N_CHIP, 2, R, C), got)


def _swap_chips(arrs, *, name):
    n = len(arrs)

    def body(*refs):
        srcs, outs = refs[:n], refs[n:2 * n]
        send_sems, recv_sems, local_sems = refs[2 * n:]
        x, y, c, chips = _place()
        jme = 2 * x + y
        local = [pltpu.make_async_copy(srcs[i].at[jme], outs[i].at[jme], local_sems.at[i]) for i in range(n)]
        for cp in local:
            cp.start()
        copies = [pltpu.make_async_remote_copy(
            src_ref=srcs[i].at[2 * chip[0] + chip[1]], dst_ref=outs[i].at[jme], send_sem=send_sems.at[3 * i + j],
            recv_sem=recv_sems.at[3 * i + j], device_id=(*chip, c), device_id_type=MESH)
            for i in range(n) for j, chip in enumerate(chips)]
        for cp in copies:
            cp.start()
        for i in range(n):
            for j, chip in enumerate(chips):
                pltpu.make_async_remote_copy(
                    src_ref=srcs[i].at[jme], dst_ref=outs[i].at[2 * chip[0] + chip[1]], send_sem=send_sems.at[3 * i + j],
                    recv_sem=recv_sems.at[3 * i + j], device_id=(*chip, c), device_id_type=MESH).wait_recv()
        for cp in copies:
            cp.wait_send()
        for cp in local:
            cp.wait()

    return pl.pallas_call(
        body, name=name, in_specs=[ANY] * n, out_specs=[ANY] * n,
        out_shape=[_sds(a.shape, a.dtype) for a in arrs],
        scratch_shapes=[pltpu.SemaphoreType.DMA((3 * n,)), pltpu.SemaphoreType.DMA((3 * n,)), pltpu.SemaphoreType.DMA((n,))],
        compiler_params=pltpu.CompilerParams(has_side_effects=True))(*arrs)


def _adamw(parts, w, m, v, *, name):
    P, R, C = parts.shape
    tr = next((t for t in (256, 128, 64, 32, 16, 8) if R % t == 0), R)
    c1 = 1.0 / (1.0 - ADAM_B1 ** ADAM_STEP)
    c2 = 1.0 / (1.0 - ADAM_B2 ** ADAM_STEP)

    def body(p_ref, w_ref, m_ref, v_ref, g_ref, d_ref, nm_ref, nv_ref):
        g = p_ref[0].astype(F32)
        for j in range(1, P):
            g = g + p_ref[j].astype(F32)
        nm = ADAM_B1 * m_ref[...] + (1.0 - ADAM_B1) * g
        nv = ADAM_B2 * v_ref[...] + (1.0 - ADAM_B2) * (g * g)
        g_ref[...] = g
        nm_ref[...] = nm
        nv_ref[...] = nv
        d_ref[...] = -ADAM_LR * ((nm * c1) / (jnp.sqrt(nv * c2) + ADAM_EPS) + ADAM_WD * w_ref[...])

    row = pl.BlockSpec((tr, C), lambda i: (i, 0))
    out = _sds((R, C), F32)
    return pl.pallas_call(
        body, name=name, grid=(R // tr,), in_specs=[pl.BlockSpec((P, tr, C), lambda i: (0, i, 0)), row, row, row],
        out_specs=[row, row, row, row], out_shape=[out, out, out, out], compiler_params=_params(("parallel",)))(parts, w, m, v)


BIG = ("w_in", "w_out", "xa_wq", "xa_wkv", "xa_wo", "ffn_w_up", "ffn_w_down")
COL_SHARDED = ("w_in", "xa_wkv", "ffn_w_up", "ml_conv_w", "ffn_conv_w")
SHARDED_SMALL = ("ml_conv_w", "ffn_conv_w")
REPLICATED = ("rel_bias", "ml_conv_b", "ml_i_bias", "ml_f_bias", "ml_norm_g", "swa_sinks", "ln1_g", "ln1_b", "ln2_g", "ln2_b",
              "ffn_conv_b", "ln3_g", "ln3_b")
NAMES = ("rel_bias", "w_in", "ml_conv_w", "ml_conv_b", "ml_i_bias", "ml_f_bias", "ml_norm_g", "swa_sinks", "w_out", "ln1_g", "ln1_b",
         "xa_wq", "xa_wkv", "xa_wo", "ln2_g", "ln2_b", "ffn_w_up", "ffn_conv_w", "ffn_conv_b", "ffn_w_down", "ln3_g", "ln3_b")


def _flat_rows(a, mult):
    f = a.reshape(-1)
    n = -(-f.shape[0] // (128 * mult)) * (128 * mult)
    if n != f.shape[0]:
        f = jnp.pad(f, (0, n - f.shape[0]))
    return f.reshape(-1, 128)


def _pack(arrs, mult):
    parts = [_flat_rows(a, mult) for a in arrs]
    return jnp.concatenate(parts, axis=0), [p.shape[0] for p in parts]


def _unpack(flat, rows, shapes):
    out, off = [], 0
    lead = flat.shape[:-2]
    for r, shp in zip(rows, shapes):
        n = int(np.prod(shp))
        piece = flat[..., off:off + r, :].reshape(lead + (r * 128,))[..., :n]
        out.append(piece.reshape(lead + tuple(shp)))
        off += r
    return out


def _full_from_shards(stacked, name):
    if name in COL_SHARDED:
        return jnp.moveaxis(stacked, 0, 2).reshape(stacked.shape[1], stacked.shape[2], N_DEV * stacked.shape[3])
    return jnp.moveaxis(stacked, 0, 1).reshape(stacked.shape[1], N_DEV * stacked.shape[2], stacked.shape[3])


def _shards_from_full(full, name):
    L, A, B = full.shape
    if name in COL_SHARDED:
        return jnp.moveaxis(full.reshape(L, A, N_DEV, B // N_DEV), 2, 0)
    return jnp.moveaxis(full.reshape(L, N_DEV, A // N_DEV, B), 1, 0)


def _pad_win(w):
    z = jnp.zeros(w.shape[:-1] + (NP_IN - N_IN,), w.dtype)
    return jnp.concatenate([w[..., :2048], w[..., 2056:], w[..., 2048:2056], z], axis=-1)


def _interleave(a):
    s = a.shape[:-1]
    return jnp.swapaxes(a.reshape(s + (2, NB_FF, 128)), -3, -2).reshape(s + (2 * DFF,))


def _deinterleave(a):
    s = a.shape[:-1]
    return jnp.swapaxes(a.reshape(s + (NB_FF, 2, 128)), -3, -2).reshape(s + (2 * DFF,))


def _row128(v):
    return jnp.pad(v, (0, 128 - v.shape[0])).reshape(1, 128)


def _gather_weights(W):
    sm_flat, sm_rows = _pack([W[n] for n in SHARDED_SMALL], 8)
    got = _gather([W[n].astype(BF16) for n in BIG] + [sm_flat], name="gather_weights")
    full = {n: _full_from_shards(s, n) for n, s in zip(BIG, got[:-1])}
    full.update({n: _full_from_shards(s, n)
                 for n, s in zip(SHARDED_SMALL, _unpack(got[-1], sm_rows, [W[n].shape for n in SHARDED_SMALL]))})
    full["w_in"] = _pad_win(full["w_in"])
    full["ffn_w_up"] = _interleave(full["ffn_w_up"])
    full["ffn_conv_w"] = _interleave(full["ffn_conv_w"])
    return full


def _reduce_and_update(Gf, loss_part, W, Mo, Vo, me, c_idx):
    send = [_shards_from_full(Gf[n], n).astype(BF16) for n in BIG]
    send = [s.reshape(N_DEV, -1, s.shape[-1]) for s in send]
    got = _swap_sibling(send, name="swap_sibling")
    sums = [_pair_sum(s, g, c_idx, name=f"pair_sum_{n}") for n, s, g in zip(BIG, send, got)]
    parts = _swap_chips(sums, name="swap_chips")
    res = {}
    for n, p in zip(BIG, parts):
        shp = W[n].shape
        flat = (shp[0] * shp[1], shp[2])
        outs = _adamw(p, W[n].reshape(flat), Mo[n].reshape(flat), Vo[n].reshape(flat), name=f"adamw_{n}")
        for kind, a in zip(("g", "d", "m", "v"), outs):
            res[kind, n] = a.reshape(shp)

    small = REPLICATED + SHARDED_SMALL
    sp_flat, sp_rows = _pack([Gf[n] for n in small] + [loss_part], 8)
    sp_all = _gather([sp_flat], name="gather_small_grads")[0]

    def widen(n, t):
        if n not in SHARDED_SMALL:
            return t[n]
        return lax.dynamic_update_slice(jnp.zeros(Gf[n].shape, F32), t[n], (0, 0, me * t[n].shape[2]))

    zl = jnp.zeros((8, 128), F32)
    wsm, _ = _pack([widen(n, W) for n in small] + [zl], 8)
    msm, _ = _pack([widen(n, Mo) for n in small] + [zl], 8)
    vsm, _ = _pack([widen(n, Vo) for n in small] + [zl], 8)
    outs_small = [_unpack(o_, sp_rows, [Gf[n].shape for n in small] + [(8, 128)])
                  for o_ in _adamw(sp_all, wsm, msm, vsm, name="adamw_small")]
    for kind, os_ in zip(("g", "d", "m", "v"), outs_small):
        for n, a in zip(small, os_[:-1]):
            if n in SHARDED_SMALL:
                a = lax.dynamic_slice(a, (0, 0, me * W[n].shape[2]), W[n].shape)
            res[kind, n] = a
    return res, outs_small[0][-1][0, 0]


def kernel(x, mem, rel_bias, w_in, ml_conv_w, ml_conv_b, ml_i_bias, ml_f_bias, ml_norm_g, swa_sinks, w_out, ln1_g, ln1_b, xa_wq, xa_wkv, xa_wo, ln2_g, ln2_b, ffn_w_up, ffn_conv_w, ffn_conv_b, ffn_w_down, ln3_g, ln3_b, loss_target, m_rel_bias, m_w_in, m_ml_conv_w, m_ml_conv_b, m_ml_i_bias, m_ml_f_bias, m_ml_norm_g, m_swa_sinks, m_w_out, m_ln1_g, m_ln1_b, m_xa_wq, m_xa_wkv, m_xa_wo, m_ln2_g, m_ln2_b, m_ffn_w_up, m_ffn_conv_w, m_ffn_conv_b, m_ffn_w_down, m_ln3_g, m_ln3_b, v_rel_bias, v_w_in, v_ml_conv_w, v_ml_conv_b, v_ml_i_bias, v_ml_f_bias, v_ml_norm_g, v_swa_sinks, v_w_out, v_ln1_g, v_ln1_b, v_xa_wq, v_xa_wkv, v_xa_wo, v_ln2_g, v_ln2_b, v_ffn_w_up, v_ffn_conv_w, v_ffn_conv_b, v_ffn_w_down, v_ln3_g, v_ln3_b):
    W = dict(rel_bias=rel_bias, w_in=w_in, ml_conv_w=ml_conv_w, ml_conv_b=ml_conv_b, ml_i_bias=ml_i_bias, ml_f_bias=ml_f_bias,
             ml_norm_g=ml_norm_g, swa_sinks=swa_sinks, w_out=w_out, ln1_g=ln1_g, ln1_b=ln1_b, xa_wq=xa_wq, xa_wkv=xa_wkv,
             xa_wo=xa_wo, ln2_g=ln2_g, ln2_b=ln2_b, ffn_w_up=ffn_w_up, ffn_conv_w=ffn_conv_w, ffn_conv_b=ffn_conv_b,
             ffn_w_down=ffn_w_down, ln3_g=ln3_g, ln3_b=ln3_b)
    Mo = dict(rel_bias=m_rel_bias, w_in=m_w_in, ml_conv_w=m_ml_conv_w, ml_conv_b=m_ml_conv_b, ml_i_bias=m_ml_i_bias,
              ml_f_bias=m_ml_f_bias, ml_norm_g=m_ml_norm_g, swa_sinks=m_swa_sinks, w_out=m_w_out, ln1_g=m_ln1_g, ln1_b=m_ln1_b,
              xa_wq=m_xa_wq, xa_wkv=m_xa_wkv, xa_wo=m_xa_wo, ln2_g=m_ln2_g, ln2_b=m_ln2_b, ffn_w_up=m_ffn_w_up,
              ffn_conv_w=m_ffn_conv_w, ffn_conv_b=m_ffn_conv_b, ffn_w_down=m_ffn_w_down, ln3_g=m_ln3_g, ln3_b=m_ln3_b)
    Vo = dict(rel_bias=v_rel_bias, w_in=v_w_in, ml_conv_w=v_ml_conv_w, ml_conv_b=v_ml_conv_b, ml_i_bias=v_ml_i_bias,
              ml_f_bias=v_ml_f_bias, ml_norm_g=v_ml_norm_g, swa_sinks=v_swa_sinks, w_out=v_w_out, ln1_g=v_ln1_g, ln1_b=v_ln1_b,
              xa_wq=v_xa_wq, xa_wkv=v_xa_wkv, xa_wo=v_xa_wo, ln2_g=v_ln2_g, ln2_b=v_ln2_b, ffn_w_up=v_ffn_w_up,
              ffn_conv_w=v_ffn_conv_w, ffn_conv_b=v_ffn_conv_b, ffn_w_down=v_ffn_w_down, ln3_g=v_ln3_g, ln3_b=v_ln3_b)
    S = x.shape[1]
    c_me = lax.axis_index("c")
    me = 4 * lax.axis_index("x") + 2 * lax.axis_index("y") + c_me
    c_idx = jnp.reshape(c_me, (1,)).astype(jnp.int32)
    xs = x.reshape(S, D)
    mems = mem.reshape(mem.shape[1], D)
    tgt = loss_target.reshape(S, D)

    full = _gather_weights(W)
    win = full["w_in"]
    win_ml = jnp.concatenate([win[..., 1024:2048], win[..., 2816:2944]], axis=-1)
    win_sw = win[..., 2048:2816]
    win_qk = win[..., :1024]
    ffn_cb = _interleave(ffn_conv_b)

    bucket = jnp.asarray(_t5_buckets())
    bias = _bias_table(rel_bias, bucket, name="bias_table")

    saved = []
    h0 = xs
    for l in range(DEPTH):
        gbias = _row128(jnp.concatenate([ml_i_bias[l], ml_f_bias[l]]))
        sinks = _row128(swa_sinks[l])
        ng = ml_norm_g[l].reshape(1, ML_W)
        proj = _mm(h0, win[l], name=f"proj{l}")
        qk = _silu_conv_fwd(proj, full["ml_conv_w"][l], ml_conv_b[l].reshape(1, -1), name=f"mlconv{l}")
        h_ml, cs, ns, ms = _mlstm_fwd(qk, proj, gbias, ng, name=f"mlstm{l}")
        h_sw = _swa_fwd(proj, bias, sinks, name=f"swa{l}")
        h1, z1 = _mm_res_ln([h_ml, h_sw], full["w_out"][l], h0, ln1_g[l].reshape(1, D), ln1_b[l].reshape(1, D), name=f"mix_out{l}")
        q = _mm(h1, full["xa_wq"][l], name=f"xa_q{l}")
        kv = _mm(mems, full["xa_wkv"][l], tm=256, name=f"xa_kv{l}")
        o = _xattn_fwd(q, kv, name=f"xattn{l}")
        h2, z2 = _mm_res_ln([o], full["xa_wo"][l], h1, ln2_g[l].reshape(1, D), ln2_b[l].reshape(1, D), name=f"xa_out{l}")
        up = _mm(h2, full["ffn_w_up"][l], name=f"ffn_up{l}")
        act = _ffn_gate_fwd(up, full["ffn_conv_w"][l], ffn_cb[l].reshape(1, -1), name=f"ffn_gate{l}")
        h3, z3 = _mm_res_ln([act], full["ffn_w_down"][l], h2, ln3_g[l].reshape(1, D), ln3_b[l].reshape(1, D), name=f"ffn_out{l}")
        saved.append(dict(h0=h0, proj=proj, qk=qk, cs=cs, ns=ns, ms=ms, h_ml=h_ml, h_sw=h_sw, z1=z1, h1=h1, q=q, kv=kv, o=o, z2=z2,
                          h2=h2, up=up, act=act, z3=z3, gbias=gbias, sinks=sinks, ng=ng))
        h0 = h3

    loss_part, dh = _loss_head(h0, tgt, name="loss_head")

    G = {n: [None] * DEPTH for n in NAMES if n != "rel_bias"}
    dbias = [None] * DEPTH
    for l in reversed(range(DEPTH)):
        sv = saved[l]
        dz3, G["ln3_g"][l], G["ln3_b"][l] = _ln_bwd(dh, sv["z3"], ln3_g[l].reshape(1, D), name=f"ln3_bwd{l}")
        G["ffn_w_down"][l] = _mm_tn(sv["act"], dz3, name=f"d_w_down{l}")
        dact = _mm(dz3, full["ffn_w_down"][l], trans_b=True, name=f"d_act{l}")
        dup, dcw, dcb = _ffn_gate_bwd(dact, sv["up"], full["ffn_conv_w"][l], ffn_cb[l].reshape(1, -1), name=f"ffn_gate_bwd{l}")
        G["ffn_conv_w"][l] = _deinterleave(dcw)
        G["ffn_conv_b"][l] = _deinterleave(dcb)
        G["ffn_w_up"][l] = _deinterleave(_mm_tn(sv["h2"], dup, name=f"d_w_up{l}"))
        dh2 = _mm(dup, full["ffn_w_up"][l], trans_b=True, add=dz3, add_scale=ALPHA, name=f"d_h2_{l}")

        dz2, G["ln2_g"][l], G["ln2_b"][l] = _ln_bwd(dh2, sv["z2"], ln2_g[l].reshape(1, D), name=f"ln2_bwd{l}")
        G["xa_wo"][l] = _mm_tn(sv["o"], dz2, name=f"d_xa_wo{l}")
        do = _mm(dz2, full["xa_wo"][l], trans_b=True, name=f"d_xa_o{l}")
        dq, dkv = _xattn_bwd(do, sv["q"], sv["kv"], name=f"xattn_bwd{l}")
        G["xa_wkv"][l] = _mm_tn(mems, dkv, name=f"d_xa_wkv{l}")
        G["xa_wq"][l] = _mm_tn(sv["h1"], dq, name=f"d_xa_wq{l}")
        dh1 = _mm(dq, full["xa_wq"][l], trans_b=True, add=dz2, add_scale=ALPHA, name=f"d_h1_{l}")

        dz1, G["ln1_g"][l], G["ln1_b"][l] = _ln_bwd(dh1, sv["z1"], ln1_g[l].reshape(1, D), name=f"ln1_bwd{l}")
        G["w_out"][l] = jnp.concatenate([_mm_tn(sv["h_ml"], dz1, name=f"d_w_out_ml{l}"),
                                         _mm_tn(sv["h_sw"], dz1, name=f"d_w_out_sw{l}")], axis=0)
        dhcat = _mm(dz1, full["w_out"][l], trans_b=True, name=f"d_hcat{l}")
        dsw, dbias[l], dsinks = _swa_bwd(dhcat, sv["proj"], bias, sv["sinks"], name=f"swa_bwd{l}")
        dqk, dml, dgb, dng = _mlstm_bwd(dhcat, sv["qk"], sv["proj"], sv["gbias"], sv["ng"], sv["cs"], sv["ns"], sv["ms"],
                                        name=f"mlstm_bwd{l}")
        dqk_pre, G["ml_conv_w"][l], G["ml_conv_b"][l] = _silu_conv_bwd(
            dqk, sv["proj"], full["ml_conv_w"][l], ml_conv_b[l].reshape(1, -1), name=f"mlconv_bwd{l}")
        dw_qk = _mm_tn(sv["h0"], dqk_pre, name=f"d_w_in_qk{l}")
        dw_ml = _mm_tn(sv["h0"], dml, name=f"d_w_in_ml{l}")
        dw_sw = _mm_tn(sv["h0"], dsw, name=f"d_w_in_sw{l}")
        G["w_in"][l] = jnp.concatenate([dw_qk, dw_ml[:, :2 * ML_W + 2 * ML_H], dw_sw], axis=1)
        dh = _mm(dqk_pre, win_qk[l], trans_b=True, add=dz1, add_scale=ALPHA, name=f"d_h0_qk{l}")
        dh = _mm(dml, win_ml[l], trans_b=True, add=dh, name=f"d_h0_ml{l}")
        dh = _mm(dsw, win_sw[l], trans_b=True, add=dh, name=f"d_h0_sw{l}")
        G["ml_i_bias"][l] = dgb[0, :ML_H]
        G["ml_f_bias"][l] = dgb[0, ML_H:2 * ML_H]
        G["ml_norm_g"][l] = dng
        G["swa_sinks"][l] = dsinks[0, :SW_H]
    grad_x = dh.reshape(x.shape)

    Gf = {n: jnp.stack([g.reshape(W[n].shape[1:]) if n in REPLICATED else g for g in G[n]]) for n in G}
    Gf["rel_bias"] = _bias_table_bwd(dbias, bucket, name="bias_table_bwd")[:, :SW_H]

    res, loss = _reduce_and_update(Gf, loss_part, W, Mo, Vo, me, c_idx)
    return (loss, grad_x, *[res["g", n] for n in NAMES], *[res["d", n] for n in NAMES], *[res["m", n] for n in NAMES],
            *[res["v", n] for n in NAMES])
```

```python
import functools
import math

import jax
import jax.numpy as jnp
import numpy as np
from jax import lax
from jax.experimental import pallas as pl
from jax.experimental.pallas import tpu as pltpu

F32 = jnp.float32
BF16 = jnp.bfloat16

N_DEV = 8
N_CHIP = 4
D = 1024
DEPTH = 2
ML_H = 4
ML_W = 512
ML_DH = 128
ML_L = 64
ML_CONV = 4
SW_DH = 64
SW_W = 512
SW_H = 8
SW_G = 4
SW_KVW = 128
BLK = 128
REL_B = 32
REL_MAXD = 128
XA_H = 4
XA_DH = 256
DFF = 2816
NB_FF = DFF // 128
FFN_CONV = 3
ALPHA = (2.0 * DEPTH) ** 0.25
EPS = 1e-5
N_IN = 2824
NP_IN = 3072
ML_GW = 2 * ML_W + 128
SW_GW = SW_W + 2 * SW_KVW
ADAM_LR = 0.001
ADAM_B1 = 0.9
ADAM_B2 = 0.999
ADAM_EPS = 1e-08
ADAM_WD = 0.01
ADAM_STEP = 10
VMEM_LIMIT = 56 * 1024 * 1024
MESH = pl.DeviceIdType.MESH

NN = ((1,), (0,))
NT = ((1,), (1,))
TN = ((0,), (0,))


def _dg(a, b, dn):
    return lax.dot_general(a.astype(BF16), b.astype(BF16), (dn, ((), ())), preferred_element_type=F32)


@jax.custom_vjp
def dot_nn(a, b):
    return _dg(a, b, NN)


dot_nn.defvjp(lambda a, b: (_dg(a, b, NN), (a, b)), lambda r, g: (_dg(g, r[1], NT), _dg(r[0], g, TN)))


@jax.custom_vjp
def dot_nt(a, b):
    return _dg(a, b, NT)


dot_nt.defvjp(lambda a, b: (_dg(a, b, NT), (a, b)), lambda r, g: (_dg(g, r[1], NN), _dg(g, r[0], TN)))


@jax.custom_vjp
def dot_tn(a, b):
    return _dg(a, b, TN)


dot_tn.defvjp(lambda a, b: (_dg(a, b, TN), (a, b)), lambda r, g: (_dg(r[1], g, NT), _dg(r[0], g, NN)))


def _params(sem=None):
    return pltpu.CompilerParams(dimension_semantics=sem, vmem_limit_bytes=VMEM_LIMIT)


def _sds(shape, dtype):
    return jax.ShapeDtypeStruct(tuple(shape), dtype)


def _mm(a, b, *, trans_b=False, out_dtype=F32, add=None, add_scale=1.0, tm=1024, tn=512, name):
    a_list = list(a) if isinstance(a, (list, tuple)) else [a]
    M = a_list[0].shape[0]
    N = b.shape[0] if trans_b else b.shape[1]
    tm = min(tm, M)
    tn = next(t for t in (tn, 384, 256, 128) if N % t == 0)
    assert M % tm == 0
    Ka = a_list[0].shape[1]
    assert all(t.shape[1] == Ka for t in a_list)
    tk = next(t for t in (Ka, 1408, 1024) if Ka % t == 0 and t <= 1408)
    na, npa = len(a_list), Ka // tk
    nk = na * npa
    has_add = add is not None

    def body(*refs):
        a_refs, b_ref = refs[:na], refs[na]
        add_ref = refs[na + 1] if has_add else None
        o_ref, acc_ref = refs[na + 1 + has_add], refs[na + 2 + has_add]
        k = pl.program_id(2)

        def finish(r):
            if has_add:
                r = r + add_scale * add_ref[...].astype(F32)
            o_ref[...] = r.astype(out_dtype)

        for t, a_ref in enumerate(a_refs):
            def step(a_ref=a_ref):
                p = _dg(a_ref[...], b_ref[...], NT if trans_b else NN)
                if nk == 1:
                    finish(p)
                    return

                @pl.when(k == 0)
                def _():
                    acc_ref[...] = p

                @pl.when((k > 0) & (k < nk - 1))
                def _():
                    acc_ref[...] += p

                @pl.when(k == nk - 1)
                def _():
                    finish(acc_ref[...] + p)

            if na == 1:
                step()
            else:
                pl.when((k >= t * npa) & (k < (t + 1) * npa))(step)

    in_specs = [pl.BlockSpec((tm, tk), lambda i, j, k, t=t: (i, jnp.clip(k - t * npa, 0, npa - 1))) for t in range(na)]
    in_specs.append(pl.BlockSpec((tn, tk), lambda i, j, k: (j, k)) if trans_b else pl.BlockSpec((tk, tn), lambda i, j, k: (k, j)))
    args = a_list + [b]
    if has_add:
        in_specs.append(pl.BlockSpec((tm, tn), lambda i, j, k: (i, j)))
        args.append(add)
    return pl.pallas_call(
        body, name=name, grid=(M // tm, N // tn, nk), in_specs=in_specs,
        out_specs=pl.BlockSpec((tm, tn), lambda i, j, k: (i, j)), out_shape=_sds((M, N), out_dtype),
        scratch_shapes=[pltpu.VMEM((tm, tn) if nk > 1 else (8, 128), F32)],
        compiler_params=_params(("parallel", "parallel", "arbitrary")))(*args)


def _mm_tn(a, g, *, name):
    S, K = a.shape
    N = g.shape[1]
    tk = K if K <= 1024 else K // 2
    tn = next(t for t in (512, 384, 256, 128) if N % t == 0)
    ts = min(512, S)
    ns = S // ts
    assert K % tk == 0 and S % ts == 0

    def body(a_ref, g_ref, o_ref):
        s = pl.program_id(2)
        p = _dg(a_ref[...], g_ref[...], TN)

        @pl.when(s == 0)
        def _():
            o_ref[...] = p

        @pl.when(s > 0)
        def _():
            o_ref[...] += p

    return pl.pallas_call(
        body, name=name, grid=(K // tk, N // tn, ns),
        in_specs=[pl.BlockSpec((ts, tk), lambda i, j, s: (s, i)), pl.BlockSpec((ts, tn), lambda i, j, s: (s, j))],
        out_specs=pl.BlockSpec((tk, tn), lambda i, j, s: (i, j)), out_shape=_sds((K, N), F32),
        compiler_params=_params(("parallel", "parallel", "arbitrary")))(a, g)


def _mm_res_ln(a_list, w, resid, gam, bet, *, name):
    M, Ka = a_list[0].shape
    na = len(a_list)
    assert w.shape[0] == na * Ka
    tm = 256

    def body(*refs):
        a_refs, w_refs = refs[:na], refs[na:2 * na]
        r_ref, g_ref, b_ref, y_ref, yb_ref, z_ref = refs[2 * na:]
        z = ALPHA * r_ref[...]
        for a_ref, w_ref in zip(a_refs, w_refs):
            z = z + _dg(a_ref[...], w_ref[...], NN)
        mu = jnp.mean(z, axis=1, keepdims=True)
        zc = z - mu
        var = jnp.mean(zc * zc, axis=1, keepdims=True)
        y = zc * lax.rsqrt(var + EPS) * g_ref[...] + b_ref[...]
        y_ref[...] = y
        yb_ref[...] = y.astype(BF16)
        z_ref[...] = z

    row = pl.BlockSpec((tm, D), lambda i: (i, 0))
    vec = pl.BlockSpec((1, D), lambda i: (0, 0))
    a_specs = [pl.BlockSpec((tm, Ka), lambda i: (i, 0)) for _ in a_list]
    w_specs = [pl.BlockSpec((Ka, D), lambda i, t=t: (t, 0)) for t in range(na)]
    return pl.pallas_call(
        body, name=name, grid=(M // tm,), in_specs=a_specs + w_specs + [row, vec, vec],
        out_specs=[row, row, row], out_shape=[_sds((M, D), F32), _sds((M, D), BF16), _sds((M, D), F32)],
        compiler_params=_params(("parallel",)))(*a_list, *([w] * na), resid, gam, bet)


def _ln_bwd(dy, z, gam, *, name):
    M = dy.shape[0]
    tm = 512

    def body(dy_ref, z_ref, g_ref, dz_ref, dzb_ref, dg_ref, db_ref):
        i = pl.program_id(0)
        z = z_ref[...]
        dy_ = dy_ref[...]
        mu = jnp.mean(z, axis=1, keepdims=True)
        zc = z - mu
        var = jnp.mean(zc * zc, axis=1, keepdims=True)
        rstd = lax.rsqrt(var + EPS)
        xh = zc * rstd
        dxh = dy_ * g_ref[...]
        m1 = jnp.mean(dxh, axis=1, keepdims=True)
        m2 = jnp.mean(dxh * xh, axis=1, keepdims=True)
        dz = rstd * (dxh - m1 - xh * m2)
        dz_ref[...] = dz
        dzb_ref[...] = dz.astype(BF16)
        pg = jnp.sum(dy_ * xh, axis=0, keepdims=True)
        pb = jnp.sum(dy_, axis=0, keepdims=True)

        @pl.when(i == 0)
        def _():
            dg_ref[...] = pg
            db_ref[...] = pb

        @pl.when(i > 0)
        def _():
            dg_ref[...] += pg
            db_ref[...] += pb

    row = pl.BlockSpec((tm, D), lambda i: (i, 0))
    vec = pl.BlockSpec((1, D), lambda i: (0, 0))
    return pl.pallas_call(
        body, name=name, grid=(M // tm,), in_specs=[row, row, vec], out_specs=[row, row, vec, vec],
        out_shape=[_sds((M, D), F32), _sds((M, D), BF16), _sds((1, D), F32), _sds((1, D), F32)],
        compiler_params=_params(("arbitrary",)))(dy, z, gam)


def _shift_down(x, d):
    if d == 0:
        return x
    rows = lax.broadcasted_iota(jnp.int32, x.shape, 0)
    return jnp.where(rows >= d, pltpu.roll(x, d, 0), 0.0)


def _shift_up(x, d):
    if d == 0:
        return x
    S = x.shape[0]
    rows = lax.broadcasted_iota(jnp.int32, x.shape, 0)
    return jnp.where(rows < S - d, pltpu.roll(x, S - d, 0), 0.0)


def _conv(x, w_ref, b_ref, cs, K):
    y = b_ref[:, cs]
    for j in range(K):
        y = y + _shift_down(x, K - 1 - j) * w_ref[j:j + 1, cs]
    return y


def _conv_bwd(dy, x, w_ref, dw_ref, db_ref, cs, K):
    dx = jnp.zeros_like(x)
    for j in range(K):
        dx = dx + _shift_up(dy, K - 1 - j) * w_ref[j:j + 1, cs]
        dw_ref[j:j + 1, cs] = jnp.sum(dy * _shift_down(x, K - 1 - j), axis=0, keepdims=True)
    db_ref[:, cs] = jnp.sum(dy, axis=0, keepdims=True)
    return dx


ALL = slice(None)


def _silu_conv_fwd(proj, cw, cb, *, name):
    S = proj.shape[0]

    def body(x_ref, w_ref, b_ref, o_ref):
        o_ref[...] = jax.nn.silu(_conv(x_ref[...], w_ref, b_ref, ALL, ML_CONV))

    col = pl.BlockSpec((S, 128), lambda j: (0, j))
    return pl.pallas_call(
        body, name=name, grid=(8,),
        in_specs=[col, pl.BlockSpec((ML_CONV, 128), lambda j: (0, j)), pl.BlockSpec((1, 128), lambda j: (0, j))],
        out_specs=col, out_shape=_sds((S, 2 * ML_W), F32), compiler_params=_params(("parallel",)))(proj, cw, cb)


def _silu_conv_bwd(dqk, proj, cw, cb, *, name):
    S = proj.shape[0]

    def body(d_ref, x_ref, w_ref, b_ref, dx_ref, dw_ref, db_ref):
        x = x_ref[...]
        y = _conv(x, w_ref, b_ref, ALL, ML_CONV)
        dy = jax.vjp(jax.nn.silu, y)[1](d_ref[...])[0]
        dx_ref[...] = _conv_bwd(dy, x, w_ref, dw_ref, db_ref, ALL, ML_CONV).astype(BF16)

    col = pl.BlockSpec((S, 128), lambda j: (0, j))
    wsp = pl.BlockSpec((ML_CONV, 128), lambda j: (0, j))
    bsp = pl.BlockSpec((1, 128), lambda j: (0, j))
    return pl.pallas_call(
        body, name=name, grid=(8,), in_specs=[col, col, wsp, bsp], out_specs=[col, wsp, bsp],
        out_shape=[_sds((S, 2 * ML_W), BF16), _sds((ML_CONV, 2 * ML_W), F32), _sds((1, 2 * ML_W), F32)],
        compiler_params=_params(("parallel",)))(dqk, proj, cw, cb)


def _gate(ug, uv):
    return jax.nn.gelu(ug) * uv


def _ffn_specs(S):
    return (pl.BlockSpec((S, 128), lambda j: (0, j)), pl.BlockSpec((S, 128), lambda j: (0, j + NB_FF)),
            pl.BlockSpec((FFN_CONV, 128), lambda j: (0, j)), pl.BlockSpec((FFN_CONV, 128), lambda j: (0, j + NB_FF)),
            pl.BlockSpec((1, 128), lambda j: (0, j)), pl.BlockSpec((1, 128), lambda j: (0, j + NB_FF)))


def _ffn_gate_fwd(up, cw, cb, *, name):
    S = up.shape[0]
    cg, cv, wg, wv, bg, bv = _ffn_specs(S)

    def body(g_ref, v_ref, wg_ref, wv_ref, bg_ref, bv_ref, o_ref):
        ug = _conv(g_ref[...], wg_ref, bg_ref, ALL, FFN_CONV)
        uv = _conv(v_ref[...], wv_ref, bv_ref, ALL, FFN_CONV)
        o_ref[...] = _gate(ug, uv).astype(BF16)

    return pl.pallas_call(
        body, name=name, grid=(NB_FF,), in_specs=[cg, cv, wg, wv, bg, bv], out_specs=cg,
        out_shape=_sds((S, DFF), BF16), compiler_params=_params(("parallel",)))(up, up, cw, cw, cb, cb)


def _ffn_gate_bwd(da, up, cw, cb, *, name):
    S = up.shape[0]
    cg, cv, wg, wv, bg, bv = _ffn_specs(S)

    def body(da_ref, g_ref, v_ref, wg_ref, wv_ref, bg_ref, bv_ref, dxg_ref, dxv_ref, dwg_ref, dwv_ref, dbg_ref, dbv_ref):
        xg, xv = g_ref[...], v_ref[...]
        ug = _conv(xg, wg_ref, bg_ref, ALL, FFN_CONV)
        uv = _conv(xv, wv_ref, bv_ref, ALL, FFN_CONV)
        dug, duv = jax.vjp(_gate, ug, uv)[1](da_ref[...])
        dxg_ref[...] = _conv_bwd(dug, xg, wg_ref, dwg_ref, dbg_ref, ALL, FFN_CONV).astype(BF16)
        dxv_ref[...] = _conv_bwd(duv, xv, wv_ref, dwv_ref, dbv_ref, ALL, FFN_CONV).astype(BF16)

    half = _sds((S, DFF), BF16)
    dxg, dxv, dwg, dwv, dbg, dbv = pl.pallas_call(
        body, name=name, grid=(NB_FF,), in_specs=[cg, cg, cv, wg, wv, bg, bv], out_specs=[cg, cg, wg, wg, bg, bg],
        out_shape=[half, half, _sds((FFN_CONV, DFF), F32), _sds((FFN_CONV, DFF), F32), _sds((1, DFF), F32), _sds((1, DFF), F32)],
        compiler_params=_params(("parallel",)))(da, up, up, cw, cw, cb, cb)
    return dxg, dxv, jnp.concatenate([dwg, dwv], axis=1), jnp.concatenate([dbg, dbv], axis=1)


def _log_sigmoid(x):
    return jnp.minimum(x, 0.0) - jnp.log1p(jnp.exp(-jnp.abs(x)))


@jax.custom_vjp
def _clamp_div(num, den, floor, shift):
    return num / jnp.maximum(jnp.abs(den), floor)


def _clamp_div_fwd(num, den, floor, shift):
    out = num / jnp.maximum(jnp.abs(den), floor)
    return out, (den, floor, out)


def _clamp_div_bwd(res, g):
    den, floor, out = res
    active = jnp.abs(den) < floor
    dinv = jnp.maximum(jnp.abs(den), floor)
    go = jnp.sum(g * out, axis=1, keepdims=True)
    ddiv = -go / dinv
    return (g / dinv, jnp.where(active, 0.0, ddiv * jnp.sign(den)), jnp.where(active, ddiv, 0.0),
            jnp.sum(jnp.where(active, go, 0.0), axis=0, keepdims=True))


_clamp_div.defvjp(_clamp_div_fwd, _clamp_div_bwd)


def _ml_head(q, k, v, o_pre, gates, gbias, C, n, ng, m, shift, hidx):
    L = q.shape[0]
    lane = lax.broadcasted_iota(jnp.int32, (L, 128), 1)
    lane1 = lax.broadcasted_iota(jnp.int32, (1, 128), 1)
    gz = gates + jnp.where(lane1 < ML_H, lax.stop_gradient(gbias), gbias)
    ig = jnp.sum(jnp.where(lane == hidx, gz, 0.0), axis=1, keepdims=True)
    lf = _log_sigmoid(jnp.sum(jnp.where(lane == ML_H + hidx, gz, 0.0), axis=1, keepdims=True))
    r = lax.broadcasted_iota(jnp.int32, (L, L), 0)
    c = lax.broadcasted_iota(jnp.int32, (L, L), 1)
    eye, tril = r == c, r >= c

    def to_row(col):
        return jnp.sum(jnp.where(eye, col, 0.0), axis=0, keepdims=True)

    b_col = jnp.sum(jnp.where(tril, to_row(lf), 0.0), axis=1, keepdims=True)
    Dm = jnp.where(tril, b_col - to_row(b_col) + to_row(ig), -jnp.inf)
    inter = b_col + m
    m_t = jnp.maximum(inter, jnp.max(Dm, axis=1, keepdims=True))
    w_inter = jnp.exp(inter - m_t)
    ks = k * (ML_DH ** -0.5)
    s = dot_nt(q, ks) * jnp.exp(Dm - m_t)
    num = w_inter * dot_nn(q, C) + dot_nn(s, v)
    den = w_inter * jnp.sum(q * n, axis=1, keepdims=True) + jnp.sum(s, axis=1, keepdims=True)
    h = _clamp_div(num, den, jnp.exp(-m_t), shift)
    g = jnp.sum(lf, axis=0, keepdims=True)
    a = g - b_col + ig
    m_new = jnp.maximum(g + m, jnp.max(a, axis=0, keepdims=True))
    decay = jnp.exp(g + m - m_new)
    wk = jnp.exp(a - m_new)
    C_new = decay * C + dot_tn(ks * wk, v)
    n_new = decay * n + jnp.sum(wk * ks, axis=0, keepdims=True)
    mu = jnp.mean(h, axis=1, keepdims=True)
    hc = h - mu
    var = jnp.mean(hc * hc, axis=1, keepdims=True)
    out = jax.nn.sigmoid(o_pre) * (hc * lax.rsqrt(var + EPS) * ng)
    return out, C_new, n_new, m_new


def _hs(h, off=0):
    return slice(off + h * ML_DH, off + (h + 1) * ML_DH)


def _mlstm_fwd(qk, proj, gbias, ng, *, name):
    S = qk.shape[0]
    nc = S // ML_L

    def body(q_ref, k_ref, v_ref, o_ref, g_ref, gb_ref, ng_ref, h_ref, cs_ref, ns_ref, ms_ref, c_s, n_s, m_s):
        @pl.when(pl.program_id(0) == 0)
        def _():
            c_s[...] = jnp.zeros_like(c_s)
            n_s[...] = jnp.zeros_like(n_s)
            m_s[...] = jnp.zeros_like(m_s)

        gates = g_ref[...]
        for h in range(ML_H):
            C, n, m = c_s[h], n_s[h:h + 1, :], m_s[h:h + 1, 0:1]
            cs_ref[0, h] = C
            ns_ref[0, h] = n
            ms_ref[0, h] = m_s[h:h + 1, :]
            out, C2, n2, m2 = _ml_head(q_ref[:, _hs(h)], k_ref[:, _hs(h)], v_ref[:, _hs(h)], o_ref[:, _hs(h)], gates,
                                       gb_ref[...], C, n, ng_ref[:, _hs(h)], m, jnp.zeros((1, 1), F32), h)
            h_ref[:, _hs(h)] = out.astype(BF16)
            c_s[h] = C2
            n_s[h:h + 1, :] = n2
            m_s[h:h + 1, :] = jnp.broadcast_to(m2, (1, 128))

    def w(j):
        return pl.BlockSpec((ML_L, ML_W), lambda c, j=j: (c, j))

    return pl.pallas_call(
        body, name=name, grid=(nc,),
        in_specs=[w(0), w(1), w(2), w(3), pl.BlockSpec((ML_L, 128), lambda c: (c, 22)),
                  pl.BlockSpec((1, 128), lambda c: (0, 0)), pl.BlockSpec((1, ML_W), lambda c: (0, 0))],
        out_specs=[w(0), pl.BlockSpec((1, ML_H, ML_DH, ML_DH), lambda c: (c, 0, 0, 0)),
                   pl.BlockSpec((1, ML_H, 1, 128), lambda c: (c, 0, 0, 0)), pl.BlockSpec((1, ML_H, 1, 128), lambda c: (c, 0, 0, 0))],
        out_shape=[_sds((S, ML_W), BF16), _sds((nc, ML_H, ML_DH, ML_DH), F32), _sds((nc, ML_H, 1, 128), F32),
                   _sds((nc, ML_H, 1, 128), F32)],
        scratch_shapes=[pltpu.VMEM((ML_H, ML_DH, ML_DH), F32), pltpu.VMEM((8, 128), F32), pltpu.VMEM((8, 128), F32)],
        compiler_params=_params(("arbitrary",)))(qk, qk, proj, proj, proj, gbias, ng)


def _mlstm_bwd(dh, qk, proj, gbias, ng, cs, ns, ms, *, name):
    S = qk.shape[0]
    nc = S // ML_L

    def body(dh_ref, q_ref, k_ref, v_ref, o_ref, g_ref, gb_ref, ng_ref, cs_ref, ns_ref, ms_ref,
             dqk_ref, dml_ref, dgb_ref, dng_ref, dc_s, dn_s, dm_s):
        @pl.when(pl.program_id(0) == 0)
        def _():
            dc_s[...] = jnp.zeros_like(dc_s)
            dn_s[...] = jnp.zeros_like(dn_s)
            dm_s[...] = jnp.zeros_like(dm_s)
            dgb_ref[...] = jnp.zeros_like(dgb_ref)
            dng_ref[...] = jnp.zeros_like(dng_ref)

        gates = g_ref[...]
        dgates = jnp.zeros_like(gates)
        dgb = jnp.zeros((1, 128), F32)
        for h in range(ML_H):
            f = functools.partial(_ml_head, hidx=h)
            _, vjp = jax.vjp(f, q_ref[:, _hs(h)], k_ref[:, _hs(h)], v_ref[:, _hs(h)], o_ref[:, _hs(h)], gates, gb_ref[...],
                             cs_ref[0, h], ns_ref[0, h], ng_ref[:, _hs(h)], ms_ref[0, h][:, 0:1], jnp.zeros((1, 1), F32))
            dq, dk, dv, do, dg_h, dgb_h, dC, dn, dng, dm, dshift = vjp(
                (dh_ref[:, _hs(h)], dc_s[h], dn_s[h:h + 1, :], dm_s[h:h + 1, 0:1]))
            dm_s[h:h + 1, :] = jnp.broadcast_to(dm, (1, 128))
            dgb_h = jnp.where(lax.broadcasted_iota(jnp.int32, (1, 128), 1) == h, dshift, dgb_h)
            dqk_ref[:, _hs(h)] = dq
            dqk_ref[:, _hs(h, ML_W)] = dk
            dml_ref[:, _hs(h)] = dv.astype(BF16)
            dml_ref[:, _hs(h, ML_W)] = do.astype(BF16)
            dng_ref[:, _hs(h)] += dng
            dgates = dgates + dg_h
            dgb = dgb + dgb_h
            dc_s[h] = dC
            dn_s[h:h + 1, :] = dn
        dml_ref[:, 2 * ML_W:] = dgates.astype(BF16)
        dgb_ref[...] += dgb

    def w(j):
        return pl.BlockSpec((ML_L, ML_W), lambda c, j=j: (nc - 1 - c, j))

    vec = pl.BlockSpec((1, 128), lambda c: (0, 0))
    vecw = pl.BlockSpec((1, ML_W), lambda c: (0, 0))
    gsp = pl.BlockSpec((ML_L, 128), lambda c: (nc - 1 - c, 22))
    st = pl.BlockSpec((1, ML_H, 1, 128), lambda c: (nc - 1 - c, 0, 0, 0))
    return pl.pallas_call(
        body, name=name, grid=(nc,),
        in_specs=[w(0), w(0), w(1), w(2), w(3), gsp, vec, vecw,
                  pl.BlockSpec((1, ML_H, ML_DH, ML_DH), lambda c: (nc - 1 - c, 0, 0, 0)), st, st],
        out_specs=[pl.BlockSpec((ML_L, 2 * ML_W), lambda c: (nc - 1 - c, 0)), pl.BlockSpec((ML_L, ML_GW), lambda c: (nc - 1 - c, 0)),
                   vec, vecw],
        out_shape=[_sds((S, 2 * ML_W), F32), _sds((S, ML_GW), BF16), _sds((1, 128), F32), _sds((1, ML_W), F32)],
        scratch_shapes=[pltpu.VMEM((ML_H, ML_DH, ML_DH), F32), pltpu.VMEM((8, 128), F32), pltpu.VMEM((8, 128), F32)],
        compiler_params=_params(("arbitrary",)))(dh, qk, qk, proj, proj, proj, gbias, ng, cs, ns, ms)


def _t5_buckets():
    r = np.arange(BLK)[:, None]
    c = np.arange(2 * BLK)[None, :]
    n = np.maximum(BLK + r - c, 0)
    max_exact = REL_B // 2
    nf = np.maximum(n, 1).astype(np.float32)
    large = max_exact + (np.log(nf / np.float32(max_exact)) / np.float32(math.log(REL_MAXD / max_exact))
                         * np.float32(REL_B - max_exact)).astype(np.int32)
    large = np.minimum(large, REL_B - 1)
    return np.where(n < max_exact, n, large).astype(np.int32)


def _bias_table(rel_bias, bucket, *, name):
    def body(rb_ref, bk_ref, o_ref):
        bk = bk_ref[...]
        for h in range(SW_H):
            acc = jnp.zeros((BLK, 2 * BLK), F32)
            for b in range(REL_B):
                acc = jnp.where(bk == b, rb_ref[b, h], acc)
            o_ref[h] = acc

    return pl.pallas_call(
        body, name=name, in_specs=[pl.BlockSpec(memory_space=pltpu.SMEM), pl.BlockSpec(memory_space=pltpu.VMEM)],
        out_specs=pl.BlockSpec(memory_space=pltpu.VMEM), out_shape=_sds((SW_H, BLK, 2 * BLK), F32),
        compiler_params=_params())(rel_bias, bucket)


def _bias_table_bwd(dbias_list, bucket, *, name):
    nl = len(dbias_list)

    def body(*refs):
        d_refs, bk_ref, o_ref = refs[:nl], refs[nl], refs[nl + 1]
        bk = bk_ref[...]
        rows = lax.broadcasted_iota(jnp.int32, (REL_B, 128), 0)
        lanes = lax.broadcasted_iota(jnp.int32, (REL_B, 128), 1)
        acc = jnp.zeros((REL_B, 128), F32)
        for h in range(SW_H):
            d = d_refs[0][h]
            for d_ref in d_refs[1:]:
                d = d + d_ref[h]
            for b in range(REL_B):
                t = jnp.sum(jnp.sum(jnp.where(bk == b, d, 0.0), axis=0, keepdims=True), axis=1, keepdims=True)
                acc = jnp.where((rows == b) & (lanes == h), t, acc)
        o_ref[...] = acc

    vm = pl.BlockSpec(memory_space=pltpu.VMEM)
    return pl.pallas_call(
        body, name=name, in_specs=[vm] * (nl + 1), out_specs=vm, out_shape=_sds((REL_B, 128), F32),
        compiler_params=_params())(*dbias_list, bucket)


def _swa_head(qh, kph, kch, vph, vch, bp, bc, sinks, has_prev, h):
    r = lax.broadcasted_iota(jnp.int32, (BLK, BLK), 0)
    c = lax.broadcasted_iota(jnp.int32, (BLK, BLK), 1)
    lane = lax.broadcasted_iota(jnp.int32, (1, 128), 1)
    sink = jnp.sum(jnp.where(lane == h, sinks, 0.0), axis=1, keepdims=True)
    lp = jnp.where((c > r) & has_prev, dot_nt(qh, kph) * (SW_DH ** -0.5) + bp, -jnp.inf)
    lc = jnp.where(c <= r, dot_nt(qh, kch) * (SW_DH ** -0.5) + bc, -jnp.inf)
    mx = jnp.maximum(jnp.maximum(jnp.max(lp, axis=1, keepdims=True), jnp.max(lc, axis=1, keepdims=True)), sink)
    mx = lax.stop_gradient(mx)
    pp, pc = jnp.exp(lp - mx), jnp.exp(lc - mx)
    den = jnp.sum(pp, axis=1, keepdims=True) + jnp.sum(pc, axis=1, keepdims=True) + jnp.exp(sink - mx)
    return dot_nn(pp / den, vph) + dot_nn(pc / den, vch)


def _qs(h):
    return slice(h * SW_DH, (h + 1) * SW_DH)


def _kvs(h):
    return slice((h // SW_G) * SW_DH, (h // SW_G + 1) * SW_DH)


def _swa_fwd(proj, bias, sinks, *, name):
    S = proj.shape[0]
    nb = S // BLK

    def body(q_ref, kp_ref, kc_ref, vp_ref, vc_ref, b_ref, s_ref, o_ref):
        has_prev = pl.program_id(0) > 0
        for h in range(SW_H):
            o_ref[:, _qs(h)] = _swa_head(q_ref[:, _qs(h)], kp_ref[:, _kvs(h)], kc_ref[:, _kvs(h)], vp_ref[:, _kvs(h)],
                                         vc_ref[:, _kvs(h)], b_ref[h, :, :BLK], b_ref[h, :, BLK:], s_ref[...], has_prev,
                                         h).astype(BF16)

    def cur(j):
        return pl.BlockSpec((BLK, 128), lambda n, j=j: (n, j))

    def prev(j):
        return pl.BlockSpec((BLK, 128), lambda n, j=j: (jnp.maximum(n - 1, 0), j))

    return pl.pallas_call(
        body, name=name, grid=(nb,),
        in_specs=[pl.BlockSpec((BLK, SW_W), lambda n: (n, 4)), prev(20), cur(20), prev(21), cur(21),
                  pl.BlockSpec((SW_H, BLK, 2 * BLK), lambda n: (0, 0, 0)), pl.BlockSpec((1, 128), lambda n: (0, 0))],
        out_specs=pl.BlockSpec((BLK, SW_W), lambda n: (n, 0)), out_shape=_sds((S, SW_W), BF16),
        compiler_params=_params(("parallel",)))(proj, proj, proj, proj, proj, bias, sinks)


def _swa_bwd(dh, proj, bias, sinks, *, name):
    S = proj.shape[0]
    nb = S // BLK

    def body(dh_ref, q_ref, kp_ref, kc_ref, vp_ref, vc_ref, b_ref, s_ref, dsw_ref, db_ref, ds_ref, ck_s, cv_s, nk_s, nv_s):
        i = pl.program_id(0)

        @pl.when(i == 0)
        def _():
            ck_s[...] = jnp.zeros_like(ck_s)
            cv_s[...] = jnp.zeros_like(cv_s)
            db_ref[...] = jnp.zeros_like(db_ref)
            ds_ref[...] = jnp.zeros_like(ds_ref)

        has_prev = i < nb - 1
        ds = jnp.zeros((1, 128), F32)
        for g in range(SW_H // SW_G):
            dkp = dkc = dvp = dvc = jnp.zeros((BLK, SW_DH), F32)
            for h in range(g * SW_G, (g + 1) * SW_G):
                f = functools.partial(_swa_head, has_prev=has_prev, h=h)
                _, vjp = jax.vjp(f, q_ref[:, _qs(h)], kp_ref[:, _kvs(h)], kc_ref[:, _kvs(h)], vp_ref[:, _kvs(h)],
                                 vc_ref[:, _kvs(h)], b_ref[h, :, :BLK], b_ref[h, :, BLK:], s_ref[...])
                dq, a, b, c, d, dbp, dbc, ds_h = vjp(dh_ref[:, _qs(h)])
                dkp, dkc, dvp, dvc, ds = dkp + a, dkc + b, dvp + c, dvc + d, ds + ds_h
                dsw_ref[:, _qs(h)] = dq.astype(BF16)
                db_ref[h, :, :BLK] += dbp
                db_ref[h, :, BLK:] += dbc
            kv = slice(g * SW_DH, (g + 1) * SW_DH)
            dsw_ref[:, SW_W + g * SW_DH:SW_W + (g + 1) * SW_DH] = (dkc + ck_s[:, kv]).astype(BF16)
            dsw_ref[:, SW_W + SW_KVW + g * SW_DH:SW_W + SW_KVW + (g + 1) * SW_DH] = (dvc + cv_s[:, kv]).astype(BF16)
            nk_s[:, kv] = dkp
            nv_s[:, kv] = dvp
        ck_s[...] = nk_s[...]
        cv_s[...] = nv_s[...]
        ds_ref[...] += ds

    def cur(j):
        return pl.BlockSpec((BLK, 128), lambda i, j=j: (nb - 1 - i, j))

    def prev(j):
        return pl.BlockSpec((BLK, 128), lambda i, j=j: (jnp.maximum(nb - 2 - i, 0), j))

    bsp = pl.BlockSpec((SW_H, BLK, 2 * BLK), lambda i: (0, 0, 0))
    vec = pl.BlockSpec((1, 128), lambda i: (0, 0))
    return pl.pallas_call(
        body, name=name, grid=(nb,),
        in_specs=[pl.BlockSpec((BLK, SW_W), lambda i: (nb - 1 - i, 1)), pl.BlockSpec((BLK, SW_W), lambda i: (nb - 1 - i, 4)),
                  prev(20), cur(20), prev(21), cur(21), bsp, vec],
        out_specs=[pl.BlockSpec((BLK, SW_GW), lambda i: (nb - 1 - i, 0)), bsp, vec],
        out_shape=[_sds((S, SW_GW), BF16), _sds((SW_H, BLK, 2 * BLK), F32), _sds((1, 128), F32)],
        scratch_shapes=[pltpu.VMEM((BLK, 128), F32)] * 4,
        compiler_params=_params(("arbitrary",)))(dh, proj, proj, proj, proj, proj, bias, sinks)


XA_TM = 512


def _xa_head(qh, kh, vh):
    logits = dot_nt(qh, kh) * (XA_DH ** -0.5)
    mx = lax.stop_gradient(jnp.max(logits, axis=1, keepdims=True))
    e = jnp.exp(logits - mx)
    return dot_nn(e / jnp.sum(e, axis=1, keepdims=True), vh)


def _xs(h, off=0):
    return slice(off + h * XA_DH, off + (h + 1) * XA_DH)


def _xattn_fwd(q, kv, *, name):
    S = q.shape[0]
    M = kv.shape[0]

    def body(q_ref, kv_ref, o_ref):
        for h in range(XA_H):
            o_ref[:, _xs(h)] = _xa_head(q_ref[:, _xs(h)], kv_ref[:, _xs(h)], kv_ref[:, _xs(h, D)]).astype(BF16)

    row = pl.BlockSpec((XA_TM, D), lambda i: (i, 0))
    return pl.pallas_call(
        body, name=name, grid=(S // XA_TM,), in_specs=[row, pl.BlockSpec((M, 2 * D), lambda i: (0, 0))], out_specs=row,
        out_shape=_sds((S, D), BF16), compiler_params=_params(("parallel",)))(q, kv)


def _xattn_bwd(do, q, kv, *, name):
    S = q.shape[0]
    M = kv.shape[0]

    def body(do_ref, q_ref, kv_ref, dq_ref, dkv_ref):
        @pl.when(pl.program_id(0) == 0)
        def _():
            dkv_ref[...] = jnp.zeros_like(dkv_ref)

        for h in range(XA_H):
            _, vjp = jax.vjp(_xa_head, q_ref[:, _xs(h)], kv_ref[:, _xs(h)], kv_ref[:, _xs(h, D)])
            dq, dk, dv = vjp(do_ref[:, _xs(h)])
            dq_ref[:, _xs(h)] = dq.astype(BF16)
            dkv_ref[:, _xs(h)] += dk
            dkv_ref[:, _xs(h, D)] += dv

    row = pl.BlockSpec((XA_TM, D), lambda i: (i, 0))
    full = pl.BlockSpec((M, 2 * D), lambda i: (0, 0))
    return pl.pallas_call(
        body, name=name, grid=(S // XA_TM,), in_specs=[row, row, full], out_specs=[row, full],
        out_shape=[_sds((S, D), BF16), _sds((M, 2 * D), F32)], compiler_params=_params(("arbitrary",)))(do, q, kv)


def _loss_head(y, tgt, *, name):
    S = y.shape[0]
    tm = 512

    def body(y_ref, t_ref, l_ref, dy_ref):
        e = y_ref[...] - t_ref[...]
        dy_ref[...] = e * (1.0 / D)
        part = 0.5 * jnp.sum(jnp.sum(e * e, axis=1, keepdims=True) * (1.0 / D), axis=0, keepdims=True)

        @pl.when(pl.program_id(0) == 0)
        def _():
            l_ref[...] = jnp.broadcast_to(part, (8, 128))

        @pl.when(pl.program_id(0) > 0)
        def _():
            l_ref[...] += jnp.broadcast_to(part, (8, 128))

    row = pl.BlockSpec((tm, D), lambda i: (i, 0))
    return pl.pallas_call(
        body, name=name, grid=(S // tm,), in_specs=[row, row], out_specs=[pl.BlockSpec((8, 128), lambda i: (0, 0)), row],
        out_shape=[_sds((8, 128), F32), _sds((S, D), F32)], compiler_params=_params(("arbitrary",)))(y, tgt)


ANY = pl.BlockSpec(memory_space=pl.ANY)


def _place():
    x, y, c = lax.axis_index("x"), lax.axis_index("y"), lax.axis_index("c")
    chips = [(1 - x, y), (x, 1 - y), (1 - x, 1 - y)]
    return x, y, c, chips


def _gather(arrs, *, name):
    n = len(arrs)

    def body(*refs):
        srcs, outs = refs[:n], refs[n:2 * n]
        send_sems, recv_sems, local_sems = refs[2 * n:]
        x, y, c, chips = _place()
        me, sib = (x, y, c), (x, y, 1 - c)

        def idx(p):
            return 4 * p[0] + 2 * p[1] + p[2]

        def copy(i, k, block, to, from_src=False):
            return pltpu.make_async_remote_copy(
                src_ref=srcs[i] if from_src else outs[i].at[idx(block)], dst_ref=outs[i].at[idx(block)],
                send_sem=send_sems.at[7 * i + k], recv_sem=recv_sems.at[7 * i + k], device_id=to, device_id_type=MESH)

        local = [pltpu.make_async_copy(srcs[i], outs[i].at[idx(me)], local_sems.at[i]) for i in range(n)]
        for cp in local:
            cp.start()
        first = []
        for i in range(n):
            first.append(copy(i, 0, me, sib, True))
            first += [copy(i, 1 + j, me, (*chip, c), True) for j, chip in enumerate(chips)]
        for cp in first:
            cp.start()
        passed = []
        for j, chip in enumerate(chips):
            for i in range(n):
                copy(i, 1 + j, (*chip, c), me).wait_recv()
                cp = copy(i, 4 + j, (*chip, c), sib)
                cp.start()
                passed.append(cp)
        for i in range(n):
            copy(i, 0, sib, me).wait_recv()
        for j, chip in enumerate(chips):
            for i in range(n):
                copy(i, 4 + j, (*chip, 1 - c), me).wait_recv()
        for cp in first + passed:
            cp.wait_send()
        for cp in local:
            cp.wait()

    return pl.pallas_call(
        body, name=name, in_specs=[ANY] * n, out_specs=[ANY] * n,
        out_shape=[_sds((N_DEV,) + a.shape, a.dtype) for a in arrs],
        scratch_shapes=[pltpu.SemaphoreType.DMA((7 * n,)), pltpu.SemaphoreType.DMA((7 * n,)), pltpu.SemaphoreType.DMA((n,))],
        compiler_params=pltpu.CompilerParams(has_side_effects=True))(*arrs)


def _swap_sibling(arrs, *, name):
    n = len(arrs)

    def body(*refs):
        srcs, outs = refs[:n], refs[n:2 * n]
        send_sems, recv_sems = refs[2 * n:]
        x, y, c, _ = _place()
        copies = [pltpu.make_async_remote_copy(
            src_ref=srcs[i].at[2 * j + (1 - c)], dst_ref=outs[i].at[j], send_sem=send_sems.at[N_CHIP * i + j],
            recv_sem=recv_sems.at[N_CHIP * i + j], device_id=(x, y, 1 - c), device_id_type=MESH)
            for i in range(n) for j in range(N_CHIP)]
        for cp in copies:
            cp.start()
        for cp in copies:
            cp.wait()

    return pl.pallas_call(
        body, name=name, in_specs=[ANY] * n, out_specs=[ANY] * n,
        out_shape=[_sds((N_CHIP,) + a.shape[1:], a.dtype) for a in arrs],
        scratch_shapes=[pltpu.SemaphoreType.DMA((N_CHIP * n,)), pltpu.SemaphoreType.DMA((N_CHIP * n,))],
        compiler_params=pltpu.CompilerParams(has_side_effects=True))(*arrs)


def _pair_sum(part, got, c_idx, *, name):
    _, R, C = part.shape
    tr = next(t for t in (512, 256, 128, 64, 32, 16) if R % t == 0)

    def body(c_ref, p_ref, g_ref, o_ref):
        o_ref[...] = (p_ref[0].astype(F32) + g_ref[...].astype(F32)).astype(o_ref.dtype)

    return pl.pallas_call(
        body, name=name,
        grid_spec=pltpu.PrefetchScalarGridSpec(
            num_scalar_prefetch=1, grid=(N_CHIP, R // tr),
            in_specs=[pl.BlockSpec((1, 1, tr, C), lambda j, r, c_ref: (j, c_ref[0], r, 0)),
                      pl.BlockSpec((1, tr, C), lambda j, r, c_ref: (j, r, 0))],
            out_specs=pl.BlockSpec((1, tr, C), lambda j, r, c_ref: (j, r, 0))),
        out_shape=_sds((N_CHIP, R, C), part.dtype),
        compiler_params=_params(("parallel", "parallel")))(c_idx, part.reshape(N_CHIP, 2, R, C), got)


def _swap_chips(arrs, *, name):
    n = len(arrs)

    def body(*refs):
        srcs, outs = refs[:n], refs[n:2 * n]
        send_sems, recv_sems, local_sems = refs[2 * n:]
        x, y, c, chips = _place()
        jme = 2 * x + y
        local = [pltpu.make_async_copy(srcs[i].at[jme], outs[i].at[jme], local_sems.at[i]) for i in range(n)]
        for cp in local:
            cp.start()
        copies = [pltpu.make_async_remote_copy(
            src_ref=srcs[i].at[2 * chip[0] + chip[1]], dst_ref=outs[i].at[jme], send_sem=send_sems.at[3 * i + j],
            recv_sem=recv_sems.at[3 * i + j], device_id=(*chip, c), device_id_type=MESH)
            for i in range(n) for j, chip in enumerate(chips)]
        for cp in copies:
            cp.start()
        for i in range(n):
            for j, chip in enumerate(chips):
                pltpu.make_async_remote_copy(
                    src_ref=srcs[i].at[jme], dst_ref=outs[i].at[2 * chip[0] + chip[1]], send_sem=send_sems.at[3 * i + j],
                    recv_sem=recv_sems.at[3 * i + j], device_id=(*chip, c), device_id_type=MESH).wait_recv()
        for cp in copies:
            cp.wait_send()
        for cp in local:
            cp.wait()

    return pl.pallas_call(
        body, name=name, in_specs=[ANY] * n, out_specs=[ANY] * n,
        out_shape=[_sds(a.shape, a.dtype) for a in arrs],
        scratch_shapes=[pltpu.SemaphoreType.DMA((3 * n,)), pltpu.SemaphoreType.DMA((3 * n,)), pltpu.SemaphoreType.DMA((n,))],
        compiler_params=pltpu.CompilerParams(has_side_effects=True))(*arrs)


def _adamw(parts, w, m, v, *, name):
    P, R, C = parts.shape
    tr = next((t for t in (256, 128, 64, 32, 16, 8) if R % t == 0), R)
    c1 = 1.0 / (1.0 - ADAM_B1 ** ADAM_STEP)
    c2 = 1.0 / (1.0 - ADAM_B2 ** ADAM_STEP)

    def body(p_ref, w_ref, m_ref, v_ref, g_ref, d_ref, nm_ref, nv_ref):
        g = p_ref[0].astype(F32)
        for j in range(1, P):
            g = g + p_ref[j].astype(F32)
        nm = ADAM_B1 * m_ref[...] + (1.0 - ADAM_B1) * g
        nv = ADAM_B2 * v_ref[...] + (1.0 - ADAM_B2) * (g * g)
        g_ref[...] = g
        nm_ref[...] = nm
        nv_ref[...] = nv
        d_ref[...] = -ADAM_LR * ((nm * c1) / (jnp.sqrt(nv * c2) + ADAM_EPS) + ADAM_WD * w_ref[...])

    row = pl.BlockSpec((tr, C), lambda i: (i, 0))
    out = _sds((R, C), F32)
    return pl.pallas_call(
        body, name=name, grid=(R // tr,), in_specs=[pl.BlockSpec((P, tr, C), lambda i: (0, i, 0)), row, row, row],
        out_specs=[row, row, row, row], out_shape=[out, out, out, out], compiler_params=_params(("parallel",)))(parts, w, m, v)


BIG = ("w_in", "w_out", "xa_wq", "xa_wkv", "xa_wo", "ffn_w_up", "ffn_w_down")
COL_SHARDED = ("w_in", "xa_wkv", "ffn_w_up", "ml_conv_w", "ffn_conv_w")
SHARDED_SMALL = ("ml_conv_w", "ffn_conv_w")
REPLICATED = ("rel_bias", "ml_conv_b", "ml_i_bias", "ml_f_bias", "ml_norm_g", "swa_sinks", "ln1_g", "ln1_b", "ln2_g", "ln2_b",
              "ffn_conv_b", "ln3_g", "ln3_b")
NAMES = ("rel_bias", "w_in", "ml_conv_w", "ml_conv_b", "ml_i_bias", "ml_f_bias", "ml_norm_g", "swa_sinks", "w_out", "ln1_g", "ln1_b",
         "xa_wq", "xa_wkv", "xa_wo", "ln2_g", "ln2_b", "ffn_w_up", "ffn_conv_w", "ffn_conv_b", "ffn_w_down", "ln3_g", "ln3_b")


def _flat_rows(a, mult):
    f = a.reshape(-1)
    n = -(-f.shape[0] // (128 * mult)) * (128 * mult)
    if n != f.shape[0]:
        f = jnp.pad(f, (0, n - f.shape[0]))
    return f.reshape(-1, 128)


def _pack(arrs, mult):
    parts = [_flat_rows(a, mult) for a in arrs]
    return jnp.concatenate(parts, axis=0), [p.shape[0] for p in parts]


def _unpack(flat, rows, shapes):
    out, off = [], 0
    lead = flat.shape[:-2]
    for r, shp in zip(rows, shapes):
        n = int(np.prod(shp))
        piece = flat[..., off:off + r, :].reshape(lead + (r * 128,))[..., :n]
        out.append(piece.reshape(lead + tuple(shp)))
        off += r
    return out


def _full_from_shards(stacked, name):
    if name in COL_SHARDED:
        return jnp.moveaxis(stacked, 0, 2).reshape(stacked.shape[1], stacked.shape[2], N_DEV * stacked.shape[3])
    return jnp.moveaxis(stacked, 0, 1).reshape(stacked.shape[1], N_DEV * stacked.shape[2], stacked.shape[3])


def _shards_from_full(full, name):
    L, A, B = full.shape
    if name in COL_SHARDED:
        return jnp.moveaxis(full.reshape(L, A, N_DEV, B // N_DEV), 2, 0)
    return jnp.moveaxis(full.reshape(L, N_DEV, A // N_DEV, B), 1, 0)


def _pad_win(w):
    z = jnp.zeros(w.shape[:-1] + (NP_IN - N_IN,), w.dtype)
    return jnp.concatenate([w[..., :2048], w[..., 2056:], w[..., 2048:2056], z], axis=-1)


def _row128(v):
    return jnp.pad(v, (0, 128 - v.shape[0])).reshape(1, 128)


def _gather_weights(W):
    sm_flat, sm_rows = _pack([W[n] for n in SHARDED_SMALL], 8)
    got = _gather([W[n].astype(BF16) for n in BIG] + [sm_flat], name="gather_weights")
    full = {n: _full_from_shards(s, n) for n, s in zip(BIG, got[:-1])}
    full.update({n: _full_from_shards(s, n)
                 for n, s in zip(SHARDED_SMALL, _unpack(got[-1], sm_rows, [W[n].shape for n in SHARDED_SMALL]))})
    full["w_in"] = _pad_win(full["w_in"])
    return full


def _reduce_and_update(Gf, loss_part, W, Mo, Vo, me, c_idx):
    send = [_shards_from_full(Gf[n], n).astype(BF16) for n in BIG]
    send = [s.reshape(N_DEV, -1, s.shape[-1]) for s in send]
    got = _swap_sibling(send, name="swap_sibling")
    sums = [_pair_sum(s, g, c_idx, name=f"pair_sum_{n}") for n, s, g in zip(BIG, send, got)]
    parts = _swap_chips(sums, name="swap_chips")
    res = {}
    for n, p in zip(BIG, parts):
        shp = W[n].shape
        flat = (shp[0] * shp[1], shp[2])
        outs = _adamw(p, W[n].reshape(flat), Mo[n].reshape(flat), Vo[n].reshape(flat), name=f"adamw_{n}")
        for kind, a in zip(("g", "d", "m", "v"), outs):
            res[kind, n] = a.reshape(shp)

    small = REPLICATED + SHARDED_SMALL
    sp_flat, sp_rows = _pack([Gf[n] for n in small] + [loss_part], 8)
    sp_all = _gather([sp_flat], name="gather_small_grads")[0]

    def widen(n, t):
        if n not in SHARDED_SMALL:
            return t[n]
        return lax.dynamic_update_slice(jnp.zeros(Gf[n].shape, F32), t[n], (0, 0, me * t[n].shape[2]))

    zl = jnp.zeros((8, 128), F32)
    wsm, _ = _pack([widen(n, W) for n in small] + [zl], 8)
    msm, _ = _pack([widen(n, Mo) for n in small] + [zl], 8)
    vsm, _ = _pack([widen(n, Vo) for n in small] + [zl], 8)
    outs_small = [_unpack(o_, sp_rows, [Gf[n].shape for n in small] + [(8, 128)])
                  for o_ in _adamw(sp_all, wsm, msm, vsm, name="adamw_small")]
    for kind, os_ in zip(("g", "d", "m", "v"), outs_small):
        for n, a in zip(small, os_[:-1]):
            if n in SHARDED_SMALL:
                a = lax.dynamic_slice(a, (0, 0, me * W[n].shape[2]), W[n].shape)
            res[kind, n] = a
    return res, outs_small[0][-1][0, 0]


def kernel(x, mem, rel_bias, w_in, ml_conv_w, ml_conv_b, ml_i_bias, ml_f_bias, ml_norm_g, swa_sinks, w_out, ln1_g, ln1_b, xa_wq, xa_wkv, xa_wo, ln2_g, ln2_b, ffn_w_up, ffn_conv_w, ffn_conv_b, ffn_w_down, ln3_g, ln3_b, loss_target, m_rel_bias, m_w_in, m_ml_conv_w, m_ml_conv_b, m_ml_i_bias, m_ml_f_bias, m_ml_norm_g, m_swa_sinks, m_w_out, m_ln1_g, m_ln1_b, m_xa_wq, m_xa_wkv, m_xa_wo, m_ln2_g, m_ln2_b, m_ffn_w_up, m_ffn_conv_w, m_ffn_conv_b, m_ffn_w_down, m_ln3_g, m_ln3_b, v_rel_bias, v_w_in, v_ml_conv_w, v_ml_conv_b, v_ml_i_bias, v_ml_f_bias, v_ml_norm_g, v_swa_sinks, v_w_out, v_ln1_g, v_ln1_b, v_xa_wq, v_xa_wkv, v_xa_wo, v_ln2_g, v_ln2_b, v_ffn_w_up, v_ffn_conv_w, v_ffn_conv_b, v_ffn_w_down, v_ln3_g, v_ln3_b):
    W = dict(rel_bias=rel_bias, w_in=w_in, ml_conv_w=ml_conv_w, ml_conv_b=ml_conv_b, ml_i_bias=ml_i_bias, ml_f_bias=ml_f_bias,
             ml_norm_g=ml_norm_g, swa_sinks=swa_sinks, w_out=w_out, ln1_g=ln1_g, ln1_b=ln1_b, xa_wq=xa_wq, xa_wkv=xa_wkv,
             xa_wo=xa_wo, ln2_g=ln2_g, ln2_b=ln2_b, ffn_w_up=ffn_w_up, ffn_conv_w=ffn_conv_w, ffn_conv_b=ffn_conv_b,
             ffn_w_down=ffn_w_down, ln3_g=ln3_g, ln3_b=ln3_b)
    Mo = dict(rel_bias=m_rel_bias, w_in=m_w_in, ml_conv_w=m_ml_conv_w, ml_conv_b=m_ml_conv_b, ml_i_bias=m_ml_i_bias,
              ml_f_bias=m_ml_f_bias, ml_norm_g=m_ml_norm_g, swa_sinks=m_swa_sinks, w_out=m_w_out, ln1_g=m_ln1_g, ln1_b=m_ln1_b,
              xa_wq=m_xa_wq, xa_wkv=m_xa_wkv, xa_wo=m_xa_wo, ln2_g=m_ln2_g, ln2_b=m_ln2_b, ffn_w_up=m_ffn_w_up,
              ffn_conv_w=m_ffn_conv_w, ffn_conv_b=m_ffn_conv_b, ffn_w_down=m_ffn_w_down, ln3_g=m_ln3_g, ln3_b=m_ln3_b)
    Vo = dict(rel_bias=v_rel_bias, w_in=v_w_in, ml_conv_w=v_ml_conv_w, ml_conv_b=v_ml_conv_b, ml_i_bias=v_ml_i_bias,
              ml_f_bias=v_ml_f_bias, ml_norm_g=v_ml_norm_g, swa_sinks=v_swa_sinks, w_out=v_w_out, ln1_g=v_ln1_g, ln1_b=v_ln1_b,
              xa_wq=v_xa_wq, xa_wkv=v_xa_wkv, xa_wo=v_xa_wo, ln2_g=v_ln2_g, ln2_b=v_ln2_b, ffn_w_up=v_ffn_w_up,
              ffn_conv_w=v_ffn_conv_w, ffn_conv_b=v_ffn_conv_b, ffn_w_down=v_ffn_w_down, ln3_g=v_ln3_g, ln3_b=v_ln3_b)
    S = x.shape[1]
    c_me = lax.axis_index("c")
    me = 4 * lax.axis_index("x") + 2 * lax.axis_index("y") + c_me
    c_idx = jnp.reshape(c_me, (1,)).astype(jnp.int32)
    xs = x.reshape(S, D)
    mems = mem.reshape(mem.shape[1], D)
    tgt = loss_target.reshape(S, D)

    full = _gather_weights(W)
    win = full["w_in"]
    win_ml = jnp.concatenate([win[..., 1024:2048], win[..., 2816:2944]], axis=-1)
    win_sw = win[..., 2048:2816]
    win_qk = win[..., :1024]

    bucket = jnp.asarray(_t5_buckets())
    bias = _bias_table(rel_bias, bucket, name="bias_table")

    saved = []
    h0, h0b = xs, xs.astype(BF16)
    for l in range(DEPTH):
        gbias = _row128(jnp.concatenate([ml_i_bias[l], ml_f_bias[l]]))
        sinks = _row128(swa_sinks[l])
        ng = ml_norm_g[l].reshape(1, ML_W)
        proj = _mm(h0b, win[l], name=f"proj{l}")
        qk = _silu_conv_fwd(proj, full["ml_conv_w"][l], ml_conv_b[l].reshape(1, -1), name=f"mlconv{l}")
        h_ml, cs, ns, ms = _mlstm_fwd(qk, proj, gbias, ng, name=f"mlstm{l}")
        h_sw = _swa_fwd(proj, bias, sinks, name=f"swa{l}")
        h1, h1b, z1 = _mm_res_ln([h_ml, h_sw], full["w_out"][l], h0, ln1_g[l].reshape(1, D), ln1_b[l].reshape(1, D),
                                 name=f"mix_out{l}")
        q = _mm(h1b, full["xa_wq"][l], name=f"xa_q{l}")
        kv = _mm(mems, full["xa_wkv"][l], tm=256, name=f"xa_kv{l}")
        o = _xattn_fwd(q, kv, name=f"xattn{l}")
        h2, h2b, z2 = _mm_res_ln([o], full["xa_wo"][l], h1, ln2_g[l].reshape(1, D), ln2_b[l].reshape(1, D), name=f"xa_out{l}")
        up = _mm(h2b, full["ffn_w_up"][l], name=f"ffn_up{l}")
        act = _ffn_gate_fwd(up, full["ffn_conv_w"][l], ffn_conv_b[l].reshape(1, -1), name=f"ffn_gate{l}")
        h3, h3b, z3 = _mm_res_ln([act], full["ffn_w_down"][l], h2, ln3_g[l].reshape(1, D), ln3_b[l].reshape(1, D),
                                 name=f"ffn_out{l}")
        saved.append(dict(h0b=h0b, proj=proj, qk=qk, cs=cs, ns=ns, ms=ms, h_ml=h_ml, h_sw=h_sw, z1=z1, h1b=h1b, q=q, kv=kv, o=o,
                          z2=z2, h2b=h2b, up=up, act=act, z3=z3, gbias=gbias, sinks=sinks, ng=ng))
        h0, h0b = h3, h3b

    loss_part, dh = _loss_head(h0, tgt, name="loss_head")

    G = {n: [None] * DEPTH for n in NAMES if n != "rel_bias"}
    dbias = [None] * DEPTH
    for l in reversed(range(DEPTH)):
        sv = saved[l]
        dz3, dz3b, G["ln3_g"][l], G["ln3_b"][l] = _ln_bwd(dh, sv["z3"], ln3_g[l].reshape(1, D), name=f"ln3_bwd{l}")
        G["ffn_w_down"][l] = _mm_tn(sv["act"], dz3b, name=f"d_w_down{l}")
        dact = _mm(dz3b, full["ffn_w_down"][l], trans_b=True, name=f"d_act{l}")
        dupg, dupv, G["ffn_conv_w"][l], G["ffn_conv_b"][l] = _ffn_gate_bwd(
            dact, sv["up"], full["ffn_conv_w"][l], ffn_conv_b[l].reshape(1, -1), name=f"ffn_gate_bwd{l}")
        G["ffn_w_up"][l] = jnp.concatenate([_mm_tn(sv["h2b"], dupg, name=f"d_w_up_g{l}"),
                                            _mm_tn(sv["h2b"], dupv, name=f"d_w_up_v{l}")], axis=1)
        dh2 = _mm([dupg, dupv], full["ffn_w_up"][l], trans_b=True, add=dz3, add_scale=ALPHA, name=f"d_h2_{l}")

        dz2, dz2b, G["ln2_g"][l], G["ln2_b"][l] = _ln_bwd(dh2, sv["z2"], ln2_g[l].reshape(1, D), name=f"ln2_bwd{l}")
        G["xa_wo"][l] = _mm_tn(sv["o"], dz2b, name=f"d_xa_wo{l}")
        do = _mm(dz2b, full["xa_wo"][l], trans_b=True, name=f"d_xa_o{l}")
        dq, dkv = _xattn_bwd(do, sv["q"], sv["kv"], name=f"xattn_bwd{l}")
        G["xa_wkv"][l] = _mm_tn(mems, dkv, name=f"d_xa_wkv{l}")
        G["xa_wq"][l] = _mm_tn(sv["h1b"], dq, name=f"d_xa_wq{l}")
        dh1 = _mm(dq, full["xa_wq"][l], trans_b=True, add=dz2, add_scale=ALPHA, name=f"d_h1_{l}")

        dz1, dz1b, G["ln1_g"][l], G["ln1_b"][l] = _ln_bwd(dh1, sv["z1"], ln1_g[l].reshape(1, D), name=f"ln1_bwd{l}")
        G["w_out"][l] = jnp.concatenate([_mm_tn(sv["h_ml"], dz1b, name=f"d_w_out_ml{l}"),
                                         _mm_tn(sv["h_sw"], dz1b, name=f"d_w_out_sw{l}")], axis=0)
        dhcat = _mm(dz1b, full["w_out"][l], trans_b=True, name=f"d_hcat{l}")
        dsw, dbias[l], dsinks = _swa_bwd(dhcat, sv["proj"], bias, sv["sinks"], name=f"swa_bwd{l}")
        dqk, dml, dgb, dng = _mlstm_bwd(dhcat, sv["qk"], sv["proj"], sv["gbias"], sv["ng"], sv["cs"], sv["ns"], sv["ms"],
                                        name=f"mlstm_bwd{l}")
        dqk_pre, G["ml_conv_w"][l], G["ml_conv_b"][l] = _silu_conv_bwd(
            dqk, sv["proj"], full["ml_conv_w"][l], ml_conv_b[l].reshape(1, -1), name=f"mlconv_bwd{l}")
        dw_qk = _mm_tn(sv["h0b"], dqk_pre, name=f"d_w_in_qk{l}")
        dw_ml = _mm_tn(sv["h0b"], dml, name=f"d_w_in_ml{l}")
        dw_sw = _mm_tn(sv["h0b"], dsw, name=f"d_w_in_sw{l}")
        G["w_in"][l] = jnp.concatenate([dw_qk, dw_ml[:, :2 * ML_W + 2 * ML_H], dw_sw], axis=1)
        dh = _mm(dqk_pre, win_qk[l], trans_b=True, add=dz1, add_scale=ALPHA, name=f"d_h0_qk{l}")
        dh = _mm(dml, win_ml[l], trans_b=True, add=dh, name=f"d_h0_ml{l}")
        dh = _mm(dsw, win_sw[l], trans_b=True, add=dh, name=f"d_h0_sw{l}")
        G["ml_i_bias"][l] = dgb[0, :ML_H]
        G["ml_f_bias"][l] = dgb[0, ML_H:2 * ML_H]
        G["ml_norm_g"][l] = dng
        G["swa_sinks"][l] = dsinks[0, :SW_H]
    grad_x = dh.reshape(x.shape)

    Gf = {n: jnp.stack([g.reshape(W[n].shape[1:]) if n in REPLICATED else g for g in G[n]]) for n in G}
    Gf["rel_bias"] = _bias_table_bwd(dbias, bucket, name="bias_table_bwd")[:, :SW_H]

    res, loss = _reduce_and_update(Gf, loss_part, W, Mo, Vo, me, c_idx)
    return (loss, grad_x, *[res["g", n] for n in NAMES], *[res["d", n] for n in NAMES], *[res["m", n] for n in NAMES],
            *[res["v", n] for n in NAMES])
```

```python
import functools
import math

import jax
import jax.numpy as jnp
import numpy as np
from jax import lax
from jax.experimental import pallas as pl
from jax.experimental.pallas import tpu as pltpu

F32 = jnp.float32
BF16 = jnp.bfloat16

N_DEV = 8
N_CHIP = 4
D = 1024
DEPTH = 2
ML_H = 4
ML_W = 512
ML_DH = 128
ML_L = 64
ML_CONV = 4
SW_DH = 64
SW_W = 512
SW_H = 8
SW_G = 4
SW_KVW = 128
BLK = 128
REL_B = 32
REL_MAXD = 128
XA_H = 4
XA_DH = 256
DFF = 2816
NB_FF = DFF // 128
FFN_CONV = 3
ALPHA = (2.0 * DEPTH) ** 0.25
EPS = 1e-5
N_IN = 2824
NP_IN = 3072
ML_GW = 2 * ML_W + 128
SW_GW = SW_W + 2 * SW_KVW
ADAM_LR = 0.001
ADAM_B1 = 0.9
ADAM_B2 = 0.999
ADAM_EPS = 1e-08
ADAM_WD = 0.01
ADAM_STEP = 10
VMEM_LIMIT = 56 * 1024 * 1024
MESH = pl.DeviceIdType.MESH

NN = ((1,), (0,))
NT = ((1,), (1,))
TN = ((0,), (0,))


def _dg(a, b, dn):
    if a.ndim == 3:
        dims = (((dn[0][0] + 1,), (dn[1][0] + 1,)), ((0,), (0,)))
    else:
        dims = (dn, ((), ()))
    return lax.dot_general(a.astype(BF16), b.astype(BF16), dims, preferred_element_type=F32)


@jax.custom_vjp
def dot_nn(a, b):
    return _dg(a, b, NN)


dot_nn.defvjp(lambda a, b: (_dg(a, b, NN), (a, b)), lambda r, g: (_dg(g, r[1], NT), _dg(r[0], g, TN)))


@jax.custom_vjp
def dot_nt(a, b):
    return _dg(a, b, NT)


dot_nt.defvjp(lambda a, b: (_dg(a, b, NT), (a, b)), lambda r, g: (_dg(g, r[1], NN), _dg(g, r[0], TN)))


@jax.custom_vjp
def dot_tn(a, b):
    return _dg(a, b, TN)


dot_tn.defvjp(lambda a, b: (_dg(a, b, TN), (a, b)), lambda r, g: (_dg(r[1], g, NT), _dg(r[0], g, NN)))


def _params(sem=None):
    return pltpu.CompilerParams(dimension_semantics=sem, vmem_limit_bytes=VMEM_LIMIT)


def _sds(shape, dtype):
    return jax.ShapeDtypeStruct(tuple(shape), dtype)


def _mm(a, b, *, trans_b=False, out_dtype=F32, add=None, add_scale=1.0, tm=1024, tn=512, name):
    a_list = list(a) if isinstance(a, (list, tuple)) else [a]
    M = a_list[0].shape[0]
    N = b.shape[0] if trans_b else b.shape[1]
    tm = min(tm, M)
    tn = next(t for t in (tn, 384, 256, 128) if N % t == 0)
    assert M % tm == 0
    Ka = a_list[0].shape[1]
    assert all(t.shape[1] == Ka for t in a_list)
    tk = next(t for t in (Ka, 1408, 1024) if Ka % t == 0 and t <= 1408)
    na, npa = len(a_list), Ka // tk
    nk = na * npa
    has_add = add is not None

    def body(*refs):
        a_refs, b_ref = refs[:na], refs[na]
        add_ref = refs[na + 1] if has_add else None
        o_ref, acc_ref = refs[na + 1 + has_add], refs[na + 2 + has_add]
        k = pl.program_id(2)

        def finish(r):
            if has_add:
                r = r + add_scale * add_ref[...].astype(F32)
            o_ref[...] = r.astype(out_dtype)

        for t, a_ref in enumerate(a_refs):
            def step(a_ref=a_ref):
                p = _dg(a_ref[...], b_ref[...], NT if trans_b else NN)
                if nk == 1:
                    finish(p)
                    return

                @pl.when(k == 0)
                def _():
                    acc_ref[...] = p

                @pl.when((k > 0) & (k < nk - 1))
                def _():
                    acc_ref[...] += p

                @pl.when(k == nk - 1)
                def _():
                    finish(acc_ref[...] + p)

            if na == 1:
                step()
            else:
                pl.when((k >= t * npa) & (k < (t + 1) * npa))(step)

    in_specs = [pl.BlockSpec((tm, tk), lambda i, j, k, t=t: (i, jnp.clip(k - t * npa, 0, npa - 1))) for t in range(na)]
    in_specs.append(pl.BlockSpec((tn, tk), lambda i, j, k: (j, k)) if trans_b else pl.BlockSpec((tk, tn), lambda i, j, k: (k, j)))
    args = a_list + [b]
    if has_add:
        in_specs.append(pl.BlockSpec((tm, tn), lambda i, j, k: (i, j)))
        args.append(add)
    return pl.pallas_call(
        body, name=name, grid=(M // tm, N // tn, nk), in_specs=in_specs,
        out_specs=pl.BlockSpec((tm, tn), lambda i, j, k: (i, j)), out_shape=_sds((M, N), out_dtype),
        scratch_shapes=[pltpu.VMEM((tm, tn) if nk > 1 else (8, 128), F32)],
        compiler_params=_params(("parallel", "parallel", "arbitrary")))(*args)


def _mm_tn(a, g, *, name):
    S, K = a.shape
    N = g.shape[1]
    tk = K if K <= 1024 else K // 2
    tn = next(t for t in range(1536, 0, -128) if N % t == 0)
    ts = min(512, S)
    ns = S // ts
    assert K % tk == 0 and S % ts == 0

    def body(a_ref, g_ref, o_ref):
        s = pl.program_id(2)
        p = _dg(a_ref[...], g_ref[...], TN)

        @pl.when(s == 0)
        def _():
            o_ref[...] = p

        @pl.when(s > 0)
        def _():
            o_ref[...] += p

    return pl.pallas_call(
        body, name=name, grid=(K // tk, N // tn, ns),
        in_specs=[pl.BlockSpec((ts, tk), lambda i, j, s: (s, i)), pl.BlockSpec((ts, tn), lambda i, j, s: (s, j))],
        out_specs=pl.BlockSpec((tk, tn), lambda i, j, s: (i, j)), out_shape=_sds((K, N), F32),
        compiler_params=_params(("parallel", "parallel", "arbitrary")))(a, g)


def _mm_res_ln(a_list, w, resid, gam, bet, *, name):
    M, Ka = a_list[0].shape
    na = len(a_list)
    assert w.shape[0] == na * Ka
    tm = 256

    def body(*refs):
        a_refs, w_refs = refs[:na], refs[na:2 * na]
        r_ref, g_ref, b_ref, y_ref, yb_ref, z_ref = refs[2 * na:]
        z = ALPHA * r_ref[...]
        for a_ref, w_ref in zip(a_refs, w_refs):
            z = z + _dg(a_ref[...], w_ref[...], NN)
        mu = jnp.mean(z, axis=1, keepdims=True)
        zc = z - mu
        var = jnp.mean(zc * zc, axis=1, keepdims=True)
        y = zc * lax.rsqrt(var + EPS) * g_ref[...] + b_ref[...]
        y_ref[...] = y
        yb_ref[...] = y.astype(BF16)
        z_ref[...] = z

    row = pl.BlockSpec((tm, D), lambda i: (i, 0))
    vec = pl.BlockSpec((1, D), lambda i: (0, 0))
    a_specs = [pl.BlockSpec((tm, Ka), lambda i: (i, 0)) for _ in a_list]
    w_specs = [pl.BlockSpec((Ka, D), lambda i, t=t: (t, 0)) for t in range(na)]
    return pl.pallas_call(
        body, name=name, grid=(M // tm,), in_specs=a_specs + w_specs + [row, vec, vec],
        out_specs=[row, row, row], out_shape=[_sds((M, D), F32), _sds((M, D), BF16), _sds((M, D), F32)],
        compiler_params=_params(("parallel",)))(*a_list, *([w] * na), resid, gam, bet)


def _ln_bwd(dy, z, gam, *, name):
    M = dy.shape[0]
    tm = 512

    def body(dy_ref, z_ref, g_ref, dz_ref, dzb_ref, dg_ref, db_ref):
        i = pl.program_id(0)
        z = z_ref[...]
        dy_ = dy_ref[...]
        mu = jnp.mean(z, axis=1, keepdims=True)
        zc = z - mu
        var = jnp.mean(zc * zc, axis=1, keepdims=True)
        rstd = lax.rsqrt(var + EPS)
        xh = zc * rstd
        dxh = dy_ * g_ref[...]
        m1 = jnp.mean(dxh, axis=1, keepdims=True)
        m2 = jnp.mean(dxh * xh, axis=1, keepdims=True)
        dz = rstd * (dxh - m1 - xh * m2)
        dz_ref[...] = dz
        dzb_ref[...] = dz.astype(BF16)
        pg = jnp.sum(dy_ * xh, axis=0, keepdims=True)
        pb = jnp.sum(dy_, axis=0, keepdims=True)

        @pl.when(i == 0)
        def _():
            dg_ref[...] = pg
            db_ref[...] = pb

        @pl.when(i > 0)
        def _():
            dg_ref[...] += pg
            db_ref[...] += pb

    row = pl.BlockSpec((tm, D), lambda i: (i, 0))
    vec = pl.BlockSpec((1, D), lambda i: (0, 0))
    return pl.pallas_call(
        body, name=name, grid=(M // tm,), in_specs=[row, row, vec], out_specs=[row, row, vec, vec],
        out_shape=[_sds((M, D), F32), _sds((M, D), BF16), _sds((1, D), F32), _sds((1, D), F32)],
        compiler_params=_params(("arbitrary",)))(dy, z, gam)


def _shift_down(x, d):
    if d == 0:
        return x
    rows = lax.broadcasted_iota(jnp.int32, x.shape, 0)
    return jnp.where(rows >= d, pltpu.roll(x, d, 0), 0.0)


def _shift_up(x, d):
    if d == 0:
        return x
    S = x.shape[0]
    rows = lax.broadcasted_iota(jnp.int32, x.shape, 0)
    return jnp.where(rows < S - d, pltpu.roll(x, S - d, 0), 0.0)


def _conv(x, w_ref, b_ref, cs, K):
    y = b_ref[:, cs]
    for j in range(K):
        y = y + _shift_down(x, K - 1 - j) * w_ref[j:j + 1, cs]
    return y


def _conv_bwd(dy, x, w_ref, dw_ref, db_ref, cs, K):
    dx = jnp.zeros_like(x)
    for j in range(K):
        dx = dx + _shift_up(dy, K - 1 - j) * w_ref[j:j + 1, cs]
        dw_ref[j:j + 1, cs] = jnp.sum(dy * _shift_down(x, K - 1 - j), axis=0, keepdims=True)
    db_ref[:, cs] = jnp.sum(dy, axis=0, keepdims=True)
    return dx


ALL = slice(None)


def _silu_conv_fwd(proj, cw, cb, *, name):
    S = proj.shape[0]

    def body(x_ref, w_ref, b_ref, o_ref):
        o_ref[...] = jax.nn.silu(_conv(x_ref[...], w_ref, b_ref, ALL, ML_CONV))

    col = pl.BlockSpec((S, 128), lambda j: (0, j))
    return pl.pallas_call(
        body, name=name, grid=(8,),
        in_specs=[col, pl.BlockSpec((ML_CONV, 128), lambda j: (0, j)), pl.BlockSpec((1, 128), lambda j: (0, j))],
        out_specs=col, out_shape=_sds((S, 2 * ML_W), F32), compiler_params=_params(("parallel",)))(proj, cw, cb)


def _silu_conv_bwd(dqk, proj, cw, cb, *, name):
    S = proj.shape[0]

    def body(d_ref, x_ref, w_ref, b_ref, dx_ref, dw_ref, db_ref):
        x = x_ref[...]
        y = _conv(x, w_ref, b_ref, ALL, ML_CONV)
        dy = jax.vjp(jax.nn.silu, y)[1](d_ref[...])[0]
        dx_ref[...] = _conv_bwd(dy, x, w_ref, dw_ref, db_ref, ALL, ML_CONV).astype(BF16)

    col = pl.BlockSpec((S, 128), lambda j: (0, j))
    wsp = pl.BlockSpec((ML_CONV, 128), lambda j: (0, j))
    bsp = pl.BlockSpec((1, 128), lambda j: (0, j))
    return pl.pallas_call(
        body, name=name, grid=(8,), in_specs=[col, col, wsp, bsp], out_specs=[col, wsp, bsp],
        out_shape=[_sds((S, 2 * ML_W), BF16), _sds((ML_CONV, 2 * ML_W), F32), _sds((1, 2 * ML_W), F32)],
        compiler_params=_params(("parallel",)))(dqk, proj, cw, cb)


def _gate(ug, uv):
    return jax.nn.gelu(ug) * uv


def _ffn_specs(S):
    return (pl.BlockSpec((S, 128), lambda j: (0, j)), pl.BlockSpec((S, 128), lambda j: (0, j + NB_FF)),
            pl.BlockSpec((FFN_CONV, 128), lambda j: (0, j)), pl.BlockSpec((FFN_CONV, 128), lambda j: (0, j + NB_FF)),
            pl.BlockSpec((1, 128), lambda j: (0, j)), pl.BlockSpec((1, 128), lambda j: (0, j + NB_FF)))


def _ffn_gate_fwd(up, cw, cb, *, name):
    S = up.shape[0]
    cg, cv, wg, wv, bg, bv = _ffn_specs(S)

    def body(g_ref, v_ref, wg_ref, wv_ref, bg_ref, bv_ref, o_ref):
        ug = _conv(g_ref[...], wg_ref, bg_ref, ALL, FFN_CONV)
        uv = _conv(v_ref[...], wv_ref, bv_ref, ALL, FFN_CONV)
        o_ref[...] = _gate(ug, uv).astype(BF16)

    return pl.pallas_call(
        body, name=name, grid=(NB_FF,), in_specs=[cg, cv, wg, wv, bg, bv], out_specs=cg,
        out_shape=_sds((S, DFF), BF16), compiler_params=_params(("parallel",)))(up, up, cw, cw, cb, cb)


def _ffn_gate_bwd(da, up, cw, cb, *, name):
    S = up.shape[0]
    cg, cv, wg, wv, bg, bv = _ffn_specs(S)

    def body(da_ref, g_ref, v_ref, wg_ref, wv_ref, bg_ref, bv_ref, dxg_ref, dxv_ref, dwg_ref, dwv_ref, dbg_ref, dbv_ref):
        xg, xv = g_ref[...], v_ref[...]
        ug = _conv(xg, wg_ref, bg_ref, ALL, FFN_CONV)
        uv = _conv(xv, wv_ref, bv_ref, ALL, FFN_CONV)
        dug, duv = jax.vjp(_gate, ug, uv)[1](da_ref[...])
        dxg_ref[...] = _conv_bwd(dug, xg, wg_ref, dwg_ref, dbg_ref, ALL, FFN_CONV).astype(BF16)
        dxv_ref[...] = _conv_bwd(duv, xv, wv_ref, dwv_ref, dbv_ref, ALL, FFN_CONV).astype(BF16)

    half = _sds((S, DFF), BF16)
    dxg, dxv, dwg, dwv, dbg, dbv = pl.pallas_call(
        body, name=name, grid=(NB_FF,), in_specs=[cg, cg, cv, wg, wv, bg, bv], out_specs=[cg, cg, wg, wg, bg, bg],
        out_shape=[half, half, _sds((FFN_CONV, DFF), F32), _sds((FFN_CONV, DFF), F32), _sds((1, DFF), F32), _sds((1, DFF), F32)],
        compiler_params=_params(("parallel",)))(da, up, up, cw, cw, cb, cb)
    return dxg, dxv, jnp.concatenate([dwg, dwv], axis=1), jnp.concatenate([dbg, dbv], axis=1)


def _log_sigmoid(x):
    return jnp.minimum(x, 0.0) - jnp.log1p(jnp.exp(-jnp.abs(x)))


@jax.custom_vjp
def _clamp_div(num, den, floor, shift):
    return num / jnp.maximum(jnp.abs(den), floor)


def _clamp_div_fwd(num, den, floor, shift):
    out = num / jnp.maximum(jnp.abs(den), floor)
    return out, (den, floor, out)


def _clamp_div_bwd(res, g):
    den, floor, out = res
    active = jnp.abs(den) < floor
    dinv = jnp.maximum(jnp.abs(den), floor)
    go = jnp.sum(g * out, axis=-1, keepdims=True)
    ddiv = -go / dinv
    return (g / dinv, jnp.where(active, 0.0, ddiv * jnp.sign(den)), jnp.where(active, ddiv, 0.0),
            jnp.sum(jnp.where(active, go, 0.0), axis=-2, keepdims=True))


_clamp_div.defvjp(_clamp_div_fwd, _clamp_div_bwd)


def _ml_heads(q, k, v, o_pre, gates, gbias, C, n, ng, m, shift):
    H, L, _ = q.shape
    lane1 = lax.broadcasted_iota(jnp.int32, (1, 128), 1)
    gz = gates + jnp.where(lane1 < ML_H, lax.stop_gradient(gbias), gbias)
    gz = jnp.broadcast_to(gz[None], (H, L, 128))
    hid = lax.broadcasted_iota(jnp.int32, (H, L, 128), 0)
    lane = lax.broadcasted_iota(jnp.int32, (H, L, 128), 2)
    ig = jnp.sum(jnp.where(lane == hid, gz, 0.0), axis=2, keepdims=True)
    lf = _log_sigmoid(jnp.sum(jnp.where(lane == ML_H + hid, gz, 0.0), axis=2, keepdims=True))
    r = lax.broadcasted_iota(jnp.int32, (H, L, L), 1)
    c = lax.broadcasted_iota(jnp.int32, (H, L, L), 2)
    eye, tril = r == c, r >= c

    def to_row(col):
        return jnp.sum(jnp.where(eye, col, 0.0), axis=1, keepdims=True)

    b_col = jnp.sum(jnp.where(tril, to_row(lf), 0.0), axis=2, keepdims=True)
    Dm = jnp.where(tril, b_col - to_row(b_col) + to_row(ig), -jnp.inf)
    inter = b_col + m
    m_t = jnp.maximum(inter, jnp.max(Dm, axis=2, keepdims=True))
    w_inter = jnp.exp(inter - m_t)
    ks = k * (ML_DH ** -0.5)
    s = dot_nt(q, ks) * jnp.exp(Dm - m_t)
    num = w_inter * dot_nn(q, C) + dot_nn(s, v)
    den = w_inter * jnp.sum(q * n, axis=2, keepdims=True) + jnp.sum(s, axis=2, keepdims=True)
    h = _clamp_div(num, den, jnp.exp(-m_t), shift)
    g = jnp.sum(lf, axis=1, keepdims=True)
    a = g - b_col + ig
    m_new = jnp.maximum(g + m, jnp.max(a, axis=1, keepdims=True))
    decay = jnp.exp(g + m - m_new)
    wk = jnp.exp(a - m_new)
    C_new = decay * C + dot_tn(ks * wk, v)
    n_new = decay * n + jnp.sum(wk * ks, axis=1, keepdims=True)
    mu = jnp.mean(h, axis=2, keepdims=True)
    hc = h - mu
    var = jnp.mean(hc * hc, axis=2, keepdims=True)
    out = jax.nn.sigmoid(o_pre) * (hc * lax.rsqrt(var + EPS) * ng)
    return out, C_new, n_new, m_new


def _hs(h, off=0):
    return slice(off + h * ML_DH, off + (h + 1) * ML_DH)


def _heads(ref, off=0):
    return jnp.stack([ref[:, _hs(h, off)] for h in range(ML_H)])


def _mlstm_fwd(qk, proj, gbias, ng, *, name):
    S = qk.shape[0]
    nc = S // ML_L

    def body(q_ref, k_ref, v_ref, o_ref, g_ref, gb_ref, ng_ref, h_ref, cs_ref, ns_ref, ms_ref, c_s, n_s, m_s):
        @pl.when(pl.program_id(0) == 0)
        def _():
            c_s[...] = jnp.zeros_like(c_s)
            n_s[...] = jnp.zeros_like(n_s)
            m_s[...] = jnp.zeros_like(m_s)

        C, n = c_s[...], n_s[...]
        cs_ref[0] = C
        ns_ref[0] = n
        ms_ref[0] = m_s[...]
        out, C2, n2, m2 = _ml_heads(_heads(q_ref), _heads(k_ref), _heads(v_ref), _heads(o_ref), g_ref[...], gb_ref[...], C, n,
                                    _heads(ng_ref), m_s[:, :, 0:1], jnp.zeros((ML_H, 1, 1), F32))
        for h in range(ML_H):
            h_ref[:, _hs(h)] = out[h].astype(BF16)
        c_s[...] = C2
        n_s[...] = n2
        m_s[...] = jnp.broadcast_to(m2, (ML_H, 1, 128))

    def w(j):
        return pl.BlockSpec((ML_L, ML_W), lambda c, j=j: (c, j))

    return pl.pallas_call(
        body, name=name, grid=(nc,),
        in_specs=[w(0), w(1), w(2), w(3), pl.BlockSpec((ML_L, 128), lambda c: (c, 22)),
                  pl.BlockSpec((1, 128), lambda c: (0, 0)), pl.BlockSpec((1, ML_W), lambda c: (0, 0))],
        out_specs=[w(0), pl.BlockSpec((1, ML_H, ML_DH, ML_DH), lambda c: (c, 0, 0, 0)),
                   pl.BlockSpec((1, ML_H, 1, 128), lambda c: (c, 0, 0, 0)), pl.BlockSpec((1, ML_H, 1, 128), lambda c: (c, 0, 0, 0))],
        out_shape=[_sds((S, ML_W), BF16), _sds((nc, ML_H, ML_DH, ML_DH), F32), _sds((nc, ML_H, 1, 128), F32),
                   _sds((nc, ML_H, 1, 128), F32)],
        scratch_shapes=[pltpu.VMEM((ML_H, ML_DH, ML_DH), F32), pltpu.VMEM((ML_H, 1, 128), F32), pltpu.VMEM((ML_H, 1, 128), F32)],
        compiler_params=_params(("arbitrary",)))(qk, qk, proj, proj, proj, gbias, ng)


def _mlstm_bwd(dh, qk, proj, gbias, ng, cs, ns, ms, *, name):
    S = qk.shape[0]
    nc = S // ML_L

    def body(dh_ref, q_ref, k_ref, v_ref, o_ref, g_ref, gb_ref, ng_ref, cs_ref, ns_ref, ms_ref,
             dqk_ref, dml_ref, dgb_ref, dng_ref, dc_s, dn_s, dm_s):
        @pl.when(pl.program_id(0) == 0)
        def _():
            dc_s[...] = jnp.zeros_like(dc_s)
            dn_s[...] = jnp.zeros_like(dn_s)
            dm_s[...] = jnp.zeros_like(dm_s)
            dgb_ref[...] = jnp.zeros_like(dgb_ref)
            dng_ref[...] = jnp.zeros_like(dng_ref)

        _, vjp = jax.vjp(_ml_heads, _heads(q_ref), _heads(k_ref), _heads(v_ref), _heads(o_ref), g_ref[...], gb_ref[...],
                         cs_ref[0], ns_ref[0], _heads(ng_ref), ms_ref[0][:, :, 0:1], jnp.zeros((ML_H, 1, 1), F32))
        dq, dk, dv, do, dgates, dgb, dC, dn, dng, dm, dshift = vjp((_heads(dh_ref), dc_s[...], dn_s[...], dm_s[:, :, 0:1]))
        lane1 = lax.broadcasted_iota(jnp.int32, (1, 128), 1)
        for h in range(ML_H):
            dqk_ref[:, _hs(h)] = dq[h]
            dqk_ref[:, _hs(h, ML_W)] = dk[h]
            dml_ref[:, _hs(h)] = dv[h].astype(BF16)
            dml_ref[:, _hs(h, ML_W)] = do[h].astype(BF16)
            dng_ref[:, _hs(h)] += dng[h]
            dgb = jnp.where(lane1 == h, dshift[h], dgb)
        dc_s[...] = dC
        dn_s[...] = dn
        dm_s[...] = jnp.broadcast_to(dm, (ML_H, 1, 128))
        dml_ref[:, 2 * ML_W:] = dgates.astype(BF16)
        dgb_ref[...] += dgb

    def w(j):
        return pl.BlockSpec((ML_L, ML_W), lambda c, j=j: (nc - 1 - c, j))

    vec = pl.BlockSpec((1, 128), lambda c: (0, 0))
    vecw = pl.BlockSpec((1, ML_W), lambda c: (0, 0))
    gsp = pl.BlockSpec((ML_L, 128), lambda c: (nc - 1 - c, 22))
    st = pl.BlockSpec((1, ML_H, 1, 128), lambda c: (nc - 1 - c, 0, 0, 0))
    return pl.pallas_call(
        body, name=name, grid=(nc,),
        in_specs=[w(0), w(0), w(1), w(2), w(3), gsp, vec, vecw,
                  pl.BlockSpec((1, ML_H, ML_DH, ML_DH), lambda c: (nc - 1 - c, 0, 0, 0)), st, st],
        out_specs=[pl.BlockSpec((ML_L, 2 * ML_W), lambda c: (nc - 1 - c, 0)), pl.BlockSpec((ML_L, ML_GW), lambda c: (nc - 1 - c, 0)),
                   vec, vecw],
        out_shape=[_sds((S, 2 * ML_W), F32), _sds((S, ML_GW), BF16), _sds((1, 128), F32), _sds((1, ML_W), F32)],
        scratch_shapes=[pltpu.VMEM((ML_H, ML_DH, ML_DH), F32), pltpu.VMEM((ML_H, 1, 128), F32), pltpu.VMEM((ML_H, 1, 128), F32)],
        compiler_params=_params(("arbitrary",)))(dh, qk, qk, proj, proj, proj, gbias, ng, cs, ns, ms)


def _t5_buckets():
    r = np.arange(BLK)[:, None]
    c = np.arange(2 * BLK)[None, :]
    n = np.maximum(BLK + r - c, 0)
    max_exact = REL_B // 2
    nf = np.maximum(n, 1).astype(np.float32)
    large = max_exact + (np.log(nf / np.float32(max_exact)) / np.float32(math.log(REL_MAXD / max_exact))
                         * np.float32(REL_B - max_exact)).astype(np.int32)
    large = np.minimum(large, REL_B - 1)
    return np.where(n < max_exact, n, large).astype(np.int32)


def _bias_table(rel_bias, bucket, *, name):
    def body(rb_ref, bk_ref, o_ref):
        bk = bk_ref[...]
        for h in range(SW_H):
            acc = jnp.zeros((BLK, 2 * BLK), F32)
            for b in range(REL_B):
                acc = jnp.where(bk == b, rb_ref[b, h], acc)
            o_ref[h] = acc

    return pl.pallas_call(
        body, name=name, in_specs=[pl.BlockSpec(memory_space=pltpu.SMEM), pl.BlockSpec(memory_space=pltpu.VMEM)],
        out_specs=pl.BlockSpec(memory_space=pltpu.VMEM), out_shape=_sds((SW_H, BLK, 2 * BLK), F32),
        compiler_params=_params())(rel_bias, bucket)


def _bias_table_bwd(dbias_list, bucket, *, name):
    nl = len(dbias_list)

    def body(*refs):
        d_refs, bk_ref, o_ref = refs[:nl], refs[nl], refs[nl + 1]
        bk = bk_ref[...]
        rows = lax.broadcasted_iota(jnp.int32, (REL_B, 128), 0)
        lanes = lax.broadcasted_iota(jnp.int32, (REL_B, 128), 1)
        acc = jnp.zeros((REL_B, 128), F32)
        for h in range(SW_H):
            d = d_refs[0][h]
            for d_ref in d_refs[1:]:
                d = d + d_ref[h]
            for b in range(REL_B):
                t = jnp.sum(jnp.sum(jnp.where(bk == b, d, 0.0), axis=0, keepdims=True), axis=1, keepdims=True)
                acc = jnp.where((rows == b) & (lanes == h), t, acc)
        o_ref[...] = acc

    vm = pl.BlockSpec(memory_space=pltpu.VMEM)
    return pl.pallas_call(
        body, name=name, in_specs=[vm] * (nl + 1), out_specs=vm, out_shape=_sds((REL_B, 128), F32),
        compiler_params=_params())(*dbias_list, bucket)


def _swa_heads(q, kp, kc, vp, vc, bp, bc, sinks, has_prev):
    def rep(t):
        return jnp.concatenate([t[g:g + 1] for g in range(SW_H // SW_G) for _ in range(SW_G)], axis=0)

    r = lax.broadcasted_iota(jnp.int32, (SW_H, BLK, BLK), 1)
    c = lax.broadcasted_iota(jnp.int32, (SW_H, BLK, BLK), 2)
    hid = lax.broadcasted_iota(jnp.int32, (SW_H, 1, 128), 0)
    lane = lax.broadcasted_iota(jnp.int32, (SW_H, 1, 128), 2)
    sink = jnp.sum(jnp.where(lane == hid, jnp.broadcast_to(sinks[None], (SW_H, 1, 128)), 0.0), axis=2, keepdims=True)
    lp = jnp.where((c > r) & has_prev, dot_nt(q, rep(kp)) * (SW_DH ** -0.5) + bp, -jnp.inf)
    lc = jnp.where(c <= r, dot_nt(q, rep(kc)) * (SW_DH ** -0.5) + bc, -jnp.inf)
    mx = jnp.maximum(jnp.maximum(jnp.max(lp, axis=2, keepdims=True), jnp.max(lc, axis=2, keepdims=True)), sink)
    mx = lax.stop_gradient(mx)
    pp, pc = jnp.exp(lp - mx), jnp.exp(lc - mx)
    den = jnp.sum(pp, axis=2, keepdims=True) + jnp.sum(pc, axis=2, keepdims=True) + jnp.exp(sink - mx)
    return dot_nn(pp / den, rep(vp)) + dot_nn(pc / den, rep(vc))


def _qs(h, off=0):
    return slice(off + h * SW_DH, off + (h + 1) * SW_DH)


def _split(ref, n):
    return jnp.stack([ref[:, _qs(h)] for h in range(n)])


def _swa_fwd(proj, bias, sinks, *, name):
    S = proj.shape[0]
    nb = S // BLK
    nkv = SW_H // SW_G

    def body(q_ref, kp_ref, kc_ref, vp_ref, vc_ref, b_ref, s_ref, o_ref):
        out = _swa_heads(_split(q_ref, SW_H), _split(kp_ref, nkv), _split(kc_ref, nkv), _split(vp_ref, nkv), _split(vc_ref, nkv),
                         b_ref[:, :, :BLK], b_ref[:, :, BLK:], s_ref[...], pl.program_id(0) > 0)
        for h in range(SW_H):
            o_ref[:, _qs(h)] = out[h].astype(BF16)

    def cur(j):
        return pl.BlockSpec((BLK, 128), lambda n, j=j: (n, j))

    def prev(j):
        return pl.BlockSpec((BLK, 128), lambda n, j=j: (jnp.maximum(n - 1, 0), j))

    return pl.pallas_call(
        body, name=name, grid=(nb,),
        in_specs=[pl.BlockSpec((BLK, SW_W), lambda n: (n, 4)), prev(20), cur(20), prev(21), cur(21),
                  pl.BlockSpec((SW_H, BLK, 2 * BLK), lambda n: (0, 0, 0)), pl.BlockSpec((1, 128), lambda n: (0, 0))],
        out_specs=pl.BlockSpec((BLK, SW_W), lambda n: (n, 0)), out_shape=_sds((S, SW_W), BF16),
        compiler_params=_params(("parallel",)))(proj, proj, proj, proj, proj, bias, sinks)


def _swa_bwd(dh, proj, bias, sinks, *, name):
    S = proj.shape[0]
    nb = S // BLK

    nkv = SW_H // SW_G

    def body(dh_ref, q_ref, kp_ref, kc_ref, vp_ref, vc_ref, b_ref, s_ref, dsw_ref, db_ref, ds_ref, ck_s, cv_s):
        i = pl.program_id(0)

        @pl.when(i == 0)
        def _():
            ck_s[...] = jnp.zeros_like(ck_s)
            cv_s[...] = jnp.zeros_like(cv_s)
            db_ref[...] = jnp.zeros_like(db_ref)
            ds_ref[...] = jnp.zeros_like(ds_ref)

        f = functools.partial(_swa_heads, has_prev=i < nb - 1)
        _, vjp = jax.vjp(f, _split(q_ref, SW_H), _split(kp_ref, nkv), _split(kc_ref, nkv), _split(vp_ref, nkv),
                         _split(vc_ref, nkv), b_ref[:, :, :BLK], b_ref[:, :, BLK:], s_ref[...])
        dq, dkp, dkc, dvp, dvc, dbp, dbc, ds = vjp(_split(dh_ref, SW_H))
        for h in range(SW_H):
            dsw_ref[:, _qs(h)] = dq[h].astype(BF16)
        for g in range(nkv):
            dsw_ref[:, _qs(g, SW_W)] = (dkc[g] + ck_s[:, _qs(g)]).astype(BF16)
            dsw_ref[:, _qs(g, SW_W + SW_KVW)] = (dvc[g] + cv_s[:, _qs(g)]).astype(BF16)
            ck_s[:, _qs(g)] = dkp[g]
            cv_s[:, _qs(g)] = dvp[g]
        db_ref[:, :, :BLK] += dbp
        db_ref[:, :, BLK:] += dbc
        ds_ref[...] += ds

    def cur(j):
        return pl.BlockSpec((BLK, 128), lambda i, j=j: (nb - 1 - i, j))

    def prev(j):
        return pl.BlockSpec((BLK, 128), lambda i, j=j: (jnp.maximum(nb - 2 - i, 0), j))

    bsp = pl.BlockSpec((SW_H, BLK, 2 * BLK), lambda i: (0, 0, 0))
    vec = pl.BlockSpec((1, 128), lambda i: (0, 0))
    return pl.pallas_call(
        body, name=name, grid=(nb,),
        in_specs=[pl.BlockSpec((BLK, SW_W), lambda i: (nb - 1 - i, 1)), pl.BlockSpec((BLK, SW_W), lambda i: (nb - 1 - i, 4)),
                  prev(20), cur(20), prev(21), cur(21), bsp, vec],
        out_specs=[pl.BlockSpec((BLK, SW_GW), lambda i: (nb - 1 - i, 0)), bsp, vec],
        out_shape=[_sds((S, SW_GW), BF16), _sds((SW_H, BLK, 2 * BLK), F32), _sds((1, 128), F32)],
        scratch_shapes=[pltpu.VMEM((BLK, 128), F32)] * 2,
        compiler_params=_params(("arbitrary",)))(dh, proj, proj, proj, proj, proj, bias, sinks)


XA_TM = 512


def _xa_head(qh, kh, vh):
    logits = dot_nt(qh, kh) * (XA_DH ** -0.5)
    mx = lax.stop_gradient(jnp.max(logits, axis=1, keepdims=True))
    e = jnp.exp(logits - mx)
    return dot_nn(e / jnp.sum(e, axis=1, keepdims=True), vh)


def _xs(h, off=0):
    return slice(off + h * XA_DH, off + (h + 1) * XA_DH)


def _xattn_fwd(q, kv, *, name):
    S = q.shape[0]
    M = kv.shape[0]

    def body(q_ref, kv_ref, o_ref):
        for h in range(XA_H):
            o_ref[:, _xs(h)] = _xa_head(q_ref[:, _xs(h)], kv_ref[:, _xs(h)], kv_ref[:, _xs(h, D)]).astype(BF16)

    row = pl.BlockSpec((XA_TM, D), lambda i: (i, 0))
    return pl.pallas_call(
        body, name=name, grid=(S // XA_TM,), in_specs=[row, pl.BlockSpec((M, 2 * D), lambda i: (0, 0))], out_specs=row,
        out_shape=_sds((S, D), BF16), compiler_params=_params(("parallel",)))(q, kv)


def _xattn_bwd(do, q, kv, *, name):
    S = q.shape[0]
    M = kv.shape[0]

    def body(do_ref, q_ref, kv_ref, dq_ref, dkv_ref):
        @pl.when(pl.program_id(0) == 0)
        def _():
            dkv_ref[...] = jnp.zeros_like(dkv_ref)

        for h in range(XA_H):
            _, vjp = jax.vjp(_xa_head, q_ref[:, _xs(h)], kv_ref[:, _xs(h)], kv_ref[:, _xs(h, D)])
            dq, dk, dv = vjp(do_ref[:, _xs(h)])
            dq_ref[:, _xs(h)] = dq.astype(BF16)
            dkv_ref[:, _xs(h)] += dk
            dkv_ref[:, _xs(h, D)] += dv

    row = pl.BlockSpec((XA_TM, D), lambda i: (i, 0))
    full = pl.BlockSpec((M, 2 * D), lambda i: (0, 0))
    return pl.pallas_call(
        body, name=name, grid=(S // XA_TM,), in_specs=[row, row, full], out_specs=[row, full],
        out_shape=[_sds((S, D), BF16), _sds((M, 2 * D), F32)], compiler_params=_params(("arbitrary",)))(do, q, kv)


def _loss_head(y, tgt, *, name):
    S = y.shape[0]
    tm = 512

    def body(y_ref, t_ref, l_ref, dy_ref):
        e = y_ref[...] - t_ref[...]
        dy_ref[...] = e * (1.0 / D)
        part = 0.5 * jnp.sum(jnp.sum(e * e, axis=1, keepdims=True) * (1.0 / D), axis=0, keepdims=True)

        @pl.when(pl.program_id(0) == 0)
        def _():
            l_ref[...] = jnp.broadcast_to(part, (8, 128))

        @pl.when(pl.program_id(0) > 0)
        def _():
            l_ref[...] += jnp.broadcast_to(part, (8, 128))

    row = pl.BlockSpec((tm, D), lambda i: (i, 0))
    return pl.pallas_call(
        body, name=name, grid=(S // tm,), in_specs=[row, row], out_specs=[pl.BlockSpec((8, 128), lambda i: (0, 0)), row],
        out_shape=[_sds((8, 128), F32), _sds((S, D), F32)], compiler_params=_params(("arbitrary",)))(y, tgt)


ANY = pl.BlockSpec(memory_space=pl.ANY)


def _place():
    x, y, c = lax.axis_index("x"), lax.axis_index("y"), lax.axis_index("c")
    chips = [(1 - x, y), (x, 1 - y), (1 - x, 1 - y)]
    return x, y, c, chips


def _gather(arrs, *, name):
    n = len(arrs)

    def body(*refs):
        srcs, outs = refs[:n], refs[n:2 * n]
        send_sems, recv_sems, local_sems = refs[2 * n:]
        x, y, c, chips = _place()
        me, sib = (x, y, c), (x, y, 1 - c)

        def idx(p):
            return 4 * p[0] + 2 * p[1] + p[2]

        def copy(i, k, block, to, from_src=False):
            return pltpu.make_async_remote_copy(
                src_ref=srcs[i] if from_src else outs[i].at[idx(block)], dst_ref=outs[i].at[idx(block)],
                send_sem=send_sems.at[7 * i + k], recv_sem=recv_sems.at[7 * i + k], device_id=to, device_id_type=MESH)

        local = [pltpu.make_async_copy(srcs[i], outs[i].at[idx(me)], local_sems.at[i]) for i in range(n)]
        for cp in local:
            cp.start()
        first = []
        for i in range(n):
            first.append(copy(i, 0, me, sib, True))
            first += [copy(i, 1 + j, me, (*chip, c), True) for j, chip in enumerate(chips)]
        for cp in first:
            cp.start()
        passed = []
        for j, chip in enumerate(chips):
            for i in range(n):
                copy(i, 1 + j, (*chip, c), me).wait_recv()
                cp = copy(i, 4 + j, (*chip, c), sib)
                cp.start()
                passed.append(cp)
        for i in range(n):
            copy(i, 0, sib, me).wait_recv()
        for j, chip in enumerate(chips):
            for i in range(n):
                copy(i, 4 + j, (*chip, 1 - c), me).wait_recv()
        for cp in first + passed:
            cp.wait_send()
        for cp in local:
            cp.wait()

    return pl.pallas_call(
        body, name=name, in_specs=[ANY] * n, out_specs=[ANY] * n,
        out_shape=[_sds((N_DEV,) + a.shape, a.dtype) for a in arrs],
        scratch_shapes=[pltpu.SemaphoreType.DMA((7 * n,)), pltpu.SemaphoreType.DMA((7 * n,)), pltpu.SemaphoreType.DMA((n,))],
        compiler_params=pltpu.CompilerParams(has_side_effects=True))(*arrs)


def _swap_sibling(arrs, *, name):
    n = len(arrs)

    def body(*refs):
        srcs, outs = refs[:n], refs[n:2 * n]
        send_sems, recv_sems = refs[2 * n:]
        x, y, c, _ = _place()
        copies = [pltpu.make_async_remote_copy(
            src_ref=srcs[i].at[2 * j + (1 - c)], dst_ref=outs[i].at[j], send_sem=send_sems.at[N_CHIP * i + j],
            recv_sem=recv_sems.at[N_CHIP * i + j], device_id=(x, y, 1 - c), device_id_type=MESH)
            for i in range(n) for j in range(N_CHIP)]
        for cp in copies:
            cp.start()
        for cp in copies:
            cp.wait()

    return pl.pallas_call(
        body, name=name, in_specs=[ANY] * n, out_specs=[ANY] * n,
        out_shape=[_sds((N_CHIP,) + a.shape[1:], a.dtype) for a in arrs],
        scratch_shapes=[pltpu.SemaphoreType.DMA((N_CHIP * n,)), pltpu.SemaphoreType.DMA((N_CHIP * n,))],
        compiler_params=pltpu.CompilerParams(has_side_effects=True))(*arrs)


def _pair_sum(part, got, c_idx, *, name):
    _, R, C = part.shape
    tr = next(t for t in (512, 256, 128, 64, 32, 16) if R % t == 0)

    def body(c_ref, p_ref, g_ref, o_ref):
        o_ref[...] = (p_ref[0].astype(F32) + g_ref[...].astype(F32)).astype(o_ref.dtype)

    return pl.pallas_call(
        body, name=name,
        grid_spec=pltpu.PrefetchScalarGridSpec(
            num_scalar_prefetch=1, grid=(N_CHIP, R // tr),
            in_specs=[pl.BlockSpec((1, 1, tr, C), lambda j, r, c_ref: (j, c_ref[0], r, 0)),
                      pl.BlockSpec((1, tr, C), lambda j, r, c_ref: (j, r, 0))],
            out_specs=pl.BlockSpec((1, tr, C), lambda j, r, c_ref: (j, r, 0))),
        out_shape=_sds((N_CHIP, R, C), part.dtype),
        compiler_params=_params(("parallel", "parallel")))(c_idx, part.reshape(N_CHIP, 2, R, C), got)


def _swap_chips(arrs, *, name):
    n = len(arrs)

    def body(*refs):
        srcs, outs = refs[:n], refs[n:2 * n]
        send_sems, recv_sems, local_sems = refs[2 * n:]
        x, y, c, chips = _place()
        jme = 2 * x + y
        local = [pltpu.make_async_copy(srcs[i].at[jme], outs[i].at[jme], local_sems.at[i]) for i in range(n)]
        for cp in local:
            cp.start()
        copies = [pltpu.make_async_remote_copy(
            src_ref=srcs[i].at[2 * chip[0] + chip[1]], dst_ref=outs[i].at[jme], send_sem=send_sems.at[3 * i + j],
            recv_sem=recv_sems.at[3 * i + j], device_id=(*chip, c), device_id_type=MESH)
            for i in range(n) for j, chip in enumerate(chips)]
        for cp in copies:
            cp.start()
        for i in range(n):
            for j, chip in enumerate(chips):
                pltpu.make_async_remote_copy(
                    src_ref=srcs[i].at[jme], dst_ref=outs[i].at[2 * chip[0] + chip[1]], send_sem=send_sems.at[3 * i + j],
                    recv_sem=recv_sems.at[3 * i + j], device_id=(*chip, c), device_id_type=MESH).wait_recv()
        for cp in copies:
            cp.wait_send()
        for cp in local:
            cp.wait()

    return pl.pallas_call(
        body, name=name, in_specs=[ANY] * n, out_specs=[ANY] * n,
        out_shape=[_sds(a.shape, a.dtype) for a in arrs],
        scratch_shapes=[pltpu.SemaphoreType.DMA((3 * n,)), pltpu.SemaphoreType.DMA((3 * n,)), pltpu.SemaphoreType.DMA((n,))],
        compiler_params=pltpu.CompilerParams(has_side_effects=True))(*arrs)


def _adamw(parts, w, m, v, *, name):
    P, R, C = parts.shape
    tr = next((t for t in (256, 128, 64, 32, 16, 8) if R % t == 0), R)
    c1 = 1.0 / (1.0 - ADAM_B1 ** ADAM_STEP)
    c2 = 1.0 / (1.0 - ADAM_B2 ** ADAM_STEP)

    def body(p_ref, w_ref, m_ref, v_ref, g_ref, d_ref, nm_ref, nv_ref):
        g = p_ref[0].astype(F32)
        for j in range(1, P):
            g = g + p_ref[j].astype(F32)
        nm = ADAM_B1 * m_ref[...] + (1.0 - ADAM_B1) * g
        nv = ADAM_B2 * v_ref[...] + (1.0 - ADAM_B2) * (g * g)
        g_ref[...] = g
        nm_ref[...] = nm
        nv_ref[...] = nv
        d_ref[...] = -ADAM_LR * ((nm * c1) / (jnp.sqrt(nv * c2) + ADAM_EPS) + ADAM_WD * w_ref[...])

    row = pl.BlockSpec((tr, C), lambda i: (i, 0))
    out = _sds((R, C), F32)
    return pl.pallas_call(
        body, name=name, grid=(R // tr,), in_specs=[pl.BlockSpec((P, tr, C), lambda i: (0, i, 0)), row, row, row],
        out_specs=[row, row, row, row], out_shape=[out, out, out, out], compiler_params=_params(("parallel",)))(parts, w, m, v)


BIG = ("w_in", "w_out", "xa_wq", "xa_wkv", "xa_wo", "ffn_w_up", "ffn_w_down")
COL_SHARDED = ("w_in", "xa_wkv", "ffn_w_up", "ml_conv_w", "ffn_conv_w")
SHARDED_SMALL = ("ml_conv_w", "ffn_conv_w")
REPLICATED = ("rel_bias", "ml_conv_b", "ml_i_bias", "ml_f_bias", "ml_norm_g", "swa_sinks", "ln1_g", "ln1_b", "ln2_g", "ln2_b",
              "ffn_conv_b", "ln3_g", "ln3_b")
NAMES = ("rel_bias", "w_in", "ml_conv_w", "ml_conv_b", "ml_i_bias", "ml_f_bias", "ml_norm_g", "swa_sinks", "w_out", "ln1_g", "ln1_b",
         "xa_wq", "xa_wkv", "xa_wo", "ln2_g", "ln2_b", "ffn_w_up", "ffn_conv_w", "ffn_conv_b", "ffn_w_down", "ln3_g", "ln3_b")


def _flat_rows(a, mult):
    f = a.reshape(-1)
    n = -(-f.shape[0] // (128 * mult)) * (128 * mult)
    if n != f.shape[0]:
        f = jnp.pad(f, (0, n - f.shape[0]))
    return f.reshape(-1, 128)


def _pack(arrs, mult):
    parts = [_flat_rows(a, mult) for a in arrs]
    return jnp.concatenate(parts, axis=0), [p.shape[0] for p in parts]


def _unpack(flat, rows, shapes):
    out, off = [], 0
    lead = flat.shape[:-2]
    for r, shp in zip(rows, shapes):
        n = int(np.prod(shp))
        piece = flat[..., off:off + r, :].reshape(lead + (r * 128,))[..., :n]
        out.append(piece.reshape(lead + tuple(shp)))
        off += r
    return out


def _full_from_shards(stacked, name):
    if name in COL_SHARDED:
        return jnp.moveaxis(stacked, 0, 2).reshape(stacked.shape[1], stacked.shape[2], N_DEV * stacked.shape[3])
    return jnp.moveaxis(stacked, 0, 1).reshape(stacked.shape[1], N_DEV * stacked.shape[2], stacked.shape[3])


def _shards_from_full(full, name):
    L, A, B = full.shape
    if name in COL_SHARDED:
        return jnp.moveaxis(full.reshape(L, A, N_DEV, B // N_DEV), 2, 0)
    return jnp.moveaxis(full.reshape(L, N_DEV, A // N_DEV, B), 1, 0)


def _pad_win(w):
    z = jnp.zeros(w.shape[:-1] + (NP_IN - N_IN,), w.dtype)
    return jnp.concatenate([w[..., :2048], w[..., 2056:], w[..., 2048:2056], z], axis=-1)


def _row128(v):
    return jnp.pad(v, (0, 128 - v.shape[0])).reshape(1, 128)


def _gather_weights(W):
    sm_flat, sm_rows = _pack([W[n] for n in SHARDED_SMALL], 8)
    got = _gather([W[n].astype(BF16) for n in BIG] + [sm_flat], name="gather_weights")
    full = {n: _full_from_shards(s, n) for n, s in zip(BIG, got[:-1])}
    full.update({n: _full_from_shards(s, n)
                 for n, s in zip(SHARDED_SMALL, _unpack(got[-1], sm_rows, [W[n].shape for n in SHARDED_SMALL]))})
    full["w_in"] = _pad_win(full["w_in"])
    return full


def _reduce_and_update(Gf, loss_part, W, Mo, Vo, me, c_idx):
    send = [_shards_from_full(Gf[n], n).astype(BF16) for n in BIG]
    send = [s.reshape(N_DEV, -1, s.shape[-1]) for s in send]
    got = _swap_sibling(send, name="swap_sibling")
    sums = [_pair_sum(s, g, c_idx, name=f"pair_sum_{n}") for n, s, g in zip(BIG, send, got)]
    parts = _swap_chips(sums, name="swap_chips")
    res = {}
    for n, p in zip(BIG, parts):
        shp = W[n].shape
        flat = (shp[0] * shp[1], shp[2])
        outs = _adamw(p, W[n].reshape(flat), Mo[n].reshape(flat), Vo[n].reshape(flat), name=f"adamw_{n}")
        for kind, a in zip(("g", "d", "m", "v"), outs):
            res[kind, n] = a.reshape(shp)

    small = REPLICATED + SHARDED_SMALL
    sp_flat, sp_rows = _pack([Gf[n] for n in small] + [loss_part], 8)
    sp_all = _gather([sp_flat], name="gather_small_grads")[0]

    def widen(n, t):
        if n not in SHARDED_SMALL:
            return t[n]
        return lax.dynamic_update_slice(jnp.zeros(Gf[n].shape, F32), t[n], (0, 0, me * t[n].shape[2]))

    zl = jnp.zeros((8, 128), F32)
    wsm, _ = _pack([widen(n, W) for n in small] + [zl], 8)
    msm, _ = _pack([widen(n, Mo) for n in small] + [zl], 8)
    vsm, _ = _pack([widen(n, Vo) for n in small] + [zl], 8)
    outs_small = [_unpack(o_, sp_rows, [Gf[n].shape for n in small] + [(8, 128)])
                  for o_ in _adamw(sp_all, wsm, msm, vsm, name="adamw_small")]
    for kind, os_ in zip(("g", "d", "m", "v"), outs_small):
        for n, a in zip(small, os_[:-1]):
            if n in SHARDED_SMALL:
                a = lax.dynamic_slice(a, (0, 0, me * W[n].shape[2]), W[n].shape)
            res[kind, n] = a
    return res, outs_small[0][-1][0, 0]


def kernel(x, mem, rel_bias, w_in, ml_conv_w, ml_conv_b, ml_i_bias, ml_f_bias, ml_norm_g, swa_sinks, w_out, ln1_g, ln1_b, xa_wq, xa_wkv, xa_wo, ln2_g, ln2_b, ffn_w_up, ffn_conv_w, ffn_conv_b, ffn_w_down, ln3_g, ln3_b, loss_target, m_rel_bias, m_w_in, m_ml_conv_w, m_ml_conv_b, m_ml_i_bias, m_ml_f_bias, m_ml_norm_g, m_swa_sinks, m_w_out, m_ln1_g, m_ln1_b, m_xa_wq, m_xa_wkv, m_xa_wo, m_ln2_g, m_ln2_b, m_ffn_w_up, m_ffn_conv_w, m_ffn_conv_b, m_ffn_w_down, m_ln3_g, m_ln3_b, v_rel_bias, v_w_in, v_ml_conv_w, v_ml_conv_b, v_ml_i_bias, v_ml_f_bias, v_ml_norm_g, v_swa_sinks, v_w_out, v_ln1_g, v_ln1_b, v_xa_wq, v_xa_wkv, v_xa_wo, v_ln2_g, v_ln2_b, v_ffn_w_up, v_ffn_conv_w, v_ffn_conv_b, v_ffn_w_down, v_ln3_g, v_ln3_b):
    W = dict(rel_bias=rel_bias, w_in=w_in, ml_conv_w=ml_conv_w, ml_conv_b=ml_conv_b, ml_i_bias=ml_i_bias, ml_f_bias=ml_f_bias,
             ml_norm_g=ml_norm_g, swa_sinks=swa_sinks, w_out=w_out, ln1_g=ln1_g, ln1_b=ln1_b, xa_wq=xa_wq, xa_wkv=xa_wkv,
             xa_wo=xa_wo, ln2_g=ln2_g, ln2_b=ln2_b, ffn_w_up=ffn_w_up, ffn_conv_w=ffn_conv_w, ffn_conv_b=ffn_conv_b,
             ffn_w_down=ffn_w_down, ln3_g=ln3_g, ln3_b=ln3_b)
    Mo = dict(rel_bias=m_rel_bias, w_in=m_w_in, ml_conv_w=m_ml_conv_w, ml_conv_b=m_ml_conv_b, ml_i_bias=m_ml_i_bias,
              ml_f_bias=m_ml_f_bias, ml_norm_g=m_ml_norm_g, swa_sinks=m_swa_sinks, w_out=m_w_out, ln1_g=m_ln1_g, ln1_b=m_ln1_b,
              xa_wq=m_xa_wq, xa_wkv=m_xa_wkv, xa_wo=m_xa_wo, ln2_g=m_ln2_g, ln2_b=m_ln2_b, ffn_w_up=m_ffn_w_up,
              ffn_conv_w=m_ffn_conv_w, ffn_conv_b=m_ffn_conv_b, ffn_w_down=m_ffn_w_down, ln3_g=m_ln3_g, ln3_b=m_ln3_b)
    Vo = dict(rel_bias=v_rel_bias, w_in=v_w_in, ml_conv_w=v_ml_conv_w, ml_conv_b=v_ml_conv_b, ml_i_bias=v_ml_i_bias,
              ml_f_bias=v_ml_f_bias, ml_norm_g=v_ml_norm_g, swa_sinks=v_swa_sinks, w_out=v_w_out, ln1_g=v_ln1_g, ln1_b=v_ln1_b,
              xa_wq=v_xa_wq, xa_wkv=v_xa_wkv, xa_wo=v_xa_wo, ln2_g=v_ln2_g, ln2_b=v_ln2_b, ffn_w_up=v_ffn_w_up,
              ffn_conv_w=v_ffn_conv_w, ffn_conv_b=v_ffn_conv_b, ffn_w_down=v_ffn_w_down, ln3_g=v_ln3_g, ln3_b=v_ln3_b)
    S = x.shape[1]
    c_me = lax.axis_index("c")
    me = 4 * lax.axis_index("x") + 2 * lax.axis_index("y") + c_me
    c_idx = jnp.reshape(c_me, (1,)).astype(jnp.int32)
    xs = x.reshape(S, D)
    mems = mem.reshape(mem.shape[1], D)
    tgt = loss_target.reshape(S, D)

    full = _gather_weights(W)
    win = full["w_in"]
    win_ml = jnp.concatenate([win[..., 1024:2048], win[..., 2816:2944]], axis=-1)
    win_sw = win[..., 2048:2816]
    win_qk = win[..., :1024]

    bucket = jnp.asarray(_t5_buckets())
    bias = _bias_table(rel_bias, bucket, name="bias_table")

    saved = []
    h0, h0b = xs, xs.astype(BF16)
    for l in range(DEPTH):
        gbias = _row128(jnp.concatenate([ml_i_bias[l], ml_f_bias[l]]))
        sinks = _row128(swa_sinks[l])
        ng = ml_norm_g[l].reshape(1, ML_W)
        proj = _mm(h0b, win[l], name=f"proj{l}")
        qk = _silu_conv_fwd(proj, full["ml_conv_w"][l], ml_conv_b[l].reshape(1, -1), name=f"mlconv{l}")
        h_ml, cs, ns, ms = _mlstm_fwd(qk, proj, gbias, ng, name=f"mlstm{l}")
        h_sw = _swa_fwd(proj, bias, sinks, name=f"swa{l}")
        h1, h1b, z1 = _mm_res_ln([h_ml, h_sw], full["w_out"][l], h0, ln1_g[l].reshape(1, D), ln1_b[l].reshape(1, D),
                                 name=f"mix_out{l}")
        q = _mm(h1b, full["xa_wq"][l], name=f"xa_q{l}")
        kv = _mm(mems, full["xa_wkv"][l], tm=256, name=f"xa_kv{l}")
        o = _xattn_fwd(q, kv, name=f"xattn{l}")
        h2, h2b, z2 = _mm_res_ln([o], full["xa_wo"][l], h1, ln2_g[l].reshape(1, D), ln2_b[l].reshape(1, D), name=f"xa_out{l}")
        up = _mm(h2b, full["ffn_w_up"][l], name=f"ffn_up{l}")
        act = _ffn_gate_fwd(up, full["ffn_conv_w"][l], ffn_conv_b[l].reshape(1, -1), name=f"ffn_gate{l}")
        h3, h3b, z3 = _mm_res_ln([act], full["ffn_w_down"][l], h2, ln3_g[l].reshape(1, D), ln3_b[l].reshape(1, D),
                                 name=f"ffn_out{l}")
        saved.append(dict(h0b=h0b, proj=proj, qk=qk, cs=cs, ns=ns, ms=ms, h_ml=h_ml, h_sw=h_sw, z1=z1, h1b=h1b, q=q, kv=kv, o=o,
                          z2=z2, h2b=h2b, up=up, act=act, z3=z3, gbias=gbias, sinks=sinks, ng=ng))
        h0, h0b = h3, h3b

    loss_part, dh = _loss_head(h0, tgt, name="loss_head")

    G = {n: [None] * DEPTH for n in NAMES if n != "rel_bias"}
    dbias = [None] * DEPTH
    for l in reversed(range(DEPTH)):
        sv = saved[l]
        dz3, dz3b, G["ln3_g"][l], G["ln3_b"][l] = _ln_bwd(dh, sv["z3"], ln3_g[l].reshape(1, D), name=f"ln3_bwd{l}")
        G["ffn_w_down"][l] = _mm_tn(sv["act"], dz3b, name=f"d_w_down{l}")
        dact = _mm(dz3b, full["ffn_w_down"][l], trans_b=True, name=f"d_act{l}")
        dupg, dupv, G["ffn_conv_w"][l], G["ffn_conv_b"][l] = _ffn_gate_bwd(
            dact, sv["up"], full["ffn_conv_w"][l], ffn_conv_b[l].reshape(1, -1), name=f"ffn_gate_bwd{l}")
        G["ffn_w_up"][l] = jnp.concatenate([_mm_tn(sv["h2b"], dupg, name=f"d_w_up_g{l}"),
                                            _mm_tn(sv["h2b"], dupv, name=f"d_w_up_v{l}")], axis=1)
        dh2 = _mm([dupg, dupv], full["ffn_w_up"][l], trans_b=True, add=dz3, add_scale=ALPHA, name=f"d_h2_{l}")

        dz2, dz2b, G["ln2_g"][l], G["ln2_b"][l] = _ln_bwd(dh2, sv["z2"], ln2_g[l].reshape(1, D), name=f"ln2_bwd{l}")
        G["xa_wo"][l] = _mm_tn(sv["o"], dz2b, name=f"d_xa_wo{l}")
        do = _mm(dz2b, full["xa_wo"][l], trans_b=True, name=f"d_xa_o{l}")
        dq, dkv = _xattn_bwd(do, sv["q"], sv["kv"], name=f"xattn_bwd{l}")
        G["xa_wkv"][l] = _mm_tn(mems, dkv, name=f"d_xa_wkv{l}")
        G["xa_wq"][l] = _mm_tn(sv["h1b"], dq, name=f"d_xa_wq{l}")
        dh1 = _mm(dq, full["xa_wq"][l], trans_b=True, add=dz2, add_scale=ALPHA, name=f"d_h1_{l}")

        dz1, dz1b, G["ln1_g"][l], G["ln1_b"][l] = _ln_bwd(dh1, sv["z1"], ln1_g[l].reshape(1, D), name=f"ln1_bwd{l}")
        G["w_out"][l] = jnp.concatenate([_mm_tn(sv["h_ml"], dz1b, name=f"d_w_out_ml{l}"),
                                         _mm_tn(sv["h_sw"], dz1b, name=f"d_w_out_sw{l}")], axis=0)
        dhcat = _mm(dz1b, full["w_out"][l], trans_b=True, name=f"d_hcat{l}")
        dsw, dbias[l], dsinks = _swa_bwd(dhcat, sv["proj"], bias, sv["sinks"], name=f"swa_bwd{l}")
        dqk, dml, dgb, dng = _mlstm_bwd(dhcat, sv["qk"], sv["proj"], sv["gbias"], sv["ng"], sv["cs"], sv["ns"], sv["ms"],
                                        name=f"mlstm_bwd{l}")
        dqk_pre, G["ml_conv_w"][l], G["ml_conv_b"][l] = _silu_conv_bwd(
            dqk, sv["proj"], full["ml_conv_w"][l], ml_conv_b[l].reshape(1, -1), name=f"mlconv_bwd{l}")
        dw_qk = _mm_tn(sv["h0b"], dqk_pre, name=f"d_w_in_qk{l}")
        dw_ml = _mm_tn(sv["h0b"], dml, name=f"d_w_in_ml{l}")
        dw_sw = _mm_tn(sv["h0b"], dsw, name=f"d_w_in_sw{l}")
        G["w_in"][l] = jnp.concatenate([dw_qk, dw_ml[:, :2 * ML_W + 2 * ML_H], dw_sw], axis=1)
        dh = _mm(dqk_pre, win_qk[l], trans_b=True, add=dz1, add_scale=ALPHA, name=f"d_h0_qk{l}")
        dh = _mm(dml, win_ml[l], trans_b=True, add=dh, name=f"d_h0_ml{l}")
        dh = _mm(dsw, win_sw[l], trans_b=True, add=dh, name=f"d_h0_sw{l}")
        G["ml_i_bias"][l] = dgb[0, :ML_H]
        G["ml_f_bias"][l] = dgb[0, ML_H:2 * ML_H]
        G["ml_norm_g"][l] = dng
        G["swa_sinks"][l] = dsinks[0, :SW_H]
    grad_x = dh.reshape(x.shape)

    Gf = {n: jnp.stack([g.reshape(W[n].shape[1:]) if n in REPLICATED else g for g in G[n]]) for n in G}
    Gf["rel_bias"] = _bias_table_bwd(dbias, bucket, name="bias_table_bwd")[:, :SW_H]

    res, loss = _reduce_and_update(Gf, loss_part, W, Mo, Vo, me, c_idx)
    return (loss, grad_x, *[res["g", n] for n in NAMES], *[res["d", n] for n in NAMES], *[res["m", n] for n in NAMES],
            *[res["v", n] for n in NAMES])
```

```python
import functools
import math

import jax
import jax.numpy as jnp
import numpy as np
from jax import lax
from jax.experimental import pallas as pl
from jax.experimental.pallas import tpu as pltpu

F32 = jnp.float32
BF16 = jnp.bfloat16

N_DEV = 8
N_CHIP = 4
D = 1024
DEPTH = 2
ML_H = 4
ML_W = 512
ML_DH = 128
ML_L = 64
ML_CONV = 4
SW_DH = 64
SW_W = 512
SW_H = 8
SW_G = 4
SW_KVW = 128
BLK = 128
REL_B = 32
REL_MAXD = 128
XA_H = 4
XA_DH = 256
DFF = 2816
NB_FF = DFF // 128
FFN_CONV = 3
ALPHA = (2.0 * DEPTH) ** 0.25
EPS = 1e-5
N_IN = 2824
NP_IN = 3072
ML_GW = 2 * ML_W + 128
SW_GW = SW_W + 2 * SW_KVW
ADAM_LR = 0.001
ADAM_B1 = 0.9
ADAM_B2 = 0.999
ADAM_EPS = 1e-08
ADAM_WD = 0.01
ADAM_STEP = 10
VMEM_LIMIT = 56 * 1024 * 1024
MESH = pl.DeviceIdType.MESH

NN = ((1,), (0,))
NT = ((1,), (1,))
TN = ((0,), (0,))


def _dg(a, b, dn):
    if a.ndim == 3:
        dims = (((dn[0][0] + 1,), (dn[1][0] + 1,)), ((0,), (0,)))
    else:
        dims = (dn, ((), ()))
    return lax.dot_general(a.astype(BF16), b.astype(BF16), dims, preferred_element_type=F32)


@jax.custom_vjp
def dot_nn(a, b):
    return _dg(a, b, NN)


dot_nn.defvjp(lambda a, b: (_dg(a, b, NN), (a, b)), lambda r, g: (_dg(g, r[1], NT), _dg(r[0], g, TN)))


@jax.custom_vjp
def dot_nt(a, b):
    return _dg(a, b, NT)


dot_nt.defvjp(lambda a, b: (_dg(a, b, NT), (a, b)), lambda r, g: (_dg(g, r[1], NN), _dg(g, r[0], TN)))


@jax.custom_vjp
def dot_tn(a, b):
    return _dg(a, b, TN)


dot_tn.defvjp(lambda a, b: (_dg(a, b, TN), (a, b)), lambda r, g: (_dg(r[1], g, NT), _dg(r[0], g, NN)))


def _params(sem=None):
    return pltpu.CompilerParams(dimension_semantics=sem, vmem_limit_bytes=VMEM_LIMIT)


def _sds(shape, dtype):
    return jax.ShapeDtypeStruct(tuple(shape), dtype)


TOKEN = pl.BlockSpec((8, 128), lambda *_: (0, 0))


def _mm(a, b, *, trans_b=False, out_dtype=F32, add=None, add_scale=1.0, tm=1024, tn=512, dep=(), name):
    a_list = list(a) if isinstance(a, (list, tuple)) else [a]
    M = a_list[0].shape[0]
    N = b.shape[0] if trans_b else b.shape[1]
    tm = min(tm, M)
    tn = next(t for t in (tn, 384, 256, 128) if N % t == 0)
    assert M % tm == 0
    Ka = a_list[0].shape[1]
    assert all(t.shape[1] == Ka for t in a_list)
    tk = next(t for t in (Ka, 1408, 1024) if Ka % t == 0 and t <= 1408)
    na, npa = len(a_list), Ka // tk
    nk = na * npa
    has_add = add is not None

    def body(*refs):
        a_refs, b_ref = refs[:na], refs[na]
        add_ref = refs[na + 1] if has_add else None
        o_ref, acc_ref = refs[-2], refs[-1]
        k = pl.program_id(2)

        def finish(r):
            if has_add:
                r = r + add_scale * add_ref[...].astype(F32)
            o_ref[...] = r.astype(out_dtype)

        for t, a_ref in enumerate(a_refs):
            def step(a_ref=a_ref):
                p = _dg(a_ref[...], b_ref[...], NT if trans_b else NN)
                if nk == 1:
                    finish(p)
                    return

                @pl.when(k == 0)
                def _():
                    acc_ref[...] = p

                @pl.when((k > 0) & (k < nk - 1))
                def _():
                    acc_ref[...] += p

                @pl.when(k == nk - 1)
                def _():
                    finish(acc_ref[...] + p)

            if na == 1:
                step()
            else:
                pl.when((k >= t * npa) & (k < (t + 1) * npa))(step)

    in_specs = [pl.BlockSpec((tm, tk), lambda i, j, k, t=t: (i, jnp.clip(k - t * npa, 0, npa - 1))) for t in range(na)]
    in_specs.append(pl.BlockSpec((tn, tk), lambda i, j, k: (j, k)) if trans_b else pl.BlockSpec((tk, tn), lambda i, j, k: (k, j)))
    args = a_list + [b]
    if has_add:
        in_specs.append(pl.BlockSpec((tm, tn), lambda i, j, k: (i, j)))
        args.append(add)
    in_specs += [TOKEN] * len(dep)
    args += list(dep)
    return pl.pallas_call(
        body, name=name, grid=(M // tm, N // tn, nk), in_specs=in_specs,
        out_specs=pl.BlockSpec((tm, tn), lambda i, j, k: (i, j)), out_shape=_sds((M, N), out_dtype),
        scratch_shapes=[pltpu.VMEM((tm, tn) if nk > 1 else (8, 128), F32)],
        compiler_params=_params(("parallel", "parallel", "arbitrary")))(*args)


def _mm_tn(a, g, *, name):
    S, K = a.shape
    N = g.shape[1]
    tk = K if K <= 1024 else K // 2
    tn = next(t for t in range(1536, 0, -128) if N % t == 0)
    ts = min(512, S)
    ns = S // ts
    assert K % tk == 0 and S % ts == 0

    def body(a_ref, g_ref, o_ref):
        s = pl.program_id(2)
        p = _dg(a_ref[...], g_ref[...], TN)

        @pl.when(s == 0)
        def _():
            o_ref[...] = p

        @pl.when(s > 0)
        def _():
            o_ref[...] += p

    return pl.pallas_call(
        body, name=name, grid=(K // tk, N // tn, ns),
        in_specs=[pl.BlockSpec((ts, tk), lambda i, j, s: (s, i)), pl.BlockSpec((ts, tn), lambda i, j, s: (s, j))],
        out_specs=pl.BlockSpec((tk, tn), lambda i, j, s: (i, j)), out_shape=_sds((K, N), F32),
        compiler_params=_params(("parallel", "parallel", "arbitrary")))(a, g)


def _mm_res_ln(a_list, w, resid, gam, bet, *, name):
    M, Ka = a_list[0].shape
    na = len(a_list)
    assert w.shape[0] == na * Ka
    tm = 256

    def body(*refs):
        a_refs, w_refs = refs[:na], refs[na:2 * na]
        r_ref, g_ref, b_ref, y_ref, yb_ref, z_ref = refs[2 * na:]
        z = ALPHA * r_ref[...]
        for a_ref, w_ref in zip(a_refs, w_refs):
            z = z + _dg(a_ref[...], w_ref[...], NN)
        mu = jnp.mean(z, axis=1, keepdims=True)
        zc = z - mu
        var = jnp.mean(zc * zc, axis=1, keepdims=True)
        y = zc * lax.rsqrt(var + EPS) * g_ref[...] + b_ref[...]
        y_ref[...] = y
        yb_ref[...] = y.astype(BF16)
        z_ref[...] = z

    row = pl.BlockSpec((tm, D), lambda i: (i, 0))
    vec = pl.BlockSpec((1, D), lambda i: (0, 0))
    a_specs = [pl.BlockSpec((tm, Ka), lambda i: (i, 0)) for _ in a_list]
    w_specs = [pl.BlockSpec((Ka, D), lambda i, t=t: (t, 0)) for t in range(na)]
    return pl.pallas_call(
        body, name=name, grid=(M // tm,), in_specs=a_specs + w_specs + [row, vec, vec],
        out_specs=[row, row, row], out_shape=[_sds((M, D), F32), _sds((M, D), BF16), _sds((M, D), F32)],
        compiler_params=_params(("parallel",)))(*a_list, *([w] * na), resid, gam, bet)


def _ln_bwd(dy, z, gam, *, dep=(), name):
    M = dy.shape[0]
    tm = 512

    def body(dy_ref, z_ref, g_ref, *rest):
        dz_ref, dzb_ref, dg_ref, db_ref = rest[len(dep):]
        i = pl.program_id(0)
        z = z_ref[...]
        dy_ = dy_ref[...]
        mu = jnp.mean(z, axis=1, keepdims=True)
        zc = z - mu
        var = jnp.mean(zc * zc, axis=1, keepdims=True)
        rstd = lax.rsqrt(var + EPS)
        xh = zc * rstd
        dxh = dy_ * g_ref[...]
        m1 = jnp.mean(dxh, axis=1, keepdims=True)
        m2 = jnp.mean(dxh * xh, axis=1, keepdims=True)
        dz = rstd * (dxh - m1 - xh * m2)
        dz_ref[...] = dz
        dzb_ref[...] = dz.astype(BF16)
        pg = jnp.sum(dy_ * xh, axis=0, keepdims=True)
        pb = jnp.sum(dy_, axis=0, keepdims=True)

        @pl.when(i == 0)
        def _():
            dg_ref[...] = pg
            db_ref[...] = pb

        @pl.when(i > 0)
        def _():
            dg_ref[...] += pg
            db_ref[...] += pb

    row = pl.BlockSpec((tm, D), lambda i: (i, 0))
    vec = pl.BlockSpec((1, D), lambda i: (0, 0))
    return pl.pallas_call(
        body, name=name, grid=(M // tm,), in_specs=[row, row, vec] + [TOKEN] * len(dep), out_specs=[row, row, vec, vec],
        out_shape=[_sds((M, D), F32), _sds((M, D), BF16), _sds((1, D), F32), _sds((1, D), F32)],
        compiler_params=_params(("arbitrary",)))(dy, z, gam, *dep)


def _shift_down(x, d):
    if d == 0:
        return x
    rows = lax.broadcasted_iota(jnp.int32, x.shape, 0)
    return jnp.where(rows >= d, pltpu.roll(x, d, 0), 0.0)


def _shift_up(x, d):
    if d == 0:
        return x
    S = x.shape[0]
    rows = lax.broadcasted_iota(jnp.int32, x.shape, 0)
    return jnp.where(rows < S - d, pltpu.roll(x, S - d, 0), 0.0)


def _conv(x, w_ref, b_ref, cs, K):
    y = b_ref[:, cs]
    for j in range(K):
        y = y + _shift_down(x, K - 1 - j) * w_ref[j:j + 1, cs]
    return y


def _conv_bwd(dy, x, w_ref, dw_ref, db_ref, cs, K):
    dx = jnp.zeros_like(x)
    for j in range(K):
        dx = dx + _shift_up(dy, K - 1 - j) * w_ref[j:j + 1, cs]
        dw_ref[j:j + 1, cs] = jnp.sum(dy * _shift_down(x, K - 1 - j), axis=0, keepdims=True)
    db_ref[:, cs] = jnp.sum(dy, axis=0, keepdims=True)
    return dx


ALL = slice(None)


def _silu_conv_fwd(proj, cw, cb, *, name):
    S = proj.shape[0]

    def body(x_ref, w_ref, b_ref, o_ref):
        o_ref[...] = jax.nn.silu(_conv(x_ref[...], w_ref, b_ref, ALL, ML_CONV))

    col = pl.BlockSpec((S, 128), lambda j: (0, j))
    return pl.pallas_call(
        body, name=name, grid=(8,),
        in_specs=[col, pl.BlockSpec((ML_CONV, 128), lambda j: (0, j)), pl.BlockSpec((1, 128), lambda j: (0, j))],
        out_specs=col, out_shape=_sds((S, 2 * ML_W), F32), compiler_params=_params(("parallel",)))(proj, cw, cb)


def _silu_conv_bwd(dqk, proj, cw, cb, *, name):
    S = proj.shape[0]

    def body(d_ref, x_ref, w_ref, b_ref, dx_ref, dw_ref, db_ref):
        x = x_ref[...]
        y = _conv(x, w_ref, b_ref, ALL, ML_CONV)
        dy = jax.vjp(jax.nn.silu, y)[1](d_ref[...])[0]
        dx_ref[...] = _conv_bwd(dy, x, w_ref, dw_ref, db_ref, ALL, ML_CONV).astype(BF16)

    col = pl.BlockSpec((S, 128), lambda j: (0, j))
    wsp = pl.BlockSpec((ML_CONV, 128), lambda j: (0, j))
    bsp = pl.BlockSpec((1, 128), lambda j: (0, j))
    return pl.pallas_call(
        body, name=name, grid=(8,), in_specs=[col, col, wsp, bsp], out_specs=[col, wsp, bsp],
        out_shape=[_sds((S, 2 * ML_W), BF16), _sds((ML_CONV, 2 * ML_W), F32), _sds((1, 2 * ML_W), F32)],
        compiler_params=_params(("parallel",)))(dqk, proj, cw, cb)


def _gate(ug, uv):
    return jax.nn.gelu(ug) * uv


def _ffn_specs(S):
    return (pl.BlockSpec((S, 128), lambda j: (0, j)), pl.BlockSpec((S, 128), lambda j: (0, j + NB_FF)),
            pl.BlockSpec((FFN_CONV, 128), lambda j: (0, j)), pl.BlockSpec((FFN_CONV, 128), lambda j: (0, j + NB_FF)),
            pl.BlockSpec((1, 128), lambda j: (0, j)), pl.BlockSpec((1, 128), lambda j: (0, j + NB_FF)))


def _ffn_gate_fwd(up, cw, cb, *, name):
    S = up.shape[0]
    cg, cv, wg, wv, bg, bv = _ffn_specs(S)

    def body(g_ref, v_ref, wg_ref, wv_ref, bg_ref, bv_ref, o_ref):
        ug = _conv(g_ref[...], wg_ref, bg_ref, ALL, FFN_CONV)
        uv = _conv(v_ref[...], wv_ref, bv_ref, ALL, FFN_CONV)
        o_ref[...] = _gate(ug, uv).astype(BF16)

    return pl.pallas_call(
        body, name=name, grid=(NB_FF,), in_specs=[cg, cv, wg, wv, bg, bv], out_specs=cg,
        out_shape=_sds((S, DFF), BF16), compiler_params=_params(("parallel",)))(up, up, cw, cw, cb, cb)


def _ffn_gate_bwd(da, up, cw, cb, *, name):
    S = up.shape[0]
    cg, cv, wg, wv, bg, bv = _ffn_specs(S)

    def body(da_ref, g_ref, v_ref, wg_ref, wv_ref, bg_ref, bv_ref, dxg_ref, dxv_ref, dwg_ref, dwv_ref, dbg_ref, dbv_ref):
        xg, xv = g_ref[...], v_ref[...]
        ug = _conv(xg, wg_ref, bg_ref, ALL, FFN_CONV)
        uv = _conv(xv, wv_ref, bv_ref, ALL, FFN_CONV)
        dug, duv = jax.vjp(_gate, ug, uv)[1](da_ref[...])
        dxg_ref[...] = _conv_bwd(dug, xg, wg_ref, dwg_ref, dbg_ref, ALL, FFN_CONV).astype(BF16)
        dxv_ref[...] = _conv_bwd(duv, xv, wv_ref, dwv_ref, dbv_ref, ALL, FFN_CONV).astype(BF16)

    half = _sds((S, DFF), BF16)
    dxg, dxv, dwg, dwv, dbg, dbv = pl.pallas_call(
        body, name=name, grid=(NB_FF,), in_specs=[cg, cg, cv, wg, wv, bg, bv], out_specs=[cg, cg, wg, wg, bg, bg],
        out_shape=[half, half, _sds((FFN_CONV, DFF), F32), _sds((FFN_CONV, DFF), F32), _sds((1, DFF), F32), _sds((1, DFF), F32)],
        compiler_params=_params(("parallel",)))(da, up, up, cw, cw, cb, cb)
    return dxg, dxv, jnp.concatenate([dwg, dwv], axis=1), jnp.concatenate([dbg, dbv], axis=1)


def _log_sigmoid(x):
    return jnp.minimum(x, 0.0) - jnp.log1p(jnp.exp(-jnp.abs(x)))


@jax.custom_vjp
def _clamp_div(num, den, floor, shift):
    return num / jnp.maximum(jnp.abs(den), floor)


def _clamp_div_fwd(num, den, floor, shift):
    out = num / jnp.maximum(jnp.abs(den), floor)
    return out, (den, floor, out)


def _clamp_div_bwd(res, g):
    den, floor, out = res
    active = jnp.abs(den) < floor
    dinv = jnp.maximum(jnp.abs(den), floor)
    go = jnp.sum(g * out, axis=-1, keepdims=True)
    ddiv = -go / dinv
    return (g / dinv, jnp.where(active, 0.0, ddiv * jnp.sign(den)), jnp.where(active, ddiv, 0.0),
            jnp.sum(jnp.where(active, go, 0.0), axis=-2, keepdims=True))


_clamp_div.defvjp(_clamp_div_fwd, _clamp_div_bwd)


def _ml_heads(q, k, v, o_pre, gates, gbias, C, n, ng, m, shift):
    H, L, _ = q.shape
    lane1 = lax.broadcasted_iota(jnp.int32, (1, 128), 1)
    gz = gates + jnp.where(lane1 < ML_H, lax.stop_gradient(gbias), gbias)
    gz = jnp.broadcast_to(gz[None], (H, L, 128))
    hid = lax.broadcasted_iota(jnp.int32, (H, L, 128), 0)
    lane = lax.broadcasted_iota(jnp.int32, (H, L, 128), 2)
    ig = jnp.sum(jnp.where(lane == hid, gz, 0.0), axis=2, keepdims=True)
    lf = _log_sigmoid(jnp.sum(jnp.where(lane == ML_H + hid, gz, 0.0), axis=2, keepdims=True))
    r = lax.broadcasted_iota(jnp.int32, (H, L, L), 1)
    c = lax.broadcasted_iota(jnp.int32, (H, L, L), 2)
    eye, tril = r == c, r >= c

    def to_row(col):
        return jnp.sum(jnp.where(eye, col, 0.0), axis=1, keepdims=True)

    b_col = jnp.sum(jnp.where(tril, to_row(lf), 0.0), axis=2, keepdims=True)
    Dm = jnp.where(tril, b_col - to_row(b_col) + to_row(ig), -jnp.inf)
    inter = b_col + m
    m_t = jnp.maximum(inter, jnp.max(Dm, axis=2, keepdims=True))
    w_inter = jnp.exp(inter - m_t)
    ks = k * (ML_DH ** -0.5)
    s = dot_nt(q, ks) * jnp.exp(Dm - m_t)
    num = w_inter * dot_nn(q, C) + dot_nn(s, v)
    den = w_inter * jnp.sum(q * n, axis=2, keepdims=True) + jnp.sum(s, axis=2, keepdims=True)
    h = _clamp_div(num, den, jnp.exp(-m_t), shift)
    g = jnp.sum(lf, axis=1, keepdims=True)
    a = g - b_col + ig
    m_new = jnp.maximum(g + m, jnp.max(a, axis=1, keepdims=True))
    decay = jnp.exp(g + m - m_new)
    wk = jnp.exp(a - m_new)
    C_new = decay * C + dot_tn(ks * wk, v)
    n_new = decay * n + jnp.sum(wk * ks, axis=1, keepdims=True)
    mu = jnp.mean(h, axis=2, keepdims=True)
    hc = h - mu
    var = jnp.mean(hc * hc, axis=2, keepdims=True)
    out = jax.nn.sigmoid(o_pre) * (hc * lax.rsqrt(var + EPS) * ng)
    return out, C_new, n_new, m_new


def _hs(h, off=0):
    return slice(off + h * ML_DH, off + (h + 1) * ML_DH)


def _heads(ref, off=0):
    return jnp.stack([ref[:, _hs(h, off)] for h in range(ML_H)])


def _mlstm_fwd(qk, proj, gbias, ng, *, name):
    S = qk.shape[0]
    nc = S // ML_L

    def body(q_ref, k_ref, v_ref, o_ref, g_ref, gb_ref, ng_ref, h_ref, cs_ref, ns_ref, ms_ref, c_s, n_s, m_s):
        @pl.when(pl.program_id(0) == 0)
        def _():
            c_s[...] = jnp.zeros_like(c_s)
            n_s[...] = jnp.zeros_like(n_s)
            m_s[...] = jnp.zeros_like(m_s)

        C, n = c_s[...], n_s[...]
        cs_ref[0] = C
        ns_ref[0] = n
        ms_ref[0] = m_s[...]
        out, C2, n2, m2 = _ml_heads(_heads(q_ref), _heads(k_ref), _heads(v_ref), _heads(o_ref), g_ref[...], gb_ref[...], C, n,
                                    _heads(ng_ref), m_s[:, :, 0:1], jnp.zeros((ML_H, 1, 1), F32))
        for h in range(ML_H):
            h_ref[:, _hs(h)] = out[h].astype(BF16)
        c_s[...] = C2
        n_s[...] = n2
        m_s[...] = jnp.broadcast_to(m2, (ML_H, 1, 128))

    def w(j):
        return pl.BlockSpec((ML_L, ML_W), lambda c, j=j: (c, j))

    return pl.pallas_call(
        body, name=name, grid=(nc,),
        in_specs=[w(0), w(1), w(2), w(3), pl.BlockSpec((ML_L, 128), lambda c: (c, 22)),
                  pl.BlockSpec((1, 128), lambda c: (0, 0)), pl.BlockSpec((1, ML_W), lambda c: (0, 0))],
        out_specs=[w(0), pl.BlockSpec((1, ML_H, ML_DH, ML_DH), lambda c: (c, 0, 0, 0)),
                   pl.BlockSpec((1, ML_H, 1, 128), lambda c: (c, 0, 0, 0)), pl.BlockSpec((1, ML_H, 1, 128), lambda c: (c, 0, 0, 0))],
        out_shape=[_sds((S, ML_W), BF16), _sds((nc, ML_H, ML_DH, ML_DH), F32), _sds((nc, ML_H, 1, 128), F32),
                   _sds((nc, ML_H, 1, 128), F32)],
        scratch_shapes=[pltpu.VMEM((ML_H, ML_DH, ML_DH), F32), pltpu.VMEM((ML_H, 1, 128), F32), pltpu.VMEM((ML_H, 1, 128), F32)],
        compiler_params=_params(("arbitrary",)))(qk, qk, proj, proj, proj, gbias, ng)


def _mlstm_bwd(dh, qk, proj, gbias, ng, cs, ns, ms, *, name):
    S = qk.shape[0]
    nc = S // ML_L

    def body(dh_ref, q_ref, k_ref, v_ref, o_ref, g_ref, gb_ref, ng_ref, cs_ref, ns_ref, ms_ref,
             dqk_ref, dml_ref, dgb_ref, dng_ref, dc_s, dn_s, dm_s):
        @pl.when(pl.program_id(0) == 0)
        def _():
            dc_s[...] = jnp.zeros_like(dc_s)
            dn_s[...] = jnp.zeros_like(dn_s)
            dm_s[...] = jnp.zeros_like(dm_s)
            dgb_ref[...] = jnp.zeros_like(dgb_ref)
            dng_ref[...] = jnp.zeros_like(dng_ref)

        _, vjp = jax.vjp(_ml_heads, _heads(q_ref), _heads(k_ref), _heads(v_ref), _heads(o_ref), g_ref[...], gb_ref[...],
                         cs_ref[0], ns_ref[0], _heads(ng_ref), ms_ref[0][:, :, 0:1], jnp.zeros((ML_H, 1, 1), F32))
        dq, dk, dv, do, dgates, dgb, dC, dn, dng, dm, dshift = vjp((_heads(dh_ref), dc_s[...], dn_s[...], dm_s[:, :, 0:1]))
        lane1 = lax.broadcasted_iota(jnp.int32, (1, 128), 1)
        for h in range(ML_H):
            dqk_ref[:, _hs(h)] = dq[h]
            dqk_ref[:, _hs(h, ML_W)] = dk[h]
            dml_ref[:, _hs(h)] = dv[h].astype(BF16)
            dml_ref[:, _hs(h, ML_W)] = do[h].astype(BF16)
            dng_ref[:, _hs(h)] += dng[h]
            dgb = jnp.where(lane1 == h, dshift[h], dgb)
        dc_s[...] = dC
        dn_s[...] = dn
        dm_s[...] = jnp.broadcast_to(dm, (ML_H, 1, 128))
        dml_ref[:, 2 * ML_W:] = dgates.astype(BF16)
        dgb_ref[...] += dgb

    def w(j):
        return pl.BlockSpec((ML_L, ML_W), lambda c, j=j: (nc - 1 - c, j))

    vec = pl.BlockSpec((1, 128), lambda c: (0, 0))
    vecw = pl.BlockSpec((1, ML_W), lambda c: (0, 0))
    gsp = pl.BlockSpec((ML_L, 128), lambda c: (nc - 1 - c, 22))
    st = pl.BlockSpec((1, ML_H, 1, 128), lambda c: (nc - 1 - c, 0, 0, 0))
    return pl.pallas_call(
        body, name=name, grid=(nc,),
        in_specs=[w(0), w(0), w(1), w(2), w(3), gsp, vec, vecw,
                  pl.BlockSpec((1, ML_H, ML_DH, ML_DH), lambda c: (nc - 1 - c, 0, 0, 0)), st, st],
        out_specs=[pl.BlockSpec((ML_L, 2 * ML_W), lambda c: (nc - 1 - c, 0)), pl.BlockSpec((ML_L, ML_GW), lambda c: (nc - 1 - c, 0)),
                   vec, vecw],
        out_shape=[_sds((S, 2 * ML_W), F32), _sds((S, ML_GW), BF16), _sds((1, 128), F32), _sds((1, ML_W), F32)],
        scratch_shapes=[pltpu.VMEM((ML_H, ML_DH, ML_DH), F32), pltpu.VMEM((ML_H, 1, 128), F32), pltpu.VMEM((ML_H, 1, 128), F32)],
        compiler_params=_params(("arbitrary",)))(dh, qk, qk, proj, proj, proj, gbias, ng, cs, ns, ms)


def _t5_buckets():
    r = np.arange(BLK)[:, None]
    c = np.arange(2 * BLK)[None, :]
    n = np.maximum(BLK + r - c, 0)
    max_exact = REL_B // 2
    nf = np.maximum(n, 1).astype(np.float32)
    large = max_exact + (np.log(nf / np.float32(max_exact)) / np.float32(math.log(REL_MAXD / max_exact))
                         * np.float32(REL_B - max_exact)).astype(np.int32)
    large = np.minimum(large, REL_B - 1)
    return np.where(n < max_exact, n, large).astype(np.int32)


def _bias_table(rel_bias, bucket, *, name):
    def body(rb_ref, bk_ref, o_ref):
        bk = bk_ref[...]
        for h in range(SW_H):
            acc = jnp.zeros((BLK, 2 * BLK), F32)
            for b in range(REL_B):
                acc = jnp.where(bk == b, rb_ref[b, h], acc)
            o_ref[h] = acc

    return pl.pallas_call(
        body, name=name, in_specs=[pl.BlockSpec(memory_space=pltpu.SMEM), pl.BlockSpec(memory_space=pltpu.VMEM)],
        out_specs=pl.BlockSpec(memory_space=pltpu.VMEM), out_shape=_sds((SW_H, BLK, 2 * BLK), F32),
        compiler_params=_params())(rel_bias, bucket)


def _bias_table_bwd(dbias_list, bucket, *, name):
    nl = len(dbias_list)

    def body(*refs):
        d_refs, bk_ref, o_ref = refs[:nl], refs[nl], refs[nl + 1]
        bk = bk_ref[...]
        rows = lax.broadcasted_iota(jnp.int32, (REL_B, 128), 0)
        lanes = lax.broadcasted_iota(jnp.int32, (REL_B, 128), 1)
        acc = jnp.zeros((REL_B, 128), F32)
        for h in range(SW_H):
            d = d_refs[0][h]
            for d_ref in d_refs[1:]:
                d = d + d_ref[h]
            for b in range(REL_B):
                t = jnp.sum(jnp.sum(jnp.where(bk == b, d, 0.0), axis=0, keepdims=True), axis=1, keepdims=True)
                acc = jnp.where((rows == b) & (lanes == h), t, acc)
        o_ref[...] = acc

    vm = pl.BlockSpec(memory_space=pltpu.VMEM)
    return pl.pallas_call(
        body, name=name, in_specs=[vm] * (nl + 1), out_specs=vm, out_shape=_sds((REL_B, 128), F32),
        compiler_params=_params())(*dbias_list, bucket)


def _swa_heads(q, kp, kc, vp, vc, bp, bc, sinks, has_prev):
    def rep(t):
        return jnp.concatenate([t[g:g + 1] for g in range(SW_H // SW_G) for _ in range(SW_G)], axis=0)

    r = lax.broadcasted_iota(jnp.int32, (SW_H, BLK, BLK), 1)
    c = lax.broadcasted_iota(jnp.int32, (SW_H, BLK, BLK), 2)
    hid = lax.broadcasted_iota(jnp.int32, (SW_H, 1, 128), 0)
    lane = lax.broadcasted_iota(jnp.int32, (SW_H, 1, 128), 2)
    sink = jnp.sum(jnp.where(lane == hid, jnp.broadcast_to(sinks[None], (SW_H, 1, 128)), 0.0), axis=2, keepdims=True)
    lp = jnp.where((c > r) & has_prev, dot_nt(q, rep(kp)) * (SW_DH ** -0.5) + bp, -jnp.inf)
    lc = jnp.where(c <= r, dot_nt(q, rep(kc)) * (SW_DH ** -0.5) + bc, -jnp.inf)
    mx = jnp.maximum(jnp.maximum(jnp.max(lp, axis=2, keepdims=True), jnp.max(lc, axis=2, keepdims=True)), sink)
    mx = lax.stop_gradient(mx)
    pp, pc = jnp.exp(lp - mx), jnp.exp(lc - mx)
    den = jnp.sum(pp, axis=2, keepdims=True) + jnp.sum(pc, axis=2, keepdims=True) + jnp.exp(sink - mx)
    return dot_nn(pp / den, rep(vp)) + dot_nn(pc / den, rep(vc))


def _qs(h, off=0):
    return slice(off + h * SW_DH, off + (h + 1) * SW_DH)


def _split(ref, n):
    return jnp.stack([ref[:, _qs(h)] for h in range(n)])


def _swa_fwd(proj, bias, sinks, *, name):
    S = proj.shape[0]
    nb = S // BLK
    nkv = SW_H // SW_G

    def body(q_ref, kp_ref, kc_ref, vp_ref, vc_ref, b_ref, s_ref, o_ref):
        out = _swa_heads(_split(q_ref, SW_H), _split(kp_ref, nkv), _split(kc_ref, nkv), _split(vp_ref, nkv), _split(vc_ref, nkv),
                         b_ref[:, :, :BLK], b_ref[:, :, BLK:], s_ref[...], pl.program_id(0) > 0)
        for h in range(SW_H):
            o_ref[:, _qs(h)] = out[h].astype(BF16)

    def cur(j):
        return pl.BlockSpec((BLK, 128), lambda n, j=j: (n, j))

    def prev(j):
        return pl.BlockSpec((BLK, 128), lambda n, j=j: (jnp.maximum(n - 1, 0), j))

    return pl.pallas_call(
        body, name=name, grid=(nb,),
        in_specs=[pl.BlockSpec((BLK, SW_W), lambda n: (n, 4)), prev(20), cur(20), prev(21), cur(21),
                  pl.BlockSpec((SW_H, BLK, 2 * BLK), lambda n: (0, 0, 0)), pl.BlockSpec((1, 128), lambda n: (0, 0))],
        out_specs=pl.BlockSpec((BLK, SW_W), lambda n: (n, 0)), out_shape=_sds((S, SW_W), BF16),
        compiler_params=_params(("parallel",)))(proj, proj, proj, proj, proj, bias, sinks)


def _swa_bwd(dh, proj, bias, sinks, *, name):
    S = proj.shape[0]
    nb = S // BLK

    nkv = SW_H // SW_G

    def body(dh_ref, q_ref, kp_ref, kc_ref, vp_ref, vc_ref, b_ref, s_ref, dsw_ref, db_ref, ds_ref, ck_s, cv_s):
        i = pl.program_id(0)

        @pl.when(i == 0)
        def _():
            ck_s[...] = jnp.zeros_like(ck_s)
            cv_s[...] = jnp.zeros_like(cv_s)
            db_ref[...] = jnp.zeros_like(db_ref)
            ds_ref[...] = jnp.zeros_like(ds_ref)

        f = functools.partial(_swa_heads, has_prev=i < nb - 1)
        _, vjp = jax.vjp(f, _split(q_ref, SW_H), _split(kp_ref, nkv), _split(kc_ref, nkv), _split(vp_ref, nkv),
                         _split(vc_ref, nkv), b_ref[:, :, :BLK], b_ref[:, :, BLK:], s_ref[...])
        dq, dkp, dkc, dvp, dvc, dbp, dbc, ds = vjp(_split(dh_ref, SW_H))
        for h in range(SW_H):
            dsw_ref[:, _qs(h)] = dq[h].astype(BF16)
        for g in range(nkv):
            dsw_ref[:, _qs(g, SW_W)] = (dkc[g] + ck_s[:, _qs(g)]).astype(BF16)
            dsw_ref[:, _qs(g, SW_W + SW_KVW)] = (dvc[g] + cv_s[:, _qs(g)]).astype(BF16)
            ck_s[:, _qs(g)] = dkp[g]
            cv_s[:, _qs(g)] = dvp[g]
        db_ref[:, :, :BLK] += dbp
        db_ref[:, :, BLK:] += dbc
        ds_ref[...] += ds

    def cur(j):
        return pl.BlockSpec((BLK, 128), lambda i, j=j: (nb - 1 - i, j))

    def prev(j):
        return pl.BlockSpec((BLK, 128), lambda i, j=j: (jnp.maximum(nb - 2 - i, 0), j))

    bsp = pl.BlockSpec((SW_H, BLK, 2 * BLK), lambda i: (0, 0, 0))
    vec = pl.BlockSpec((1, 128), lambda i: (0, 0))
    return pl.pallas_call(
        body, name=name, grid=(nb,),
        in_specs=[pl.BlockSpec((BLK, SW_W), lambda i: (nb - 1 - i, 1)), pl.BlockSpec((BLK, SW_W), lambda i: (nb - 1 - i, 4)),
                  prev(20), cur(20), prev(21), cur(21), bsp, vec],
        out_specs=[pl.BlockSpec((BLK, SW_GW), lambda i: (nb - 1 - i, 0)), bsp, vec],
        out_shape=[_sds((S, SW_GW), BF16), _sds((SW_H, BLK, 2 * BLK), F32), _sds((1, 128), F32)],
        scratch_shapes=[pltpu.VMEM((BLK, 128), F32)] * 2,
        compiler_params=_params(("arbitrary",)))(dh, proj, proj, proj, proj, proj, bias, sinks)


XA_TM = 512


def _xa_head(qh, kh, vh):
    logits = dot_nt(qh, kh) * (XA_DH ** -0.5)
    mx = lax.stop_gradient(jnp.max(logits, axis=1, keepdims=True))
    e = jnp.exp(logits - mx)
    return dot_nn(e / jnp.sum(e, axis=1, keepdims=True), vh)


def _xs(h, off=0):
    return slice(off + h * XA_DH, off + (h + 1) * XA_DH)


def _xattn_fwd(q, kv, *, name):
    S = q.shape[0]
    M = kv.shape[0]

    def body(q_ref, kv_ref, o_ref):
        for h in range(XA_H):
            o_ref[:, _xs(h)] = _xa_head(q_ref[:, _xs(h)], kv_ref[:, _xs(h)], kv_ref[:, _xs(h, D)]).astype(BF16)

    row = pl.BlockSpec((XA_TM, D), lambda i: (i, 0))
    return pl.pallas_call(
        body, name=name, grid=(S // XA_TM,), in_specs=[row, pl.BlockSpec((M, 2 * D), lambda i: (0, 0))], out_specs=row,
        out_shape=_sds((S, D), BF16), compiler_params=_params(("parallel",)))(q, kv)


def _xattn_bwd(do, q, kv, *, name):
    S = q.shape[0]
    M = kv.shape[0]

    def body(do_ref, q_ref, kv_ref, dq_ref, dkv_ref):
        @pl.when(pl.program_id(0) == 0)
        def _():
            dkv_ref[...] = jnp.zeros_like(dkv_ref)

        for h in range(XA_H):
            _, vjp = jax.vjp(_xa_head, q_ref[:, _xs(h)], kv_ref[:, _xs(h)], kv_ref[:, _xs(h, D)])
            dq, dk, dv = vjp(do_ref[:, _xs(h)])
            dq_ref[:, _xs(h)] = dq.astype(BF16)
            dkv_ref[:, _xs(h)] += dk
            dkv_ref[:, _xs(h, D)] += dv

    row = pl.BlockSpec((XA_TM, D), lambda i: (i, 0))
    full = pl.BlockSpec((M, 2 * D), lambda i: (0, 0))
    return pl.pallas_call(
        body, name=name, grid=(S // XA_TM,), in_specs=[row, row, full], out_specs=[row, full],
        out_shape=[_sds((S, D), BF16), _sds((M, 2 * D), F32)], compiler_params=_params(("arbitrary",)))(do, q, kv)


def _loss_head(y, tgt, *, name):
    S = y.shape[0]
    tm = 512

    def body(y_ref, t_ref, l_ref, dy_ref):
        e = y_ref[...] - t_ref[...]
        dy_ref[...] = e * (1.0 / D)
        part = 0.5 * jnp.sum(jnp.sum(e * e, axis=1, keepdims=True) * (1.0 / D), axis=0, keepdims=True)

        @pl.when(pl.program_id(0) == 0)
        def _():
            l_ref[...] = jnp.broadcast_to(part, (8, 128))

        @pl.when(pl.program_id(0) > 0)
        def _():
            l_ref[...] += jnp.broadcast_to(part, (8, 128))

    row = pl.BlockSpec((tm, D), lambda i: (i, 0))
    return pl.pallas_call(
        body, name=name, grid=(S // tm,), in_specs=[row, row], out_specs=[pl.BlockSpec((8, 128), lambda i: (0, 0)), row],
        out_shape=[_sds((8, 128), F32), _sds((S, D), F32)], compiler_params=_params(("arbitrary",)))(y, tgt)


ANY = pl.BlockSpec(memory_space=pl.ANY)


def _place():
    x, y, c = lax.axis_index("x"), lax.axis_index("y"), lax.axis_index("c")
    chips = [(1 - x, y), (x, 1 - y), (1 - x, 1 - y)]
    return x, y, c, chips


def _gather(arrs, *, name):
    n = len(arrs)

    def body(*refs):
        srcs, outs = refs[:n], refs[n:2 * n]
        send_sems, recv_sems, local_sems = refs[2 * n:]
        x, y, c, chips = _place()
        me, sib = (x, y, c), (x, y, 1 - c)

        def idx(p):
            return 4 * p[0] + 2 * p[1] + p[2]

        def copy(i, k, block, to, from_src=False):
            return pltpu.make_async_remote_copy(
                src_ref=srcs[i] if from_src else outs[i].at[idx(block)], dst_ref=outs[i].at[idx(block)],
                send_sem=send_sems.at[7 * i + k], recv_sem=recv_sems.at[7 * i + k], device_id=to, device_id_type=MESH)

        local = [pltpu.make_async_copy(srcs[i], outs[i].at[idx(me)], local_sems.at[i]) for i in range(n)]
        for cp in local:
            cp.start()
        first = []
        for i in range(n):
            first.append(copy(i, 0, me, sib, True))
            first += [copy(i, 1 + j, me, (*chip, c), True) for j, chip in enumerate(chips)]
        for cp in first:
            cp.start()
        passed = []
        for j, chip in enumerate(chips):
            for i in range(n):
                copy(i, 1 + j, (*chip, c), me).wait_recv()
                cp = copy(i, 4 + j, (*chip, c), sib)
                cp.start()
                passed.append(cp)
        for i in range(n):
            copy(i, 0, sib, me).wait_recv()
        for j, chip in enumerate(chips):
            for i in range(n):
                copy(i, 4 + j, (*chip, 1 - c), me).wait_recv()
        for cp in first + passed:
            cp.wait_send()
        for cp in local:
            cp.wait()

    return pl.pallas_call(
        body, name=name, in_specs=[ANY] * n, out_specs=[ANY] * n,
        out_shape=[_sds((N_DEV,) + a.shape, a.dtype) for a in arrs],
        scratch_shapes=[pltpu.SemaphoreType.DMA((7 * n,)), pltpu.SemaphoreType.DMA((7 * n,)), pltpu.SemaphoreType.DMA((n,))],
        compiler_params=pltpu.CompilerParams(has_side_effects=True))(*arrs)


def _swap_sibling(arrs, *, name):
    n = len(arrs)

    def body(*refs):
        srcs, outs = refs[:n], refs[n:2 * n]
        send_sems, recv_sems = refs[2 * n:]
        x, y, c, _ = _place()
        copies = [pltpu.make_async_remote_copy(
            src_ref=srcs[i].at[2 * j + (1 - c)], dst_ref=outs[i].at[j], send_sem=send_sems.at[N_CHIP * i + j],
            recv_sem=recv_sems.at[N_CHIP * i + j], device_id=(x, y, 1 - c), device_id_type=MESH)
            for i in range(n) for j in range(N_CHIP)]
        for cp in copies:
            cp.start()
        for cp in copies:
            cp.wait()

    return pl.pallas_call(
        body, name=name, in_specs=[ANY] * n, out_specs=[ANY] * n,
        out_shape=[_sds((N_CHIP,) + a.shape[1:], a.dtype) for a in arrs],
        scratch_shapes=[pltpu.SemaphoreType.DMA((N_CHIP * n,)), pltpu.SemaphoreType.DMA((N_CHIP * n,))],
        compiler_params=pltpu.CompilerParams(has_side_effects=True))(*arrs)


def _pair_sum(part, got, c_idx, *, name):
    _, R, C = part.shape
    tr = next(t for t in (512, 256, 128, 64, 32, 16) if R % t == 0)

    def body(c_ref, p_ref, g_ref, o_ref):
        o_ref[...] = (p_ref[0].astype(F32) + g_ref[...].astype(F32)).astype(o_ref.dtype)

    return pl.pallas_call(
        body, name=name,
        grid_spec=pltpu.PrefetchScalarGridSpec(
            num_scalar_prefetch=1, grid=(N_CHIP, R // tr),
            in_specs=[pl.BlockSpec((1, 1, tr, C), lambda j, r, c_ref: (j, c_ref[0], r, 0)),
                      pl.BlockSpec((1, tr, C), lambda j, r, c_ref: (j, r, 0))],
            out_specs=pl.BlockSpec((1, tr, C), lambda j, r, c_ref: (j, r, 0))),
        out_shape=_sds((N_CHIP, R, C), part.dtype),
        compiler_params=_params(("parallel", "parallel")))(c_idx, part.reshape(N_CHIP, 2, R, C), got)


def _swap_chips(arrs, *, name):
    n = len(arrs)

    def body(*refs):
        srcs, outs = refs[:n], refs[n:2 * n]
        send_sems, recv_sems, local_sems = refs[2 * n:]
        x, y, c, chips = _place()
        jme = 2 * x + y
        local = [pltpu.make_async_copy(srcs[i].at[jme], outs[i].at[jme], local_sems.at[i]) for i in range(n)]
        for cp in local:
            cp.start()
        copies = [pltpu.make_async_remote_copy(
            src_ref=srcs[i].at[2 * chip[0] + chip[1]], dst_ref=outs[i].at[jme], send_sem=send_sems.at[3 * i + j],
            recv_sem=recv_sems.at[3 * i + j], device_id=(*chip, c), device_id_type=MESH)
            for i in range(n) for j, chip in enumerate(chips)]
        for cp in copies:
            cp.start()
        for i in range(n):
            for j, chip in enumerate(chips):
                pltpu.make_async_remote_copy(
                    src_ref=srcs[i].at[jme], dst_ref=outs[i].at[2 * chip[0] + chip[1]], send_sem=send_sems.at[3 * i + j],
                    recv_sem=recv_sems.at[3 * i + j], device_id=(*chip, c), device_id_type=MESH).wait_recv()
        for cp in copies:
            cp.wait_send()
        for cp in local:
            cp.wait()

    return pl.pallas_call(
        body, name=name, in_specs=[ANY] * n, out_specs=[ANY] * n,
        out_shape=[_sds(a.shape, a.dtype) for a in arrs],
        scratch_shapes=[pltpu.SemaphoreType.DMA((3 * n,)), pltpu.SemaphoreType.DMA((3 * n,)), pltpu.SemaphoreType.DMA((n,))],
        compiler_params=pltpu.CompilerParams(has_side_effects=True))(*arrs)


HBM = pl.BlockSpec(memory_space=pltpu.HBM)
SEM = pl.BlockSpec(memory_space=pltpu.SEMAPHORE)
EFFECT = pltpu.SideEffectType.DATAFLOW_SIDE_EFFECTING


def _peer_copies(srcs, lands, send_sems, recv_sems):
    x, y, c, _ = _place()
    me = 4 * x + 2 * y + c
    out = []
    for i in range(len(srcs)):
        for j in range(1, N_DEV):
            px, py, pc = x ^ ((j >> 2) & 1), y ^ ((j >> 1) & 1), c ^ (j & 1)
            k = (N_DEV - 1) * i + j - 1
            out.append(tuple(pltpu.make_async_remote_copy(
                src_ref=srcs[i], dst_ref=lands[i].at[slot], send_sem=send_sems.at[k], recv_sem=recv_sems.at[k],
                device_id=(px, py, pc), device_id_type=MESH) for slot in (me, 4 * px + 2 * py + pc)))
    return out


def _chip_copies(srcs, lands, send_sems, recv_sems):
    x, y, c, chips = _place()
    jme = 2 * x + y
    out = []
    for i in range(len(srcs)):
        for j, chip in enumerate(chips):
            jt = 2 * chip[0] + chip[1]
            out.append(tuple(pltpu.make_async_remote_copy(
                src_ref=srcs[i].at[s], dst_ref=lands[i].at[d], send_sem=send_sems.at[3 * i + j], recv_sem=recv_sems.at[3 * i + j],
                device_id=(*chip, c), device_id_type=MESH) for s, d in ((jt, jme), (jme, jt))))
    return out


def _split_start(srcs, lands, pattern, ncopy, *, name):
    n = len(srcs)

    def body(*refs):
        for send, _ in pattern(refs[:n], refs[n:2 * n], refs[2 * n], refs[2 * n + 1]):
            send.start()
        refs[-1][...] = jnp.zeros_like(refs[-1])

    arrs = list(srcs) + list(lands)
    return pl.pallas_call(
        body, name=name,
        out_shape=(pltpu.SemaphoreType.DMA((ncopy,)), pltpu.SemaphoreType.DMA((ncopy,)),
                   *[pltpu.HBM(a.shape, a.dtype) for a in arrs], _sds((8, 128), F32)),
        in_specs=[HBM] * (2 * n), out_specs=(SEM, SEM, *[HBM] * (2 * n), pl.BlockSpec(memory_space=pltpu.VMEM)),
        input_output_aliases={i: 2 + i for i in range(2 * n)},
        compiler_params=pltpu.CompilerParams(has_side_effects=EFFECT))(
            *[pltpu.with_memory_space_constraint(a, pltpu.HBM) for a in arrs])


def _split_wait(started, after, pattern, *, name):
    send_sems, recv_sems, *arrs = started[:-1]
    n = len(arrs) // 2

    def body(*refs):
        for send, recv in pattern(refs[:n], refs[n:2 * n], refs[2 * n], refs[2 * n + 1]):
            send.wait_send()
            recv.wait_recv()

    outs = pl.pallas_call(
        body, name=name, out_shape=tuple(pltpu.HBM(a.shape, a.dtype) for a in arrs),
        in_specs=[HBM] * (2 * n) + [SEM, SEM, ANY], out_specs=tuple([HBM] * (2 * n)),
        input_output_aliases={i: i for i in range(2 * n)},
        compiler_params=pltpu.CompilerParams(has_side_effects=EFFECT))(*arrs, send_sems, recv_sems, after)
    return list(outs[n:])


def _adamw(parts, w, m, v, *, layer=None, prev=None, name):
    P, R, C = parts.shape
    tr = next((t for t in (256, 128, 64, 32, 16, 8) if R % t == 0), R)
    c1 = 1.0 / (1.0 - ADAM_B1 ** ADAM_STEP)
    c2 = 1.0 / (1.0 - ADAM_B2 ** ADAM_STEP)
    nprev = 0 if prev is None else 4

    def body(p_ref, w_ref, m_ref, v_ref, *rest):
        g_ref, d_ref, nm_ref, nv_ref = rest[nprev:]
        g = p_ref[0].astype(F32)
        for j in range(1, P):
            g = g + p_ref[j].astype(F32)
        g = g.reshape(w_ref.shape)
        nm = ADAM_B1 * m_ref[...] + (1.0 - ADAM_B1) * g
        nv = ADAM_B2 * v_ref[...] + (1.0 - ADAM_B2) * (g * g)
        g_ref[...] = g
        nm_ref[...] = nm
        nv_ref[...] = nv
        d_ref[...] = -ADAM_LR * ((nm * c1) / (jnp.sqrt(nv * c2) + ADAM_EPS) + ADAM_WD * w_ref[...])

    if layer is None:
        row = pl.BlockSpec((tr, C), lambda i: (i, 0))
    else:
        row = pl.BlockSpec((1, tr, C), lambda i: (layer, i, 0))
    out = _sds(w.shape, F32)
    return pl.pallas_call(
        body, name=name, grid=(R // tr,),
        in_specs=[pl.BlockSpec((P, tr, C), lambda i: (0, i, 0)), row, row, row] + [ANY] * nprev,
        out_specs=[row, row, row, row], out_shape=[out, out, out, out],
        input_output_aliases={4 + k: k for k in range(nprev)},
        compiler_params=_params(("parallel",)))(parts, w, m, v, *(prev or ()))


BIG = ("w_in", "w_out", "xa_wq", "xa_wkv", "xa_wo", "ffn_w_up", "ffn_w_down")
COL_SHARDED = ("w_in", "xa_wkv", "ffn_w_up", "ml_conv_w", "ffn_conv_w")
SHARDED_SMALL = ("ml_conv_w", "ffn_conv_w")
REPLICATED = ("rel_bias", "ml_conv_b", "ml_i_bias", "ml_f_bias", "ml_norm_g", "swa_sinks", "ln1_g", "ln1_b", "ln2_g", "ln2_b",
              "ffn_conv_b", "ln3_g", "ln3_b")
NAMES = ("rel_bias", "w_in", "ml_conv_w", "ml_conv_b", "ml_i_bias", "ml_f_bias", "ml_norm_g", "swa_sinks", "w_out", "ln1_g", "ln1_b",
         "xa_wq", "xa_wkv", "xa_wo", "ln2_g", "ln2_b", "ffn_w_up", "ffn_conv_w", "ffn_conv_b", "ffn_w_down", "ln3_g", "ln3_b")


def _flat_rows(a, mult):
    f = a.reshape(-1)
    n = -(-f.shape[0] // (128 * mult)) * (128 * mult)
    if n != f.shape[0]:
        f = jnp.pad(f, (0, n - f.shape[0]))
    return f.reshape(-1, 128)


def _pack(arrs, mult):
    parts = [_flat_rows(a, mult) for a in arrs]
    return jnp.concatenate(parts, axis=0), [p.shape[0] for p in parts]


def _unpack(flat, rows, shapes):
    out, off = [], 0
    lead = flat.shape[:-2]
    for r, shp in zip(rows, shapes):
        n = int(np.prod(shp))
        piece = flat[..., off:off + r, :].reshape(lead + (r * 128,))[..., :n]
        out.append(piece.reshape(lead + tuple(shp)))
        off += r
    return out


def _full_from_shards(stacked, name):
    if name in COL_SHARDED:
        return jnp.moveaxis(stacked, 0, 2).reshape(stacked.shape[1], stacked.shape[2], N_DEV * stacked.shape[3])
    return jnp.moveaxis(stacked, 0, 1).reshape(stacked.shape[1], N_DEV * stacked.shape[2], stacked.shape[3])


def _shards_from_full(full, name):
    L, A, B = full.shape
    if name in COL_SHARDED:
        return jnp.moveaxis(full.reshape(L, A, N_DEV, B // N_DEV), 2, 0)
    return jnp.moveaxis(full.reshape(L, N_DEV, A // N_DEV, B), 1, 0)


def _pad_win(w):
    z = jnp.zeros(w.shape[:-1] + (NP_IN - N_IN,), w.dtype)
    return jnp.concatenate([w[..., :2048], w[..., 2056:], w[..., 2048:2056], z], axis=-1)


def _row128(v):
    return jnp.pad(v, (0, 128 - v.shape[0])).reshape(1, 128)


REST = BIG[1:]


def _layer_full(stacked, n):
    if n in COL_SHARDED:
        full = jnp.moveaxis(stacked, 0, 1).reshape(stacked.shape[1], N_DEV * stacked.shape[2])
    else:
        full = stacked.reshape(N_DEV * stacked.shape[1], stacked.shape[2])
    return _pad_win(full) if n == "w_in" else full


def _layer_shards(g, n):
    A, B = g.shape
    if n in COL_SHARDED:
        return jnp.moveaxis(g.reshape(A, N_DEV, B // N_DEV), 1, 0).astype(BF16)
    return g.reshape(N_DEV, A // N_DEV, B).astype(BF16)


def _with_own_block(a, idx, nblk):
    return lax.dynamic_update_slice(jnp.zeros((nblk,) + a.shape, a.dtype), a[None], (idx,) + (0,) * a.ndim)


def _gather_start(arrs, me, *, name):
    return _split_start(arrs, [_with_own_block(a, me, N_DEV) for a in arrs], _peer_copies, (N_DEV - 1) * len(arrs), name=name)


def _reduce_chips(tag, names, grads, c_idx, jme, *, split):
    send = [_layer_shards(g, n) for n, g in zip(names, grads)]
    got = _swap_sibling(send, name=f"swap_sibling_{tag}")
    sums = [_pair_sum(s, g, c_idx, name=f"pair_sum_{tag}_{n}") for n, s, g in zip(names, send, got)]
    if not split:
        return _swap_chips(sums, name=f"swap_chips_{tag}")
    lands = [_with_own_block(lax.dynamic_index_in_dim(q, jme, 0, keepdims=False), jme, N_CHIP) for q in sums]
    return _split_start(sums, lands, _chip_copies, 3 * len(sums), name=f"swap_chips_{tag}_start")


def _update_small(Gf, loss_part, W, Mo, Vo, me):
    res = {}
    small = REPLICATED + SHARDED_SMALL
    sp_flat, sp_rows = _pack([Gf[n] for n in small] + [loss_part], 8)
    sp_all = _gather([sp_flat], name="gather_small_grads")[0]

    def widen(n, t):
        if n not in SHARDED_SMALL:
            return t[n]
        return lax.dynamic_update_slice(jnp.zeros(Gf[n].shape, F32), t[n], (0, 0, me * t[n].shape[2]))

    zl = jnp.zeros((8, 128), F32)
    wsm, _ = _pack([widen(n, W) for n in small] + [zl], 8)
    msm, _ = _pack([widen(n, Mo) for n in small] + [zl], 8)
    vsm, _ = _pack([widen(n, Vo) for n in small] + [zl], 8)
    outs_small = [_unpack(o_, sp_rows, [Gf[n].shape for n in small] + [(8, 128)])
                  for o_ in _adamw(sp_all, wsm, msm, vsm, name="adamw_small")]
    for kind, os_ in zip(("g", "d", "m", "v"), outs_small):
        for n, a in zip(small, os_[:-1]):
            if n in SHARDED_SMALL:
                a = lax.dynamic_slice(a, (0, 0, me * W[n].shape[2]), W[n].shape)
            res[kind, n] = a
    return res, outs_small[0][-1][0, 0]


def kernel(x, mem, rel_bias, w_in, ml_conv_w, ml_conv_b, ml_i_bias, ml_f_bias, ml_norm_g, swa_sinks, w_out, ln1_g, ln1_b, xa_wq, xa_wkv, xa_wo, ln2_g, ln2_b, ffn_w_up, ffn_conv_w, ffn_conv_b, ffn_w_down, ln3_g, ln3_b, loss_target, m_rel_bias, m_w_in, m_ml_conv_w, m_ml_conv_b, m_ml_i_bias, m_ml_f_bias, m_ml_norm_g, m_swa_sinks, m_w_out, m_ln1_g, m_ln1_b, m_xa_wq, m_xa_wkv, m_xa_wo, m_ln2_g, m_ln2_b, m_ffn_w_up, m_ffn_conv_w, m_ffn_conv_b, m_ffn_w_down, m_ln3_g, m_ln3_b, v_rel_bias, v_w_in, v_ml_conv_w, v_ml_conv_b, v_ml_i_bias, v_ml_f_bias, v_ml_norm_g, v_swa_sinks, v_w_out, v_ln1_g, v_ln1_b, v_xa_wq, v_xa_wkv, v_xa_wo, v_ln2_g, v_ln2_b, v_ffn_w_up, v_ffn_conv_w, v_ffn_conv_b, v_ffn_w_down, v_ln3_g, v_ln3_b):
    W = dict(rel_bias=rel_bias, w_in=w_in, ml_conv_w=ml_conv_w, ml_conv_b=ml_conv_b, ml_i_bias=ml_i_bias, ml_f_bias=ml_f_bias,
             ml_norm_g=ml_norm_g, swa_sinks=swa_sinks, w_out=w_out, ln1_g=ln1_g, ln1_b=ln1_b, xa_wq=xa_wq, xa_wkv=xa_wkv,
             xa_wo=xa_wo, ln2_g=ln2_g, ln2_b=ln2_b, ffn_w_up=ffn_w_up, ffn_conv_w=ffn_conv_w, ffn_conv_b=ffn_conv_b,
             ffn_w_down=ffn_w_down, ln3_g=ln3_g, ln3_b=ln3_b)
    Mo = dict(rel_bias=m_rel_bias, w_in=m_w_in, ml_conv_w=m_ml_conv_w, ml_conv_b=m_ml_conv_b, ml_i_bias=m_ml_i_bias,
              ml_f_bias=m_ml_f_bias, ml_norm_g=m_ml_norm_g, swa_sinks=m_swa_sinks, w_out=m_w_out, ln1_g=m_ln1_g, ln1_b=m_ln1_b,
              xa_wq=m_xa_wq, xa_wkv=m_xa_wkv, xa_wo=m_xa_wo, ln2_g=m_ln2_g, ln2_b=m_ln2_b, ffn_w_up=m_ffn_w_up,
              ffn_conv_w=m_ffn_conv_w, ffn_conv_b=m_ffn_conv_b, ffn_w_down=m_ffn_w_down, ln3_g=m_ln3_g, ln3_b=m_ln3_b)
    Vo = dict(rel_bias=v_rel_bias, w_in=v_w_in, ml_conv_w=v_ml_conv_w, ml_conv_b=v_ml_conv_b, ml_i_bias=v_ml_i_bias,
              ml_f_bias=v_ml_f_bias, ml_norm_g=v_ml_norm_g, swa_sinks=v_swa_sinks, w_out=v_w_out, ln1_g=v_ln1_g, ln1_b=v_ln1_b,
              xa_wq=v_xa_wq, xa_wkv=v_xa_wkv, xa_wo=v_xa_wo, ln2_g=v_ln2_g, ln2_b=v_ln2_b, ffn_w_up=v_ffn_w_up,
              ffn_conv_w=v_ffn_conv_w, ffn_conv_b=v_ffn_conv_b, ffn_w_down=v_ffn_w_down, ln3_g=v_ln3_g, ln3_b=v_ln3_b)
    S = x.shape[1]
    c_me = lax.axis_index("c")
    me = 4 * lax.axis_index("x") + 2 * lax.axis_index("y") + c_me
    c_idx = jnp.reshape(c_me, (1,)).astype(jnp.int32)
    xs = x.reshape(S, D)
    mems = mem.reshape(mem.shape[1], D)
    tgt = loss_target.reshape(S, D)

    jme = 2 * lax.axis_index("x") + lax.axis_index("y")

    sm_flat, sm_rows = _pack([W[n] for n in SHARDED_SMALL], 8)
    first = _gather([w_in[0].astype(BF16), sm_flat], name="gather_first")
    rest0 = _gather_start([W[n][0].astype(BF16) for n in REST], me, name="gather_rest0_start")
    layer1 = _gather_start([W[n][1].astype(BF16) for n in BIG], me, name="gather_layer1_start")
    full = [{"w_in": _layer_full(first[0], "w_in")}, None]
    conv_w = {n: _full_from_shards(s, n) for n, s in zip(SHARDED_SMALL, _unpack(first[1], sm_rows, [W[n].shape for n in SHARDED_SMALL]))}

    bucket = jnp.asarray(_t5_buckets())
    bias = _bias_table(rel_bias, bucket, name="bias_table")

    saved = []
    h0, h0b = xs, xs.astype(BF16)
    for l in range(DEPTH):
        gbias = _row128(jnp.concatenate([ml_i_bias[l], ml_f_bias[l]]))
        sinks = _row128(swa_sinks[l])
        ng = ml_norm_g[l].reshape(1, ML_W)
        proj = _mm(h0b, full[l]["w_in"], dep=(rest0[-1], layer1[-1]) if l == 0 else (), name=f"proj{l}")
        qk = _silu_conv_fwd(proj, conv_w["ml_conv_w"][l], ml_conv_b[l].reshape(1, -1), name=f"mlconv{l}")
        h_ml, cs, ns, ms = _mlstm_fwd(qk, proj, gbias, ng, name=f"mlstm{l}")
        h_sw = _swa_fwd(proj, bias, sinks, name=f"swa{l}")
        if l == 0:
            landed = _split_wait(rest0, h_sw, _peer_copies, name="gather_rest0_wait")
            full[0].update({n: _layer_full(s, n) for n, s in zip(REST, landed)})
        fw = full[l]
        h1, h1b, z1 = _mm_res_ln([h_ml, h_sw], fw["w_out"], h0, ln1_g[l].reshape(1, D), ln1_b[l].reshape(1, D), name=f"mix_out{l}")
        q = _mm(h1b, fw["xa_wq"], name=f"xa_q{l}")
        kv = _mm(mems, fw["xa_wkv"], tm=256, name=f"xa_kv{l}")
        o = _xattn_fwd(q, kv, name=f"xattn{l}")
        h2, h2b, z2 = _mm_res_ln([o], fw["xa_wo"], h1, ln2_g[l].reshape(1, D), ln2_b[l].reshape(1, D), name=f"xa_out{l}")
        up = _mm(h2b, fw["ffn_w_up"], name=f"ffn_up{l}")
        act = _ffn_gate_fwd(up, conv_w["ffn_conv_w"][l], ffn_conv_b[l].reshape(1, -1), name=f"ffn_gate{l}")
        h3, h3b, z3 = _mm_res_ln([act], fw["ffn_w_down"], h2, ln3_g[l].reshape(1, D), ln3_b[l].reshape(1, D), name=f"ffn_out{l}")
        saved.append(dict(h0b=h0b, proj=proj, qk=qk, cs=cs, ns=ns, ms=ms, h_ml=h_ml, h_sw=h_sw, z1=z1, h1b=h1b, q=q, kv=kv, o=o,
                          z2=z2, h2b=h2b, up=up, act=act, z3=z3, gbias=gbias, sinks=sinks, ng=ng))
        h0, h0b = h3, h3b
        if l == 0:
            landed = _split_wait(layer1, h3b, _peer_copies, name="gather_layer1_wait")
            full[1] = {n: _layer_full(s, n) for n, s in zip(BIG, landed)}

    loss_part, dh = _loss_head(h0, tgt, name="loss_head")

    G = {n: [None] * DEPTH for n in NAMES if n != "rel_bias"}
    dbias = [None] * DEPTH
    dep, pending = (), []
    for l in reversed(range(DEPTH)):
        sv, fw = saved[l], full[l]
        win = fw["w_in"]
        dz3, dz3b, G["ln3_g"][l], G["ln3_b"][l] = _ln_bwd(dh, sv["z3"], ln3_g[l].reshape(1, D), dep=dep, name=f"ln3_bwd{l}")
        G["ffn_w_down"][l] = _mm_tn(sv["act"], dz3b, name=f"d_w_down{l}")
        dact = _mm(dz3b, fw["ffn_w_down"], trans_b=True, name=f"d_act{l}")
        dupg, dupv, G["ffn_conv_w"][l], G["ffn_conv_b"][l] = _ffn_gate_bwd(
            dact, sv["up"], conv_w["ffn_conv_w"][l], ffn_conv_b[l].reshape(1, -1), name=f"ffn_gate_bwd{l}")
        G["ffn_w_up"][l] = jnp.concatenate([_mm_tn(sv["h2b"], dupg, name=f"d_w_up_g{l}"),
                                            _mm_tn(sv["h2b"], dupv, name=f"d_w_up_v{l}")], axis=1)
        dep = ()
        if l == 0:
            names = ("ffn_w_up", "ffn_w_down")
            pending.append((names, 0, _reduce_chips("ffn0", names, [G[n][0] for n in names], c_idx, jme, split=True)))
            dep = (pending[-1][2][-1],)
        dh2 = _mm([dupg, dupv], fw["ffn_w_up"], trans_b=True, add=dz3, add_scale=ALPHA, dep=dep, name=f"d_h2_{l}")

        dz2, dz2b, G["ln2_g"][l], G["ln2_b"][l] = _ln_bwd(dh2, sv["z2"], ln2_g[l].reshape(1, D), name=f"ln2_bwd{l}")
        G["xa_wo"][l] = _mm_tn(sv["o"], dz2b, name=f"d_xa_wo{l}")
        do = _mm(dz2b, fw["xa_wo"], trans_b=True, name=f"d_xa_o{l}")
        dq, dkv = _xattn_bwd(do, sv["q"], sv["kv"], name=f"xattn_bwd{l}")
        G["xa_wkv"][l] = _mm_tn(mems, dkv, name=f"d_xa_wkv{l}")
        G["xa_wq"][l] = _mm_tn(sv["h1b"], dq, name=f"d_xa_wq{l}")
        dh1 = _mm(dq, fw["xa_wq"], trans_b=True, add=dz2, add_scale=ALPHA, name=f"d_h1_{l}")

        dz1, dz1b, G["ln1_g"][l], G["ln1_b"][l] = _ln_bwd(dh1, sv["z1"], ln1_g[l].reshape(1, D), name=f"ln1_bwd{l}")
        G["w_out"][l] = jnp.concatenate([_mm_tn(sv["h_ml"], dz1b, name=f"d_w_out_ml{l}"),
                                         _mm_tn(sv["h_sw"], dz1b, name=f"d_w_out_sw{l}")], axis=0)
        dhcat = _mm(dz1b, fw["w_out"], trans_b=True, name=f"d_hcat{l}")
        dsw, dbias[l], dsinks = _swa_bwd(dhcat, sv["proj"], bias, sv["sinks"], name=f"swa_bwd{l}")
        dqk, dml, dgb, dng = _mlstm_bwd(dhcat, sv["qk"], sv["proj"], sv["gbias"], sv["ng"], sv["cs"], sv["ns"], sv["ms"],
                                        name=f"mlstm_bwd{l}")
        dqk_pre, G["ml_conv_w"][l], G["ml_conv_b"][l] = _silu_conv_bwd(
            dqk, sv["proj"], conv_w["ml_conv_w"][l], ml_conv_b[l].reshape(1, -1), name=f"mlconv_bwd{l}")
        dw_qk = _mm_tn(sv["h0b"], dqk_pre, name=f"d_w_in_qk{l}")
        dw_ml = _mm_tn(sv["h0b"], dml, name=f"d_w_in_ml{l}")
        dw_sw = _mm_tn(sv["h0b"], dsw, name=f"d_w_in_sw{l}")
        G["w_in"][l] = jnp.concatenate([dw_qk, dw_ml[:, :2 * ML_W + 2 * ML_H], dw_sw], axis=1)
        win_ml = jnp.concatenate([win[:, 1024:2048], win[:, 2816:2944]], axis=1)
        dh = _mm(dqk_pre, win[:, :1024], trans_b=True, add=dz1, add_scale=ALPHA, name=f"d_h0_qk{l}")
        dh = _mm(dml, win_ml, trans_b=True, add=dh, name=f"d_h0_ml{l}")
        dh = _mm(dsw, win[:, 2048:2816], trans_b=True, add=dh, name=f"d_h0_sw{l}")
        G["ml_i_bias"][l] = dgb[0, :ML_H]
        G["ml_f_bias"][l] = dgb[0, ML_H:2 * ML_H]
        G["ml_norm_g"][l] = dng
        G["swa_sinks"][l] = dsinks[0, :SW_H]
        if l == 1:
            pending.append((BIG, 1, _reduce_chips("l1", BIG, [G[n][1] for n in BIG], c_idx, jme, split=True)))
            dep = (pending[-1][2][-1],)
    grad_x = dh.reshape(x.shape)

    names = ("w_in", "w_out", "xa_wq", "xa_wkv", "xa_wo")
    parts = {(n, 0): p for n, p in zip(names, _reduce_chips("rest0", names, [G[n][0] for n in names], c_idx, jme, split=False))}
    for names, l, started in pending:
        landed = _split_wait(started, parts["w_in", 0], _chip_copies, name=f"swap_chips_{'l1' if l else 'ffn0'}_wait")
        parts.update({(n, l): p for n, p in zip(names, landed)})
    res = {}
    for n in BIG:
        outs = None
        for l in reversed(range(DEPTH)):
            outs = _adamw(parts[n, l], W[n], Mo[n], Vo[n], layer=l, prev=outs, name=f"adamw_{n}{l}")
        for kind, a in zip(("g", "d", "m", "v"), outs):
            res[kind, n] = a

    Gf = {n: jnp.stack([g.reshape(W[n].shape[1:]) if n in REPLICATED else g for g in G[n]]) for n in G if n not in BIG}
    Gf["rel_bias"] = _bias_table_bwd(dbias, bucket, name="bias_table_bwd")[:, :SW_H]
    res_small, loss = _update_small(Gf, loss_part, W, Mo, Vo, me)
    res.update(res_small)
    return (loss, grad_x, *[res["g", n] for n in NAMES], *[res["d", n] for n in NAMES], *[res["m", n] for n in NAMES],
            *[res["v", n] for n in NAMES])
```

```python
import functools
import math

import jax
import jax.numpy as jnp
import numpy as np
from jax import lax
from jax.experimental import pallas as pl
from jax.experimental.pallas import tpu as pltpu

F32 = jnp.float32
BF16 = jnp.bfloat16

N_DEV = 8
N_CHIP = 4
D = 1024
DEPTH = 2
ML_H = 4
ML_W = 512
ML_DH = 128
ML_L = 64
ML_CONV = 4
SW_DH = 64
SW_W = 512
SW_H = 8
SW_G = 4
SW_KVW = 128
BLK = 128
REL_B = 32
REL_MAXD = 128
XA_H = 4
XA_DH = 256
DFF = 2816
NB_FF = DFF // 128
FFN_CONV = 3
ALPHA = (2.0 * DEPTH) ** 0.25
EPS = 1e-5
N_IN = 2824
NP_IN = 3072
ML_GW = 2 * ML_W + 128
SW_GW = SW_W + 2 * SW_KVW
ADAM_LR = 0.001
ADAM_B1 = 0.9
ADAM_B2 = 0.999
ADAM_EPS = 1e-08
ADAM_WD = 0.01
ADAM_STEP = 10
VMEM_LIMIT = 56 * 1024 * 1024
MESH = pl.DeviceIdType.MESH

NN = ((1,), (0,))
NT = ((1,), (1,))
TN = ((0,), (0,))


def _dg(a, b, dn):
    if a.ndim == 3:
        dims = (((dn[0][0] + 1,), (dn[1][0] + 1,)), ((0,), (0,)))
    else:
        dims = (dn, ((), ()))
    return lax.dot_general(a.astype(BF16), b.astype(BF16), dims, preferred_element_type=F32)


@jax.custom_vjp
def dot_nn(a, b):
    return _dg(a, b, NN)


dot_nn.defvjp(lambda a, b: (_dg(a, b, NN), (a, b)), lambda r, g: (_dg(g, r[1], NT), _dg(r[0], g, TN)))


@jax.custom_vjp
def dot_nt(a, b):
    return _dg(a, b, NT)


dot_nt.defvjp(lambda a, b: (_dg(a, b, NT), (a, b)), lambda r, g: (_dg(g, r[1], NN), _dg(g, r[0], TN)))


@jax.custom_vjp
def dot_tn(a, b):
    return _dg(a, b, TN)


dot_tn.defvjp(lambda a, b: (_dg(a, b, TN), (a, b)), lambda r, g: (_dg(r[1], g, NT), _dg(r[0], g, NN)))


def _params(sem=None):
    return pltpu.CompilerParams(dimension_semantics=sem, vmem_limit_bytes=VMEM_LIMIT)


def _sds(shape, dtype):
    return jax.ShapeDtypeStruct(tuple(shape), dtype)


TOKEN = pl.BlockSpec((8, 128), lambda *_: (0, 0))


def _mm(a, b, *, trans_b=False, out_dtype=F32, add=None, add_scale=1.0, tm=1024, tn=512, dep=(), name):
    a_list = list(a) if isinstance(a, (list, tuple)) else [a]
    M = a_list[0].shape[0]
    N = b.shape[0] if trans_b else b.shape[1]
    tm = min(tm, M)
    tn = next(t for t in (tn, 384, 256, 128) if N % t == 0)
    assert M % tm == 0
    Ka = a_list[0].shape[1]
    assert all(t.shape[1] == Ka for t in a_list)
    tk = next(t for t in (Ka, 1408, 1024) if Ka % t == 0 and t <= 1408)
    na, npa = len(a_list), Ka // tk
    nk = na * npa
    has_add = add is not None

    def body(*refs):
        a_refs, b_ref = refs[:na], refs[na]
        add_ref = refs[na + 1] if has_add else None
        o_ref, acc_ref = refs[-2], refs[-1]
        k = pl.program_id(2)

        def finish(r):
            if has_add:
                r = r + add_scale * add_ref[...].astype(F32)
            o_ref[...] = r.astype(out_dtype)

        for t, a_ref in enumerate(a_refs):
            def step(a_ref=a_ref):
                p = _dg(a_ref[...], b_ref[...], NT if trans_b else NN)
                if nk == 1:
                    finish(p)
                    return

                @pl.when(k == 0)
                def _():
                    acc_ref[...] = p

                @pl.when((k > 0) & (k < nk - 1))
                def _():
                    acc_ref[...] += p

                @pl.when(k == nk - 1)
                def _():
                    finish(acc_ref[...] + p)

            if na == 1:
                step()
            else:
                pl.when((k >= t * npa) & (k < (t + 1) * npa))(step)

    in_specs = [pl.BlockSpec((tm, tk), lambda i, j, k, t=t: (i, jnp.clip(k - t * npa, 0, npa - 1))) for t in range(na)]
    in_specs.append(pl.BlockSpec((tn, tk), lambda i, j, k: (j, k)) if trans_b else pl.BlockSpec((tk, tn), lambda i, j, k: (k, j)))
    args = a_list + [b]
    if has_add:
        in_specs.append(pl.BlockSpec((tm, tn), lambda i, j, k: (i, j)))
        args.append(add)
    in_specs += [TOKEN] * len(dep)
    args += list(dep)
    return pl.pallas_call(
        body, name=name, grid=(M // tm, N // tn, nk), in_specs=in_specs,
        out_specs=pl.BlockSpec((tm, tn), lambda i, j, k: (i, j)), out_shape=_sds((M, N), out_dtype),
        scratch_shapes=[pltpu.VMEM((tm, tn) if nk > 1 else (8, 128), F32)],
        compiler_params=_params(("parallel", "parallel", "arbitrary")))(*args)


def _mm_tn(a, g, *, name):
    S, K = a.shape
    N = g.shape[1]
    tk = K if K <= 1024 else K // 2
    tn = next(t for t in range(1536, 0, -128) if N % t == 0)
    ts = min(512, S)
    ns = S // ts
    assert K % tk == 0 and S % ts == 0

    def body(a_ref, g_ref, o_ref):
        s = pl.program_id(2)
        p = _dg(a_ref[...], g_ref[...], TN)

        @pl.when(s == 0)
        def _():
            o_ref[...] = p

        @pl.when(s > 0)
        def _():
            o_ref[...] += p

    return pl.pallas_call(
        body, name=name, grid=(K // tk, N // tn, ns),
        in_specs=[pl.BlockSpec((ts, tk), lambda i, j, s: (s, i)), pl.BlockSpec((ts, tn), lambda i, j, s: (s, j))],
        out_specs=pl.BlockSpec((tk, tn), lambda i, j, s: (i, j)), out_shape=_sds((K, N), F32),
        compiler_params=_params(("parallel", "parallel", "arbitrary")))(a, g)


def _mm_res_ln(a_list, w, resid, gam, bet, *, name):
    M, Ka = a_list[0].shape
    na = len(a_list)
    assert w.shape[0] == na * Ka
    tm = 256

    def body(*refs):
        a_refs, w_refs = refs[:na], refs[na:2 * na]
        r_ref, g_ref, b_ref, y_ref, yb_ref, z_ref = refs[2 * na:]
        z = ALPHA * r_ref[...]
        for a_ref, w_ref in zip(a_refs, w_refs):
            z = z + _dg(a_ref[...], w_ref[...], NN)
        mu = jnp.mean(z, axis=1, keepdims=True)
        zc = z - mu
        var = jnp.mean(zc * zc, axis=1, keepdims=True)
        y = zc * lax.rsqrt(var + EPS) * g_ref[...] + b_ref[...]
        y_ref[...] = y
        yb_ref[...] = y.astype(BF16)
        z_ref[...] = z

    row = pl.BlockSpec((tm, D), lambda i: (i, 0))
    vec = pl.BlockSpec((1, D), lambda i: (0, 0))
    a_specs = [pl.BlockSpec((tm, Ka), lambda i: (i, 0)) for _ in a_list]
    w_specs = [pl.BlockSpec((Ka, D), lambda i, t=t: (t, 0)) for t in range(na)]
    return pl.pallas_call(
        body, name=name, grid=(M // tm,), in_specs=a_specs + w_specs + [row, vec, vec],
        out_specs=[row, row, row], out_shape=[_sds((M, D), F32), _sds((M, D), BF16), _sds((M, D), F32)],
        compiler_params=_params(("parallel",)))(*a_list, *([w] * na), resid, gam, bet)


def _ln_bwd(dy, z, gam, *, dep=(), name):
    M = dy.shape[0]
    tm = 512

    def body(dy_ref, z_ref, g_ref, *rest):
        dz_ref, dzb_ref, dg_ref, db_ref = rest[len(dep):]
        i = pl.program_id(0)
        z = z_ref[...]
        dy_ = dy_ref[...]
        mu = jnp.mean(z, axis=1, keepdims=True)
        zc = z - mu
        var = jnp.mean(zc * zc, axis=1, keepdims=True)
        rstd = lax.rsqrt(var + EPS)
        xh = zc * rstd
        dxh = dy_ * g_ref[...]
        m1 = jnp.mean(dxh, axis=1, keepdims=True)
        m2 = jnp.mean(dxh * xh, axis=1, keepdims=True)
        dz = rstd * (dxh - m1 - xh * m2)
        dz_ref[...] = dz
        dzb_ref[...] = dz.astype(BF16)
        pg = jnp.sum(dy_ * xh, axis=0, keepdims=True)
        pb = jnp.sum(dy_, axis=0, keepdims=True)

        @pl.when(i == 0)
        def _():
            dg_ref[...] = pg
            db_ref[...] = pb

        @pl.when(i > 0)
        def _():
            dg_ref[...] += pg
            db_ref[...] += pb

    row = pl.BlockSpec((tm, D), lambda i: (i, 0))
    vec = pl.BlockSpec((1, D), lambda i: (0, 0))
    return pl.pallas_call(
        body, name=name, grid=(M // tm,), in_specs=[row, row, vec] + [TOKEN] * len(dep), out_specs=[row, row, vec, vec],
        out_shape=[_sds((M, D), F32), _sds((M, D), BF16), _sds((1, D), F32), _sds((1, D), F32)],
        compiler_params=_params(("arbitrary",)))(dy, z, gam, *dep)


def _shift_down(x, d):
    if d == 0:
        return x
    rows = lax.broadcasted_iota(jnp.int32, x.shape, 0)
    return jnp.where(rows >= d, pltpu.roll(x, d, 0), 0.0)


def _shift_up(x, d):
    if d == 0:
        return x
    S = x.shape[0]
    rows = lax.broadcasted_iota(jnp.int32, x.shape, 0)
    return jnp.where(rows < S - d, pltpu.roll(x, S - d, 0), 0.0)


def _conv(x, w_ref, b_ref, cs, K):
    y = b_ref[:, cs]
    for j in range(K):
        y = y + _shift_down(x, K - 1 - j) * w_ref[j:j + 1, cs]
    return y


def _conv_bwd(dy, x, w_ref, dw_ref, db_ref, cs, K):
    dx = jnp.zeros_like(x)
    for j in range(K):
        dx = dx + _shift_up(dy, K - 1 - j) * w_ref[j:j + 1, cs]
        dw_ref[j:j + 1, cs] = jnp.sum(dy * _shift_down(x, K - 1 - j), axis=0, keepdims=True)
    db_ref[:, cs] = jnp.sum(dy, axis=0, keepdims=True)
    return dx


ALL = slice(None)


def _silu_conv_fwd(proj, cw, cb, *, name):
    S = proj.shape[0]

    def body(x_ref, w_ref, b_ref, o_ref):
        o_ref[...] = jax.nn.silu(_conv(x_ref[...], w_ref, b_ref, ALL, ML_CONV))

    col = pl.BlockSpec((S, 128), lambda j: (0, j))
    return pl.pallas_call(
        body, name=name, grid=(8,),
        in_specs=[col, pl.BlockSpec((ML_CONV, 128), lambda j: (0, j)), pl.BlockSpec((1, 128), lambda j: (0, j))],
        out_specs=col, out_shape=_sds((S, 2 * ML_W), F32), compiler_params=_params(("parallel",)))(proj, cw, cb)


def _silu_conv_bwd(dqk, proj, cw, cb, *, name):
    S = proj.shape[0]

    def body(d_ref, x_ref, w_ref, b_ref, dx_ref, dw_ref, db_ref):
        x = x_ref[...]
        y = _conv(x, w_ref, b_ref, ALL, ML_CONV)
        dy = jax.vjp(jax.nn.silu, y)[1](d_ref[...])[0]
        dx_ref[...] = _conv_bwd(dy, x, w_ref, dw_ref, db_ref, ALL, ML_CONV).astype(BF16)

    col = pl.BlockSpec((S, 128), lambda j: (0, j))
    wsp = pl.BlockSpec((ML_CONV, 128), lambda j: (0, j))
    bsp = pl.BlockSpec((1, 128), lambda j: (0, j))
    return pl.pallas_call(
        body, name=name, grid=(8,), in_specs=[col, col, wsp, bsp], out_specs=[col, wsp, bsp],
        out_shape=[_sds((S, 2 * ML_W), BF16), _sds((ML_CONV, 2 * ML_W), F32), _sds((1, 2 * ML_W), F32)],
        compiler_params=_params(("parallel",)))(dqk, proj, cw, cb)


def _gate(ug, uv):
    return jax.nn.gelu(ug) * uv


def _ffn_specs(S):
    return (pl.BlockSpec((S, 128), lambda j: (0, j)), pl.BlockSpec((S, 128), lambda j: (0, j + NB_FF)),
            pl.BlockSpec((FFN_CONV, 128), lambda j: (0, j)), pl.BlockSpec((FFN_CONV, 128), lambda j: (0, j + NB_FF)),
            pl.BlockSpec((1, 128), lambda j: (0, j)), pl.BlockSpec((1, 128), lambda j: (0, j + NB_FF)))


def _ffn_gate_fwd(up, cw, cb, *, name):
    S = up.shape[0]
    cg, cv, wg, wv, bg, bv = _ffn_specs(S)

    def body(g_ref, v_ref, wg_ref, wv_ref, bg_ref, bv_ref, o_ref):
        ug = _conv(g_ref[...], wg_ref, bg_ref, ALL, FFN_CONV)
        uv = _conv(v_ref[...], wv_ref, bv_ref, ALL, FFN_CONV)
        o_ref[...] = _gate(ug, uv).astype(BF16)

    return pl.pallas_call(
        body, name=name, grid=(NB_FF,), in_specs=[cg, cv, wg, wv, bg, bv], out_specs=cg,
        out_shape=_sds((S, DFF), BF16), compiler_params=_params(("parallel",)))(up, up, cw, cw, cb, cb)


def _ffn_gate_bwd(da, up, cw, cb, *, name):
    S = up.shape[0]
    cg, cv, wg, wv, bg, bv = _ffn_specs(S)

    def body(da_ref, g_ref, v_ref, wg_ref, wv_ref, bg_ref, bv_ref, dxg_ref, dxv_ref, dwg_ref, dwv_ref, dbg_ref, dbv_ref):
        xg, xv = g_ref[...], v_ref[...]
        ug = _conv(xg, wg_ref, bg_ref, ALL, FFN_CONV)
        uv = _conv(xv, wv_ref, bv_ref, ALL, FFN_CONV)
        dug, duv = jax.vjp(_gate, ug, uv)[1](da_ref[...])
        dxg_ref[...] = _conv_bwd(dug, xg, wg_ref, dwg_ref, dbg_ref, ALL, FFN_CONV).astype(BF16)
        dxv_ref[...] = _conv_bwd(duv, xv, wv_ref, dwv_ref, dbv_ref, ALL, FFN_CONV).astype(BF16)

    half = _sds((S, DFF), BF16)
    dxg, dxv, dwg, dwv, dbg, dbv = pl.pallas_call(
        body, name=name, grid=(NB_FF,), in_specs=[cg, cg, cv, wg, wv, bg, bv], out_specs=[cg, cg, wg, wg, bg, bg],
        out_shape=[half, half, _sds((FFN_CONV, DFF), F32), _sds((FFN_CONV, DFF), F32), _sds((1, DFF), F32), _sds((1, DFF), F32)],
        compiler_params=_params(("parallel",)))(da, up, up, cw, cw, cb, cb)
    return dxg, dxv, jnp.concatenate([dwg, dwv], axis=1), jnp.concatenate([dbg, dbv], axis=1)


def _log_sigmoid(x):
    return jnp.minimum(x, 0.0) - jnp.log1p(jnp.exp(-jnp.abs(x)))


@jax.custom_vjp
def _clamp_div(num, den, floor, shift):
    return num / jnp.maximum(jnp.abs(den), floor)


def _clamp_div_fwd(num, den, floor, shift):
    out = num / jnp.maximum(jnp.abs(den), floor)
    return out, (den, floor, out)


def _clamp_div_bwd(res, g):
    den, floor, out = res
    active = jnp.abs(den) < floor
    dinv = jnp.maximum(jnp.abs(den), floor)
    go = jnp.sum(g * out, axis=-1, keepdims=True)
    ddiv = -go / dinv
    return (g / dinv, jnp.where(active, 0.0, ddiv * jnp.sign(den)), jnp.where(active, ddiv, 0.0),
            jnp.sum(jnp.where(active, go, 0.0), axis=-2, keepdims=True))


_clamp_div.defvjp(_clamp_div_fwd, _clamp_div_bwd)


def _ml_heads(q, k, v, o_pre, gates, gbias, C, n, ng, m, shift):
    H, L, _ = q.shape
    lane1 = lax.broadcasted_iota(jnp.int32, (1, 128), 1)
    gz = gates + jnp.where(lane1 < ML_H, lax.stop_gradient(gbias), gbias)
    gz = jnp.broadcast_to(gz[None], (H, L, 128))
    hid = lax.broadcasted_iota(jnp.int32, (H, L, 128), 0)
    lane = lax.broadcasted_iota(jnp.int32, (H, L, 128), 2)
    ig = jnp.sum(jnp.where(lane == hid, gz, 0.0), axis=2, keepdims=True)
    lf = _log_sigmoid(jnp.sum(jnp.where(lane == ML_H + hid, gz, 0.0), axis=2, keepdims=True))
    r = lax.broadcasted_iota(jnp.int32, (H, L, L), 1)
    c = lax.broadcasted_iota(jnp.int32, (H, L, L), 2)
    eye, tril = r == c, r >= c

    def to_row(col):
        return jnp.sum(jnp.where(eye, col, 0.0), axis=1, keepdims=True)

    b_col = jnp.sum(jnp.where(tril, to_row(lf), 0.0), axis=2, keepdims=True)
    Dm = jnp.where(tril, b_col - to_row(b_col) + to_row(ig), -jnp.inf)
    inter = b_col + m
    m_t = jnp.maximum(inter, jnp.max(Dm, axis=2, keepdims=True))
    w_inter = jnp.exp(inter - m_t)
    ks = k * (ML_DH ** -0.5)
    s = dot_nt(q, ks) * jnp.exp(Dm - m_t)
    num = w_inter * dot_nn(q, C) + dot_nn(s, v)
    den = w_inter * jnp.sum(q * n, axis=2, keepdims=True) + jnp.sum(s, axis=2, keepdims=True)
    h = _clamp_div(num, den, jnp.exp(-m_t), shift)
    g = jnp.sum(lf, axis=1, keepdims=True)
    a = g - b_col + ig
    m_new = jnp.maximum(g + m, jnp.max(a, axis=1, keepdims=True))
    decay = jnp.exp(g + m - m_new)
    wk = jnp.exp(a - m_new)
    C_new = decay * C + dot_tn(ks * wk, v)
    n_new = decay * n + jnp.sum(wk * ks, axis=1, keepdims=True)
    mu = jnp.mean(h, axis=2, keepdims=True)
    hc = h - mu
    var = jnp.mean(hc * hc, axis=2, keepdims=True)
    out = jax.nn.sigmoid(o_pre) * (hc * lax.rsqrt(var + EPS) * ng)
    return out, C_new, n_new, m_new


def _hs(h, off=0):
    return slice(off + h * ML_DH, off + (h + 1) * ML_DH)


def _heads(ref, off=0):
    return jnp.stack([ref[:, _hs(h, off)] for h in range(ML_H)])


def _mlstm_fwd(qk, proj, gbias, ng, *, name):
    S = qk.shape[0]
    nc = S // ML_L

    def body(q_ref, k_ref, v_ref, o_ref, g_ref, gb_ref, ng_ref, h_ref, cs_ref, ns_ref, ms_ref, c_s, n_s, m_s):
        @pl.when(pl.program_id(0) == 0)
        def _():
            c_s[...] = jnp.zeros_like(c_s)
            n_s[...] = jnp.zeros_like(n_s)
            m_s[...] = jnp.zeros_like(m_s)

        C, n = c_s[...], n_s[...]
        cs_ref[0] = C
        ns_ref[0] = n
        ms_ref[0] = m_s[...]
        out, C2, n2, m2 = _ml_heads(_heads(q_ref), _heads(k_ref), _heads(v_ref), _heads(o_ref), g_ref[...], gb_ref[...], C, n,
                                    _heads(ng_ref), m_s[:, :, 0:1], jnp.zeros((ML_H, 1, 1), F32))
        for h in range(ML_H):
            h_ref[:, _hs(h)] = out[h].astype(BF16)
        c_s[...] = C2
        n_s[...] = n2
        m_s[...] = jnp.broadcast_to(m2, (ML_H, 1, 128))

    def w(j):
        return pl.BlockSpec((ML_L, ML_W), lambda c, j=j: (c, j))

    return pl.pallas_call(
        body, name=name, grid=(nc,),
        in_specs=[w(0), w(1), w(2), w(3), pl.BlockSpec((ML_L, 128), lambda c: (c, 22)),
                  pl.BlockSpec((1, 128), lambda c: (0, 0)), pl.BlockSpec((1, ML_W), lambda c: (0, 0))],
        out_specs=[w(0), pl.BlockSpec((1, ML_H, ML_DH, ML_DH), lambda c: (c, 0, 0, 0)),
                   pl.BlockSpec((1, ML_H, 1, 128), lambda c: (c, 0, 0, 0)), pl.BlockSpec((1, ML_H, 1, 128), lambda c: (c, 0, 0, 0))],
        out_shape=[_sds((S, ML_W), BF16), _sds((nc, ML_H, ML_DH, ML_DH), F32), _sds((nc, ML_H, 1, 128), F32),
                   _sds((nc, ML_H, 1, 128), F32)],
        scratch_shapes=[pltpu.VMEM((ML_H, ML_DH, ML_DH), F32), pltpu.VMEM((ML_H, 1, 128), F32), pltpu.VMEM((ML_H, 1, 128), F32)],
        compiler_params=_params(("arbitrary",)))(qk, qk, proj, proj, proj, gbias, ng)


def _mlstm_bwd(dh, qk, proj, gbias, ng, cs, ns, ms, *, name):
    S = qk.shape[0]
    nc = S // ML_L

    def body(dh_ref, q_ref, k_ref, v_ref, o_ref, g_ref, gb_ref, ng_ref, cs_ref, ns_ref, ms_ref,
             dqk_ref, dml_ref, dgb_ref, dng_ref, dc_s, dn_s, dm_s):
        @pl.when(pl.program_id(0) == 0)
        def _():
            dc_s[...] = jnp.zeros_like(dc_s)
            dn_s[...] = jnp.zeros_like(dn_s)
            dm_s[...] = jnp.zeros_like(dm_s)
            dgb_ref[...] = jnp.zeros_like(dgb_ref)
            dng_ref[...] = jnp.zeros_like(dng_ref)

        _, vjp = jax.vjp(_ml_heads, _heads(q_ref), _heads(k_ref), _heads(v_ref), _heads(o_ref), g_ref[...], gb_ref[...],
                         cs_ref[0], ns_ref[0], _heads(ng_ref), ms_ref[0][:, :, 0:1], jnp.zeros((ML_H, 1, 1), F32))
        dq, dk, dv, do, dgates, dgb, dC, dn, dng, dm, dshift = vjp((_heads(dh_ref), dc_s[...], dn_s[...], dm_s[:, :, 0:1]))
        lane1 = lax.broadcasted_iota(jnp.int32, (1, 128), 1)
        for h in range(ML_H):
            dqk_ref[:, _hs(h)] = dq[h]
            dqk_ref[:, _hs(h, ML_W)] = dk[h]
            dml_ref[:, _hs(h)] = dv[h].astype(BF16)
            dml_ref[:, _hs(h, ML_W)] = do[h].astype(BF16)
            dng_ref[:, _hs(h)] += dng[h]
            dgb = jnp.where(lane1 == h, dshift[h], dgb)
        dc_s[...] = dC
        dn_s[...] = dn
        dm_s[...] = jnp.broadcast_to(dm, (ML_H, 1, 128))
        dml_ref[:, 2 * ML_W:] = dgates.astype(BF16)
        dgb_ref[...] += dgb

    def w(j):
        return pl.BlockSpec((ML_L, ML_W), lambda c, j=j: (nc - 1 - c, j))

    vec = pl.BlockSpec((1, 128), lambda c: (0, 0))
    vecw = pl.BlockSpec((1, ML_W), lambda c: (0, 0))
    gsp = pl.BlockSpec((ML_L, 128), lambda c: (nc - 1 - c, 22))
    st = pl.BlockSpec((1, ML_H, 1, 128), lambda c: (nc - 1 - c, 0, 0, 0))
    return pl.pallas_call(
        body, name=name, grid=(nc,),
        in_specs=[w(0), w(0), w(1), w(2), w(3), gsp, vec, vecw,
                  pl.BlockSpec((1, ML_H, ML_DH, ML_DH), lambda c: (nc - 1 - c, 0, 0, 0)), st, st],
        out_specs=[pl.BlockSpec((ML_L, 2 * ML_W), lambda c: (nc - 1 - c, 0)), pl.BlockSpec((ML_L, ML_GW), lambda c: (nc - 1 - c, 0)),
                   vec, vecw],
        out_shape=[_sds((S, 2 * ML_W), F32), _sds((S, ML_GW), BF16), _sds((1, 128), F32), _sds((1, ML_W), F32)],
        scratch_shapes=[pltpu.VMEM((ML_H, ML_DH, ML_DH), F32), pltpu.VMEM((ML_H, 1, 128), F32), pltpu.VMEM((ML_H, 1, 128), F32)],
        compiler_params=_params(("arbitrary",)))(dh, qk, qk, proj, proj, proj, gbias, ng, cs, ns, ms)


def _t5_buckets():
    r = np.arange(BLK)[:, None]
    c = np.arange(2 * BLK)[None, :]
    n = np.maximum(BLK + r - c, 0)
    max_exact = REL_B // 2
    nf = np.maximum(n, 1).astype(np.float32)
    large = max_exact + (np.log(nf / np.float32(max_exact)) / np.float32(math.log(REL_MAXD / max_exact))
                         * np.float32(REL_B - max_exact)).astype(np.int32)
    large = np.minimum(large, REL_B - 1)
    return np.where(n < max_exact, n, large).astype(np.int32)


def _bias_table(rel_bias, bucket, *, name):
    def body(rb_ref, bk_ref, o_ref):
        bk = bk_ref[...]
        for h in range(SW_H):
            acc = jnp.zeros((BLK, 2 * BLK), F32)
            for b in range(REL_B):
                acc = jnp.where(bk == b, rb_ref[b, h], acc)
            o_ref[h] = acc

    return pl.pallas_call(
        body, name=name, in_specs=[pl.BlockSpec(memory_space=pltpu.SMEM), pl.BlockSpec(memory_space=pltpu.VMEM)],
        out_specs=pl.BlockSpec(memory_space=pltpu.VMEM), out_shape=_sds((SW_H, BLK, 2 * BLK), F32),
        compiler_params=_params())(rel_bias, bucket)


def _bias_table_bwd(dbias_list, bucket, *, name):
    nl = len(dbias_list)

    def body(*refs):
        d_refs, bk_ref, o_ref = refs[:nl], refs[nl], refs[nl + 1]
        bk = bk_ref[...]
        rows = lax.broadcasted_iota(jnp.int32, (REL_B, 128), 0)
        lanes = lax.broadcasted_iota(jnp.int32, (REL_B, 128), 1)
        acc = jnp.zeros((REL_B, 128), F32)
        for h in range(SW_H):
            d = d_refs[0][h]
            for d_ref in d_refs[1:]:
                d = d + d_ref[h]
            for b in range(REL_B):
                t = jnp.sum(jnp.sum(jnp.where(bk == b, d, 0.0), axis=0, keepdims=True), axis=1, keepdims=True)
                acc = jnp.where((rows == b) & (lanes == h), t, acc)
        o_ref[...] = acc

    vm = pl.BlockSpec(memory_space=pltpu.VMEM)
    return pl.pallas_call(
        body, name=name, in_specs=[vm] * (nl + 1), out_specs=vm, out_shape=_sds((REL_B, 128), F32),
        compiler_params=_params())(*dbias_list, bucket)


def _swa_heads(q, kp, kc, vp, vc, bp, bc, sinks, has_prev):
    def rep(t):
        return jnp.concatenate([t[g:g + 1] for g in range(SW_H // SW_G) for _ in range(SW_G)], axis=0)

    r = lax.broadcasted_iota(jnp.int32, (SW_H, BLK, BLK), 1)
    c = lax.broadcasted_iota(jnp.int32, (SW_H, BLK, BLK), 2)
    hid = lax.broadcasted_iota(jnp.int32, (SW_H, 1, 128), 0)
    lane = lax.broadcasted_iota(jnp.int32, (SW_H, 1, 128), 2)
    sink = jnp.sum(jnp.where(lane == hid, jnp.broadcast_to(sinks[None], (SW_H, 1, 128)), 0.0), axis=2, keepdims=True)
    lp = jnp.where((c > r) & has_prev, dot_nt(q, rep(kp)) * (SW_DH ** -0.5) + bp, -jnp.inf)
    lc = jnp.where(c <= r, dot_nt(q, rep(kc)) * (SW_DH ** -0.5) + bc, -jnp.inf)
    mx = jnp.maximum(jnp.maximum(jnp.max(lp, axis=2, keepdims=True), jnp.max(lc, axis=2, keepdims=True)), sink)
    mx = lax.stop_gradient(mx)
    pp, pc = jnp.exp(lp - mx), jnp.exp(lc - mx)
    den = jnp.sum(pp, axis=2, keepdims=True) + jnp.sum(pc, axis=2, keepdims=True) + jnp.exp(sink - mx)
    return dot_nn(pp / den, rep(vp)) + dot_nn(pc / den, rep(vc))


def _qs(h, off=0):
    return slice(off + h * SW_DH, off + (h + 1) * SW_DH)


def _split(ref, n):
    return jnp.stack([ref[:, _qs(h)] for h in range(n)])


def _swa_fwd(proj, bias, sinks, *, name):
    S = proj.shape[0]
    nb = S // BLK
    nkv = SW_H // SW_G

    def body(q_ref, kp_ref, kc_ref, vp_ref, vc_ref, b_ref, s_ref, o_ref):
        out = _swa_heads(_split(q_ref, SW_H), _split(kp_ref, nkv), _split(kc_ref, nkv), _split(vp_ref, nkv), _split(vc_ref, nkv),
                         b_ref[:, :, :BLK], b_ref[:, :, BLK:], s_ref[...], pl.program_id(0) > 0)
        for h in range(SW_H):
            o_ref[:, _qs(h)] = out[h].astype(BF16)

    def cur(j):
        return pl.BlockSpec((BLK, 128), lambda n, j=j: (n, j))

    def prev(j):
        return pl.BlockSpec((BLK, 128), lambda n, j=j: (jnp.maximum(n - 1, 0), j))

    return pl.pallas_call(
        body, name=name, grid=(nb,),
        in_specs=[pl.BlockSpec((BLK, SW_W), lambda n: (n, 4)), prev(20), cur(20), prev(21), cur(21),
                  pl.BlockSpec((SW_H, BLK, 2 * BLK), lambda n: (0, 0, 0)), pl.BlockSpec((1, 128), lambda n: (0, 0))],
        out_specs=pl.BlockSpec((BLK, SW_W), lambda n: (n, 0)), out_shape=_sds((S, SW_W), BF16),
        compiler_params=_params(("parallel",)))(proj, proj, proj, proj, proj, bias, sinks)


def _swa_bwd(dh, proj, bias, sinks, *, name):
    S = proj.shape[0]
    nb = S // BLK

    nkv = SW_H // SW_G

    def body(dh_ref, q_ref, kp_ref, kc_ref, vp_ref, vc_ref, b_ref, s_ref, dsw_ref, db_ref, ds_ref, ck_s, cv_s):
        i = pl.program_id(0)

        @pl.when(i == 0)
        def _():
            ck_s[...] = jnp.zeros_like(ck_s)
            cv_s[...] = jnp.zeros_like(cv_s)
            db_ref[...] = jnp.zeros_like(db_ref)
            ds_ref[...] = jnp.zeros_like(ds_ref)

        f = functools.partial(_swa_heads, has_prev=i < nb - 1)
        _, vjp = jax.vjp(f, _split(q_ref, SW_H), _split(kp_ref, nkv), _split(kc_ref, nkv), _split(vp_ref, nkv),
                         _split(vc_ref, nkv), b_ref[:, :, :BLK], b_ref[:, :, BLK:], s_ref[...])
        dq, dkp, dkc, dvp, dvc, dbp, dbc, ds = vjp(_split(dh_ref, SW_H))
        for h in range(SW_H):
            dsw_ref[:, _qs(h)] = dq[h].astype(BF16)
        for g in range(nkv):
            dsw_ref[:, _qs(g, SW_W)] = (dkc[g] + ck_s[:, _qs(g)]).astype(BF16)
            dsw_ref[:, _qs(g, SW_W + SW_KVW)] = (dvc[g] + cv_s[:, _qs(g)]).astype(BF16)
            ck_s[:, _qs(g)] = dkp[g]
            cv_s[:, _qs(g)] = dvp[g]
        db_ref[:, :, :BLK] += dbp
        db_ref[:, :, BLK:] += dbc
        ds_ref[...] += ds

    def cur(j):
        return pl.BlockSpec((BLK, 128), lambda i, j=j: (nb - 1 - i, j))

    def prev(j):
        return pl.BlockSpec((BLK, 128), lambda i, j=j: (jnp.maximum(nb - 2 - i, 0), j))

    bsp = pl.BlockSpec((SW_H, BLK, 2 * BLK), lambda i: (0, 0, 0))
    vec = pl.BlockSpec((1, 128), lambda i: (0, 0))
    return pl.pallas_call(
        body, name=name, grid=(nb,),
        in_specs=[pl.BlockSpec((BLK, SW_W), lambda i: (nb - 1 - i, 1)), pl.BlockSpec((BLK, SW_W), lambda i: (nb - 1 - i, 4)),
                  prev(20), cur(20), prev(21), cur(21), bsp, vec],
        out_specs=[pl.BlockSpec((BLK, SW_GW), lambda i: (nb - 1 - i, 0)), bsp, vec],
        out_shape=[_sds((S, SW_GW), BF16), _sds((SW_H, BLK, 2 * BLK), F32), _sds((1, 128), F32)],
        scratch_shapes=[pltpu.VMEM((BLK, 128), F32)] * 2,
        compiler_params=_params(("arbitrary",)))(dh, proj, proj, proj, proj, proj, bias, sinks)


XA_TM = 512


def _xa_head(qh, kh, vh):
    logits = dot_nt(qh, kh) * (XA_DH ** -0.5)
    mx = lax.stop_gradient(jnp.max(logits, axis=1, keepdims=True))
    e = jnp.exp(logits - mx)
    return dot_nn(e / jnp.sum(e, axis=1, keepdims=True), vh)


def _xs(h, off=0):
    return slice(off + h * XA_DH, off + (h + 1) * XA_DH)


def _xattn_fwd(q, kv, *, name):
    S = q.shape[0]
    M = kv.shape[0]

    def body(q_ref, kv_ref, o_ref):
        for h in range(XA_H):
            o_ref[:, _xs(h)] = _xa_head(q_ref[:, _xs(h)], kv_ref[:, _xs(h)], kv_ref[:, _xs(h, D)]).astype(BF16)

    row = pl.BlockSpec((XA_TM, D), lambda i: (i, 0))
    return pl.pallas_call(
        body, name=name, grid=(S // XA_TM,), in_specs=[row, pl.BlockSpec((M, 2 * D), lambda i: (0, 0))], out_specs=row,
        out_shape=_sds((S, D), BF16), compiler_params=_params(("parallel",)))(q, kv)


def _xattn_bwd(do, q, kv, *, name):
    S = q.shape[0]
    M = kv.shape[0]

    def body(do_ref, q_ref, kv_ref, dq_ref, dkv_ref):
        @pl.when(pl.program_id(0) == 0)
        def _():
            dkv_ref[...] = jnp.zeros_like(dkv_ref)

        for h in range(XA_H):
            _, vjp = jax.vjp(_xa_head, q_ref[:, _xs(h)], kv_ref[:, _xs(h)], kv_ref[:, _xs(h, D)])
            dq, dk, dv = vjp(do_ref[:, _xs(h)])
            dq_ref[:, _xs(h)] = dq.astype(BF16)
            dkv_ref[:, _xs(h)] += dk
            dkv_ref[:, _xs(h, D)] += dv

    row = pl.BlockSpec((XA_TM, D), lambda i: (i, 0))
    full = pl.BlockSpec((M, 2 * D), lambda i: (0, 0))
    return pl.pallas_call(
        body, name=name, grid=(S // XA_TM,), in_specs=[row, row, full], out_specs=[row, full],
        out_shape=[_sds((S, D), BF16), _sds((M, 2 * D), F32)], compiler_params=_params(("arbitrary",)))(do, q, kv)


def _loss_head(y, tgt, *, name):
    S = y.shape[0]
    tm = 512

    def body(y_ref, t_ref, l_ref, dy_ref):
        e = y_ref[...] - t_ref[...]
        dy_ref[...] = e * (1.0 / D)
        part = 0.5 * jnp.sum(jnp.sum(e * e, axis=1, keepdims=True) * (1.0 / D), axis=0, keepdims=True)

        @pl.when(pl.program_id(0) == 0)
        def _():
            l_ref[...] = jnp.broadcast_to(part, (8, 128))

        @pl.when(pl.program_id(0) > 0)
        def _():
            l_ref[...] += jnp.broadcast_to(part, (8, 128))

    row = pl.BlockSpec((tm, D), lambda i: (i, 0))
    return pl.pallas_call(
        body, name=name, grid=(S // tm,), in_specs=[row, row], out_specs=[pl.BlockSpec((8, 128), lambda i: (0, 0)), row],
        out_shape=[_sds((8, 128), F32), _sds((S, D), F32)], compiler_params=_params(("arbitrary",)))(y, tgt)


ANY = pl.BlockSpec(memory_space=pl.ANY)


def _place():
    x, y, c = lax.axis_index("x"), lax.axis_index("y"), lax.axis_index("c")
    chips = [(1 - x, y), (x, 1 - y), (1 - x, 1 - y)]
    return x, y, c, chips


def _gather(arrs, *, name):
    n = len(arrs)

    def body(*refs):
        srcs, outs = refs[:n], refs[n:2 * n]
        send_sems, recv_sems, local_sems = refs[2 * n:]
        x, y, c, chips = _place()
        me, sib = (x, y, c), (x, y, 1 - c)

        def idx(p):
            return 4 * p[0] + 2 * p[1] + p[2]

        def copy(i, k, block, to, from_src=False):
            return pltpu.make_async_remote_copy(
                src_ref=srcs[i] if from_src else outs[i].at[idx(block)], dst_ref=outs[i].at[idx(block)],
                send_sem=send_sems.at[7 * i + k], recv_sem=recv_sems.at[7 * i + k], device_id=to, device_id_type=MESH)

        local = [pltpu.make_async_copy(srcs[i], outs[i].at[idx(me)], local_sems.at[i]) for i in range(n)]
        for cp in local:
            cp.start()
        first = []
        for i in range(n):
            first.append(copy(i, 0, me, sib, True))
            first += [copy(i, 1 + j, me, (*chip, c), True) for j, chip in enumerate(chips)]
        for cp in first:
            cp.start()
        passed = []
        for j, chip in enumerate(chips):
            for i in range(n):
                copy(i, 1 + j, (*chip, c), me).wait_recv()
                cp = copy(i, 4 + j, (*chip, c), sib)
                cp.start()
                passed.append(cp)
        for i in range(n):
            copy(i, 0, sib, me).wait_recv()
        for j, chip in enumerate(chips):
            for i in range(n):
                copy(i, 4 + j, (*chip, 1 - c), me).wait_recv()
        for cp in first + passed:
            cp.wait_send()
        for cp in local:
            cp.wait()

    return pl.pallas_call(
        body, name=name, in_specs=[ANY] * n, out_specs=[ANY] * n,
        out_shape=[_sds((N_DEV,) + a.shape, a.dtype) for a in arrs],
        scratch_shapes=[pltpu.SemaphoreType.DMA((7 * n,)), pltpu.SemaphoreType.DMA((7 * n,)), pltpu.SemaphoreType.DMA((n,))],
        compiler_params=pltpu.CompilerParams(has_side_effects=True))(*arrs)


def _swap_sibling(arrs, *, name):
    n = len(arrs)

    def body(*refs):
        srcs, outs = refs[:n], refs[n:2 * n]
        send_sems, recv_sems = refs[2 * n:]
        x, y, c, _ = _place()
        copies = [pltpu.make_async_remote_copy(
            src_ref=srcs[i].at[2 * j + (1 - c)], dst_ref=outs[i].at[j], send_sem=send_sems.at[N_CHIP * i + j],
            recv_sem=recv_sems.at[N_CHIP * i + j], device_id=(x, y, 1 - c), device_id_type=MESH)
            for i in range(n) for j in range(N_CHIP)]
        for cp in copies:
            cp.start()
        for cp in copies:
            cp.wait()

    return pl.pallas_call(
        body, name=name, in_specs=[ANY] * n, out_specs=[ANY] * n,
        out_shape=[_sds((N_CHIP,) + a.shape[1:], a.dtype) for a in arrs],
        scratch_shapes=[pltpu.SemaphoreType.DMA((N_CHIP * n,)), pltpu.SemaphoreType.DMA((N_CHIP * n,))],
        compiler_params=pltpu.CompilerParams(has_side_effects=True))(*arrs)


def _pair_sum(part, got, c_idx, *, name):
    _, R, C = part.shape
    tr = next(t for t in (R, 512, 256, 128, 64, 32, 16) if R % t == 0 and t * C <= 1024 * 1024)

    def body(c_ref, p_ref, g_ref, o_ref):
        o_ref[...] = (p_ref[0].astype(F32) + g_ref[...].astype(F32)).astype(o_ref.dtype)

    return pl.pallas_call(
        body, name=name,
        grid_spec=pltpu.PrefetchScalarGridSpec(
            num_scalar_prefetch=1, grid=(N_CHIP, R // tr),
            in_specs=[pl.BlockSpec((1, 1, tr, C), lambda j, r, c_ref: (j, c_ref[0], r, 0)),
                      pl.BlockSpec((1, tr, C), lambda j, r, c_ref: (j, r, 0))],
            out_specs=pl.BlockSpec((1, tr, C), lambda j, r, c_ref: (j, r, 0))),
        out_shape=_sds((N_CHIP, R, C), part.dtype),
        compiler_params=_params(("parallel", "parallel")))(c_idx, part.reshape(N_CHIP, 2, R, C), got)


def _swap_chips(arrs, *, name):
    n = len(arrs)

    def body(*refs):
        srcs, outs = refs[:n], refs[n:2 * n]
        send_sems, recv_sems, local_sems = refs[2 * n:]
        x, y, c, chips = _place()
        jme = 2 * x + y
        local = [pltpu.make_async_copy(srcs[i].at[jme], outs[i].at[jme], local_sems.at[i]) for i in range(n)]
        for cp in local:
            cp.start()
        copies = [pltpu.make_async_remote_copy(
            src_ref=srcs[i].at[2 * chip[0] + chip[1]], dst_ref=outs[i].at[jme], send_sem=send_sems.at[3 * i + j],
            recv_sem=recv_sems.at[3 * i + j], device_id=(*chip, c), device_id_type=MESH)
            for i in range(n) for j, chip in enumerate(chips)]
        for cp in copies:
            cp.start()
        for i in range(n):
            for j, chip in enumerate(chips):
                pltpu.make_async_remote_copy(
                    src_ref=srcs[i].at[jme], dst_ref=outs[i].at[2 * chip[0] + chip[1]], send_sem=send_sems.at[3 * i + j],
                    recv_sem=recv_sems.at[3 * i + j], device_id=(*chip, c), device_id_type=MESH).wait_recv()
        for cp in copies:
            cp.wait_send()
        for cp in local:
            cp.wait()

    return pl.pallas_call(
        body, name=name, in_specs=[ANY] * n, out_specs=[ANY] * n,
        out_shape=[_sds(a.shape, a.dtype) for a in arrs],
        scratch_shapes=[pltpu.SemaphoreType.DMA((3 * n,)), pltpu.SemaphoreType.DMA((3 * n,)), pltpu.SemaphoreType.DMA((n,))],
        compiler_params=pltpu.CompilerParams(has_side_effects=True))(*arrs)


HBM = pl.BlockSpec(memory_space=pltpu.HBM)
SEM = pl.BlockSpec(memory_space=pltpu.SEMAPHORE)
EFFECT = pltpu.SideEffectType.DATAFLOW_SIDE_EFFECTING


def _near_copies(srcs, lands, send_sems, recv_sems):
    x, y, c, chips = _place()
    me = 4 * x + 2 * y + c
    out = []
    for i in range(len(srcs)):
        for k, (px, py, pc) in enumerate([(x, y, 1 - c)] + [(*chip, c) for chip in chips]):
            out.append(tuple(pltpu.make_async_remote_copy(
                src_ref=srcs[i], dst_ref=lands[i].at[slot], send_sem=send_sems.at[4 * i + k], recv_sem=recv_sems.at[4 * i + k],
                device_id=(px, py, pc), device_id_type=MESH) for slot in (me, 4 * px + 2 * py + pc)))
    return out


def _forward_sibling(lands, *, name):
    n = len(lands)

    def body(*refs):
        bufs = refs[n:2 * n]
        send_sems, recv_sems = refs[2 * n:]
        x, y, c, chips = _place()
        copies = [tuple(pltpu.make_async_remote_copy(
            src_ref=bufs[i].at[4 * chip[0] + 2 * chip[1] + c], dst_ref=bufs[i].at[4 * chip[0] + 2 * chip[1] + cc],
            send_sem=send_sems.at[3 * i + j], recv_sem=recv_sems.at[3 * i + j], device_id=(x, y, 1 - c), device_id_type=MESH)
            for cc in (c, 1 - c)) for i in range(n) for j, chip in enumerate(chips)]
        for send, _ in copies:
            send.start()
        for send, recv in copies:
            send.wait_send()
            recv.wait_recv()

    return pl.pallas_call(
        body, name=name, in_specs=[ANY] * n, out_specs=[ANY] * n, out_shape=[_sds(a.shape, a.dtype) for a in lands],
        input_output_aliases={i: i for i in range(n)},
        scratch_shapes=[pltpu.SemaphoreType.DMA((3 * n,)), pltpu.SemaphoreType.DMA((3 * n,))],
        compiler_params=pltpu.CompilerParams(has_side_effects=True))(*lands)


def _chip_copies(srcs, lands, send_sems, recv_sems):
    x, y, c, chips = _place()
    jme = 2 * x + y
    out = []
    for i in range(len(srcs)):
        for j, chip in enumerate(chips):
            jt = 2 * chip[0] + chip[1]
            out.append(tuple(pltpu.make_async_remote_copy(
                src_ref=srcs[i].at[s], dst_ref=lands[i].at[d], send_sem=send_sems.at[3 * i + j], recv_sem=recv_sems.at[3 * i + j],
                device_id=(*chip, c), device_id_type=MESH) for s, d in ((jt, jme), (jme, jt))))
    return out


def _split_start(srcs, lands, pattern, ncopy, *, after=(), name):
    n = len(srcs)
    na = len(after)

    def body(*refs):
        sems = refs[2 * n + na:]
        for send, _ in pattern(refs[:n], refs[n:2 * n], sems[0], sems[1]):
            send.start()
        refs[-1][...] = jnp.zeros_like(refs[-1])

    arrs = list(srcs) + list(lands)
    return pl.pallas_call(
        body, name=name,
        out_shape=(pltpu.SemaphoreType.DMA((ncopy,)), pltpu.SemaphoreType.DMA((ncopy,)),
                   *[pltpu.HBM(a.shape, a.dtype) for a in arrs], _sds((8, 128), F32)),
        in_specs=[HBM] * (2 * n) + [ANY] * na, out_specs=(SEM, SEM, *[HBM] * (2 * n), pl.BlockSpec(memory_space=pltpu.VMEM)),
        input_output_aliases={i: 2 + i for i in range(2 * n)},
        compiler_params=pltpu.CompilerParams(has_side_effects=EFFECT))(
            *[pltpu.with_memory_space_constraint(a, pltpu.HBM) for a in arrs], *after)


def _split_wait(started, after, pattern, *, name):
    send_sems, recv_sems, *arrs = started[:-1]
    n = len(arrs) // 2

    def body(*refs):
        for send, recv in pattern(refs[:n], refs[n:2 * n], refs[2 * n], refs[2 * n + 1]):
            send.wait_send()
            recv.wait_recv()

    outs = pl.pallas_call(
        body, name=name, out_shape=tuple(pltpu.HBM(a.shape, a.dtype) for a in arrs),
        in_specs=[HBM] * (2 * n) + [SEM, SEM, ANY], out_specs=tuple([HBM] * (2 * n)),
        input_output_aliases={i: i for i in range(2 * n)},
        compiler_params=pltpu.CompilerParams(has_side_effects=EFFECT))(*arrs, send_sems, recv_sems, after)
    return list(outs[n:])


def _adamw(parts, w, m, v, *, layer=None, prev=None, name):
    P, R, C = parts.shape
    tr = next((t for t in (512, 256, 176, 128, 64, 32, 16, 8) if R % t == 0 and t * C <= 256 * 1024), R)
    c1 = 1.0 / (1.0 - ADAM_B1 ** ADAM_STEP)
    c2 = 1.0 / (1.0 - ADAM_B2 ** ADAM_STEP)
    nprev = 0 if prev is None else 4

    def body(p_ref, w_ref, m_ref, v_ref, *rest):
        g_ref, d_ref, nm_ref, nv_ref = rest[nprev:]
        g = p_ref[0].astype(F32)
        for j in range(1, P):
            g = g + p_ref[j].astype(F32)
        g = g.reshape(w_ref.shape)
        nm = ADAM_B1 * m_ref[...] + (1.0 - ADAM_B1) * g
        nv = ADAM_B2 * v_ref[...] + (1.0 - ADAM_B2) * (g * g)
        g_ref[...] = g
        nm_ref[...] = nm
        nv_ref[...] = nv
        d_ref[...] = -ADAM_LR * ((nm * c1) / (jnp.sqrt(nv * c2) + ADAM_EPS) + ADAM_WD * w_ref[...])

    if layer is None:
        row = pl.BlockSpec((tr, C), lambda i: (i, 0))
    else:
        row = pl.BlockSpec((1, tr, C), lambda i: (layer, i, 0))
    out = _sds(w.shape, F32)
    return pl.pallas_call(
        body, name=name, grid=(R // tr,),
        in_specs=[pl.BlockSpec((P, tr, C), lambda i: (0, i, 0)), row, row, row] + [ANY] * nprev,
        out_specs=[row, row, row, row], out_shape=[out, out, out, out],
        input_output_aliases={4 + k: k for k in range(nprev)},
        compiler_params=_params(("parallel",)))(parts, w, m, v, *(prev or ()))


BIG = ("w_in", "w_out", "xa_wq", "xa_wkv", "xa_wo", "ffn_w_up", "ffn_w_down")
COL_SHARDED = ("w_in", "xa_wkv", "ffn_w_up", "ml_conv_w", "ffn_conv_w")
SHARDED_SMALL = ("ml_conv_w", "ffn_conv_w")
REPLICATED = ("rel_bias", "ml_conv_b", "ml_i_bias", "ml_f_bias", "ml_norm_g", "swa_sinks", "ln1_g", "ln1_b", "ln2_g", "ln2_b",
              "ffn_conv_b", "ln3_g", "ln3_b")
NAMES = ("rel_bias", "w_in", "ml_conv_w", "ml_conv_b", "ml_i_bias", "ml_f_bias", "ml_norm_g", "swa_sinks", "w_out", "ln1_g", "ln1_b",
         "xa_wq", "xa_wkv", "xa_wo", "ln2_g", "ln2_b", "ffn_w_up", "ffn_conv_w", "ffn_conv_b", "ffn_w_down", "ln3_g", "ln3_b")


def _flat_rows(a, mult):
    f = a.reshape(-1)
    n = -(-f.shape[0] // (128 * mult)) * (128 * mult)
    if n != f.shape[0]:
        f = jnp.pad(f, (0, n - f.shape[0]))
    return f.reshape(-1, 128)


def _pack(arrs, mult):
    parts = [_flat_rows(a, mult) for a in arrs]
    return jnp.concatenate(parts, axis=0), [p.shape[0] for p in parts]


def _unpack(flat, rows, shapes):
    out, off = [], 0
    lead = flat.shape[:-2]
    for r, shp in zip(rows, shapes):
        n = int(np.prod(shp))
        piece = flat[..., off:off + r, :].reshape(lead + (r * 128,))[..., :n]
        out.append(piece.reshape(lead + tuple(shp)))
        off += r
    return out


def _full_from_shards(stacked, name):
    if name in COL_SHARDED:
        return jnp.moveaxis(stacked, 0, 2).reshape(stacked.shape[1], stacked.shape[2], N_DEV * stacked.shape[3])
    return jnp.moveaxis(stacked, 0, 1).reshape(stacked.shape[1], N_DEV * stacked.shape[2], stacked.shape[3])


def _shards_from_full(full, name):
    L, A, B = full.shape
    if name in COL_SHARDED:
        return jnp.moveaxis(full.reshape(L, A, N_DEV, B // N_DEV), 2, 0)
    return jnp.moveaxis(full.reshape(L, N_DEV, A // N_DEV, B), 1, 0)


def _pad_win(w):
    z = jnp.zeros(w.shape[:-1] + (NP_IN - N_IN,), w.dtype)
    return jnp.concatenate([w[..., :2048], w[..., 2056:], w[..., 2048:2056], z], axis=-1)


def _row128(v):
    return jnp.pad(v, (0, 128 - v.shape[0])).reshape(1, 128)


REST = BIG[1:]


def _layer_full(stacked, n):
    if n in COL_SHARDED:
        full = jnp.moveaxis(stacked, 0, 1).reshape(stacked.shape[1], N_DEV * stacked.shape[2])
    else:
        full = stacked.reshape(N_DEV * stacked.shape[1], stacked.shape[2])
    return _pad_win(full) if n == "w_in" else full


def _layer_shards(g, n):
    A, B = g.shape
    if n in COL_SHARDED:
        return jnp.moveaxis(g.reshape(A, N_DEV, B // N_DEV), 1, 0).astype(BF16)
    return g.reshape(N_DEV, A // N_DEV, B).astype(BF16)


def _with_own_block(a, idx, nblk):
    return lax.dynamic_update_slice(jnp.zeros((nblk,) + a.shape, a.dtype), a[None], (idx,) + (0,) * a.ndim)


def _gather_start(arrs, me, *, after=(), name):
    return _split_start(arrs, [_with_own_block(a, me, N_DEV) for a in arrs], _near_copies, 4 * len(arrs), after=after, name=name)


def _gather_finish(started, after, *, name):
    return _forward_sibling(_split_wait(started, after, _near_copies, name=name + "_wait"), name=name + "_forward")


def _reduce_chips(tag, names, grads, c_idx, jme, *, split):
    send = [_layer_shards(g, n) for n, g in zip(names, grads)]
    got = _swap_sibling(send, name=f"swap_sibling_{tag}")
    sums = [_pair_sum(s, g, c_idx, name=f"pair_sum_{tag}_{n}") for n, s, g in zip(names, send, got)]
    if not split:
        return _swap_chips(sums, name=f"swap_chips_{tag}")
    lands = [_with_own_block(lax.dynamic_index_in_dim(q, jme, 0, keepdims=False), jme, N_CHIP) for q in sums]
    return _split_start(sums, lands, _chip_copies, 3 * len(sums), name=f"swap_chips_{tag}_start")


def _update_small(Gf, loss_part, W, Mo, Vo, me):
    res = {}
    small = REPLICATED + SHARDED_SMALL
    sp_flat, sp_rows = _pack([Gf[n] for n in small] + [loss_part], 8)
    sp_all = _gather([sp_flat], name="gather_small_grads")[0]

    def widen(n, t):
        if n not in SHARDED_SMALL:
            return t[n]
        return lax.dynamic_update_slice(jnp.zeros(Gf[n].shape, F32), t[n], (0, 0, me * t[n].shape[2]))

    zl = jnp.zeros((8, 128), F32)
    wsm, _ = _pack([widen(n, W) for n in small] + [zl], 8)
    msm, _ = _pack([widen(n, Mo) for n in small] + [zl], 8)
    vsm, _ = _pack([widen(n, Vo) for n in small] + [zl], 8)
    outs_small = [_unpack(o_, sp_rows, [Gf[n].shape for n in small] + [(8, 128)])
                  for o_ in _adamw(sp_all, wsm, msm, vsm, name="adamw_small")]
    for kind, os_ in zip(("g", "d", "m", "v"), outs_small):
        for n, a in zip(small, os_[:-1]):
            if n in SHARDED_SMALL:
                a = lax.dynamic_slice(a, (0, 0, me * W[n].shape[2]), W[n].shape)
            res[kind, n] = a
    return res, outs_small[0][-1][0, 0]


def kernel(x, mem, rel_bias, w_in, ml_conv_w, ml_conv_b, ml_i_bias, ml_f_bias, ml_norm_g, swa_sinks, w_out, ln1_g, ln1_b, xa_wq, xa_wkv, xa_wo, ln2_g, ln2_b, ffn_w_up, ffn_conv_w, ffn_conv_b, ffn_w_down, ln3_g, ln3_b, loss_target, m_rel_bias, m_w_in, m_ml_conv_w, m_ml_conv_b, m_ml_i_bias, m_ml_f_bias, m_ml_norm_g, m_swa_sinks, m_w_out, m_ln1_g, m_ln1_b, m_xa_wq, m_xa_wkv, m_xa_wo, m_ln2_g, m_ln2_b, m_ffn_w_up, m_ffn_conv_w, m_ffn_conv_b, m_ffn_w_down, m_ln3_g, m_ln3_b, v_rel_bias, v_w_in, v_ml_conv_w, v_ml_conv_b, v_ml_i_bias, v_ml_f_bias, v_ml_norm_g, v_swa_sinks, v_w_out, v_ln1_g, v_ln1_b, v_xa_wq, v_xa_wkv, v_xa_wo, v_ln2_g, v_ln2_b, v_ffn_w_up, v_ffn_conv_w, v_ffn_conv_b, v_ffn_w_down, v_ln3_g, v_ln3_b):
    W = dict(rel_bias=rel_bias, w_in=w_in, ml_conv_w=ml_conv_w, ml_conv_b=ml_conv_b, ml_i_bias=ml_i_bias, ml_f_bias=ml_f_bias,
             ml_norm_g=ml_norm_g, swa_sinks=swa_sinks, w_out=w_out, ln1_g=ln1_g, ln1_b=ln1_b, xa_wq=xa_wq, xa_wkv=xa_wkv,
             xa_wo=xa_wo, ln2_g=ln2_g, ln2_b=ln2_b, ffn_w_up=ffn_w_up, ffn_conv_w=ffn_conv_w, ffn_conv_b=ffn_conv_b,
             ffn_w_down=ffn_w_down, ln3_g=ln3_g, ln3_b=ln3_b)
    Mo = dict(rel_bias=m_rel_bias, w_in=m_w_in, ml_conv_w=m_ml_conv_w, ml_conv_b=m_ml_conv_b, ml_i_bias=m_ml_i_bias,
              ml_f_bias=m_ml_f_bias, ml_norm_g=m_ml_norm_g, swa_sinks=m_swa_sinks, w_out=m_w_out, ln1_g=m_ln1_g, ln1_b=m_ln1_b,
              xa_wq=m_xa_wq, xa_wkv=m_xa_wkv, xa_wo=m_xa_wo, ln2_g=m_ln2_g, ln2_b=m_ln2_b, ffn_w_up=m_ffn_w_up,
              ffn_conv_w=m_ffn_conv_w, ffn_conv_b=m_ffn_conv_b, ffn_w_down=m_ffn_w_down, ln3_g=m_ln3_g, ln3_b=m_ln3_b)
    Vo = dict(rel_bias=v_rel_bias, w_in=v_w_in, ml_conv_w=v_ml_conv_w, ml_conv_b=v_ml_conv_b, ml_i_bias=v_ml_i_bias,
              ml_f_bias=v_ml_f_bias, ml_norm_g=v_ml_norm_g, swa_sinks=v_swa_sinks, w_out=v_w_out, ln1_g=v_ln1_g, ln1_b=v_ln1_b,
              xa_wq=v_xa_wq, xa_wkv=v_xa_wkv, xa_wo=v_xa_wo, ln2_g=v_ln2_g, ln2_b=v_ln2_b, ffn_w_up=v_ffn_w_up,
              ffn_conv_w=v_ffn_conv_w, ffn_conv_b=v_ffn_conv_b, ffn_w_down=v_ffn_w_down, ln3_g=v_ln3_g, ln3_b=v_ln3_b)
    S = x.shape[1]
    c_me = lax.axis_index("c")
    me = 4 * lax.axis_index("x") + 2 * lax.axis_index("y") + c_me
    c_idx = jnp.reshape(c_me, (1,)).astype(jnp.int32)
    xs = x.reshape(S, D)
    mems = mem.reshape(mem.shape[1], D)
    tgt = loss_target.reshape(S, D)

    jme = 2 * lax.axis_index("x") + lax.axis_index("y")

    sm_flat, sm_rows = _pack([W[n] for n in SHARDED_SMALL], 8)
    first = _gather([w_in[0].astype(BF16), sm_flat], name="gather_first")
    rest0 = _gather_start([W[n][0].astype(BF16) for n in REST], me, name="gather_rest0_start")
    full = [{"w_in": _layer_full(first[0], "w_in")}, None]
    conv_w = {n: _full_from_shards(s, n) for n, s in zip(SHARDED_SMALL, _unpack(first[1], sm_rows, [W[n].shape for n in SHARDED_SMALL]))}

    bucket = jnp.asarray(_t5_buckets())
    bias = _bias_table(rel_bias, bucket, name="bias_table")

    saved = []
    h0, h0b = xs, xs.astype(BF16)
    for l in range(DEPTH):
        gbias = _row128(jnp.concatenate([ml_i_bias[l], ml_f_bias[l]]))
        sinks = _row128(swa_sinks[l])
        ng = ml_norm_g[l].reshape(1, ML_W)
        proj = _mm(h0b, full[l]["w_in"], dep=(rest0[-1],) if l == 0 else (), name=f"proj{l}")
        qk = _silu_conv_fwd(proj, conv_w["ml_conv_w"][l], ml_conv_b[l].reshape(1, -1), name=f"mlconv{l}")
        h_ml, cs, ns, ms = _mlstm_fwd(qk, proj, gbias, ng, name=f"mlstm{l}")
        h_sw = _swa_fwd(proj, bias, sinks, name=f"swa{l}")
        dep = ()
        if l == 0:
            landed = _gather_finish(rest0, h_sw, name="gather_rest0")
            full[0].update({n: _layer_full(s, n) for n, s in zip(REST, landed)})
            layer1 = _gather_start([W[n][1].astype(BF16) for n in BIG], me, after=(landed[0],), name="gather_layer1_start")
            dep = (layer1[-1],)
        fw = full[l]
        h1, h1b, z1 = _mm_res_ln([h_ml, h_sw], fw["w_out"], h0, ln1_g[l].reshape(1, D), ln1_b[l].reshape(1, D), name=f"mix_out{l}")
        q = _mm(h1b, fw["xa_wq"], dep=dep, name=f"xa_q{l}")
        kv = _mm(mems, fw["xa_wkv"], tm=256, name=f"xa_kv{l}")
        o = _xattn_fwd(q, kv, name=f"xattn{l}")
        h2, h2b, z2 = _mm_res_ln([o], fw["xa_wo"], h1, ln2_g[l].reshape(1, D), ln2_b[l].reshape(1, D), name=f"xa_out{l}")
        up = _mm(h2b, fw["ffn_w_up"], name=f"ffn_up{l}")
        act = _ffn_gate_fwd(up, conv_w["ffn_conv_w"][l], ffn_conv_b[l].reshape(1, -1), name=f"ffn_gate{l}")
        h3, h3b, z3 = _mm_res_ln([act], fw["ffn_w_down"], h2, ln3_g[l].reshape(1, D), ln3_b[l].reshape(1, D), name=f"ffn_out{l}")
        saved.append(dict(h0b=h0b, proj=proj, qk=qk, cs=cs, ns=ns, ms=ms, h_ml=h_ml, h_sw=h_sw, z1=z1, h1b=h1b, q=q, kv=kv, o=o,
                          z2=z2, h2b=h2b, up=up, act=act, z3=z3, gbias=gbias, sinks=sinks, ng=ng))
        h0, h0b = h3, h3b
        if l == 0:
            landed = _gather_finish(layer1, h3b, name="gather_layer1")
            full[1] = {n: _layer_full(s, n) for n, s in zip(BIG, landed)}

    loss_part, dh = _loss_head(h0, tgt, name="loss_head")

    G = {n: [None] * DEPTH for n in NAMES if n != "rel_bias"}
    dbias = [None] * DEPTH
    dep, pending = (), []
    for l in reversed(range(DEPTH)):
        sv, fw = saved[l], full[l]
        win = fw["w_in"]
        dz3, dz3b, G["ln3_g"][l], G["ln3_b"][l] = _ln_bwd(dh, sv["z3"], ln3_g[l].reshape(1, D), dep=dep, name=f"ln3_bwd{l}")
        G["ffn_w_down"][l] = _mm_tn(sv["act"], dz3b, name=f"d_w_down{l}")
        dact = _mm(dz3b, fw["ffn_w_down"], trans_b=True, name=f"d_act{l}")
        dupg, dupv, G["ffn_conv_w"][l], G["ffn_conv_b"][l] = _ffn_gate_bwd(
            dact, sv["up"], conv_w["ffn_conv_w"][l], ffn_conv_b[l].reshape(1, -1), name=f"ffn_gate_bwd{l}")
        G["ffn_w_up"][l] = jnp.concatenate([_mm_tn(sv["h2b"], dupg, name=f"d_w_up_g{l}"),
                                            _mm_tn(sv["h2b"], dupv, name=f"d_w_up_v{l}")], axis=1)
        dep = ()
        if l == 0:
            names = ("ffn_w_up", "ffn_w_down")
            pending.append((names, 0, _reduce_chips("ffn0", names, [G[n][0] for n in names], c_idx, jme, split=True)))
            dep = (pending[-1][2][-1],)
        dh2 = _mm([dupg, dupv], fw["ffn_w_up"], trans_b=True, add=dz3, add_scale=ALPHA, dep=dep, name=f"d_h2_{l}")

        dz2, dz2b, G["ln2_g"][l], G["ln2_b"][l] = _ln_bwd(dh2, sv["z2"], ln2_g[l].reshape(1, D), name=f"ln2_bwd{l}")
        G["xa_wo"][l] = _mm_tn(sv["o"], dz2b, name=f"d_xa_wo{l}")
        do = _mm(dz2b, fw["xa_wo"], trans_b=True, name=f"d_xa_o{l}")
        dq, dkv = _xattn_bwd(do, sv["q"], sv["kv"], name=f"xattn_bwd{l}")
        G["xa_wkv"][l] = _mm_tn(mems, dkv, name=f"d_xa_wkv{l}")
        G["xa_wq"][l] = _mm_tn(sv["h1b"], dq, name=f"d_xa_wq{l}")
        dh1 = _mm(dq, fw["xa_wq"], trans_b=True, add=dz2, add_scale=ALPHA, name=f"d_h1_{l}")

        dz1, dz1b, G["ln1_g"][l], G["ln1_b"][l] = _ln_bwd(dh1, sv["z1"], ln1_g[l].reshape(1, D), name=f"ln1_bwd{l}")
        G["w_out"][l] = jnp.concatenate([_mm_tn(sv["h_ml"], dz1b, name=f"d_w_out_ml{l}"),
                                         _mm_tn(sv["h_sw"], dz1b, name=f"d_w_out_sw{l}")], axis=0)
        dhcat = _mm(dz1b, fw["w_out"], trans_b=True, name=f"d_hcat{l}")
        dsw, dbias[l], dsinks = _swa_bwd(dhcat, sv["proj"], bias, sv["sinks"], name=f"swa_bwd{l}")
        dqk, dml, dgb, dng = _mlstm_bwd(dhcat, sv["qk"], sv["proj"], sv["gbias"], sv["ng"], sv["cs"], sv["ns"], sv["ms"],
                                        name=f"mlstm_bwd{l}")
        dqk_pre, G["ml_conv_w"][l], G["ml_conv_b"][l] = _silu_conv_bwd(
            dqk, sv["proj"], conv_w["ml_conv_w"][l], ml_conv_b[l].reshape(1, -1), name=f"mlconv_bwd{l}")
        dw_qk = _mm_tn(sv["h0b"], dqk_pre, name=f"d_w_in_qk{l}")
        dw_ml = _mm_tn(sv["h0b"], dml, name=f"d_w_in_ml{l}")
        dw_sw = _mm_tn(sv["h0b"], dsw, name=f"d_w_in_sw{l}")
        G["w_in"][l] = jnp.concatenate([dw_qk, dw_ml[:, :2 * ML_W + 2 * ML_H], dw_sw], axis=1)
        win_ml = jnp.concatenate([win[:, 1024:2048], win[:, 2816:2944]], axis=1)
        dh = _mm(dqk_pre, win[:, :1024], trans_b=True, add=dz1, add_scale=ALPHA, name=f"d_h0_qk{l}")
        dh = _mm(dml, win_ml, trans_b=True, add=dh, name=f"d_h0_ml{l}")
        dh = _mm(dsw, win[:, 2048:2816], trans_b=True, add=dh, name=f"d_h0_sw{l}")
        G["ml_i_bias"][l] = dgb[0, :ML_H]
        G["ml_f_bias"][l] = dgb[0, ML_H:2 * ML_H]
        G["ml_norm_g"][l] = dng
        G["swa_sinks"][l] = dsinks[0, :SW_H]
        if l == 1:
            pending.append((BIG, 1, _reduce_chips("l1", BIG, [G[n][1] for n in BIG], c_idx, jme, split=True)))
            dep = (pending[-1][2][-1],)
    grad_x = dh.reshape(x.shape)

    names = ("w_in", "w_out", "xa_wq", "xa_wkv", "xa_wo")
    parts = {(n, 0): p for n, p in zip(names, _reduce_chips("rest0", names, [G[n][0] for n in names], c_idx, jme, split=False))}
    for names, l, started in pending:
        landed = _split_wait(started, parts["w_in", 0], _chip_copies, name=f"swap_chips_{'l1' if l else 'ffn0'}_wait")
        parts.update({(n, l): p for n, p in zip(names, landed)})
    res = {}
    for n in BIG:
        outs = None
        for l in reversed(range(DEPTH)):
            outs = _adamw(parts[n, l], W[n], Mo[n], Vo[n], layer=l, prev=outs, name=f"adamw_{n}{l}")
        for kind, a in zip(("g", "d", "m", "v"), outs):
            res[kind, n] = a

    Gf = {n: jnp.stack([g.reshape(W[n].shape[1:]) if n in REPLICATED else g for g in G[n]]) for n in G if n not in BIG}
    Gf["rel_bias"] = _bias_table_bwd(dbias, bucket, name="bias_table_bwd")[:, :SW_H]
    res_small, loss = _update_small(Gf, loss_part, W, Mo, Vo, me)
    res.update(res_small)
    return (loss, grad_x, *[res["g", n] for n in NAMES], *[res["d", n] for n in NAMES], *[res["m", n] for n in NAMES],
            *[res["v", n] for n in NAMES])
```

```python
import functools
import math

import jax
import jax.numpy as jnp
import numpy as np
from jax import lax
from jax.experimental import pallas as pl
from jax.experimental.pallas import tpu as pltpu

F32 = jnp.float32
BF16 = jnp.bfloat16

N_DEV = 8
N_CHIP = 4
D = 1024
DEPTH = 2
ML_H = 4
ML_W = 512
ML_DH = 128
ML_L = 64
ML_CONV = 4
SW_DH = 64
SW_W = 512
SW_H = 8
SW_G = 4
SW_KVW = 128
BLK = 128
REL_B = 32
REL_MAXD = 128
XA_H = 4
XA_DH = 256
DFF = 2816
NB_FF = DFF // 128
FFN_CONV = 3
ALPHA = (2.0 * DEPTH) ** 0.25
EPS = 1e-5
N_IN = 2824
NP_IN = 3072
ML_GW = 2 * ML_W + 128
SW_GW = SW_W + 2 * SW_KVW
ADAM_LR = 0.001
ADAM_B1 = 0.9
ADAM_B2 = 0.999
ADAM_EPS = 1e-08
ADAM_WD = 0.01
ADAM_STEP = 10
VMEM_LIMIT = 56 * 1024 * 1024
MESH = pl.DeviceIdType.MESH

NN = ((1,), (0,))
NT = ((1,), (1,))
TN = ((0,), (0,))


def _dg(a, b, dn):
    if a.ndim == 3:
        dims = (((dn[0][0] + 1,), (dn[1][0] + 1,)), ((0,), (0,)))
    else:
        dims = (dn, ((), ()))
    return lax.dot_general(a.astype(BF16), b.astype(BF16), dims, preferred_element_type=F32)


@jax.custom_vjp
def dot_nn(a, b):
    return _dg(a, b, NN)


dot_nn.defvjp(lambda a, b: (_dg(a, b, NN), (a, b)), lambda r, g: (_dg(g, r[1], NT), _dg(r[0], g, TN)))


@jax.custom_vjp
def dot_nt(a, b):
    return _dg(a, b, NT)


dot_nt.defvjp(lambda a, b: (_dg(a, b, NT), (a, b)), lambda r, g: (_dg(g, r[1], NN), _dg(g, r[0], TN)))


@jax.custom_vjp
def dot_tn(a, b):
    return _dg(a, b, TN)


dot_tn.defvjp(lambda a, b: (_dg(a, b, TN), (a, b)), lambda r, g: (_dg(r[1], g, NT), _dg(r[0], g, NN)))


def _params(sem=None):
    return pltpu.CompilerParams(dimension_semantics=sem, vmem_limit_bytes=VMEM_LIMIT)


def _sds(shape, dtype):
    return jax.ShapeDtypeStruct(tuple(shape), dtype)


TOKEN = pl.BlockSpec((8, 128), lambda *_: (0, 0))


def _mm(a, b, *, trans_b=False, out_dtype=F32, add=None, add_scale=1.0, tm=1024, tn=512, dep=(), name):
    a_list = list(a) if isinstance(a, (list, tuple)) else [a]
    M = a_list[0].shape[0]
    N = b.shape[0] if trans_b else b.shape[1]
    tm = min(tm, M)
    tn = next(t for t in (tn, 384, 256, 128) if N % t == 0)
    assert M % tm == 0
    Ka = a_list[0].shape[1]
    assert all(t.shape[1] == Ka for t in a_list)
    tk = next(t for t in (Ka, 1408, 1024) if Ka % t == 0 and t <= 1408)
    na, npa = len(a_list), Ka // tk
    nk = na * npa
    has_add = add is not None

    def body(*refs):
        a_refs, b_ref = refs[:na], refs[na]
        add_ref = refs[na + 1] if has_add else None
        o_ref, acc_ref = refs[-2], refs[-1]
        k = pl.program_id(2)

        def finish(r):
            if has_add:
                r = r + add_scale * add_ref[...].astype(F32)
            o_ref[...] = r.astype(out_dtype)

        for t, a_ref in enumerate(a_refs):
            def step(a_ref=a_ref):
                p = _dg(a_ref[...], b_ref[...], NT if trans_b else NN)
                if nk == 1:
                    finish(p)
                    return

                @pl.when(k == 0)
                def _():
                    acc_ref[...] = p

                @pl.when((k > 0) & (k < nk - 1))
                def _():
                    acc_ref[...] += p

                @pl.when(k == nk - 1)
                def _():
                    finish(acc_ref[...] + p)

            if na == 1:
                step()
            else:
                pl.when((k >= t * npa) & (k < (t + 1) * npa))(step)

    in_specs = [pl.BlockSpec((tm, tk), lambda i, j, k, t=t: (i, jnp.clip(k - t * npa, 0, npa - 1))) for t in range(na)]
    in_specs.append(pl.BlockSpec((tn, tk), lambda i, j, k: (j, k)) if trans_b else pl.BlockSpec((tk, tn), lambda i, j, k: (k, j)))
    args = a_list + [b]
    if has_add:
        in_specs.append(pl.BlockSpec((tm, tn), lambda i, j, k: (i, j)))
        args.append(add)
    in_specs += [TOKEN] * len(dep)
    args += list(dep)
    return pl.pallas_call(
        body, name=name, grid=(M // tm, N // tn, nk), in_specs=in_specs,
        out_specs=pl.BlockSpec((tm, tn), lambda i, j, k: (i, j)), out_shape=_sds((M, N), out_dtype),
        scratch_shapes=[pltpu.VMEM((tm, tn) if nk > 1 else (8, 128), F32)],
        compiler_params=_params(("parallel", "parallel", "arbitrary")))(*args)


def _mm_tn(a, g, *, name):
    S, K = a.shape
    N = g.shape[1]
    tk = K if K <= 1024 else K // 2
    tn = next(t for t in range(1536, 0, -128) if N % t == 0)
    ts = min(512, S)
    ns = S // ts
    assert K % tk == 0 and S % ts == 0

    def body(a_ref, g_ref, o_ref):
        s = pl.program_id(2)
        p = _dg(a_ref[...], g_ref[...], TN)

        @pl.when(s == 0)
        def _():
            o_ref[...] = p

        @pl.when(s > 0)
        def _():
            o_ref[...] += p

    return pl.pallas_call(
        body, name=name, grid=(K // tk, N // tn, ns),
        in_specs=[pl.BlockSpec((ts, tk), lambda i, j, s: (s, i)), pl.BlockSpec((ts, tn), lambda i, j, s: (s, j))],
        out_specs=pl.BlockSpec((tk, tn), lambda i, j, s: (i, j)), out_shape=_sds((K, N), F32),
        compiler_params=_params(("parallel", "parallel", "arbitrary")))(a, g)


def _mm_res_ln(a_list, w, resid, gam, bet, *, name):
    M, Ka = a_list[0].shape
    na = len(a_list)
    assert w.shape[0] == na * Ka
    tm = min(256, M)

    def body(*refs):
        a_refs, w_refs = refs[:na], refs[na:2 * na]
        r_ref, g_ref, b_ref, y_ref, yb_ref, z_ref = refs[2 * na:]
        z = ALPHA * r_ref[...]
        for a_ref, w_ref in zip(a_refs, w_refs):
            z = z + _dg(a_ref[...], w_ref[...], NN)
        mu = jnp.mean(z, axis=1, keepdims=True)
        zc = z - mu
        var = jnp.mean(zc * zc, axis=1, keepdims=True)
        y = zc * lax.rsqrt(var + EPS) * g_ref[...] + b_ref[...]
        y_ref[...] = y
        yb_ref[...] = y.astype(BF16)
        z_ref[...] = z

    row = pl.BlockSpec((tm, D), lambda i: (i, 0))
    vec = pl.BlockSpec((1, D), lambda i: (0, 0))
    a_specs = [pl.BlockSpec((tm, Ka), lambda i: (i, 0)) for _ in a_list]
    w_specs = [pl.BlockSpec((Ka, D), lambda i, t=t: (t, 0)) for t in range(na)]
    return pl.pallas_call(
        body, name=name, grid=(M // tm,), in_specs=a_specs + w_specs + [row, vec, vec],
        out_specs=[row, row, row], out_shape=[_sds((M, D), F32), _sds((M, D), BF16), _sds((M, D), F32)],
        compiler_params=_params(("parallel",)))(*a_list, *([w] * na), resid, gam, bet)


def _ln_bwd(dy, z, gam, *, dep=(), name):
    M = dy.shape[0]
    tm = min(512, M)

    def body(dy_ref, z_ref, g_ref, *rest):
        dz_ref, dzb_ref, dg_ref, db_ref = rest[len(dep):]
        i = pl.program_id(0)
        z = z_ref[...]
        dy_ = dy_ref[...]
        mu = jnp.mean(z, axis=1, keepdims=True)
        zc = z - mu
        var = jnp.mean(zc * zc, axis=1, keepdims=True)
        rstd = lax.rsqrt(var + EPS)
        xh = zc * rstd
        dxh = dy_ * g_ref[...]
        m1 = jnp.mean(dxh, axis=1, keepdims=True)
        m2 = jnp.mean(dxh * xh, axis=1, keepdims=True)
        dz = rstd * (dxh - m1 - xh * m2)
        dz_ref[...] = dz
        dzb_ref[...] = dz.astype(BF16)
        pg = jnp.sum(dy_ * xh, axis=0, keepdims=True)
        pb = jnp.sum(dy_, axis=0, keepdims=True)

        @pl.when(i == 0)
        def _():
            dg_ref[...] = pg
            db_ref[...] = pb

        @pl.when(i > 0)
        def _():
            dg_ref[...] += pg
            db_ref[...] += pb

    row = pl.BlockSpec((tm, D), lambda i: (i, 0))
    vec = pl.BlockSpec((1, D), lambda i: (0, 0))
    return pl.pallas_call(
        body, name=name, grid=(M // tm,), in_specs=[row, row, vec] + [TOKEN] * len(dep), out_specs=[row, row, vec, vec],
        out_shape=[_sds((M, D), F32), _sds((M, D), BF16), _sds((1, D), F32), _sds((1, D), F32)],
        compiler_params=_params(("arbitrary",)))(dy, z, gam, *dep)


def _shift_down(x, d):
    if d == 0:
        return x
    rows = lax.broadcasted_iota(jnp.int32, x.shape, 0)
    return jnp.where(rows >= d, pltpu.roll(x, d, 0), 0.0)


def _shift_up(x, d):
    if d == 0:
        return x
    S = x.shape[0]
    rows = lax.broadcasted_iota(jnp.int32, x.shape, 0)
    return jnp.where(rows < S - d, pltpu.roll(x, S - d, 0), 0.0)


def _conv(x, w_ref, b_ref, cs, K):
    y = b_ref[:, cs]
    for j in range(K):
        y = y + _shift_down(x, K - 1 - j) * w_ref[j:j + 1, cs]
    return y


def _conv_bwd(dy, x, w_ref, dw_ref, db_ref, cs, K):
    dx = jnp.zeros_like(x)
    for j in range(K):
        dx = dx + _shift_up(dy, K - 1 - j) * w_ref[j:j + 1, cs]
        dw_ref[j:j + 1, cs] = jnp.sum(dy * _shift_down(x, K - 1 - j), axis=0, keepdims=True)
    db_ref[:, cs] = jnp.sum(dy, axis=0, keepdims=True)
    return dx


ALL = slice(None)


def _silu_conv_fwd(proj, cw, cb, *, name):
    S = proj.shape[0]

    def body(x_ref, w_ref, b_ref, o_ref):
        o_ref[...] = jax.nn.silu(_conv(x_ref[...], w_ref, b_ref, ALL, ML_CONV))

    col = pl.BlockSpec((S, 128), lambda j: (0, j))
    return pl.pallas_call(
        body, name=name, grid=(8,),
        in_specs=[col, pl.BlockSpec((ML_CONV, 128), lambda j: (0, j)), pl.BlockSpec((1, 128), lambda j: (0, j))],
        out_specs=col, out_shape=_sds((S, 2 * ML_W), F32), compiler_params=_params(("parallel",)))(proj, cw, cb)


def _silu_conv_bwd(dqk, proj, cw, cb, *, name):
    S = proj.shape[0]

    def body(d_ref, x_ref, w_ref, b_ref, dx_ref, dw_ref, db_ref):
        x = x_ref[...]
        y = _conv(x, w_ref, b_ref, ALL, ML_CONV)
        dy = jax.vjp(jax.nn.silu, y)[1](d_ref[...])[0]
        dx_ref[...] = _conv_bwd(dy, x, w_ref, dw_ref, db_ref, ALL, ML_CONV).astype(BF16)

    col = pl.BlockSpec((S, 128), lambda j: (0, j))
    wsp = pl.BlockSpec((ML_CONV, 128), lambda j: (0, j))
    bsp = pl.BlockSpec((1, 128), lambda j: (0, j))
    return pl.pallas_call(
        body, name=name, grid=(8,), in_specs=[col, col, wsp, bsp], out_specs=[col, wsp, bsp],
        out_shape=[_sds((S, 2 * ML_W), BF16), _sds((ML_CONV, 2 * ML_W), F32), _sds((1, 2 * ML_W), F32)],
        compiler_params=_params(("parallel",)))(dqk, proj, cw, cb)


GELU_C0 = math.sqrt(2.0 / math.pi)
GELU_C1 = 0.044715


def _gate_bwd(ug, uv, da):
    t = jnp.tanh(GELU_C0 * (ug + GELU_C1 * (ug * ug * ug)))
    half = 0.5 * (1.0 + t)
    dgelu = half + 0.5 * ug * (1.0 - t * t) * (GELU_C0 * (1.0 + 3.0 * GELU_C1 * (ug * ug)))
    return da * uv * dgelu, da * (ug * half)


def _ffn_specs(S):
    return (pl.BlockSpec((D, 128), lambda j: (0, j)), pl.BlockSpec((D, 128), lambda j: (0, j + NB_FF)),
            pl.BlockSpec((FFN_CONV, 128), lambda j: (0, j)), pl.BlockSpec((FFN_CONV, 128), lambda j: (0, j + NB_FF)),
            pl.BlockSpec((1, 128), lambda j: (0, j)), pl.BlockSpec((1, 128), lambda j: (0, j + NB_FF)))


def _ffn_gate_fwd(hb, w_up, cw, cb, *, name):
    S = hb.shape[0]
    ug_, uv_, wg, wv, bg, bv = _ffn_specs(S)

    def body(h_ref, ugw_ref, uvw_ref, wg_ref, wv_ref, bg_ref, bv_ref, o_ref, h_s):
        @pl.when(pl.program_id(0) == 0)
        def _():
            pltpu.sync_copy(h_ref, h_s)

        ug = _conv(_dg(h_s[...], ugw_ref[...], NN), wg_ref, bg_ref, ALL, FFN_CONV)
        uv = _conv(_dg(h_s[...], uvw_ref[...], NN), wv_ref, bv_ref, ALL, FFN_CONV)
        o_ref[...] = (jax.nn.gelu(ug) * uv).astype(BF16)

    return pl.pallas_call(
        body, name=name, grid=(NB_FF,), in_specs=[ANY, ug_, uv_, wg, wv, bg, bv],
        out_specs=pl.BlockSpec((S, 128), lambda j: (0, j)), out_shape=_sds((S, DFF), BF16),
        scratch_shapes=[pltpu.VMEM((S, D), BF16)],
        compiler_params=_params(("arbitrary",)))(hb, w_up, w_up, cw, cw, cb, cb)


def _ffn_gate_bwd(dzb, w_down, hb, w_up, cw, cb, *, name):
    S = hb.shape[0]
    ug_, uv_, wg, wv, bg, bv = _ffn_specs(S)

    def body(dz_ref, h_ref, wd_ref, ugw_ref, uvw_ref, wg_ref, wv_ref, bg_ref, bv_ref,
             dxg_ref, dxv_ref, dwg_ref, dwv_ref, dbg_ref, dbv_ref, dz_s, h_s):
        @pl.when(pl.program_id(0) == 0)
        def _():
            pltpu.sync_copy(dz_ref, dz_s)
            pltpu.sync_copy(h_ref, h_s)

        xg = _dg(h_s[...], ugw_ref[...], NN)
        xv = _dg(h_s[...], uvw_ref[...], NN)
        ug = _conv(xg, wg_ref, bg_ref, ALL, FFN_CONV)
        uv = _conv(xv, wv_ref, bv_ref, ALL, FFN_CONV)
        dug, duv = _gate_bwd(ug, uv, _dg(dz_s[...], wd_ref[...], NT))
        dxg_ref[...] = _conv_bwd(dug, xg, wg_ref, dwg_ref, dbg_ref, ALL, FFN_CONV).astype(BF16)
        dxv_ref[...] = _conv_bwd(duv, xv, wv_ref, dwv_ref, dbv_ref, ALL, FFN_CONV).astype(BF16)

    col = pl.BlockSpec((S, 128), lambda j: (0, j))
    half = _sds((S, DFF), BF16)
    dxg, dxv, dwg, dwv, dbg, dbv = pl.pallas_call(
        body, name=name, grid=(NB_FF,),
        in_specs=[ANY, ANY, pl.BlockSpec((128, D), lambda j: (j, 0)), ug_, uv_, wg, wv, bg, bv],
        out_specs=[col, col, wg, wg, bg, bg],
        out_shape=[half, half, _sds((FFN_CONV, DFF), F32), _sds((FFN_CONV, DFF), F32), _sds((1, DFF), F32), _sds((1, DFF), F32)],
        scratch_shapes=[pltpu.VMEM((S, D), BF16), pltpu.VMEM((S, D), BF16)],
        compiler_params=_params(("arbitrary",)))(dzb, hb, w_down, w_up, w_up, cw, cw, cb, cb)
    return dxg, dxv, jnp.concatenate([dwg, dwv], axis=1), jnp.concatenate([dbg, dbv], axis=1)


def _log_sigmoid(x):
    return jnp.minimum(x, 0.0) - jnp.log1p(jnp.exp(-jnp.abs(x)))


@jax.custom_vjp
def _clamp_div(num, den, floor, shift):
    return num / jnp.maximum(jnp.abs(den), floor)


def _clamp_div_fwd(num, den, floor, shift):
    out = num / jnp.maximum(jnp.abs(den), floor)
    return out, (den, floor, out)


def _clamp_div_bwd(res, g):
    den, floor, out = res
    active = jnp.abs(den) < floor
    dinv = jnp.maximum(jnp.abs(den), floor)
    go = jnp.sum(g * out, axis=-1, keepdims=True)
    ddiv = -go / dinv
    return (g / dinv, jnp.where(active, 0.0, ddiv * jnp.sign(den)), jnp.where(active, ddiv, 0.0),
            jnp.sum(jnp.where(active, go, 0.0), axis=-2, keepdims=True))


_clamp_div.defvjp(_clamp_div_fwd, _clamp_div_bwd)


def _ml_heads(q, k, v, o_pre, gates, gbias, C, n, ng, m, shift):
    H, L, _ = q.shape
    lane1 = lax.broadcasted_iota(jnp.int32, (1, 128), 1)
    gz = gates + jnp.where(lane1 < ML_H, lax.stop_gradient(gbias), gbias)
    gz = jnp.broadcast_to(gz[None], (H, L, 128))
    hid = lax.broadcasted_iota(jnp.int32, (H, L, 128), 0)
    lane = lax.broadcasted_iota(jnp.int32, (H, L, 128), 2)
    ig = jnp.sum(jnp.where(lane == hid, gz, 0.0), axis=2, keepdims=True)
    lf = _log_sigmoid(jnp.sum(jnp.where(lane == ML_H + hid, gz, 0.0), axis=2, keepdims=True))
    r = lax.broadcasted_iota(jnp.int32, (H, L, L), 1)
    c = lax.broadcasted_iota(jnp.int32, (H, L, L), 2)
    eye, tril = r == c, r >= c

    def to_row(col):
        return jnp.sum(jnp.where(eye, col, 0.0), axis=1, keepdims=True)

    b_col = jnp.sum(jnp.where(tril, to_row(lf), 0.0), axis=2, keepdims=True)
    Dm = jnp.where(tril, b_col - to_row(b_col) + to_row(ig), -jnp.inf)
    inter = b_col + m
    m_t = jnp.maximum(inter, jnp.max(Dm, axis=2, keepdims=True))
    w_inter = jnp.exp(inter - m_t)
    ks = k * (ML_DH ** -0.5)
    s = dot_nt(q, ks) * jnp.exp(Dm - m_t)
    num = w_inter * dot_nn(q, C) + dot_nn(s, v)
    den = w_inter * jnp.sum(q * n, axis=2, keepdims=True) + jnp.sum(s, axis=2, keepdims=True)
    h = _clamp_div(num, den, jnp.exp(-m_t), shift)
    g = jnp.sum(lf, axis=1, keepdims=True)
    a = g - b_col + ig
    m_new = jnp.maximum(g + m, jnp.max(a, axis=1, keepdims=True))
    decay = jnp.exp(g + m - m_new)
    wk = jnp.exp(a - m_new)
    C_new = decay * C + dot_tn(ks * wk, v)
    n_new = decay * n + jnp.sum(wk * ks, axis=1, keepdims=True)
    mu = jnp.mean(h, axis=2, keepdims=True)
    hc = h - mu
    var = jnp.mean(hc * hc, axis=2, keepdims=True)
    out = jax.nn.sigmoid(o_pre) * (hc * lax.rsqrt(var + EPS) * ng)
    return out, C_new, n_new, m_new


def _hs(h, off=0):
    return slice(off + h * ML_DH, off + (h + 1) * ML_DH)


def _heads(ref, off=0):
    return jnp.stack([ref[:, _hs(h, off)] for h in range(ML_H)])


def _mlstm_fwd(qk, proj, gbias, ng, *, name):
    S = qk.shape[0]
    nc = S // ML_L

    def body(q_ref, k_ref, v_ref, o_ref, g_ref, gb_ref, ng_ref, h_ref, cs_ref, ns_ref, ms_ref, c_s, n_s, m_s):
        @pl.when(pl.program_id(0) == 0)
        def _():
            c_s[...] = jnp.zeros_like(c_s)
            n_s[...] = jnp.zeros_like(n_s)
            m_s[...] = jnp.zeros_like(m_s)

        C, n = c_s[...], n_s[...]
        cs_ref[0] = C
        ns_ref[0] = n
        ms_ref[0] = m_s[...]
        out, C2, n2, m2 = _ml_heads(_heads(q_ref), _heads(k_ref), _heads(v_ref), _heads(o_ref), g_ref[...], gb_ref[...], C, n,
                                    _heads(ng_ref), m_s[:, :, 0:1], jnp.zeros((ML_H, 1, 1), F32))
        for h in range(ML_H):
            h_ref[:, _hs(h)] = out[h].astype(BF16)
        c_s[...] = C2
        n_s[...] = n2
        m_s[...] = jnp.broadcast_to(m2, (ML_H, 1, 128))

    def w(j):
        return pl.BlockSpec((ML_L, ML_W), lambda c, j=j: (c, j))

    return pl.pallas_call(
        body, name=name, grid=(nc,),
        in_specs=[w(0), w(1), w(2), w(3), pl.BlockSpec((ML_L, 128), lambda c: (c, 22)),
                  pl.BlockSpec((1, 128), lambda c: (0, 0)), pl.BlockSpec((1, ML_W), lambda c: (0, 0))],
        out_specs=[w(0), pl.BlockSpec((1, ML_H, ML_DH, ML_DH), lambda c: (c, 0, 0, 0)),
                   pl.BlockSpec((1, ML_H, 1, 128), lambda c: (c, 0, 0, 0)), pl.BlockSpec((1, ML_H, 1, 128), lambda c: (c, 0, 0, 0))],
        out_shape=[_sds((S, ML_W), BF16), _sds((nc, ML_H, ML_DH, ML_DH), F32), _sds((nc, ML_H, 1, 128), F32),
                   _sds((nc, ML_H, 1, 128), F32)],
        scratch_shapes=[pltpu.VMEM((ML_H, ML_DH, ML_DH), F32), pltpu.VMEM((ML_H, 1, 128), F32), pltpu.VMEM((ML_H, 1, 128), F32)],
        compiler_params=_params(("arbitrary",)))(qk, qk, proj, proj, proj, gbias, ng)


def _mlstm_bwd(dh, qk, proj, gbias, ng, cs, ns, ms, *, name):
    S = qk.shape[0]
    nc = S // ML_L

    def body(dh_ref, q_ref, k_ref, v_ref, o_ref, g_ref, gb_ref, ng_ref, cs_ref, ns_ref, ms_ref,
             dqk_ref, dml_ref, dgb_ref, dng_ref, dc_s, dn_s, dm_s):
        @pl.when(pl.program_id(0) == 0)
        def _():
            dc_s[...] = jnp.zeros_like(dc_s)
            dn_s[...] = jnp.zeros_like(dn_s)
            dm_s[...] = jnp.zeros_like(dm_s)
            dgb_ref[...] = jnp.zeros_like(dgb_ref)
            dng_ref[...] = jnp.zeros_like(dng_ref)

        _, vjp = jax.vjp(_ml_heads, _heads(q_ref), _heads(k_ref), _heads(v_ref), _heads(o_ref), g_ref[...], gb_ref[...],
                         cs_ref[0], ns_ref[0], _heads(ng_ref), ms_ref[0][:, :, 0:1], jnp.zeros((ML_H, 1, 1), F32))
        dq, dk, dv, do, dgates, dgb, dC, dn, dng, dm, dshift = vjp((_heads(dh_ref), dc_s[...], dn_s[...], dm_s[:, :, 0:1]))
        lane1 = lax.broadcasted_iota(jnp.int32, (1, 128), 1)
        for h in range(ML_H):
            dqk_ref[:, _hs(h)] = dq[h]
            dqk_ref[:, _hs(h, ML_W)] = dk[h]
            dml_ref[:, _hs(h)] = dv[h].astype(BF16)
            dml_ref[:, _hs(h, ML_W)] = do[h].astype(BF16)
            dng_ref[:, _hs(h)] += dng[h]
            dgb = jnp.where(lane1 == h, dshift[h], dgb)
        dc_s[...] = dC
        dn_s[...] = dn
        dm_s[...] = jnp.broadcast_to(dm, (ML_H, 1, 128))
        dml_ref[:, 2 * ML_W:] = dgates.astype(BF16)
        dgb_ref[...] += dgb

    def w(j):
        return pl.BlockSpec((ML_L, ML_W), lambda c, j=j: (nc - 1 - c, j))

    vec = pl.BlockSpec((1, 128), lambda c: (0, 0))
    vecw = pl.BlockSpec((1, ML_W), lambda c: (0, 0))
    gsp = pl.BlockSpec((ML_L, 128), lambda c: (nc - 1 - c, 22))
    st = pl.BlockSpec((1, ML_H, 1, 128), lambda c: (nc - 1 - c, 0, 0, 0))
    return pl.pallas_call(
        body, name=name, grid=(nc,),
        in_specs=[w(0), w(0), w(1), w(2), w(3), gsp, vec, vecw,
                  pl.BlockSpec((1, ML_H, ML_DH, ML_DH), lambda c: (nc - 1 - c, 0, 0, 0)), st, st],
        out_specs=[pl.BlockSpec((ML_L, 2 * ML_W), lambda c: (nc - 1 - c, 0)), pl.BlockSpec((ML_L, ML_GW), lambda c: (nc - 1 - c, 0)),
                   vec, vecw],
        out_shape=[_sds((S, 2 * ML_W), F32), _sds((S, ML_GW), BF16), _sds((1, 128), F32), _sds((1, ML_W), F32)],
        scratch_shapes=[pltpu.VMEM((ML_H, ML_DH, ML_DH), F32), pltpu.VMEM((ML_H, 1, 128), F32), pltpu.VMEM((ML_H, 1, 128), F32)],
        compiler_params=_params(("arbitrary",)))(dh, qk, qk, proj, proj, proj, gbias, ng, cs, ns, ms)


def _t5_buckets():
    r = np.arange(BLK)[:, None]
    c = np.arange(2 * BLK)[None, :]
    n = np.maximum(BLK + r - c, 0)
    max_exact = REL_B // 2
    nf = np.maximum(n, 1).astype(np.float32)
    large = max_exact + (np.log(nf / np.float32(max_exact)) / np.float32(math.log(REL_MAXD / max_exact))
                         * np.float32(REL_B - max_exact)).astype(np.int32)
    large = np.minimum(large, REL_B - 1)
    return np.where(n < max_exact, n, large).astype(np.int32)


def _bias_table(rel_bias, bucket, *, name):
    def body(rb_ref, bk_ref, o_ref):
        bk = bk_ref[...]
        for h in range(SW_H):
            acc = jnp.zeros((BLK, 2 * BLK), F32)
            for b in range(REL_B):
                acc = jnp.where(bk == b, rb_ref[b, h], acc)
            o_ref[h] = acc

    return pl.pallas_call(
        body, name=name, in_specs=[pl.BlockSpec(memory_space=pltpu.SMEM), pl.BlockSpec(memory_space=pltpu.VMEM)],
        out_specs=pl.BlockSpec(memory_space=pltpu.VMEM), out_shape=_sds((SW_H, BLK, 2 * BLK), F32),
        compiler_params=_params())(rel_bias, bucket)


def _bias_table_bwd(dbias_list, bucket, *, name):
    nl = len(dbias_list)

    def body(*refs):
        d_refs, bk_ref, o_ref = refs[:nl], refs[nl], refs[nl + 1]
        bk = bk_ref[...]
        rows = lax.broadcasted_iota(jnp.int32, (REL_B, 128), 0)
        lanes = lax.broadcasted_iota(jnp.int32, (REL_B, 128), 1)
        acc = jnp.zeros((REL_B, 128), F32)
        for h in range(SW_H):
            d = d_refs[0][h]
            for d_ref in d_refs[1:]:
                d = d + d_ref[h]
            for b in range(REL_B):
                t = jnp.sum(jnp.sum(jnp.where(bk == b, d, 0.0), axis=0, keepdims=True), axis=1, keepdims=True)
                acc = jnp.where((rows == b) & (lanes == h), t, acc)
        o_ref[...] = acc

    vm = pl.BlockSpec(memory_space=pltpu.VMEM)
    return pl.pallas_call(
        body, name=name, in_specs=[vm] * (nl + 1), out_specs=vm, out_shape=_sds((REL_B, 128), F32),
        compiler_params=_params())(*dbias_list, bucket)


def _swa_heads(q, kp, kc, vp, vc, bp, bc, sinks, has_prev):
    def rep(t):
        return jnp.concatenate([t[g:g + 1] for g in range(SW_H // SW_G) for _ in range(SW_G)], axis=0)

    r = lax.broadcasted_iota(jnp.int32, (SW_H, BLK, BLK), 1)
    c = lax.broadcasted_iota(jnp.int32, (SW_H, BLK, BLK), 2)
    hid = lax.broadcasted_iota(jnp.int32, (SW_H, 1, 128), 0)
    lane = lax.broadcasted_iota(jnp.int32, (SW_H, 1, 128), 2)
    sink = jnp.sum(jnp.where(lane == hid, jnp.broadcast_to(sinks[None], (SW_H, 1, 128)), 0.0), axis=2, keepdims=True)
    lp = jnp.where((c > r) & has_prev, dot_nt(q, rep(kp)) * (SW_DH ** -0.5) + bp, -jnp.inf)
    lc = jnp.where(c <= r, dot_nt(q, rep(kc)) * (SW_DH ** -0.5) + bc, -jnp.inf)
    mx = jnp.maximum(jnp.maximum(jnp.max(lp, axis=2, keepdims=True), jnp.max(lc, axis=2, keepdims=True)), sink)
    mx = lax.stop_gradient(mx)
    pp, pc = jnp.exp(lp - mx), jnp.exp(lc - mx)
    den = jnp.sum(pp, axis=2, keepdims=True) + jnp.sum(pc, axis=2, keepdims=True) + jnp.exp(sink - mx)
    return dot_nn(pp / den, rep(vp)) + dot_nn(pc / den, rep(vc))


def _qs(h, off=0):
    return slice(off + h * SW_DH, off + (h + 1) * SW_DH)


def _split(ref, n):
    return jnp.stack([ref[:, _qs(h)] for h in range(n)])


def _swa_fwd(proj, bias, sinks, *, name):
    S = proj.shape[0]
    nb = S // BLK
    nkv = SW_H // SW_G

    def body(q_ref, kp_ref, kc_ref, vp_ref, vc_ref, b_ref, s_ref, o_ref):
        out = _swa_heads(_split(q_ref, SW_H), _split(kp_ref, nkv), _split(kc_ref, nkv), _split(vp_ref, nkv), _split(vc_ref, nkv),
                         b_ref[:, :, :BLK], b_ref[:, :, BLK:], s_ref[...], pl.program_id(0) > 0)
        for h in range(SW_H):
            o_ref[:, _qs(h)] = out[h].astype(BF16)

    def cur(j):
        return pl.BlockSpec((BLK, 128), lambda n, j=j: (n, j))

    def prev(j):
        return pl.BlockSpec((BLK, 128), lambda n, j=j: (jnp.maximum(n - 1, 0), j))

    return pl.pallas_call(
        body, name=name, grid=(nb,),
        in_specs=[pl.BlockSpec((BLK, SW_W), lambda n: (n, 4)), prev(20), cur(20), prev(21), cur(21),
                  pl.BlockSpec((SW_H, BLK, 2 * BLK), lambda n: (0, 0, 0)), pl.BlockSpec((1, 128), lambda n: (0, 0))],
        out_specs=pl.BlockSpec((BLK, SW_W), lambda n: (n, 0)), out_shape=_sds((S, SW_W), BF16),
        compiler_params=_params(("parallel",)))(proj, proj, proj, proj, proj, bias, sinks)


def _swa_bwd(dh, proj, bias, sinks, *, name):
    S = proj.shape[0]
    nb = S // BLK

    nkv = SW_H // SW_G

    def body(dh_ref, q_ref, kp_ref, kc_ref, vp_ref, vc_ref, b_ref, s_ref, dsw_ref, db_ref, ds_ref, ck_s, cv_s):
        i = pl.program_id(0)

        @pl.when(i == 0)
        def _():
            ck_s[...] = jnp.zeros_like(ck_s)
            cv_s[...] = jnp.zeros_like(cv_s)
            db_ref[...] = jnp.zeros_like(db_ref)
            ds_ref[...] = jnp.zeros_like(ds_ref)

        f = functools.partial(_swa_heads, has_prev=i < nb - 1)
        _, vjp = jax.vjp(f, _split(q_ref, SW_H), _split(kp_ref, nkv), _split(kc_ref, nkv), _split(vp_ref, nkv),
                         _split(vc_ref, nkv), b_ref[:, :, :BLK], b_ref[:, :, BLK:], s_ref[...])
        dq, dkp, dkc, dvp, dvc, dbp, dbc, ds = vjp(_split(dh_ref, SW_H))
        for h in range(SW_H):
            dsw_ref[:, _qs(h)] = dq[h].astype(BF16)
        for g in range(nkv):
            dsw_ref[:, _qs(g, SW_W)] = (dkc[g] + ck_s[:, _qs(g)]).astype(BF16)
            dsw_ref[:, _qs(g, SW_W + SW_KVW)] = (dvc[g] + cv_s[:, _qs(g)]).astype(BF16)
            ck_s[:, _qs(g)] = dkp[g]
            cv_s[:, _qs(g)] = dvp[g]
        db_ref[:, :, :BLK] += dbp
        db_ref[:, :, BLK:] += dbc
        ds_ref[...] += ds

    def cur(j):
        return pl.BlockSpec((BLK, 128), lambda i, j=j: (nb - 1 - i, j))

    def prev(j):
        return pl.BlockSpec((BLK, 128), lambda i, j=j: (jnp.maximum(nb - 2 - i, 0), j))

    bsp = pl.BlockSpec((SW_H, BLK, 2 * BLK), lambda i: (0, 0, 0))
    vec = pl.BlockSpec((1, 128), lambda i: (0, 0))
    return pl.pallas_call(
        body, name=name, grid=(nb,),
        in_specs=[pl.BlockSpec((BLK, SW_W), lambda i: (nb - 1 - i, 1)), pl.BlockSpec((BLK, SW_W), lambda i: (nb - 1 - i, 4)),
                  prev(20), cur(20), prev(21), cur(21), bsp, vec],
        out_specs=[pl.BlockSpec((BLK, SW_GW), lambda i: (nb - 1 - i, 0)), bsp, vec],
        out_shape=[_sds((S, SW_GW), BF16), _sds((SW_H, BLK, 2 * BLK), F32), _sds((1, 128), F32)],
        scratch_shapes=[pltpu.VMEM((BLK, 128), F32)] * 2,
        compiler_params=_params(("arbitrary",)))(dh, proj, proj, proj, proj, proj, bias, sinks)


XA_TM = 512


def _xa_head(qh, kh, vh):
    logits = dot_nt(qh, kh) * (XA_DH ** -0.5)
    mx = lax.stop_gradient(jnp.max(logits, axis=1, keepdims=True))
    e = jnp.exp(logits - mx)
    return dot_nn(e / jnp.sum(e, axis=1, keepdims=True), vh)


def _xs(h, off=0):
    return slice(off + h * XA_DH, off + (h + 1) * XA_DH)


def _xattn_fwd(q, kv, *, name):
    S = q.shape[0]
    M = kv.shape[0]

    def body(q_ref, kv_ref, o_ref):
        for h in range(XA_H):
            o_ref[:, _xs(h)] = _xa_head(q_ref[:, _xs(h)], kv_ref[:, _xs(h)], kv_ref[:, _xs(h, D)]).astype(BF16)

    tm = min(XA_TM, S)
    row = pl.BlockSpec((tm, D), lambda i: (i, 0))
    return pl.pallas_call(
        body, name=name, grid=(S // tm,), in_specs=[row, pl.BlockSpec((M, 2 * D), lambda i: (0, 0))], out_specs=row,
        out_shape=_sds((S, D), BF16), compiler_params=_params(("parallel",)))(q, kv)


def _xattn_bwd(do, q, kv, *, name):
    S = q.shape[0]
    M = kv.shape[0]

    def body(do_ref, q_ref, kv_ref, dq_ref, dkv_ref):
        @pl.when(pl.program_id(0) == 0)
        def _():
            dkv_ref[...] = jnp.zeros_like(dkv_ref)

        for h in range(XA_H):
            _, vjp = jax.vjp(_xa_head, q_ref[:, _xs(h)], kv_ref[:, _xs(h)], kv_ref[:, _xs(h, D)])
            dq, dk, dv = vjp(do_ref[:, _xs(h)])
            dq_ref[:, _xs(h)] = dq.astype(BF16)
            dkv_ref[:, _xs(h)] += dk
            dkv_ref[:, _xs(h, D)] += dv

    tm = min(XA_TM, S)
    row = pl.BlockSpec((tm, D), lambda i: (i, 0))
    full = pl.BlockSpec((M, 2 * D), lambda i: (0, 0))
    return pl.pallas_call(
        body, name=name, grid=(S // tm,), in_specs=[row, row, full], out_specs=[row, full],
        out_shape=[_sds((S, D), BF16), _sds((M, 2 * D), F32)], compiler_params=_params(("arbitrary",)))(do, q, kv)


def _loss_head(y, tgt, *, name):
    S = y.shape[0]
    tm = min(512, S)

    def body(y_ref, t_ref, l_ref, dy_ref):
        e = y_ref[...] - t_ref[...]
        dy_ref[...] = e * (1.0 / D)
        part = 0.5 * jnp.sum(jnp.sum(e * e, axis=1, keepdims=True) * (1.0 / D), axis=0, keepdims=True)

        @pl.when(pl.program_id(0) == 0)
        def _():
            l_ref[...] = jnp.broadcast_to(part, (8, 128))

        @pl.when(pl.program_id(0) > 0)
        def _():
            l_ref[...] += jnp.broadcast_to(part, (8, 128))

    row = pl.BlockSpec((tm, D), lambda i: (i, 0))
    return pl.pallas_call(
        body, name=name, grid=(S // tm,), in_specs=[row, row], out_specs=[pl.BlockSpec((8, 128), lambda i: (0, 0)), row],
        out_shape=[_sds((8, 128), F32), _sds((S, D), F32)], compiler_params=_params(("arbitrary",)))(y, tgt)


ANY = pl.BlockSpec(memory_space=pl.ANY)


def _place():
    x, y, c = lax.axis_index("x"), lax.axis_index("y"), lax.axis_index("c")
    chips = [(1 - x, y), (x, 1 - y), (1 - x, 1 - y)]
    return x, y, c, chips


def _gather(arrs, *, name):
    n = len(arrs)

    def body(*refs):
        srcs, outs = refs[:n], refs[n:2 * n]
        send_sems, recv_sems, local_sems = refs[2 * n:]
        x, y, c, chips = _place()
        me, sib = (x, y, c), (x, y, 1 - c)

        def idx(p):
            return 4 * p[0] + 2 * p[1] + p[2]

        def copy(i, k, block, to, from_src=False):
            return pltpu.make_async_remote_copy(
                src_ref=srcs[i] if from_src else outs[i].at[idx(block)], dst_ref=outs[i].at[idx(block)],
                send_sem=send_sems.at[7 * i + k], recv_sem=recv_sems.at[7 * i + k], device_id=to, device_id_type=MESH)

        local = [pltpu.make_async_copy(srcs[i], outs[i].at[idx(me)], local_sems.at[i]) for i in range(n)]
        for cp in local:
            cp.start()
        first = []
        for i in range(n):
            first.append(copy(i, 0, me, sib, True))
            first += [copy(i, 1 + j, me, (*chip, c), True) for j, chip in enumerate(chips)]
        for cp in first:
            cp.start()
        passed = []
        for j, chip in enumerate(chips):
            for i in range(n):
                copy(i, 1 + j, (*chip, c), me).wait_recv()
                cp = copy(i, 4 + j, (*chip, c), sib)
                cp.start()
                passed.append(cp)
        for i in range(n):
            copy(i, 0, sib, me).wait_recv()
        for j, chip in enumerate(chips):
            for i in range(n):
                copy(i, 4 + j, (*chip, 1 - c), me).wait_recv()
        for cp in first + passed:
            cp.wait_send()
        for cp in local:
            cp.wait()

    return pl.pallas_call(
        body, name=name, in_specs=[ANY] * n, out_specs=[ANY] * n,
        out_shape=[_sds((N_DEV,) + a.shape, a.dtype) for a in arrs],
        scratch_shapes=[pltpu.SemaphoreType.DMA((7 * n,)), pltpu.SemaphoreType.DMA((7 * n,)), pltpu.SemaphoreType.DMA((n,))],
        compiler_params=pltpu.CompilerParams(has_side_effects=True))(*arrs)


def _swap_sibling(arrs, *, name):
    n = len(arrs)

    def body(*refs):
        srcs, outs = refs[:n], refs[n:2 * n]
        send_sems, recv_sems = refs[2 * n:]
        x, y, c, _ = _place()
        copies = [pltpu.make_async_remote_copy(
            src_ref=srcs[i].at[2 * j + (1 - c)], dst_ref=outs[i].at[j], send_sem=send_sems.at[N_CHIP * i + j],
            recv_sem=recv_sems.at[N_CHIP * i + j], device_id=(x, y, 1 - c), device_id_type=MESH)
            for i in range(n) for j in range(N_CHIP)]
        for cp in copies:
            cp.start()
        for cp in copies:
            cp.wait()

    return pl.pallas_call(
        body, name=name, in_specs=[ANY] * n, out_specs=[ANY] * n,
        out_shape=[_sds((N_CHIP,) + a.shape[1:], a.dtype) for a in arrs],
        scratch_shapes=[pltpu.SemaphoreType.DMA((N_CHIP * n,)), pltpu.SemaphoreType.DMA((N_CHIP * n,))],
        compiler_params=pltpu.CompilerParams(has_side_effects=True))(*arrs)


def _pair_sum(part, got, c_idx, *, name):
    _, R, C = part.shape
    tr = next(t for t in (R, 512, 256, 128, 64, 32, 16) if R % t == 0 and t * C <= 1024 * 1024)

    def body(c_ref, p_ref, g_ref, o_ref):
        o_ref[...] = (p_ref[0].astype(F32) + g_ref[...].astype(F32)).astype(o_ref.dtype)

    return pl.pallas_call(
        body, name=name,
        grid_spec=pltpu.PrefetchScalarGridSpec(
            num_scalar_prefetch=1, grid=(N_CHIP, R // tr),
            in_specs=[pl.BlockSpec((1, 1, tr, C), lambda j, r, c_ref: (j, c_ref[0], r, 0)),
                      pl.BlockSpec((1, tr, C), lambda j, r, c_ref: (j, r, 0))],
            out_specs=pl.BlockSpec((1, tr, C), lambda j, r, c_ref: (j, r, 0))),
        out_shape=_sds((N_CHIP, R, C), part.dtype),
        compiler_params=_params(("parallel", "parallel")))(c_idx, part.reshape(N_CHIP, 2, R, C), got)


def _swap_chips(arrs, *, name):
    n = len(arrs)

    def body(*refs):
        srcs, outs = refs[:n], refs[n:2 * n]
        send_sems, recv_sems, local_sems = refs[2 * n:]
        x, y, c, chips = _place()
        jme = 2 * x + y
        local = [pltpu.make_async_copy(srcs[i].at[jme], outs[i].at[jme], local_sems.at[i]) for i in range(n)]
        for cp in local:
            cp.start()
        copies = [pltpu.make_async_remote_copy(
            src_ref=srcs[i].at[2 * chip[0] + chip[1]], dst_ref=outs[i].at[jme], send_sem=send_sems.at[3 * i + j],
            recv_sem=recv_sems.at[3 * i + j], device_id=(*chip, c), device_id_type=MESH)
            for i in range(n) for j, chip in enumerate(chips)]
        for cp in copies:
            cp.start()
        for i in range(n):
            for j, chip in enumerate(chips):
                pltpu.make_async_remote_copy(
                    src_ref=srcs[i].at[jme], dst_ref=outs[i].at[2 * chip[0] + chip[1]], send_sem=send_sems.at[3 * i + j],
                    recv_sem=recv_sems.at[3 * i + j], device_id=(*chip, c), device_id_type=MESH).wait_recv()
        for cp in copies:
            cp.wait_send()
        for cp in local:
            cp.wait()

    return pl.pallas_call(
        body, name=name, in_specs=[ANY] * n, out_specs=[ANY] * n,
        out_shape=[_sds(a.shape, a.dtype) for a in arrs],
        scratch_shapes=[pltpu.SemaphoreType.DMA((3 * n,)), pltpu.SemaphoreType.DMA((3 * n,)), pltpu.SemaphoreType.DMA((n,))],
        compiler_params=pltpu.CompilerParams(has_side_effects=True))(*arrs)


HBM = pl.BlockSpec(memory_space=pltpu.HBM)
SEM = pl.BlockSpec(memory_space=pltpu.SEMAPHORE)
EFFECT = pltpu.SideEffectType.DATAFLOW_SIDE_EFFECTING


def _near_copies(srcs, lands, send_sems, recv_sems):
    x, y, c, chips = _place()
    me = 4 * x + 2 * y + c
    out = []
    for i in range(len(srcs)):
        for k, (px, py, pc) in enumerate([(x, y, 1 - c)] + [(*chip, c) for chip in chips]):
            out.append(tuple(pltpu.make_async_remote_copy(
                src_ref=srcs[i], dst_ref=lands[i].at[slot], send_sem=send_sems.at[4 * i + k], recv_sem=recv_sems.at[4 * i + k],
                device_id=(px, py, pc), device_id_type=MESH) for slot in (me, 4 * px + 2 * py + pc)))
    return out


def _forward_sibling(lands, *, name):
    n = len(lands)

    def body(*refs):
        bufs = refs[n:2 * n]
        send_sems, recv_sems = refs[2 * n:]
        x, y, c, chips = _place()
        copies = [tuple(pltpu.make_async_remote_copy(
            src_ref=bufs[i].at[4 * chip[0] + 2 * chip[1] + c], dst_ref=bufs[i].at[4 * chip[0] + 2 * chip[1] + cc],
            send_sem=send_sems.at[3 * i + j], recv_sem=recv_sems.at[3 * i + j], device_id=(x, y, 1 - c), device_id_type=MESH)
            for cc in (c, 1 - c)) for i in range(n) for j, chip in enumerate(chips)]
        for send, _ in copies:
            send.start()
        for send, recv in copies:
            send.wait_send()
            recv.wait_recv()

    return pl.pallas_call(
        body, name=name, in_specs=[ANY] * n, out_specs=[ANY] * n, out_shape=[_sds(a.shape, a.dtype) for a in lands],
        input_output_aliases={i: i for i in range(n)},
        scratch_shapes=[pltpu.SemaphoreType.DMA((3 * n,)), pltpu.SemaphoreType.DMA((3 * n,))],
        compiler_params=pltpu.CompilerParams(has_side_effects=True))(*lands)


def _chip_copies(srcs, lands, send_sems, recv_sems):
    x, y, c, chips = _place()
    jme = 2 * x + y
    out = []
    for i in range(len(srcs)):
        for j, chip in enumerate(chips):
            jt = 2 * chip[0] + chip[1]
            out.append(tuple(pltpu.make_async_remote_copy(
                src_ref=srcs[i].at[s], dst_ref=lands[i].at[d], send_sem=send_sems.at[3 * i + j], recv_sem=recv_sems.at[3 * i + j],
                device_id=(*chip, c), device_id_type=MESH) for s, d in ((jt, jme), (jme, jt))))
    return out


def _split_start(srcs, lands, pattern, ncopy, *, after=(), name):
    n = len(srcs)
    na = len(after)

    def body(*refs):
        sems = refs[2 * n + na:]
        for send, _ in pattern(refs[:n], refs[n:2 * n], sems[0], sems[1]):
            send.start()
        refs[-1][...] = jnp.zeros_like(refs[-1])

    arrs = list(srcs) + list(lands)
    return pl.pallas_call(
        body, name=name,
        out_shape=(pltpu.SemaphoreType.DMA((ncopy,)), pltpu.SemaphoreType.DMA((ncopy,)),
                   *[pltpu.HBM(a.shape, a.dtype) for a in arrs], _sds((8, 128), F32)),
        in_specs=[HBM] * (2 * n) + [ANY] * na, out_specs=(SEM, SEM, *[HBM] * (2 * n), pl.BlockSpec(memory_space=pltpu.VMEM)),
        input_output_aliases={i: 2 + i for i in range(2 * n)},
        compiler_params=pltpu.CompilerParams(has_side_effects=EFFECT))(
            *[pltpu.with_memory_space_constraint(a, pltpu.HBM) for a in arrs], *after)


def _split_wait(started, after, pattern, *, name):
    send_sems, recv_sems, *arrs = started[:-1]
    n = len(arrs) // 2

    def body(*refs):
        for send, recv in pattern(refs[:n], refs[n:2 * n], refs[2 * n], refs[2 * n + 1]):
            send.wait_send()
            recv.wait_recv()

    outs = pl.pallas_call(
        body, name=name, out_shape=tuple(pltpu.HBM(a.shape, a.dtype) for a in arrs),
        in_specs=[HBM] * (2 * n) + [SEM, SEM, ANY], out_specs=tuple([HBM] * (2 * n)),
        input_output_aliases={i: i for i in range(2 * n)},
        compiler_params=pltpu.CompilerParams(has_side_effects=EFFECT))(*arrs, send_sems, recv_sems, after)
    return list(outs[n:])


def _adamw(parts, w, m, v, *, layer=None, prev=None, name):
    P, R, C = parts.shape
    tr = next((t for t in (512, 256, 176, 128, 64, 32, 16, 8) if R % t == 0 and t * C <= 256 * 1024), R)
    c1 = 1.0 / (1.0 - ADAM_B1 ** ADAM_STEP)
    c2 = 1.0 / (1.0 - ADAM_B2 ** ADAM_STEP)
    nprev = 0 if prev is None else 4

    def body(p_ref, w_ref, m_ref, v_ref, *rest):
        g_ref, d_ref, nm_ref, nv_ref = rest[nprev:]
        g = p_ref[0].astype(F32)
        for j in range(1, P):
            g = g + p_ref[j].astype(F32)
        g = g.reshape(w_ref.shape)
        nm = ADAM_B1 * m_ref[...] + (1.0 - ADAM_B1) * g
        nv = ADAM_B2 * v_ref[...] + (1.0 - ADAM_B2) * (g * g)
        g_ref[...] = g
        nm_ref[...] = nm
        nv_ref[...] = nv
        d_ref[...] = -ADAM_LR * ((nm * c1) / (jnp.sqrt(nv * c2) + ADAM_EPS) + ADAM_WD * w_ref[...])

    if layer is None:
        row = pl.BlockSpec((tr, C), lambda i: (i, 0))
    else:
        row = pl.BlockSpec((1, tr, C), lambda i: (layer, i, 0))
    out = _sds(w.shape, F32)
    return pl.pallas_call(
        body, name=name, grid=(R // tr,),
        in_specs=[pl.BlockSpec((P, tr, C), lambda i: (0, i, 0)), row, row, row] + [ANY] * nprev,
        out_specs=[row, row, row, row], out_shape=[out, out, out, out],
        input_output_aliases={4 + k: k for k in range(nprev)},
        compiler_params=_params(("parallel",)))(parts, w, m, v, *(prev or ()))


BIG = ("w_in", "w_out", "xa_wq", "xa_wkv", "xa_wo", "ffn_w_up", "ffn_w_down")
COL_SHARDED = ("w_in", "xa_wkv", "ffn_w_up", "ml_conv_w", "ffn_conv_w")
SHARDED_SMALL = ("ml_conv_w", "ffn_conv_w")
REPLICATED = ("rel_bias", "ml_conv_b", "ml_i_bias", "ml_f_bias", "ml_norm_g", "swa_sinks", "ln1_g", "ln1_b", "ln2_g", "ln2_b",
              "ffn_conv_b", "ln3_g", "ln3_b")
NAMES = ("rel_bias", "w_in", "ml_conv_w", "ml_conv_b", "ml_i_bias", "ml_f_bias", "ml_norm_g", "swa_sinks", "w_out", "ln1_g", "ln1_b",
         "xa_wq", "xa_wkv", "xa_wo", "ln2_g", "ln2_b", "ffn_w_up", "ffn_conv_w", "ffn_conv_b", "ffn_w_down", "ln3_g", "ln3_b")


def _flat_rows(a, mult):
    f = a.reshape(-1)
    n = -(-f.shape[0] // (128 * mult)) * (128 * mult)
    if n != f.shape[0]:
        f = jnp.pad(f, (0, n - f.shape[0]))
    return f.reshape(-1, 128)


def _pack(arrs, mult):
    parts = [_flat_rows(a, mult) for a in arrs]
    return jnp.concatenate(parts, axis=0), [p.shape[0] for p in parts]


def _unpack(flat, rows, shapes):
    out, off = [], 0
    lead = flat.shape[:-2]
    for r, shp in zip(rows, shapes):
        n = int(np.prod(shp))
        piece = flat[..., off:off + r, :].reshape(lead + (r * 128,))[..., :n]
        out.append(piece.reshape(lead + tuple(shp)))
        off += r
    return out


def _full_from_shards(stacked, name):
    if name in COL_SHARDED:
        return jnp.moveaxis(stacked, 0, 2).reshape(stacked.shape[1], stacked.shape[2], N_DEV * stacked.shape[3])
    return jnp.moveaxis(stacked, 0, 1).reshape(stacked.shape[1], N_DEV * stacked.shape[2], stacked.shape[3])


def _shards_from_full(full, name):
    L, A, B = full.shape
    if name in COL_SHARDED:
        return jnp.moveaxis(full.reshape(L, A, N_DEV, B // N_DEV), 2, 0)
    return jnp.moveaxis(full.reshape(L, N_DEV, A // N_DEV, B), 1, 0)


def _pad_win(w):
    z = jnp.zeros(w.shape[:-1] + (NP_IN - N_IN,), w.dtype)
    return jnp.concatenate([w[..., :2048], w[..., 2056:], w[..., 2048:2056], z], axis=-1)


def _row128(v):
    return jnp.pad(v, (0, 128 - v.shape[0])).reshape(1, 128)


REST = BIG[1:]


def _layer_full(stacked, n):
    if n in COL_SHARDED:
        full = jnp.moveaxis(stacked, 0, 1).reshape(stacked.shape[1], N_DEV * stacked.shape[2])
    else:
        full = stacked.reshape(N_DEV * stacked.shape[1], stacked.shape[2])
    return _pad_win(full) if n == "w_in" else full


def _layer_shards(g, n):
    A, B = g.shape
    if n in COL_SHARDED:
        return jnp.moveaxis(g.reshape(A, N_DEV, B // N_DEV), 1, 0).astype(BF16)
    return g.reshape(N_DEV, A // N_DEV, B).astype(BF16)


def _with_own_block(a, idx, nblk):
    return lax.dynamic_update_slice(jnp.zeros((nblk,) + a.shape, a.dtype), a[None], (idx,) + (0,) * a.ndim)


def _gather_start(arrs, me, *, after=(), name):
    return _split_start(arrs, [_with_own_block(a, me, N_DEV) for a in arrs], _near_copies, 4 * len(arrs), after=after, name=name)


def _gather_finish(started, after, *, name):
    return _forward_sibling(_split_wait(started, after, _near_copies, name=name + "_wait"), name=name + "_forward")


def _reduce_chips(tag, names, grads, c_idx, jme, *, split):
    send = [_layer_shards(g, n) for n, g in zip(names, grads)]
    got = _swap_sibling(send, name=f"swap_sibling_{tag}")
    sums = [_pair_sum(s, g, c_idx, name=f"pair_sum_{tag}_{n}") for n, s, g in zip(names, send, got)]
    if not split:
        return _swap_chips(sums, name=f"swap_chips_{tag}")
    lands = [_with_own_block(lax.dynamic_index_in_dim(q, jme, 0, keepdims=False), jme, N_CHIP) for q in sums]
    return _split_start(sums, lands, _chip_copies, 3 * len(sums), name=f"swap_chips_{tag}_start")


def _update_small(Gf, loss_part, W, Mo, Vo, me):
    res = {}
    small = REPLICATED + SHARDED_SMALL
    sp_flat, sp_rows = _pack([Gf[n] for n in small] + [loss_part], 8)
    sp_all = _gather([sp_flat], name="gather_small_grads")[0]

    def widen(n, t):
        if n not in SHARDED_SMALL:
            return t[n]
        return lax.dynamic_update_slice(jnp.zeros(Gf[n].shape, F32), t[n], (0, 0, me * t[n].shape[2]))

    zl = jnp.zeros((8, 128), F32)
    wsm, _ = _pack([widen(n, W) for n in small] + [zl], 8)
    msm, _ = _pack([widen(n, Mo) for n in small] + [zl], 8)
    vsm, _ = _pack([widen(n, Vo) for n in small] + [zl], 8)
    outs_small = [_unpack(o_, sp_rows, [Gf[n].shape for n in small] + [(8, 128)])
                  for o_ in _adamw(sp_all, wsm, msm, vsm, name="adamw_small")]
    for kind, os_ in zip(("g", "d", "m", "v"), outs_small):
        for n, a in zip(small, os_[:-1]):
            if n in SHARDED_SMALL:
                a = lax.dynamic_slice(a, (0, 0, me * W[n].shape[2]), W[n].shape)
            res[kind, n] = a
    return res, outs_small[0][-1][0, 0]


def kernel(x, mem, rel_bias, w_in, ml_conv_w, ml_conv_b, ml_i_bias, ml_f_bias, ml_norm_g, swa_sinks, w_out, ln1_g, ln1_b, xa_wq, xa_wkv, xa_wo, ln2_g, ln2_b, ffn_w_up, ffn_conv_w, ffn_conv_b, ffn_w_down, ln3_g, ln3_b, loss_target, m_rel_bias, m_w_in, m_ml_conv_w, m_ml_conv_b, m_ml_i_bias, m_ml_f_bias, m_ml_norm_g, m_swa_sinks, m_w_out, m_ln1_g, m_ln1_b, m_xa_wq, m_xa_wkv, m_xa_wo, m_ln2_g, m_ln2_b, m_ffn_w_up, m_ffn_conv_w, m_ffn_conv_b, m_ffn_w_down, m_ln3_g, m_ln3_b, v_rel_bias, v_w_in, v_ml_conv_w, v_ml_conv_b, v_ml_i_bias, v_ml_f_bias, v_ml_norm_g, v_swa_sinks, v_w_out, v_ln1_g, v_ln1_b, v_xa_wq, v_xa_wkv, v_xa_wo, v_ln2_g, v_ln2_b, v_ffn_w_up, v_ffn_conv_w, v_ffn_conv_b, v_ffn_w_down, v_ln3_g, v_ln3_b):
    W = dict(rel_bias=rel_bias, w_in=w_in, ml_conv_w=ml_conv_w, ml_conv_b=ml_conv_b, ml_i_bias=ml_i_bias, ml_f_bias=ml_f_bias,
             ml_norm_g=ml_norm_g, swa_sinks=swa_sinks, w_out=w_out, ln1_g=ln1_g, ln1_b=ln1_b, xa_wq=xa_wq, xa_wkv=xa_wkv,
             xa_wo=xa_wo, ln2_g=ln2_g, ln2_b=ln2_b, ffn_w_up=ffn_w_up, ffn_conv_w=ffn_conv_w, ffn_conv_b=ffn_conv_b,
             ffn_w_down=ffn_w_down, ln3_g=ln3_g, ln3_b=ln3_b)
    Mo = dict(rel_bias=m_rel_bias, w_in=m_w_in, ml_conv_w=m_ml_conv_w, ml_conv_b=m_ml_conv_b, ml_i_bias=m_ml_i_bias,
              ml_f_bias=m_ml_f_bias, ml_norm_g=m_ml_norm_g, swa_sinks=m_swa_sinks, w_out=m_w_out, ln1_g=m_ln1_g, ln1_b=m_ln1_b,
              xa_wq=m_xa_wq, xa_wkv=m_xa_wkv, xa_wo=m_xa_wo, ln2_g=m_ln2_g, ln2_b=m_ln2_b, ffn_w_up=m_ffn_w_up,
              ffn_conv_w=m_ffn_conv_w, ffn_conv_b=m_ffn_conv_b, ffn_w_down=m_ffn_w_down, ln3_g=m_ln3_g, ln3_b=m_ln3_b)
    Vo = dict(rel_bias=v_rel_bias, w_in=v_w_in, ml_conv_w=v_ml_conv_w, ml_conv_b=v_ml_conv_b, ml_i_bias=v_ml_i_bias,
              ml_f_bias=v_ml_f_bias, ml_norm_g=v_ml_norm_g, swa_sinks=v_swa_sinks, w_out=v_w_out, ln1_g=v_ln1_g, ln1_b=v_ln1_b,
              xa_wq=v_xa_wq, xa_wkv=v_xa_wkv, xa_wo=v_xa_wo, ln2_g=v_ln2_g, ln2_b=v_ln2_b, ffn_w_up=v_ffn_w_up,
              ffn_conv_w=v_ffn_conv_w, ffn_conv_b=v_ffn_conv_b, ffn_w_down=v_ffn_w_down, ln3_g=v_ln3_g, ln3_b=v_ln3_b)
    S = x.shape[1]
    c_me = lax.axis_index("c")
    me = 4 * lax.axis_index("x") + 2 * lax.axis_index("y") + c_me
    c_idx = jnp.reshape(c_me, (1,)).astype(jnp.int32)
    xs = x.reshape(S, D)
    mems = mem.reshape(mem.shape[1], D)
    tgt = loss_target.reshape(S, D)

    jme = 2 * lax.axis_index("x") + lax.axis_index("y")

    sm_flat, sm_rows = _pack([W[n] for n in SHARDED_SMALL], 8)
    first = _gather([w_in[0].astype(BF16), sm_flat], name="gather_first")
    rest0 = _gather_start([W[n][0].astype(BF16) for n in REST], me, name="gather_rest0_start")
    full = [{"w_in": _layer_full(first[0], "w_in")}, None]
    conv_w = {n: _full_from_shards(s, n) for n, s in zip(SHARDED_SMALL, _unpack(first[1], sm_rows, [W[n].shape for n in SHARDED_SMALL]))}

    bucket = jnp.asarray(_t5_buckets())
    bias = _bias_table(rel_bias, bucket, name="bias_table")

    saved = []
    h0, h0b = xs, xs.astype(BF16)
    for l in range(DEPTH):
        gbias = _row128(jnp.concatenate([ml_i_bias[l], ml_f_bias[l]]))
        sinks = _row128(swa_sinks[l])
        ng = ml_norm_g[l].reshape(1, ML_W)
        proj = _mm(h0b, full[l]["w_in"], dep=(rest0[-1],) if l == 0 else (), name=f"proj{l}")
        qk = _silu_conv_fwd(proj, conv_w["ml_conv_w"][l], ml_conv_b[l].reshape(1, -1), name=f"mlconv{l}")
        h_ml, cs, ns, ms = _mlstm_fwd(qk, proj, gbias, ng, name=f"mlstm{l}")
        h_sw = _swa_fwd(proj, bias, sinks, name=f"swa{l}")
        dep = ()
        if l == 0:
            landed = _gather_finish(rest0, h_sw, name="gather_rest0")
            full[0].update({n: _layer_full(s, n) for n, s in zip(REST, landed)})
            layer1 = _gather_start([W[n][1].astype(BF16) for n in BIG], me, after=(landed[0],), name="gather_layer1_start")
            dep = (layer1[-1],)
        fw = full[l]
        h1, h1b, z1 = _mm_res_ln([h_ml, h_sw], fw["w_out"], h0, ln1_g[l].reshape(1, D), ln1_b[l].reshape(1, D), name=f"mix_out{l}")
        q = _mm(h1b, fw["xa_wq"], dep=dep, name=f"xa_q{l}")
        kv = _mm(mems, fw["xa_wkv"], tm=256, name=f"xa_kv{l}")
        o = _xattn_fwd(q, kv, name=f"xattn{l}")
        h2, h2b, z2 = _mm_res_ln([o], fw["xa_wo"], h1, ln2_g[l].reshape(1, D), ln2_b[l].reshape(1, D), name=f"xa_out{l}")
        act = _ffn_gate_fwd(h2b, fw["ffn_w_up"], conv_w["ffn_conv_w"][l], ffn_conv_b[l].reshape(1, -1), name=f"ffn_gate{l}")
        h3, h3b, z3 = _mm_res_ln([act], fw["ffn_w_down"], h2, ln3_g[l].reshape(1, D), ln3_b[l].reshape(1, D), name=f"ffn_out{l}")
        saved.append(dict(h0b=h0b, proj=proj, qk=qk, cs=cs, ns=ns, ms=ms, h_ml=h_ml, h_sw=h_sw, z1=z1, h1b=h1b, q=q, kv=kv, o=o,
                          z2=z2, h2b=h2b, act=act, z3=z3, gbias=gbias, sinks=sinks, ng=ng))
        h0, h0b = h3, h3b
        if l == 0:
            landed = _gather_finish(layer1, h3b, name="gather_layer1")
            full[1] = {n: _layer_full(s, n) for n, s in zip(BIG, landed)}

    loss_part, dh = _loss_head(h0, tgt, name="loss_head")

    G = {n: [None] * DEPTH for n in NAMES if n != "rel_bias"}
    dbias = [None] * DEPTH
    dep, pending = (), []
    for l in reversed(range(DEPTH)):
        sv, fw = saved[l], full[l]
        win = fw["w_in"]
        dz3, dz3b, G["ln3_g"][l], G["ln3_b"][l] = _ln_bwd(dh, sv["z3"], ln3_g[l].reshape(1, D), dep=dep, name=f"ln3_bwd{l}")
        G["ffn_w_down"][l] = _mm_tn(sv["act"], dz3b, name=f"d_w_down{l}")
        dupg, dupv, G["ffn_conv_w"][l], G["ffn_conv_b"][l] = _ffn_gate_bwd(
            dz3b, fw["ffn_w_down"], sv["h2b"], fw["ffn_w_up"], conv_w["ffn_conv_w"][l], ffn_conv_b[l].reshape(1, -1),
            name=f"ffn_gate_bwd{l}")
        G["ffn_w_up"][l] = jnp.concatenate([_mm_tn(sv["h2b"], dupg, name=f"d_w_up_g{l}"),
                                            _mm_tn(sv["h2b"], dupv, name=f"d_w_up_v{l}")], axis=1)
        dep = ()
        if l == 0:
            names = ("ffn_w_up", "ffn_w_down")
            pending.append((names, 0, _reduce_chips("ffn0", names, [G[n][0] for n in names], c_idx, jme, split=True)))
            dep = (pending[-1][2][-1],)
        dh2 = _mm([dupg, dupv], fw["ffn_w_up"], trans_b=True, add=dz3, add_scale=ALPHA, dep=dep, name=f"d_h2_{l}")

        dz2, dz2b, G["ln2_g"][l], G["ln2_b"][l] = _ln_bwd(dh2, sv["z2"], ln2_g[l].reshape(1, D), name=f"ln2_bwd{l}")
        G["xa_wo"][l] = _mm_tn(sv["o"], dz2b, name=f"d_xa_wo{l}")
        do = _mm(dz2b, fw["xa_wo"], trans_b=True, name=f"d_xa_o{l}")
        dq, dkv = _xattn_bwd(do, sv["q"], sv["kv"], name=f"xattn_bwd{l}")
        G["xa_wkv"][l] = _mm_tn(mems, dkv, name=f"d_xa_wkv{l}")
        G["xa_wq"][l] = _mm_tn(sv["h1b"], dq, name=f"d_xa_wq{l}")
        dh1 = _mm(dq, fw["xa_wq"], trans_b=True, add=dz2, add_scale=ALPHA, name=f"d_h1_{l}")

        dz1, dz1b, G["ln1_g"][l], G["ln1_b"][l] = _ln_bwd(dh1, sv["z1"], ln1_g[l].reshape(1, D), name=f"ln1_bwd{l}")
        G["w_out"][l] = jnp.concatenate([_mm_tn(sv["h_ml"], dz1b, name=f"d_w_out_ml{l}"),
                                         _mm_tn(sv["h_sw"], dz1b, name=f"d_w_out_sw{l}")], axis=0)
        dhcat = _mm(dz1b, fw["w_out"], trans_b=True, name=f"d_hcat{l}")
        dsw, dbias[l], dsinks = _swa_bwd(dhcat, sv["proj"], bias, sv["sinks"], name=f"swa_bwd{l}")
        dqk, dml, dgb, dng = _mlstm_bwd(dhcat, sv["qk"], sv["proj"], sv["gbias"], sv["ng"], sv["cs"], sv["ns"], sv["ms"],
                                        name=f"mlstm_bwd{l}")
        dqk_pre, G["ml_conv_w"][l], G["ml_conv_b"][l] = _silu_conv_bwd(
            dqk, sv["proj"], conv_w["ml_conv_w"][l], ml_conv_b[l].reshape(1, -1), name=f"mlconv_bwd{l}")
        dw_qk = _mm_tn(sv["h0b"], dqk_pre, name=f"d_w_in_qk{l}")
        dw_ml = _mm_tn(sv["h0b"], dml, name=f"d_w_in_ml{l}")
        dw_sw = _mm_tn(sv["h0b"], dsw, name=f"d_w_in_sw{l}")
        G["w_in"][l] = jnp.concatenate([dw_qk, dw_ml[:, :2 * ML_W + 2 * ML_H], dw_sw], axis=1)
        win_ml = jnp.concatenate([win[:, 1024:2048], win[:, 2816:2944]], axis=1)
        dh = _mm(dqk_pre, win[:, :1024], trans_b=True, add=dz1, add_scale=ALPHA, name=f"d_h0_qk{l}")
        dh = _mm(dml, win_ml, trans_b=True, add=dh, name=f"d_h0_ml{l}")
        dh = _mm(dsw, win[:, 2048:2816], trans_b=True, add=dh, name=f"d_h0_sw{l}")
        G["ml_i_bias"][l] = dgb[0, :ML_H]
        G["ml_f_bias"][l] = dgb[0, ML_H:2 * ML_H]
        G["ml_norm_g"][l] = dng
        G["swa_sinks"][l] = dsinks[0, :SW_H]
        if l == 1:
            pending.append((BIG, 1, _reduce_chips("l1", BIG, [G[n][1] for n in BIG], c_idx, jme, split=True)))
            dep = (pending[-1][2][-1],)
    grad_x = dh.reshape(x.shape)

    names = ("w_in", "w_out", "xa_wq", "xa_wkv", "xa_wo")
    parts = {(n, 0): p for n, p in zip(names, _reduce_chips("rest0", names, [G[n][0] for n in names], c_idx, jme, split=False))}
    for names, l, started in pending:
        landed = _split_wait(started, parts["w_in", 0], _chip_copies, name=f"swap_chips_{'l1' if l else 'ffn0'}_wait")
        parts.update({(n, l): p for n, p in zip(names, landed)})
    res = {}
    for n in BIG:
        outs = None
        for l in reversed(range(DEPTH)):
            outs = _adamw(parts[n, l], W[n], Mo[n], Vo[n], layer=l, prev=outs, name=f"adamw_{n}{l}")
        for kind, a in zip(("g", "d", "m", "v"), outs):
            res[kind, n] = a

    Gf = {n: jnp.stack([g.reshape(W[n].shape[1:]) if n in REPLICATED else g for g in G[n]]) for n in G if n not in BIG}
    Gf["rel_bias"] = _bias_table_bwd(dbias, bucket, name="bias_table_bwd")[:, :SW_H]
    res_small, loss = _update_small(Gf, loss_part, W, Mo, Vo, me)
    res.update(res_small)
    return (loss, grad_x, *[res["g", n] for n in NAMES], *[res["d", n] for n in NAMES], *[res["m", n] for n in NAMES],
            *[res["v", n] for n in NAMES])
```

```python
import functools
import math

import jax
import jax.numpy as jnp
import numpy as np
from jax import lax
from jax.experimental import pallas as pl
from jax.experimental.pallas import tpu as pltpu

F32 = jnp.float32
BF16 = jnp.bfloat16

N_DEV = 8
N_CHIP = 4
D = 1024
DEPTH = 2
ML_H = 4
ML_W = 512
ML_DH = 128
ML_L = 64
ML_CONV = 4
SW_DH = 64
SW_W = 512
SW_H = 8
SW_G = 4
SW_KVW = 128
BLK = 128
REL_B = 32
REL_MAXD = 128
XA_H = 4
XA_DH = 256
DFF = 2816
NB_FF = DFF // 128
FFN_CONV = 3
ALPHA = (2.0 * DEPTH) ** 0.25
EPS = 1e-5
N_IN = 2824
NP_IN = 3072
ML_GW = 2 * ML_W + 128
SW_GW = SW_W + 2 * SW_KVW
ADAM_LR = 0.001
ADAM_B1 = 0.9
ADAM_B2 = 0.999
ADAM_EPS = 1e-08
ADAM_WD = 0.01
ADAM_STEP = 10
VMEM_LIMIT = 56 * 1024 * 1024
MESH = pl.DeviceIdType.MESH

NN = ((1,), (0,))
NT = ((1,), (1,))
TN = ((0,), (0,))


def _dg(a, b, dn):
    if a.ndim == 3:
        dims = (((dn[0][0] + 1,), (dn[1][0] + 1,)), ((0,), (0,)))
    else:
        dims = (dn, ((), ()))
    return lax.dot_general(a.astype(BF16), b.astype(BF16), dims, preferred_element_type=F32)


@jax.custom_vjp
def dot_nn(a, b):
    return _dg(a, b, NN)


dot_nn.defvjp(lambda a, b: (_dg(a, b, NN), (a, b)), lambda r, g: (_dg(g, r[1], NT), _dg(r[0], g, TN)))


@jax.custom_vjp
def dot_nt(a, b):
    return _dg(a, b, NT)


dot_nt.defvjp(lambda a, b: (_dg(a, b, NT), (a, b)), lambda r, g: (_dg(g, r[1], NN), _dg(g, r[0], TN)))


@jax.custom_vjp
def dot_tn(a, b):
    return _dg(a, b, TN)


dot_tn.defvjp(lambda a, b: (_dg(a, b, TN), (a, b)), lambda r, g: (_dg(r[1], g, NT), _dg(r[0], g, NN)))


def _params(sem=None):
    return pltpu.CompilerParams(dimension_semantics=sem, vmem_limit_bytes=VMEM_LIMIT)


def _sds(shape, dtype):
    return jax.ShapeDtypeStruct(tuple(shape), dtype)


TOKEN = pl.BlockSpec((8, 128), lambda *_: (0, 0))


def _mm(a, b, *, trans_b=False, out_dtype=F32, add=None, add_scale=1.0, tm=1024, tn=512, dep=(), name):
    a_list = list(a) if isinstance(a, (list, tuple)) else [a]
    M = a_list[0].shape[0]
    N = b.shape[0] if trans_b else b.shape[1]
    tm = min(tm, M)
    tn = next(t for t in (tn, 384, 256, 128) if N % t == 0)
    assert M % tm == 0
    Ka = a_list[0].shape[1]
    assert all(t.shape[1] == Ka for t in a_list)
    tk = next(t for t in (Ka, 1408, 1024) if Ka % t == 0 and t <= 1408)
    na, npa = len(a_list), Ka // tk
    nk = na * npa
    has_add = add is not None

    def body(*refs):
        a_refs, b_ref = refs[:na], refs[na]
        add_ref = refs[na + 1] if has_add else None
        o_ref, acc_ref = refs[-2], refs[-1]
        k = pl.program_id(2)

        def finish(r):
            if has_add:
                r = r + add_scale * add_ref[...].astype(F32)
            o_ref[...] = r.astype(out_dtype)

        for t, a_ref in enumerate(a_refs):
            def step(a_ref=a_ref):
                p = _dg(a_ref[...], b_ref[...], NT if trans_b else NN)
                if nk == 1:
                    finish(p)
                    return

                @pl.when(k == 0)
                def _():
                    acc_ref[...] = p

                @pl.when((k > 0) & (k < nk - 1))
                def _():
                    acc_ref[...] += p

                @pl.when(k == nk - 1)
                def _():
                    finish(acc_ref[...] + p)

            if na == 1:
                step()
            else:
                pl.when((k >= t * npa) & (k < (t + 1) * npa))(step)

    in_specs = [pl.BlockSpec((tm, tk), lambda i, j, k, t=t: (i, jnp.clip(k - t * npa, 0, npa - 1))) for t in range(na)]
    in_specs.append(pl.BlockSpec((tn, tk), lambda i, j, k: (j, k)) if trans_b else pl.BlockSpec((tk, tn), lambda i, j, k: (k, j)))
    args = a_list + [b]
    if has_add:
        in_specs.append(pl.BlockSpec((tm, tn), lambda i, j, k: (i, j)))
        args.append(add)
    in_specs += [TOKEN] * len(dep)
    args += list(dep)
    return pl.pallas_call(
        body, name=name, grid=(M // tm, N // tn, nk), in_specs=in_specs,
        out_specs=pl.BlockSpec((tm, tn), lambda i, j, k: (i, j)), out_shape=_sds((M, N), out_dtype),
        scratch_shapes=[pltpu.VMEM((tm, tn) if nk > 1 else (8, 128), F32)],
        compiler_params=_params(("parallel", "parallel", "arbitrary")))(*args)


def _mm_tn(a, g, *, name):
    S, K = a.shape
    N = g.shape[1]
    tk = K if K <= 1024 else K // 2
    tn = next(t for t in range(1536, 0, -128) if N % t == 0)
    ts = min(512, S)
    ns = S // ts
    assert K % tk == 0 and S % ts == 0

    def body(a_ref, g_ref, o_ref):
        s = pl.program_id(2)
        p = _dg(a_ref[...], g_ref[...], TN)

        @pl.when(s == 0)
        def _():
            o_ref[...] = p

        @pl.when(s > 0)
        def _():
            o_ref[...] += p

    return pl.pallas_call(
        body, name=name, grid=(K // tk, N // tn, ns),
        in_specs=[pl.BlockSpec((ts, tk), lambda i, j, s: (s, i)), pl.BlockSpec((ts, tn), lambda i, j, s: (s, j))],
        out_specs=pl.BlockSpec((tk, tn), lambda i, j, s: (i, j)), out_shape=_sds((K, N), F32),
        compiler_params=_params(("parallel", "parallel", "arbitrary")))(a, g)


def _mm_res_ln(a_list, w, resid, gam, bet, *, name):
    M, Ka = a_list[0].shape
    na = len(a_list)
    assert w.shape[0] == na * Ka
    tm = min(256, M)

    def body(*refs):
        a_refs, w_refs = refs[:na], refs[na:2 * na]
        r_ref, g_ref, b_ref, y_ref, yb_ref, z_ref = refs[2 * na:]
        z = ALPHA * r_ref[...]
        for a_ref, w_ref in zip(a_refs, w_refs):
            z = z + _dg(a_ref[...], w_ref[...], NN)
        mu = jnp.mean(z, axis=1, keepdims=True)
        zc = z - mu
        var = jnp.mean(zc * zc, axis=1, keepdims=True)
        y = zc * lax.rsqrt(var + EPS) * g_ref[...] + b_ref[...]
        y_ref[...] = y
        yb_ref[...] = y.astype(BF16)
        z_ref[...] = z

    row = pl.BlockSpec((tm, D), lambda i: (i, 0))
    vec = pl.BlockSpec((1, D), lambda i: (0, 0))
    a_specs = [pl.BlockSpec((tm, Ka), lambda i: (i, 0)) for _ in a_list]
    w_specs = [pl.BlockSpec((Ka, D), lambda i, t=t: (t, 0)) for t in range(na)]
    return pl.pallas_call(
        body, name=name, grid=(M // tm,), in_specs=a_specs + w_specs + [row, vec, vec],
        out_specs=[row, row, row], out_shape=[_sds((M, D), F32), _sds((M, D), BF16), _sds((M, D), F32)],
        compiler_params=_params(("parallel",)))(*a_list, *([w] * na), resid, gam, bet)


def _grad_in(pairs, add, *, ln=None, loss=None, dep=(), name):
    M = (add if add is not None else loss[0]).shape[0]
    tm = min(256, M)
    npair, nd = len(pairs), len(dep)
    has_ln, has_loss = ln is not None, loss is not None

    def body(*refs):
        n_in = 2 * npair + (2 if has_loss else 1) + 2 * has_ln + nd
        ins, outs = refs[:n_in], refs[n_in:]
        i = pl.program_id(0)
        pos = 2 * npair
        if has_loss:
            e = ins[pos][...] - ins[pos + 1][...]
            pos += 2
            dy = e * (1.0 / D)
            part = 0.5 * jnp.sum(jnp.sum(e * e, axis=1, keepdims=True) * (1.0 / D), axis=0, keepdims=True)
        else:
            dy = ALPHA * ins[pos][...]
            pos += 1
            for t in range(npair):
                dy = dy + _dg(ins[2 * t][...], ins[2 * t + 1][...], NT)
        if not has_ln:
            outs[0][...] = dy
            return
        z, g_ref = ins[pos][...], ins[pos + 1]
        mu = jnp.mean(z, axis=1, keepdims=True)
        zc = z - mu
        var = jnp.mean(zc * zc, axis=1, keepdims=True)
        rstd = lax.rsqrt(var + EPS)
        xh = zc * rstd
        dxh = dy * g_ref[...]
        m1 = jnp.mean(dxh, axis=1, keepdims=True)
        m2 = jnp.mean(dxh * xh, axis=1, keepdims=True)
        dz = rstd * (dxh - m1 - xh * m2)
        outs[0][...] = dz
        outs[1][...] = dz.astype(BF16)
        acc = [(outs[2], jnp.sum(dy * xh, axis=0, keepdims=True)), (outs[3], jnp.sum(dy, axis=0, keepdims=True))]
        if has_loss:
            acc.append((outs[4], jnp.broadcast_to(part, (8, 128))))

        @pl.when(i == 0)
        def _():
            for ref, val in acc:
                ref[...] = val

        @pl.when(i > 0)
        def _():
            for ref, val in acc:
                ref[...] += val

    row = pl.BlockSpec((tm, D), lambda i: (i, 0))
    vec = pl.BlockSpec((1, D), lambda i: (0, 0))
    in_specs, args = [], []
    for a, b, blk in pairs:
        in_specs += [pl.BlockSpec((tm, a.shape[1]), lambda i: (i, 0)), pl.BlockSpec((D, a.shape[1]), lambda i, blk=blk: (0, blk))]
        args += [a, b]
    if has_loss:
        in_specs += [row, row]
        args += list(loss)
    else:
        in_specs.append(row)
        args.append(add)
    if has_ln:
        in_specs += [row, vec]
        args += list(ln)
    in_specs += [TOKEN] * nd
    args += list(dep)
    if has_ln:
        out_specs = [row, row, vec, vec] + ([pl.BlockSpec((8, 128), lambda i: (0, 0))] if has_loss else [])
        out_shape = [_sds((M, D), F32), _sds((M, D), BF16), _sds((1, D), F32), _sds((1, D), F32)] + ([_sds((8, 128), F32)] if has_loss else [])
    else:
        out_specs, out_shape = row, _sds((M, D), F32)
    return pl.pallas_call(
        body, name=name, grid=(M // tm,), in_specs=in_specs, out_specs=out_specs, out_shape=out_shape,
        compiler_params=_params(("arbitrary",) if has_ln else ("parallel",)))(*args)


def _shift_down(x, d):
    if d == 0:
        return x
    rows = lax.broadcasted_iota(jnp.int32, x.shape, 0)
    return jnp.where(rows >= d, pltpu.roll(x, d, 0), 0.0)


def _shift_up(x, d):
    if d == 0:
        return x
    S = x.shape[0]
    rows = lax.broadcasted_iota(jnp.int32, x.shape, 0)
    return jnp.where(rows < S - d, pltpu.roll(x, S - d, 0), 0.0)


def _conv(x, w_ref, b_ref, cs, K):
    y = b_ref[:, cs]
    for j in range(K):
        y = y + _shift_down(x, K - 1 - j) * w_ref[j:j + 1, cs]
    return y


def _conv_bwd(dy, x, w_ref, dw_ref, db_ref, cs, K):
    dx = jnp.zeros_like(x)
    for j in range(K):
        dx = dx + _shift_up(dy, K - 1 - j) * w_ref[j:j + 1, cs]
        dw_ref[j:j + 1, cs] = jnp.sum(dy * _shift_down(x, K - 1 - j), axis=0, keepdims=True)
    db_ref[:, cs] = jnp.sum(dy, axis=0, keepdims=True)
    return dx


ALL = slice(None)


def _silu_conv_fwd(proj, cw, cb, *, name):
    S = proj.shape[0]

    def body(x_ref, w_ref, b_ref, o_ref):
        o_ref[...] = jax.nn.silu(_conv(x_ref[...], w_ref, b_ref, ALL, ML_CONV))

    col = pl.BlockSpec((S, 128), lambda j: (0, j))
    return pl.pallas_call(
        body, name=name, grid=(8,),
        in_specs=[col, pl.BlockSpec((ML_CONV, 128), lambda j: (0, j)), pl.BlockSpec((1, 128), lambda j: (0, j))],
        out_specs=col, out_shape=_sds((S, 2 * ML_W), F32), compiler_params=_params(("parallel",)))(proj, cw, cb)


def _silu_conv_bwd(dqk, proj, cw, cb, *, name):
    S = proj.shape[0]

    def body(d_ref, x_ref, w_ref, b_ref, dx_ref, dw_ref, db_ref):
        x = x_ref[...]
        y = _conv(x, w_ref, b_ref, ALL, ML_CONV)
        dy = jax.vjp(jax.nn.silu, y)[1](d_ref[...])[0]
        dx_ref[...] = _conv_bwd(dy, x, w_ref, dw_ref, db_ref, ALL, ML_CONV).astype(BF16)

    col = pl.BlockSpec((S, 128), lambda j: (0, j))
    wsp = pl.BlockSpec((ML_CONV, 128), lambda j: (0, j))
    bsp = pl.BlockSpec((1, 128), lambda j: (0, j))
    return pl.pallas_call(
        body, name=name, grid=(8,), in_specs=[col, col, wsp, bsp], out_specs=[col, wsp, bsp],
        out_shape=[_sds((S, 2 * ML_W), BF16), _sds((ML_CONV, 2 * ML_W), F32), _sds((1, 2 * ML_W), F32)],
        compiler_params=_params(("parallel",)))(dqk, proj, cw, cb)


GELU_C0 = math.sqrt(2.0 / math.pi)
GELU_C1 = 0.044715


def _gate_bwd(ug, uv, da):
    t = jnp.tanh(GELU_C0 * (ug + GELU_C1 * (ug * ug * ug)))
    half = 0.5 * (1.0 + t)
    dgelu = half + 0.5 * ug * (1.0 - t * t) * (GELU_C0 * (1.0 + 3.0 * GELU_C1 * (ug * ug)))
    return da * uv * dgelu, da * (ug * half)


def _ffn_specs(S):
    return (pl.BlockSpec((D, 128), lambda j: (0, j)), pl.BlockSpec((D, 128), lambda j: (0, j + NB_FF)),
            pl.BlockSpec((FFN_CONV, 128), lambda j: (0, j)), pl.BlockSpec((FFN_CONV, 128), lambda j: (0, j + NB_FF)),
            pl.BlockSpec((1, 128), lambda j: (0, j)), pl.BlockSpec((1, 128), lambda j: (0, j + NB_FF)))


def _ffn_gate_fwd(hb, w_up, cw, cb, *, name):
    S = hb.shape[0]
    ug_, uv_, wg, wv, bg, bv = _ffn_specs(S)

    def body(h_ref, ugw_ref, uvw_ref, wg_ref, wv_ref, bg_ref, bv_ref, o_ref, h_s):
        @pl.when(pl.program_id(0) == 0)
        def _():
            pltpu.sync_copy(h_ref, h_s)

        ug = _conv(_dg(h_s[...], ugw_ref[...], NN), wg_ref, bg_ref, ALL, FFN_CONV)
        uv = _conv(_dg(h_s[...], uvw_ref[...], NN), wv_ref, bv_ref, ALL, FFN_CONV)
        o_ref[...] = (jax.nn.gelu(ug) * uv).astype(BF16)

    return pl.pallas_call(
        body, name=name, grid=(NB_FF,), in_specs=[ANY, ug_, uv_, wg, wv, bg, bv],
        out_specs=pl.BlockSpec((S, 128), lambda j: (0, j)), out_shape=_sds((S, DFF), BF16),
        scratch_shapes=[pltpu.VMEM((S, D), BF16)],
        compiler_params=_params(("arbitrary",)))(hb, w_up, w_up, cw, cw, cb, cb)


def _ffn_gate_bwd(dzb, w_down, hb, w_up, cw, cb, *, name):
    S = hb.shape[0]
    ug_, uv_, wg, wv, bg, bv = _ffn_specs(S)

    def body(dz_ref, h_ref, wd_ref, ugw_ref, uvw_ref, wg_ref, wv_ref, bg_ref, bv_ref,
             dxg_ref, dxv_ref, dwg_ref, dwv_ref, dbg_ref, dbv_ref, dz_s, h_s):
        @pl.when(pl.program_id(0) == 0)
        def _():
            pltpu.sync_copy(dz_ref, dz_s)
            pltpu.sync_copy(h_ref, h_s)

        xg = _dg(h_s[...], ugw_ref[...], NN)
        xv = _dg(h_s[...], uvw_ref[...], NN)
        ug = _conv(xg, wg_ref, bg_ref, ALL, FFN_CONV)
        uv = _conv(xv, wv_ref, bv_ref, ALL, FFN_CONV)
        dug, duv = _gate_bwd(ug, uv, _dg(dz_s[...], wd_ref[...], NT))
        dxg_ref[...] = _conv_bwd(dug, xg, wg_ref, dwg_ref, dbg_ref, ALL, FFN_CONV).astype(BF16)
        dxv_ref[...] = _conv_bwd(duv, xv, wv_ref, dwv_ref, dbv_ref, ALL, FFN_CONV).astype(BF16)

    col = pl.BlockSpec((S, 128), lambda j: (0, j))
    half = _sds((S, DFF), BF16)
    dxg, dxv, dwg, dwv, dbg, dbv = pl.pallas_call(
        body, name=name, grid=(NB_FF,),
        in_specs=[ANY, ANY, pl.BlockSpec((128, D), lambda j: (j, 0)), ug_, uv_, wg, wv, bg, bv],
        out_specs=[col, col, wg, wg, bg, bg],
        out_shape=[half, half, _sds((FFN_CONV, DFF), F32), _sds((FFN_CONV, DFF), F32), _sds((1, DFF), F32), _sds((1, DFF), F32)],
        scratch_shapes=[pltpu.VMEM((S, D), BF16), pltpu.VMEM((S, D), BF16)],
        compiler_params=_params(("arbitrary",)))(dzb, hb, w_down, w_up, w_up, cw, cw, cb, cb)
    return dxg, dxv, jnp.concatenate([dwg, dwv], axis=1), jnp.concatenate([dbg, dbv], axis=1)


def _log_sigmoid(x):
    return jnp.minimum(x, 0.0) - jnp.log1p(jnp.exp(-jnp.abs(x)))


@jax.custom_vjp
def _clamp_div(num, den, floor, shift):
    return num / jnp.maximum(jnp.abs(den), floor)


def _clamp_div_fwd(num, den, floor, shift):
    out = num / jnp.maximum(jnp.abs(den), floor)
    return out, (den, floor, out)


def _clamp_div_bwd(res, g):
    den, floor, out = res
    active = jnp.abs(den) < floor
    dinv = jnp.maximum(jnp.abs(den), floor)
    go = jnp.sum(g * out, axis=-1, keepdims=True)
    ddiv = -go / dinv
    return (g / dinv, jnp.where(active, 0.0, ddiv * jnp.sign(den)), jnp.where(active, ddiv, 0.0),
            jnp.sum(jnp.where(active, go, 0.0), axis=-2, keepdims=True))


_clamp_div.defvjp(_clamp_div_fwd, _clamp_div_bwd)


def _ml_heads(q, k, v, o_pre, gates, gbias, C, n, ng, m, shift):
    H, L, _ = q.shape
    lane1 = lax.broadcasted_iota(jnp.int32, (1, 128), 1)
    gz = gates + jnp.where(lane1 < ML_H, lax.stop_gradient(gbias), gbias)
    gz = jnp.broadcast_to(gz[None], (H, L, 128))
    hid = lax.broadcasted_iota(jnp.int32, (H, L, 128), 0)
    lane = lax.broadcasted_iota(jnp.int32, (H, L, 128), 2)
    ig = jnp.sum(jnp.where(lane == hid, gz, 0.0), axis=2, keepdims=True)
    lf = _log_sigmoid(jnp.sum(jnp.where(lane == ML_H + hid, gz, 0.0), axis=2, keepdims=True))
    r = lax.broadcasted_iota(jnp.int32, (H, L, L), 1)
    c = lax.broadcasted_iota(jnp.int32, (H, L, L), 2)
    eye, tril = r == c, r >= c

    def to_row(col):
        return jnp.sum(jnp.where(eye, col, 0.0), axis=1, keepdims=True)

    b_col = jnp.sum(jnp.where(tril, to_row(lf), 0.0), axis=2, keepdims=True)
    Dm = jnp.where(tril, b_col - to_row(b_col) + to_row(ig), -jnp.inf)
    inter = b_col + m
    m_t = jnp.maximum(inter, jnp.max(Dm, axis=2, keepdims=True))
    w_inter = jnp.exp(inter - m_t)
    ks = k * (ML_DH ** -0.5)
    s = dot_nt(q, ks) * jnp.exp(Dm - m_t)
    num = w_inter * dot_nn(q, C) + dot_nn(s, v)
    den = w_inter * jnp.sum(q * n, axis=2, keepdims=True) + jnp.sum(s, axis=2, keepdims=True)
    h = _clamp_div(num, den, jnp.exp(-m_t), shift)
    g = jnp.sum(lf, axis=1, keepdims=True)
    a = g - b_col + ig
    m_new = jnp.maximum(g + m, jnp.max(a, axis=1, keepdims=True))
    decay = jnp.exp(g + m - m_new)
    wk = jnp.exp(a - m_new)
    C_new = decay * C + dot_tn(ks * wk, v)
    n_new = decay * n + jnp.sum(wk * ks, axis=1, keepdims=True)
    mu = jnp.mean(h, axis=2, keepdims=True)
    hc = h - mu
    var = jnp.mean(hc * hc, axis=2, keepdims=True)
    out = jax.nn.sigmoid(o_pre) * (hc * lax.rsqrt(var + EPS) * ng)
    return out, C_new, n_new, m_new


def _hs(h, off=0):
    return slice(off + h * ML_DH, off + (h + 1) * ML_DH)


def _heads(ref, off=0):
    return jnp.stack([ref[:, _hs(h, off)] for h in range(ML_H)])


def _mlstm_fwd(qk, proj, gbias, ng, *, name):
    S = qk.shape[0]
    nc = S // ML_L

    def body(q_ref, k_ref, v_ref, o_ref, g_ref, gb_ref, ng_ref, h_ref, cs_ref, ns_ref, ms_ref, c_s, n_s, m_s):
        @pl.when(pl.program_id(0) == 0)
        def _():
            c_s[...] = jnp.zeros_like(c_s)
            n_s[...] = jnp.zeros_like(n_s)
            m_s[...] = jnp.zeros_like(m_s)

        C, n = c_s[...], n_s[...]
        cs_ref[0] = C
        ns_ref[0] = n
        ms_ref[0] = m_s[...]
        out, C2, n2, m2 = _ml_heads(_heads(q_ref), _heads(k_ref), _heads(v_ref), _heads(o_ref), g_ref[...], gb_ref[...], C, n,
                                    _heads(ng_ref), m_s[:, :, 0:1], jnp.zeros((ML_H, 1, 1), F32))
        for h in range(ML_H):
            h_ref[:, _hs(h)] = out[h].astype(BF16)
        c_s[...] = C2
        n_s[...] = n2
        m_s[...] = jnp.broadcast_to(m2, (ML_H, 1, 128))

    def w(j):
        return pl.BlockSpec((ML_L, ML_W), lambda c, j=j: (c, j))

    return pl.pallas_call(
        body, name=name, grid=(nc,),
        in_specs=[w(0), w(1), w(2), w(3), pl.BlockSpec((ML_L, 128), lambda c: (c, 22)),
                  pl.BlockSpec((1, 128), lambda c: (0, 0)), pl.BlockSpec((1, ML_W), lambda c: (0, 0))],
        out_specs=[w(0), pl.BlockSpec((1, ML_H, ML_DH, ML_DH), lambda c: (c, 0, 0, 0)),
                   pl.BlockSpec((1, ML_H, 1, 128), lambda c: (c, 0, 0, 0)), pl.BlockSpec((1, ML_H, 1, 128), lambda c: (c, 0, 0, 0))],
        out_shape=[_sds((S, ML_W), BF16), _sds((nc, ML_H, ML_DH, ML_DH), F32), _sds((nc, ML_H, 1, 128), F32),
                   _sds((nc, ML_H, 1, 128), F32)],
        scratch_shapes=[pltpu.VMEM((ML_H, ML_DH, ML_DH), F32), pltpu.VMEM((ML_H, 1, 128), F32), pltpu.VMEM((ML_H, 1, 128), F32)],
        compiler_params=_params(("arbitrary",)))(qk, qk, proj, proj, proj, gbias, ng)


def _mlstm_bwd(dh, qk, proj, gbias, ng, cs, ns, ms, *, name):
    S = qk.shape[0]
    nc = S // ML_L

    def body(dh_ref, q_ref, k_ref, v_ref, o_ref, g_ref, gb_ref, ng_ref, cs_ref, ns_ref, ms_ref,
             dqk_ref, dml_ref, dgb_ref, dng_ref, dc_s, dn_s, dm_s):
        @pl.when(pl.program_id(0) == 0)
        def _():
            dc_s[...] = jnp.zeros_like(dc_s)
            dn_s[...] = jnp.zeros_like(dn_s)
            dm_s[...] = jnp.zeros_like(dm_s)
            dgb_ref[...] = jnp.zeros_like(dgb_ref)
            dng_ref[...] = jnp.zeros_like(dng_ref)

        _, vjp = jax.vjp(_ml_heads, _heads(q_ref), _heads(k_ref), _heads(v_ref), _heads(o_ref), g_ref[...], gb_ref[...],
                         cs_ref[0], ns_ref[0], _heads(ng_ref), ms_ref[0][:, :, 0:1], jnp.zeros((ML_H, 1, 1), F32))
        dq, dk, dv, do, dgates, dgb, dC, dn, dng, dm, dshift = vjp((_heads(dh_ref), dc_s[...], dn_s[...], dm_s[:, :, 0:1]))
        lane1 = lax.broadcasted_iota(jnp.int32, (1, 128), 1)
        for h in range(ML_H):
            dqk_ref[:, _hs(h)] = dq[h]
            dqk_ref[:, _hs(h, ML_W)] = dk[h]
            dml_ref[:, _hs(h)] = dv[h].astype(BF16)
            dml_ref[:, _hs(h, ML_W)] = do[h].astype(BF16)
            dng_ref[:, _hs(h)] += dng[h]
            dgb = jnp.where(lane1 == h, dshift[h], dgb)
        dc_s[...] = dC
        dn_s[...] = dn
        dm_s[...] = jnp.broadcast_to(dm, (ML_H, 1, 128))
        dml_ref[:, 2 * ML_W:] = dgates.astype(BF16)
        dgb_ref[...] += dgb

    def w(j):
        return pl.BlockSpec((ML_L, ML_W), lambda c, j=j: (nc - 1 - c, j))

    vec = pl.BlockSpec((1, 128), lambda c: (0, 0))
    vecw = pl.BlockSpec((1, ML_W), lambda c: (0, 0))
    gsp = pl.BlockSpec((ML_L, 128), lambda c: (nc - 1 - c, 22))
    st = pl.BlockSpec((1, ML_H, 1, 128), lambda c: (nc - 1 - c, 0, 0, 0))
    return pl.pallas_call(
        body, name=name, grid=(nc,),
        in_specs=[w(0), w(0), w(1), w(2), w(3), gsp, vec, vecw,
                  pl.BlockSpec((1, ML_H, ML_DH, ML_DH), lambda c: (nc - 1 - c, 0, 0, 0)), st, st],
        out_specs=[pl.BlockSpec((ML_L, 2 * ML_W), lambda c: (nc - 1 - c, 0)), pl.BlockSpec((ML_L, ML_GW), lambda c: (nc - 1 - c, 0)),
                   vec, vecw],
        out_shape=[_sds((S, 2 * ML_W), F32), _sds((S, ML_GW), BF16), _sds((1, 128), F32), _sds((1, ML_W), F32)],
        scratch_shapes=[pltpu.VMEM((ML_H, ML_DH, ML_DH), F32), pltpu.VMEM((ML_H, 1, 128), F32), pltpu.VMEM((ML_H, 1, 128), F32)],
        compiler_params=_params(("arbitrary",)))(dh, qk, qk, proj, proj, proj, gbias, ng, cs, ns, ms)


def _t5_buckets():
    r = np.arange(BLK)[:, None]
    c = np.arange(2 * BLK)[None, :]
    n = np.maximum(BLK + r - c, 0)
    max_exact = REL_B // 2
    nf = np.maximum(n, 1).astype(np.float32)
    large = max_exact + (np.log(nf / np.float32(max_exact)) / np.float32(math.log(REL_MAXD / max_exact))
                         * np.float32(REL_B - max_exact)).astype(np.int32)
    large = np.minimum(large, REL_B - 1)
    return np.where(n < max_exact, n, large).astype(np.int32)


def _bias_table(rel_bias, bucket, *, name):
    def body(rb_ref, bk_ref, o_ref):
        bk = bk_ref[...]
        for h in range(SW_H):
            acc = jnp.zeros((BLK, 2 * BLK), F32)
            for b in range(REL_B):
                acc = jnp.where(bk == b, rb_ref[b, h], acc)
            o_ref[h] = acc

    return pl.pallas_call(
        body, name=name, in_specs=[pl.BlockSpec(memory_space=pltpu.SMEM), pl.BlockSpec(memory_space=pltpu.VMEM)],
        out_specs=pl.BlockSpec(memory_space=pltpu.VMEM), out_shape=_sds((SW_H, BLK, 2 * BLK), F32),
        compiler_params=_params())(rel_bias, bucket)


def _bias_table_bwd(dbias_list, bucket, *, name):
    nl = len(dbias_list)

    def body(*refs):
        d_refs, bk_ref, o_ref = refs[:nl], refs[nl], refs[nl + 1]
        bk = bk_ref[...]
        rows = lax.broadcasted_iota(jnp.int32, (REL_B, 128), 0)
        lanes = lax.broadcasted_iota(jnp.int32, (REL_B, 128), 1)
        acc = jnp.zeros((REL_B, 128), F32)
        for h in range(SW_H):
            d = d_refs[0][h]
            for d_ref in d_refs[1:]:
                d = d + d_ref[h]
            for b in range(REL_B):
                t = jnp.sum(jnp.sum(jnp.where(bk == b, d, 0.0), axis=0, keepdims=True), axis=1, keepdims=True)
                acc = jnp.where((rows == b) & (lanes == h), t, acc)
        o_ref[...] = acc

    vm = pl.BlockSpec(memory_space=pltpu.VMEM)
    return pl.pallas_call(
        body, name=name, in_specs=[vm] * (nl + 1), out_specs=vm, out_shape=_sds((REL_B, 128), F32),
        compiler_params=_params())(*dbias_list, bucket)


def _swa_heads(q, kp, kc, vp, vc, bp, bc, sinks, has_prev):
    def rep(t):
        return jnp.concatenate([t[g:g + 1] for g in range(SW_H // SW_G) for _ in range(SW_G)], axis=0)

    r = lax.broadcasted_iota(jnp.int32, (SW_H, BLK, BLK), 1)
    c = lax.broadcasted_iota(jnp.int32, (SW_H, BLK, BLK), 2)
    hid = lax.broadcasted_iota(jnp.int32, (SW_H, 1, 128), 0)
    lane = lax.broadcasted_iota(jnp.int32, (SW_H, 1, 128), 2)
    sink = jnp.sum(jnp.where(lane == hid, jnp.broadcast_to(sinks[None], (SW_H, 1, 128)), 0.0), axis=2, keepdims=True)
    lp = jnp.where((c > r) & has_prev, dot_nt(q, rep(kp)) * (SW_DH ** -0.5) + bp, -jnp.inf)
    lc = jnp.where(c <= r, dot_nt(q, rep(kc)) * (SW_DH ** -0.5) + bc, -jnp.inf)
    mx = jnp.maximum(jnp.maximum(jnp.max(lp, axis=2, keepdims=True), jnp.max(lc, axis=2, keepdims=True)), sink)
    mx = lax.stop_gradient(mx)
    pp, pc = jnp.exp(lp - mx), jnp.exp(lc - mx)
    den = jnp.sum(pp, axis=2, keepdims=True) + jnp.sum(pc, axis=2, keepdims=True) + jnp.exp(sink - mx)
    return dot_nn(pp / den, rep(vp)) + dot_nn(pc / den, rep(vc))


def _qs(h, off=0):
    return slice(off + h * SW_DH, off + (h + 1) * SW_DH)


def _split(ref, n):
    return jnp.stack([ref[:, _qs(h)] for h in range(n)])


def _swa_fwd(proj, bias, sinks, *, name):
    S = proj.shape[0]
    nb = S // BLK
    nkv = SW_H // SW_G

    def body(q_ref, kp_ref, kc_ref, vp_ref, vc_ref, b_ref, s_ref, o_ref):
        out = _swa_heads(_split(q_ref, SW_H), _split(kp_ref, nkv), _split(kc_ref, nkv), _split(vp_ref, nkv), _split(vc_ref, nkv),
                         b_ref[:, :, :BLK], b_ref[:, :, BLK:], s_ref[...], pl.program_id(0) > 0)
        for h in range(SW_H):
            o_ref[:, _qs(h)] = out[h].astype(BF16)

    def cur(j):
        return pl.BlockSpec((BLK, 128), lambda n, j=j: (n, j))

    def prev(j):
        return pl.BlockSpec((BLK, 128), lambda n, j=j: (jnp.maximum(n - 1, 0), j))

    return pl.pallas_call(
        body, name=name, grid=(nb,),
        in_specs=[pl.BlockSpec((BLK, SW_W), lambda n: (n, 4)), prev(20), cur(20), prev(21), cur(21),
                  pl.BlockSpec((SW_H, BLK, 2 * BLK), lambda n: (0, 0, 0)), pl.BlockSpec((1, 128), lambda n: (0, 0))],
        out_specs=pl.BlockSpec((BLK, SW_W), lambda n: (n, 0)), out_shape=_sds((S, SW_W), BF16),
        compiler_params=_params(("parallel",)))(proj, proj, proj, proj, proj, bias, sinks)


def _swa_bwd(dh, proj, bias, sinks, *, name):
    S = proj.shape[0]
    nb = S // BLK

    nkv = SW_H // SW_G

    def body(dh_ref, q_ref, kp_ref, kc_ref, vp_ref, vc_ref, b_ref, s_ref, dsw_ref, db_ref, ds_ref, ck_s, cv_s):
        i = pl.program_id(0)

        @pl.when(i == 0)
        def _():
            ck_s[...] = jnp.zeros_like(ck_s)
            cv_s[...] = jnp.zeros_like(cv_s)
            db_ref[...] = jnp.zeros_like(db_ref)
            ds_ref[...] = jnp.zeros_like(ds_ref)

        f = functools.partial(_swa_heads, has_prev=i < nb - 1)
        _, vjp = jax.vjp(f, _split(q_ref, SW_H), _split(kp_ref, nkv), _split(kc_ref, nkv), _split(vp_ref, nkv),
                         _split(vc_ref, nkv), b_ref[:, :, :BLK], b_ref[:, :, BLK:], s_ref[...])
        dq, dkp, dkc, dvp, dvc, dbp, dbc, ds = vjp(_split(dh_ref, SW_H))
        for h in range(SW_H):
            dsw_ref[:, _qs(h)] = dq[h].astype(BF16)
        for g in range(nkv):
            dsw_ref[:, _qs(g, SW_W)] = (dkc[g] + ck_s[:, _qs(g)]).astype(BF16)
            dsw_ref[:, _qs(g, SW_W + SW_KVW)] = (dvc[g] + cv_s[:, _qs(g)]).astype(BF16)
            ck_s[:, _qs(g)] = dkp[g]
            cv_s[:, _qs(g)] = dvp[g]
        db_ref[:, :, :BLK] += dbp
        db_ref[:, :, BLK:] += dbc
        ds_ref[...] += ds

    def cur(j):
        return pl.BlockSpec((BLK, 128), lambda i, j=j: (nb - 1 - i, j))

    def prev(j):
        return pl.BlockSpec((BLK, 128), lambda i, j=j: (jnp.maximum(nb - 2 - i, 0), j))

    bsp = pl.BlockSpec((SW_H, BLK, 2 * BLK), lambda i: (0, 0, 0))
    vec = pl.BlockSpec((1, 128), lambda i: (0, 0))
    return pl.pallas_call(
        body, name=name, grid=(nb,),
        in_specs=[pl.BlockSpec((BLK, SW_W), lambda i: (nb - 1 - i, 1)), pl.BlockSpec((BLK, SW_W), lambda i: (nb - 1 - i, 4)),
                  prev(20), cur(20), prev(21), cur(21), bsp, vec],
        out_specs=[pl.BlockSpec((BLK, SW_GW), lambda i: (nb - 1 - i, 0)), bsp, vec],
        out_shape=[_sds((S, SW_GW), BF16), _sds((SW_H, BLK, 2 * BLK), F32), _sds((1, 128), F32)],
        scratch_shapes=[pltpu.VMEM((BLK, 128), F32)] * 2,
        compiler_params=_params(("arbitrary",)))(dh, proj, proj, proj, proj, proj, bias, sinks)


XA_TM = 512


def _xa_head(qh, kh, vh):
    logits = dot_nt(qh, kh) * (XA_DH ** -0.5)
    mx = lax.stop_gradient(jnp.max(logits, axis=1, keepdims=True))
    e = jnp.exp(logits - mx)
    return dot_nn(e / jnp.sum(e, axis=1, keepdims=True), vh)


def _xs(h, off=0):
    return slice(off + h * XA_DH, off + (h + 1) * XA_DH)


def _xattn_fwd(q, kv, *, name):
    S = q.shape[0]
    M = kv.shape[0]

    def body(q_ref, kv_ref, o_ref):
        for h in range(XA_H):
            o_ref[:, _xs(h)] = _xa_head(q_ref[:, _xs(h)], kv_ref[:, _xs(h)], kv_ref[:, _xs(h, D)]).astype(BF16)

    tm = min(XA_TM, S)
    row = pl.BlockSpec((tm, D), lambda i: (i, 0))
    return pl.pallas_call(
        body, name=name, grid=(S // tm,), in_specs=[row, pl.BlockSpec((M, 2 * D), lambda i: (0, 0))], out_specs=row,
        out_shape=_sds((S, D), BF16), compiler_params=_params(("parallel",)))(q, kv)


def _xattn_bwd(do, q, kv, *, name):
    S = q.shape[0]
    M = kv.shape[0]

    def body(do_ref, q_ref, kv_ref, dq_ref, dkv_ref):
        @pl.when(pl.program_id(0) == 0)
        def _():
            dkv_ref[...] = jnp.zeros_like(dkv_ref)

        for h in range(XA_H):
            _, vjp = jax.vjp(_xa_head, q_ref[:, _xs(h)], kv_ref[:, _xs(h)], kv_ref[:, _xs(h, D)])
            dq, dk, dv = vjp(do_ref[:, _xs(h)])
            dq_ref[:, _xs(h)] = dq.astype(BF16)
            dkv_ref[:, _xs(h)] += dk
            dkv_ref[:, _xs(h, D)] += dv

    tm = min(XA_TM, S)
    row = pl.BlockSpec((tm, D), lambda i: (i, 0))
    full = pl.BlockSpec((M, 2 * D), lambda i: (0, 0))
    return pl.pallas_call(
        body, name=name, grid=(S // tm,), in_specs=[row, row, full], out_specs=[row, full],
        out_shape=[_sds((S, D), BF16), _sds((M, 2 * D), F32)], compiler_params=_params(("arbitrary",)))(do, q, kv)


ANY = pl.BlockSpec(memory_space=pl.ANY)


def _place():
    x, y, c = lax.axis_index("x"), lax.axis_index("y"), lax.axis_index("c")
    chips = [(1 - x, y), (x, 1 - y), (1 - x, 1 - y)]
    return x, y, c, chips


def _gather(arrs, *, name):
    n = len(arrs)

    def body(*refs):
        srcs, outs = refs[:n], refs[n:2 * n]
        send_sems, recv_sems, local_sems = refs[2 * n:]
        x, y, c, chips = _place()
        me, sib = (x, y, c), (x, y, 1 - c)

        def idx(p):
            return 4 * p[0] + 2 * p[1] + p[2]

        def copy(i, k, block, to, from_src=False):
            return pltpu.make_async_remote_copy(
                src_ref=srcs[i] if from_src else outs[i].at[idx(block)], dst_ref=outs[i].at[idx(block)],
                send_sem=send_sems.at[7 * i + k], recv_sem=recv_sems.at[7 * i + k], device_id=to, device_id_type=MESH)

        local = [pltpu.make_async_copy(srcs[i], outs[i].at[idx(me)], local_sems.at[i]) for i in range(n)]
        for cp in local:
            cp.start()
        first = []
        for i in range(n):
            first.append(copy(i, 0, me, sib, True))
            first += [copy(i, 1 + j, me, (*chip, c), True) for j, chip in enumerate(chips)]
        for cp in first:
            cp.start()
        passed = []
        for j, chip in enumerate(chips):
            for i in range(n):
                copy(i, 1 + j, (*chip, c), me).wait_recv()
                cp = copy(i, 4 + j, (*chip, c), sib)
                cp.start()
                passed.append(cp)
        for i in range(n):
            copy(i, 0, sib, me).wait_recv()
        for j, chip in enumerate(chips):
            for i in range(n):
                copy(i, 4 + j, (*chip, 1 - c), me).wait_recv()
        for cp in first + passed:
            cp.wait_send()
        for cp in local:
            cp.wait()

    return pl.pallas_call(
        body, name=name, in_specs=[ANY] * n, out_specs=[ANY] * n,
        out_shape=[_sds((N_DEV,) + a.shape, a.dtype) for a in arrs],
        scratch_shapes=[pltpu.SemaphoreType.DMA((7 * n,)), pltpu.SemaphoreType.DMA((7 * n,)), pltpu.SemaphoreType.DMA((n,))],
        compiler_params=pltpu.CompilerParams(has_side_effects=True))(*arrs)


def _swap_sibling(arrs, *, name):
    n = len(arrs)

    def body(*refs):
        srcs, outs = refs[:n], refs[n:2 * n]
        send_sems, recv_sems = refs[2 * n:]
        x, y, c, _ = _place()
        copies = [pltpu.make_async_remote_copy(
            src_ref=srcs[i].at[2 * j + (1 - c)], dst_ref=outs[i].at[j], send_sem=send_sems.at[N_CHIP * i + j],
            recv_sem=recv_sems.at[N_CHIP * i + j], device_id=(x, y, 1 - c), device_id_type=MESH)
            for i in range(n) for j in range(N_CHIP)]
        for cp in copies:
            cp.start()
        for cp in copies:
            cp.wait()

    return pl.pallas_call(
        body, name=name, in_specs=[ANY] * n, out_specs=[ANY] * n,
        out_shape=[_sds((N_CHIP,) + a.shape[1:], a.dtype) for a in arrs],
        scratch_shapes=[pltpu.SemaphoreType.DMA((N_CHIP * n,)), pltpu.SemaphoreType.DMA((N_CHIP * n,))],
        compiler_params=pltpu.CompilerParams(has_side_effects=True))(*arrs)


def _pair_sum(part, got, c_idx, *, name):
    _, R, C = part.shape
    tr = next(t for t in (R, 512, 256, 128, 64, 32, 16) if R % t == 0 and t * C <= 1024 * 1024)

    def body(c_ref, p_ref, g_ref, o_ref):
        o_ref[...] = (p_ref[0].astype(F32) + g_ref[...].astype(F32)).astype(o_ref.dtype)

    return pl.pallas_call(
        body, name=name,
        grid_spec=pltpu.PrefetchScalarGridSpec(
            num_scalar_prefetch=1, grid=(N_CHIP, R // tr),
            in_specs=[pl.BlockSpec((1, 1, tr, C), lambda j, r, c_ref: (j, c_ref[0], r, 0)),
                      pl.BlockSpec((1, tr, C), lambda j, r, c_ref: (j, r, 0))],
            out_specs=pl.BlockSpec((1, tr, C), lambda j, r, c_ref: (j, r, 0))),
        out_shape=_sds((N_CHIP, R, C), part.dtype),
        compiler_params=_params(("parallel", "parallel")))(c_idx, part.reshape(N_CHIP, 2, R, C), got)


def _swap_chips(arrs, *, name):
    n = len(arrs)

    def body(*refs):
        srcs, outs = refs[:n], refs[n:2 * n]
        send_sems, recv_sems, local_sems = refs[2 * n:]
        x, y, c, chips = _place()
        jme = 2 * x + y
        local = [pltpu.make_async_copy(srcs[i].at[jme], outs[i].at[jme], local_sems.at[i]) for i in range(n)]
        for cp in local:
            cp.start()
        copies = [pltpu.make_async_remote_copy(
            src_ref=srcs[i].at[2 * chip[0] + chip[1]], dst_ref=outs[i].at[jme], send_sem=send_sems.at[3 * i + j],
            recv_sem=recv_sems.at[3 * i + j], device_id=(*chip, c), device_id_type=MESH)
            for i in range(n) for j, chip in enumerate(chips)]
        for cp in copies:
            cp.start()
        for i in range(n):
            for j, chip in enumerate(chips):
                pltpu.make_async_remote_copy(
                    src_ref=srcs[i].at[jme], dst_ref=outs[i].at[2 * chip[0] + chip[1]], send_sem=send_sems.at[3 * i + j],
                    recv_sem=recv_sems.at[3 * i + j], device_id=(*chip, c), device_id_type=MESH).wait_recv()
        for cp in copies:
            cp.wait_send()
        for cp in local:
            cp.wait()

    return pl.pallas_call(
        body, name=name, in_specs=[ANY] * n, out_specs=[ANY] * n,
        out_shape=[_sds(a.shape, a.dtype) for a in arrs],
        scratch_shapes=[pltpu.SemaphoreType.DMA((3 * n,)), pltpu.SemaphoreType.DMA((3 * n,)), pltpu.SemaphoreType.DMA((n,))],
        compiler_params=pltpu.CompilerParams(has_side_effects=True))(*arrs)


HBM = pl.BlockSpec(memory_space=pltpu.HBM)
SEM = pl.BlockSpec(memory_space=pltpu.SEMAPHORE)
EFFECT = pltpu.SideEffectType.DATAFLOW_SIDE_EFFECTING


def _near_copies(srcs, lands, send_sems, recv_sems):
    x, y, c, chips = _place()
    me = 4 * x + 2 * y + c
    out = []
    for i in range(len(srcs)):
        for k, (px, py, pc) in enumerate([(x, y, 1 - c)] + [(*chip, c) for chip in chips]):
            out.append(tuple(pltpu.make_async_remote_copy(
                src_ref=srcs[i], dst_ref=lands[i].at[slot], send_sem=send_sems.at[4 * i + k], recv_sem=recv_sems.at[4 * i + k],
                device_id=(px, py, pc), device_id_type=MESH) for slot in (me, 4 * px + 2 * py + pc)))
    return out


def _forward_sibling(lands, *, name):
    n = len(lands)

    def body(*refs):
        bufs = refs[n:2 * n]
        send_sems, recv_sems = refs[2 * n:]
        x, y, c, chips = _place()
        copies = [tuple(pltpu.make_async_remote_copy(
            src_ref=bufs[i].at[4 * chip[0] + 2 * chip[1] + c], dst_ref=bufs[i].at[4 * chip[0] + 2 * chip[1] + cc],
            send_sem=send_sems.at[3 * i + j], recv_sem=recv_sems.at[3 * i + j], device_id=(x, y, 1 - c), device_id_type=MESH)
            for cc in (c, 1 - c)) for i in range(n) for j, chip in enumerate(chips)]
        for send, _ in copies:
            send.start()
        for send, recv in copies:
            send.wait_send()
            recv.wait_recv()

    return pl.pallas_call(
        body, name=name, in_specs=[ANY] * n, out_specs=[ANY] * n, out_shape=[_sds(a.shape, a.dtype) for a in lands],
        input_output_aliases={i: i for i in range(n)},
        scratch_shapes=[pltpu.SemaphoreType.DMA((3 * n,)), pltpu.SemaphoreType.DMA((3 * n,))],
        compiler_params=pltpu.CompilerParams(has_side_effects=True))(*lands)


def _chip_copies(srcs, lands, send_sems, recv_sems):
    x, y, c, chips = _place()
    jme = 2 * x + y
    out = []
    for i in range(len(srcs)):
        for j, chip in enumerate(chips):
            jt = 2 * chip[0] + chip[1]
            out.append(tuple(pltpu.make_async_remote_copy(
                src_ref=srcs[i].at[s], dst_ref=lands[i].at[d], send_sem=send_sems.at[3 * i + j], recv_sem=recv_sems.at[3 * i + j],
                device_id=(*chip, c), device_id_type=MESH) for s, d in ((jt, jme), (jme, jt))))
    return out


def _split_start(srcs, lands, pattern, ncopy, *, after=(), name):
    n = len(srcs)
    na = len(after)

    def body(*refs):
        sems = refs[2 * n + na:]
        for send, _ in pattern(refs[:n], refs[n:2 * n], sems[0], sems[1]):
            send.start()
        refs[-1][...] = jnp.zeros_like(refs[-1])

    arrs = list(srcs) + list(lands)
    return pl.pallas_call(
        body, name=name,
        out_shape=(pltpu.SemaphoreType.DMA((ncopy,)), pltpu.SemaphoreType.DMA((ncopy,)),
                   *[pltpu.HBM(a.shape, a.dtype) for a in arrs], _sds((8, 128), F32)),
        in_specs=[HBM] * (2 * n) + [ANY] * na, out_specs=(SEM, SEM, *[HBM] * (2 * n), pl.BlockSpec(memory_space=pltpu.VMEM)),
        input_output_aliases={i: 2 + i for i in range(2 * n)},
        compiler_params=pltpu.CompilerParams(has_side_effects=EFFECT))(
            *[pltpu.with_memory_space_constraint(a, pltpu.HBM) for a in arrs], *after)


def _split_wait(started, after, pattern, *, name):
    send_sems, recv_sems, *arrs = started[:-1]
    n = len(arrs) // 2

    def body(*refs):
        for send, recv in pattern(refs[:n], refs[n:2 * n], refs[2 * n], refs[2 * n + 1]):
            send.wait_send()
            recv.wait_recv()

    outs = pl.pallas_call(
        body, name=name, out_shape=tuple(pltpu.HBM(a.shape, a.dtype) for a in arrs),
        in_specs=[HBM] * (2 * n) + [SEM, SEM, ANY], out_specs=tuple([HBM] * (2 * n)),
        input_output_aliases={i: i for i in range(2 * n)},
        compiler_params=pltpu.CompilerParams(has_side_effects=EFFECT))(*arrs, send_sems, recv_sems, after)
    return list(outs[n:])


def _adamw(parts, w, m, v, *, layer=None, prev=None, name):
    P, R, C = parts.shape
    tr = next((t for t in (512, 256, 176, 128, 64, 32, 16, 8) if R % t == 0 and t * C <= 256 * 1024), R)
    c1 = 1.0 / (1.0 - ADAM_B1 ** ADAM_STEP)
    c2 = 1.0 / (1.0 - ADAM_B2 ** ADAM_STEP)
    nprev = 0 if prev is None else 4

    def body(p_ref, w_ref, m_ref, v_ref, *rest):
        g_ref, d_ref, nm_ref, nv_ref = rest[nprev:]
        g = p_ref[0].astype(F32)
        for j in range(1, P):
            g = g + p_ref[j].astype(F32)
        g = g.reshape(w_ref.shape)
        nm = ADAM_B1 * m_ref[...] + (1.0 - ADAM_B1) * g
        nv = ADAM_B2 * v_ref[...] + (1.0 - ADAM_B2) * (g * g)
        g_ref[...] = g
        nm_ref[...] = nm
        nv_ref[...] = nv
        d_ref[...] = -ADAM_LR * ((nm * c1) / (jnp.sqrt(nv * c2) + ADAM_EPS) + ADAM_WD * w_ref[...])

    if layer is None:
        row = pl.BlockSpec((tr, C), lambda i: (i, 0))
    else:
        row = pl.BlockSpec((1, tr, C), lambda i: (layer, i, 0))
    out = _sds(w.shape, F32)
    return pl.pallas_call(
        body, name=name, grid=(R // tr,),
        in_specs=[pl.BlockSpec((P, tr, C), lambda i: (0, i, 0)), row, row, row] + [ANY] * nprev,
        out_specs=[row, row, row, row], out_shape=[out, out, out, out],
        input_output_aliases={4 + k: k for k in range(nprev)},
        compiler_params=_params(("parallel",)))(parts, w, m, v, *(prev or ()))


BIG = ("w_in", "w_out", "xa_wq", "xa_wkv", "xa_wo", "ffn_w_up", "ffn_w_down")
COL_SHARDED = ("w_in", "xa_wkv", "ffn_w_up", "ml_conv_w", "ffn_conv_w")
SHARDED_SMALL = ("ml_conv_w", "ffn_conv_w")
REPLICATED = ("rel_bias", "ml_conv_b", "ml_i_bias", "ml_f_bias", "ml_norm_g", "swa_sinks", "ln1_g", "ln1_b", "ln2_g", "ln2_b",
              "ffn_conv_b", "ln3_g", "ln3_b")
NAMES = ("rel_bias", "w_in", "ml_conv_w", "ml_conv_b", "ml_i_bias", "ml_f_bias", "ml_norm_g", "swa_sinks", "w_out", "ln1_g", "ln1_b",
         "xa_wq", "xa_wkv", "xa_wo", "ln2_g", "ln2_b", "ffn_w_up", "ffn_conv_w", "ffn_conv_b", "ffn_w_down", "ln3_g", "ln3_b")


def _flat_rows(a, mult):
    f = a.reshape(-1)
    n = -(-f.shape[0] // (128 * mult)) * (128 * mult)
    if n != f.shape[0]:
        f = jnp.pad(f, (0, n - f.shape[0]))
    return f.reshape(-1, 128)


def _pack(arrs, mult):
    parts = [_flat_rows(a, mult) for a in arrs]
    return jnp.concatenate(parts, axis=0), [p.shape[0] for p in parts]


def _unpack(flat, rows, shapes):
    out, off = [], 0
    lead = flat.shape[:-2]
    for r, shp in zip(rows, shapes):
        n = int(np.prod(shp))
        piece = flat[..., off:off + r, :].reshape(lead + (r * 128,))[..., :n]
        out.append(piece.reshape(lead + tuple(shp)))
        off += r
    return out


def _full_from_shards(stacked, name):
    if name in COL_SHARDED:
        return jnp.moveaxis(stacked, 0, 2).reshape(stacked.shape[1], stacked.shape[2], N_DEV * stacked.shape[3])
    return jnp.moveaxis(stacked, 0, 1).reshape(stacked.shape[1], N_DEV * stacked.shape[2], stacked.shape[3])


def _shards_from_full(full, name):
    L, A, B = full.shape
    if name in COL_SHARDED:
        return jnp.moveaxis(full.reshape(L, A, N_DEV, B // N_DEV), 2, 0)
    return jnp.moveaxis(full.reshape(L, N_DEV, A // N_DEV, B), 1, 0)


def _pad_win(w):
    z = jnp.zeros(w.shape[:-1] + (NP_IN - N_IN,), w.dtype)
    return jnp.concatenate([w[..., :2048], w[..., 2056:], w[..., 2048:2056], z], axis=-1)


def _row128(v):
    return jnp.pad(v, (0, 128 - v.shape[0])).reshape(1, 128)


REST = BIG[1:]


def _layer_full(stacked, n):
    if n in COL_SHARDED:
        full = jnp.moveaxis(stacked, 0, 1).reshape(stacked.shape[1], N_DEV * stacked.shape[2])
    else:
        full = stacked.reshape(N_DEV * stacked.shape[1], stacked.shape[2])
    return _pad_win(full) if n == "w_in" else full


def _layer_shards(g, n):
    A, B = g.shape
    if n in COL_SHARDED:
        return jnp.moveaxis(g.reshape(A, N_DEV, B // N_DEV), 1, 0).astype(BF16)
    return g.reshape(N_DEV, A // N_DEV, B).astype(BF16)


def _with_own_block(a, idx, nblk):
    return lax.dynamic_update_slice(jnp.zeros((nblk,) + a.shape, a.dtype), a[None], (idx,) + (0,) * a.ndim)


def _gather_start(arrs, me, *, after=(), name):
    return _split_start(arrs, [_with_own_block(a, me, N_DEV) for a in arrs], _near_copies, 4 * len(arrs), after=after, name=name)


def _gather_finish(started, after, *, name):
    return _forward_sibling(_split_wait(started, after, _near_copies, name=name + "_wait"), name=name + "_forward")


def _reduce_chips(tag, names, grads, c_idx, jme, *, split):
    send = [_layer_shards(g, n) for n, g in zip(names, grads)]
    got = _swap_sibling(send, name=f"swap_sibling_{tag}")
    sums = [_pair_sum(s, g, c_idx, name=f"pair_sum_{tag}_{n}") for n, s, g in zip(names, send, got)]
    if not split:
        return _swap_chips(sums, name=f"swap_chips_{tag}")
    lands = [_with_own_block(lax.dynamic_index_in_dim(q, jme, 0, keepdims=False), jme, N_CHIP) for q in sums]
    return _split_start(sums, lands, _chip_copies, 3 * len(sums), name=f"swap_chips_{tag}_start")


def _update_small(Gf, loss_part, W, Mo, Vo, me):
    res = {}
    small = REPLICATED + SHARDED_SMALL
    sp_flat, sp_rows = _pack([Gf[n] for n in small] + [loss_part], 8)
    sp_all = _gather([sp_flat], name="gather_small_grads")[0]

    def widen(n, t):
        if n not in SHARDED_SMALL:
            return t[n]
        return lax.dynamic_update_slice(jnp.zeros(Gf[n].shape, F32), t[n], (0, 0, me * t[n].shape[2]))

    zl = jnp.zeros((8, 128), F32)
    wsm, _ = _pack([widen(n, W) for n in small] + [zl], 8)
    msm, _ = _pack([widen(n, Mo) for n in small] + [zl], 8)
    vsm, _ = _pack([widen(n, Vo) for n in small] + [zl], 8)
    outs_small = [_unpack(o_, sp_rows, [Gf[n].shape for n in small] + [(8, 128)])
                  for o_ in _adamw(sp_all, wsm, msm, vsm, name="adamw_small")]
    for kind, os_ in zip(("g", "d", "m", "v"), outs_small):
        for n, a in zip(small, os_[:-1]):
            if n in SHARDED_SMALL:
                a = lax.dynamic_slice(a, (0, 0, me * W[n].shape[2]), W[n].shape)
            res[kind, n] = a
    return res, outs_small[0][-1][0, 0]


def kernel(x, mem, rel_bias, w_in, ml_conv_w, ml_conv_b, ml_i_bias, ml_f_bias, ml_norm_g, swa_sinks, w_out, ln1_g, ln1_b, xa_wq, xa_wkv, xa_wo, ln2_g, ln2_b, ffn_w_up, ffn_conv_w, ffn_conv_b, ffn_w_down, ln3_g, ln3_b, loss_target, m_rel_bias, m_w_in, m_ml_conv_w, m_ml_conv_b, m_ml_i_bias, m_ml_f_bias, m_ml_norm_g, m_swa_sinks, m_w_out, m_ln1_g, m_ln1_b, m_xa_wq, m_xa_wkv, m_xa_wo, m_ln2_g, m_ln2_b, m_ffn_w_up, m_ffn_conv_w, m_ffn_conv_b, m_ffn_w_down, m_ln3_g, m_ln3_b, v_rel_bias, v_w_in, v_ml_conv_w, v_ml_conv_b, v_ml_i_bias, v_ml_f_bias, v_ml_norm_g, v_swa_sinks, v_w_out, v_ln1_g, v_ln1_b, v_xa_wq, v_xa_wkv, v_xa_wo, v_ln2_g, v_ln2_b, v_ffn_w_up, v_ffn_conv_w, v_ffn_conv_b, v_ffn_w_down, v_ln3_g, v_ln3_b):
    W = dict(rel_bias=rel_bias, w_in=w_in, ml_conv_w=ml_conv_w, ml_conv_b=ml_conv_b, ml_i_bias=ml_i_bias, ml_f_bias=ml_f_bias,
             ml_norm_g=ml_norm_g, swa_sinks=swa_sinks, w_out=w_out, ln1_g=ln1_g, ln1_b=ln1_b, xa_wq=xa_wq, xa_wkv=xa_wkv,
             xa_wo=xa_wo, ln2_g=ln2_g, ln2_b=ln2_b, ffn_w_up=ffn_w_up, ffn_conv_w=ffn_conv_w, ffn_conv_b=ffn_conv_b,
             ffn_w_down=ffn_w_down, ln3_g=ln3_g, ln3_b=ln3_b)
    Mo = dict(rel_bias=m_rel_bias, w_in=m_w_in, ml_conv_w=m_ml_conv_w, ml_conv_b=m_ml_conv_b, ml_i_bias=m_ml_i_bias,
              ml_f_bias=m_ml_f_bias, ml_norm_g=m_ml_norm_g, swa_sinks=m_swa_sinks, w_out=m_w_out, ln1_g=m_ln1_g, ln1_b=m_ln1_b,
              xa_wq=m_xa_wq, xa_wkv=m_xa_wkv, xa_wo=m_xa_wo, ln2_g=m_ln2_g, ln2_b=m_ln2_b, ffn_w_up=m_ffn_w_up,
              ffn_conv_w=m_ffn_conv_w, ffn_conv_b=m_ffn_conv_b, ffn_w_down=m_ffn_w_down, ln3_g=m_ln3_g, ln3_b=m_ln3_b)
    Vo = dict(rel_bias=v_rel_bias, w_in=v_w_in, ml_conv_w=v_ml_conv_w, ml_conv_b=v_ml_conv_b, ml_i_bias=v_ml_i_bias,
              ml_f_bias=v_ml_f_bias, ml_norm_g=v_ml_norm_g, swa_sinks=v_swa_sinks, w_out=v_w_out, ln1_g=v_ln1_g, ln1_b=v_ln1_b,
              xa_wq=v_xa_wq, xa_wkv=v_xa_wkv, xa_wo=v_xa_wo, ln2_g=v_ln2_g, ln2_b=v_ln2_b, ffn_w_up=v_ffn_w_up,
              ffn_conv_w=v_ffn_conv_w, ffn_conv_b=v_ffn_conv_b, ffn_w_down=v_ffn_w_down, ln3_g=v_ln3_g, ln3_b=v_ln3_b)
    S = x.shape[1]
    c_me = lax.axis_index("c")
    me = 4 * lax.axis_index("x") + 2 * lax.axis_index("y") + c_me
    c_idx = jnp.reshape(c_me, (1,)).astype(jnp.int32)
    xs = x.reshape(S, D)
    mems = mem.reshape(mem.shape[1], D)
    tgt = loss_target.reshape(S, D)

    jme = 2 * lax.axis_index("x") + lax.axis_index("y")

    sm_flat, sm_rows = _pack([W[n] for n in SHARDED_SMALL], 8)
    first = _gather([w_in[0].astype(BF16), sm_flat], name="gather_first")
    rest0 = _gather_start([W[n][0].astype(BF16) for n in REST], me, name="gather_rest0_start")
    full = [{"w_in": _layer_full(first[0], "w_in")}, None]
    conv_w = {n: _full_from_shards(s, n) for n, s in zip(SHARDED_SMALL, _unpack(first[1], sm_rows, [W[n].shape for n in SHARDED_SMALL]))}

    bucket = jnp.asarray(_t5_buckets())
    bias = _bias_table(rel_bias, bucket, name="bias_table")

    saved = []
    h0, h0b = xs, xs.astype(BF16)
    for l in range(DEPTH):
        gbias = _row128(jnp.concatenate([ml_i_bias[l], ml_f_bias[l]]))
        sinks = _row128(swa_sinks[l])
        ng = ml_norm_g[l].reshape(1, ML_W)
        proj = _mm(h0b, full[l]["w_in"], dep=(rest0[-1],) if l == 0 else (), name=f"proj{l}")
        qk = _silu_conv_fwd(proj, conv_w["ml_conv_w"][l], ml_conv_b[l].reshape(1, -1), name=f"mlconv{l}")
        h_ml, cs, ns, ms = _mlstm_fwd(qk, proj, gbias, ng, name=f"mlstm{l}")
        h_sw = _swa_fwd(proj, bias, sinks, name=f"swa{l}")
        dep = ()
        if l == 0:
            landed = _gather_finish(rest0, h_sw, name="gather_rest0")
            full[0].update({n: _layer_full(s, n) for n, s in zip(REST, landed)})
            layer1 = _gather_start([W[n][1].astype(BF16) for n in BIG], me, after=(landed[0],), name="gather_layer1_start")
            dep = (layer1[-1],)
        fw = full[l]
        h1, h1b, z1 = _mm_res_ln([h_ml, h_sw], fw["w_out"], h0, ln1_g[l].reshape(1, D), ln1_b[l].reshape(1, D), name=f"mix_out{l}")
        q = _mm(h1b, fw["xa_wq"], dep=dep, name=f"xa_q{l}")
        kv = _mm(mems, fw["xa_wkv"], tm=256, name=f"xa_kv{l}")
        o = _xattn_fwd(q, kv, name=f"xattn{l}")
        h2, h2b, z2 = _mm_res_ln([o], fw["xa_wo"], h1, ln2_g[l].reshape(1, D), ln2_b[l].reshape(1, D), name=f"xa_out{l}")
        act = _ffn_gate_fwd(h2b, fw["ffn_w_up"], conv_w["ffn_conv_w"][l], ffn_conv_b[l].reshape(1, -1), name=f"ffn_gate{l}")
        h3, h3b, z3 = _mm_res_ln([act], fw["ffn_w_down"], h2, ln3_g[l].reshape(1, D), ln3_b[l].reshape(1, D), name=f"ffn_out{l}")
        saved.append(dict(h0b=h0b, proj=proj, qk=qk, cs=cs, ns=ns, ms=ms, h_ml=h_ml, h_sw=h_sw, z1=z1, h1b=h1b, q=q, kv=kv, o=o,
                          z2=z2, h2b=h2b, act=act, z3=z3, gbias=gbias, sinks=sinks, ng=ng))
        h0, h0b = h3, h3b
        if l == 0:
            landed = _gather_finish(layer1, h3b, name="gather_layer1")
            full[1] = {n: _layer_full(s, n) for n, s in zip(BIG, landed)}

    G = {n: [None] * DEPTH for n in NAMES if n != "rel_bias"}
    dbias = [None] * DEPTH
    pending = []
    dz3, dz3b, G["ln3_g"][DEPTH - 1], G["ln3_b"][DEPTH - 1], loss_part = _grad_in(
        [], None, ln=(saved[-1]["z3"], ln3_g[DEPTH - 1].reshape(1, D)), loss=(h0, tgt), name="loss_head")
    for l in reversed(range(DEPTH)):
        sv, fw = saved[l], full[l]
        win = fw["w_in"]
        G["ffn_w_down"][l] = _mm_tn(sv["act"], dz3b, name=f"d_w_down{l}")
        dupg, dupv, G["ffn_conv_w"][l], G["ffn_conv_b"][l] = _ffn_gate_bwd(
            dz3b, fw["ffn_w_down"], sv["h2b"], fw["ffn_w_up"], conv_w["ffn_conv_w"][l], ffn_conv_b[l].reshape(1, -1),
            name=f"ffn_gate_bwd{l}")
        G["ffn_w_up"][l] = jnp.concatenate([_mm_tn(sv["h2b"], dupg, name=f"d_w_up_g{l}"),
                                            _mm_tn(sv["h2b"], dupv, name=f"d_w_up_v{l}")], axis=1)
        dep = ()
        if l == 0:
            names = ("ffn_w_up", "ffn_w_down")
            pending.append((names, 0, _reduce_chips("ffn0", names, [G[n][0] for n in names], c_idx, jme, split=True)))
            dep = (pending[-1][2][-1],)
        dz2, dz2b, G["ln2_g"][l], G["ln2_b"][l] = _grad_in(
            [(dupg, fw["ffn_w_up"], 0), (dupv, fw["ffn_w_up"], 1)], dz3, ln=(sv["z2"], ln2_g[l].reshape(1, D)), dep=dep,
            name=f"d_h2_{l}")
        G["xa_wo"][l] = _mm_tn(sv["o"], dz2b, name=f"d_xa_wo{l}")
        do = _mm(dz2b, fw["xa_wo"], trans_b=True, name=f"d_xa_o{l}")
        dq, dkv = _xattn_bwd(do, sv["q"], sv["kv"], name=f"xattn_bwd{l}")
        G["xa_wkv"][l] = _mm_tn(mems, dkv, name=f"d_xa_wkv{l}")
        G["xa_wq"][l] = _mm_tn(sv["h1b"], dq, name=f"d_xa_wq{l}")
        dz1, dz1b, G["ln1_g"][l], G["ln1_b"][l] = _grad_in(
            [(dq, fw["xa_wq"], 0)], dz2, ln=(sv["z1"], ln1_g[l].reshape(1, D)), name=f"d_h1_{l}")
        G["w_out"][l] = jnp.concatenate([_mm_tn(sv["h_ml"], dz1b, name=f"d_w_out_ml{l}"),
                                         _mm_tn(sv["h_sw"], dz1b, name=f"d_w_out_sw{l}")], axis=0)
        dhcat = _mm(dz1b, fw["w_out"], trans_b=True, name=f"d_hcat{l}")
        dsw, dbias[l], dsinks = _swa_bwd(dhcat, sv["proj"], bias, sv["sinks"], name=f"swa_bwd{l}")
        dqk, dml, dgb, dng = _mlstm_bwd(dhcat, sv["qk"], sv["proj"], sv["gbias"], sv["ng"], sv["cs"], sv["ns"], sv["ms"],
                                        name=f"mlstm_bwd{l}")
        dqk_pre, G["ml_conv_w"][l], G["ml_conv_b"][l] = _silu_conv_bwd(
            dqk, sv["proj"], conv_w["ml_conv_w"][l], ml_conv_b[l].reshape(1, -1), name=f"mlconv_bwd{l}")
        dw_qk = _mm_tn(sv["h0b"], dqk_pre, name=f"d_w_in_qk{l}")
        dw_ml = _mm_tn(sv["h0b"], dml, name=f"d_w_in_ml{l}")
        dw_sw = _mm_tn(sv["h0b"], dsw, name=f"d_w_in_sw{l}")
        G["w_in"][l] = jnp.concatenate([dw_qk, dw_ml[:, :2 * ML_W + 2 * ML_H], dw_sw], axis=1)
        win_ml = jnp.concatenate([win[:, 1024:2048], win[:, 2816:2944]], axis=1)
        pairs = [(dqk_pre, win, 0), (dml, win_ml, 0), (dsw, win[:, 2048:2816], 0)]
        G["ml_i_bias"][l] = dgb[0, :ML_H]
        G["ml_f_bias"][l] = dgb[0, ML_H:2 * ML_H]
        G["ml_norm_g"][l] = dng
        G["swa_sinks"][l] = dsinks[0, :SW_H]
        if l > 0:
            pending.append((BIG, l, _reduce_chips(f"l{l}", BIG, [G[n][l] for n in BIG], c_idx, jme, split=True)))
            dz3, dz3b, G["ln3_g"][l - 1], G["ln3_b"][l - 1] = _grad_in(
                pairs, dz1, ln=(saved[l - 1]["z3"], ln3_g[l - 1].reshape(1, D)), dep=(pending[-1][2][-1],), name=f"d_h0_{l}")
        else:
            grad_x = _grad_in(pairs, dz1, name="d_h0_0").reshape(x.shape)

    names = ("w_in", "w_out", "xa_wq", "xa_wkv", "xa_wo")
    parts = {(n, 0): p for n, p in zip(names, _reduce_chips("rest0", names, [G[n][0] for n in names], c_idx, jme, split=False))}
    for names, l, started in pending:
        landed = _split_wait(started, parts["w_in", 0], _chip_copies, name=f"swap_chips_{'l1' if l else 'ffn0'}_wait")
        parts.update({(n, l): p for n, p in zip(names, landed)})
    res = {}
    for n in BIG:
        outs = None
        for l in reversed(range(DEPTH)):
            outs = _adamw(parts[n, l], W[n], Mo[n], Vo[n], layer=l, prev=outs, name=f"adamw_{n}{l}")
        for kind, a in zip(("g", "d", "m", "v"), outs):
            res[kind, n] = a

    Gf = {n: jnp.stack([g.reshape(W[n].shape[1:]) if n in REPLICATED else g for g in G[n]]) for n in G if n not in BIG}
    Gf["rel_bias"] = _bias_table_bwd(dbias, bucket, name="bias_table_bwd")[:, :SW_H]
    res_small, loss = _update_small(Gf, loss_part, W, Mo, Vo, me)
    res.update(res_small)
    return (loss, grad_x, *[res["g", n] for n in NAMES], *[res["d", n] for n in NAMES], *[res["m", n] for n in NAMES],
            *[res["v", n] for n in NAMES])
```

```python
import functools
import math

import jax
import jax.numpy as jnp
import numpy as np
from jax import lax
from jax.experimental import pallas as pl
from jax.experimental.pallas import tpu as pltpu

F32 = jnp.float32
BF16 = jnp.bfloat16

N_DEV = 8
N_CHIP = 4
D = 1024
DEPTH = 2
ML_H = 4
ML_W = 512
ML_DH = 128
ML_L = 64
ML_CONV = 4
SW_DH = 64
SW_W = 512
SW_H = 8
SW_G = 4
SW_KVW = 128
BLK = 128
REL_B = 32
REL_MAXD = 128
XA_H = 4
XA_DH = 256
DFF = 2816
NB_FF = DFF // 128
FFN_CONV = 3
ALPHA = (2.0 * DEPTH) ** 0.25
EPS = 1e-5
N_IN = 2824
NP_IN = 3072
ML_GW = 2 * ML_W + 128
SW_GW = SW_W + 2 * SW_KVW
ADAM_LR = 0.001
ADAM_B1 = 0.9
ADAM_B2 = 0.999
ADAM_EPS = 1e-08
ADAM_WD = 0.01
ADAM_STEP = 10
VMEM_LIMIT = 56 * 1024 * 1024
MESH = pl.DeviceIdType.MESH

NN = ((1,), (0,))
NT = ((1,), (1,))
TN = ((0,), (0,))


def _dg(a, b, dn):
    if a.ndim == 3:
        dims = (((dn[0][0] + 1,), (dn[1][0] + 1,)), ((0,), (0,)))
    else:
        dims = (dn, ((), ()))
    return lax.dot_general(a.astype(BF16), b.astype(BF16), dims, preferred_element_type=F32)


@jax.custom_vjp
def dot_nn(a, b):
    return _dg(a, b, NN)


dot_nn.defvjp(lambda a, b: (_dg(a, b, NN), (a, b)), lambda r, g: (_dg(g, r[1], NT), _dg(r[0], g, TN)))


@jax.custom_vjp
def dot_nt(a, b):
    return _dg(a, b, NT)


dot_nt.defvjp(lambda a, b: (_dg(a, b, NT), (a, b)), lambda r, g: (_dg(g, r[1], NN), _dg(g, r[0], TN)))


@jax.custom_vjp
def dot_tn(a, b):
    return _dg(a, b, TN)


dot_tn.defvjp(lambda a, b: (_dg(a, b, TN), (a, b)), lambda r, g: (_dg(r[1], g, NT), _dg(r[0], g, NN)))


def _params(sem=None):
    return pltpu.CompilerParams(dimension_semantics=sem, vmem_limit_bytes=VMEM_LIMIT)


def _sds(shape, dtype):
    return jax.ShapeDtypeStruct(tuple(shape), dtype)


TOKEN = pl.BlockSpec((8, 128), lambda *_: (0, 0))


def _mm(a, b, *, trans_b=False, out_dtype=F32, add=None, add_scale=1.0, tm=1024, tn=512, dep=(), name):
    a_list = list(a) if isinstance(a, (list, tuple)) else [a]
    M = a_list[0].shape[0]
    N = b.shape[0] if trans_b else b.shape[1]
    tm = min(tm, M)
    tn = next(t for t in (tn, 384, 256, 128) if N % t == 0)
    assert M % tm == 0
    Ka = a_list[0].shape[1]
    assert all(t.shape[1] == Ka for t in a_list)
    tk = next(t for t in (Ka, 1408, 1024) if Ka % t == 0 and t <= 1408)
    na, npa = len(a_list), Ka // tk
    nk = na * npa
    has_add = add is not None

    def body(*refs):
        a_refs, b_ref = refs[:na], refs[na]
        add_ref = refs[na + 1] if has_add else None
        o_ref, acc_ref = refs[-2], refs[-1]
        k = pl.program_id(2)

        def finish(r):
            if has_add:
                r = r + add_scale * add_ref[...].astype(F32)
            o_ref[...] = r.astype(out_dtype)

        for t, a_ref in enumerate(a_refs):
            def step(a_ref=a_ref):
                p = _dg(a_ref[...], b_ref[...], NT if trans_b else NN)
                if nk == 1:
                    finish(p)
                    return

                @pl.when(k == 0)
                def _():
                    acc_ref[...] = p

                @pl.when((k > 0) & (k < nk - 1))
                def _():
                    acc_ref[...] += p

                @pl.when(k == nk - 1)
                def _():
                    finish(acc_ref[...] + p)

            if na == 1:
                step()
            else:
                pl.when((k >= t * npa) & (k < (t + 1) * npa))(step)

    in_specs = [pl.BlockSpec((tm, tk), lambda i, j, k, t=t: (i, jnp.clip(k - t * npa, 0, npa - 1))) for t in range(na)]
    in_specs.append(pl.BlockSpec((tn, tk), lambda i, j, k: (j, k)) if trans_b else pl.BlockSpec((tk, tn), lambda i, j, k: (k, j)))
    args = a_list + [b]
    if has_add:
        in_specs.append(pl.BlockSpec((tm, tn), lambda i, j, k: (i, j)))
        args.append(add)
    in_specs += [TOKEN] * len(dep)
    args += list(dep)
    return pl.pallas_call(
        body, name=name, grid=(M // tm, N // tn, nk), in_specs=in_specs,
        out_specs=pl.BlockSpec((tm, tn), lambda i, j, k: (i, j)), out_shape=_sds((M, N), out_dtype),
        scratch_shapes=[pltpu.VMEM((tm, tn) if nk > 1 else (8, 128), F32)],
        compiler_params=_params(("parallel", "parallel", "arbitrary")))(*args)


def _mm_tn(a, g, *, name):
    S, K = a.shape
    N = g.shape[1]
    tk = K if K <= 1024 else K // 2
    tn = next(t for t in range(1536, 0, -128) if N % t == 0)
    ts = min(512, S)
    ns = S // ts
    assert K % tk == 0 and S % ts == 0

    def body(a_ref, g_ref, o_ref):
        s = pl.program_id(2)
        p = _dg(a_ref[...], g_ref[...], TN)

        @pl.when(s == 0)
        def _():
            o_ref[...] = p

        @pl.when(s > 0)
        def _():
            o_ref[...] += p

    return pl.pallas_call(
        body, name=name, grid=(K // tk, N // tn, ns),
        in_specs=[pl.BlockSpec((ts, tk), lambda i, j, s: (s, i)), pl.BlockSpec((ts, tn), lambda i, j, s: (s, j))],
        out_specs=pl.BlockSpec((tk, tn), lambda i, j, s: (i, j)), out_shape=_sds((K, N), F32),
        compiler_params=_params(("parallel", "parallel", "arbitrary")))(a, g)


def _mm_res_ln(a_list, w, resid, gam, bet, *, name):
    M, Ka = a_list[0].shape
    na = len(a_list)
    assert w.shape[0] == na * Ka
    tm = min(256, M)

    def body(*refs):
        a_refs, w_refs = refs[:na], refs[na:2 * na]
        r_ref, g_ref, b_ref, y_ref, yb_ref, z_ref = refs[2 * na:]
        z = ALPHA * r_ref[...]
        for a_ref, w_ref in zip(a_refs, w_refs):
            z = z + _dg(a_ref[...], w_ref[...], NN)
        mu = jnp.mean(z, axis=1, keepdims=True)
        zc = z - mu
        var = jnp.mean(zc * zc, axis=1, keepdims=True)
        y = zc * lax.rsqrt(var + EPS) * g_ref[...] + b_ref[...]
        y_ref[...] = y
        yb_ref[...] = y.astype(BF16)
        z_ref[...] = z

    row = pl.BlockSpec((tm, D), lambda i: (i, 0))
    vec = pl.BlockSpec((1, D), lambda i: (0, 0))
    a_specs = [pl.BlockSpec((tm, Ka), lambda i: (i, 0)) for _ in a_list]
    w_specs = [pl.BlockSpec((Ka, D), lambda i, t=t: (t, 0)) for t in range(na)]
    return pl.pallas_call(
        body, name=name, grid=(M // tm,), in_specs=a_specs + w_specs + [row, vec, vec],
        out_specs=[row, row, row], out_shape=[_sds((M, D), F32), _sds((M, D), BF16), _sds((M, D), F32)],
        compiler_params=_params(("parallel",)))(*a_list, *([w] * na), resid, gam, bet)


def _grad_in(pairs, add, *, ln=None, loss=None, dep=(), name):
    M = (add if add is not None else loss[0]).shape[0]
    tm = min(256, M)
    npair, nd = len(pairs), len(dep)
    has_ln, has_loss = ln is not None, loss is not None

    def body(*refs):
        n_in = 2 * npair + (2 if has_loss else 1) + 2 * has_ln + nd
        ins, outs = refs[:n_in], refs[n_in:]
        i = pl.program_id(0)
        pos = 2 * npair
        if has_loss:
            e = ins[pos][...] - ins[pos + 1][...]
            pos += 2
            dy = e * (1.0 / D)
            part = 0.5 * jnp.sum(jnp.sum(e * e, axis=1, keepdims=True) * (1.0 / D), axis=0, keepdims=True)
        else:
            dy = ALPHA * ins[pos][...]
            pos += 1
            for t in range(npair):
                dy = dy + _dg(ins[2 * t][...], ins[2 * t + 1][...], NT)
        if not has_ln:
            outs[0][...] = dy
            return
        z, g_ref = ins[pos][...], ins[pos + 1]
        mu = jnp.mean(z, axis=1, keepdims=True)
        zc = z - mu
        var = jnp.mean(zc * zc, axis=1, keepdims=True)
        rstd = lax.rsqrt(var + EPS)
        xh = zc * rstd
        dxh = dy * g_ref[...]
        m1 = jnp.mean(dxh, axis=1, keepdims=True)
        m2 = jnp.mean(dxh * xh, axis=1, keepdims=True)
        dz = rstd * (dxh - m1 - xh * m2)
        outs[0][...] = dz
        outs[1][...] = dz.astype(BF16)
        acc = [(outs[2], jnp.sum(dy * xh, axis=0, keepdims=True)), (outs[3], jnp.sum(dy, axis=0, keepdims=True))]
        if has_loss:
            acc.append((outs[4], jnp.broadcast_to(part, (8, 128))))

        @pl.when(i == 0)
        def _():
            for ref, val in acc:
                ref[...] = val

        @pl.when(i > 0)
        def _():
            for ref, val in acc:
                ref[...] += val

    row = pl.BlockSpec((tm, D), lambda i: (i, 0))
    vec = pl.BlockSpec((1, D), lambda i: (0, 0))
    in_specs, args = [], []
    for a, b, blk in pairs:
        in_specs += [pl.BlockSpec((tm, a.shape[1]), lambda i: (i, 0)), pl.BlockSpec((D, a.shape[1]), lambda i, blk=blk: (0, blk))]
        args += [a, b]
    if has_loss:
        in_specs += [row, row]
        args += list(loss)
    else:
        in_specs.append(row)
        args.append(add)
    if has_ln:
        in_specs += [row, vec]
        args += list(ln)
    in_specs += [TOKEN] * nd
    args += list(dep)
    if has_ln:
        out_specs = [row, row, vec, vec] + ([pl.BlockSpec((8, 128), lambda i: (0, 0))] if has_loss else [])
        out_shape = [_sds((M, D), F32), _sds((M, D), BF16), _sds((1, D), F32), _sds((1, D), F32)] + ([_sds((8, 128), F32)] if has_loss else [])
    else:
        out_specs, out_shape = row, _sds((M, D), F32)
    return pl.pallas_call(
        body, name=name, grid=(M // tm,), in_specs=in_specs, out_specs=out_specs, out_shape=out_shape,
        compiler_params=_params(("arbitrary",) if has_ln else ("parallel",)))(*args)


def _shift_down(x, d):
    if d == 0:
        return x
    rows = lax.broadcasted_iota(jnp.int32, x.shape, 0)
    return jnp.where(rows >= d, pltpu.roll(x, d, 0), 0.0)


def _shift_up(x, d):
    if d == 0:
        return x
    S = x.shape[0]
    rows = lax.broadcasted_iota(jnp.int32, x.shape, 0)
    return jnp.where(rows < S - d, pltpu.roll(x, S - d, 0), 0.0)


def _conv(x, w_ref, b_ref, cs, K):
    y = b_ref[:, cs]
    for j in range(K):
        y = y + _shift_down(x, K - 1 - j) * w_ref[j:j + 1, cs]
    return y


def _conv_bwd(dy, x, w_ref, dw_ref, db_ref, cs, K):
    dx = jnp.zeros_like(x)
    for j in range(K):
        sdy = _shift_up(dy, K - 1 - j)
        dx = dx + sdy * w_ref[j:j + 1, cs]
        dw_ref[j:j + 1, cs] = jnp.sum(sdy * x, axis=0, keepdims=True)
    db_ref[:, cs] = jnp.sum(dy, axis=0, keepdims=True)
    return dx


ALL = slice(None)


def _silu_conv_fwd(proj, cw, cb, *, name):
    S = proj.shape[0]

    def body(x_ref, w_ref, b_ref, o_ref):
        o_ref[...] = jax.nn.silu(_conv(x_ref[...], w_ref, b_ref, ALL, ML_CONV))

    col = pl.BlockSpec((S, 128), lambda j: (0, j))
    return pl.pallas_call(
        body, name=name, grid=(8,),
        in_specs=[col, pl.BlockSpec((ML_CONV, 128), lambda j: (0, j)), pl.BlockSpec((1, 128), lambda j: (0, j))],
        out_specs=col, out_shape=_sds((S, 2 * ML_W), F32), compiler_params=_params(("parallel",)))(proj, cw, cb)


def _silu_conv_bwd(dqk, proj, cw, cb, *, name):
    S = proj.shape[0]

    def body(d_ref, x_ref, w_ref, b_ref, dx_ref, dw_ref, db_ref):
        x = x_ref[...]
        y = _conv(x, w_ref, b_ref, ALL, ML_CONV)
        dy = jax.vjp(jax.nn.silu, y)[1](d_ref[...])[0]
        dx_ref[...] = _conv_bwd(dy, x, w_ref, dw_ref, db_ref, ALL, ML_CONV).astype(BF16)

    col = pl.BlockSpec((S, 128), lambda j: (0, j))
    wsp = pl.BlockSpec((ML_CONV, 128), lambda j: (0, j))
    bsp = pl.BlockSpec((1, 128), lambda j: (0, j))
    return pl.pallas_call(
        body, name=name, grid=(8,), in_specs=[col, col, wsp, bsp], out_specs=[col, wsp, bsp],
        out_shape=[_sds((S, 2 * ML_W), BF16), _sds((ML_CONV, 2 * ML_W), F32), _sds((1, 2 * ML_W), F32)],
        compiler_params=_params(("parallel",)))(dqk, proj, cw, cb)


GELU_C0 = math.sqrt(2.0 / math.pi)
GELU_C1 = 0.044715


def _gate_bwd(ug, uv, da):
    t = jnp.tanh(GELU_C0 * (ug + GELU_C1 * (ug * ug * ug)))
    half = 0.5 * (1.0 + t)
    dgelu = half + 0.5 * ug * (1.0 - t * t) * (GELU_C0 * (1.0 + 3.0 * GELU_C1 * (ug * ug)))
    return da * uv * dgelu, da * (ug * half)


def _up_pair(h_ref, ugw_ref, uvw_ref):
    return _dg(h_ref[...], jnp.concatenate([ugw_ref[...], uvw_ref[...]], axis=1), NN)


def _ffn_specs(S):
    return (pl.BlockSpec((D, 128), lambda j: (0, j)), pl.BlockSpec((D, 128), lambda j: (0, j + NB_FF)),
            pl.BlockSpec((FFN_CONV, 128), lambda j: (0, j)), pl.BlockSpec((FFN_CONV, 128), lambda j: (0, j + NB_FF)),
            pl.BlockSpec((1, 128), lambda j: (0, j)), pl.BlockSpec((1, 128), lambda j: (0, j + NB_FF)))


def _ffn_gate_fwd(hb, w_up, cw, cb, *, name):
    S = hb.shape[0]
    ug_, uv_, wg, wv, bg, bv = _ffn_specs(S)

    def body(h_ref, ugw_ref, uvw_ref, wg_ref, wv_ref, bg_ref, bv_ref, o_ref, h_s):
        @pl.when(pl.program_id(0) == 0)
        def _():
            pltpu.sync_copy(h_ref, h_s)

        x2 = _up_pair(h_s, ugw_ref, uvw_ref)
        ug = _conv(x2[:, :128], wg_ref, bg_ref, ALL, FFN_CONV)
        uv = _conv(x2[:, 128:], wv_ref, bv_ref, ALL, FFN_CONV)
        o_ref[...] = (jax.nn.gelu(ug) * uv).astype(BF16)

    return pl.pallas_call(
        body, name=name, grid=(NB_FF,), in_specs=[ANY, ug_, uv_, wg, wv, bg, bv],
        out_specs=pl.BlockSpec((S, 128), lambda j: (0, j)), out_shape=_sds((S, DFF), BF16),
        scratch_shapes=[pltpu.VMEM((S, D), BF16)],
        compiler_params=_params(("arbitrary",)))(hb, w_up, w_up, cw, cw, cb, cb)


def _ffn_gate_bwd(dzb, w_down, hb, w_up, cw, cb, *, name):
    S = hb.shape[0]
    ug_, uv_, wg, wv, bg, bv = _ffn_specs(S)

    def body(dz_ref, h_ref, wd_ref, ugw_ref, uvw_ref, wg_ref, wv_ref, bg_ref, bv_ref,
             dxg_ref, dxv_ref, dwg_ref, dwv_ref, dbg_ref, dbv_ref, dz_s, h_s):
        @pl.when(pl.program_id(0) == 0)
        def _():
            pltpu.sync_copy(dz_ref, dz_s)
            pltpu.sync_copy(h_ref, h_s)

        x2 = _up_pair(h_s, ugw_ref, uvw_ref)
        xg, xv = x2[:, :128], x2[:, 128:]
        ug = _conv(xg, wg_ref, bg_ref, ALL, FFN_CONV)
        uv = _conv(xv, wv_ref, bv_ref, ALL, FFN_CONV)
        dug, duv = _gate_bwd(ug, uv, _dg(dz_s[...], wd_ref[...], NT))
        dxg_ref[...] = _conv_bwd(dug, xg, wg_ref, dwg_ref, dbg_ref, ALL, FFN_CONV).astype(BF16)
        dxv_ref[...] = _conv_bwd(duv, xv, wv_ref, dwv_ref, dbv_ref, ALL, FFN_CONV).astype(BF16)

    col = pl.BlockSpec((S, 128), lambda j: (0, j))
    half = _sds((S, DFF), BF16)
    dxg, dxv, dwg, dwv, dbg, dbv = pl.pallas_call(
        body, name=name, grid=(NB_FF,),
        in_specs=[ANY, ANY, pl.BlockSpec((128, D), lambda j: (j, 0)), ug_, uv_, wg, wv, bg, bv],
        out_specs=[col, col, wg, wg, bg, bg],
        out_shape=[half, half, _sds((FFN_CONV, DFF), F32), _sds((FFN_CONV, DFF), F32), _sds((1, DFF), F32), _sds((1, DFF), F32)],
        scratch_shapes=[pltpu.VMEM((S, D), BF16), pltpu.VMEM((S, D), BF16)],
        compiler_params=_params(("arbitrary",)))(dzb, hb, w_down, w_up, w_up, cw, cw, cb, cb)
    return dxg, dxv, jnp.concatenate([dwg, dwv], axis=1), jnp.concatenate([dbg, dbv], axis=1)


def _log_sigmoid(x):
    return jnp.minimum(x, 0.0) - jnp.log1p(jnp.exp(-jnp.abs(x)))


@jax.custom_vjp
def _clamp_div(num, den, floor, shift):
    return num / jnp.maximum(jnp.abs(den), floor)


def _clamp_div_fwd(num, den, floor, shift):
    out = num / jnp.maximum(jnp.abs(den), floor)
    return out, (den, floor, out)


def _clamp_div_bwd(res, g):
    den, floor, out = res
    active = jnp.abs(den) < floor
    dinv = jnp.maximum(jnp.abs(den), floor)
    go = jnp.sum(g * out, axis=-1, keepdims=True)
    ddiv = -go / dinv
    return (g / dinv, jnp.where(active, 0.0, ddiv * jnp.sign(den)), jnp.where(active, ddiv, 0.0),
            jnp.sum(jnp.where(active, go, 0.0), axis=-2, keepdims=True))


_clamp_div.defvjp(_clamp_div_fwd, _clamp_div_bwd)


def _ml_heads(q, k, v, o_pre, gates, gbias, C, n, ng, m, shift):
    H, L, _ = q.shape
    lane1 = lax.broadcasted_iota(jnp.int32, (1, 128), 1)
    gz = gates + jnp.where(lane1 < ML_H, lax.stop_gradient(gbias), gbias)
    gz = jnp.broadcast_to(gz[None], (H, L, 128))
    hid = lax.broadcasted_iota(jnp.int32, (H, L, 128), 0)
    lane = lax.broadcasted_iota(jnp.int32, (H, L, 128), 2)
    ig = jnp.sum(jnp.where(lane == hid, gz, 0.0), axis=2, keepdims=True)
    lf = _log_sigmoid(jnp.sum(jnp.where(lane == ML_H + hid, gz, 0.0), axis=2, keepdims=True))
    r = lax.broadcasted_iota(jnp.int32, (H, L, L), 1)
    c = lax.broadcasted_iota(jnp.int32, (H, L, L), 2)
    eye, tril = r == c, r >= c

    def to_row(col):
        return jnp.sum(jnp.where(eye, col, 0.0), axis=1, keepdims=True)

    b_col = jnp.sum(jnp.where(tril, to_row(lf), 0.0), axis=2, keepdims=True)
    Dm = jnp.where(tril, b_col - to_row(b_col) + to_row(ig), -jnp.inf)
    inter = b_col + m
    m_t = jnp.maximum(inter, jnp.max(Dm, axis=2, keepdims=True))
    w_inter = jnp.exp(inter - m_t)
    ks = k * (ML_DH ** -0.5)
    s = dot_nt(q, ks) * jnp.exp(Dm - m_t)
    num = w_inter * dot_nn(q, C) + dot_nn(s, v)
    den = w_inter * jnp.sum(q * n, axis=2, keepdims=True) + jnp.sum(s, axis=2, keepdims=True)
    h = _clamp_div(num, den, jnp.exp(-m_t), shift)
    g = jnp.sum(lf, axis=1, keepdims=True)
    a = g - b_col + ig
    m_new = jnp.maximum(g + m, jnp.max(a, axis=1, keepdims=True))
    decay = jnp.exp(g + m - m_new)
    wk = jnp.exp(a - m_new)
    C_new = decay * C + dot_tn(ks * wk, v)
    n_new = decay * n + jnp.sum(wk * ks, axis=1, keepdims=True)
    mu = jnp.mean(h, axis=2, keepdims=True)
    hc = h - mu
    var = jnp.mean(hc * hc, axis=2, keepdims=True)
    out = jax.nn.sigmoid(o_pre) * (hc * lax.rsqrt(var + EPS) * ng)
    return out, C_new, n_new, m_new


def _hs(h, off=0):
    return slice(off + h * ML_DH, off + (h + 1) * ML_DH)


def _heads(ref, off=0):
    return jnp.stack([ref[:, _hs(h, off)] for h in range(ML_H)])


def _mlstm_fwd(qk, proj, gbias, ng, *, name):
    S = qk.shape[0]
    nc = S // ML_L

    def body(q_ref, k_ref, v_ref, o_ref, g_ref, gb_ref, ng_ref, h_ref, cs_ref, ns_ref, ms_ref, c_s, n_s, m_s):
        @pl.when(pl.program_id(0) == 0)
        def _():
            c_s[...] = jnp.zeros_like(c_s)
            n_s[...] = jnp.zeros_like(n_s)
            m_s[...] = jnp.zeros_like(m_s)

        C, n = c_s[...], n_s[...]
        cs_ref[0] = C
        ns_ref[0] = n
        ms_ref[0] = m_s[...]
        out, C2, n2, m2 = _ml_heads(_heads(q_ref), _heads(k_ref), _heads(v_ref), _heads(o_ref), g_ref[...], gb_ref[...], C, n,
                                    _heads(ng_ref), m_s[:, :, 0:1], jnp.zeros((ML_H, 1, 1), F32))
        for h in range(ML_H):
            h_ref[:, _hs(h)] = out[h].astype(BF16)
        c_s[...] = C2
        n_s[...] = n2
        m_s[...] = jnp.broadcast_to(m2, (ML_H, 1, 128))

    def w(j):
        return pl.BlockSpec((ML_L, ML_W), lambda c, j=j: (c, j))

    return pl.pallas_call(
        body, name=name, grid=(nc,),
        in_specs=[w(0), w(1), w(2), w(3), pl.BlockSpec((ML_L, 128), lambda c: (c, 22)),
                  pl.BlockSpec((1, 128), lambda c: (0, 0)), pl.BlockSpec((1, ML_W), lambda c: (0, 0))],
        out_specs=[w(0), pl.BlockSpec((1, ML_H, ML_DH, ML_DH), lambda c: (c, 0, 0, 0)),
                   pl.BlockSpec((1, ML_H, 1, 128), lambda c: (c, 0, 0, 0)), pl.BlockSpec((1, ML_H, 1, 128), lambda c: (c, 0, 0, 0))],
        out_shape=[_sds((S, ML_W), BF16), _sds((nc, ML_H, ML_DH, ML_DH), F32), _sds((nc, ML_H, 1, 128), F32),
                   _sds((nc, ML_H, 1, 128), F32)],
        scratch_shapes=[pltpu.VMEM((ML_H, ML_DH, ML_DH), F32), pltpu.VMEM((ML_H, 1, 128), F32), pltpu.VMEM((ML_H, 1, 128), F32)],
        compiler_params=_params(("arbitrary",)))(qk, qk, proj, proj, proj, gbias, ng)


def _mlstm_bwd(dh, qk, proj, gbias, ng, cs, ns, ms, *, name):
    S = qk.shape[0]
    nc = S // ML_L

    def body(dh_ref, q_ref, k_ref, v_ref, o_ref, g_ref, gb_ref, ng_ref, cs_ref, ns_ref, ms_ref,
             dqk_ref, dml_ref, dgb_ref, dng_ref, dc_s, dn_s, dm_s):
        @pl.when(pl.program_id(0) == 0)
        def _():
            dc_s[...] = jnp.zeros_like(dc_s)
            dn_s[...] = jnp.zeros_like(dn_s)
            dm_s[...] = jnp.zeros_like(dm_s)
            dgb_ref[...] = jnp.zeros_like(dgb_ref)
            dng_ref[...] = jnp.zeros_like(dng_ref)

        _, vjp = jax.vjp(_ml_heads, _heads(q_ref), _heads(k_ref), _heads(v_ref), _heads(o_ref), g_ref[...], gb_ref[...],
                         cs_ref[0], ns_ref[0], _heads(ng_ref), ms_ref[0][:, :, 0:1], jnp.zeros((ML_H, 1, 1), F32))
        dq, dk, dv, do, dgates, dgb, dC, dn, dng, dm, dshift = vjp((_heads(dh_ref), dc_s[...], dn_s[...], dm_s[:, :, 0:1]))
        lane1 = lax.broadcasted_iota(jnp.int32, (1, 128), 1)
        for h in range(ML_H):
            dqk_ref[:, _hs(h)] = dq[h]
            dqk_ref[:, _hs(h, ML_W)] = dk[h]
            dml_ref[:, _hs(h)] = dv[h].astype(BF16)
            dml_ref[:, _hs(h, ML_W)] = do[h].astype(BF16)
            dng_ref[:, _hs(h)] += dng[h]
            dgb = jnp.where(lane1 == h, dshift[h], dgb)
        dc_s[...] = dC
        dn_s[...] = dn
        dm_s[...] = jnp.broadcast_to(dm, (ML_H, 1, 128))
        dml_ref[:, 2 * ML_W:] = dgates.astype(BF16)
        dgb_ref[...] += dgb

    def w(j):
        return pl.BlockSpec((ML_L, ML_W), lambda c, j=j: (nc - 1 - c, j))

    vec = pl.BlockSpec((1, 128), lambda c: (0, 0))
    vecw = pl.BlockSpec((1, ML_W), lambda c: (0, 0))
    gsp = pl.BlockSpec((ML_L, 128), lambda c: (nc - 1 - c, 22))
    st = pl.BlockSpec((1, ML_H, 1, 128), lambda c: (nc - 1 - c, 0, 0, 0))
    return pl.pallas_call(
        body, name=name, grid=(nc,),
        in_specs=[w(0), w(0), w(1), w(2), w(3), gsp, vec, vecw,
                  pl.BlockSpec((1, ML_H, ML_DH, ML_DH), lambda c: (nc - 1 - c, 0, 0, 0)), st, st],
        out_specs=[pl.BlockSpec((ML_L, 2 * ML_W), lambda c: (nc - 1 - c, 0)), pl.BlockSpec((ML_L, ML_GW), lambda c: (nc - 1 - c, 0)),
                   vec, vecw],
        out_shape=[_sds((S, 2 * ML_W), F32), _sds((S, ML_GW), BF16), _sds((1, 128), F32), _sds((1, ML_W), F32)],
        scratch_shapes=[pltpu.VMEM((ML_H, ML_DH, ML_DH), F32), pltpu.VMEM((ML_H, 1, 128), F32), pltpu.VMEM((ML_H, 1, 128), F32)],
        compiler_params=_params(("arbitrary",)))(dh, qk, qk, proj, proj, proj, gbias, ng, cs, ns, ms)


def _t5_buckets():
    r = np.arange(BLK)[:, None]
    c = np.arange(2 * BLK)[None, :]
    n = np.maximum(BLK + r - c, 0)
    max_exact = REL_B // 2
    nf = np.maximum(n, 1).astype(np.float32)
    large = max_exact + (np.log(nf / np.float32(max_exact)) / np.float32(math.log(REL_MAXD / max_exact))
                         * np.float32(REL_B - max_exact)).astype(np.int32)
    large = np.minimum(large, REL_B - 1)
    return np.where(n < max_exact, n, large).astype(np.int32)


def _bias_table(rel_bias, bucket, *, name):
    def body(rb_ref, bk_ref, o_ref):
        bk = bk_ref[...]
        for h in range(SW_H):
            acc = jnp.zeros((BLK, 2 * BLK), F32)
            for b in range(REL_B):
                acc = jnp.where(bk == b, rb_ref[b, h], acc)
            o_ref[h] = acc

    return pl.pallas_call(
        body, name=name, in_specs=[pl.BlockSpec(memory_space=pltpu.SMEM), pl.BlockSpec(memory_space=pltpu.VMEM)],
        out_specs=pl.BlockSpec(memory_space=pltpu.VMEM), out_shape=_sds((SW_H, BLK, 2 * BLK), F32),
        compiler_params=_params())(rel_bias, bucket)


def _bias_table_bwd(dbias_list, bucket, *, name):
    nl = len(dbias_list)

    def body(*refs):
        d_refs, bk_ref, o_ref = refs[:nl], refs[nl], refs[nl + 1]
        bk = bk_ref[...]
        rows = lax.broadcasted_iota(jnp.int32, (REL_B, 128), 0)
        lanes = lax.broadcasted_iota(jnp.int32, (REL_B, 128), 1)
        acc = jnp.zeros((REL_B, 128), F32)
        for h in range(SW_H):
            d = d_refs[0][h]
            for d_ref in d_refs[1:]:
                d = d + d_ref[h]
            for b in range(REL_B):
                t = jnp.sum(jnp.sum(jnp.where(bk == b, d, 0.0), axis=0, keepdims=True), axis=1, keepdims=True)
                acc = jnp.where((rows == b) & (lanes == h), t, acc)
        o_ref[...] = acc

    vm = pl.BlockSpec(memory_space=pltpu.VMEM)
    return pl.pallas_call(
        body, name=name, in_specs=[vm] * (nl + 1), out_specs=vm, out_shape=_sds((REL_B, 128), F32),
        compiler_params=_params())(*dbias_list, bucket)


def _swa_heads(q, kp, kc, vp, vc, bp, bc, sinks, has_prev):
    def rep(t):
        return jnp.concatenate([t[g:g + 1] for g in range(SW_H // SW_G) for _ in range(SW_G)], axis=0)

    r = lax.broadcasted_iota(jnp.int32, (SW_H, BLK, BLK), 1)
    c = lax.broadcasted_iota(jnp.int32, (SW_H, BLK, BLK), 2)
    hid = lax.broadcasted_iota(jnp.int32, (SW_H, 1, 128), 0)
    lane = lax.broadcasted_iota(jnp.int32, (SW_H, 1, 128), 2)
    sink = jnp.sum(jnp.where(lane == hid, jnp.broadcast_to(sinks[None], (SW_H, 1, 128)), 0.0), axis=2, keepdims=True)
    lp = jnp.where((c > r) & has_prev, dot_nt(q, rep(kp)) * (SW_DH ** -0.5) + bp, -jnp.inf)
    lc = jnp.where(c <= r, dot_nt(q, rep(kc)) * (SW_DH ** -0.5) + bc, -jnp.inf)
    mx = jnp.maximum(jnp.maximum(jnp.max(lp, axis=2, keepdims=True), jnp.max(lc, axis=2, keepdims=True)), sink)
    mx = lax.stop_gradient(mx)
    pp, pc = jnp.exp(lp - mx), jnp.exp(lc - mx)
    den = jnp.sum(pp, axis=2, keepdims=True) + jnp.sum(pc, axis=2, keepdims=True) + jnp.exp(sink - mx)
    return dot_nn(pp / den, rep(vp)) + dot_nn(pc / den, rep(vc))


def _qs(h, off=0):
    return slice(off + h * SW_DH, off + (h + 1) * SW_DH)


def _split(ref, n):
    return jnp.stack([ref[:, _qs(h)] for h in range(n)])


def _swa_fwd(proj, bias, sinks, *, name):
    S = proj.shape[0]
    nb = S // BLK
    nkv = SW_H // SW_G

    def body(q_ref, kp_ref, kc_ref, vp_ref, vc_ref, b_ref, s_ref, o_ref):
        out = _swa_heads(_split(q_ref, SW_H), _split(kp_ref, nkv), _split(kc_ref, nkv), _split(vp_ref, nkv), _split(vc_ref, nkv),
                         b_ref[:, :, :BLK], b_ref[:, :, BLK:], s_ref[...], pl.program_id(0) > 0)
        for h in range(SW_H):
            o_ref[:, _qs(h)] = out[h].astype(BF16)

    def cur(j):
        return pl.BlockSpec((BLK, 128), lambda n, j=j: (n, j))

    def prev(j):
        return pl.BlockSpec((BLK, 128), lambda n, j=j: (jnp.maximum(n - 1, 0), j))

    return pl.pallas_call(
        body, name=name, grid=(nb,),
        in_specs=[pl.BlockSpec((BLK, SW_W), lambda n: (n, 4)), prev(20), cur(20), prev(21), cur(21),
                  pl.BlockSpec((SW_H, BLK, 2 * BLK), lambda n: (0, 0, 0)), pl.BlockSpec((1, 128), lambda n: (0, 0))],
        out_specs=pl.BlockSpec((BLK, SW_W), lambda n: (n, 0)), out_shape=_sds((S, SW_W), BF16),
        compiler_params=_params(("parallel",)))(proj, proj, proj, proj, proj, bias, sinks)


def _swa_bwd(dh, proj, bias, sinks, *, name):
    S = proj.shape[0]
    nb = S // BLK

    nkv = SW_H // SW_G

    def body(dh_ref, q_ref, kp_ref, kc_ref, vp_ref, vc_ref, b_ref, s_ref, dsw_ref, db_ref, ds_ref, ck_s, cv_s):
        i = pl.program_id(0)

        @pl.when(i == 0)
        def _():
            ck_s[...] = jnp.zeros_like(ck_s)
            cv_s[...] = jnp.zeros_like(cv_s)
            db_ref[...] = jnp.zeros_like(db_ref)
            ds_ref[...] = jnp.zeros_like(ds_ref)

        f = functools.partial(_swa_heads, has_prev=i < nb - 1)
        _, vjp = jax.vjp(f, _split(q_ref, SW_H), _split(kp_ref, nkv), _split(kc_ref, nkv), _split(vp_ref, nkv),
                         _split(vc_ref, nkv), b_ref[:, :, :BLK], b_ref[:, :, BLK:], s_ref[...])
        dq, dkp, dkc, dvp, dvc, dbp, dbc, ds = vjp(_split(dh_ref, SW_H))
        for h in range(SW_H):
            dsw_ref[:, _qs(h)] = dq[h].astype(BF16)
        for g in range(nkv):
            dsw_ref[:, _qs(g, SW_W)] = (dkc[g] + ck_s[:, _qs(g)]).astype(BF16)
            dsw_ref[:, _qs(g, SW_W + SW_KVW)] = (dvc[g] + cv_s[:, _qs(g)]).astype(BF16)
            ck_s[:, _qs(g)] = dkp[g]
            cv_s[:, _qs(g)] = dvp[g]
        db_ref[:, :, :BLK] += dbp
        db_ref[:, :, BLK:] += dbc
        ds_ref[...] += ds

    def cur(j):
        return pl.BlockSpec((BLK, 128), lambda i, j=j: (nb - 1 - i, j))

    def prev(j):
        return pl.BlockSpec((BLK, 128), lambda i, j=j: (jnp.maximum(nb - 2 - i, 0), j))

    bsp = pl.BlockSpec((SW_H, BLK, 2 * BLK), lambda i: (0, 0, 0))
    vec = pl.BlockSpec((1, 128), lambda i: (0, 0))
    return pl.pallas_call(
        body, name=name, grid=(nb,),
        in_specs=[pl.BlockSpec((BLK, SW_W), lambda i: (nb - 1 - i, 1)), pl.BlockSpec((BLK, SW_W), lambda i: (nb - 1 - i, 4)),
                  prev(20), cur(20), prev(21), cur(21), bsp, vec],
        out_specs=[pl.BlockSpec((BLK, SW_GW), lambda i: (nb - 1 - i, 0)), bsp, vec],
        out_shape=[_sds((S, SW_GW), BF16), _sds((SW_H, BLK, 2 * BLK), F32), _sds((1, 128), F32)],
        scratch_shapes=[pltpu.VMEM((BLK, 128), F32)] * 2,
        compiler_params=_params(("arbitrary",)))(dh, proj, proj, proj, proj, proj, bias, sinks)


XA_TM = 512


def _xa_head(qh, kh, vh):
    logits = dot_nt(qh, kh) * (XA_DH ** -0.5)
    mx = lax.stop_gradient(jnp.max(logits, axis=1, keepdims=True))
    e = jnp.exp(logits - mx)
    return dot_nn(e / jnp.sum(e, axis=1, keepdims=True), vh)


def _xs(h, off=0):
    return slice(off + h * XA_DH, off + (h + 1) * XA_DH)


def _xattn_fwd(q, kv, *, name):
    S = q.shape[0]
    M = kv.shape[0]

    def body(q_ref, kv_ref, o_ref):
        for h in range(XA_H):
            o_ref[:, _xs(h)] = _xa_head(q_ref[:, _xs(h)], kv_ref[:, _xs(h)], kv_ref[:, _xs(h, D)]).astype(BF16)

    tm = min(XA_TM, S)
    row = pl.BlockSpec((tm, D), lambda i: (i, 0))
    return pl.pallas_call(
        body, name=name, grid=(S // tm,), in_specs=[row, pl.BlockSpec((M, 2 * D), lambda i: (0, 0))], out_specs=row,
        out_shape=_sds((S, D), BF16), compiler_params=_params(("parallel",)))(q, kv)


def _xattn_bwd(do, q, kv, *, name):
    S = q.shape[0]
    M = kv.shape[0]

    def body(do_ref, q_ref, kv_ref, dq_ref, dkv_ref):
        @pl.when(pl.program_id(0) == 0)
        def _():
            dkv_ref[...] = jnp.zeros_like(dkv_ref)

        for h in range(XA_H):
            _, vjp = jax.vjp(_xa_head, q_ref[:, _xs(h)], kv_ref[:, _xs(h)], kv_ref[:, _xs(h, D)])
            dq, dk, dv = vjp(do_ref[:, _xs(h)])
            dq_ref[:, _xs(h)] = dq.astype(BF16)
            dkv_ref[:, _xs(h)] += dk
            dkv_ref[:, _xs(h, D)] += dv

    tm = min(XA_TM, S)
    row = pl.BlockSpec((tm, D), lambda i: (i, 0))
    full = pl.BlockSpec((M, 2 * D), lambda i: (0, 0))
    return pl.pallas_call(
        body, name=name, grid=(S // tm,), in_specs=[row, row, full], out_specs=[row, full],
        out_shape=[_sds((S, D), BF16), _sds((M, 2 * D), F32)], compiler_params=_params(("arbitrary",)))(do, q, kv)


ANY = pl.BlockSpec(memory_space=pl.ANY)


def _place():
    x, y, c = lax.axis_index("x"), lax.axis_index("y"), lax.axis_index("c")
    chips = [(1 - x, y), (x, 1 - y), (1 - x, 1 - y)]
    return x, y, c, chips


def _gather(arrs, *, name):
    n = len(arrs)

    def body(*refs):
        srcs, outs = refs[:n], refs[n:2 * n]
        send_sems, recv_sems, local_sems = refs[2 * n:]
        x, y, c, chips = _place()
        me, sib = (x, y, c), (x, y, 1 - c)

        def idx(p):
            return 4 * p[0] + 2 * p[1] + p[2]

        def copy(i, k, block, to, from_src=False):
            return pltpu.make_async_remote_copy(
                src_ref=srcs[i] if from_src else outs[i].at[idx(block)], dst_ref=outs[i].at[idx(block)],
                send_sem=send_sems.at[7 * i + k], recv_sem=recv_sems.at[7 * i + k], device_id=to, device_id_type=MESH)

        local = [pltpu.make_async_copy(srcs[i], outs[i].at[idx(me)], local_sems.at[i]) for i in range(n)]
        for cp in local:
            cp.start()
        first = []
        for i in range(n):
            first.append(copy(i, 0, me, sib, True))
            first += [copy(i, 1 + j, me, (*chip, c), True) for j, chip in enumerate(chips)]
        for cp in first:
            cp.start()
        passed = []
        for j, chip in enumerate(chips):
            for i in range(n):
                copy(i, 1 + j, (*chip, c), me).wait_recv()
                cp = copy(i, 4 + j, (*chip, c), sib)
                cp.start()
                passed.append(cp)
        for i in range(n):
            copy(i, 0, sib, me).wait_recv()
        for j, chip in enumerate(chips):
            for i in range(n):
                copy(i, 4 + j, (*chip, 1 - c), me).wait_recv()
        for cp in first + passed:
            cp.wait_send()
        for cp in local:
            cp.wait()

    return pl.pallas_call(
        body, name=name, in_specs=[ANY] * n, out_specs=[ANY] * n,
        out_shape=[_sds((N_DEV,) + a.shape, a.dtype) for a in arrs],
        scratch_shapes=[pltpu.SemaphoreType.DMA((7 * n,)), pltpu.SemaphoreType.DMA((7 * n,)), pltpu.SemaphoreType.DMA((n,))],
        compiler_params=pltpu.CompilerParams(has_side_effects=True))(*arrs)


def _swap_sibling(arrs, *, name):
    n = len(arrs)

    def body(*refs):
        srcs, outs = refs[:n], refs[n:2 * n]
        send_sems, recv_sems = refs[2 * n:]
        x, y, c, _ = _place()
        copies = [pltpu.make_async_remote_copy(
            src_ref=srcs[i].at[2 * j + (1 - c)], dst_ref=outs[i].at[j], send_sem=send_sems.at[N_CHIP * i + j],
            recv_sem=recv_sems.at[N_CHIP * i + j], device_id=(x, y, 1 - c), device_id_type=MESH)
            for i in range(n) for j in range(N_CHIP)]
        for cp in copies:
            cp.start()
        for cp in copies:
            cp.wait()

    return pl.pallas_call(
        body, name=name, in_specs=[ANY] * n, out_specs=[ANY] * n,
        out_shape=[_sds((N_CHIP,) + a.shape[1:], a.dtype) for a in arrs],
        scratch_shapes=[pltpu.SemaphoreType.DMA((N_CHIP * n,)), pltpu.SemaphoreType.DMA((N_CHIP * n,))],
        compiler_params=pltpu.CompilerParams(has_side_effects=True))(*arrs)


def _pair_sum(part, got, c_idx, *, name):
    _, R, C = part.shape
    tr = next(t for t in (R, 512, 256, 128, 64, 32, 16) if R % t == 0 and t * C <= 1024 * 1024)

    def body(c_ref, p_ref, g_ref, o_ref):
        o_ref[...] = (p_ref[0].astype(F32) + g_ref[...].astype(F32)).astype(o_ref.dtype)

    return pl.pallas_call(
        body, name=name,
        grid_spec=pltpu.PrefetchScalarGridSpec(
            num_scalar_prefetch=1, grid=(N_CHIP, R // tr),
            in_specs=[pl.BlockSpec((1, 1, tr, C), lambda j, r, c_ref: (j, c_ref[0], r, 0)),
                      pl.BlockSpec((1, tr, C), lambda j, r, c_ref: (j, r, 0))],
            out_specs=pl.BlockSpec((1, tr, C), lambda j, r, c_ref: (j, r, 0))),
        out_shape=_sds((N_CHIP, R, C), part.dtype),
        compiler_params=_params(("parallel", "parallel")))(c_idx, part.reshape(N_CHIP, 2, R, C), got)


def _swap_chips(arrs, *, name):
    n = len(arrs)

    def body(*refs):
        srcs, outs = refs[:n], refs[n:2 * n]
        send_sems, recv_sems, local_sems = refs[2 * n:]
        x, y, c, chips = _place()
        jme = 2 * x + y
        local = [pltpu.make_async_copy(srcs[i].at[jme], outs[i].at[jme], local_sems.at[i]) for i in range(n)]
        for cp in local:
            cp.start()
        copies = [pltpu.make_async_remote_copy(
            src_ref=srcs[i].at[2 * chip[0] + chip[1]], dst_ref=outs[i].at[jme], send_sem=send_sems.at[3 * i + j],
            recv_sem=recv_sems.at[3 * i + j], device_id=(*chip, c), device_id_type=MESH)
            for i in range(n) for j, chip in enumerate(chips)]
        for cp in copies:
            cp.start()
        for i in range(n):
            for j, chip in enumerate(chips):
                pltpu.make_async_remote_copy(
                    src_ref=srcs[i].at[jme], dst_ref=outs[i].at[2 * chip[0] + chip[1]], send_sem=send_sems.at[3 * i + j],
                    recv_sem=recv_sems.at[3 * i + j], device_id=(*chip, c), device_id_type=MESH).wait_recv()
        for cp in copies:
            cp.wait_send()
        for cp in local:
            cp.wait()

    return pl.pallas_call(
        body, name=name, in_specs=[ANY] * n, out_specs=[ANY] * n,
        out_shape=[_sds(a.shape, a.dtype) for a in arrs],
        scratch_shapes=[pltpu.SemaphoreType.DMA((3 * n,)), pltpu.SemaphoreType.DMA((3 * n,)), pltpu.SemaphoreType.DMA((n,))],
        compiler_params=pltpu.CompilerParams(has_side_effects=True))(*arrs)


HBM = pl.BlockSpec(memory_space=pltpu.HBM)
SEM = pl.BlockSpec(memory_space=pltpu.SEMAPHORE)
EFFECT = pltpu.SideEffectType.DATAFLOW_SIDE_EFFECTING


def _near_copies(srcs, lands, send_sems, recv_sems):
    x, y, c, chips = _place()
    me = 4 * x + 2 * y + c
    out = []
    for i in range(len(srcs)):
        for k, (px, py, pc) in enumerate([(x, y, 1 - c)] + [(*chip, c) for chip in chips]):
            out.append(tuple(pltpu.make_async_remote_copy(
                src_ref=srcs[i], dst_ref=lands[i].at[slot], send_sem=send_sems.at[4 * i + k], recv_sem=recv_sems.at[4 * i + k],
                device_id=(px, py, pc), device_id_type=MESH) for slot in (me, 4 * px + 2 * py + pc)))
    return out


def _forward_sibling(lands, *, name):
    n = len(lands)

    def body(*refs):
        bufs = refs[n:2 * n]
        send_sems, recv_sems = refs[2 * n:]
        x, y, c, chips = _place()
        copies = [tuple(pltpu.make_async_remote_copy(
            src_ref=bufs[i].at[4 * chip[0] + 2 * chip[1] + c], dst_ref=bufs[i].at[4 * chip[0] + 2 * chip[1] + cc],
            send_sem=send_sems.at[3 * i + j], recv_sem=recv_sems.at[3 * i + j], device_id=(x, y, 1 - c), device_id_type=MESH)
            for cc in (c, 1 - c)) for i in range(n) for j, chip in enumerate(chips)]
        for send, _ in copies:
            send.start()
        for send, recv in copies:
            send.wait_send()
            recv.wait_recv()

    return pl.pallas_call(
        body, name=name, in_specs=[ANY] * n, out_specs=[ANY] * n, out_shape=[_sds(a.shape, a.dtype) for a in lands],
        input_output_aliases={i: i for i in range(n)},
        scratch_shapes=[pltpu.SemaphoreType.DMA((3 * n,)), pltpu.SemaphoreType.DMA((3 * n,))],
        compiler_params=pltpu.CompilerParams(has_side_effects=True))(*lands)


def _scatter_copies(srcs, lands, send_sems, recv_sems):
    x, y, c, _ = _place()
    me = 4 * x + 2 * y + c
    out = []
    for i in range(len(srcs)):
        for j in range(1, N_DEV):
            px, py, pc = x ^ ((j >> 2) & 1), y ^ ((j >> 1) & 1), c ^ (j & 1)
            p = 4 * px + 2 * py + pc
            k = (N_DEV - 1) * i + j - 1
            out.append(tuple(pltpu.make_async_remote_copy(
                src_ref=srcs[i].at[s], dst_ref=lands[i].at[d], send_sem=send_sems.at[k], recv_sem=recv_sems.at[k],
                device_id=(px, py, pc), device_id_type=MESH) for s, d in ((p, me), (me, p))))
    return out


def _split_start(srcs, lands, pattern, ncopy, *, after=(), name):
    n = len(srcs)
    na = len(after)

    def body(*refs):
        sems = refs[2 * n + na:]
        for send, _ in pattern(refs[:n], refs[n:2 * n], sems[0], sems[1]):
            send.start()
        refs[-1][...] = jnp.zeros_like(refs[-1])

    arrs = list(srcs) + list(lands)
    return pl.pallas_call(
        body, name=name,
        out_shape=(pltpu.SemaphoreType.DMA((ncopy,)), pltpu.SemaphoreType.DMA((ncopy,)),
                   *[pltpu.HBM(a.shape, a.dtype) for a in arrs], _sds((8, 128), F32)),
        in_specs=[HBM] * (2 * n) + [ANY] * na, out_specs=(SEM, SEM, *[HBM] * (2 * n), pl.BlockSpec(memory_space=pltpu.VMEM)),
        input_output_aliases={i: 2 + i for i in range(2 * n)},
        compiler_params=pltpu.CompilerParams(has_side_effects=EFFECT))(
            *[pltpu.with_memory_space_constraint(a, pltpu.HBM) for a in arrs], *after)


def _split_wait(started, after, pattern, *, name):
    send_sems, recv_sems, *arrs = started[:-1]
    n = len(arrs) // 2

    def body(*refs):
        for send, recv in pattern(refs[:n], refs[n:2 * n], refs[2 * n], refs[2 * n + 1]):
            send.wait_send()
            recv.wait_recv()

    outs = pl.pallas_call(
        body, name=name, out_shape=tuple(pltpu.HBM(a.shape, a.dtype) for a in arrs),
        in_specs=[HBM] * (2 * n) + [SEM, SEM, ANY], out_specs=tuple([HBM] * (2 * n)),
        input_output_aliases={i: i for i in range(2 * n)},
        compiler_params=pltpu.CompilerParams(has_side_effects=EFFECT))(*arrs, send_sems, recv_sems, after)
    return list(outs[n:])


def _adamw(parts, w, m, v, *, layer=None, prev=None, name):
    P, R, C = parts.shape
    tr = next((t for t in (512, 256, 176, 128, 64, 32, 16, 8) if R % t == 0 and t * C <= 256 * 1024), R)
    c1 = 1.0 / (1.0 - ADAM_B1 ** ADAM_STEP)
    c2 = 1.0 / (1.0 - ADAM_B2 ** ADAM_STEP)
    nprev = 0 if prev is None else 4

    def body(p_ref, w_ref, m_ref, v_ref, *rest):
        g_ref, d_ref, nm_ref, nv_ref = rest[nprev:]
        g = p_ref[0].astype(F32)
        for j in range(1, P):
            g = g + p_ref[j].astype(F32)
        g = g.reshape(w_ref.shape)
        nm = ADAM_B1 * m_ref[...] + (1.0 - ADAM_B1) * g
        nv = ADAM_B2 * v_ref[...] + (1.0 - ADAM_B2) * (g * g)
        g_ref[...] = g
        nm_ref[...] = nm
        nv_ref[...] = nv
        d_ref[...] = -ADAM_LR * ((nm * c1) / (jnp.sqrt(nv * c2) + ADAM_EPS) + ADAM_WD * w_ref[...])

    if layer is None:
        row = pl.BlockSpec((tr, C), lambda i: (i, 0))
    else:
        row = pl.BlockSpec((1, tr, C), lambda i: (layer, i, 0))
    out = _sds(w.shape, F32)
    return pl.pallas_call(
        body, name=name, grid=(R // tr,),
        in_specs=[pl.BlockSpec((P, tr, C), lambda i: (0, i, 0)), row, row, row] + [ANY] * nprev,
        out_specs=[row, row, row, row], out_shape=[out, out, out, out],
        input_output_aliases={4 + k: k for k in range(nprev)},
        compiler_params=_params(("parallel",)))(parts, w, m, v, *(prev or ()))


BIG = ("w_in", "w_out", "xa_wq", "xa_wkv", "xa_wo", "ffn_w_up", "ffn_w_down")
COL_SHARDED = ("w_in", "xa_wkv", "ffn_w_up", "ml_conv_w", "ffn_conv_w")
SHARDED_SMALL = ("ml_conv_w", "ffn_conv_w")
REPLICATED = ("rel_bias", "ml_conv_b", "ml_i_bias", "ml_f_bias", "ml_norm_g", "swa_sinks", "ln1_g", "ln1_b", "ln2_g", "ln2_b",
              "ffn_conv_b", "ln3_g", "ln3_b")
NAMES = ("rel_bias", "w_in", "ml_conv_w", "ml_conv_b", "ml_i_bias", "ml_f_bias", "ml_norm_g", "swa_sinks", "w_out", "ln1_g", "ln1_b",
         "xa_wq", "xa_wkv", "xa_wo", "ln2_g", "ln2_b", "ffn_w_up", "ffn_conv_w", "ffn_conv_b", "ffn_w_down", "ln3_g", "ln3_b")


def _flat_rows(a, mult):
    f = a.reshape(-1)
    n = -(-f.shape[0] // (128 * mult)) * (128 * mult)
    if n != f.shape[0]:
        f = jnp.pad(f, (0, n - f.shape[0]))
    return f.reshape(-1, 128)


def _pack(arrs, mult):
    parts = [_flat_rows(a, mult) for a in arrs]
    return jnp.concatenate(parts, axis=0), [p.shape[0] for p in parts]


def _unpack(flat, rows, shapes):
    out, off = [], 0
    lead = flat.shape[:-2]
    for r, shp in zip(rows, shapes):
        n = int(np.prod(shp))
        piece = flat[..., off:off + r, :].reshape(lead + (r * 128,))[..., :n]
        out.append(piece.reshape(lead + tuple(shp)))
        off += r
    return out


def _full_from_shards(stacked, name):
    if name in COL_SHARDED:
        return jnp.moveaxis(stacked, 0, 2).reshape(stacked.shape[1], stacked.shape[2], N_DEV * stacked.shape[3])
    return jnp.moveaxis(stacked, 0, 1).reshape(stacked.shape[1], N_DEV * stacked.shape[2], stacked.shape[3])


def _shards_from_full(full, name):
    L, A, B = full.shape
    if name in COL_SHARDED:
        return jnp.moveaxis(full.reshape(L, A, N_DEV, B // N_DEV), 2, 0)
    return jnp.moveaxis(full.reshape(L, N_DEV, A // N_DEV, B), 1, 0)


def _pad_win(w):
    z = jnp.zeros(w.shape[:-1] + (NP_IN - N_IN,), w.dtype)
    return jnp.concatenate([w[..., :2048], w[..., 2056:], w[..., 2048:2056], z], axis=-1)


def _row128(v):
    return jnp.pad(v, (0, 128 - v.shape[0])).reshape(1, 128)


REST = BIG[1:]


def _layer_full(stacked, n):
    if n in COL_SHARDED:
        full = jnp.moveaxis(stacked, 0, 1).reshape(stacked.shape[1], N_DEV * stacked.shape[2])
    else:
        full = stacked.reshape(N_DEV * stacked.shape[1], stacked.shape[2])
    return _pad_win(full) if n == "w_in" else full


def _layer_shards(g, n):
    A, B = g.shape
    if n in COL_SHARDED:
        return jnp.moveaxis(g.reshape(A, N_DEV, B // N_DEV), 1, 0).astype(BF16)
    return g.reshape(N_DEV, A // N_DEV, B).astype(BF16)


def _with_own_block(a, idx, nblk):
    return lax.dynamic_update_slice(jnp.zeros((nblk,) + a.shape, a.dtype), a[None], (idx,) + (0,) * a.ndim)


def _gather_start(arrs, me, *, after=(), name):
    return _split_start(arrs, [_with_own_block(a, me, N_DEV) for a in arrs], _near_copies, 4 * len(arrs), after=after, name=name)


def _gather_finish(started, after, *, name):
    return _forward_sibling(_split_wait(started, after, _near_copies, name=name + "_wait"), name=name + "_forward")


def _reduce_now(tag, names, grads, c_idx):
    send = [_layer_shards(g, n) for n, g in zip(names, grads)]
    got = _swap_sibling(send, name=f"swap_sibling_{tag}")
    sums = [_pair_sum(s, g, c_idx, name=f"pair_sum_{tag}_{n}") for n, s, g in zip(names, send, got)]
    return _swap_chips(sums, name=f"swap_chips_{tag}")


def _reduce_start(tag, names, grads, me):
    send = [_layer_shards(g, n) for n, g in zip(names, grads)]
    lands = [_with_own_block(lax.dynamic_index_in_dim(s, me, 0, keepdims=False), me, N_DEV) for s in send]
    return _split_start(send, lands, _scatter_copies, (N_DEV - 1) * len(send), name=f"reduce_{tag}_start")


def _update_small(Gf, loss_part, W, Mo, Vo, me):
    res = {}
    small = REPLICATED + SHARDED_SMALL
    sp_flat, sp_rows = _pack([Gf[n] for n in small] + [loss_part], 8)
    sp_all = _gather([sp_flat], name="gather_small_grads")[0]

    def widen(n, t):
        if n not in SHARDED_SMALL:
            return t[n]
        return lax.dynamic_update_slice(jnp.zeros(Gf[n].shape, F32), t[n], (0, 0, me * t[n].shape[2]))

    zl = jnp.zeros((8, 128), F32)
    wsm, _ = _pack([widen(n, W) for n in small] + [zl], 8)
    msm, _ = _pack([widen(n, Mo) for n in small] + [zl], 8)
    vsm, _ = _pack([widen(n, Vo) for n in small] + [zl], 8)
    outs_small = [_unpack(o_, sp_rows, [Gf[n].shape for n in small] + [(8, 128)])
                  for o_ in _adamw(sp_all, wsm, msm, vsm, name="adamw_small")]
    for kind, os_ in zip(("g", "d", "m", "v"), outs_small):
        for n, a in zip(small, os_[:-1]):
            if n in SHARDED_SMALL:
                a = lax.dynamic_slice(a, (0, 0, me * W[n].shape[2]), W[n].shape)
            res[kind, n] = a
    return res, outs_small[0][-1][0, 0]


def kernel(x, mem, rel_bias, w_in, ml_conv_w, ml_conv_b, ml_i_bias, ml_f_bias, ml_norm_g, swa_sinks, w_out, ln1_g, ln1_b, xa_wq, xa_wkv, xa_wo, ln2_g, ln2_b, ffn_w_up, ffn_conv_w, ffn_conv_b, ffn_w_down, ln3_g, ln3_b, loss_target, m_rel_bias, m_w_in, m_ml_conv_w, m_ml_conv_b, m_ml_i_bias, m_ml_f_bias, m_ml_norm_g, m_swa_sinks, m_w_out, m_ln1_g, m_ln1_b, m_xa_wq, m_xa_wkv, m_xa_wo, m_ln2_g, m_ln2_b, m_ffn_w_up, m_ffn_conv_w, m_ffn_conv_b, m_ffn_w_down, m_ln3_g, m_ln3_b, v_rel_bias, v_w_in, v_ml_conv_w, v_ml_conv_b, v_ml_i_bias, v_ml_f_bias, v_ml_norm_g, v_swa_sinks, v_w_out, v_ln1_g, v_ln1_b, v_xa_wq, v_xa_wkv, v_xa_wo, v_ln2_g, v_ln2_b, v_ffn_w_up, v_ffn_conv_w, v_ffn_conv_b, v_ffn_w_down, v_ln3_g, v_ln3_b):
    W = dict(rel_bias=rel_bias, w_in=w_in, ml_conv_w=ml_conv_w, ml_conv_b=ml_conv_b, ml_i_bias=ml_i_bias, ml_f_bias=ml_f_bias,
             ml_norm_g=ml_norm_g, swa_sinks=swa_sinks, w_out=w_out, ln1_g=ln1_g, ln1_b=ln1_b, xa_wq=xa_wq, xa_wkv=xa_wkv,
             xa_wo=xa_wo, ln2_g=ln2_g, ln2_b=ln2_b, ffn_w_up=ffn_w_up, ffn_conv_w=ffn_conv_w, ffn_conv_b=ffn_conv_b,
             ffn_w_down=ffn_w_down, ln3_g=ln3_g, ln3_b=ln3_b)
    Mo = dict(rel_bias=m_rel_bias, w_in=m_w_in, ml_conv_w=m_ml_conv_w, ml_conv_b=m_ml_conv_b, ml_i_bias=m_ml_i_bias,
              ml_f_bias=m_ml_f_bias, ml_norm_g=m_ml_norm_g, swa_sinks=m_swa_sinks, w_out=m_w_out, ln1_g=m_ln1_g, ln1_b=m_ln1_b,
              xa_wq=m_xa_wq, xa_wkv=m_xa_wkv, xa_wo=m_xa_wo, ln2_g=m_ln2_g, ln2_b=m_ln2_b, ffn_w_up=m_ffn_w_up,
              ffn_conv_w=m_ffn_conv_w, ffn_conv_b=m_ffn_conv_b, ffn_w_down=m_ffn_w_down, ln3_g=m_ln3_g, ln3_b=m_ln3_b)
    Vo = dict(rel_bias=v_rel_bias, w_in=v_w_in, ml_conv_w=v_ml_conv_w, ml_conv_b=v_ml_conv_b, ml_i_bias=v_ml_i_bias,
              ml_f_bias=v_ml_f_bias, ml_norm_g=v_ml_norm_g, swa_sinks=v_swa_sinks, w_out=v_w_out, ln1_g=v_ln1_g, ln1_b=v_ln1_b,
              xa_wq=v_xa_wq, xa_wkv=v_xa_wkv, xa_wo=v_xa_wo, ln2_g=v_ln2_g, ln2_b=v_ln2_b, ffn_w_up=v_ffn_w_up,
              ffn_conv_w=v_ffn_conv_w, ffn_conv_b=v_ffn_conv_b, ffn_w_down=v_ffn_w_down, ln3_g=v_ln3_g, ln3_b=v_ln3_b)
    S = x.shape[1]
    c_me = lax.axis_index("c")
    me = 4 * lax.axis_index("x") + 2 * lax.axis_index("y") + c_me
    c_idx = jnp.reshape(c_me, (1,)).astype(jnp.int32)
    xs = x.reshape(S, D)
    mems = mem.reshape(mem.shape[1], D)
    tgt = loss_target.reshape(S, D)

    jme = 2 * lax.axis_index("x") + lax.axis_index("y")

    sm_flat, sm_rows = _pack([W[n] for n in SHARDED_SMALL], 8)
    first = _gather([w_in[0].astype(BF16), sm_flat], name="gather_first")
    rest0 = _gather_start([W[n][0].astype(BF16) for n in REST], me, name="gather_rest0_start")
    full = [{"w_in": _layer_full(first[0], "w_in")}, None]
    conv_w = {n: _full_from_shards(s, n) for n, s in zip(SHARDED_SMALL, _unpack(first[1], sm_rows, [W[n].shape for n in SHARDED_SMALL]))}

    bucket = jnp.asarray(_t5_buckets())
    bias = _bias_table(rel_bias, bucket, name="bias_table")

    saved = []
    h0, h0b = xs, xs.astype(BF16)
    for l in range(DEPTH):
        gbias = _row128(jnp.concatenate([ml_i_bias[l], ml_f_bias[l]]))
        sinks = _row128(swa_sinks[l])
        ng = ml_norm_g[l].reshape(1, ML_W)
        proj = _mm(h0b, full[l]["w_in"], dep=(rest0[-1],) if l == 0 else (), name=f"proj{l}")
        qk = _silu_conv_fwd(proj, conv_w["ml_conv_w"][l], ml_conv_b[l].reshape(1, -1), name=f"mlconv{l}")
        h_ml, cs, ns, ms = _mlstm_fwd(qk, proj, gbias, ng, name=f"mlstm{l}")
        h_sw = _swa_fwd(proj, bias, sinks, name=f"swa{l}")
        dep = ()
        if l == 0:
            landed = _gather_finish(rest0, h_sw, name="gather_rest0")
            full[0].update({n: _layer_full(s, n) for n, s in zip(REST, landed)})
            layer1 = _gather_start([W[n][1].astype(BF16) for n in BIG], me, after=(landed[0],), name="gather_layer1_start")
            dep = (layer1[-1],)
        fw = full[l]
        h1, h1b, z1 = _mm_res_ln([h_ml, h_sw], fw["w_out"], h0, ln1_g[l].reshape(1, D), ln1_b[l].reshape(1, D), name=f"mix_out{l}")
        q = _mm(h1b, fw["xa_wq"], dep=dep, name=f"xa_q{l}")
        kv = _mm(mems, fw["xa_wkv"], tm=256, name=f"xa_kv{l}")
        o = _xattn_fwd(q, kv, name=f"xattn{l}")
        h2, h2b, z2 = _mm_res_ln([o], fw["xa_wo"], h1, ln2_g[l].reshape(1, D), ln2_b[l].reshape(1, D), name=f"xa_out{l}")
        act = _ffn_gate_fwd(h2b, fw["ffn_w_up"], conv_w["ffn_conv_w"][l], ffn_conv_b[l].reshape(1, -1), name=f"ffn_gate{l}")
        h3, h3b, z3 = _mm_res_ln([act], fw["ffn_w_down"], h2, ln3_g[l].reshape(1, D), ln3_b[l].reshape(1, D), name=f"ffn_out{l}")
        saved.append(dict(h0b=h0b, proj=proj, qk=qk, cs=cs, ns=ns, ms=ms, h_ml=h_ml, h_sw=h_sw, z1=z1, h1b=h1b, q=q, kv=kv, o=o,
                          z2=z2, h2b=h2b, act=act, z3=z3, gbias=gbias, sinks=sinks, ng=ng))
        h0, h0b = h3, h3b
        if l == 0:
            landed = _gather_finish(layer1, h3b, name="gather_layer1")
            full[1] = {n: _layer_full(s, n) for n, s in zip(BIG, landed)}

    G = {n: [None] * DEPTH for n in NAMES if n != "rel_bias"}
    dbias = [None] * DEPTH
    pending = []
    dz3, dz3b, G["ln3_g"][DEPTH - 1], G["ln3_b"][DEPTH - 1], loss_part = _grad_in(
        [], None, ln=(saved[-1]["z3"], ln3_g[DEPTH - 1].reshape(1, D)), loss=(h0, tgt), name="loss_head")
    for l in reversed(range(DEPTH)):
        sv, fw = saved[l], full[l]
        win = fw["w_in"]
        G["ffn_w_down"][l] = _mm_tn(sv["act"], dz3b, name=f"d_w_down{l}")
        dupg, dupv, G["ffn_conv_w"][l], G["ffn_conv_b"][l] = _ffn_gate_bwd(
            dz3b, fw["ffn_w_down"], sv["h2b"], fw["ffn_w_up"], conv_w["ffn_conv_w"][l], ffn_conv_b[l].reshape(1, -1),
            name=f"ffn_gate_bwd{l}")
        G["ffn_w_up"][l] = jnp.concatenate([_mm_tn(sv["h2b"], dupg, name=f"d_w_up_g{l}"),
                                            _mm_tn(sv["h2b"], dupv, name=f"d_w_up_v{l}")], axis=1)
        dep = ()
        if l == 0:
            names = ("ffn_w_up", "ffn_w_down")
            pending.append((names, 0, "ffn0", _reduce_start("ffn0", names, [G[n][0] for n in names], me)))
            dep = (pending[-1][3][-1],)
        dz2, dz2b, G["ln2_g"][l], G["ln2_b"][l] = _grad_in(
            [(dupg, fw["ffn_w_up"], 0), (dupv, fw["ffn_w_up"], 1)], dz3, ln=(sv["z2"], ln2_g[l].reshape(1, D)), dep=dep,
            name=f"d_h2_{l}")
        G["xa_wo"][l] = _mm_tn(sv["o"], dz2b, name=f"d_xa_wo{l}")
        do = _mm(dz2b, fw["xa_wo"], trans_b=True, name=f"d_xa_o{l}")
        dq, dkv = _xattn_bwd(do, sv["q"], sv["kv"], name=f"xattn_bwd{l}")
        G["xa_wkv"][l] = _mm_tn(mems, dkv, name=f"d_xa_wkv{l}")
        G["xa_wq"][l] = _mm_tn(sv["h1b"], dq, name=f"d_xa_wq{l}")
        dep = ()
        if l == 0:
            names = ("xa_wq", "xa_wkv", "xa_wo")
            pending.append((names, 0, "xa0", _reduce_start("xa0", names, [G[n][0] for n in names], me)))
            dep = (pending[-1][3][-1],)
        dz1, dz1b, G["ln1_g"][l], G["ln1_b"][l] = _grad_in(
            [(dq, fw["xa_wq"], 0)], dz2, ln=(sv["z1"], ln1_g[l].reshape(1, D)), dep=dep, name=f"d_h1_{l}")
        G["w_out"][l] = jnp.concatenate([_mm_tn(sv["h_ml"], dz1b, name=f"d_w_out_ml{l}"),
                                         _mm_tn(sv["h_sw"], dz1b, name=f"d_w_out_sw{l}")], axis=0)
        dhcat = _mm(dz1b, fw["w_out"], trans_b=True, name=f"d_hcat{l}")
        dsw, dbias[l], dsinks = _swa_bwd(dhcat, sv["proj"], bias, sv["sinks"], name=f"swa_bwd{l}")
        dqk, dml, dgb, dng = _mlstm_bwd(dhcat, sv["qk"], sv["proj"], sv["gbias"], sv["ng"], sv["cs"], sv["ns"], sv["ms"],
                                        name=f"mlstm_bwd{l}")
        dqk_pre, G["ml_conv_w"][l], G["ml_conv_b"][l] = _silu_conv_bwd(
            dqk, sv["proj"], conv_w["ml_conv_w"][l], ml_conv_b[l].reshape(1, -1), name=f"mlconv_bwd{l}")
        dw_qk = _mm_tn(sv["h0b"], dqk_pre, name=f"d_w_in_qk{l}")
        dw_ml = _mm_tn(sv["h0b"], dml, name=f"d_w_in_ml{l}")
        dw_sw = _mm_tn(sv["h0b"], dsw, name=f"d_w_in_sw{l}")
        G["w_in"][l] = jnp.concatenate([dw_qk, dw_ml[:, :2 * ML_W + 2 * ML_H], dw_sw], axis=1)
        win_ml = jnp.concatenate([win[:, 1024:2048], win[:, 2816:2944]], axis=1)
        pairs = [(dqk_pre, win, 0), (dml, win_ml, 0), (dsw, win[:, 2048:2816], 0)]
        G["ml_i_bias"][l] = dgb[0, :ML_H]
        G["ml_f_bias"][l] = dgb[0, ML_H:2 * ML_H]
        G["ml_norm_g"][l] = dng
        G["swa_sinks"][l] = dsinks[0, :SW_H]
        if l > 0:
            pending.append((BIG, l, f"l{l}", _reduce_start(f"l{l}", BIG, [G[n][l] for n in BIG], me)))
            dz3, dz3b, G["ln3_g"][l - 1], G["ln3_b"][l - 1] = _grad_in(
                pairs, dz1, ln=(saved[l - 1]["z3"], ln3_g[l - 1].reshape(1, D)), dep=(pending[-1][3][-1],), name=f"d_h0_{l}")
        else:
            grad_x = _grad_in(pairs, dz1, name="d_h0_0").reshape(x.shape)

    names = ("w_in", "w_out")
    parts = {(n, 0): p for n, p in zip(names, _reduce_now("mix0", names, [G[n][0] for n in names], c_idx))}
    for names, l, tag, started in pending:
        landed = _split_wait(started, parts["w_in", 0], _scatter_copies, name=f"reduce_{tag}_wait")
        parts.update({(n, l): p for n, p in zip(names, landed)})
    res = {}
    for n in BIG:
        outs = None
        for l in reversed(range(DEPTH)):
            outs = _adamw(parts[n, l], W[n], Mo[n], Vo[n], layer=l, prev=outs, name=f"adamw_{n}{l}")
        for kind, a in zip(("g", "d", "m", "v"), outs):
            res[kind, n] = a

    Gf = {n: jnp.stack([g.reshape(W[n].shape[1:]) if n in REPLICATED else g for g in G[n]]) for n in G if n not in BIG}
    Gf["rel_bias"] = _bias_table_bwd(dbias, bucket, name="bias_table_bwd")[:, :SW_H]
    res_small, loss = _update_small(Gf, loss_part, W, Mo, Vo, me)
    res.update(res_small)
    return (loss, grad_x, *[res["g", n] for n in NAMES], *[res["d", n] for n in NAMES], *[res["m", n] for n in NAMES],
            *[res["v", n] for n in NAMES])
```

```python
import functools
import math

import jax
import jax.numpy as jnp
import numpy as np
from jax import lax
from jax.experimental import pallas as pl
from jax.experimental.pallas import tpu as pltpu

F32 = jnp.float32
BF16 = jnp.bfloat16

N_DEV = 8
N_CHIP = 4
D = 1024
DEPTH = 2
ML_H = 4
ML_W = 512
ML_DH = 128
ML_L = 64
ML_CONV = 4
SW_DH = 64
SW_W = 512
SW_H = 8
SW_G = 4
SW_KVW = 128
BLK = 128
REL_B = 32
REL_MAXD = 128
XA_H = 4
XA_DH = 256
DFF = 2816
NB_FF = DFF // 128
FFN_CONV = 3
ALPHA = (2.0 * DEPTH) ** 0.25
EPS = 1e-5
N_IN = 2824
NP_IN = 3072
ML_GW = 2 * ML_W + 128
SW_GW = SW_W + 2 * SW_KVW
ADAM_LR = 0.001
ADAM_B1 = 0.9
ADAM_B2 = 0.999
ADAM_EPS = 1e-08
ADAM_WD = 0.01
ADAM_STEP = 10
VMEM_LIMIT = 56 * 1024 * 1024
MESH = pl.DeviceIdType.MESH

NN = ((1,), (0,))
NT = ((1,), (1,))
TN = ((0,), (0,))


def _dg(a, b, dn):
    if a.ndim == 3:
        dims = (((dn[0][0] + 1,), (dn[1][0] + 1,)), ((0,), (0,)))
    else:
        dims = (dn, ((), ()))
    return lax.dot_general(a.astype(BF16), b.astype(BF16), dims, preferred_element_type=F32)


@jax.custom_vjp
def dot_nn(a, b):
    return _dg(a, b, NN)


dot_nn.defvjp(lambda a, b: (_dg(a, b, NN), (a, b)), lambda r, g: (_dg(g, r[1], NT), _dg(r[0], g, TN)))


@jax.custom_vjp
def dot_nt(a, b):
    return _dg(a, b, NT)


dot_nt.defvjp(lambda a, b: (_dg(a, b, NT), (a, b)), lambda r, g: (_dg(g, r[1], NN), _dg(g, r[0], TN)))


@jax.custom_vjp
def dot_tn(a, b):
    return _dg(a, b, TN)


dot_tn.defvjp(lambda a, b: (_dg(a, b, TN), (a, b)), lambda r, g: (_dg(r[1], g, NT), _dg(r[0], g, NN)))


def _params(sem=None):
    return pltpu.CompilerParams(dimension_semantics=sem, vmem_limit_bytes=VMEM_LIMIT)


def _sds(shape, dtype):
    return jax.ShapeDtypeStruct(tuple(shape), dtype)


TOKEN = pl.BlockSpec((8, 128), lambda *_: (0, 0))


def _mm(a, b, *, trans_b=False, out_dtype=F32, add=None, add_scale=1.0, tm=1024, tn=512, dep=(), name):
    a_list = list(a) if isinstance(a, (list, tuple)) else [a]
    M = a_list[0].shape[0]
    N = b.shape[0] if trans_b else b.shape[1]
    tm = min(tm, M)
    tn = next(t for t in (tn, 384, 256, 128) if N % t == 0)
    assert M % tm == 0
    Ka = a_list[0].shape[1]
    assert all(t.shape[1] == Ka for t in a_list)
    tk = next(t for t in (Ka, 1408, 1024) if Ka % t == 0 and t <= 1408)
    na, npa = len(a_list), Ka // tk
    nk = na * npa
    has_add = add is not None

    def body(*refs):
        a_refs, b_ref = refs[:na], refs[na]
        add_ref = refs[na + 1] if has_add else None
        o_ref, acc_ref = refs[-2], refs[-1]
        k = pl.program_id(2)

        def finish(r):
            if has_add:
                r = r + add_scale * add_ref[...].astype(F32)
            o_ref[...] = r.astype(out_dtype)

        for t, a_ref in enumerate(a_refs):
            def step(a_ref=a_ref):
                p = _dg(a_ref[...], b_ref[...], NT if trans_b else NN)
                if nk == 1:
                    finish(p)
                    return

                @pl.when(k == 0)
                def _():
                    acc_ref[...] = p

                @pl.when((k > 0) & (k < nk - 1))
                def _():
                    acc_ref[...] += p

                @pl.when(k == nk - 1)
                def _():
                    finish(acc_ref[...] + p)

            if na == 1:
                step()
            else:
                pl.when((k >= t * npa) & (k < (t + 1) * npa))(step)

    in_specs = [pl.BlockSpec((tm, tk), lambda i, j, k, t=t: (i, jnp.clip(k - t * npa, 0, npa - 1))) for t in range(na)]
    in_specs.append(pl.BlockSpec((tn, tk), lambda i, j, k: (j, k)) if trans_b else pl.BlockSpec((tk, tn), lambda i, j, k: (k, j)))
    args = a_list + [b]
    if has_add:
        in_specs.append(pl.BlockSpec((tm, tn), lambda i, j, k: (i, j)))
        args.append(add)
    in_specs += [TOKEN] * len(dep)
    args += list(dep)
    return pl.pallas_call(
        body, name=name, grid=(M // tm, N // tn, nk), in_specs=in_specs,
        out_specs=pl.BlockSpec((tm, tn), lambda i, j, k: (i, j)), out_shape=_sds((M, N), out_dtype),
        scratch_shapes=[pltpu.VMEM((tm, tn) if nk > 1 else (8, 128), F32)],
        compiler_params=_params(("parallel", "parallel", "arbitrary")))(*args)


def _wgrad(a_t, g, *, name):
    K, N = a_t.shape[0], g.shape[1]
    return _mm(a_t, g, tm=K if K <= 1024 else K // 2, tn=next(t for t in range(1536, 0, -128) if N % t == 0), name=name)


def _mm_tn(a, g, *, name):
    S, K = a.shape
    N = g.shape[1]
    tk = K if K <= 1024 else K // 2
    tn = next(t for t in range(1536, 0, -128) if N % t == 0)
    ts = min(512, S)
    ns = S // ts
    assert K % tk == 0 and S % ts == 0

    def body(a_ref, g_ref, o_ref):
        s = pl.program_id(2)
        p = _dg(a_ref[...], g_ref[...], TN)

        @pl.when(s == 0)
        def _():
            o_ref[...] = p

        @pl.when(s > 0)
        def _():
            o_ref[...] += p

    return pl.pallas_call(
        body, name=name, grid=(K // tk, N // tn, ns),
        in_specs=[pl.BlockSpec((ts, tk), lambda i, j, s: (s, i)), pl.BlockSpec((ts, tn), lambda i, j, s: (s, j))],
        out_specs=pl.BlockSpec((tk, tn), lambda i, j, s: (i, j)), out_shape=_sds((K, N), F32),
        compiler_params=_params(("parallel", "parallel", "arbitrary")))(a, g)


def _mm_res_ln(a_list, w, resid, gam, bet, *, name):
    M, Ka = a_list[0].shape
    na = len(a_list)
    assert w.shape[0] == na * Ka
    tm = min(256, M)

    def body(*refs):
        a_refs, w_refs = refs[:na], refs[na:2 * na]
        r_ref, g_ref, b_ref, y_ref, yb_ref, yt_ref, z_ref = refs[2 * na:]
        z = ALPHA * r_ref[...]
        for a_ref, w_ref in zip(a_refs, w_refs):
            z = z + _dg(a_ref[...], w_ref[...], NN)
        mu = jnp.mean(z, axis=1, keepdims=True)
        zc = z - mu
        var = jnp.mean(zc * zc, axis=1, keepdims=True)
        y = zc * lax.rsqrt(var + EPS) * g_ref[...] + b_ref[...]
        y_ref[...] = y
        yb_ref[...] = y.astype(BF16)
        yt_ref[...] = y.T.astype(BF16)
        z_ref[...] = z

    row = pl.BlockSpec((tm, D), lambda i: (i, 0))
    vec = pl.BlockSpec((1, D), lambda i: (0, 0))
    a_specs = [pl.BlockSpec((tm, Ka), lambda i: (i, 0)) for _ in a_list]
    w_specs = [pl.BlockSpec((Ka, D), lambda i, t=t: (t, 0)) for t in range(na)]
    return pl.pallas_call(
        body, name=name, grid=(M // tm,), in_specs=a_specs + w_specs + [row, vec, vec],
        out_specs=[row, row, pl.BlockSpec((D, tm), lambda i: (0, i)), row],
        out_shape=[_sds((M, D), F32), _sds((M, D), BF16), _sds((D, M), BF16), _sds((M, D), F32)],
        compiler_params=_params(("parallel",)))(*a_list, *([w] * na), resid, gam, bet)


def _grad_in(pairs, add, *, ln=None, loss=None, dep=(), name):
    M = (add if add is not None else loss[0]).shape[0]
    tm = min(256, M)
    npair, nd = len(pairs), len(dep)
    has_ln, has_loss = ln is not None, loss is not None

    def body(*refs):
        n_in = 2 * npair + (2 if has_loss else 1) + 2 * has_ln + nd
        ins, outs = refs[:n_in], refs[n_in:]
        i = pl.program_id(0)
        pos = 2 * npair
        if has_loss:
            e = ins[pos][...] - ins[pos + 1][...]
            pos += 2
            dy = e * (1.0 / D)
            part = 0.5 * jnp.sum(jnp.sum(e * e, axis=1, keepdims=True) * (1.0 / D), axis=0, keepdims=True)
        else:
            dy = ALPHA * ins[pos][...]
            pos += 1
            for t in range(npair):
                dy = dy + _dg(ins[2 * t][...], ins[2 * t + 1][...], NT)
        if not has_ln:
            outs[0][...] = dy
            return
        z, g_ref = ins[pos][...], ins[pos + 1]
        mu = jnp.mean(z, axis=1, keepdims=True)
        zc = z - mu
        var = jnp.mean(zc * zc, axis=1, keepdims=True)
        rstd = lax.rsqrt(var + EPS)
        xh = zc * rstd
        dxh = dy * g_ref[...]
        m1 = jnp.mean(dxh, axis=1, keepdims=True)
        m2 = jnp.mean(dxh * xh, axis=1, keepdims=True)
        dz = rstd * (dxh - m1 - xh * m2)
        outs[0][...] = dz
        outs[1][...] = dz.astype(BF16)
        acc = [(outs[2], jnp.sum(dy * xh, axis=0, keepdims=True)), (outs[3], jnp.sum(dy, axis=0, keepdims=True))]
        if has_loss:
            acc.append((outs[4], jnp.broadcast_to(part, (8, 128))))

        @pl.when(i == 0)
        def _():
            for ref, val in acc:
                ref[...] = val

        @pl.when(i > 0)
        def _():
            for ref, val in acc:
                ref[...] += val

    row = pl.BlockSpec((tm, D), lambda i: (i, 0))
    vec = pl.BlockSpec((1, D), lambda i: (0, 0))
    in_specs, args = [], []
    for a, b, blk in pairs:
        in_specs += [pl.BlockSpec((tm, a.shape[1]), lambda i: (i, 0)), pl.BlockSpec((D, a.shape[1]), lambda i, blk=blk: (0, blk))]
        args += [a, b]
    if has_loss:
        in_specs += [row, row]
        args += list(loss)
    else:
        in_specs.append(row)
        args.append(add)
    if has_ln:
        in_specs += [row, vec]
        args += list(ln)
    in_specs += [TOKEN] * nd
    args += list(dep)
    if has_ln:
        out_specs = [row, row, vec, vec] + ([pl.BlockSpec((8, 128), lambda i: (0, 0))] if has_loss else [])
        out_shape = [_sds((M, D), F32), _sds((M, D), BF16), _sds((1, D), F32), _sds((1, D), F32)] + ([_sds((8, 128), F32)] if has_loss else [])
    else:
        out_specs, out_shape = row, _sds((M, D), F32)
    return pl.pallas_call(
        body, name=name, grid=(M // tm,), in_specs=in_specs, out_specs=out_specs, out_shape=out_shape,
        compiler_params=_params(("arbitrary",) if has_ln else ("parallel",)))(*args)


def _shift_down(x, d):
    if d == 0:
        return x
    rows = lax.broadcasted_iota(jnp.int32, x.shape, 0)
    return jnp.where(rows >= d, pltpu.roll(x, d, 0), 0.0)


def _shift_up(x, d):
    if d == 0:
        return x
    S = x.shape[0]
    rows = lax.broadcasted_iota(jnp.int32, x.shape, 0)
    return jnp.where(rows < S - d, pltpu.roll(x, S - d, 0), 0.0)


def _conv(x, w_ref, b_ref, cs, K):
    y = b_ref[:, cs]
    for j in range(K):
        y = y + _shift_down(x, K - 1 - j) * w_ref[j:j + 1, cs]
    return y


def _conv_bwd(dy, x, w_ref, dw_ref, db_ref, cs, K):
    dx = jnp.zeros_like(x)
    for j in range(K):
        sdy = _shift_up(dy, K - 1 - j)
        dx = dx + sdy * w_ref[j:j + 1, cs]
        dw_ref[j:j + 1, cs] = jnp.sum(sdy * x, axis=0, keepdims=True)
    db_ref[:, cs] = jnp.sum(dy, axis=0, keepdims=True)
    return dx


ALL = slice(None)


def _silu_conv_fwd(proj, cw, cb, *, name):
    S = proj.shape[0]

    def body(x_ref, w_ref, b_ref, o_ref):
        o_ref[...] = jax.nn.silu(_conv(x_ref[...], w_ref, b_ref, ALL, ML_CONV))

    col = pl.BlockSpec((S, 128), lambda j: (0, j))
    return pl.pallas_call(
        body, name=name, grid=(8,),
        in_specs=[col, pl.BlockSpec((ML_CONV, 128), lambda j: (0, j)), pl.BlockSpec((1, 128), lambda j: (0, j))],
        out_specs=col, out_shape=_sds((S, 2 * ML_W), F32), compiler_params=_params(("parallel",)))(proj, cw, cb)


def _silu_conv_bwd(dqk, proj, cw, cb, *, name):
    S = proj.shape[0]

    def body(d_ref, x_ref, w_ref, b_ref, dx_ref, dw_ref, db_ref):
        x = x_ref[...]
        y = _conv(x, w_ref, b_ref, ALL, ML_CONV)
        dy = jax.vjp(jax.nn.silu, y)[1](d_ref[...])[0]
        dx_ref[...] = _conv_bwd(dy, x, w_ref, dw_ref, db_ref, ALL, ML_CONV).astype(BF16)

    col = pl.BlockSpec((S, 128), lambda j: (0, j))
    wsp = pl.BlockSpec((ML_CONV, 128), lambda j: (0, j))
    bsp = pl.BlockSpec((1, 128), lambda j: (0, j))
    return pl.pallas_call(
        body, name=name, grid=(8,), in_specs=[col, col, wsp, bsp], out_specs=[col, wsp, bsp],
        out_shape=[_sds((S, 2 * ML_W), BF16), _sds((ML_CONV, 2 * ML_W), F32), _sds((1, 2 * ML_W), F32)],
        compiler_params=_params(("parallel",)))(dqk, proj, cw, cb)


GELU_C0 = math.sqrt(2.0 / math.pi)
GELU_C1 = 0.044715


def _gate_bwd(ug, uv, da):
    t = jnp.tanh(GELU_C0 * (ug + GELU_C1 * (ug * ug * ug)))
    half = 0.5 * (1.0 + t)
    dgelu = half + 0.5 * ug * (1.0 - t * t) * (GELU_C0 * (1.0 + 3.0 * GELU_C1 * (ug * ug)))
    return da * uv * dgelu, da * (ug * half)


def _up_pair(h_ref, ugw_ref, uvw_ref):
    return _dg(h_ref[...], jnp.concatenate([ugw_ref[...], uvw_ref[...]], axis=1), NN)


def _ffn_specs(S):
    return (pl.BlockSpec((D, 128), lambda j: (0, j)), pl.BlockSpec((D, 128), lambda j: (0, j + NB_FF)),
            pl.BlockSpec((FFN_CONV, 128), lambda j: (0, j)), pl.BlockSpec((FFN_CONV, 128), lambda j: (0, j + NB_FF)),
            pl.BlockSpec((1, 128), lambda j: (0, j)), pl.BlockSpec((1, 128), lambda j: (0, j + NB_FF)))


def _ffn_gate_fwd(hb, w_up, cw, cb, *, name):
    S = hb.shape[0]
    ug_, uv_, wg, wv, bg, bv = _ffn_specs(S)

    def body(h_ref, ugw_ref, uvw_ref, wg_ref, wv_ref, bg_ref, bv_ref, o_ref, ot_ref, h_s):
        @pl.when(pl.program_id(0) == 0)
        def _():
            pltpu.sync_copy(h_ref, h_s)

        x2 = _up_pair(h_s, ugw_ref, uvw_ref)
        ug = _conv(x2[:, :128], wg_ref, bg_ref, ALL, FFN_CONV)
        uv = _conv(x2[:, 128:], wv_ref, bv_ref, ALL, FFN_CONV)
        act = jax.nn.gelu(ug) * uv
        o_ref[...] = act.astype(BF16)
        ot_ref[...] = act.T.astype(BF16)

    return pl.pallas_call(
        body, name=name, grid=(NB_FF,), in_specs=[ANY, ug_, uv_, wg, wv, bg, bv],
        out_specs=[pl.BlockSpec((S, 128), lambda j: (0, j)), pl.BlockSpec((128, S), lambda j: (j, 0))],
        out_shape=[_sds((S, DFF), BF16), _sds((DFF, S), BF16)],
        scratch_shapes=[pltpu.VMEM((S, D), BF16)],
        compiler_params=_params(("arbitrary",)))(hb, w_up, w_up, cw, cw, cb, cb)


def _ffn_gate_bwd(dzb, w_down, hb, w_up, cw, cb, *, name):
    S = hb.shape[0]
    ug_, uv_, wg, wv, bg, bv = _ffn_specs(S)

    def body(dz_ref, h_ref, wd_ref, ugw_ref, uvw_ref, wg_ref, wv_ref, bg_ref, bv_ref,
             dxg_ref, dxv_ref, dwg_ref, dwv_ref, dbg_ref, dbv_ref, dz_s, h_s):
        @pl.when(pl.program_id(0) == 0)
        def _():
            pltpu.sync_copy(dz_ref, dz_s)
            pltpu.sync_copy(h_ref, h_s)

        x2 = _up_pair(h_s, ugw_ref, uvw_ref)
        xg, xv = x2[:, :128], x2[:, 128:]
        ug = _conv(xg, wg_ref, bg_ref, ALL, FFN_CONV)
        uv = _conv(xv, wv_ref, bv_ref, ALL, FFN_CONV)
        dug, duv = _gate_bwd(ug, uv, _dg(dz_s[...], wd_ref[...], NT))
        dxg_ref[...] = _conv_bwd(dug, xg, wg_ref, dwg_ref, dbg_ref, ALL, FFN_CONV).astype(BF16)
        dxv_ref[...] = _conv_bwd(duv, xv, wv_ref, dwv_ref, dbv_ref, ALL, FFN_CONV).astype(BF16)

    col = pl.BlockSpec((S, 128), lambda j: (0, j))
    half = _sds((S, DFF), BF16)
    dxg, dxv, dwg, dwv, dbg, dbv = pl.pallas_call(
        body, name=name, grid=(NB_FF,),
        in_specs=[ANY, ANY, pl.BlockSpec((128, D), lambda j: (j, 0)), ug_, uv_, wg, wv, bg, bv],
        out_specs=[col, col, wg, wg, bg, bg],
        out_shape=[half, half, _sds((FFN_CONV, DFF), F32), _sds((FFN_CONV, DFF), F32), _sds((1, DFF), F32), _sds((1, DFF), F32)],
        scratch_shapes=[pltpu.VMEM((S, D), BF16), pltpu.VMEM((S, D), BF16)],
        compiler_params=_params(("arbitrary",)))(dzb, hb, w_down, w_up, w_up, cw, cw, cb, cb)
    return dxg, dxv, jnp.concatenate([dwg, dwv], axis=1), jnp.concatenate([dbg, dbv], axis=1)


def _log_sigmoid(x):
    return jnp.minimum(x, 0.0) - jnp.log1p(jnp.exp(-jnp.abs(x)))


@jax.custom_vjp
def _clamp_div(num, den, floor, shift):
    return num / jnp.maximum(jnp.abs(den), floor)


def _clamp_div_fwd(num, den, floor, shift):
    out = num / jnp.maximum(jnp.abs(den), floor)
    return out, (den, floor, out)


def _clamp_div_bwd(res, g):
    den, floor, out = res
    active = jnp.abs(den) < floor
    dinv = jnp.maximum(jnp.abs(den), floor)
    go = jnp.sum(g * out, axis=-1, keepdims=True)
    ddiv = -go / dinv
    return (g / dinv, jnp.where(active, 0.0, ddiv * jnp.sign(den)), jnp.where(active, ddiv, 0.0),
            jnp.sum(jnp.where(active, go, 0.0), axis=-2, keepdims=True))


_clamp_div.defvjp(_clamp_div_fwd, _clamp_div_bwd)


def _ml_heads(q, k, v, o_pre, gates, gbias, C, n, ng, m, shift):
    H, L, _ = q.shape
    lane1 = lax.broadcasted_iota(jnp.int32, (1, 128), 1)
    gz = gates + jnp.where(lane1 < ML_H, lax.stop_gradient(gbias), gbias)
    gz = jnp.broadcast_to(gz[None], (H, L, 128))
    hid = lax.broadcasted_iota(jnp.int32, (H, L, 128), 0)
    lane = lax.broadcasted_iota(jnp.int32, (H, L, 128), 2)
    ig = jnp.sum(jnp.where(lane == hid, gz, 0.0), axis=2, keepdims=True)
    lf = _log_sigmoid(jnp.sum(jnp.where(lane == ML_H + hid, gz, 0.0), axis=2, keepdims=True))
    r = lax.broadcasted_iota(jnp.int32, (H, L, L), 1)
    c = lax.broadcasted_iota(jnp.int32, (H, L, L), 2)
    eye, tril = r == c, r >= c

    def to_row(col):
        return jnp.sum(jnp.where(eye, col, 0.0), axis=1, keepdims=True)

    b_col = jnp.sum(jnp.where(tril, to_row(lf), 0.0), axis=2, keepdims=True)
    Dm = jnp.where(tril, b_col - to_row(b_col) + to_row(ig), -jnp.inf)
    inter = b_col + m
    m_t = jnp.maximum(inter, jnp.max(Dm, axis=2, keepdims=True))
    w_inter = jnp.exp(inter - m_t)
    ks = k * (ML_DH ** -0.5)
    s = dot_nt(q, ks) * jnp.exp(Dm - m_t)
    num = w_inter * dot_nn(q, C) + dot_nn(s, v)
    den = w_inter * jnp.sum(q * n, axis=2, keepdims=True) + jnp.sum(s, axis=2, keepdims=True)
    h = _clamp_div(num, den, jnp.exp(-m_t), shift)
    g = jnp.sum(lf, axis=1, keepdims=True)
    a = g - b_col + ig
    m_new = jnp.maximum(g + m, jnp.max(a, axis=1, keepdims=True))
    decay = jnp.exp(g + m - m_new)
    wk = jnp.exp(a - m_new)
    C_new = decay * C + dot_tn(ks * wk, v)
    n_new = decay * n + jnp.sum(wk * ks, axis=1, keepdims=True)
    mu = jnp.mean(h, axis=2, keepdims=True)
    hc = h - mu
    var = jnp.mean(hc * hc, axis=2, keepdims=True)
    out = jax.nn.sigmoid(o_pre) * (hc * lax.rsqrt(var + EPS) * ng)
    return out, C_new, n_new, m_new


def _hs(h, off=0):
    return slice(off + h * ML_DH, off + (h + 1) * ML_DH)


def _heads(ref, off=0):
    return jnp.stack([ref[:, _hs(h, off)] for h in range(ML_H)])


def _mlstm_fwd(qk, proj, gbias, ng, *, name):
    S = qk.shape[0]
    nc = S // ML_L

    def body(q_ref, k_ref, v_ref, o_ref, g_ref, gb_ref, ng_ref, h_ref, cs_ref, ns_ref, ms_ref, c_s, n_s, m_s):
        @pl.when(pl.program_id(0) == 0)
        def _():
            c_s[...] = jnp.zeros_like(c_s)
            n_s[...] = jnp.zeros_like(n_s)
            m_s[...] = jnp.zeros_like(m_s)

        C, n = c_s[...], n_s[...]
        cs_ref[0] = C
        ns_ref[0] = n
        ms_ref[0] = m_s[...]
        out, C2, n2, m2 = _ml_heads(_heads(q_ref), _heads(k_ref), _heads(v_ref), _heads(o_ref), g_ref[...], gb_ref[...], C, n,
                                    _heads(ng_ref), m_s[:, :, 0:1], jnp.zeros((ML_H, 1, 1), F32))
        for h in range(ML_H):
            h_ref[:, _hs(h)] = out[h].astype(BF16)
        c_s[...] = C2
        n_s[...] = n2
        m_s[...] = jnp.broadcast_to(m2, (ML_H, 1, 128))

    def w(j):
        return pl.BlockSpec((ML_L, ML_W), lambda c, j=j: (c, j))

    return pl.pallas_call(
        body, name=name, grid=(nc,),
        in_specs=[w(0), w(1), w(2), w(3), pl.BlockSpec((ML_L, 128), lambda c: (c, 22)),
                  pl.BlockSpec((1, 128), lambda c: (0, 0)), pl.BlockSpec((1, ML_W), lambda c: (0, 0))],
        out_specs=[w(0), pl.BlockSpec((1, ML_H, ML_DH, ML_DH), lambda c: (c, 0, 0, 0)),
                   pl.BlockSpec((1, ML_H, 1, 128), lambda c: (c, 0, 0, 0)), pl.BlockSpec((1, ML_H, 1, 128), lambda c: (c, 0, 0, 0))],
        out_shape=[_sds((S, ML_W), BF16), _sds((nc, ML_H, ML_DH, ML_DH), F32), _sds((nc, ML_H, 1, 128), F32),
                   _sds((nc, ML_H, 1, 128), F32)],
        scratch_shapes=[pltpu.VMEM((ML_H, ML_DH, ML_DH), F32), pltpu.VMEM((ML_H, 1, 128), F32), pltpu.VMEM((ML_H, 1, 128), F32)],
        compiler_params=_params(("arbitrary",)))(qk, qk, proj, proj, proj, gbias, ng)


def _mlstm_bwd(dh, qk, proj, gbias, ng, cs, ns, ms, *, name):
    S = qk.shape[0]
    nc = S // ML_L

    def body(dh_ref, q_ref, k_ref, v_ref, o_ref, g_ref, gb_ref, ng_ref, cs_ref, ns_ref, ms_ref,
             dqk_ref, dml_ref, dgb_ref, dng_ref, dc_s, dn_s, dm_s):
        @pl.when(pl.program_id(0) == 0)
        def _():
            dc_s[...] = jnp.zeros_like(dc_s)
            dn_s[...] = jnp.zeros_like(dn_s)
            dm_s[...] = jnp.zeros_like(dm_s)
            dgb_ref[...] = jnp.zeros_like(dgb_ref)
            dng_ref[...] = jnp.zeros_like(dng_ref)

        _, vjp = jax.vjp(_ml_heads, _heads(q_ref), _heads(k_ref), _heads(v_ref), _heads(o_ref), g_ref[...], gb_ref[...],
                         cs_ref[0], ns_ref[0], _heads(ng_ref), ms_ref[0][:, :, 0:1], jnp.zeros((ML_H, 1, 1), F32))
        dq, dk, dv, do, dgates, dgb, dC, dn, dng, dm, dshift = vjp((_heads(dh_ref), dc_s[...], dn_s[...], dm_s[:, :, 0:1]))
        lane1 = lax.broadcasted_iota(jnp.int32, (1, 128), 1)
        for h in range(ML_H):
            dqk_ref[:, _hs(h)] = dq[h]
            dqk_ref[:, _hs(h, ML_W)] = dk[h]
            dml_ref[:, _hs(h)] = dv[h].astype(BF16)
            dml_ref[:, _hs(h, ML_W)] = do[h].astype(BF16)
            dng_ref[:, _hs(h)] += dng[h]
            dgb = jnp.where(lane1 == h, dshift[h], dgb)
        dc_s[...] = dC
        dn_s[...] = dn
        dm_s[...] = jnp.broadcast_to(dm, (ML_H, 1, 128))
        dml_ref[:, 2 * ML_W:] = dgates.astype(BF16)
        dgb_ref[...] += dgb

    def w(j):
        return pl.BlockSpec((ML_L, ML_W), lambda c, j=j: (nc - 1 - c, j))

    vec = pl.BlockSpec((1, 128), lambda c: (0, 0))
    vecw = pl.BlockSpec((1, ML_W), lambda c: (0, 0))
    gsp = pl.BlockSpec((ML_L, 128), lambda c: (nc - 1 - c, 22))
    st = pl.BlockSpec((1, ML_H, 1, 128), lambda c: (nc - 1 - c, 0, 0, 0))
    return pl.pallas_call(
        body, name=name, grid=(nc,),
        in_specs=[w(0), w(0), w(1), w(2), w(3), gsp, vec, vecw,
                  pl.BlockSpec((1, ML_H, ML_DH, ML_DH), lambda c: (nc - 1 - c, 0, 0, 0)), st, st],
        out_specs=[pl.BlockSpec((ML_L, 2 * ML_W), lambda c: (nc - 1 - c, 0)), pl.BlockSpec((ML_L, ML_GW), lambda c: (nc - 1 - c, 0)),
                   vec, vecw],
        out_shape=[_sds((S, 2 * ML_W), F32), _sds((S, ML_GW), BF16), _sds((1, 128), F32), _sds((1, ML_W), F32)],
        scratch_shapes=[pltpu.VMEM((ML_H, ML_DH, ML_DH), F32), pltpu.VMEM((ML_H, 1, 128), F32), pltpu.VMEM((ML_H, 1, 128), F32)],
        compiler_params=_params(("arbitrary",)))(dh, qk, qk, proj, proj, proj, gbias, ng, cs, ns, ms)


def _t5_buckets():
    r = np.arange(BLK)[:, None]
    c = np.arange(2 * BLK)[None, :]
    n = np.maximum(BLK + r - c, 0)
    max_exact = REL_B // 2
    nf = np.maximum(n, 1).astype(np.float32)
    large = max_exact + (np.log(nf / np.float32(max_exact)) / np.float32(math.log(REL_MAXD / max_exact))
                         * np.float32(REL_B - max_exact)).astype(np.int32)
    large = np.minimum(large, REL_B - 1)
    return np.where(n < max_exact, n, large).astype(np.int32)


def _bias_table(rel_bias, bucket, *, name):
    def body(rb_ref, bk_ref, o_ref):
        bk = bk_ref[...]
        for h in range(SW_H):
            acc = jnp.zeros((BLK, 2 * BLK), F32)
            for b in range(REL_B):
                acc = jnp.where(bk == b, rb_ref[b, h], acc)
            o_ref[h] = acc

    return pl.pallas_call(
        body, name=name, in_specs=[pl.BlockSpec(memory_space=pltpu.SMEM), pl.BlockSpec(memory_space=pltpu.VMEM)],
        out_specs=pl.BlockSpec(memory_space=pltpu.VMEM), out_shape=_sds((SW_H, BLK, 2 * BLK), F32),
        compiler_params=_params())(rel_bias, bucket)


def _bias_table_bwd(dbias_list, bucket, *, name):
    nl = len(dbias_list)

    def body(*refs):
        d_refs, bk_ref, o_ref = refs[:nl], refs[nl], refs[nl + 1]
        bk = bk_ref[...]
        rows = lax.broadcasted_iota(jnp.int32, (REL_B, 128), 0)
        lanes = lax.broadcasted_iota(jnp.int32, (REL_B, 128), 1)
        acc = jnp.zeros((REL_B, 128), F32)
        for h in range(SW_H):
            d = d_refs[0][h]
            for d_ref in d_refs[1:]:
                d = d + d_ref[h]
            for b in range(REL_B):
                t = jnp.sum(jnp.sum(jnp.where(bk == b, d, 0.0), axis=0, keepdims=True), axis=1, keepdims=True)
                acc = jnp.where((rows == b) & (lanes == h), t, acc)
        o_ref[...] = acc

    vm = pl.BlockSpec(memory_space=pltpu.VMEM)
    return pl.pallas_call(
        body, name=name, in_specs=[vm] * (nl + 1), out_specs=vm, out_shape=_sds((REL_B, 128), F32),
        compiler_params=_params())(*dbias_list, bucket)


def _swa_heads(q, kp, kc, vp, vc, bp, bc, sinks, has_prev):
    def rep(t):
        return jnp.concatenate([t[g:g + 1] for g in range(SW_H // SW_G) for _ in range(SW_G)], axis=0)

    r = lax.broadcasted_iota(jnp.int32, (SW_H, BLK, BLK), 1)
    c = lax.broadcasted_iota(jnp.int32, (SW_H, BLK, BLK), 2)
    hid = lax.broadcasted_iota(jnp.int32, (SW_H, 1, 128), 0)
    lane = lax.broadcasted_iota(jnp.int32, (SW_H, 1, 128), 2)
    sink = jnp.sum(jnp.where(lane == hid, jnp.broadcast_to(sinks[None], (SW_H, 1, 128)), 0.0), axis=2, keepdims=True)
    lp = jnp.where((c > r) & has_prev, dot_nt(q, rep(kp)) * (SW_DH ** -0.5) + bp, -jnp.inf)
    lc = jnp.where(c <= r, dot_nt(q, rep(kc)) * (SW_DH ** -0.5) + bc, -jnp.inf)
    mx = jnp.maximum(jnp.maximum(jnp.max(lp, axis=2, keepdims=True), jnp.max(lc, axis=2, keepdims=True)), sink)
    mx = lax.stop_gradient(mx)
    pp, pc = jnp.exp(lp - mx), jnp.exp(lc - mx)
    den = jnp.sum(pp, axis=2, keepdims=True) + jnp.sum(pc, axis=2, keepdims=True) + jnp.exp(sink - mx)
    return dot_nn(pp / den, rep(vp)) + dot_nn(pc / den, rep(vc))


def _qs(h, off=0):
    return slice(off + h * SW_DH, off + (h + 1) * SW_DH)


def _split(ref, n):
    return jnp.stack([ref[:, _qs(h)] for h in range(n)])


def _swa_fwd(proj, bias, sinks, *, name):
    S = proj.shape[0]
    nb = S // BLK
    nkv = SW_H // SW_G

    def body(q_ref, kp_ref, kc_ref, vp_ref, vc_ref, b_ref, s_ref, o_ref):
        out = _swa_heads(_split(q_ref, SW_H), _split(kp_ref, nkv), _split(kc_ref, nkv), _split(vp_ref, nkv), _split(vc_ref, nkv),
                         b_ref[:, :, :BLK], b_ref[:, :, BLK:], s_ref[...], pl.program_id(0) > 0)
        for h in range(SW_H):
            o_ref[:, _qs(h)] = out[h].astype(BF16)

    def cur(j):
        return pl.BlockSpec((BLK, 128), lambda n, j=j: (n, j))

    def prev(j):
        return pl.BlockSpec((BLK, 128), lambda n, j=j: (jnp.maximum(n - 1, 0), j))

    return pl.pallas_call(
        body, name=name, grid=(nb,),
        in_specs=[pl.BlockSpec((BLK, SW_W), lambda n: (n, 4)), prev(20), cur(20), prev(21), cur(21),
                  pl.BlockSpec((SW_H, BLK, 2 * BLK), lambda n: (0, 0, 0)), pl.BlockSpec((1, 128), lambda n: (0, 0))],
        out_specs=pl.BlockSpec((BLK, SW_W), lambda n: (n, 0)), out_shape=_sds((S, SW_W), BF16),
        compiler_params=_params(("parallel",)))(proj, proj, proj, proj, proj, bias, sinks)


def _swa_bwd(dh, proj, bias, sinks, *, name):
    S = proj.shape[0]
    nb = S // BLK

    nkv = SW_H // SW_G

    def body(dh_ref, q_ref, kp_ref, kc_ref, vp_ref, vc_ref, b_ref, s_ref, dsw_ref, db_ref, ds_ref, ck_s, cv_s):
        i = pl.program_id(0)

        @pl.when(i == 0)
        def _():
            ck_s[...] = jnp.zeros_like(ck_s)
            cv_s[...] = jnp.zeros_like(cv_s)
            db_ref[...] = jnp.zeros_like(db_ref)
            ds_ref[...] = jnp.zeros_like(ds_ref)

        f = functools.partial(_swa_heads, has_prev=i < nb - 1)
        _, vjp = jax.vjp(f, _split(q_ref, SW_H), _split(kp_ref, nkv), _split(kc_ref, nkv), _split(vp_ref, nkv),
                         _split(vc_ref, nkv), b_ref[:, :, :BLK], b_ref[:, :, BLK:], s_ref[...])
        dq, dkp, dkc, dvp, dvc, dbp, dbc, ds = vjp(_split(dh_ref, SW_H))
        for h in range(SW_H):
            dsw_ref[:, _qs(h)] = dq[h].astype(BF16)
        for g in range(nkv):
            dsw_ref[:, _qs(g, SW_W)] = (dkc[g] + ck_s[:, _qs(g)]).astype(BF16)
            dsw_ref[:, _qs(g, SW_W + SW_KVW)] = (dvc[g] + cv_s[:, _qs(g)]).astype(BF16)
            ck_s[:, _qs(g)] = dkp[g]
            cv_s[:, _qs(g)] = dvp[g]
        db_ref[:, :, :BLK] += dbp
        db_ref[:, :, BLK:] += dbc
        ds_ref[...] += ds

    def cur(j):
        return pl.BlockSpec((BLK, 128), lambda i, j=j: (nb - 1 - i, j))

    def prev(j):
        return pl.BlockSpec((BLK, 128), lambda i, j=j: (jnp.maximum(nb - 2 - i, 0), j))

    bsp = pl.BlockSpec((SW_H, BLK, 2 * BLK), lambda i: (0, 0, 0))
    vec = pl.BlockSpec((1, 128), lambda i: (0, 0))
    return pl.pallas_call(
        body, name=name, grid=(nb,),
        in_specs=[pl.BlockSpec((BLK, SW_W), lambda i: (nb - 1 - i, 1)), pl.BlockSpec((BLK, SW_W), lambda i: (nb - 1 - i, 4)),
                  prev(20), cur(20), prev(21), cur(21), bsp, vec],
        out_specs=[pl.BlockSpec((BLK, SW_GW), lambda i: (nb - 1 - i, 0)), bsp, vec],
        out_shape=[_sds((S, SW_GW), BF16), _sds((SW_H, BLK, 2 * BLK), F32), _sds((1, 128), F32)],
        scratch_shapes=[pltpu.VMEM((BLK, 128), F32)] * 2,
        compiler_params=_params(("arbitrary",)))(dh, proj, proj, proj, proj, proj, bias, sinks)


XA_TM = 512


def _xa_head(qh, kh, vh):
    logits = dot_nt(qh, kh) * (XA_DH ** -0.5)
    mx = lax.stop_gradient(jnp.max(logits, axis=1, keepdims=True))
    e = jnp.exp(logits - mx)
    return dot_nn(e / jnp.sum(e, axis=1, keepdims=True), vh)


def _xs(h, off=0):
    return slice(off + h * XA_DH, off + (h + 1) * XA_DH)


def _xattn_fwd(q, kv, *, name):
    S = q.shape[0]
    M = kv.shape[0]

    def body(q_ref, kv_ref, o_ref):
        for h in range(XA_H):
            o_ref[:, _xs(h)] = _xa_head(q_ref[:, _xs(h)], kv_ref[:, _xs(h)], kv_ref[:, _xs(h, D)]).astype(BF16)

    tm = min(XA_TM, S)
    row = pl.BlockSpec((tm, D), lambda i: (i, 0))
    return pl.pallas_call(
        body, name=name, grid=(S // tm,), in_specs=[row, pl.BlockSpec((M, 2 * D), lambda i: (0, 0))], out_specs=row,
        out_shape=_sds((S, D), BF16), compiler_params=_params(("parallel",)))(q, kv)


def _xattn_bwd(do, q, kv, *, name):
    S = q.shape[0]
    M = kv.shape[0]

    def body(do_ref, q_ref, kv_ref, dq_ref, dkv_ref):
        @pl.when(pl.program_id(0) == 0)
        def _():
            dkv_ref[...] = jnp.zeros_like(dkv_ref)

        for h in range(XA_H):
            _, vjp = jax.vjp(_xa_head, q_ref[:, _xs(h)], kv_ref[:, _xs(h)], kv_ref[:, _xs(h, D)])
            dq, dk, dv = vjp(do_ref[:, _xs(h)])
            dq_ref[:, _xs(h)] = dq.astype(BF16)
            dkv_ref[:, _xs(h)] += dk
            dkv_ref[:, _xs(h, D)] += dv

    tm = min(XA_TM, S)
    row = pl.BlockSpec((tm, D), lambda i: (i, 0))
    full = pl.BlockSpec((M, 2 * D), lambda i: (0, 0))
    return pl.pallas_call(
        body, name=name, grid=(S // tm,), in_specs=[row, row, full], out_specs=[row, full],
        out_shape=[_sds((S, D), BF16), _sds((M, 2 * D), F32)], compiler_params=_params(("arbitrary",)))(do, q, kv)


ANY = pl.BlockSpec(memory_space=pl.ANY)


def _place():
    x, y, c = lax.axis_index("x"), lax.axis_index("y"), lax.axis_index("c")
    chips = [(1 - x, y), (x, 1 - y), (1 - x, 1 - y)]
    return x, y, c, chips


def _gather(arrs, *, name):
    n = len(arrs)

    def body(*refs):
        srcs, outs = refs[:n], refs[n:2 * n]
        send_sems, recv_sems, local_sems = refs[2 * n:]
        x, y, c, chips = _place()
        me, sib = (x, y, c), (x, y, 1 - c)

        def idx(p):
            return 4 * p[0] + 2 * p[1] + p[2]

        def copy(i, k, block, to, from_src=False):
            return pltpu.make_async_remote_copy(
                src_ref=srcs[i] if from_src else outs[i].at[idx(block)], dst_ref=outs[i].at[idx(block)],
                send_sem=send_sems.at[7 * i + k], recv_sem=recv_sems.at[7 * i + k], device_id=to, device_id_type=MESH)

        local = [pltpu.make_async_copy(srcs[i], outs[i].at[idx(me)], local_sems.at[i]) for i in range(n)]
        for cp in local:
            cp.start()
        first = []
        for i in range(n):
            first.append(copy(i, 0, me, sib, True))
            first += [copy(i, 1 + j, me, (*chip, c), True) for j, chip in enumerate(chips)]
        for cp in first:
            cp.start()
        passed = []
        for j, chip in enumerate(chips):
            for i in range(n):
                copy(i, 1 + j, (*chip, c), me).wait_recv()
                cp = copy(i, 4 + j, (*chip, c), sib)
                cp.start()
                passed.append(cp)
        for i in range(n):
            copy(i, 0, sib, me).wait_recv()
        for j, chip in enumerate(chips):
            for i in range(n):
                copy(i, 4 + j, (*chip, 1 - c), me).wait_recv()
        for cp in first + passed:
            cp.wait_send()
        for cp in local:
            cp.wait()

    return pl.pallas_call(
        body, name=name, in_specs=[ANY] * n, out_specs=[ANY] * n,
        out_shape=[_sds((N_DEV,) + a.shape, a.dtype) for a in arrs],
        scratch_shapes=[pltpu.SemaphoreType.DMA((7 * n,)), pltpu.SemaphoreType.DMA((7 * n,)), pltpu.SemaphoreType.DMA((n,))],
        compiler_params=pltpu.CompilerParams(has_side_effects=True))(*arrs)


def _swap_sibling(arrs, *, name):
    n = len(arrs)

    def body(*refs):
        srcs, outs = refs[:n], refs[n:2 * n]
        send_sems, recv_sems = refs[2 * n:]
        x, y, c, _ = _place()
        copies = [pltpu.make_async_remote_copy(
            src_ref=srcs[i].at[2 * j + (1 - c)], dst_ref=outs[i].at[j], send_sem=send_sems.at[N_CHIP * i + j],
            recv_sem=recv_sems.at[N_CHIP * i + j], device_id=(x, y, 1 - c), device_id_type=MESH)
            for i in range(n) for j in range(N_CHIP)]
        for cp in copies:
            cp.start()
        for cp in copies:
            cp.wait()

    return pl.pallas_call(
        body, name=name, in_specs=[ANY] * n, out_specs=[ANY] * n,
        out_shape=[_sds((N_CHIP,) + a.shape[1:], a.dtype) for a in arrs],
        scratch_shapes=[pltpu.SemaphoreType.DMA((N_CHIP * n,)), pltpu.SemaphoreType.DMA((N_CHIP * n,))],
        compiler_params=pltpu.CompilerParams(has_side_effects=True))(*arrs)


def _pair_sum(part, got, c_idx, *, name):
    _, R, C = part.shape
    tr = next(t for t in (R, 512, 256, 128, 64, 32, 16) if R % t == 0 and t * C <= 1024 * 1024)

    def body(c_ref, p_ref, g_ref, o_ref):
        o_ref[...] = (p_ref[0].astype(F32) + g_ref[...].astype(F32)).astype(o_ref.dtype)

    return pl.pallas_call(
        body, name=name,
        grid_spec=pltpu.PrefetchScalarGridSpec(
            num_scalar_prefetch=1, grid=(N_CHIP, R // tr),
            in_specs=[pl.BlockSpec((1, 1, tr, C), lambda j, r, c_ref: (j, c_ref[0], r, 0)),
                      pl.BlockSpec((1, tr, C), lambda j, r, c_ref: (j, r, 0))],
            out_specs=pl.BlockSpec((1, tr, C), lambda j, r, c_ref: (j, r, 0))),
        out_shape=_sds((N_CHIP, R, C), part.dtype),
        compiler_params=_params(("parallel", "parallel")))(c_idx, part.reshape(N_CHIP, 2, R, C), got)


def _swap_chips(arrs, *, name):
    n = len(arrs)

    def body(*refs):
        srcs, outs = refs[:n], refs[n:2 * n]
        send_sems, recv_sems, local_sems = refs[2 * n:]
        x, y, c, chips = _place()
        jme = 2 * x + y
        local = [pltpu.make_async_copy(srcs[i].at[jme], outs[i].at[jme], local_sems.at[i]) for i in range(n)]
        for cp in local:
            cp.start()
        copies = [pltpu.make_async_remote_copy(
            src_ref=srcs[i].at[2 * chip[0] + chip[1]], dst_ref=outs[i].at[jme], send_sem=send_sems.at[3 * i + j],
            recv_sem=recv_sems.at[3 * i + j], device_id=(*chip, c), device_id_type=MESH)
            for i in range(n) for j, chip in enumerate(chips)]
        for cp in copies:
            cp.start()
        for i in range(n):
            for j, chip in enumerate(chips):
                pltpu.make_async_remote_copy(
                    src_ref=srcs[i].at[jme], dst_ref=outs[i].at[2 * chip[0] + chip[1]], send_sem=send_sems.at[3 * i + j],
                    recv_sem=recv_sems.at[3 * i + j], device_id=(*chip, c), device_id_type=MESH).wait_recv()
        for cp in copies:
            cp.wait_send()
        for cp in local:
            cp.wait()

    return pl.pallas_call(
        body, name=name, in_specs=[ANY] * n, out_specs=[ANY] * n,
        out_shape=[_sds(a.shape, a.dtype) for a in arrs],
        scratch_shapes=[pltpu.SemaphoreType.DMA((3 * n,)), pltpu.SemaphoreType.DMA((3 * n,)), pltpu.SemaphoreType.DMA((n,))],
        compiler_params=pltpu.CompilerParams(has_side_effects=True))(*arrs)


HBM = pl.BlockSpec(memory_space=pltpu.HBM)
SEM = pl.BlockSpec(memory_space=pltpu.SEMAPHORE)
EFFECT = pltpu.SideEffectType.DATAFLOW_SIDE_EFFECTING


def _near_copies(srcs, lands, send_sems, recv_sems):
    x, y, c, chips = _place()
    me = 4 * x + 2 * y + c
    out = []
    for i in range(len(srcs)):
        for k, (px, py, pc) in enumerate([(x, y, 1 - c)] + [(*chip, c) for chip in chips]):
            out.append(tuple(pltpu.make_async_remote_copy(
                src_ref=srcs[i], dst_ref=lands[i].at[slot], send_sem=send_sems.at[4 * i + k], recv_sem=recv_sems.at[4 * i + k],
                device_id=(px, py, pc), device_id_type=MESH) for slot in (me, 4 * px + 2 * py + pc)))
    return out


def _forward_sibling(lands, *, name):
    n = len(lands)

    def body(*refs):
        bufs = refs[n:2 * n]
        send_sems, recv_sems = refs[2 * n:]
        x, y, c, chips = _place()
        copies = [tuple(pltpu.make_async_remote_copy(
            src_ref=bufs[i].at[4 * chip[0] + 2 * chip[1] + c], dst_ref=bufs[i].at[4 * chip[0] + 2 * chip[1] + cc],
            send_sem=send_sems.at[3 * i + j], recv_sem=recv_sems.at[3 * i + j], device_id=(x, y, 1 - c), device_id_type=MESH)
            for cc in (c, 1 - c)) for i in range(n) for j, chip in enumerate(chips)]
        for send, _ in copies:
            send.start()
        for send, recv in copies:
            send.wait_send()
            recv.wait_recv()

    return pl.pallas_call(
        body, name=name, in_specs=[ANY] * n, out_specs=[ANY] * n, out_shape=[_sds(a.shape, a.dtype) for a in lands],
        input_output_aliases={i: i for i in range(n)},
        scratch_shapes=[pltpu.SemaphoreType.DMA((3 * n,)), pltpu.SemaphoreType.DMA((3 * n,))],
        compiler_params=pltpu.CompilerParams(has_side_effects=True))(*lands)


def _scatter_copies(srcs, lands, send_sems, recv_sems):
    x, y, c, _ = _place()
    me = 4 * x + 2 * y + c
    out = []
    for i in range(len(srcs)):
        for j in range(1, N_DEV):
            px, py, pc = x ^ ((j >> 2) & 1), y ^ ((j >> 1) & 1), c ^ (j & 1)
            p = 4 * px + 2 * py + pc
            k = (N_DEV - 1) * i + j - 1
            out.append(tuple(pltpu.make_async_remote_copy(
                src_ref=srcs[i].at[s], dst_ref=lands[i].at[d], send_sem=send_sems.at[k], recv_sem=recv_sems.at[k],
                device_id=(px, py, pc), device_id_type=MESH) for s, d in ((p, me), (me, p))))
    return out


def _split_start(srcs, lands, pattern, ncopy, *, after=(), name):
    n = len(srcs)
    na = len(after)

    def body(*refs):
        sems = refs[2 * n + na:]
        for send, _ in pattern(refs[:n], refs[n:2 * n], sems[0], sems[1]):
            send.start()
        refs[-1][...] = jnp.zeros_like(refs[-1])

    arrs = list(srcs) + list(lands)
    return pl.pallas_call(
        body, name=name,
        out_shape=(pltpu.SemaphoreType.DMA((ncopy,)), pltpu.SemaphoreType.DMA((ncopy,)),
                   *[pltpu.HBM(a.shape, a.dtype) for a in arrs], _sds((8, 128), F32)),
        in_specs=[HBM] * (2 * n) + [ANY] * na, out_specs=(SEM, SEM, *[HBM] * (2 * n), pl.BlockSpec(memory_space=pltpu.VMEM)),
        input_output_aliases={i: 2 + i for i in range(2 * n)},
        compiler_params=pltpu.CompilerParams(has_side_effects=EFFECT))(
            *[pltpu.with_memory_space_constraint(a, pltpu.HBM) for a in arrs], *after)


def _split_wait(started, after, pattern, *, name):
    send_sems, recv_sems, *arrs = started[:-1]
    n = len(arrs) // 2

    def body(*refs):
        for send, recv in pattern(refs[:n], refs[n:2 * n], refs[2 * n], refs[2 * n + 1]):
            send.wait_send()
            recv.wait_recv()

    outs = pl.pallas_call(
        body, name=name, out_shape=tuple(pltpu.HBM(a.shape, a.dtype) for a in arrs),
        in_specs=[HBM] * (2 * n) + [SEM, SEM, ANY], out_specs=tuple([HBM] * (2 * n)),
        input_output_aliases={i: i for i in range(2 * n)},
        compiler_params=pltpu.CompilerParams(has_side_effects=EFFECT))(*arrs, send_sems, recv_sems, after)
    return list(outs[n:])


def _adamw(parts, w, m, v, *, layer=None, prev=None, name):
    P, R, C = parts.shape
    tr = next((t for t in (512, 256, 176, 128, 64, 32, 16, 8) if R % t == 0 and t * C <= 256 * 1024), R)
    c1 = 1.0 / (1.0 - ADAM_B1 ** ADAM_STEP)
    c2 = 1.0 / (1.0 - ADAM_B2 ** ADAM_STEP)
    nprev = 0 if prev is None else 4

    def body(p_ref, w_ref, m_ref, v_ref, *rest):
        g_ref, d_ref, nm_ref, nv_ref = rest[nprev:]
        g = p_ref[0].astype(F32)
        for j in range(1, P):
            g = g + p_ref[j].astype(F32)
        g = g.reshape(w_ref.shape)
        nm = ADAM_B1 * m_ref[...] + (1.0 - ADAM_B1) * g
        nv = ADAM_B2 * v_ref[...] + (1.0 - ADAM_B2) * (g * g)
        g_ref[...] = g
        nm_ref[...] = nm
        nv_ref[...] = nv
        d_ref[...] = -ADAM_LR * ((nm * c1) / (jnp.sqrt(nv * c2) + ADAM_EPS) + ADAM_WD * w_ref[...])

    if layer is None:
        row = pl.BlockSpec((tr, C), lambda i: (i, 0))
    else:
        row = pl.BlockSpec((1, tr, C), lambda i: (layer, i, 0))
    out = _sds(w.shape, F32)
    return pl.pallas_call(
        body, name=name, grid=(R // tr,),
        in_specs=[pl.BlockSpec((P, tr, C), lambda i: (0, i, 0)), row, row, row] + [ANY] * nprev,
        out_specs=[row, row, row, row], out_shape=[out, out, out, out],
        input_output_aliases={4 + k: k for k in range(nprev)},
        compiler_params=_params(("parallel",)))(parts, w, m, v, *(prev or ()))


BIG = ("w_in", "w_out", "xa_wq", "xa_wkv", "xa_wo", "ffn_w_up", "ffn_w_down")
COL_SHARDED = ("w_in", "xa_wkv", "ffn_w_up", "ml_conv_w", "ffn_conv_w")
SHARDED_SMALL = ("ml_conv_w", "ffn_conv_w")
REPLICATED = ("rel_bias", "ml_conv_b", "ml_i_bias", "ml_f_bias", "ml_norm_g", "swa_sinks", "ln1_g", "ln1_b", "ln2_g", "ln2_b",
              "ffn_conv_b", "ln3_g", "ln3_b")
NAMES = ("rel_bias", "w_in", "ml_conv_w", "ml_conv_b", "ml_i_bias", "ml_f_bias", "ml_norm_g", "swa_sinks", "w_out", "ln1_g", "ln1_b",
         "xa_wq", "xa_wkv", "xa_wo", "ln2_g", "ln2_b", "ffn_w_up", "ffn_conv_w", "ffn_conv_b", "ffn_w_down", "ln3_g", "ln3_b")


def _flat_rows(a, mult):
    f = a.reshape(-1)
    n = -(-f.shape[0] // (128 * mult)) * (128 * mult)
    if n != f.shape[0]:
        f = jnp.pad(f, (0, n - f.shape[0]))
    return f.reshape(-1, 128)


def _pack(arrs, mult):
    parts = [_flat_rows(a, mult) for a in arrs]
    return jnp.concatenate(parts, axis=0), [p.shape[0] for p in parts]


def _unpack(flat, rows, shapes):
    out, off = [], 0
    lead = flat.shape[:-2]
    for r, shp in zip(rows, shapes):
        n = int(np.prod(shp))
        piece = flat[..., off:off + r, :].reshape(lead + (r * 128,))[..., :n]
        out.append(piece.reshape(lead + tuple(shp)))
        off += r
    return out


def _full_from_shards(stacked, name):
    if name in COL_SHARDED:
        return jnp.moveaxis(stacked, 0, 2).reshape(stacked.shape[1], stacked.shape[2], N_DEV * stacked.shape[3])
    return jnp.moveaxis(stacked, 0, 1).reshape(stacked.shape[1], N_DEV * stacked.shape[2], stacked.shape[3])


def _shards_from_full(full, name):
    L, A, B = full.shape
    if name in COL_SHARDED:
        return jnp.moveaxis(full.reshape(L, A, N_DEV, B // N_DEV), 2, 0)
    return jnp.moveaxis(full.reshape(L, N_DEV, A // N_DEV, B), 1, 0)


def _pad_win(w):
    z = jnp.zeros(w.shape[:-1] + (NP_IN - N_IN,), w.dtype)
    return jnp.concatenate([w[..., :2048], w[..., 2056:], w[..., 2048:2056], z], axis=-1)


def _row128(v):
    return jnp.pad(v, (0, 128 - v.shape[0])).reshape(1, 128)


REST = BIG[1:]


def _layer_full(stacked, n):
    if n in COL_SHARDED:
        full = jnp.moveaxis(stacked, 0, 1).reshape(stacked.shape[1], N_DEV * stacked.shape[2])
    else:
        full = stacked.reshape(N_DEV * stacked.shape[1], stacked.shape[2])
    return _pad_win(full) if n == "w_in" else full


def _layer_shards(g, n):
    A, B = g.shape
    if n in COL_SHARDED:
        return jnp.moveaxis(g.reshape(A, N_DEV, B // N_DEV), 1, 0).astype(BF16)
    return g.reshape(N_DEV, A // N_DEV, B).astype(BF16)


def _with_own_block(a, idx, nblk):
    return lax.dynamic_update_slice(jnp.zeros((nblk,) + a.shape, a.dtype), a[None], (idx,) + (0,) * a.ndim)


def _gather_start(arrs, me, *, after=(), name):
    return _split_start(arrs, [_with_own_block(a, me, N_DEV) for a in arrs], _near_copies, 4 * len(arrs), after=after, name=name)


def _gather_finish(started, after, *, name):
    return _forward_sibling(_split_wait(started, after, _near_copies, name=name + "_wait"), name=name + "_forward")


def _reduce_now(tag, names, grads, c_idx):
    send = [_layer_shards(g, n) for n, g in zip(names, grads)]
    got = _swap_sibling(send, name=f"swap_sibling_{tag}")
    sums = [_pair_sum(s, g, c_idx, name=f"pair_sum_{tag}_{n}") for n, s, g in zip(names, send, got)]
    return _swap_chips(sums, name=f"swap_chips_{tag}")


def _reduce_start(tag, names, grads, me):
    send = [_layer_shards(g, n) for n, g in zip(names, grads)]
    lands = [_with_own_block(lax.dynamic_index_in_dim(s, me, 0, keepdims=False), me, N_DEV) for s in send]
    return _split_start(send, lands, _scatter_copies, (N_DEV - 1) * len(send), name=f"reduce_{tag}_start")


def _update_small(Gf, loss_part, W, Mo, Vo, me):
    res = {}
    small = REPLICATED + SHARDED_SMALL
    sp_flat, sp_rows = _pack([Gf[n] for n in small] + [loss_part], 8)
    sp_all = _gather([sp_flat], name="gather_small_grads")[0]

    def widen(n, t):
        if n not in SHARDED_SMALL:
            return t[n]
        return lax.dynamic_update_slice(jnp.zeros(Gf[n].shape, F32), t[n], (0, 0, me * t[n].shape[2]))

    zl = jnp.zeros((8, 128), F32)
    wsm, _ = _pack([widen(n, W) for n in small] + [zl], 8)
    msm, _ = _pack([widen(n, Mo) for n in small] + [zl], 8)
    vsm, _ = _pack([widen(n, Vo) for n in small] + [zl], 8)
    outs_small = [_unpack(o_, sp_rows, [Gf[n].shape for n in small] + [(8, 128)])
                  for o_ in _adamw(sp_all, wsm, msm, vsm, name="adamw_small")]
    for kind, os_ in zip(("g", "d", "m", "v"), outs_small):
        for n, a in zip(small, os_[:-1]):
            if n in SHARDED_SMALL:
                a = lax.dynamic_slice(a, (0, 0, me * W[n].shape[2]), W[n].shape)
            res[kind, n] = a
    return res, outs_small[0][-1][0, 0]


def kernel(x, mem, rel_bias, w_in, ml_conv_w, ml_conv_b, ml_i_bias, ml_f_bias, ml_norm_g, swa_sinks, w_out, ln1_g, ln1_b, xa_wq, xa_wkv, xa_wo, ln2_g, ln2_b, ffn_w_up, ffn_conv_w, ffn_conv_b, ffn_w_down, ln3_g, ln3_b, loss_target, m_rel_bias, m_w_in, m_ml_conv_w, m_ml_conv_b, m_ml_i_bias, m_ml_f_bias, m_ml_norm_g, m_swa_sinks, m_w_out, m_ln1_g, m_ln1_b, m_xa_wq, m_xa_wkv, m_xa_wo, m_ln2_g, m_ln2_b, m_ffn_w_up, m_ffn_conv_w, m_ffn_conv_b, m_ffn_w_down, m_ln3_g, m_ln3_b, v_rel_bias, v_w_in, v_ml_conv_w, v_ml_conv_b, v_ml_i_bias, v_ml_f_bias, v_ml_norm_g, v_swa_sinks, v_w_out, v_ln1_g, v_ln1_b, v_xa_wq, v_xa_wkv, v_xa_wo, v_ln2_g, v_ln2_b, v_ffn_w_up, v_ffn_conv_w, v_ffn_conv_b, v_ffn_w_down, v_ln3_g, v_ln3_b):
    W = dict(rel_bias=rel_bias, w_in=w_in, ml_conv_w=ml_conv_w, ml_conv_b=ml_conv_b, ml_i_bias=ml_i_bias, ml_f_bias=ml_f_bias,
             ml_norm_g=ml_norm_g, swa_sinks=swa_sinks, w_out=w_out, ln1_g=ln1_g, ln1_b=ln1_b, xa_wq=xa_wq, xa_wkv=xa_wkv,
             xa_wo=xa_wo, ln2_g=ln2_g, ln2_b=ln2_b, ffn_w_up=ffn_w_up, ffn_conv_w=ffn_conv_w, ffn_conv_b=ffn_conv_b,
             ffn_w_down=ffn_w_down, ln3_g=ln3_g, ln3_b=ln3_b)
    Mo = dict(rel_bias=m_rel_bias, w_in=m_w_in, ml_conv_w=m_ml_conv_w, ml_conv_b=m_ml_conv_b, ml_i_bias=m_ml_i_bias,
              ml_f_bias=m_ml_f_bias, ml_norm_g=m_ml_norm_g, swa_sinks=m_swa_sinks, w_out=m_w_out, ln1_g=m_ln1_g, ln1_b=m_ln1_b,
              xa_wq=m_xa_wq, xa_wkv=m_xa_wkv, xa_wo=m_xa_wo, ln2_g=m_ln2_g, ln2_b=m_ln2_b, ffn_w_up=m_ffn_w_up,
              ffn_conv_w=m_ffn_conv_w, ffn_conv_b=m_ffn_conv_b, ffn_w_down=m_ffn_w_down, ln3_g=m_ln3_g, ln3_b=m_ln3_b)
    Vo = dict(rel_bias=v_rel_bias, w_in=v_w_in, ml_conv_w=v_ml_conv_w, ml_conv_b=v_ml_conv_b, ml_i_bias=v_ml_i_bias,
              ml_f_bias=v_ml_f_bias, ml_norm_g=v_ml_norm_g, swa_sinks=v_swa_sinks, w_out=v_w_out, ln1_g=v_ln1_g, ln1_b=v_ln1_b,
              xa_wq=v_xa_wq, xa_wkv=v_xa_wkv, xa_wo=v_xa_wo, ln2_g=v_ln2_g, ln2_b=v_ln2_b, ffn_w_up=v_ffn_w_up,
              ffn_conv_w=v_ffn_conv_w, ffn_conv_b=v_ffn_conv_b, ffn_w_down=v_ffn_w_down, ln3_g=v_ln3_g, ln3_b=v_ln3_b)
    S = x.shape[1]
    c_me = lax.axis_index("c")
    me = 4 * lax.axis_index("x") + 2 * lax.axis_index("y") + c_me
    c_idx = jnp.reshape(c_me, (1,)).astype(jnp.int32)
    xs = x.reshape(S, D)
    mems = mem.reshape(mem.shape[1], D)
    tgt = loss_target.reshape(S, D)

    jme = 2 * lax.axis_index("x") + lax.axis_index("y")

    sm_flat, sm_rows = _pack([W[n] for n in SHARDED_SMALL], 8)
    first = _gather([w_in[0].astype(BF16), sm_flat], name="gather_first")
    rest0 = _gather_start([W[n][0].astype(BF16) for n in REST], me, name="gather_rest0_start")
    full = [{"w_in": _layer_full(first[0], "w_in")}, None]
    conv_w = {n: _full_from_shards(s, n) for n, s in zip(SHARDED_SMALL, _unpack(first[1], sm_rows, [W[n].shape for n in SHARDED_SMALL]))}

    bucket = jnp.asarray(_t5_buckets())
    bias = _bias_table(rel_bias, bucket, name="bias_table")

    saved = []
    h0, h0b = xs, xs.astype(BF16)
    h0t = h0b.T
    for l in range(DEPTH):
        gbias = _row128(jnp.concatenate([ml_i_bias[l], ml_f_bias[l]]))
        sinks = _row128(swa_sinks[l])
        ng = ml_norm_g[l].reshape(1, ML_W)
        proj = _mm(h0b, full[l]["w_in"], dep=(rest0[-1],) if l == 0 else (), name=f"proj{l}")
        qk = _silu_conv_fwd(proj, conv_w["ml_conv_w"][l], ml_conv_b[l].reshape(1, -1), name=f"mlconv{l}")
        h_ml, cs, ns, ms = _mlstm_fwd(qk, proj, gbias, ng, name=f"mlstm{l}")
        h_sw = _swa_fwd(proj, bias, sinks, name=f"swa{l}")
        dep = ()
        if l == 0:
            landed = _gather_finish(rest0, h_sw, name="gather_rest0")
            full[0].update({n: _layer_full(s, n) for n, s in zip(REST, landed)})
            layer1 = _gather_start([W[n][1].astype(BF16) for n in BIG], me, after=(landed[0],), name="gather_layer1_start")
            dep = (layer1[-1],)
        fw = full[l]
        h1, h1b, h1t, z1 = _mm_res_ln([h_ml, h_sw], fw["w_out"], h0, ln1_g[l].reshape(1, D), ln1_b[l].reshape(1, D),
                                      name=f"mix_out{l}")
        q = _mm(h1b, fw["xa_wq"], dep=dep, name=f"xa_q{l}")
        kv = _mm(mems, fw["xa_wkv"], tm=256, name=f"xa_kv{l}")
        o = _xattn_fwd(q, kv, name=f"xattn{l}")
        h2, h2b, h2t, z2 = _mm_res_ln([o], fw["xa_wo"], h1, ln2_g[l].reshape(1, D), ln2_b[l].reshape(1, D), name=f"xa_out{l}")
        act, act_t = _ffn_gate_fwd(h2b, fw["ffn_w_up"], conv_w["ffn_conv_w"][l], ffn_conv_b[l].reshape(1, -1), name=f"ffn_gate{l}")
        h3, h3b, h3t, z3 = _mm_res_ln([act], fw["ffn_w_down"], h2, ln3_g[l].reshape(1, D), ln3_b[l].reshape(1, D),
                                      name=f"ffn_out{l}")
        saved.append(dict(h0t=h0t, proj=proj, qk=qk, cs=cs, ns=ns, ms=ms, h_ml=h_ml, h_sw=h_sw, z1=z1, h1t=h1t, q=q, kv=kv, o=o,
                          z2=z2, h2b=h2b, h2t=h2t, act_t=act_t, z3=z3, gbias=gbias, sinks=sinks, ng=ng))
        h0, h0b, h0t = h3, h3b, h3t
        if l == 0:
            landed = _gather_finish(layer1, h3b, name="gather_layer1")
            full[1] = {n: _layer_full(s, n) for n, s in zip(BIG, landed)}

    G = {n: [None] * DEPTH for n in NAMES if n != "rel_bias"}
    dbias = [None] * DEPTH
    pending = []
    dz3, dz3b, G["ln3_g"][DEPTH - 1], G["ln3_b"][DEPTH - 1], loss_part = _grad_in(
        [], None, ln=(saved[-1]["z3"], ln3_g[DEPTH - 1].reshape(1, D)), loss=(h0, tgt), name="loss_head")
    for l in reversed(range(DEPTH)):
        sv, fw = saved[l], full[l]
        win = fw["w_in"]
        G["ffn_w_down"][l] = _wgrad(sv["act_t"], dz3b, name=f"d_w_down{l}")
        dupg, dupv, G["ffn_conv_w"][l], G["ffn_conv_b"][l] = _ffn_gate_bwd(
            dz3b, fw["ffn_w_down"], sv["h2b"], fw["ffn_w_up"], conv_w["ffn_conv_w"][l], ffn_conv_b[l].reshape(1, -1),
            name=f"ffn_gate_bwd{l}")
        G["ffn_w_up"][l] = jnp.concatenate([_wgrad(sv["h2t"], dupg, name=f"d_w_up_g{l}"),
                                            _wgrad(sv["h2t"], dupv, name=f"d_w_up_v{l}")], axis=1)
        dep = ()
        if l == 0:
            names = ("ffn_w_up", "ffn_w_down")
            pending.append((names, 0, "ffn0", _reduce_start("ffn0", names, [G[n][0] for n in names], me)))
            dep = (pending[-1][3][-1],)
        dz2, dz2b, G["ln2_g"][l], G["ln2_b"][l] = _grad_in(
            [(dupg, fw["ffn_w_up"], 0), (dupv, fw["ffn_w_up"], 1)], dz3, ln=(sv["z2"], ln2_g[l].reshape(1, D)), dep=dep,
            name=f"d_h2_{l}")
        G["xa_wo"][l] = _mm_tn(sv["o"], dz2b, name=f"d_xa_wo{l}")
        do = _mm(dz2b, fw["xa_wo"], trans_b=True, name=f"d_xa_o{l}")
        dq, dkv = _xattn_bwd(do, sv["q"], sv["kv"], name=f"xattn_bwd{l}")
        G["xa_wkv"][l] = _mm_tn(mems, dkv, name=f"d_xa_wkv{l}")
        G["xa_wq"][l] = _wgrad(sv["h1t"], dq, name=f"d_xa_wq{l}")
        dep = ()
        if l == 0:
            names = ("xa_wq", "xa_wkv", "xa_wo")
            pending.append((names, 0, "xa0", _reduce_start("xa0", names, [G[n][0] for n in names], me)))
            dep = (pending[-1][3][-1],)
        dz1, dz1b, G["ln1_g"][l], G["ln1_b"][l] = _grad_in(
            [(dq, fw["xa_wq"], 0)], dz2, ln=(sv["z1"], ln1_g[l].reshape(1, D)), dep=dep, name=f"d_h1_{l}")
        G["w_out"][l] = jnp.concatenate([_mm_tn(sv["h_ml"], dz1b, name=f"d_w_out_ml{l}"),
                                         _mm_tn(sv["h_sw"], dz1b, name=f"d_w_out_sw{l}")], axis=0)
        dhcat = _mm(dz1b, fw["w_out"], trans_b=True, name=f"d_hcat{l}")
        dsw, dbias[l], dsinks = _swa_bwd(dhcat, sv["proj"], bias, sv["sinks"], name=f"swa_bwd{l}")
        dqk, dml, dgb, dng = _mlstm_bwd(dhcat, sv["qk"], sv["proj"], sv["gbias"], sv["ng"], sv["cs"], sv["ns"], sv["ms"],
                                        name=f"mlstm_bwd{l}")
        dqk_pre, G["ml_conv_w"][l], G["ml_conv_b"][l] = _silu_conv_bwd(
            dqk, sv["proj"], conv_w["ml_conv_w"][l], ml_conv_b[l].reshape(1, -1), name=f"mlconv_bwd{l}")
        dw_qk = _wgrad(sv["h0t"], dqk_pre, name=f"d_w_in_qk{l}")
        dw_ml = _wgrad(sv["h0t"], dml, name=f"d_w_in_ml{l}")
        dw_sw = _wgrad(sv["h0t"], dsw, name=f"d_w_in_sw{l}")
        G["w_in"][l] = jnp.concatenate([dw_qk, dw_ml[:, :2 * ML_W + 2 * ML_H], dw_sw], axis=1)
        win_ml = jnp.concatenate([win[:, 1024:2048], win[:, 2816:2944]], axis=1)
        pairs = [(dqk_pre, win, 0), (dml, win_ml, 0), (dsw, win[:, 2048:2816], 0)]
        G["ml_i_bias"][l] = dgb[0, :ML_H]
        G["ml_f_bias"][l] = dgb[0, ML_H:2 * ML_H]
        G["ml_norm_g"][l] = dng
        G["swa_sinks"][l] = dsinks[0, :SW_H]
        if l > 0:
            pending.append((BIG, l, f"l{l}", _reduce_start(f"l{l}", BIG, [G[n][l] for n in BIG], me)))
            dz3, dz3b, G["ln3_g"][l - 1], G["ln3_b"][l - 1] = _grad_in(
                pairs, dz1, ln=(saved[l - 1]["z3"], ln3_g[l - 1].reshape(1, D)), dep=(pending[-1][3][-1],), name=f"d_h0_{l}")
        else:
            grad_x = _grad_in(pairs, dz1, name="d_h0_0").reshape(x.shape)

    names = ("w_in", "w_out")
    parts = {(n, 0): p for n, p in zip(names, _reduce_now("mix0", names, [G[n][0] for n in names], c_idx))}
    for names, l, tag, started in pending:
        landed = _split_wait(started, parts["w_in", 0], _scatter_copies, name=f"reduce_{tag}_wait")
        parts.update({(n, l): p for n, p in zip(names, landed)})
    res = {}
    for n in BIG:
        outs = None
        for l in reversed(range(DEPTH)):
            outs = _adamw(parts[n, l], W[n], Mo[n], Vo[n], layer=l, prev=outs, name=f"adamw_{n}{l}")
        for kind, a in zip(("g", "d", "m", "v"), outs):
            res[kind, n] = a

    Gf = {n: jnp.stack([g.reshape(W[n].shape[1:]) if n in REPLICATED else g for g in G[n]]) for n in G if n not in BIG}
    Gf["rel_bias"] = _bias_table_bwd(dbias, bucket, name="bias_table_bwd")[:, :SW_H]
    res_small, loss = _update_small(Gf, loss_part, W, Mo, Vo, me)
    res.update(res_small)
    return (loss, grad_x, *[res["g", n] for n in NAMES], *[res["d", n] for n in NAMES], *[res["m", n] for n in NAMES],
            *[res["v", n] for n in NAMES])
```

```python
import functools
import math

import jax
import jax.numpy as jnp
import numpy as np
from jax import lax
from jax.experimental import pallas as pl
from jax.experimental.pallas import tpu as pltpu

F32 = jnp.float32
BF16 = jnp.bfloat16

N_DEV = 8
N_CHIP = 4
D = 1024
DEPTH = 2
ML_H = 4
ML_W = 512
ML_DH = 128
ML_L = 64
ML_CONV = 4
SW_DH = 64
SW_W = 512
SW_H = 8
SW_G = 4
SW_KVW = 128
BLK = 128
REL_B = 32
REL_MAXD = 128
XA_H = 4
XA_DH = 256
DFF = 2816
NB_FF = DFF // 128
FFN_CONV = 3
ALPHA = (2.0 * DEPTH) ** 0.25
EPS = 1e-5
N_IN = 2824
NP_IN = 3072
ML_GW = 2 * ML_W + 128
SW_GW = SW_W + 2 * SW_KVW
ADAM_LR = 0.001
ADAM_B1 = 0.9
ADAM_B2 = 0.999
ADAM_EPS = 1e-08
ADAM_WD = 0.01
ADAM_STEP = 10
VMEM_LIMIT = 56 * 1024 * 1024
MESH = pl.DeviceIdType.MESH

NN = ((1,), (0,))
NT = ((1,), (1,))
TN = ((0,), (0,))


def _dg(a, b, dn):
    if a.ndim == 3:
        dims = (((dn[0][0] + 1,), (dn[1][0] + 1,)), ((0,), (0,)))
    else:
        dims = (dn, ((), ()))
    return lax.dot_general(a.astype(BF16), b.astype(BF16), dims, preferred_element_type=F32)


@jax.custom_vjp
def dot_nn(a, b):
    return _dg(a, b, NN)


dot_nn.defvjp(lambda a, b: (_dg(a, b, NN), (a, b)), lambda r, g: (_dg(g, r[1], NT), _dg(r[0], g, TN)))


@jax.custom_vjp
def dot_nt(a, b):
    return _dg(a, b, NT)


dot_nt.defvjp(lambda a, b: (_dg(a, b, NT), (a, b)), lambda r, g: (_dg(g, r[1], NN), _dg(g, r[0], TN)))


@jax.custom_vjp
def dot_tn(a, b):
    return _dg(a, b, TN)


dot_tn.defvjp(lambda a, b: (_dg(a, b, TN), (a, b)), lambda r, g: (_dg(r[1], g, NT), _dg(r[0], g, NN)))


def _params(sem=None):
    return pltpu.CompilerParams(dimension_semantics=sem, vmem_limit_bytes=VMEM_LIMIT)


def _sds(shape, dtype):
    return jax.ShapeDtypeStruct(tuple(shape), dtype)


TOKEN = pl.BlockSpec((8, 128), lambda *_: (0, 0))


def _mm(a, b, *, trans_b=False, out_dtype=F32, add=None, add_scale=1.0, tm=1024, tn=512, dep=(), name):
    a_list = list(a) if isinstance(a, (list, tuple)) else [a]
    M = a_list[0].shape[0]
    N = b.shape[0] if trans_b else b.shape[1]
    tm = min(tm, M)
    tn = next(t for t in (tn, 384, 256, 128) if N % t == 0)
    assert M % tm == 0
    Ka = a_list[0].shape[1]
    assert all(t.shape[1] == Ka for t in a_list)
    tk = next(t for t in (Ka, 1408, 1024) if Ka % t == 0 and t <= 1408)
    na, npa = len(a_list), Ka // tk
    nk = na * npa
    has_add = add is not None

    def body(*refs):
        a_refs, b_ref = refs[:na], refs[na]
        add_ref = refs[na + 1] if has_add else None
        o_ref, acc_ref = refs[-2], refs[-1]
        k = pl.program_id(2)

        def finish(r):
            if has_add:
                r = r + add_scale * add_ref[...].astype(F32)
            o_ref[...] = r.astype(out_dtype)

        for t, a_ref in enumerate(a_refs):
            def step(a_ref=a_ref):
                p = _dg(a_ref[...], b_ref[...], NT if trans_b else NN)
                if nk == 1:
                    finish(p)
                    return

                @pl.when(k == 0)
                def _():
                    acc_ref[...] = p

                @pl.when((k > 0) & (k < nk - 1))
                def _():
                    acc_ref[...] += p

                @pl.when(k == nk - 1)
                def _():
                    finish(acc_ref[...] + p)

            if na == 1:
                step()
            else:
                pl.when((k >= t * npa) & (k < (t + 1) * npa))(step)

    in_specs = [pl.BlockSpec((tm, tk), lambda i, j, k, t=t: (i, jnp.clip(k - t * npa, 0, npa - 1))) for t in range(na)]
    in_specs.append(pl.BlockSpec((tn, tk), lambda i, j, k: (j, k)) if trans_b else pl.BlockSpec((tk, tn), lambda i, j, k: (k, j)))
    args = a_list + [b]
    if has_add:
        in_specs.append(pl.BlockSpec((tm, tn), lambda i, j, k: (i, j)))
        args.append(add)
    in_specs += [TOKEN] * len(dep)
    args += list(dep)
    return pl.pallas_call(
        body, name=name, grid=(M // tm, N // tn, nk), in_specs=in_specs,
        out_specs=pl.BlockSpec((tm, tn), lambda i, j, k: (i, j)), out_shape=_sds((M, N), out_dtype),
        scratch_shapes=[pltpu.VMEM((tm, tn) if nk > 1 else (8, 128), F32)],
        compiler_params=_params(("parallel", "parallel", "arbitrary")))(*args)


def _wgrad(a_t, g, *, name):
    K, N = a_t.shape[0], g.shape[1]
    return _mm(a_t, g, tm=K if K <= 1024 else K // 2, tn=next(t for t in range(1536, 0, -128) if N % t == 0), name=name)


def _mm_tn(a, g, *, name):
    S, K = a.shape
    N = g.shape[1]
    tk = K if K <= 1024 else K // 2
    tn = next(t for t in range(1536, 0, -128) if N % t == 0)
    ts = min(512, S)
    ns = S // ts
    assert K % tk == 0 and S % ts == 0

    def body(a_ref, g_ref, o_ref):
        s = pl.program_id(2)
        p = _dg(a_ref[...], g_ref[...], TN)

        @pl.when(s == 0)
        def _():
            o_ref[...] = p

        @pl.when(s > 0)
        def _():
            o_ref[...] += p

    return pl.pallas_call(
        body, name=name, grid=(K // tk, N // tn, ns),
        in_specs=[pl.BlockSpec((ts, tk), lambda i, j, s: (s, i)), pl.BlockSpec((ts, tn), lambda i, j, s: (s, j))],
        out_specs=pl.BlockSpec((tk, tn), lambda i, j, s: (i, j)), out_shape=_sds((K, N), F32),
        compiler_params=_params(("parallel", "parallel", "arbitrary")))(a, g)


def _mm_res_ln(a_list, w, resid, gam, bet, *, name):
    M, Ka = a_list[0].shape
    na = len(a_list)
    assert w.shape[0] == na * Ka
    tm = min(256, M)

    def body(*refs):
        a_refs, w_refs = refs[:na], refs[na:2 * na]
        r_ref, g_ref, b_ref, y_ref, yb_ref, yt_ref, z_ref = refs[2 * na:]
        z = ALPHA * r_ref[...]
        for a_ref, w_ref in zip(a_refs, w_refs):
            z = z + _dg(a_ref[...], w_ref[...], NN)
        mu = jnp.mean(z, axis=1, keepdims=True)
        zc = z - mu
        var = jnp.mean(zc * zc, axis=1, keepdims=True)
        y = zc * lax.rsqrt(var + EPS) * g_ref[...] + b_ref[...]
        y_ref[...] = y
        yb_ref[...] = y.astype(BF16)
        yt_ref[...] = y.T.astype(BF16)
        z_ref[...] = z

    row = pl.BlockSpec((tm, D), lambda i: (i, 0))
    vec = pl.BlockSpec((1, D), lambda i: (0, 0))
    a_specs = [pl.BlockSpec((tm, Ka), lambda i: (i, 0)) for _ in a_list]
    w_specs = [pl.BlockSpec((Ka, D), lambda i, t=t: (t, 0)) for t in range(na)]
    return pl.pallas_call(
        body, name=name, grid=(M // tm,), in_specs=a_specs + w_specs + [row, vec, vec],
        out_specs=[row, row, pl.BlockSpec((D, tm), lambda i: (0, i)), row],
        out_shape=[_sds((M, D), F32), _sds((M, D), BF16), _sds((D, M), BF16), _sds((M, D), F32)],
        compiler_params=_params(("parallel",)))(*a_list, *([w] * na), resid, gam, bet)


def _grad_in(pairs, add, *, ln=None, loss=None, dep=(), name):
    M = (add if add is not None else loss[0]).shape[0]
    tm = min(256, M)
    npair, nd = len(pairs), len(dep)
    has_ln, has_loss = ln is not None, loss is not None

    def body(*refs):
        n_in = 2 * npair + (2 if has_loss else 1) + 2 * has_ln + nd
        ins, outs = refs[:n_in], refs[n_in:]
        i = pl.program_id(0)
        pos = 2 * npair
        if has_loss:
            e = ins[pos][...] - ins[pos + 1][...]
            pos += 2
            dy = e * (1.0 / D)
            part = 0.5 * jnp.sum(jnp.sum(e * e, axis=1, keepdims=True) * (1.0 / D), axis=0, keepdims=True)
        else:
            dy = ALPHA * ins[pos][...]
            pos += 1
            for t in range(npair):
                dy = dy + _dg(ins[2 * t][...], ins[2 * t + 1][...], NT)
        if not has_ln:
            outs[0][...] = dy
            return
        z, g_ref = ins[pos][...], ins[pos + 1]
        mu = jnp.mean(z, axis=1, keepdims=True)
        zc = z - mu
        var = jnp.mean(zc * zc, axis=1, keepdims=True)
        rstd = lax.rsqrt(var + EPS)
        xh = zc * rstd
        dxh = dy * g_ref[...]
        m1 = jnp.mean(dxh, axis=1, keepdims=True)
        m2 = jnp.mean(dxh * xh, axis=1, keepdims=True)
        dz = rstd * (dxh - m1 - xh * m2)
        outs[0][...] = dz
        outs[1][...] = dz.astype(BF16)
        acc = [(outs[2], jnp.sum(dy * xh, axis=0, keepdims=True)), (outs[3], jnp.sum(dy, axis=0, keepdims=True))]
        if has_loss:
            acc.append((outs[4], jnp.broadcast_to(part, (8, 128))))

        @pl.when(i == 0)
        def _():
            for ref, val in acc:
                ref[...] = val

        @pl.when(i > 0)
        def _():
            for ref, val in acc:
                ref[...] += val

    row = pl.BlockSpec((tm, D), lambda i: (i, 0))
    vec = pl.BlockSpec((1, D), lambda i: (0, 0))
    in_specs, args = [], []
    for a, b, blk in pairs:
        in_specs += [pl.BlockSpec((tm, a.shape[1]), lambda i: (i, 0)), pl.BlockSpec((D, a.shape[1]), lambda i, blk=blk: (0, blk))]
        args += [a, b]
    if has_loss:
        in_specs += [row, row]
        args += list(loss)
    else:
        in_specs.append(row)
        args.append(add)
    if has_ln:
        in_specs += [row, vec]
        args += list(ln)
    in_specs += [TOKEN] * nd
    args += list(dep)
    if has_ln:
        out_specs = [row, row, vec, vec] + ([pl.BlockSpec((8, 128), lambda i: (0, 0))] if has_loss else [])
        out_shape = [_sds((M, D), F32), _sds((M, D), BF16), _sds((1, D), F32), _sds((1, D), F32)] + ([_sds((8, 128), F32)] if has_loss else [])
    else:
        out_specs, out_shape = row, _sds((M, D), F32)
    return pl.pallas_call(
        body, name=name, grid=(M // tm,), in_specs=in_specs, out_specs=out_specs, out_shape=out_shape,
        compiler_params=_params(("arbitrary",) if has_ln else ("parallel",)))(*args)


def _shift_down(x, d):
    if d == 0:
        return x
    rows = lax.broadcasted_iota(jnp.int32, x.shape, 0)
    return jnp.where(rows >= d, pltpu.roll(x, d, 0), 0.0)


def _shift_up(x, d):
    if d == 0:
        return x
    S = x.shape[0]
    rows = lax.broadcasted_iota(jnp.int32, x.shape, 0)
    return jnp.where(rows < S - d, pltpu.roll(x, S - d, 0), 0.0)


def _conv(x, w_ref, b_ref, cs, K):
    y = b_ref[:, cs]
    for j in range(K):
        y = y + _shift_down(x, K - 1 - j) * w_ref[j:j + 1, cs]
    return y


def _conv_bwd(dy, x, w_ref, dw_ref, db_ref, cs, K):
    dx = jnp.zeros_like(x)
    for j in range(K):
        sdy = _shift_up(dy, K - 1 - j)
        dx = dx + sdy * w_ref[j:j + 1, cs]
        dw_ref[j:j + 1, cs] = jnp.sum(sdy * x, axis=0, keepdims=True)
    db_ref[:, cs] = jnp.sum(dy, axis=0, keepdims=True)
    return dx


ALL = slice(None)


def _silu_conv_fwd(proj, cw, cb, *, name):
    S = proj.shape[0]

    def body(x_ref, w_ref, b_ref, o_ref):
        o_ref[...] = jax.nn.silu(_conv(x_ref[...], w_ref, b_ref, ALL, ML_CONV))

    col = pl.BlockSpec((S, 128), lambda j: (0, j))
    return pl.pallas_call(
        body, name=name, grid=(8,),
        in_specs=[col, pl.BlockSpec((ML_CONV, 128), lambda j: (0, j)), pl.BlockSpec((1, 128), lambda j: (0, j))],
        out_specs=col, out_shape=_sds((S, 2 * ML_W), F32), compiler_params=_params(("parallel",)))(proj, cw, cb)


def _silu_conv_bwd(dqk, proj, cw, cb, *, name):
    S = proj.shape[0]

    def body(d_ref, x_ref, w_ref, b_ref, dx_ref, dw_ref, db_ref):
        x = x_ref[...]
        y = _conv(x, w_ref, b_ref, ALL, ML_CONV)
        dy = jax.vjp(jax.nn.silu, y)[1](d_ref[...])[0]
        dx_ref[...] = _conv_bwd(dy, x, w_ref, dw_ref, db_ref, ALL, ML_CONV).astype(BF16)

    col = pl.BlockSpec((S, 128), lambda j: (0, j))
    wsp = pl.BlockSpec((ML_CONV, 128), lambda j: (0, j))
    bsp = pl.BlockSpec((1, 128), lambda j: (0, j))
    return pl.pallas_call(
        body, name=name, grid=(8,), in_specs=[col, col, wsp, bsp], out_specs=[col, wsp, bsp],
        out_shape=[_sds((S, 2 * ML_W), BF16), _sds((ML_CONV, 2 * ML_W), F32), _sds((1, 2 * ML_W), F32)],
        compiler_params=_params(("parallel",)))(dqk, proj, cw, cb)


GELU_C0 = math.sqrt(2.0 / math.pi)
GELU_C1 = 0.044715


def _gate_bwd(ug, uv, da):
    t = jnp.tanh(GELU_C0 * (ug + GELU_C1 * (ug * ug * ug)))
    half = 0.5 * (1.0 + t)
    dgelu = half + 0.5 * ug * (1.0 - t * t) * (GELU_C0 * (1.0 + 3.0 * GELU_C1 * (ug * ug)))
    return da * uv * dgelu, da * (ug * half)


def _up_pair(h_ref, ugw_ref, uvw_ref):
    return _dg(h_ref[...], jnp.concatenate([ugw_ref[...], uvw_ref[...]], axis=1), NN)


def _ffn_specs(S):
    return (pl.BlockSpec((D, 128), lambda j: (0, j)), pl.BlockSpec((D, 128), lambda j: (0, j + NB_FF)),
            pl.BlockSpec((FFN_CONV, 128), lambda j: (0, j)), pl.BlockSpec((FFN_CONV, 128), lambda j: (0, j + NB_FF)),
            pl.BlockSpec((1, 128), lambda j: (0, j)), pl.BlockSpec((1, 128), lambda j: (0, j + NB_FF)))


def _ffn_gate_fwd(hb, w_up, cw, cb, *, name):
    S = hb.shape[0]
    ug_, uv_, wg, wv, bg, bv = _ffn_specs(S)

    def body(h_ref, ugw_ref, uvw_ref, wg_ref, wv_ref, bg_ref, bv_ref, o_ref, ot_ref, h_s):
        @pl.when(pl.program_id(0) == 0)
        def _():
            pltpu.sync_copy(h_ref, h_s)

        x2 = _up_pair(h_s, ugw_ref, uvw_ref)
        ug = _conv(x2[:, :128], wg_ref, bg_ref, ALL, FFN_CONV)
        uv = _conv(x2[:, 128:], wv_ref, bv_ref, ALL, FFN_CONV)
        act = jax.nn.gelu(ug) * uv
        o_ref[...] = act.astype(BF16)
        ot_ref[...] = act.T.astype(BF16)

    return pl.pallas_call(
        body, name=name, grid=(NB_FF,), in_specs=[ANY, ug_, uv_, wg, wv, bg, bv],
        out_specs=[pl.BlockSpec((S, 128), lambda j: (0, j)), pl.BlockSpec((128, S), lambda j: (j, 0))],
        out_shape=[_sds((S, DFF), BF16), _sds((DFF, S), BF16)],
        scratch_shapes=[pltpu.VMEM((S, D), BF16)],
        compiler_params=_params(("arbitrary",)))(hb, w_up, w_up, cw, cw, cb, cb)


def _ffn_gate_bwd(dzb, w_down, hb, w_up, cw, cb, *, name):
    S = hb.shape[0]
    ug_, uv_, wg, wv, bg, bv = _ffn_specs(S)

    def body(dz_ref, h_ref, wd_ref, ugw_ref, uvw_ref, wg_ref, wv_ref, bg_ref, bv_ref,
             dxg_ref, dxv_ref, dwg_ref, dwv_ref, dbg_ref, dbv_ref, dz_s, h_s):
        @pl.when(pl.program_id(0) == 0)
        def _():
            pltpu.sync_copy(dz_ref, dz_s)
            pltpu.sync_copy(h_ref, h_s)

        x2 = _up_pair(h_s, ugw_ref, uvw_ref)
        xg, xv = x2[:, :128], x2[:, 128:]
        ug = _conv(xg, wg_ref, bg_ref, ALL, FFN_CONV)
        uv = _conv(xv, wv_ref, bv_ref, ALL, FFN_CONV)
        dug, duv = _gate_bwd(ug, uv, _dg(dz_s[...], wd_ref[...], NT))
        dxg_ref[...] = _conv_bwd(dug, xg, wg_ref, dwg_ref, dbg_ref, ALL, FFN_CONV).astype(BF16)
        dxv_ref[...] = _conv_bwd(duv, xv, wv_ref, dwv_ref, dbv_ref, ALL, FFN_CONV).astype(BF16)

    col = pl.BlockSpec((S, 128), lambda j: (0, j))
    half = _sds((S, DFF), BF16)
    dxg, dxv, dwg, dwv, dbg, dbv = pl.pallas_call(
        body, name=name, grid=(NB_FF,),
        in_specs=[ANY, ANY, pl.BlockSpec((128, D), lambda j: (j, 0)), ug_, uv_, wg, wv, bg, bv],
        out_specs=[col, col, wg, wg, bg, bg],
        out_shape=[half, half, _sds((FFN_CONV, DFF), F32), _sds((FFN_CONV, DFF), F32), _sds((1, DFF), F32), _sds((1, DFF), F32)],
        scratch_shapes=[pltpu.VMEM((S, D), BF16), pltpu.VMEM((S, D), BF16)],
        compiler_params=_params(("arbitrary",)))(dzb, hb, w_down, w_up, w_up, cw, cw, cb, cb)
    return dxg, dxv, jnp.concatenate([dwg, dwv], axis=1), jnp.concatenate([dbg, dbv], axis=1)


def _log_sigmoid(x):
    return jnp.minimum(x, 0.0) - jnp.log1p(jnp.exp(-jnp.abs(x)))


@jax.custom_vjp
def _clamp_div(num, den, floor, shift):
    return num / jnp.maximum(jnp.abs(den), floor)


def _clamp_div_fwd(num, den, floor, shift):
    out = num / jnp.maximum(jnp.abs(den), floor)
    return out, (den, floor, out)


def _clamp_div_bwd(res, g):
    den, floor, out = res
    active = jnp.abs(den) < floor
    dinv = jnp.maximum(jnp.abs(den), floor)
    go = jnp.sum(g * out, axis=-1, keepdims=True)
    ddiv = -go / dinv
    return (g / dinv, jnp.where(active, 0.0, ddiv * jnp.sign(den)), jnp.where(active, ddiv, 0.0),
            jnp.sum(jnp.where(active, go, 0.0), axis=-2, keepdims=True))


_clamp_div.defvjp(_clamp_div_fwd, _clamp_div_bwd)


def _ml_heads(q, k, v, o_pre, gates, gbias, C, n, ng, m, shift):
    H, L, _ = q.shape
    lane1 = lax.broadcasted_iota(jnp.int32, (1, 128), 1)
    gz = gates + jnp.where(lane1 < ML_H, lax.stop_gradient(gbias), gbias)
    gz = jnp.broadcast_to(gz[None], (H, L, 128))
    hid = lax.broadcasted_iota(jnp.int32, (H, L, 128), 0)
    lane = lax.broadcasted_iota(jnp.int32, (H, L, 128), 2)
    ig = jnp.sum(jnp.where(lane == hid, gz, 0.0), axis=2, keepdims=True)
    lf = _log_sigmoid(jnp.sum(jnp.where(lane == ML_H + hid, gz, 0.0), axis=2, keepdims=True))
    r = lax.broadcasted_iota(jnp.int32, (H, L, L), 1)
    c = lax.broadcasted_iota(jnp.int32, (H, L, L), 2)
    eye, tril = r == c, r >= c

    def to_row(col):
        return jnp.sum(jnp.where(eye, col, 0.0), axis=1, keepdims=True)

    b_col = jnp.sum(jnp.where(tril, to_row(lf), 0.0), axis=2, keepdims=True)
    Dm = jnp.where(tril, b_col - to_row(b_col) + to_row(ig), -jnp.inf)
    inter = b_col + m
    m_t = jnp.maximum(inter, jnp.max(Dm, axis=2, keepdims=True))
    w_inter = jnp.exp(inter - m_t)
    ks = k * (ML_DH ** -0.5)
    s = dot_nt(q, ks) * jnp.exp(Dm - m_t)
    num = w_inter * dot_nn(q, C) + dot_nn(s, v)
    den = w_inter * jnp.sum(q * n, axis=2, keepdims=True) + jnp.sum(s, axis=2, keepdims=True)
    h = _clamp_div(num, den, jnp.exp(-m_t), shift)
    g = jnp.sum(lf, axis=1, keepdims=True)
    a = g - b_col + ig
    m_new = jnp.maximum(g + m, jnp.max(a, axis=1, keepdims=True))
    decay = jnp.exp(g + m - m_new)
    wk = jnp.exp(a - m_new)
    C_new = decay * C + dot_tn(ks * wk, v)
    n_new = decay * n + jnp.sum(wk * ks, axis=1, keepdims=True)
    mu = jnp.mean(h, axis=2, keepdims=True)
    hc = h - mu
    var = jnp.mean(hc * hc, axis=2, keepdims=True)
    out = jax.nn.sigmoid(o_pre) * (hc * lax.rsqrt(var + EPS) * ng)
    return out, C_new, n_new, m_new


def _hs(h, off=0):
    return slice(off + h * ML_DH, off + (h + 1) * ML_DH)


def _heads(ref, off=0):
    return jnp.stack([ref[:, _hs(h, off)] for h in range(ML_H)])


def _mlstm_fwd(qk, proj, gbias, ng, *, name):
    S = qk.shape[0]
    nc = S // ML_L

    def body(q_ref, k_ref, v_ref, o_ref, g_ref, gb_ref, ng_ref, h_ref, cs_ref, ns_ref, ms_ref, c_s, n_s, m_s):
        @pl.when(pl.program_id(0) == 0)
        def _():
            c_s[...] = jnp.zeros_like(c_s)
            n_s[...] = jnp.zeros_like(n_s)
            m_s[...] = jnp.zeros_like(m_s)

        C, n = c_s[...], n_s[...]
        cs_ref[0] = C
        ns_ref[0] = n
        ms_ref[0] = m_s[...]
        out, C2, n2, m2 = _ml_heads(_heads(q_ref), _heads(k_ref), _heads(v_ref), _heads(o_ref), g_ref[...], gb_ref[...], C, n,
                                    _heads(ng_ref), m_s[:, :, 0:1], jnp.zeros((ML_H, 1, 1), F32))
        for h in range(ML_H):
            h_ref[:, _hs(h)] = out[h].astype(BF16)
        c_s[...] = C2
        n_s[...] = n2
        m_s[...] = jnp.broadcast_to(m2, (ML_H, 1, 128))

    def w(j):
        return pl.BlockSpec((ML_L, ML_W), lambda c, j=j: (c, j))

    return pl.pallas_call(
        body, name=name, grid=(nc,),
        in_specs=[w(0), w(1), w(2), w(3), pl.BlockSpec((ML_L, 128), lambda c: (c, 22)),
                  pl.BlockSpec((1, 128), lambda c: (0, 0)), pl.BlockSpec((1, ML_W), lambda c: (0, 0))],
        out_specs=[w(0), pl.BlockSpec((1, ML_H, ML_DH, ML_DH), lambda c: (c, 0, 0, 0)),
                   pl.BlockSpec((1, ML_H, 1, 128), lambda c: (c, 0, 0, 0)), pl.BlockSpec((1, ML_H, 1, 128), lambda c: (c, 0, 0, 0))],
        out_shape=[_sds((S, ML_W), BF16), _sds((nc, ML_H, ML_DH, ML_DH), F32), _sds((nc, ML_H, 1, 128), F32),
                   _sds((nc, ML_H, 1, 128), F32)],
        scratch_shapes=[pltpu.VMEM((ML_H, ML_DH, ML_DH), F32), pltpu.VMEM((ML_H, 1, 128), F32), pltpu.VMEM((ML_H, 1, 128), F32)],
        compiler_params=_params(("arbitrary",)))(qk, qk, proj, proj, proj, gbias, ng)


def _mlstm_bwd(dh, qk, proj, gbias, ng, cs, ns, ms, *, name):
    S = qk.shape[0]
    nc = S // ML_L

    def body(dh_ref, q_ref, k_ref, v_ref, o_ref, g_ref, gb_ref, ng_ref, cs_ref, ns_ref, ms_ref,
             dqk_ref, dml_ref, dgb_ref, dng_ref, dc_s, dn_s, dm_s):
        @pl.when(pl.program_id(0) == 0)
        def _():
            dc_s[...] = jnp.zeros_like(dc_s)
            dn_s[...] = jnp.zeros_like(dn_s)
            dm_s[...] = jnp.zeros_like(dm_s)
            dgb_ref[...] = jnp.zeros_like(dgb_ref)
            dng_ref[...] = jnp.zeros_like(dng_ref)

        _, vjp = jax.vjp(_ml_heads, _heads(q_ref), _heads(k_ref), _heads(v_ref), _heads(o_ref), g_ref[...], gb_ref[...],
                         cs_ref[0], ns_ref[0], _heads(ng_ref), ms_ref[0][:, :, 0:1], jnp.zeros((ML_H, 1, 1), F32))
        dq, dk, dv, do, dgates, dgb, dC, dn, dng, dm, dshift = vjp((_heads(dh_ref), dc_s[...], dn_s[...], dm_s[:, :, 0:1]))
        lane1 = lax.broadcasted_iota(jnp.int32, (1, 128), 1)
        for h in range(ML_H):
            dqk_ref[:, _hs(h)] = dq[h]
            dqk_ref[:, _hs(h, ML_W)] = dk[h]
            dml_ref[:, _hs(h)] = dv[h].astype(BF16)
            dml_ref[:, _hs(h, ML_W)] = do[h].astype(BF16)
            dng_ref[:, _hs(h)] += dng[h]
            dgb = jnp.where(lane1 == h, dshift[h], dgb)
        dc_s[...] = dC
        dn_s[...] = dn
        dm_s[...] = jnp.broadcast_to(dm, (ML_H, 1, 128))
        dml_ref[:, 2 * ML_W:] = dgates.astype(BF16)
        dgb_ref[...] += dgb

    def w(j):
        return pl.BlockSpec((ML_L, ML_W), lambda c, j=j: (nc - 1 - c, j))

    vec = pl.BlockSpec((1, 128), lambda c: (0, 0))
    vecw = pl.BlockSpec((1, ML_W), lambda c: (0, 0))
    gsp = pl.BlockSpec((ML_L, 128), lambda c: (nc - 1 - c, 22))
    st = pl.BlockSpec((1, ML_H, 1, 128), lambda c: (nc - 1 - c, 0, 0, 0))
    return pl.pallas_call(
        body, name=name, grid=(nc,),
        in_specs=[w(0), w(0), w(1), w(2), w(3), gsp, vec, vecw,
                  pl.BlockSpec((1, ML_H, ML_DH, ML_DH), lambda c: (nc - 1 - c, 0, 0, 0)), st, st],
        out_specs=[pl.BlockSpec((ML_L, 2 * ML_W), lambda c: (nc - 1 - c, 0)), pl.BlockSpec((ML_L, ML_GW), lambda c: (nc - 1 - c, 0)),
                   vec, vecw],
        out_shape=[_sds((S, 2 * ML_W), F32), _sds((S, ML_GW), BF16), _sds((1, 128), F32), _sds((1, ML_W), F32)],
        scratch_shapes=[pltpu.VMEM((ML_H, ML_DH, ML_DH), F32), pltpu.VMEM((ML_H, 1, 128), F32), pltpu.VMEM((ML_H, 1, 128), F32)],
        compiler_params=_params(("arbitrary",)))(dh, qk, qk, proj, proj, proj, gbias, ng, cs, ns, ms)


def _t5_buckets():
    r = np.arange(BLK)[:, None]
    c = np.arange(2 * BLK)[None, :]
    n = np.maximum(BLK + r - c, 0)
    max_exact = REL_B // 2
    nf = np.maximum(n, 1).astype(np.float32)
    large = max_exact + (np.log(nf / np.float32(max_exact)) / np.float32(math.log(REL_MAXD / max_exact))
                         * np.float32(REL_B - max_exact)).astype(np.int32)
    large = np.minimum(large, REL_B - 1)
    return np.where(n < max_exact, n, large).astype(np.int32)


def _bias_table(rel_bias, bucket, *, name):
    def body(rb_ref, bk_ref, o_ref):
        bk = bk_ref[...]
        for h in range(SW_H):
            acc = jnp.zeros((BLK, 2 * BLK), F32)
            for b in range(REL_B):
                acc = jnp.where(bk == b, rb_ref[b, h], acc)
            o_ref[h] = acc

    return pl.pallas_call(
        body, name=name, in_specs=[pl.BlockSpec(memory_space=pltpu.SMEM), pl.BlockSpec(memory_space=pltpu.VMEM)],
        out_specs=pl.BlockSpec(memory_space=pltpu.VMEM), out_shape=_sds((SW_H, BLK, 2 * BLK), F32),
        compiler_params=_params())(rel_bias, bucket)


def _bias_table_bwd(dbias_list, bucket, *, name):
    nl = len(dbias_list)

    def body(*refs):
        d_refs, bk_ref, o_ref = refs[:nl], refs[nl], refs[nl + 1]
        bk = bk_ref[...]
        rows = lax.broadcasted_iota(jnp.int32, (REL_B, 128), 0)
        lanes = lax.broadcasted_iota(jnp.int32, (REL_B, 128), 1)
        acc = jnp.zeros((REL_B, 128), F32)
        for h in range(SW_H):
            d = d_refs[0][h]
            for d_ref in d_refs[1:]:
                d = d + d_ref[h]
            for b in range(REL_B):
                t = jnp.sum(jnp.sum(jnp.where(bk == b, d, 0.0), axis=0, keepdims=True), axis=1, keepdims=True)
                acc = jnp.where((rows == b) & (lanes == h), t, acc)
        o_ref[...] = acc

    vm = pl.BlockSpec(memory_space=pltpu.VMEM)
    return pl.pallas_call(
        body, name=name, in_specs=[vm] * (nl + 1), out_specs=vm, out_shape=_sds((REL_B, 128), F32),
        compiler_params=_params())(*dbias_list, bucket)


def _swa_heads(q, kp, kc, vp, vc, bp, bc, sinks, has_prev):
    def rep(t):
        return jnp.concatenate([t[g:g + 1] for g in range(SW_H // SW_G) for _ in range(SW_G)], axis=0)

    r = lax.broadcasted_iota(jnp.int32, (SW_H, BLK, BLK), 1)
    c = lax.broadcasted_iota(jnp.int32, (SW_H, BLK, BLK), 2)
    hid = lax.broadcasted_iota(jnp.int32, (SW_H, 1, 128), 0)
    lane = lax.broadcasted_iota(jnp.int32, (SW_H, 1, 128), 2)
    sink = jnp.sum(jnp.where(lane == hid, jnp.broadcast_to(sinks[None], (SW_H, 1, 128)), 0.0), axis=2, keepdims=True)
    lp = jnp.where((c > r) & has_prev, dot_nt(q, rep(kp)) * (SW_DH ** -0.5) + bp, -jnp.inf)
    lc = jnp.where(c <= r, dot_nt(q, rep(kc)) * (SW_DH ** -0.5) + bc, -jnp.inf)
    mx = jnp.maximum(jnp.maximum(jnp.max(lp, axis=2, keepdims=True), jnp.max(lc, axis=2, keepdims=True)), sink)
    mx = lax.stop_gradient(mx)
    pp, pc = jnp.exp(lp - mx), jnp.exp(lc - mx)
    den = jnp.sum(pp, axis=2, keepdims=True) + jnp.sum(pc, axis=2, keepdims=True) + jnp.exp(sink - mx)
    return dot_nn(pp / den, rep(vp)) + dot_nn(pc / den, rep(vc))


def _qs(h, off=0):
    return slice(off + h * SW_DH, off + (h + 1) * SW_DH)


def _split(ref, n):
    return jnp.stack([ref[:, _qs(h)] for h in range(n)])


def _swa_fwd(proj, bias, sinks, *, name):
    S = proj.shape[0]
    nb = S // BLK
    nkv = SW_H // SW_G

    def body(q_ref, kp_ref, kc_ref, vp_ref, vc_ref, b_ref, s_ref, o_ref):
        out = _swa_heads(_split(q_ref, SW_H), _split(kp_ref, nkv), _split(kc_ref, nkv), _split(vp_ref, nkv), _split(vc_ref, nkv),
                         b_ref[:, :, :BLK], b_ref[:, :, BLK:], s_ref[...], pl.program_id(0) > 0)
        for h in range(SW_H):
            o_ref[:, _qs(h)] = out[h].astype(BF16)

    def cur(j):
        return pl.BlockSpec((BLK, 128), lambda n, j=j: (n, j))

    def prev(j):
        return pl.BlockSpec((BLK, 128), lambda n, j=j: (jnp.maximum(n - 1, 0), j))

    return pl.pallas_call(
        body, name=name, grid=(nb,),
        in_specs=[pl.BlockSpec((BLK, SW_W), lambda n: (n, 4)), prev(20), cur(20), prev(21), cur(21),
                  pl.BlockSpec((SW_H, BLK, 2 * BLK), lambda n: (0, 0, 0)), pl.BlockSpec((1, 128), lambda n: (0, 0))],
        out_specs=pl.BlockSpec((BLK, SW_W), lambda n: (n, 0)), out_shape=_sds((S, SW_W), BF16),
        compiler_params=_params(("parallel",)))(proj, proj, proj, proj, proj, bias, sinks)


def _swa_bwd(dh, proj, bias, sinks, *, name):
    S = proj.shape[0]
    nb = S // BLK

    nkv = SW_H // SW_G

    def body(dh_ref, q_ref, kp_ref, kc_ref, vp_ref, vc_ref, b_ref, s_ref, dsw_ref, db_ref, ds_ref, ck_s, cv_s):
        i = pl.program_id(0)

        @pl.when(i == 0)
        def _():
            ck_s[...] = jnp.zeros_like(ck_s)
            cv_s[...] = jnp.zeros_like(cv_s)
            db_ref[...] = jnp.zeros_like(db_ref)
            ds_ref[...] = jnp.zeros_like(ds_ref)

        f = functools.partial(_swa_heads, has_prev=i < nb - 1)
        _, vjp = jax.vjp(f, _split(q_ref, SW_H), _split(kp_ref, nkv), _split(kc_ref, nkv), _split(vp_ref, nkv),
                         _split(vc_ref, nkv), b_ref[:, :, :BLK], b_ref[:, :, BLK:], s_ref[...])
        dq, dkp, dkc, dvp, dvc, dbp, dbc, ds = vjp(_split(dh_ref, SW_H))
        for h in range(SW_H):
            dsw_ref[:, _qs(h)] = dq[h].astype(BF16)
        for g in range(nkv):
            dsw_ref[:, _qs(g, SW_W)] = (dkc[g] + ck_s[:, _qs(g)]).astype(BF16)
            dsw_ref[:, _qs(g, SW_W + SW_KVW)] = (dvc[g] + cv_s[:, _qs(g)]).astype(BF16)
            ck_s[:, _qs(g)] = dkp[g]
            cv_s[:, _qs(g)] = dvp[g]
        db_ref[:, :, :BLK] += dbp
        db_ref[:, :, BLK:] += dbc
        ds_ref[...] += ds

    def cur(j):
        return pl.BlockSpec((BLK, 128), lambda i, j=j: (nb - 1 - i, j))

    def prev(j):
        return pl.BlockSpec((BLK, 128), lambda i, j=j: (jnp.maximum(nb - 2 - i, 0), j))

    bsp = pl.BlockSpec((SW_H, BLK, 2 * BLK), lambda i: (0, 0, 0))
    vec = pl.BlockSpec((1, 128), lambda i: (0, 0))
    return pl.pallas_call(
        body, name=name, grid=(nb,),
        in_specs=[pl.BlockSpec((BLK, SW_W), lambda i: (nb - 1 - i, 1)), pl.BlockSpec((BLK, SW_W), lambda i: (nb - 1 - i, 4)),
                  prev(20), cur(20), prev(21), cur(21), bsp, vec],
        out_specs=[pl.BlockSpec((BLK, SW_GW), lambda i: (nb - 1 - i, 0)), bsp, vec],
        out_shape=[_sds((S, SW_GW), BF16), _sds((SW_H, BLK, 2 * BLK), F32), _sds((1, 128), F32)],
        scratch_shapes=[pltpu.VMEM((BLK, 128), F32)] * 2,
        compiler_params=_params(("arbitrary",)))(dh, proj, proj, proj, proj, proj, bias, sinks)


XA_TM = 512


def _xa_head(qh, kh, vh):
    logits = dot_nt(qh, kh) * (XA_DH ** -0.5)
    mx = lax.stop_gradient(jnp.max(logits, axis=1, keepdims=True))
    e = jnp.exp(logits - mx)
    return dot_nn(e / jnp.sum(e, axis=1, keepdims=True), vh)


def _xs(h, off=0):
    return slice(off + h * XA_DH, off + (h + 1) * XA_DH)


def _xattn_fwd(q, kv, *, name):
    S = q.shape[0]
    M = kv.shape[0]

    def body(q_ref, kv_ref, o_ref):
        for h in range(XA_H):
            o_ref[:, _xs(h)] = _xa_head(q_ref[:, _xs(h)], kv_ref[:, _xs(h)], kv_ref[:, _xs(h, D)]).astype(BF16)

    tm = min(XA_TM, S)
    row = pl.BlockSpec((tm, D), lambda i: (i, 0))
    return pl.pallas_call(
        body, name=name, grid=(S // tm,), in_specs=[row, pl.BlockSpec((M, 2 * D), lambda i: (0, 0))], out_specs=row,
        out_shape=_sds((S, D), BF16), compiler_params=_params(("parallel",)))(q, kv)


def _xattn_bwd(do, q, kv, *, name):
    S = q.shape[0]
    M = kv.shape[0]

    def body(do_ref, q_ref, kv_ref, dq_ref, dkv_ref):
        @pl.when(pl.program_id(0) == 0)
        def _():
            dkv_ref[...] = jnp.zeros_like(dkv_ref)

        for h in range(XA_H):
            _, vjp = jax.vjp(_xa_head, q_ref[:, _xs(h)], kv_ref[:, _xs(h)], kv_ref[:, _xs(h, D)])
            dq, dk, dv = vjp(do_ref[:, _xs(h)])
            dq_ref[:, _xs(h)] = dq.astype(BF16)
            dkv_ref[:, _xs(h)] += dk
            dkv_ref[:, _xs(h, D)] += dv

    tm = min(XA_TM, S)
    row = pl.BlockSpec((tm, D), lambda i: (i, 0))
    full = pl.BlockSpec((M, 2 * D), lambda i: (0, 0))
    return pl.pallas_call(
        body, name=name, grid=(S // tm,), in_specs=[row, row, full], out_specs=[row, full],
        out_shape=[_sds((S, D), BF16), _sds((M, 2 * D), F32)], compiler_params=_params(("arbitrary",)))(do, q, kv)


ANY = pl.BlockSpec(memory_space=pl.ANY)


def _place():
    x, y, c = lax.axis_index("x"), lax.axis_index("y"), lax.axis_index("c")
    chips = [(1 - x, y), (x, 1 - y), (1 - x, 1 - y)]
    return x, y, c, chips


def _gather(arrs, *, name):
    n = len(arrs)

    def body(*refs):
        srcs, outs = refs[:n], refs[n:2 * n]
        send_sems, recv_sems, local_sems = refs[2 * n:]
        x, y, c, chips = _place()
        me, sib = (x, y, c), (x, y, 1 - c)

        def idx(p):
            return 4 * p[0] + 2 * p[1] + p[2]

        def copy(i, k, block, to, from_src=False):
            return pltpu.make_async_remote_copy(
                src_ref=srcs[i] if from_src else outs[i].at[idx(block)], dst_ref=outs[i].at[idx(block)],
                send_sem=send_sems.at[7 * i + k], recv_sem=recv_sems.at[7 * i + k], device_id=to, device_id_type=MESH)

        local = [pltpu.make_async_copy(srcs[i], outs[i].at[idx(me)], local_sems.at[i]) for i in range(n)]
        for cp in local:
            cp.start()
        first = []
        for i in range(n):
            first.append(copy(i, 0, me, sib, True))
            first += [copy(i, 1 + j, me, (*chip, c), True) for j, chip in enumerate(chips)]
        for cp in first:
            cp.start()
        passed = []
        for j, chip in enumerate(chips):
            for i in range(n):
                copy(i, 1 + j, (*chip, c), me).wait_recv()
                cp = copy(i, 4 + j, (*chip, c), sib)
                cp.start()
                passed.append(cp)
        for i in range(n):
            copy(i, 0, sib, me).wait_recv()
        for j, chip in enumerate(chips):
            for i in range(n):
                copy(i, 4 + j, (*chip, 1 - c), me).wait_recv()
        for cp in first + passed:
            cp.wait_send()
        for cp in local:
            cp.wait()

    return pl.pallas_call(
        body, name=name, in_specs=[ANY] * n, out_specs=[ANY] * n,
        out_shape=[_sds((N_DEV,) + a.shape, a.dtype) for a in arrs],
        scratch_shapes=[pltpu.SemaphoreType.DMA((7 * n,)), pltpu.SemaphoreType.DMA((7 * n,)), pltpu.SemaphoreType.DMA((n,))],
        compiler_params=pltpu.CompilerParams(has_side_effects=True))(*arrs)


def _swap_sibling(arrs, *, name):
    n = len(arrs)

    def body(*refs):
        srcs, outs = refs[:n], refs[n:2 * n]
        send_sems, recv_sems = refs[2 * n:]
        x, y, c, _ = _place()
        copies = [pltpu.make_async_remote_copy(
            src_ref=srcs[i].at[2 * j + (1 - c)], dst_ref=outs[i].at[j], send_sem=send_sems.at[N_CHIP * i + j],
            recv_sem=recv_sems.at[N_CHIP * i + j], device_id=(x, y, 1 - c), device_id_type=MESH)
            for i in range(n) for j in range(N_CHIP)]
        for cp in copies:
            cp.start()
        for cp in copies:
            cp.wait()

    return pl.pallas_call(
        body, name=name, in_specs=[ANY] * n, out_specs=[ANY] * n,
        out_shape=[_sds((N_CHIP,) + a.shape[1:], a.dtype) for a in arrs],
        scratch_shapes=[pltpu.SemaphoreType.DMA((N_CHIP * n,)), pltpu.SemaphoreType.DMA((N_CHIP * n,))],
        compiler_params=pltpu.CompilerParams(has_side_effects=True))(*arrs)


def _pair_sum(part, got, c_idx, *, name):
    _, R, C = part.shape
    tr = next(t for t in (R, 512, 256, 128, 64, 32, 16) if R % t == 0 and t * C <= 1024 * 1024)

    def body(c_ref, p_ref, g_ref, o_ref):
        o_ref[...] = (p_ref[0].astype(F32) + g_ref[...].astype(F32)).astype(o_ref.dtype)

    return pl.pallas_call(
        body, name=name,
        grid_spec=pltpu.PrefetchScalarGridSpec(
            num_scalar_prefetch=1, grid=(N_CHIP, R // tr),
            in_specs=[pl.BlockSpec((1, 1, tr, C), lambda j, r, c_ref: (j, c_ref[0], r, 0)),
                      pl.BlockSpec((1, tr, C), lambda j, r, c_ref: (j, r, 0))],
            out_specs=pl.BlockSpec((1, tr, C), lambda j, r, c_ref: (j, r, 0))),
        out_shape=_sds((N_CHIP, R, C), part.dtype),
        compiler_params=_params(("parallel", "parallel")))(c_idx, part.reshape(N_CHIP, 2, R, C), got)


def _swap_chips(arrs, *, name):
    n = len(arrs)

    def body(*refs):
        srcs, outs = refs[:n], refs[n:2 * n]
        send_sems, recv_sems, local_sems = refs[2 * n:]
        x, y, c, chips = _place()
        jme = 2 * x + y
        local = [pltpu.make_async_copy(srcs[i].at[jme], outs[i].at[jme], local_sems.at[i]) for i in range(n)]
        for cp in local:
            cp.start()
        copies = [pltpu.make_async_remote_copy(
            src_ref=srcs[i].at[2 * chip[0] + chip[1]], dst_ref=outs[i].at[jme], send_sem=send_sems.at[3 * i + j],
            recv_sem=recv_sems.at[3 * i + j], device_id=(*chip, c), device_id_type=MESH)
            for i in range(n) for j, chip in enumerate(chips)]
        for cp in copies:
            cp.start()
        for i in range(n):
            for j, chip in enumerate(chips):
                pltpu.make_async_remote_copy(
                    src_ref=srcs[i].at[jme], dst_ref=outs[i].at[2 * chip[0] + chip[1]], send_sem=send_sems.at[3 * i + j],
                    recv_sem=recv_sems.at[3 * i + j], device_id=(*chip, c), device_id_type=MESH).wait_recv()
        for cp in copies:
            cp.wait_send()
        for cp in local:
            cp.wait()

    return pl.pallas_call(
        body, name=name, in_specs=[ANY] * n, out_specs=[ANY] * n,
        out_shape=[_sds(a.shape, a.dtype) for a in arrs],
        scratch_shapes=[pltpu.SemaphoreType.DMA((3 * n,)), pltpu.SemaphoreType.DMA((3 * n,)), pltpu.SemaphoreType.DMA((n,))],
        compiler_params=pltpu.CompilerParams(has_side_effects=True))(*arrs)


HBM = pl.BlockSpec(memory_space=pltpu.HBM)
SEM = pl.BlockSpec(memory_space=pltpu.SEMAPHORE)
EFFECT = pltpu.SideEffectType.DATAFLOW_SIDE_EFFECTING


def _near_copies(srcs, lands, send_sems, recv_sems):
    x, y, c, chips = _place()
    me = 4 * x + 2 * y + c
    out = []
    for i in range(len(srcs)):
        for k, (px, py, pc) in enumerate([(x, y, 1 - c)] + [(*chip, c) for chip in chips]):
            out.append(tuple(pltpu.make_async_remote_copy(
                src_ref=srcs[i], dst_ref=lands[i].at[slot], send_sem=send_sems.at[4 * i + k], recv_sem=recv_sems.at[4 * i + k],
                device_id=(px, py, pc), device_id_type=MESH) for slot in (me, 4 * px + 2 * py + pc)))
    return out


def _forward_sibling(lands, *, name):
    n = len(lands)

    def body(*refs):
        bufs = refs[n:2 * n]
        send_sems, recv_sems = refs[2 * n:]
        x, y, c, chips = _place()
        copies = [tuple(pltpu.make_async_remote_copy(
            src_ref=bufs[i].at[4 * chip[0] + 2 * chip[1] + c], dst_ref=bufs[i].at[4 * chip[0] + 2 * chip[1] + cc],
            send_sem=send_sems.at[3 * i + j], recv_sem=recv_sems.at[3 * i + j], device_id=(x, y, 1 - c), device_id_type=MESH)
            for cc in (c, 1 - c)) for i in range(n) for j, chip in enumerate(chips)]
        for send, _ in copies:
            send.start()
        for send, recv in copies:
            send.wait_send()
            recv.wait_recv()

    return pl.pallas_call(
        body, name=name, in_specs=[ANY] * n, out_specs=[ANY] * n, out_shape=[_sds(a.shape, a.dtype) for a in lands],
        input_output_aliases={i: i for i in range(n)},
        scratch_shapes=[pltpu.SemaphoreType.DMA((3 * n,)), pltpu.SemaphoreType.DMA((3 * n,))],
        compiler_params=pltpu.CompilerParams(has_side_effects=True))(*lands)


def _scatter_copies(srcs, lands, send_sems, recv_sems):
    x, y, c, _ = _place()
    me = 4 * x + 2 * y + c
    out = []
    for i in range(len(srcs)):
        for j in range(1, N_DEV):
            px, py, pc = x ^ ((j >> 2) & 1), y ^ ((j >> 1) & 1), c ^ (j & 1)
            p = 4 * px + 2 * py + pc
            k = (N_DEV - 1) * i + j - 1
            out.append(tuple(pltpu.make_async_remote_copy(
                src_ref=srcs[i].at[s], dst_ref=lands[i].at[d], send_sem=send_sems.at[k], recv_sem=recv_sems.at[k],
                device_id=(px, py, pc), device_id_type=MESH) for s, d in ((p, me), (me, p))))
    return out


def _split_start(srcs, lands, pattern, ncopy, *, after=(), name):
    n = len(srcs)
    na = len(after)

    def body(*refs):
        sems = refs[2 * n + na:]
        for send, _ in pattern(refs[:n], refs[n:2 * n], sems[0], sems[1]):
            send.start()
        refs[-1][...] = jnp.zeros_like(refs[-1])

    arrs = list(srcs) + list(lands)
    return pl.pallas_call(
        body, name=name,
        out_shape=(pltpu.SemaphoreType.DMA((ncopy,)), pltpu.SemaphoreType.DMA((ncopy,)),
                   *[pltpu.HBM(a.shape, a.dtype) for a in arrs], _sds((8, 128), F32)),
        in_specs=[HBM] * (2 * n) + [ANY] * na, out_specs=(SEM, SEM, *[HBM] * (2 * n), pl.BlockSpec(memory_space=pltpu.VMEM)),
        input_output_aliases={i: 2 + i for i in range(2 * n)},
        compiler_params=pltpu.CompilerParams(has_side_effects=EFFECT))(
            *[pltpu.with_memory_space_constraint(a, pltpu.HBM) for a in arrs], *after)


def _split_wait(started, after, pattern, *, name):
    send_sems, recv_sems, *arrs = started[:-1]
    n = len(arrs) // 2

    def body(*refs):
        for send, recv in pattern(refs[:n], refs[n:2 * n], refs[2 * n], refs[2 * n + 1]):
            send.wait_send()
            recv.wait_recv()

    outs = pl.pallas_call(
        body, name=name, out_shape=tuple(pltpu.HBM(a.shape, a.dtype) for a in arrs),
        in_specs=[HBM] * (2 * n) + [SEM, SEM, ANY], out_specs=tuple([HBM] * (2 * n)),
        input_output_aliases={i: i for i in range(2 * n)},
        compiler_params=pltpu.CompilerParams(has_side_effects=EFFECT))(*arrs, send_sems, recv_sems, after)
    return list(outs[n:])


def _adamw(parts, w, m, v, *, layer=None, prev=None, name):
    P, R, C = parts.shape
    tr = next((t for t in (512, 256, 176, 128, 64, 32, 16, 8) if R % t == 0 and t * C <= 256 * 1024), R)
    c1 = 1.0 / (1.0 - ADAM_B1 ** ADAM_STEP)
    c2 = 1.0 / (1.0 - ADAM_B2 ** ADAM_STEP)
    nprev = 0 if prev is None else 4

    def body(p_ref, w_ref, m_ref, v_ref, *rest):
        g_ref, d_ref, nm_ref, nv_ref = rest[nprev:]
        g = p_ref[0].astype(F32)
        for j in range(1, P):
            g = g + p_ref[j].astype(F32)
        g = g.reshape(w_ref.shape)
        nm = ADAM_B1 * m_ref[...] + (1.0 - ADAM_B1) * g
        nv = ADAM_B2 * v_ref[...] + (1.0 - ADAM_B2) * (g * g)
        g_ref[...] = g
        nm_ref[...] = nm
        nv_ref[...] = nv
        d_ref[...] = -ADAM_LR * ((nm * c1) / (jnp.sqrt(nv * c2) + ADAM_EPS) + ADAM_WD * w_ref[...])

    if layer is None:
        row = pl.BlockSpec((tr, C), lambda i: (i, 0))
    else:
        row = pl.BlockSpec((1, tr, C), lambda i: (layer, i, 0))
    out = _sds(w.shape, F32)
    return pl.pallas_call(
        body, name=name, grid=(R // tr,),
        in_specs=[pl.BlockSpec((P, tr, C), lambda i: (0, i, 0)), row, row, row] + [ANY] * nprev,
        out_specs=[row, row, row, row], out_shape=[out, out, out, out],
        input_output_aliases={4 + k: k for k in range(nprev)},
        compiler_params=_params(("parallel",)))(parts, w, m, v, *(prev or ()))


BIG = ("w_in", "w_out", "xa_wq", "xa_wkv", "xa_wo", "ffn_w_up", "ffn_w_down")
COL_SHARDED = ("w_in", "xa_wkv", "ffn_w_up", "ml_conv_w", "ffn_conv_w")
SHARDED_SMALL = ("ml_conv_w", "ffn_conv_w")
REPLICATED = ("rel_bias", "ml_conv_b", "ml_i_bias", "ml_f_bias", "ml_norm_g", "swa_sinks", "ln1_g", "ln1_b", "ln2_g", "ln2_b",
              "ffn_conv_b", "ln3_g", "ln3_b")
NAMES = ("rel_bias", "w_in", "ml_conv_w", "ml_conv_b", "ml_i_bias", "ml_f_bias", "ml_norm_g", "swa_sinks", "w_out", "ln1_g", "ln1_b",
         "xa_wq", "xa_wkv", "xa_wo", "ln2_g", "ln2_b", "ffn_w_up", "ffn_conv_w", "ffn_conv_b", "ffn_w_down", "ln3_g", "ln3_b")


def _flat_rows(a, mult):
    f = a.reshape(-1)
    n = -(-f.shape[0] // (128 * mult)) * (128 * mult)
    if n != f.shape[0]:
        f = jnp.pad(f, (0, n - f.shape[0]))
    return f.reshape(-1, 128)


def _pack(arrs, mult):
    parts = [_flat_rows(a, mult) for a in arrs]
    return jnp.concatenate(parts, axis=0), [p.shape[0] for p in parts]


def _unpack(flat, rows, shapes):
    out, off = [], 0
    lead = flat.shape[:-2]
    for r, shp in zip(rows, shapes):
        n = int(np.prod(shp))
        piece = flat[..., off:off + r, :].reshape(lead + (r * 128,))[..., :n]
        out.append(piece.reshape(lead + tuple(shp)))
        off += r
    return out


def _full_from_shards(stacked, name):
    if name in COL_SHARDED:
        return jnp.moveaxis(stacked, 0, 2).reshape(stacked.shape[1], stacked.shape[2], N_DEV * stacked.shape[3])
    return jnp.moveaxis(stacked, 0, 1).reshape(stacked.shape[1], N_DEV * stacked.shape[2], stacked.shape[3])


def _shards_from_full(full, name):
    L, A, B = full.shape
    if name in COL_SHARDED:
        return jnp.moveaxis(full.reshape(L, A, N_DEV, B // N_DEV), 2, 0)
    return jnp.moveaxis(full.reshape(L, N_DEV, A // N_DEV, B), 1, 0)


def _pad_win(w):
    z = jnp.zeros(w.shape[:-1] + (NP_IN - N_IN,), w.dtype)
    return jnp.concatenate([w[..., :2048], w[..., 2056:], w[..., 2048:2056], z], axis=-1)


def _row128(v):
    return jnp.pad(v, (0, 128 - v.shape[0])).reshape(1, 128)


REST = BIG[1:]


def _layer_full(stacked, n):
    if n in COL_SHARDED:
        full = jnp.moveaxis(stacked, 0, 1).reshape(stacked.shape[1], N_DEV * stacked.shape[2])
    else:
        full = stacked.reshape(N_DEV * stacked.shape[1], stacked.shape[2])
    return _pad_win(full) if n == "w_in" else full


def _layer_shards(g, n):
    A, B = g.shape
    if n in COL_SHARDED:
        return jnp.moveaxis(g.reshape(A, N_DEV, B // N_DEV), 1, 0).astype(BF16)
    return g.reshape(N_DEV, A // N_DEV, B).astype(BF16)


def _with_own_block(a, idx, nblk):
    return lax.dynamic_update_slice(lax.empty((nblk,) + a.shape, a.dtype), a[None], (idx,) + (0,) * a.ndim)


def _gather_start(arrs, me, *, after=(), name):
    return _split_start(arrs, [_with_own_block(a, me, N_DEV) for a in arrs], _near_copies, 4 * len(arrs), after=after, name=name)


def _gather_finish(started, after, *, name):
    return _forward_sibling(_split_wait(started, after, _near_copies, name=name + "_wait"), name=name + "_forward")


def _reduce_now(tag, names, grads, c_idx):
    send = [_layer_shards(g, n) for n, g in zip(names, grads)]
    got = _swap_sibling(send, name=f"swap_sibling_{tag}")
    sums = [_pair_sum(s, g, c_idx, name=f"pair_sum_{tag}_{n}") for n, s, g in zip(names, send, got)]
    return _swap_chips(sums, name=f"swap_chips_{tag}")


def _reduce_start(tag, names, grads, me):
    send = [_layer_shards(g, n) for n, g in zip(names, grads)]
    lands = [_with_own_block(lax.dynamic_index_in_dim(s, me, 0, keepdims=False), me, N_DEV) for s in send]
    return _split_start(send, lands, _scatter_copies, (N_DEV - 1) * len(send), name=f"reduce_{tag}_start")


def _update_small(Gf, loss_part, W, Mo, Vo, me):
    res = {}
    small = REPLICATED + SHARDED_SMALL
    sp_flat, sp_rows = _pack([Gf[n] for n in small] + [loss_part], 8)
    sp_all = _gather([sp_flat], name="gather_small_grads")[0]

    def widen(n, t):
        if n not in SHARDED_SMALL:
            return t[n]
        return lax.dynamic_update_slice(jnp.zeros(Gf[n].shape, F32), t[n], (0, 0, me * t[n].shape[2]))

    zl = jnp.zeros((8, 128), F32)
    wsm, _ = _pack([widen(n, W) for n in small] + [zl], 8)
    msm, _ = _pack([widen(n, Mo) for n in small] + [zl], 8)
    vsm, _ = _pack([widen(n, Vo) for n in small] + [zl], 8)
    outs_small = [_unpack(o_, sp_rows, [Gf[n].shape for n in small] + [(8, 128)])
                  for o_ in _adamw(sp_all, wsm, msm, vsm, name="adamw_small")]
    for kind, os_ in zip(("g", "d", "m", "v"), outs_small):
        for n, a in zip(small, os_[:-1]):
            if n in SHARDED_SMALL:
                a = lax.dynamic_slice(a, (0, 0, me * W[n].shape[2]), W[n].shape)
            res[kind, n] = a
    return res, outs_small[0][-1][0, 0]


def kernel(x, mem, rel_bias, w_in, ml_conv_w, ml_conv_b, ml_i_bias, ml_f_bias, ml_norm_g, swa_sinks, w_out, ln1_g, ln1_b, xa_wq, xa_wkv, xa_wo, ln2_g, ln2_b, ffn_w_up, ffn_conv_w, ffn_conv_b, ffn_w_down, ln3_g, ln3_b, loss_target, m_rel_bias, m_w_in, m_ml_conv_w, m_ml_conv_b, m_ml_i_bias, m_ml_f_bias, m_ml_norm_g, m_swa_sinks, m_w_out, m_ln1_g, m_ln1_b, m_xa_wq, m_xa_wkv, m_xa_wo, m_ln2_g, m_ln2_b, m_ffn_w_up, m_ffn_conv_w, m_ffn_conv_b, m_ffn_w_down, m_ln3_g, m_ln3_b, v_rel_bias, v_w_in, v_ml_conv_w, v_ml_conv_b, v_ml_i_bias, v_ml_f_bias, v_ml_norm_g, v_swa_sinks, v_w_out, v_ln1_g, v_ln1_b, v_xa_wq, v_xa_wkv, v_xa_wo, v_ln2_g, v_ln2_b, v_ffn_w_up, v_ffn_conv_w, v_ffn_conv_b, v_ffn_w_down, v_ln3_g, v_ln3_b):
    W = dict(rel_bias=rel_bias, w_in=w_in, ml_conv_w=ml_conv_w, ml_conv_b=ml_conv_b, ml_i_bias=ml_i_bias, ml_f_bias=ml_f_bias,
             ml_norm_g=ml_norm_g, swa_sinks=swa_sinks, w_out=w_out, ln1_g=ln1_g, ln1_b=ln1_b, xa_wq=xa_wq, xa_wkv=xa_wkv,
             xa_wo=xa_wo, ln2_g=ln2_g, ln2_b=ln2_b, ffn_w_up=ffn_w_up, ffn_conv_w=ffn_conv_w, ffn_conv_b=ffn_conv_b,
             ffn_w_down=ffn_w_down, ln3_g=ln3_g, ln3_b=ln3_b)
    Mo = dict(rel_bias=m_rel_bias, w_in=m_w_in, ml_conv_w=m_ml_conv_w, ml_conv_b=m_ml_conv_b, ml_i_bias=m_ml_i_bias,
              ml_f_bias=m_ml_f_bias, ml_norm_g=m_ml_norm_g, swa_sinks=m_swa_sinks, w_out=m_w_out, ln1_g=m_ln1_g, ln1_b=m_ln1_b,
              xa_wq=m_xa_wq, xa_wkv=m_xa_wkv, xa_wo=m_xa_wo, ln2_g=m_ln2_g, ln2_b=m_ln2_b, ffn_w_up=m_ffn_w_up,
              ffn_conv_w=m_ffn_conv_w, ffn_conv_b=m_ffn_conv_b, ffn_w_down=m_ffn_w_down, ln3_g=m_ln3_g, ln3_b=m_ln3_b)
    Vo = dict(rel_bias=v_rel_bias, w_in=v_w_in, ml_conv_w=v_ml_conv_w, ml_conv_b=v_ml_conv_b, ml_i_bias=v_ml_i_bias,
              ml_f_bias=v_ml_f_bias, ml_norm_g=v_ml_norm_g, swa_sinks=v_swa_sinks, w_out=v_w_out, ln1_g=v_ln1_g, ln1_b=v_ln1_b,
              xa_wq=v_xa_wq, xa_wkv=v_xa_wkv, xa_wo=v_xa_wo, ln2_g=v_ln2_g, ln2_b=v_ln2_b, ffn_w_up=v_ffn_w_up,
              ffn_conv_w=v_ffn_conv_w, ffn_conv_b=v_ffn_conv_b, ffn_w_down=v_ffn_w_down, ln3_g=v_ln3_g, ln3_b=v_ln3_b)
    S = x.shape[1]
    c_me = lax.axis_index("c")
    me = 4 * lax.axis_index("x") + 2 * lax.axis_index("y") + c_me
    c_idx = jnp.reshape(c_me, (1,)).astype(jnp.int32)
    xs = x.reshape(S, D)
    mems = mem.reshape(mem.shape[1], D)
    tgt = loss_target.reshape(S, D)

    jme = 2 * lax.axis_index("x") + lax.axis_index("y")

    sm_flat, sm_rows = _pack([W[n] for n in SHARDED_SMALL], 8)
    first = _gather([w_in[0].astype(BF16), sm_flat], name="gather_first")
    rest0 = _gather_start([W[n][0].astype(BF16) for n in REST], me, name="gather_rest0_start")
    full = [{"w_in": _layer_full(first[0], "w_in")}, None]
    conv_w = {n: _full_from_shards(s, n) for n, s in zip(SHARDED_SMALL, _unpack(first[1], sm_rows, [W[n].shape for n in SHARDED_SMALL]))}

    bucket = jnp.asarray(_t5_buckets())
    bias = _bias_table(rel_bias, bucket, name="bias_table")

    saved = []
    h0, h0b = xs, xs.astype(BF16)
    h0t = h0b.T
    for l in range(DEPTH):
        gbias = _row128(jnp.concatenate([ml_i_bias[l], ml_f_bias[l]]))
        sinks = _row128(swa_sinks[l])
        ng = ml_norm_g[l].reshape(1, ML_W)
        proj = _mm(h0b, full[l]["w_in"], dep=(rest0[-1],) if l == 0 else (), name=f"proj{l}")
        qk = _silu_conv_fwd(proj, conv_w["ml_conv_w"][l], ml_conv_b[l].reshape(1, -1), name=f"mlconv{l}")
        h_ml, cs, ns, ms = _mlstm_fwd(qk, proj, gbias, ng, name=f"mlstm{l}")
        h_sw = _swa_fwd(proj, bias, sinks, name=f"swa{l}")
        dep = ()
        if l == 0:
            landed = _gather_finish(rest0, h_sw, name="gather_rest0")
            full[0].update({n: _layer_full(s, n) for n, s in zip(REST, landed)})
            layer1 = _gather_start([W[n][1].astype(BF16) for n in BIG], me, after=(landed[0],), name="gather_layer1_start")
            dep = (layer1[-1],)
        fw = full[l]
        h1, h1b, h1t, z1 = _mm_res_ln([h_ml, h_sw], fw["w_out"], h0, ln1_g[l].reshape(1, D), ln1_b[l].reshape(1, D),
                                      name=f"mix_out{l}")
        q = _mm(h1b, fw["xa_wq"], dep=dep, name=f"xa_q{l}")
        kv = _mm(mems, fw["xa_wkv"], tm=256, name=f"xa_kv{l}")
        o = _xattn_fwd(q, kv, name=f"xattn{l}")
        h2, h2b, h2t, z2 = _mm_res_ln([o], fw["xa_wo"], h1, ln2_g[l].reshape(1, D), ln2_b[l].reshape(1, D), name=f"xa_out{l}")
        act, act_t = _ffn_gate_fwd(h2b, fw["ffn_w_up"], conv_w["ffn_conv_w"][l], ffn_conv_b[l].reshape(1, -1), name=f"ffn_gate{l}")
        h3, h3b, h3t, z3 = _mm_res_ln([act], fw["ffn_w_down"], h2, ln3_g[l].reshape(1, D), ln3_b[l].reshape(1, D),
                                      name=f"ffn_out{l}")
        saved.append(dict(h0t=h0t, proj=proj, qk=qk, cs=cs, ns=ns, ms=ms, h_ml=h_ml, h_sw=h_sw, z1=z1, h1t=h1t, q=q, kv=kv, o=o,
                          z2=z2, h2b=h2b, h2t=h2t, act_t=act_t, z3=z3, gbias=gbias, sinks=sinks, ng=ng))
        h0, h0b, h0t = h3, h3b, h3t
        if l == 0:
            landed = _gather_finish(layer1, h3b, name="gather_layer1")
            full[1] = {n: _layer_full(s, n) for n, s in zip(BIG, landed)}

    G = {n: [None] * DEPTH for n in NAMES if n != "rel_bias"}
    dbias = [None] * DEPTH
    pending = []
    dz3, dz3b, G["ln3_g"][DEPTH - 1], G["ln3_b"][DEPTH - 1], loss_part = _grad_in(
        [], None, ln=(saved[-1]["z3"], ln3_g[DEPTH - 1].reshape(1, D)), loss=(h0, tgt), name="loss_head")
    for l in reversed(range(DEPTH)):
        sv, fw = saved[l], full[l]
        win = fw["w_in"]
        G["ffn_w_down"][l] = _wgrad(sv["act_t"], dz3b, name=f"d_w_down{l}")
        dupg, dupv, G["ffn_conv_w"][l], G["ffn_conv_b"][l] = _ffn_gate_bwd(
            dz3b, fw["ffn_w_down"], sv["h2b"], fw["ffn_w_up"], conv_w["ffn_conv_w"][l], ffn_conv_b[l].reshape(1, -1),
            name=f"ffn_gate_bwd{l}")
        G["ffn_w_up"][l] = jnp.concatenate([_wgrad(sv["h2t"], dupg, name=f"d_w_up_g{l}"),
                                            _wgrad(sv["h2t"], dupv, name=f"d_w_up_v{l}")], axis=1)
        dep = ()
        if l == 0:
            names = ("ffn_w_up", "ffn_w_down")
            pending.append((names, 0, "ffn0", _reduce_start("ffn0", names, [G[n][0] for n in names], me)))
            dep = (pending[-1][3][-1],)
        dz2, dz2b, G["ln2_g"][l], G["ln2_b"][l] = _grad_in(
            [(dupg, fw["ffn_w_up"], 0), (dupv, fw["ffn_w_up"], 1)], dz3, ln=(sv["z2"], ln2_g[l].reshape(1, D)), dep=dep,
            name=f"d_h2_{l}")
        G["xa_wo"][l] = _mm_tn(sv["o"], dz2b, name=f"d_xa_wo{l}")
        do = _mm(dz2b, fw["xa_wo"], trans_b=True, name=f"d_xa_o{l}")
        dq, dkv = _xattn_bwd(do, sv["q"], sv["kv"], name=f"xattn_bwd{l}")
        G["xa_wkv"][l] = _mm_tn(mems, dkv, name=f"d_xa_wkv{l}")
        G["xa_wq"][l] = _wgrad(sv["h1t"], dq, name=f"d_xa_wq{l}")
        dep = ()
        if l == 0:
            names = ("xa_wq", "xa_wkv", "xa_wo")
            pending.append((names, 0, "xa0", _reduce_start("xa0", names, [G[n][0] for n in names], me)))
            dep = (pending[-1][3][-1],)
        dz1, dz1b, G["ln1_g"][l], G["ln1_b"][l] = _grad_in(
            [(dq, fw["xa_wq"], 0)], dz2, ln=(sv["z1"], ln1_g[l].reshape(1, D)), dep=dep, name=f"d_h1_{l}")
        G["w_out"][l] = jnp.concatenate([_mm_tn(sv["h_ml"], dz1b, name=f"d_w_out_ml{l}"),
                                         _mm_tn(sv["h_sw"], dz1b, name=f"d_w_out_sw{l}")], axis=0)
        dep = ()
        if l == 0:
            pending.append((("w_out",), 0, "out0", _reduce_start("out0", ("w_out",), [G["w_out"][0]], me)))
            dep = (pending[-1][3][-1],)
        dhcat = _mm(dz1b, fw["w_out"], trans_b=True, dep=dep, name=f"d_hcat{l}")
        dsw, dbias[l], dsinks = _swa_bwd(dhcat, sv["proj"], bias, sv["sinks"], name=f"swa_bwd{l}")
        dqk, dml, dgb, dng = _mlstm_bwd(dhcat, sv["qk"], sv["proj"], sv["gbias"], sv["ng"], sv["cs"], sv["ns"], sv["ms"],
                                        name=f"mlstm_bwd{l}")
        dqk_pre, G["ml_conv_w"][l], G["ml_conv_b"][l] = _silu_conv_bwd(
            dqk, sv["proj"], conv_w["ml_conv_w"][l], ml_conv_b[l].reshape(1, -1), name=f"mlconv_bwd{l}")
        dw_qk = _wgrad(sv["h0t"], dqk_pre, name=f"d_w_in_qk{l}")
        dw_ml = _wgrad(sv["h0t"], dml, name=f"d_w_in_ml{l}")
        dw_sw = _wgrad(sv["h0t"], dsw, name=f"d_w_in_sw{l}")
        G["w_in"][l] = jnp.concatenate([dw_qk, dw_ml[:, :2 * ML_W + 2 * ML_H], dw_sw], axis=1)
        win_ml = jnp.concatenate([win[:, 1024:2048], win[:, 2816:2944]], axis=1)
        pairs = [(dqk_pre, win, 0), (dml, win_ml, 0), (dsw, win[:, 2048:2816], 0)]
        G["ml_i_bias"][l] = dgb[0, :ML_H]
        G["ml_f_bias"][l] = dgb[0, ML_H:2 * ML_H]
        G["ml_norm_g"][l] = dng
        G["swa_sinks"][l] = dsinks[0, :SW_H]
        if l > 0:
            pending.append((BIG, l, f"l{l}", _reduce_start(f"l{l}", BIG, [G[n][l] for n in BIG], me)))
            dz3, dz3b, G["ln3_g"][l - 1], G["ln3_b"][l - 1] = _grad_in(
                pairs, dz1, ln=(saved[l - 1]["z3"], ln3_g[l - 1].reshape(1, D)), dep=(pending[-1][3][-1],), name=f"d_h0_{l}")
        else:
            grad_x = _grad_in(pairs, dz1, name="d_h0_0").reshape(x.shape)

    names = ("w_in",)
    parts = {(n, 0): p for n, p in zip(names, _reduce_now("in0", names, [G[n][0] for n in names], c_idx))}
    for names, l, tag, started in pending:
        landed = _split_wait(started, parts["w_in", 0], _scatter_copies, name=f"reduce_{tag}_wait")
        parts.update({(n, l): p for n, p in zip(names, landed)})
    res = {}
    for n in BIG:
        outs = None
        for l in reversed(range(DEPTH)):
            outs = _adamw(parts[n, l], W[n], Mo[n], Vo[n], layer=l, prev=outs, name=f"adamw_{n}{l}")
        for kind, a in zip(("g", "d", "m", "v"), outs):
            res[kind, n] = a

    Gf = {n: jnp.stack([g.reshape(W[n].shape[1:]) if n in REPLICATED else g for g in G[n]]) for n in G if n not in BIG}
    Gf["rel_bias"] = _bias_table_bwd(dbias, bucket, name="bias_table_bwd")[:, :SW_H]
    res_small, loss = _update_small(Gf, loss_part, W, Mo, Vo, me)
    res.update(res_small)
    return (loss, grad_x, *[res["g", n] for n in NAMES], *[res["d", n] for n in NAMES], *[res["m", n] for n in NAMES],
            *[res["v", n] for n in NAMES])
```

```python
import functools
import math

import jax
import jax.numpy as jnp
import numpy as np
from jax import lax
from jax.experimental import pallas as pl
from jax.experimental.pallas import tpu as pltpu

F32 = jnp.float32
BF16 = jnp.bfloat16

N_DEV = 8
N_CHIP = 4
D = 1024
DEPTH = 2
ML_H = 4
ML_W = 512
ML_DH = 128
ML_L = 64
ML_CONV = 4
SW_DH = 64
SW_W = 512
SW_H = 8
SW_G = 4
SW_KVW = 128
BLK = 128
REL_B = 32
REL_MAXD = 128
XA_H = 4
XA_DH = 256
DFF = 2816
NB_FF = DFF // 128
FFN_CONV = 3
ALPHA = (2.0 * DEPTH) ** 0.25
EPS = 1e-5
N_IN = 2824
NP_IN = 3072
ML_GW = 2 * ML_W + 128
SW_GW = SW_W + 2 * SW_KVW
ADAM_LR = 0.001
ADAM_B1 = 0.9
ADAM_B2 = 0.999
ADAM_EPS = 1e-08
ADAM_WD = 0.01
ADAM_STEP = 10
VMEM_LIMIT = 56 * 1024 * 1024
MESH = pl.DeviceIdType.MESH

NN = ((1,), (0,))
NT = ((1,), (1,))
TN = ((0,), (0,))


def _dg(a, b, dn):
    if a.ndim == 3:
        dims = (((dn[0][0] + 1,), (dn[1][0] + 1,)), ((0,), (0,)))
    else:
        dims = (dn, ((), ()))
    return lax.dot_general(a.astype(BF16), b.astype(BF16), dims, preferred_element_type=F32)


@jax.custom_vjp
def dot_nn(a, b):
    return _dg(a, b, NN)


dot_nn.defvjp(lambda a, b: (_dg(a, b, NN), (a, b)), lambda r, g: (_dg(g, r[1], NT), _dg(r[0], g, TN)))


@jax.custom_vjp
def dot_nt(a, b):
    return _dg(a, b, NT)


dot_nt.defvjp(lambda a, b: (_dg(a, b, NT), (a, b)), lambda r, g: (_dg(g, r[1], NN), _dg(g, r[0], TN)))


@jax.custom_vjp
def dot_tn(a, b):
    return _dg(a, b, TN)


dot_tn.defvjp(lambda a, b: (_dg(a, b, TN), (a, b)), lambda r, g: (_dg(r[1], g, NT), _dg(r[0], g, NN)))


def _params(sem=None):
    return pltpu.CompilerParams(dimension_semantics=sem, vmem_limit_bytes=VMEM_LIMIT)


def _sds(shape, dtype):
    return jax.ShapeDtypeStruct(tuple(shape), dtype)


TOKEN = pl.BlockSpec((8, 128), lambda *_: (0, 0))


def _mm(a, b, *, trans_b=False, out_dtype=F32, add=None, add_scale=1.0, tm=1024, tn=512, dep=(), name):
    a_list = list(a) if isinstance(a, (list, tuple)) else [a]
    M = a_list[0].shape[0]
    N = b.shape[0] if trans_b else b.shape[1]
    tm = min(tm, M)
    tn = next(t for t in (tn, 384, 256, 128) if N % t == 0)
    assert M % tm == 0
    Ka = a_list[0].shape[1]
    assert all(t.shape[1] == Ka for t in a_list)
    tk = next(t for t in (Ka, 1408, 1024) if Ka % t == 0 and t <= 1408)
    na, npa = len(a_list), Ka // tk
    nk = na * npa
    has_add = add is not None

    def body(*refs):
        a_refs, b_ref = refs[:na], refs[na]
        add_ref = refs[na + 1] if has_add else None
        o_ref, acc_ref = refs[-2], refs[-1]
        k = pl.program_id(2)

        def finish(r):
            if has_add:
                r = r + add_scale * add_ref[...].astype(F32)
            o_ref[...] = r.astype(out_dtype)

        for t, a_ref in enumerate(a_refs):
            def step(a_ref=a_ref):
                p = _dg(a_ref[...], b_ref[...], NT if trans_b else NN)
                if nk == 1:
                    finish(p)
                    return

                @pl.when(k == 0)
                def _():
                    acc_ref[...] = p

                @pl.when((k > 0) & (k < nk - 1))
                def _():
                    acc_ref[...] += p

                @pl.when(k == nk - 1)
                def _():
                    finish(acc_ref[...] + p)

            if na == 1:
                step()
            else:
                pl.when((k >= t * npa) & (k < (t + 1) * npa))(step)

    in_specs = [pl.BlockSpec((tm, tk), lambda i, j, k, t=t: (i, jnp.clip(k - t * npa, 0, npa - 1))) for t in range(na)]
    in_specs.append(pl.BlockSpec((tn, tk), lambda i, j, k: (j, k)) if trans_b else pl.BlockSpec((tk, tn), lambda i, j, k: (k, j)))
    args = a_list + [b]
    if has_add:
        in_specs.append(pl.BlockSpec((tm, tn), lambda i, j, k: (i, j)))
        args.append(add)
    in_specs += [TOKEN] * len(dep)
    args += list(dep)
    return pl.pallas_call(
        body, name=name, grid=(M // tm, N // tn, nk), in_specs=in_specs,
        out_specs=pl.BlockSpec((tm, tn), lambda i, j, k: (i, j)), out_shape=_sds((M, N), out_dtype),
        scratch_shapes=[pltpu.VMEM((tm, tn) if nk > 1 else (8, 128), F32)],
        compiler_params=_params(("parallel", "parallel", "arbitrary")))(*args)


def _wgrad(a_t, g, *, name):
    K, N = a_t.shape[0], g.shape[1]
    return _mm(a_t, g, tm=K if K <= 1024 else K // 2, tn=next(t for t in range(1536, 0, -128) if N % t == 0), name=name)


def _mm_tn(a, g, *, name):
    S, K = a.shape
    N = g.shape[1]
    tk = K if K <= 1024 else K // 2
    tn = next(t for t in range(1536, 0, -128) if N % t == 0)
    ts = min(512, S)
    ns = S // ts
    assert K % tk == 0 and S % ts == 0

    def body(a_ref, g_ref, o_ref):
        s = pl.program_id(2)
        p = _dg(a_ref[...], g_ref[...], TN)

        @pl.when(s == 0)
        def _():
            o_ref[...] = p

        @pl.when(s > 0)
        def _():
            o_ref[...] += p

    return pl.pallas_call(
        body, name=name, grid=(K // tk, N // tn, ns),
        in_specs=[pl.BlockSpec((ts, tk), lambda i, j, s: (s, i)), pl.BlockSpec((ts, tn), lambda i, j, s: (s, j))],
        out_specs=pl.BlockSpec((tk, tn), lambda i, j, s: (i, j)), out_shape=_sds((K, N), F32),
        compiler_params=_params(("parallel", "parallel", "arbitrary")))(a, g)


def _mm_res_ln(a_list, w, resid, gam, bet, *, name):
    M, Ka = a_list[0].shape
    na = len(a_list)
    assert w.shape[0] == na * Ka
    tm = min(256, M)

    def body(*refs):
        a_refs, w_refs = refs[:na], refs[na:2 * na]
        r_ref, g_ref, b_ref, y_ref, yb_ref, yt_ref, z_ref = refs[2 * na:]
        z = ALPHA * r_ref[...]
        for a_ref, w_ref in zip(a_refs, w_refs):
            z = z + _dg(a_ref[...], w_ref[...], NN)
        mu = jnp.mean(z, axis=1, keepdims=True)
        zc = z - mu
        var = jnp.mean(zc * zc, axis=1, keepdims=True)
        y = zc * lax.rsqrt(var + EPS) * g_ref[...] + b_ref[...]
        y_ref[...] = y
        yb_ref[...] = y.astype(BF16)
        yt_ref[...] = y.T.astype(BF16)
        z_ref[...] = z

    row = pl.BlockSpec((tm, D), lambda i: (i, 0))
    vec = pl.BlockSpec((1, D), lambda i: (0, 0))
    a_specs = [pl.BlockSpec((tm, Ka), lambda i: (i, 0)) for _ in a_list]
    w_specs = [pl.BlockSpec((Ka, D), lambda i, t=t: (t, 0)) for t in range(na)]
    return pl.pallas_call(
        body, name=name, grid=(M // tm,), in_specs=a_specs + w_specs + [row, vec, vec],
        out_specs=[row, row, pl.BlockSpec((D, tm), lambda i: (0, i)), row],
        out_shape=[_sds((M, D), F32), _sds((M, D), BF16), _sds((D, M), BF16), _sds((M, D), F32)],
        compiler_params=_params(("parallel",)))(*a_list, *([w] * na), resid, gam, bet)


def _grad_in(pairs, add, *, ln=None, loss=None, dep=(), name):
    M = (add if add is not None else loss[0]).shape[0]
    tm = min(256, M)
    npair, nd = len(pairs), len(dep)
    has_ln, has_loss = ln is not None, loss is not None

    def body(*refs):
        n_in = 2 * npair + (2 if has_loss else 1) + 2 * has_ln + nd
        ins, outs = refs[:n_in], refs[n_in:]
        i = pl.program_id(0)
        pos = 2 * npair
        if has_loss:
            e = ins[pos][...] - ins[pos + 1][...]
            pos += 2
            dy = e * (1.0 / D)
            part = 0.5 * jnp.sum(jnp.sum(e * e, axis=1, keepdims=True) * (1.0 / D), axis=0, keepdims=True)
        else:
            dy = ALPHA * ins[pos][...]
            pos += 1
            for t in range(npair):
                dy = dy + _dg(ins[2 * t][...], ins[2 * t + 1][...], NT)
        if not has_ln:
            outs[0][...] = dy
            return
        z, g_ref = ins[pos][...], ins[pos + 1]
        mu = jnp.mean(z, axis=1, keepdims=True)
        zc = z - mu
        var = jnp.mean(zc * zc, axis=1, keepdims=True)
        rstd = lax.rsqrt(var + EPS)
        xh = zc * rstd
        dxh = dy * g_ref[...]
        m1 = jnp.mean(dxh, axis=1, keepdims=True)
        m2 = jnp.mean(dxh * xh, axis=1, keepdims=True)
        dz = rstd * (dxh - m1 - xh * m2)
        outs[0][...] = dz
        outs[1][...] = dz.astype(BF16)
        acc = [(outs[2], jnp.sum(dy * xh, axis=0, keepdims=True)), (outs[3], jnp.sum(dy, axis=0, keepdims=True))]
        if has_loss:
            acc.append((outs[4], jnp.broadcast_to(part, (8, 128))))

        @pl.when(i == 0)
        def _():
            for ref, val in acc:
                ref[...] = val

        @pl.when(i > 0)
        def _():
            for ref, val in acc:
                ref[...] += val

    row = pl.BlockSpec((tm, D), lambda i: (i, 0))
    vec = pl.BlockSpec((1, D), lambda i: (0, 0))
    in_specs, args = [], []
    for a, b, blk in pairs:
        in_specs += [pl.BlockSpec((tm, a.shape[1]), lambda i: (i, 0)), pl.BlockSpec((D, a.shape[1]), lambda i, blk=blk: (0, blk))]
        args += [a, b]
    if has_loss:
        in_specs += [row, row]
        args += list(loss)
    else:
        in_specs.append(row)
        args.append(add)
    if has_ln:
        in_specs += [row, vec]
        args += list(ln)
    in_specs += [TOKEN] * nd
    args += list(dep)
    if has_ln:
        out_specs = [row, row, vec, vec] + ([pl.BlockSpec((8, 128), lambda i: (0, 0))] if has_loss else [])
        out_shape = [_sds((M, D), F32), _sds((M, D), BF16), _sds((1, D), F32), _sds((1, D), F32)] + ([_sds((8, 128), F32)] if has_loss else [])
    else:
        out_specs, out_shape = row, _sds((M, D), F32)
    return pl.pallas_call(
        body, name=name, grid=(M // tm,), in_specs=in_specs, out_specs=out_specs, out_shape=out_shape,
        compiler_params=_params(("arbitrary",) if has_ln else ("parallel",)))(*args)


def _shift_down(x, d):
    if d == 0:
        return x
    rows = lax.broadcasted_iota(jnp.int32, x.shape, 0)
    return jnp.where(rows >= d, pltpu.roll(x, d, 0), 0.0)


def _shift_up(x, d):
    if d == 0:
        return x
    S = x.shape[0]
    rows = lax.broadcasted_iota(jnp.int32, x.shape, 0)
    return jnp.where(rows < S - d, pltpu.roll(x, S - d, 0), 0.0)


def _conv(x, w_ref, b_ref, cs, K):
    y = b_ref[:, cs]
    for j in range(K):
        y = y + _shift_down(x, K - 1 - j) * w_ref[j:j + 1, cs]
    return y


def _conv_bwd(dy, x, w_ref, dw_ref, db_ref, cs, K):
    dx = jnp.zeros_like(x)
    for j in range(K):
        sdy = _shift_up(dy, K - 1 - j)
        dx = dx + sdy * w_ref[j:j + 1, cs]
        dw_ref[j:j + 1, cs] = jnp.sum(sdy * x, axis=0, keepdims=True)
    db_ref[:, cs] = jnp.sum(dy, axis=0, keepdims=True)
    return dx


ALL = slice(None)


def _silu_conv_fwd(proj, cw, cb, *, name):
    S = proj.shape[0]

    def body(x_ref, w_ref, b_ref, o_ref):
        o_ref[...] = jax.nn.silu(_conv(x_ref[...], w_ref, b_ref, ALL, ML_CONV))

    col = pl.BlockSpec((S, 128), lambda j: (0, j))
    return pl.pallas_call(
        body, name=name, grid=(8,),
        in_specs=[col, pl.BlockSpec((ML_CONV, 128), lambda j: (0, j)), pl.BlockSpec((1, 128), lambda j: (0, j))],
        out_specs=col, out_shape=_sds((S, 2 * ML_W), F32), compiler_params=_params(("parallel",)))(proj, cw, cb)


def _silu_conv_bwd(dqk, proj, cw, cb, *, name):
    S = proj.shape[0]

    def body(d_ref, x_ref, w_ref, b_ref, dx_ref, dw_ref, db_ref):
        x = x_ref[...]
        y = _conv(x, w_ref, b_ref, ALL, ML_CONV)
        dy = jax.vjp(jax.nn.silu, y)[1](d_ref[...])[0]
        dx_ref[...] = _conv_bwd(dy, x, w_ref, dw_ref, db_ref, ALL, ML_CONV).astype(BF16)

    col = pl.BlockSpec((S, 128), lambda j: (0, j))
    wsp = pl.BlockSpec((ML_CONV, 128), lambda j: (0, j))
    bsp = pl.BlockSpec((1, 128), lambda j: (0, j))
    return pl.pallas_call(
        body, name=name, grid=(8,), in_specs=[col, col, wsp, bsp], out_specs=[col, wsp, bsp],
        out_shape=[_sds((S, 2 * ML_W), BF16), _sds((ML_CONV, 2 * ML_W), F32), _sds((1, 2 * ML_W), F32)],
        compiler_params=_params(("parallel",)))(dqk, proj, cw, cb)


GELU_C0 = math.sqrt(2.0 / math.pi)
GELU_C1 = 0.044715


def _gate_bwd(ug, uv, da):
    t = jnp.tanh(GELU_C0 * (ug + GELU_C1 * (ug * ug * ug)))
    half = 0.5 * (1.0 + t)
    dgelu = half + 0.5 * ug * (1.0 - t * t) * (GELU_C0 * (1.0 + 3.0 * GELU_C1 * (ug * ug)))
    return da * uv * dgelu, da * (ug * half)


def _up_pair(h_ref, ugw_ref, uvw_ref):
    return _dg(h_ref[...], jnp.concatenate([ugw_ref[...], uvw_ref[...]], axis=1), NN)


def _ffn_specs(S):
    return (pl.BlockSpec((D, 128), lambda j: (0, j)), pl.BlockSpec((D, 128), lambda j: (0, j + NB_FF)),
            pl.BlockSpec((FFN_CONV, 128), lambda j: (0, j)), pl.BlockSpec((FFN_CONV, 128), lambda j: (0, j + NB_FF)),
            pl.BlockSpec((1, 128), lambda j: (0, j)), pl.BlockSpec((1, 128), lambda j: (0, j + NB_FF)))


def _ffn_gate_fwd(hb, w_up, cw, cb, *, name):
    S = hb.shape[0]
    ug_, uv_, wg, wv, bg, bv = _ffn_specs(S)

    def body(h_ref, ugw_ref, uvw_ref, wg_ref, wv_ref, bg_ref, bv_ref, o_ref, ot_ref, h_s):
        @pl.when(pl.program_id(0) == 0)
        def _():
            pltpu.sync_copy(h_ref, h_s)

        x2 = _up_pair(h_s, ugw_ref, uvw_ref)
        ug = _conv(x2[:, :128], wg_ref, bg_ref, ALL, FFN_CONV)
        uv = _conv(x2[:, 128:], wv_ref, bv_ref, ALL, FFN_CONV)
        act = jax.nn.gelu(ug) * uv
        o_ref[...] = act.astype(BF16)
        ot_ref[...] = act.T.astype(BF16)

    return pl.pallas_call(
        body, name=name, grid=(NB_FF,), in_specs=[ANY, ug_, uv_, wg, wv, bg, bv],
        out_specs=[pl.BlockSpec((S, 128), lambda j: (0, j)), pl.BlockSpec((128, S), lambda j: (j, 0))],
        out_shape=[_sds((S, DFF), BF16), _sds((DFF, S), BF16)],
        scratch_shapes=[pltpu.VMEM((S, D), BF16)],
        compiler_params=_params(("arbitrary",)))(hb, w_up, w_up, cw, cw, cb, cb)


def _ffn_gate_bwd(dzb, w_down, hb, w_up, cw, cb, *, name):
    S = hb.shape[0]
    ug_, uv_, wg, wv, bg, bv = _ffn_specs(S)

    def body(dz_ref, h_ref, wd_ref, ugw_ref, uvw_ref, wg_ref, wv_ref, bg_ref, bv_ref,
             dxg_ref, dxv_ref, dwg_ref, dwv_ref, dbg_ref, dbv_ref, dz_s, h_s):
        @pl.when(pl.program_id(0) == 0)
        def _():
            pltpu.sync_copy(dz_ref, dz_s)
            pltpu.sync_copy(h_ref, h_s)

        x2 = _up_pair(h_s, ugw_ref, uvw_ref)
        xg, xv = x2[:, :128], x2[:, 128:]
        ug = _conv(xg, wg_ref, bg_ref, ALL, FFN_CONV)
        uv = _conv(xv, wv_ref, bv_ref, ALL, FFN_CONV)
        dug, duv = _gate_bwd(ug, uv, _dg(dz_s[...], wd_ref[...], NT))
        dxg_ref[...] = _conv_bwd(dug, xg, wg_ref, dwg_ref, dbg_ref, ALL, FFN_CONV).astype(BF16)
        dxv_ref[...] = _conv_bwd(duv, xv, wv_ref, dwv_ref, dbv_ref, ALL, FFN_CONV).astype(BF16)

    col = pl.BlockSpec((S, 128), lambda j: (0, j))
    half = _sds((S, DFF), BF16)
    dxg, dxv, dwg, dwv, dbg, dbv = pl.pallas_call(
        body, name=name, grid=(NB_FF,),
        in_specs=[ANY, ANY, pl.BlockSpec((128, D), lambda j: (j, 0)), ug_, uv_, wg, wv, bg, bv],
        out_specs=[col, col, wg, wg, bg, bg],
        out_shape=[half, half, _sds((FFN_CONV, DFF), F32), _sds((FFN_CONV, DFF), F32), _sds((1, DFF), F32), _sds((1, DFF), F32)],
        scratch_shapes=[pltpu.VMEM((S, D), BF16), pltpu.VMEM((S, D), BF16)],
        compiler_params=_params(("arbitrary",)))(dzb, hb, w_down, w_up, w_up, cw, cw, cb, cb)
    return dxg, dxv, jnp.concatenate([dwg, dwv], axis=1), jnp.concatenate([dbg, dbv], axis=1)


def _log_sigmoid(x):
    return jnp.minimum(x, 0.0) - jnp.log1p(jnp.exp(-jnp.abs(x)))


@jax.custom_vjp
def _clamp_div(num, den, floor, shift):
    return num / jnp.maximum(jnp.abs(den), floor)


def _clamp_div_fwd(num, den, floor, shift):
    out = num / jnp.maximum(jnp.abs(den), floor)
    return out, (den, floor, out)


def _clamp_div_bwd(res, g):
    den, floor, out = res
    active = jnp.abs(den) < floor
    dinv = jnp.maximum(jnp.abs(den), floor)
    go = jnp.sum(g * out, axis=-1, keepdims=True)
    ddiv = -go / dinv
    return (g / dinv, jnp.where(active, 0.0, ddiv * jnp.sign(den)), jnp.where(active, ddiv, 0.0),
            jnp.sum(jnp.where(active, go, 0.0), axis=-2, keepdims=True))


_clamp_div.defvjp(_clamp_div_fwd, _clamp_div_bwd)


def _ml_heads(q, k, v, o_pre, gates, gbias, C, n, ng, m, shift):
    H, L, _ = q.shape
    lane1 = lax.broadcasted_iota(jnp.int32, (1, 128), 1)
    gz = gates + jnp.where(lane1 < ML_H, lax.stop_gradient(gbias), gbias)
    gz = jnp.broadcast_to(gz[None], (H, L, 128))
    hid = lax.broadcasted_iota(jnp.int32, (H, L, 128), 0)
    lane = lax.broadcasted_iota(jnp.int32, (H, L, 128), 2)
    ig = jnp.sum(jnp.where(lane == hid, gz, 0.0), axis=2, keepdims=True)
    lf = _log_sigmoid(jnp.sum(jnp.where(lane == ML_H + hid, gz, 0.0), axis=2, keepdims=True))
    r = lax.broadcasted_iota(jnp.int32, (H, L, L), 1)
    c = lax.broadcasted_iota(jnp.int32, (H, L, L), 2)
    eye, tril = r == c, r >= c

    def to_row(col):
        return jnp.sum(jnp.where(eye, col, 0.0), axis=1, keepdims=True)

    b_col = jnp.sum(jnp.where(tril, to_row(lf), 0.0), axis=2, keepdims=True)
    Dm = jnp.where(tril, b_col - to_row(b_col) + to_row(ig), -jnp.inf)
    inter = b_col + m
    m_t = lax.stop_gradient(jnp.maximum(inter, jnp.max(Dm, axis=2, keepdims=True)))
    w_inter = jnp.exp(inter - m_t)
    ks = k * (ML_DH ** -0.5)
    s = dot_nt(q, ks) * jnp.exp(Dm - m_t)
    num = w_inter * dot_nn(q, C) + dot_nn(s, v)
    den = w_inter * jnp.sum(q * n, axis=2, keepdims=True) + jnp.sum(s, axis=2, keepdims=True)
    h = _clamp_div(num, den, jnp.exp(-m_t), shift)
    g = jnp.sum(lf, axis=1, keepdims=True)
    a = g - b_col + ig
    m_new = lax.stop_gradient(jnp.maximum(g + m, jnp.max(a, axis=1, keepdims=True)))
    decay = jnp.exp(g + m - m_new)
    wk = jnp.exp(a - m_new)
    C_new = decay * C + dot_tn(ks * wk, v)
    n_new = decay * n + jnp.sum(wk * ks, axis=1, keepdims=True)
    mu = jnp.mean(h, axis=2, keepdims=True)
    hc = h - mu
    var = jnp.mean(hc * hc, axis=2, keepdims=True)
    out = jax.nn.sigmoid(o_pre) * (hc * lax.rsqrt(var + EPS) * ng)
    return out, C_new, n_new, m_new


def _hs(h, off=0):
    return slice(off + h * ML_DH, off + (h + 1) * ML_DH)


def _heads(ref, off=0):
    return jnp.stack([ref[:, _hs(h, off)] for h in range(ML_H)])


def _mlstm_fwd(qk, proj, gbias, ng, *, name):
    S = qk.shape[0]
    nc = S // ML_L

    def body(q_ref, k_ref, v_ref, o_ref, g_ref, gb_ref, ng_ref, h_ref, cs_ref, ns_ref, ms_ref, c_s, n_s, m_s):
        @pl.when(pl.program_id(0) == 0)
        def _():
            c_s[...] = jnp.zeros_like(c_s)
            n_s[...] = jnp.zeros_like(n_s)
            m_s[...] = jnp.zeros_like(m_s)

        C, n = c_s[...], n_s[...]
        cs_ref[0] = C
        ns_ref[0] = n
        ms_ref[0] = m_s[...]
        out, C2, n2, m2 = _ml_heads(_heads(q_ref), _heads(k_ref), _heads(v_ref), _heads(o_ref), g_ref[...], gb_ref[...], C, n,
                                    _heads(ng_ref), m_s[:, :, 0:1], jnp.zeros((ML_H, 1, 1), F32))
        for h in range(ML_H):
            h_ref[:, _hs(h)] = out[h].astype(BF16)
        c_s[...] = C2
        n_s[...] = n2
        m_s[...] = jnp.broadcast_to(m2, (ML_H, 1, 128))

    def w(j):
        return pl.BlockSpec((ML_L, ML_W), lambda c, j=j: (c, j))

    return pl.pallas_call(
        body, name=name, grid=(nc,),
        in_specs=[w(0), w(1), w(2), w(3), pl.BlockSpec((ML_L, 128), lambda c: (c, 22)),
                  pl.BlockSpec((1, 128), lambda c: (0, 0)), pl.BlockSpec((1, ML_W), lambda c: (0, 0))],
        out_specs=[w(0), pl.BlockSpec((1, ML_H, ML_DH, ML_DH), lambda c: (c, 0, 0, 0)),
                   pl.BlockSpec((1, ML_H, 1, 128), lambda c: (c, 0, 0, 0)), pl.BlockSpec((1, ML_H, 1, 128), lambda c: (c, 0, 0, 0))],
        out_shape=[_sds((S, ML_W), BF16), _sds((nc, ML_H, ML_DH, ML_DH), F32), _sds((nc, ML_H, 1, 128), F32),
                   _sds((nc, ML_H, 1, 128), F32)],
        scratch_shapes=[pltpu.VMEM((ML_H, ML_DH, ML_DH), F32), pltpu.VMEM((ML_H, 1, 128), F32), pltpu.VMEM((ML_H, 1, 128), F32)],
        compiler_params=_params(("arbitrary",)))(qk, qk, proj, proj, proj, gbias, ng)


def _mlstm_bwd(dh, qk, proj, gbias, ng, cs, ns, ms, *, name):
    S = qk.shape[0]
    nc = S // ML_L

    def body(dh_ref, q_ref, k_ref, v_ref, o_ref, g_ref, gb_ref, ng_ref, cs_ref, ns_ref, ms_ref,
             dqk_ref, dml_ref, dgb_ref, dng_ref, dc_s, dn_s):
        @pl.when(pl.program_id(0) == 0)
        def _():
            dc_s[...] = jnp.zeros_like(dc_s)
            dn_s[...] = jnp.zeros_like(dn_s)
            dgb_ref[...] = jnp.zeros_like(dgb_ref)
            dng_ref[...] = jnp.zeros_like(dng_ref)

        m = ms_ref[0][:, :, 0:1]

        def f(q, k, v, o_pre, gates, gb, C, n, ng_, shift):
            return _ml_heads(q, k, v, o_pre, gates, gb, C, n, ng_, m, shift)[:3]

        _, vjp = jax.vjp(f, _heads(q_ref), _heads(k_ref), _heads(v_ref), _heads(o_ref), g_ref[...], gb_ref[...],
                         cs_ref[0], ns_ref[0], _heads(ng_ref), jnp.zeros((ML_H, 1, 1), F32))
        dq, dk, dv, do, dgates, dgb, dC, dn, dng, dshift = vjp((_heads(dh_ref), dc_s[...], dn_s[...]))
        lane1 = lax.broadcasted_iota(jnp.int32, (1, 128), 1)
        for h in range(ML_H):
            dqk_ref[:, _hs(h)] = dq[h]
            dqk_ref[:, _hs(h, ML_W)] = dk[h]
            dml_ref[:, _hs(h)] = dv[h].astype(BF16)
            dml_ref[:, _hs(h, ML_W)] = do[h].astype(BF16)
            dng_ref[:, _hs(h)] += dng[h]
            dgb = jnp.where(lane1 == h, dshift[h], dgb)
        dc_s[...] = dC
        dn_s[...] = dn
        dml_ref[:, 2 * ML_W:] = dgates.astype(BF16)
        dgb_ref[...] += dgb

    def w(j):
        return pl.BlockSpec((ML_L, ML_W), lambda c, j=j: (nc - 1 - c, j))

    vec = pl.BlockSpec((1, 128), lambda c: (0, 0))
    vecw = pl.BlockSpec((1, ML_W), lambda c: (0, 0))
    gsp = pl.BlockSpec((ML_L, 128), lambda c: (nc - 1 - c, 22))
    st = pl.BlockSpec((1, ML_H, 1, 128), lambda c: (nc - 1 - c, 0, 0, 0))
    return pl.pallas_call(
        body, name=name, grid=(nc,),
        in_specs=[w(0), w(0), w(1), w(2), w(3), gsp, vec, vecw,
                  pl.BlockSpec((1, ML_H, ML_DH, ML_DH), lambda c: (nc - 1 - c, 0, 0, 0)), st, st],
        out_specs=[pl.BlockSpec((ML_L, 2 * ML_W), lambda c: (nc - 1 - c, 0)), pl.BlockSpec((ML_L, ML_GW), lambda c: (nc - 1 - c, 0)),
                   vec, vecw],
        out_shape=[_sds((S, 2 * ML_W), F32), _sds((S, ML_GW), BF16), _sds((1, 128), F32), _sds((1, ML_W), F32)],
        scratch_shapes=[pltpu.VMEM((ML_H, ML_DH, ML_DH), F32), pltpu.VMEM((ML_H, 1, 128), F32)],
        compiler_params=_params(("arbitrary",)))(dh, qk, qk, proj, proj, proj, gbias, ng, cs, ns, ms)


def _t5_buckets():
    r = np.arange(BLK)[:, None]
    c = np.arange(2 * BLK)[None, :]
    n = np.maximum(BLK + r - c, 0)
    max_exact = REL_B // 2
    nf = np.maximum(n, 1).astype(np.float32)
    large = max_exact + (np.log(nf / np.float32(max_exact)) / np.float32(math.log(REL_MAXD / max_exact))
                         * np.float32(REL_B - max_exact)).astype(np.int32)
    large = np.minimum(large, REL_B - 1)
    return np.where(n < max_exact, n, large).astype(np.int32)


def _bias_table(rel_bias, bucket, *, name):
    def body(rb_ref, bk_ref, o_ref):
        bk = bk_ref[...]
        for h in range(SW_H):
            acc = jnp.zeros((BLK, 2 * BLK), F32)
            for b in range(REL_B):
                acc = jnp.where(bk == b, rb_ref[b, h], acc)
            o_ref[h] = acc

    return pl.pallas_call(
        body, name=name, in_specs=[pl.BlockSpec(memory_space=pltpu.SMEM), pl.BlockSpec(memory_space=pltpu.VMEM)],
        out_specs=pl.BlockSpec(memory_space=pltpu.VMEM), out_shape=_sds((SW_H, BLK, 2 * BLK), F32),
        compiler_params=_params())(rel_bias, bucket)


def _bias_table_bwd(dbias_list, bucket, *, name):
    nl = len(dbias_list)

    def body(*refs):
        d_refs, bk_ref, o_ref = refs[:nl], refs[nl], refs[nl + 1]
        bk = bk_ref[...]
        rows = lax.broadcasted_iota(jnp.int32, (REL_B, 128), 0)
        lanes = lax.broadcasted_iota(jnp.int32, (REL_B, 128), 1)
        acc = jnp.zeros((REL_B, 128), F32)
        for h in range(SW_H):
            d = d_refs[0][h]
            for d_ref in d_refs[1:]:
                d = d + d_ref[h]
            for b in range(REL_B):
                t = jnp.sum(jnp.sum(jnp.where(bk == b, d, 0.0), axis=0, keepdims=True), axis=1, keepdims=True)
                acc = jnp.where((rows == b) & (lanes == h), t, acc)
        o_ref[...] = acc

    vm = pl.BlockSpec(memory_space=pltpu.VMEM)
    return pl.pallas_call(
        body, name=name, in_specs=[vm] * (nl + 1), out_specs=vm, out_shape=_sds((REL_B, 128), F32),
        compiler_params=_params())(*dbias_list, bucket)


def _swa_heads(q, kp, kc, vp, vc, bp, bc, sinks, has_prev):
    def rep(t):
        return jnp.concatenate([t[g:g + 1] for g in range(SW_H // SW_G) for _ in range(SW_G)], axis=0)

    r = lax.broadcasted_iota(jnp.int32, (SW_H, BLK, BLK), 1)
    c = lax.broadcasted_iota(jnp.int32, (SW_H, BLK, BLK), 2)
    hid = lax.broadcasted_iota(jnp.int32, (SW_H, 1, 128), 0)
    lane = lax.broadcasted_iota(jnp.int32, (SW_H, 1, 128), 2)
    sink = jnp.sum(jnp.where(lane == hid, jnp.broadcast_to(sinks[None], (SW_H, 1, 128)), 0.0), axis=2, keepdims=True)
    lp = jnp.where((c > r) & has_prev, dot_nt(q, rep(kp)) * (SW_DH ** -0.5) + bp, -jnp.inf)
    lc = jnp.where(c <= r, dot_nt(q, rep(kc)) * (SW_DH ** -0.5) + bc, -jnp.inf)
    mx = jnp.maximum(jnp.maximum(jnp.max(lp, axis=2, keepdims=True), jnp.max(lc, axis=2, keepdims=True)), sink)
    mx = lax.stop_gradient(mx)
    pp, pc = jnp.exp(lp - mx), jnp.exp(lc - mx)
    den = jnp.sum(pp, axis=2, keepdims=True) + jnp.sum(pc, axis=2, keepdims=True) + jnp.exp(sink - mx)
    return dot_nn(pp / den, rep(vp)) + dot_nn(pc / den, rep(vc))


def _qs(h, off=0):
    return slice(off + h * SW_DH, off + (h + 1) * SW_DH)


def _split(ref, n):
    return jnp.stack([ref[:, _qs(h)] for h in range(n)])


def _swa_fwd(proj, bias, sinks, *, name):
    S = proj.shape[0]
    nb = S // BLK
    nkv = SW_H // SW_G

    def body(q_ref, kp_ref, kc_ref, vp_ref, vc_ref, b_ref, s_ref, o_ref):
        out = _swa_heads(_split(q_ref, SW_H), _split(kp_ref, nkv), _split(kc_ref, nkv), _split(vp_ref, nkv), _split(vc_ref, nkv),
                         b_ref[:, :, :BLK], b_ref[:, :, BLK:], s_ref[...], pl.program_id(0) > 0)
        for h in range(SW_H):
            o_ref[:, _qs(h)] = out[h].astype(BF16)

    def cur(j):
        return pl.BlockSpec((BLK, 128), lambda n, j=j: (n, j))

    def prev(j):
        return pl.BlockSpec((BLK, 128), lambda n, j=j: (jnp.maximum(n - 1, 0), j))

    return pl.pallas_call(
        body, name=name, grid=(nb,),
        in_specs=[pl.BlockSpec((BLK, SW_W), lambda n: (n, 4)), prev(20), cur(20), prev(21), cur(21),
                  pl.BlockSpec((SW_H, BLK, 2 * BLK), lambda n: (0, 0, 0)), pl.BlockSpec((1, 128), lambda n: (0, 0))],
        out_specs=pl.BlockSpec((BLK, SW_W), lambda n: (n, 0)), out_shape=_sds((S, SW_W), BF16),
        compiler_params=_params(("parallel",)))(proj, proj, proj, proj, proj, bias, sinks)


def _swa_bwd(dh, proj, bias, sinks, *, name):
    S = proj.shape[0]
    nb = S // BLK

    nkv = SW_H // SW_G

    def body(dh_ref, q_ref, kp_ref, kc_ref, vp_ref, vc_ref, b_ref, s_ref, dsw_ref, db_ref, ds_ref, ck_s, cv_s):
        i = pl.program_id(0)

        @pl.when(i == 0)
        def _():
            ck_s[...] = jnp.zeros_like(ck_s)
            cv_s[...] = jnp.zeros_like(cv_s)
            db_ref[...] = jnp.zeros_like(db_ref)
            ds_ref[...] = jnp.zeros_like(ds_ref)

        f = functools.partial(_swa_heads, has_prev=i < nb - 1)
        _, vjp = jax.vjp(f, _split(q_ref, SW_H), _split(kp_ref, nkv), _split(kc_ref, nkv), _split(vp_ref, nkv),
                         _split(vc_ref, nkv), b_ref[:, :, :BLK], b_ref[:, :, BLK:], s_ref[...])
        dq, dkp, dkc, dvp, dvc, dbp, dbc, ds = vjp(_split(dh_ref, SW_H))
        for h in range(SW_H):
            dsw_ref[:, _qs(h)] = dq[h].astype(BF16)
        for g in range(nkv):
            dsw_ref[:, _qs(g, SW_W)] = (dkc[g] + ck_s[:, _qs(g)]).astype(BF16)
            dsw_ref[:, _qs(g, SW_W + SW_KVW)] = (dvc[g] + cv_s[:, _qs(g)]).astype(BF16)
            ck_s[:, _qs(g)] = dkp[g]
            cv_s[:, _qs(g)] = dvp[g]
        db_ref[:, :, :BLK] += dbp
        db_ref[:, :, BLK:] += dbc
        ds_ref[...] += ds

    def cur(j):
        return pl.BlockSpec((BLK, 128), lambda i, j=j: (nb - 1 - i, j))

    def prev(j):
        return pl.BlockSpec((BLK, 128), lambda i, j=j: (jnp.maximum(nb - 2 - i, 0), j))

    bsp = pl.BlockSpec((SW_H, BLK, 2 * BLK), lambda i: (0, 0, 0))
    vec = pl.BlockSpec((1, 128), lambda i: (0, 0))
    return pl.pallas_call(
        body, name=name, grid=(nb,),
        in_specs=[pl.BlockSpec((BLK, SW_W), lambda i: (nb - 1 - i, 1)), pl.BlockSpec((BLK, SW_W), lambda i: (nb - 1 - i, 4)),
                  prev(20), cur(20), prev(21), cur(21), bsp, vec],
        out_specs=[pl.BlockSpec((BLK, SW_GW), lambda i: (nb - 1 - i, 0)), bsp, vec],
        out_shape=[_sds((S, SW_GW), BF16), _sds((SW_H, BLK, 2 * BLK), F32), _sds((1, 128), F32)],
        scratch_shapes=[pltpu.VMEM((BLK, 128), F32)] * 2,
        compiler_params=_params(("arbitrary",)))(dh, proj, proj, proj, proj, proj, bias, sinks)


XA_TM = 512


def _xa_head(qh, kh, vh):
    logits = dot_nt(qh, kh) * (XA_DH ** -0.5)
    mx = lax.stop_gradient(jnp.max(logits, axis=1, keepdims=True))
    e = jnp.exp(logits - mx)
    return dot_nn(e / jnp.sum(e, axis=1, keepdims=True), vh)


def _xs(h, off=0):
    return slice(off + h * XA_DH, off + (h + 1) * XA_DH)


def _xattn_fwd(hb, wq, kv, *, name):
    S = hb.shape[0]
    M = kv.shape[0]

    def body(h_ref, wq_ref, kv_ref, o_ref):
        q = _dg(h_ref[...], wq_ref[...], NN)
        for h in range(XA_H):
            o_ref[:, _xs(h)] = _xa_head(q[:, _xs(h)], kv_ref[:, _xs(h)], kv_ref[:, _xs(h, D)]).astype(BF16)

    tm = min(XA_TM, S)
    row = pl.BlockSpec((tm, D), lambda i: (i, 0))
    return pl.pallas_call(
        body, name=name, grid=(S // tm,),
        in_specs=[row, pl.BlockSpec((D, D), lambda i: (0, 0)), pl.BlockSpec((M, 2 * D), lambda i: (0, 0))], out_specs=row,
        out_shape=_sds((S, D), BF16), compiler_params=_params(("parallel",)))(hb, wq, kv)


def _xattn_bwd(do, hb, wq, kv, *, name):
    S = hb.shape[0]
    M = kv.shape[0]

    def body(do_ref, h_ref, wq_ref, kv_ref, dq_ref, dkv_ref):
        @pl.when(pl.program_id(0) == 0)
        def _():
            dkv_ref[...] = jnp.zeros_like(dkv_ref)

        q = _dg(h_ref[...], wq_ref[...], NN)
        for h in range(XA_H):
            _, vjp = jax.vjp(_xa_head, q[:, _xs(h)], kv_ref[:, _xs(h)], kv_ref[:, _xs(h, D)])
            dq, dk, dv = vjp(do_ref[:, _xs(h)])
            dq_ref[:, _xs(h)] = dq.astype(BF16)
            dkv_ref[:, _xs(h)] += dk
            dkv_ref[:, _xs(h, D)] += dv

    tm = min(XA_TM, S)
    row = pl.BlockSpec((tm, D), lambda i: (i, 0))
    full = pl.BlockSpec((M, 2 * D), lambda i: (0, 0))
    return pl.pallas_call(
        body, name=name, grid=(S // tm,), in_specs=[row, row, pl.BlockSpec((D, D), lambda i: (0, 0)), full],
        out_specs=[row, full], out_shape=[_sds((S, D), BF16), _sds((M, 2 * D), F32)],
        compiler_params=_params(("arbitrary",)))(do, hb, wq, kv)


ANY = pl.BlockSpec(memory_space=pl.ANY)


def _place():
    x, y, c = lax.axis_index("x"), lax.axis_index("y"), lax.axis_index("c")
    chips = [(1 - x, y), (x, 1 - y), (1 - x, 1 - y)]
    return x, y, c, chips


def _gather(arrs, *, name):
    n = len(arrs)

    def body(*refs):
        srcs, outs = refs[:n], refs[n:2 * n]
        send_sems, recv_sems, local_sems = refs[2 * n:]
        x, y, c, chips = _place()
        me, sib = (x, y, c), (x, y, 1 - c)

        def idx(p):
            return 4 * p[0] + 2 * p[1] + p[2]

        def copy(i, k, block, to, from_src=False):
            return pltpu.make_async_remote_copy(
                src_ref=srcs[i] if from_src else outs[i].at[idx(block)], dst_ref=outs[i].at[idx(block)],
                send_sem=send_sems.at[7 * i + k], recv_sem=recv_sems.at[7 * i + k], device_id=to, device_id_type=MESH)

        local = [pltpu.make_async_copy(srcs[i], outs[i].at[idx(me)], local_sems.at[i]) for i in range(n)]
        for cp in local:
            cp.start()
        first = []
        for i in range(n):
            first.append(copy(i, 0, me, sib, True))
            first += [copy(i, 1 + j, me, (*chip, c), True) for j, chip in enumerate(chips)]
        for cp in first:
            cp.start()
        passed = []
        for j, chip in enumerate(chips):
            for i in range(n):
                copy(i, 1 + j, (*chip, c), me).wait_recv()
                cp = copy(i, 4 + j, (*chip, c), sib)
                cp.start()
                passed.append(cp)
        for i in range(n):
            copy(i, 0, sib, me).wait_recv()
        for j, chip in enumerate(chips):
            for i in range(n):
                copy(i, 4 + j, (*chip, 1 - c), me).wait_recv()
        for cp in first + passed:
            cp.wait_send()
        for cp in local:
            cp.wait()

    return pl.pallas_call(
        body, name=name, in_specs=[ANY] * n, out_specs=[ANY] * n,
        out_shape=[_sds((N_DEV,) + a.shape, a.dtype) for a in arrs],
        scratch_shapes=[pltpu.SemaphoreType.DMA((7 * n,)), pltpu.SemaphoreType.DMA((7 * n,)), pltpu.SemaphoreType.DMA((n,))],
        compiler_params=pltpu.CompilerParams(has_side_effects=True))(*arrs)


def _swap_sibling(arrs, *, name):
    n = len(arrs)

    def body(*refs):
        srcs, outs = refs[:n], refs[n:2 * n]
        send_sems, recv_sems = refs[2 * n:]
        x, y, c, _ = _place()
        copies = [pltpu.make_async_remote_copy(
            src_ref=srcs[i].at[2 * j + (1 - c)], dst_ref=outs[i].at[j], send_sem=send_sems.at[N_CHIP * i + j],
            recv_sem=recv_sems.at[N_CHIP * i + j], device_id=(x, y, 1 - c), device_id_type=MESH)
            for i in range(n) for j in range(N_CHIP)]
        for cp in copies:
            cp.start()
        for cp in copies:
            cp.wait()

    return pl.pallas_call(
        body, name=name, in_specs=[ANY] * n, out_specs=[ANY] * n,
        out_shape=[_sds((N_CHIP,) + a.shape[1:], a.dtype) for a in arrs],
        scratch_shapes=[pltpu.SemaphoreType.DMA((N_CHIP * n,)), pltpu.SemaphoreType.DMA((N_CHIP * n,))],
        compiler_params=pltpu.CompilerParams(has_side_effects=True))(*arrs)


def _pair_sum(part, got, c_idx, *, name):
    _, R, C = part.shape
    tr = next(t for t in (R, 512, 256, 128, 64, 32, 16) if R % t == 0 and t * C <= 1024 * 1024)

    def body(c_ref, p_ref, g_ref, o_ref):
        o_ref[...] = (p_ref[0].astype(F32) + g_ref[...].astype(F32)).astype(o_ref.dtype)

    return pl.pallas_call(
        body, name=name,
        grid_spec=pltpu.PrefetchScalarGridSpec(
            num_scalar_prefetch=1, grid=(N_CHIP, R // tr),
            in_specs=[pl.BlockSpec((1, 1, tr, C), lambda j, r, c_ref: (j, c_ref[0], r, 0)),
                      pl.BlockSpec((1, tr, C), lambda j, r, c_ref: (j, r, 0))],
            out_specs=pl.BlockSpec((1, tr, C), lambda j, r, c_ref: (j, r, 0))),
        out_shape=_sds((N_CHIP, R, C), part.dtype),
        compiler_params=_params(("parallel", "parallel")))(c_idx, part.reshape(N_CHIP, 2, R, C), got)


def _swap_chips(arrs, *, name):
    n = len(arrs)

    def body(*refs):
        srcs, outs = refs[:n], refs[n:2 * n]
        send_sems, recv_sems, local_sems = refs[2 * n:]
        x, y, c, chips = _place()
        jme = 2 * x + y
        local = [pltpu.make_async_copy(srcs[i].at[jme], outs[i].at[jme], local_sems.at[i]) for i in range(n)]
        for cp in local:
            cp.start()
        copies = [pltpu.make_async_remote_copy(
            src_ref=srcs[i].at[2 * chip[0] + chip[1]], dst_ref=outs[i].at[jme], send_sem=send_sems.at[3 * i + j],
            recv_sem=recv_sems.at[3 * i + j], device_id=(*chip, c), device_id_type=MESH)
            for i in range(n) for j, chip in enumerate(chips)]
        for cp in copies:
            cp.start()
        for i in range(n):
            for j, chip in enumerate(chips):
                pltpu.make_async_remote_copy(
                    src_ref=srcs[i].at[jme], dst_ref=outs[i].at[2 * chip[0] + chip[1]], send_sem=send_sems.at[3 * i + j],
                    recv_sem=recv_sems.at[3 * i + j], device_id=(*chip, c), device_id_type=MESH).wait_recv()
        for cp in copies:
            cp.wait_send()
        for cp in local:
            cp.wait()

    return pl.pallas_call(
        body, name=name, in_specs=[ANY] * n, out_specs=[ANY] * n,
        out_shape=[_sds(a.shape, a.dtype) for a in arrs],
        scratch_shapes=[pltpu.SemaphoreType.DMA((3 * n,)), pltpu.SemaphoreType.DMA((3 * n,)), pltpu.SemaphoreType.DMA((n,))],
        compiler_params=pltpu.CompilerParams(has_side_effects=True))(*arrs)


HBM = pl.BlockSpec(memory_space=pltpu.HBM)
SEM = pl.BlockSpec(memory_space=pltpu.SEMAPHORE)
EFFECT = pltpu.SideEffectType.DATAFLOW_SIDE_EFFECTING


def _near_copies(srcs, lands, send_sems, recv_sems):
    x, y, c, chips = _place()
    me = 4 * x + 2 * y + c
    out = []
    for i in range(len(srcs)):
        for k, (px, py, pc) in enumerate([(x, y, 1 - c)] + [(*chip, c) for chip in chips]):
            out.append(tuple(pltpu.make_async_remote_copy(
                src_ref=srcs[i], dst_ref=lands[i].at[slot], send_sem=send_sems.at[4 * i + k], recv_sem=recv_sems.at[4 * i + k],
                device_id=(px, py, pc), device_id_type=MESH) for slot in (me, 4 * px + 2 * py + pc)))
    return out


def _forward_sibling(lands, *, name):
    n = len(lands)

    def body(*refs):
        bufs = refs[n:2 * n]
        send_sems, recv_sems = refs[2 * n:]
        x, y, c, chips = _place()
        copies = [tuple(pltpu.make_async_remote_copy(
            src_ref=bufs[i].at[4 * chip[0] + 2 * chip[1] + c], dst_ref=bufs[i].at[4 * chip[0] + 2 * chip[1] + cc],
            send_sem=send_sems.at[3 * i + j], recv_sem=recv_sems.at[3 * i + j], device_id=(x, y, 1 - c), device_id_type=MESH)
            for cc in (c, 1 - c)) for i in range(n) for j, chip in enumerate(chips)]
        for send, _ in copies:
            send.start()
        for send, recv in copies:
            send.wait_send()
            recv.wait_recv()

    return pl.pallas_call(
        body, name=name, in_specs=[ANY] * n, out_specs=[ANY] * n, out_shape=[_sds(a.shape, a.dtype) for a in lands],
        input_output_aliases={i: i for i in range(n)},
        scratch_shapes=[pltpu.SemaphoreType.DMA((3 * n,)), pltpu.SemaphoreType.DMA((3 * n,))],
        compiler_params=pltpu.CompilerParams(has_side_effects=True))(*lands)


def _scatter_copies(srcs, lands, send_sems, recv_sems):
    x, y, c, _ = _place()
    me = 4 * x + 2 * y + c
    out = []
    for i in range(len(srcs)):
        for j in range(1, N_DEV):
            px, py, pc = x ^ ((j >> 2) & 1), y ^ ((j >> 1) & 1), c ^ (j & 1)
            p = 4 * px + 2 * py + pc
            k = (N_DEV - 1) * i + j - 1
            out.append(tuple(pltpu.make_async_remote_copy(
                src_ref=srcs[i].at[s], dst_ref=lands[i].at[d], send_sem=send_sems.at[k], recv_sem=recv_sems.at[k],
                device_id=(px, py, pc), device_id_type=MESH) for s, d in ((p, me), (me, p))))
    return out


def _split_start(srcs, lands, pattern, ncopy, *, after=(), name):
    n = len(srcs)
    na = len(after)

    def body(*refs):
        sems = refs[2 * n + na:]
        for send, _ in pattern(refs[:n], refs[n:2 * n], sems[0], sems[1]):
            send.start()
        refs[-1][...] = jnp.zeros_like(refs[-1])

    arrs = list(srcs) + list(lands)
    return pl.pallas_call(
        body, name=name,
        out_shape=(pltpu.SemaphoreType.DMA((ncopy,)), pltpu.SemaphoreType.DMA((ncopy,)),
                   *[pltpu.HBM(a.shape, a.dtype) for a in arrs], _sds((8, 128), F32)),
        in_specs=[HBM] * (2 * n) + [ANY] * na, out_specs=(SEM, SEM, *[HBM] * (2 * n), pl.BlockSpec(memory_space=pltpu.VMEM)),
        input_output_aliases={i: 2 + i for i in range(2 * n)},
        compiler_params=pltpu.CompilerParams(has_side_effects=EFFECT))(
            *[pltpu.with_memory_space_constraint(a, pltpu.HBM) for a in arrs], *after)


def _split_wait(started, after, pattern, *, name):
    send_sems, recv_sems, *arrs = started[:-1]
    n = len(arrs) // 2

    def body(*refs):
        for send, recv in pattern(refs[:n], refs[n:2 * n], refs[2 * n], refs[2 * n + 1]):
            send.wait_send()
            recv.wait_recv()

    outs = pl.pallas_call(
        body, name=name, out_shape=tuple(pltpu.HBM(a.shape, a.dtype) for a in arrs),
        in_specs=[HBM] * (2 * n) + [SEM, SEM, ANY], out_specs=tuple([HBM] * (2 * n)),
        input_output_aliases={i: i for i in range(2 * n)},
        compiler_params=pltpu.CompilerParams(has_side_effects=EFFECT))(*arrs, send_sems, recv_sems, after)
    return list(outs[n:])


def _adamw(parts, w, m, v, *, layer=None, prev=None, name):
    P, R, C = parts.shape
    tr = next((t for t in (512, 256, 176, 128, 64, 32, 16, 8) if R % t == 0 and t * C <= 256 * 1024), R)
    c1 = 1.0 / (1.0 - ADAM_B1 ** ADAM_STEP)
    c2 = 1.0 / (1.0 - ADAM_B2 ** ADAM_STEP)
    nprev = 0 if prev is None else 4

    def body(p_ref, w_ref, m_ref, v_ref, *rest):
        g_ref, d_ref, nm_ref, nv_ref = rest[nprev:]
        g = p_ref[0].astype(F32)
        for j in range(1, P):
            g = g + p_ref[j].astype(F32)
        g = g.reshape(w_ref.shape)
        nm = ADAM_B1 * m_ref[...] + (1.0 - ADAM_B1) * g
        nv = ADAM_B2 * v_ref[...] + (1.0 - ADAM_B2) * (g * g)
        g_ref[...] = g
        nm_ref[...] = nm
        nv_ref[...] = nv
        d_ref[...] = -ADAM_LR * ((nm * c1) / (jnp.sqrt(nv * c2) + ADAM_EPS) + ADAM_WD * w_ref[...])

    if layer is None:
        row = pl.BlockSpec((tr, C), lambda i: (i, 0))
    else:
        row = pl.BlockSpec((1, tr, C), lambda i: (layer, i, 0))
    out = _sds(w.shape, F32)
    return pl.pallas_call(
        body, name=name, grid=(R // tr,),
        in_specs=[pl.BlockSpec((P, tr, C), lambda i: (0, i, 0)), row, row, row] + [ANY] * nprev,
        out_specs=[row, row, row, row], out_shape=[out, out, out, out],
        input_output_aliases={4 + k: k for k in range(nprev)},
        compiler_params=_params(("parallel",)))(parts, w, m, v, *(prev or ()))


BIG = ("w_in", "w_out", "xa_wq", "xa_wkv", "xa_wo", "ffn_w_up", "ffn_w_down")
COL_SHARDED = ("w_in", "xa_wkv", "ffn_w_up", "ml_conv_w", "ffn_conv_w")
SHARDED_SMALL = ("ml_conv_w", "ffn_conv_w")
REPLICATED = ("rel_bias", "ml_conv_b", "ml_i_bias", "ml_f_bias", "ml_norm_g", "swa_sinks", "ln1_g", "ln1_b", "ln2_g", "ln2_b",
              "ffn_conv_b", "ln3_g", "ln3_b")
NAMES = ("rel_bias", "w_in", "ml_conv_w", "ml_conv_b", "ml_i_bias", "ml_f_bias", "ml_norm_g", "swa_sinks", "w_out", "ln1_g", "ln1_b",
         "xa_wq", "xa_wkv", "xa_wo", "ln2_g", "ln2_b", "ffn_w_up", "ffn_conv_w", "ffn_conv_b", "ffn_w_down", "ln3_g", "ln3_b")


def _flat_rows(a, mult):
    f = a.reshape(-1)
    n = -(-f.shape[0] // (128 * mult)) * (128 * mult)
    if n != f.shape[0]:
        f = jnp.pad(f, (0, n - f.shape[0]))
    return f.reshape(-1, 128)


def _pack(arrs, mult):
    parts = [_flat_rows(a, mult) for a in arrs]
    return jnp.concatenate(parts, axis=0), [p.shape[0] for p in parts]


def _unpack(flat, rows, shapes):
    out, off = [], 0
    lead = flat.shape[:-2]
    for r, shp in zip(rows, shapes):
        n = int(np.prod(shp))
        piece = flat[..., off:off + r, :].reshape(lead + (r * 128,))[..., :n]
        out.append(piece.reshape(lead + tuple(shp)))
        off += r
    return out


def _full_from_shards(stacked, name):
    if name in COL_SHARDED:
        return jnp.moveaxis(stacked, 0, 2).reshape(stacked.shape[1], stacked.shape[2], N_DEV * stacked.shape[3])
    return jnp.moveaxis(stacked, 0, 1).reshape(stacked.shape[1], N_DEV * stacked.shape[2], stacked.shape[3])


def _shards_from_full(full, name):
    L, A, B = full.shape
    if name in COL_SHARDED:
        return jnp.moveaxis(full.reshape(L, A, N_DEV, B // N_DEV), 2, 0)
    return jnp.moveaxis(full.reshape(L, N_DEV, A // N_DEV, B), 1, 0)


def _pad_win(w):
    z = jnp.zeros(w.shape[:-1] + (NP_IN - N_IN,), w.dtype)
    return jnp.concatenate([w[..., :2048], w[..., 2056:], w[..., 2048:2056], z], axis=-1)


def _row128(v):
    return jnp.pad(v, (0, 128 - v.shape[0])).reshape(1, 128)


REST = BIG[1:]


def _layer_full(stacked, n):
    if n in COL_SHARDED:
        full = jnp.moveaxis(stacked, 0, 1).reshape(stacked.shape[1], N_DEV * stacked.shape[2])
    else:
        full = stacked.reshape(N_DEV * stacked.shape[1], stacked.shape[2])
    return _pad_win(full) if n == "w_in" else full


def _layer_shards(g, n):
    A, B = g.shape
    if n in COL_SHARDED:
        return jnp.moveaxis(g.reshape(A, N_DEV, B // N_DEV), 1, 0).astype(BF16)
    return g.reshape(N_DEV, A // N_DEV, B).astype(BF16)


def _with_own_block(a, idx, nblk):
    return lax.dynamic_update_slice(lax.empty((nblk,) + a.shape, a.dtype), a[None], (idx,) + (0,) * a.ndim)


def _gather_start(arrs, me, *, after=(), name):
    return _split_start(arrs, [_with_own_block(a, me, N_DEV) for a in arrs], _near_copies, 4 * len(arrs), after=after, name=name)


def _gather_finish(started, after, *, name):
    return _forward_sibling(_split_wait(started, after, _near_copies, name=name + "_wait"), name=name + "_forward")


def _reduce_now(tag, names, grads, c_idx):
    send = [_layer_shards(g, n) for n, g in zip(names, grads)]
    got = _swap_sibling(send, name=f"swap_sibling_{tag}")
    sums = [_pair_sum(s, g, c_idx, name=f"pair_sum_{tag}_{n}") for n, s, g in zip(names, send, got)]
    return _swap_chips(sums, name=f"swap_chips_{tag}")


def _reduce_start(tag, names, grads, me):
    send = [_layer_shards(g, n) for n, g in zip(names, grads)]
    lands = [_with_own_block(lax.dynamic_index_in_dim(s, me, 0, keepdims=False), me, N_DEV) for s in send]
    return _split_start(send, lands, _scatter_copies, (N_DEV - 1) * len(send), name=f"reduce_{tag}_start")


def _update_small(Gf, loss_part, W, Mo, Vo, me):
    res = {}
    small = REPLICATED + SHARDED_SMALL
    sp_flat, sp_rows = _pack([Gf[n] for n in small] + [loss_part], 8)
    sp_all = _gather([sp_flat], name="gather_small_grads")[0]

    def widen(n, t):
        if n not in SHARDED_SMALL:
            return t[n]
        return lax.dynamic_update_slice(jnp.zeros(Gf[n].shape, F32), t[n], (0, 0, me * t[n].shape[2]))

    zl = jnp.zeros((8, 128), F32)
    wsm, _ = _pack([widen(n, W) for n in small] + [zl], 8)
    msm, _ = _pack([widen(n, Mo) for n in small] + [zl], 8)
    vsm, _ = _pack([widen(n, Vo) for n in small] + [zl], 8)
    outs_small = [_unpack(o_, sp_rows, [Gf[n].shape for n in small] + [(8, 128)])
                  for o_ in _adamw(sp_all, wsm, msm, vsm, name="adamw_small")]
    for kind, os_ in zip(("g", "d", "m", "v"), outs_small):
        for n, a in zip(small, os_[:-1]):
            if n in SHARDED_SMALL:
                a = lax.dynamic_slice(a, (0, 0, me * W[n].shape[2]), W[n].shape)
            res[kind, n] = a
    return res, outs_small[0][-1][0, 0]


def kernel(x, mem, rel_bias, w_in, ml_conv_w, ml_conv_b, ml_i_bias, ml_f_bias, ml_norm_g, swa_sinks, w_out, ln1_g, ln1_b, xa_wq, xa_wkv, xa_wo, ln2_g, ln2_b, ffn_w_up, ffn_conv_w, ffn_conv_b, ffn_w_down, ln3_g, ln3_b, loss_target, m_rel_bias, m_w_in, m_ml_conv_w, m_ml_conv_b, m_ml_i_bias, m_ml_f_bias, m_ml_norm_g, m_swa_sinks, m_w_out, m_ln1_g, m_ln1_b, m_xa_wq, m_xa_wkv, m_xa_wo, m_ln2_g, m_ln2_b, m_ffn_w_up, m_ffn_conv_w, m_ffn_conv_b, m_ffn_w_down, m_ln3_g, m_ln3_b, v_rel_bias, v_w_in, v_ml_conv_w, v_ml_conv_b, v_ml_i_bias, v_ml_f_bias, v_ml_norm_g, v_swa_sinks, v_w_out, v_ln1_g, v_ln1_b, v_xa_wq, v_xa_wkv, v_xa_wo, v_ln2_g, v_ln2_b, v_ffn_w_up, v_ffn_conv_w, v_ffn_conv_b, v_ffn_w_down, v_ln3_g, v_ln3_b):
    W = dict(rel_bias=rel_bias, w_in=w_in, ml_conv_w=ml_conv_w, ml_conv_b=ml_conv_b, ml_i_bias=ml_i_bias, ml_f_bias=ml_f_bias,
             ml_norm_g=ml_norm_g, swa_sinks=swa_sinks, w_out=w_out, ln1_g=ln1_g, ln1_b=ln1_b, xa_wq=xa_wq, xa_wkv=xa_wkv,
             xa_wo=xa_wo, ln2_g=ln2_g, ln2_b=ln2_b, ffn_w_up=ffn_w_up, ffn_conv_w=ffn_conv_w, ffn_conv_b=ffn_conv_b,
             ffn_w_down=ffn_w_down, ln3_g=ln3_g, ln3_b=ln3_b)
    Mo = dict(rel_bias=m_rel_bias, w_in=m_w_in, ml_conv_w=m_ml_conv_w, ml_conv_b=m_ml_conv_b, ml_i_bias=m_ml_i_bias,
              ml_f_bias=m_ml_f_bias, ml_norm_g=m_ml_norm_g, swa_sinks=m_swa_sinks, w_out=m_w_out, ln1_g=m_ln1_g, ln1_b=m_ln1_b,
              xa_wq=m_xa_wq, xa_wkv=m_xa_wkv, xa_wo=m_xa_wo, ln2_g=m_ln2_g, ln2_b=m_ln2_b, ffn_w_up=m_ffn_w_up,
              ffn_conv_w=m_ffn_conv_w, ffn_conv_b=m_ffn_conv_b, ffn_w_down=m_ffn_w_down, ln3_g=m_ln3_g, ln3_b=m_ln3_b)
    Vo = dict(rel_bias=v_rel_bias, w_in=v_w_in, ml_conv_w=v_ml_conv_w, ml_conv_b=v_ml_conv_b, ml_i_bias=v_ml_i_bias,
              ml_f_bias=v_ml_f_bias, ml_norm_g=v_ml_norm_g, swa_sinks=v_swa_sinks, w_out=v_w_out, ln1_g=v_ln1_g, ln1_b=v_ln1_b,
              xa_wq=v_xa_wq, xa_wkv=v_xa_wkv, xa_wo=v_xa_wo, ln2_g=v_ln2_g, ln2_b=v_ln2_b, ffn_w_up=v_ffn_w_up,
              ffn_conv_w=v_ffn_conv_w, ffn_conv_b=v_ffn_conv_b, ffn_w_down=v_ffn_w_down, ln3_g=v_ln3_g, ln3_b=v_ln3_b)
    S = x.shape[1]
    c_me = lax.axis_index("c")
    me = 4 * lax.axis_index("x") + 2 * lax.axis_index("y") + c_me
    c_idx = jnp.reshape(c_me, (1,)).astype(jnp.int32)
    xs = x.reshape(S, D)
    mems = mem.reshape(mem.shape[1], D)
    tgt = loss_target.reshape(S, D)

    jme = 2 * lax.axis_index("x") + lax.axis_index("y")

    sm_flat, sm_rows = _pack([W[n] for n in SHARDED_SMALL], 8)
    first = _gather([w_in[0].astype(BF16), sm_flat], name="gather_first")
    rest0 = _gather_start([W[n][0].astype(BF16) for n in REST], me, name="gather_rest0_start")
    full = [{"w_in": _layer_full(first[0], "w_in")}, None]
    conv_w = {n: _full_from_shards(s, n) for n, s in zip(SHARDED_SMALL, _unpack(first[1], sm_rows, [W[n].shape for n in SHARDED_SMALL]))}

    bucket = jnp.asarray(_t5_buckets())
    bias = _bias_table(rel_bias, bucket, name="bias_table")

    saved = []
    h0, h0b = xs, xs.astype(BF16)
    h0t = h0b.T
    for l in range(DEPTH):
        gbias = _row128(jnp.concatenate([ml_i_bias[l], ml_f_bias[l]]))
        sinks = _row128(swa_sinks[l])
        ng = ml_norm_g[l].reshape(1, ML_W)
        proj = _mm(h0b, full[l]["w_in"], dep=(rest0[-1],) if l == 0 else (), name=f"proj{l}")
        qk = _silu_conv_fwd(proj, conv_w["ml_conv_w"][l], ml_conv_b[l].reshape(1, -1), name=f"mlconv{l}")
        h_ml, cs, ns, ms = _mlstm_fwd(qk, proj, gbias, ng, name=f"mlstm{l}")
        h_sw = _swa_fwd(proj, bias, sinks, name=f"swa{l}")
        dep = ()
        if l == 0:
            landed = _gather_finish(rest0, h_sw, name="gather_rest0")
            full[0].update({n: _layer_full(s, n) for n, s in zip(REST, landed)})
            layer1 = _gather_start([W[n][1].astype(BF16) for n in BIG], me, after=(landed[0],), name="gather_layer1_start")
            dep = (layer1[-1],)
        fw = full[l]
        h1, h1b, h1t, z1 = _mm_res_ln([h_ml, h_sw], fw["w_out"], h0, ln1_g[l].reshape(1, D), ln1_b[l].reshape(1, D),
                                      name=f"mix_out{l}")
        kv = _mm(mems, fw["xa_wkv"], tm=256, dep=dep, name=f"xa_kv{l}")
        o = _xattn_fwd(h1b, fw["xa_wq"], kv, name=f"xattn{l}")
        h2, h2b, h2t, z2 = _mm_res_ln([o], fw["xa_wo"], h1, ln2_g[l].reshape(1, D), ln2_b[l].reshape(1, D), name=f"xa_out{l}")
        act, act_t = _ffn_gate_fwd(h2b, fw["ffn_w_up"], conv_w["ffn_conv_w"][l], ffn_conv_b[l].reshape(1, -1), name=f"ffn_gate{l}")
        h3, h3b, h3t, z3 = _mm_res_ln([act], fw["ffn_w_down"], h2, ln3_g[l].reshape(1, D), ln3_b[l].reshape(1, D),
                                      name=f"ffn_out{l}")
        saved.append(dict(h0t=h0t, proj=proj, qk=qk, cs=cs, ns=ns, ms=ms, h_ml=h_ml, h_sw=h_sw, z1=z1, h1b=h1b, h1t=h1t, kv=kv, o=o,
                          z2=z2, h2b=h2b, h2t=h2t, act_t=act_t, z3=z3, gbias=gbias, sinks=sinks, ng=ng))
        h0, h0b, h0t = h3, h3b, h3t
        if l == 0:
            landed = _gather_finish(layer1, h3b, name="gather_layer1")
            full[1] = {n: _layer_full(s, n) for n, s in zip(BIG, landed)}

    G = {n: [None] * DEPTH for n in NAMES if n != "rel_bias"}
    dbias = [None] * DEPTH
    pending = []
    dz3, dz3b, G["ln3_g"][DEPTH - 1], G["ln3_b"][DEPTH - 1], loss_part = _grad_in(
        [], None, ln=(saved[-1]["z3"], ln3_g[DEPTH - 1].reshape(1, D)), loss=(h0, tgt), name="loss_head")
    for l in reversed(range(DEPTH)):
        sv, fw = saved[l], full[l]
        win = fw["w_in"]
        G["ffn_w_down"][l] = _wgrad(sv["act_t"], dz3b, name=f"d_w_down{l}")
        dupg, dupv, G["ffn_conv_w"][l], G["ffn_conv_b"][l] = _ffn_gate_bwd(
            dz3b, fw["ffn_w_down"], sv["h2b"], fw["ffn_w_up"], conv_w["ffn_conv_w"][l], ffn_conv_b[l].reshape(1, -1),
            name=f"ffn_gate_bwd{l}")
        G["ffn_w_up"][l] = jnp.concatenate([_wgrad(sv["h2t"], dupg, name=f"d_w_up_g{l}"),
                                            _wgrad(sv["h2t"], dupv, name=f"d_w_up_v{l}")], axis=1)
        dep = ()
        if l == 0:
            names = ("ffn_w_up", "ffn_w_down")
            pending.append((names, 0, "ffn0", _reduce_start("ffn0", names, [G[n][0] for n in names], me)))
            dep = (pending[-1][3][-1],)
        dz2, dz2b, G["ln2_g"][l], G["ln2_b"][l] = _grad_in(
            [(dupg, fw["ffn_w_up"], 0), (dupv, fw["ffn_w_up"], 1)], dz3, ln=(sv["z2"], ln2_g[l].reshape(1, D)), dep=dep,
            name=f"d_h2_{l}")
        G["xa_wo"][l] = _mm_tn(sv["o"], dz2b, name=f"d_xa_wo{l}")
        do = _mm(dz2b, fw["xa_wo"], trans_b=True, name=f"d_xa_o{l}")
        dq, dkv = _xattn_bwd(do, sv["h1b"], fw["xa_wq"], sv["kv"], name=f"xattn_bwd{l}")
        G["xa_wkv"][l] = _mm_tn(mems, dkv, name=f"d_xa_wkv{l}")
        G["xa_wq"][l] = _wgrad(sv["h1t"], dq, name=f"d_xa_wq{l}")
        dep = ()
        if l == 0:
            names = ("xa_wq", "xa_wkv", "xa_wo")
            pending.append((names, 0, "xa0", _reduce_start("xa0", names, [G[n][0] for n in names], me)))
            dep = (pending[-1][3][-1],)
        dz1, dz1b, G["ln1_g"][l], G["ln1_b"][l] = _grad_in(
            [(dq, fw["xa_wq"], 0)], dz2, ln=(sv["z1"], ln1_g[l].reshape(1, D)), dep=dep, name=f"d_h1_{l}")
        G["w_out"][l] = jnp.concatenate([_mm_tn(sv["h_ml"], dz1b, name=f"d_w_out_ml{l}"),
                                         _mm_tn(sv["h_sw"], dz1b, name=f"d_w_out_sw{l}")], axis=0)
        dep = ()
        if l == 0:
            pending.append((("w_out",), 0, "out0", _reduce_start("out0", ("w_out",), [G["w_out"][0]], me)))
            dep = (pending[-1][3][-1],)
        dhcat = _mm(dz1b, fw["w_out"], trans_b=True, dep=dep, name=f"d_hcat{l}")
        dsw, dbias[l], dsinks = _swa_bwd(dhcat, sv["proj"], bias, sv["sinks"], name=f"swa_bwd{l}")
        dqk, dml, dgb, dng = _mlstm_bwd(dhcat, sv["qk"], sv["proj"], sv["gbias"], sv["ng"], sv["cs"], sv["ns"], sv["ms"],
                                        name=f"mlstm_bwd{l}")
        dqk_pre, G["ml_conv_w"][l], G["ml_conv_b"][l] = _silu_conv_bwd(
            dqk, sv["proj"], conv_w["ml_conv_w"][l], ml_conv_b[l].reshape(1, -1), name=f"mlconv_bwd{l}")
        dw_qk = _wgrad(sv["h0t"], dqk_pre, name=f"d_w_in_qk{l}")
        dw_ml = _wgrad(sv["h0t"], dml, name=f"d_w_in_ml{l}")
        dw_sw = _wgrad(sv["h0t"], dsw, name=f"d_w_in_sw{l}")
        G["w_in"][l] = jnp.concatenate([dw_qk, dw_ml[:, :2 * ML_W + 2 * ML_H], dw_sw], axis=1)
        win_ml = jnp.concatenate([win[:, 1024:2048], win[:, 2816:2944]], axis=1)
        pairs = [(dqk_pre, win, 0), (dml, win_ml, 0), (dsw, win[:, 2048:2816], 0)]
        G["ml_i_bias"][l] = dgb[0, :ML_H]
        G["ml_f_bias"][l] = dgb[0, ML_H:2 * ML_H]
        G["ml_norm_g"][l] = dng
        G["swa_sinks"][l] = dsinks[0, :SW_H]
        if l > 0:
            pending.append((BIG, l, f"l{l}", _reduce_start(f"l{l}", BIG, [G[n][l] for n in BIG], me)))
            dz3, dz3b, G["ln3_g"][l - 1], G["ln3_b"][l - 1] = _grad_in(
                pairs, dz1, ln=(saved[l - 1]["z3"], ln3_g[l - 1].reshape(1, D)), dep=(pending[-1][3][-1],), name=f"d_h0_{l}")
        else:
            grad_x = _grad_in(pairs, dz1, name="d_h0_0").reshape(x.shape)

    names = ("w_in",)
    parts = {(n, 0): p for n, p in zip(names, _reduce_now("in0", names, [G[n][0] for n in names], c_idx))}
    for names, l, tag, started in pending:
        landed = _split_wait(started, parts["w_in", 0], _scatter_copies, name=f"reduce_{tag}_wait")
        parts.update({(n, l): p for n, p in zip(names, landed)})
    res = {}
    for n in BIG:
        outs = None
        for l in reversed(range(DEPTH)):
            outs = _adamw(parts[n, l], W[n], Mo[n], Vo[n], layer=l, prev=outs, name=f"adamw_{n}{l}")
        for kind, a in zip(("g", "d", "m", "v"), outs):
            res[kind, n] = a

    Gf = {n: jnp.stack([g.reshape(W[n].shape[1:]) if n in REPLICATED else g for g in G[n]]) for n in G if n not in BIG}
    Gf["rel_bias"] = _bias_table_bwd(dbias, bucket, name="bias_table_bwd")[:, :SW_H]
    res_small, loss = _update_small(Gf, loss_part, W, Mo, Vo, me)
    res.update(res_small)
    return (loss, grad_x, *[res["g", n] for n in NAMES], *[res["d", n] for n in NAMES], *[res["m", n] for n in NAMES],
            *[res["v", n] for n in NAMES])
```

```python
import functools
import math

import jax
import jax.numpy as jnp
import numpy as np
from jax import lax
from jax.experimental import pallas as pl
from jax.experimental.pallas import tpu as pltpu

F32 = jnp.float32
BF16 = jnp.bfloat16

N_DEV = 8
N_CHIP = 4
D = 1024
DEPTH = 2
ML_H = 4
ML_W = 512
ML_DH = 128
ML_L = 64
ML_CONV = 4
SW_DH = 64
SW_W = 512
SW_H = 8
SW_G = 4
SW_KVW = 128
BLK = 128
REL_B = 32
REL_MAXD = 128
XA_H = 4
XA_DH = 256
DFF = 2816
NB_FF = DFF // 128
FFN_CONV = 3
ALPHA = (2.0 * DEPTH) ** 0.25
EPS = 1e-5
N_IN = 2824
NP_IN = 3072
ML_GW = 2 * ML_W + 128
SW_GW = SW_W + 2 * SW_KVW
ADAM_LR = 0.001
ADAM_B1 = 0.9
ADAM_B2 = 0.999
ADAM_EPS = 1e-08
ADAM_WD = 0.01
ADAM_STEP = 10
VMEM_LIMIT = 56 * 1024 * 1024
MESH = pl.DeviceIdType.MESH

NN = ((1,), (0,))
NT = ((1,), (1,))
TN = ((0,), (0,))


def _dg(a, b, dn):
    if a.ndim == 3:
        dims = (((dn[0][0] + 1,), (dn[1][0] + 1,)), ((0,), (0,)))
    else:
        dims = (dn, ((), ()))
    return lax.dot_general(a.astype(BF16), b.astype(BF16), dims, preferred_element_type=F32)


@jax.custom_vjp
def dot_nn(a, b):
    return _dg(a, b, NN)


dot_nn.defvjp(lambda a, b: (_dg(a, b, NN), (a, b)), lambda r, g: (_dg(g, r[1], NT), _dg(r[0], g, TN)))


@jax.custom_vjp
def dot_nt(a, b):
    return _dg(a, b, NT)


dot_nt.defvjp(lambda a, b: (_dg(a, b, NT), (a, b)), lambda r, g: (_dg(g, r[1], NN), _dg(g, r[0], TN)))


@jax.custom_vjp
def dot_tn(a, b):
    return _dg(a, b, TN)


dot_tn.defvjp(lambda a, b: (_dg(a, b, TN), (a, b)), lambda r, g: (_dg(r[1], g, NT), _dg(r[0], g, NN)))


def _params(sem=None):
    return pltpu.CompilerParams(dimension_semantics=sem, vmem_limit_bytes=VMEM_LIMIT)


def _sds(shape, dtype):
    return jax.ShapeDtypeStruct(tuple(shape), dtype)


TOKEN = pl.BlockSpec((8, 128), lambda *_: (0, 0))


def _mm(a, b, *, trans_b=False, out_dtype=F32, add=None, add_scale=1.0, tm=1024, tn=512, dep=(), name):
    a_list = list(a) if isinstance(a, (list, tuple)) else [a]
    M = a_list[0].shape[0]
    N = b.shape[0] if trans_b else b.shape[1]
    tm = min(tm, M)
    tn = next(t for t in (tn, 384, 256, 128) if N % t == 0)
    assert M % tm == 0
    Ka = a_list[0].shape[1]
    assert all(t.shape[1] == Ka for t in a_list)
    tk = next(t for t in (Ka, 1408, 1024) if Ka % t == 0 and t <= 1408)
    na, npa = len(a_list), Ka // tk
    nk = na * npa
    has_add = add is not None

    def body(*refs):
        a_refs, b_ref = refs[:na], refs[na]
        add_ref = refs[na + 1] if has_add else None
        o_ref, acc_ref = refs[-2], refs[-1]
        k = pl.program_id(2)

        def finish(r):
            if has_add:
                r = r + add_scale * add_ref[...].astype(F32)
            o_ref[...] = r.astype(out_dtype)

        for t, a_ref in enumerate(a_refs):
            def step(a_ref=a_ref):
                p = _dg(a_ref[...], b_ref[...], NT if trans_b else NN)
                if nk == 1:
                    finish(p)
                    return

                @pl.when(k == 0)
                def _():
                    acc_ref[...] = p

                @pl.when((k > 0) & (k < nk - 1))
                def _():
                    acc_ref[...] += p

                @pl.when(k == nk - 1)
                def _():
                    finish(acc_ref[...] + p)

            if na == 1:
                step()
            else:
                pl.when((k >= t * npa) & (k < (t + 1) * npa))(step)

    in_specs = [pl.BlockSpec((tm, tk), lambda i, j, k, t=t: (i, jnp.clip(k - t * npa, 0, npa - 1))) for t in range(na)]
    in_specs.append(pl.BlockSpec((tn, tk), lambda i, j, k: (j, k)) if trans_b else pl.BlockSpec((tk, tn), lambda i, j, k: (k, j)))
    args = a_list + [b]
    if has_add:
        in_specs.append(pl.BlockSpec((tm, tn), lambda i, j, k: (i, j)))
        args.append(add)
    in_specs += [TOKEN] * len(dep)
    args += list(dep)
    return pl.pallas_call(
        body, name=name, grid=(M // tm, N // tn, nk), in_specs=in_specs,
        out_specs=pl.BlockSpec((tm, tn), lambda i, j, k: (i, j)), out_shape=_sds((M, N), out_dtype),
        scratch_shapes=[pltpu.VMEM((tm, tn) if nk > 1 else (8, 128), F32)],
        compiler_params=_params(("parallel", "parallel", "arbitrary")))(*args)


def _wgrad(a_t, g, *, name):
    K, N = a_t.shape[0], g.shape[1]
    return _mm(a_t, g, tm=K if K <= 1024 else K // 2, tn=next(t for t in range(1536, 0, -128) if N % t == 0), name=name)


def _mm_tn(a, g, *, name):
    S, K = a.shape
    N = g.shape[1]
    tk = K if K <= 1024 else K // 2
    tn = next(t for t in range(1536, 0, -128) if N % t == 0)
    ts = min(512, S)
    ns = S // ts
    assert K % tk == 0 and S % ts == 0

    def body(a_ref, g_ref, o_ref):
        s = pl.program_id(2)
        p = _dg(a_ref[...], g_ref[...], TN)

        @pl.when(s == 0)
        def _():
            o_ref[...] = p

        @pl.when(s > 0)
        def _():
            o_ref[...] += p

    return pl.pallas_call(
        body, name=name, grid=(K // tk, N // tn, ns),
        in_specs=[pl.BlockSpec((ts, tk), lambda i, j, s: (s, i)), pl.BlockSpec((ts, tn), lambda i, j, s: (s, j))],
        out_specs=pl.BlockSpec((tk, tn), lambda i, j, s: (i, j)), out_shape=_sds((K, N), F32),
        compiler_params=_params(("parallel", "parallel", "arbitrary")))(a, g)


def _mm_res_ln(a_list, w, resid, gam, bet, *, name):
    M, Ka = a_list[0].shape
    na = len(a_list)
    assert w.shape[0] == na * Ka
    tm = min(256, M)

    def body(*refs):
        a_refs, w_refs = refs[:na], refs[na:2 * na]
        r_ref, g_ref, b_ref, y_ref, yb_ref, yt_ref, z_ref = refs[2 * na:]
        z = ALPHA * r_ref[...]
        for a_ref, w_ref in zip(a_refs, w_refs):
            z = z + _dg(a_ref[...], w_ref[...], NN)
        mu = jnp.mean(z, axis=1, keepdims=True)
        zc = z - mu
        var = jnp.mean(zc * zc, axis=1, keepdims=True)
        y = zc * lax.rsqrt(var + EPS) * g_ref[...] + b_ref[...]
        y_ref[...] = y
        yb_ref[...] = y.astype(BF16)
        yt_ref[...] = y.T.astype(BF16)
        z_ref[...] = z

    row = pl.BlockSpec((tm, D), lambda i: (i, 0))
    vec = pl.BlockSpec((1, D), lambda i: (0, 0))
    a_specs = [pl.BlockSpec((tm, Ka), lambda i: (i, 0)) for _ in a_list]
    w_specs = [pl.BlockSpec((Ka, D), lambda i, t=t: (t, 0)) for t in range(na)]
    return pl.pallas_call(
        body, name=name, grid=(M // tm,), in_specs=a_specs + w_specs + [row, vec, vec],
        out_specs=[row, row, pl.BlockSpec((D, tm), lambda i: (0, i)), row],
        out_shape=[_sds((M, D), F32), _sds((M, D), BF16), _sds((D, M), BF16), _sds((M, D), F32)],
        compiler_params=_params(("parallel",)))(*a_list, *([w] * na), resid, gam, bet)


def _grad_in(pairs, add, *, ln=None, loss=None, dep=(), name):
    M = (add if add is not None else loss[0]).shape[0]
    tm = min(256, M)
    npair, nd = len(pairs), len(dep)
    has_ln, has_loss = ln is not None, loss is not None

    def body(*refs):
        n_in = 2 * npair + (2 if has_loss else 1) + 2 * has_ln + nd
        ins, outs = refs[:n_in], refs[n_in:]
        i = pl.program_id(0)
        pos = 2 * npair
        if has_loss:
            e = ins[pos][...] - ins[pos + 1][...]
            pos += 2
            dy = e * (1.0 / D)
            part = 0.5 * jnp.sum(jnp.sum(e * e, axis=1, keepdims=True) * (1.0 / D), axis=0, keepdims=True)
        else:
            dy = ALPHA * ins[pos][...]
            pos += 1
            for t in range(npair):
                dy = dy + _dg(ins[2 * t][...], ins[2 * t + 1][...], NT)
        if not has_ln:
            outs[0][...] = dy
            return
        z, g_ref = ins[pos][...], ins[pos + 1]
        mu = jnp.mean(z, axis=1, keepdims=True)
        zc = z - mu
        var = jnp.mean(zc * zc, axis=1, keepdims=True)
        rstd = lax.rsqrt(var + EPS)
        xh = zc * rstd
        dxh = dy * g_ref[...]
        m1 = jnp.mean(dxh, axis=1, keepdims=True)
        m2 = jnp.mean(dxh * xh, axis=1, keepdims=True)
        dz = rstd * (dxh - m1 - xh * m2)
        outs[0][...] = dz
        outs[1][...] = dz.astype(BF16)
        acc = [(outs[2], jnp.sum(dy * xh, axis=0, keepdims=True)), (outs[3], jnp.sum(dy, axis=0, keepdims=True))]
        if has_loss:
            acc.append((outs[4], jnp.broadcast_to(part, (8, 128))))

        @pl.when(i == 0)
        def _():
            for ref, val in acc:
                ref[...] = val

        @pl.when(i > 0)
        def _():
            for ref, val in acc:
                ref[...] += val

    row = pl.BlockSpec((tm, D), lambda i: (i, 0))
    vec = pl.BlockSpec((1, D), lambda i: (0, 0))
    in_specs, args = [], []
    for a, b, blk in pairs:
        in_specs += [pl.BlockSpec((tm, a.shape[1]), lambda i: (i, 0)), pl.BlockSpec((D, a.shape[1]), lambda i, blk=blk: (0, blk))]
        args += [a, b]
    if has_loss:
        in_specs += [row, row]
        args += list(loss)
    else:
        in_specs.append(row)
        args.append(add)
    if has_ln:
        in_specs += [row, vec]
        args += list(ln)
    in_specs += [TOKEN] * nd
    args += list(dep)
    if has_ln:
        out_specs = [row, row, vec, vec] + ([pl.BlockSpec((8, 128), lambda i: (0, 0))] if has_loss else [])
        out_shape = [_sds((M, D), F32), _sds((M, D), BF16), _sds((1, D), F32), _sds((1, D), F32)] + ([_sds((8, 128), F32)] if has_loss else [])
    else:
        out_specs, out_shape = row, _sds((M, D), F32)
    return pl.pallas_call(
        body, name=name, grid=(M // tm,), in_specs=in_specs, out_specs=out_specs, out_shape=out_shape,
        compiler_params=_params(("arbitrary",) if has_ln else ("parallel",)))(*args)


def _shift_down(x, d):
    if d == 0:
        return x
    rows = lax.broadcasted_iota(jnp.int32, x.shape, 0)
    return jnp.where(rows >= d, pltpu.roll(x, d, 0), 0.0)


def _shift_up(x, d):
    if d == 0:
        return x
    S = x.shape[0]
    rows = lax.broadcasted_iota(jnp.int32, x.shape, 0)
    return jnp.where(rows < S - d, pltpu.roll(x, S - d, 0), 0.0)


def _conv(x, w_ref, b_ref, cs, K):
    y = b_ref[:, cs]
    for j in range(K):
        y = y + _shift_down(x, K - 1 - j) * w_ref[j:j + 1, cs]
    return y


def _conv_bwd(dy, x, w_ref, dw_ref, db_ref, cs, K):
    dx = jnp.zeros_like(x)
    for j in range(K):
        sdy = _shift_up(dy, K - 1 - j)
        dx = dx + sdy * w_ref[j:j + 1, cs]
        dw_ref[j:j + 1, cs] = jnp.sum(sdy * x, axis=0, keepdims=True)
    db_ref[:, cs] = jnp.sum(dy, axis=0, keepdims=True)
    return dx


ALL = slice(None)


def _silu_conv_fwd(proj, cw, cb, *, name):
    S = proj.shape[0]

    def body(x_ref, w_ref, b_ref, o_ref):
        o_ref[...] = jax.nn.silu(_conv(x_ref[...], w_ref, b_ref, ALL, ML_CONV))

    col = pl.BlockSpec((S, 128), lambda j: (0, j))
    return pl.pallas_call(
        body, name=name, grid=(8,),
        in_specs=[col, pl.BlockSpec((ML_CONV, 128), lambda j: (0, j)), pl.BlockSpec((1, 128), lambda j: (0, j))],
        out_specs=col, out_shape=_sds((S, 2 * ML_W), F32), compiler_params=_params(("parallel",)))(proj, cw, cb)


def _silu_conv_bwd(dqk, proj, cw, cb, *, name):
    S = proj.shape[0]

    def body(d_ref, x_ref, w_ref, b_ref, dx_ref, dw_ref, db_ref):
        x = x_ref[...]
        y = _conv(x, w_ref, b_ref, ALL, ML_CONV)
        dy = jax.vjp(jax.nn.silu, y)[1](d_ref[...])[0]
        dx_ref[...] = _conv_bwd(dy, x, w_ref, dw_ref, db_ref, ALL, ML_CONV).astype(BF16)

    col = pl.BlockSpec((S, 128), lambda j: (0, j))
    wsp = pl.BlockSpec((ML_CONV, 128), lambda j: (0, j))
    bsp = pl.BlockSpec((1, 128), lambda j: (0, j))
    return pl.pallas_call(
        body, name=name, grid=(8,), in_specs=[col, col, wsp, bsp], out_specs=[col, wsp, bsp],
        out_shape=[_sds((S, 2 * ML_W), BF16), _sds((ML_CONV, 2 * ML_W), F32), _sds((1, 2 * ML_W), F32)],
        compiler_params=_params(("parallel",)))(dqk, proj, cw, cb)


GELU_C0 = math.sqrt(2.0 / math.pi)
GELU_C1 = 0.044715


def _gate_bwd(ug, uv, da):
    t = jnp.tanh(GELU_C0 * (ug + GELU_C1 * (ug * ug * ug)))
    half = 0.5 * (1.0 + t)
    dgelu = half + 0.5 * ug * (1.0 - t * t) * (GELU_C0 * (1.0 + 3.0 * GELU_C1 * (ug * ug)))
    return da * uv * dgelu, da * (ug * half)


def _up_pair(h_ref, ugw_ref, uvw_ref):
    return _dg(h_ref[...], jnp.concatenate([ugw_ref[...], uvw_ref[...]], axis=1), NN)


def _ffn_specs(S):
    return (pl.BlockSpec((D, 128), lambda j: (0, j)), pl.BlockSpec((D, 128), lambda j: (0, j + NB_FF)),
            pl.BlockSpec((FFN_CONV, 128), lambda j: (0, j)), pl.BlockSpec((FFN_CONV, 128), lambda j: (0, j + NB_FF)),
            pl.BlockSpec((1, 128), lambda j: (0, j)), pl.BlockSpec((1, 128), lambda j: (0, j + NB_FF)))


def _ffn_gate_fwd(hb, w_up, cw, cb, *, name):
    S = hb.shape[0]
    ug_, uv_, wg, wv, bg, bv = _ffn_specs(S)

    def body(h_ref, ugw_ref, uvw_ref, wg_ref, wv_ref, bg_ref, bv_ref, o_ref, ot_ref, h_s):
        @pl.when(pl.program_id(0) == 0)
        def _():
            pltpu.sync_copy(h_ref, h_s)

        x2 = _up_pair(h_s, ugw_ref, uvw_ref)
        ug = _conv(x2[:, :128], wg_ref, bg_ref, ALL, FFN_CONV)
        uv = _conv(x2[:, 128:], wv_ref, bv_ref, ALL, FFN_CONV)
        act = jax.nn.gelu(ug) * uv
        o_ref[...] = act.astype(BF16)
        ot_ref[...] = act.T.astype(BF16)

    return pl.pallas_call(
        body, name=name, grid=(NB_FF,), in_specs=[ANY, ug_, uv_, wg, wv, bg, bv],
        out_specs=[pl.BlockSpec((S, 128), lambda j: (0, j)), pl.BlockSpec((128, S), lambda j: (j, 0))],
        out_shape=[_sds((S, DFF), BF16), _sds((DFF, S), BF16)],
        scratch_shapes=[pltpu.VMEM((S, D), BF16)],
        compiler_params=_params(("arbitrary",)))(hb, w_up, w_up, cw, cw, cb, cb)


def _ffn_gate_bwd(dzb, w_down, hb, w_up, cw, cb, *, name):
    S = hb.shape[0]
    ug_, uv_, wg, wv, bg, bv = _ffn_specs(S)

    def body(dz_ref, h_ref, wd_ref, ugw_ref, uvw_ref, wg_ref, wv_ref, bg_ref, bv_ref,
             dxg_ref, dxv_ref, dwg_ref, dwv_ref, dbg_ref, dbv_ref, dz_s, h_s):
        @pl.when(pl.program_id(0) == 0)
        def _():
            pltpu.sync_copy(dz_ref, dz_s)
            pltpu.sync_copy(h_ref, h_s)

        x2 = _up_pair(h_s, ugw_ref, uvw_ref)
        xg, xv = x2[:, :128], x2[:, 128:]
        ug = _conv(xg, wg_ref, bg_ref, ALL, FFN_CONV)
        uv = _conv(xv, wv_ref, bv_ref, ALL, FFN_CONV)
        dug, duv = _gate_bwd(ug, uv, _dg(dz_s[...], wd_ref[...], NT))
        dxg_ref[...] = _conv_bwd(dug, xg, wg_ref, dwg_ref, dbg_ref, ALL, FFN_CONV).astype(BF16)
        dxv_ref[...] = _conv_bwd(duv, xv, wv_ref, dwv_ref, dbv_ref, ALL, FFN_CONV).astype(BF16)

    col = pl.BlockSpec((S, 128), lambda j: (0, j))
    half = _sds((S, DFF), BF16)
    dxg, dxv, dwg, dwv, dbg, dbv = pl.pallas_call(
        body, name=name, grid=(NB_FF,),
        in_specs=[ANY, ANY, pl.BlockSpec((128, D), lambda j: (j, 0)), ug_, uv_, wg, wv, bg, bv],
        out_specs=[col, col, wg, wg, bg, bg],
        out_shape=[half, half, _sds((FFN_CONV, DFF), F32), _sds((FFN_CONV, DFF), F32), _sds((1, DFF), F32), _sds((1, DFF), F32)],
        scratch_shapes=[pltpu.VMEM((S, D), BF16), pltpu.VMEM((S, D), BF16)],
        compiler_params=_params(("arbitrary",)))(dzb, hb, w_down, w_up, w_up, cw, cw, cb, cb)
    return dxg, dxv, jnp.concatenate([dwg, dwv], axis=1), jnp.concatenate([dbg, dbv], axis=1)


def _log_sigmoid(x):
    return jnp.minimum(x, 0.0) - jnp.log1p(jnp.exp(-jnp.abs(x)))


@jax.custom_vjp
def _clamp_div(num, den, floor, shift):
    return num / jnp.maximum(jnp.abs(den), floor)


def _clamp_div_fwd(num, den, floor, shift):
    out = num / jnp.maximum(jnp.abs(den), floor)
    return out, (den, floor, out)


def _clamp_div_bwd(res, g):
    den, floor, out = res
    active = jnp.abs(den) < floor
    dinv = jnp.maximum(jnp.abs(den), floor)
    go = jnp.sum(g * out, axis=-1, keepdims=True)
    ddiv = -go / dinv
    return (g / dinv, jnp.where(active, 0.0, ddiv * jnp.sign(den)), jnp.where(active, ddiv, 0.0),
            jnp.sum(jnp.where(active, go, 0.0), axis=-2, keepdims=True))


_clamp_div.defvjp(_clamp_div_fwd, _clamp_div_bwd)


def _ml_heads(q, k, v, o_pre, gates, gbias, C, n, ng, m, shift):
    H, L, _ = q.shape
    lane1 = lax.broadcasted_iota(jnp.int32, (1, 128), 1)
    gz = gates + jnp.where(lane1 < ML_H, lax.stop_gradient(gbias), gbias)
    gz = jnp.broadcast_to(gz[None], (H, L, 128))
    hid = lax.broadcasted_iota(jnp.int32, (H, L, 128), 0)
    lane = lax.broadcasted_iota(jnp.int32, (H, L, 128), 2)
    ig = jnp.sum(jnp.where(lane == hid, gz, 0.0), axis=2, keepdims=True)
    lf = _log_sigmoid(jnp.sum(jnp.where(lane == ML_H + hid, gz, 0.0), axis=2, keepdims=True))
    r = lax.broadcasted_iota(jnp.int32, (H, L, L), 1)
    c = lax.broadcasted_iota(jnp.int32, (H, L, L), 2)
    eye, tril = r == c, r >= c

    def to_row(col):
        return jnp.sum(jnp.where(eye, col, 0.0), axis=1, keepdims=True)

    b_col = jnp.sum(jnp.where(tril, to_row(lf), 0.0), axis=2, keepdims=True)
    Dm = jnp.where(tril, b_col - to_row(b_col) + to_row(ig), -jnp.inf)
    inter = b_col + m
    m_t = lax.stop_gradient(jnp.maximum(inter, jnp.max(Dm, axis=2, keepdims=True)))
    w_inter = jnp.exp(inter - m_t)
    ks = k * (ML_DH ** -0.5)
    s = dot_nt(q, ks) * jnp.exp(Dm - m_t)
    num = w_inter * dot_nn(q, C) + dot_nn(s, v)
    den = w_inter * jnp.sum(q * n, axis=2, keepdims=True) + jnp.sum(s, axis=2, keepdims=True)
    h = _clamp_div(num, den, jnp.exp(-m_t), shift)
    g = jnp.sum(lf, axis=1, keepdims=True)
    a = g - b_col + ig
    m_new = lax.stop_gradient(jnp.maximum(g + m, jnp.max(a, axis=1, keepdims=True)))
    decay = jnp.exp(g + m - m_new)
    wk = jnp.exp(a - m_new)
    C_new = decay * C + dot_tn(ks * wk, v)
    n_new = decay * n + jnp.sum(wk * ks, axis=1, keepdims=True)
    mu = jnp.mean(h, axis=2, keepdims=True)
    hc = h - mu
    var = jnp.mean(hc * hc, axis=2, keepdims=True)
    out = jax.nn.sigmoid(o_pre) * (hc * lax.rsqrt(var + EPS) * ng)
    return out, C_new, n_new, m_new


def _hs(h, off=0):
    return slice(off + h * ML_DH, off + (h + 1) * ML_DH)


def _heads(ref, off=0):
    return jnp.stack([ref[:, _hs(h, off)] for h in range(ML_H)])


def _mlstm_fwd(qk, proj, gbias, ng, *, name):
    S = qk.shape[0]
    nc = S // ML_L

    def body(q_ref, k_ref, v_ref, o_ref, g_ref, gb_ref, ng_ref, h_ref, cs_ref, ns_ref, ms_ref, c_s, n_s, m_s):
        @pl.when(pl.program_id(0) == 0)
        def _():
            c_s[...] = jnp.zeros_like(c_s)
            n_s[...] = jnp.zeros_like(n_s)
            m_s[...] = jnp.zeros_like(m_s)

        C, n = c_s[...], n_s[...]
        cs_ref[0] = C
        ns_ref[0] = n
        ms_ref[0] = m_s[...]
        out, C2, n2, m2 = _ml_heads(_heads(q_ref), _heads(k_ref), _heads(v_ref), _heads(o_ref), g_ref[...], gb_ref[...], C, n,
                                    _heads(ng_ref), m_s[:, :, 0:1], jnp.zeros((ML_H, 1, 1), F32))
        for h in range(ML_H):
            h_ref[:, _hs(h)] = out[h].astype(BF16)
        c_s[...] = C2
        n_s[...] = n2
        m_s[...] = jnp.broadcast_to(m2, (ML_H, 1, 128))

    def w(j):
        return pl.BlockSpec((ML_L, ML_W), lambda c, j=j: (c, j))

    return pl.pallas_call(
        body, name=name, grid=(nc,),
        in_specs=[w(0), w(1), w(2), w(3), pl.BlockSpec((ML_L, 128), lambda c: (c, 22)),
                  pl.BlockSpec((1, 128), lambda c: (0, 0)), pl.BlockSpec((1, ML_W), lambda c: (0, 0))],
        out_specs=[w(0), pl.BlockSpec((1, ML_H, ML_DH, ML_DH), lambda c: (c, 0, 0, 0)),
                   pl.BlockSpec((1, ML_H, 1, 128), lambda c: (c, 0, 0, 0)), pl.BlockSpec((1, ML_H, 1, 128), lambda c: (c, 0, 0, 0))],
        out_shape=[_sds((S, ML_W), BF16), _sds((nc, ML_H, ML_DH, ML_DH), F32), _sds((nc, ML_H, 1, 128), F32),
                   _sds((nc, ML_H, 1, 128), F32)],
        scratch_shapes=[pltpu.VMEM((ML_H, ML_DH, ML_DH), F32), pltpu.VMEM((ML_H, 1, 128), F32), pltpu.VMEM((ML_H, 1, 128), F32)],
        compiler_params=_params(("arbitrary",)))(qk, qk, proj, proj, proj, gbias, ng)


def _mlstm_bwd(dh, qk, proj, gbias, ng, cs, ns, ms, *, name):
    S = qk.shape[0]
    nc = S // ML_L

    def body(dh_ref, q_ref, k_ref, v_ref, o_ref, g_ref, gb_ref, ng_ref, cs_ref, ns_ref, ms_ref,
             dqk_ref, dml_ref, dgb_ref, dng_ref, dc_s, dn_s):
        @pl.when(pl.program_id(0) == 0)
        def _():
            dc_s[...] = jnp.zeros_like(dc_s)
            dn_s[...] = jnp.zeros_like(dn_s)
            dgb_ref[...] = jnp.zeros_like(dgb_ref)
            dng_ref[...] = jnp.zeros_like(dng_ref)

        m = ms_ref[0][:, :, 0:1]

        def f(q, k, v, o_pre, gates, gb, C, n, ng_, shift):
            return _ml_heads(q, k, v, o_pre, gates, gb, C, n, ng_, m, shift)[:3]

        _, vjp = jax.vjp(f, _heads(q_ref), _heads(k_ref), _heads(v_ref), _heads(o_ref), g_ref[...], gb_ref[...],
                         cs_ref[0], ns_ref[0], _heads(ng_ref), jnp.zeros((ML_H, 1, 1), F32))
        dq, dk, dv, do, dgates, dgb, dC, dn, dng, dshift = vjp((_heads(dh_ref), dc_s[...], dn_s[...]))
        lane1 = lax.broadcasted_iota(jnp.int32, (1, 128), 1)
        for h in range(ML_H):
            dqk_ref[:, _hs(h)] = dq[h]
            dqk_ref[:, _hs(h, ML_W)] = dk[h]
            dml_ref[:, _hs(h)] = dv[h].astype(BF16)
            dml_ref[:, _hs(h, ML_W)] = do[h].astype(BF16)
            dng_ref[:, _hs(h)] += dng[h]
            dgb = jnp.where(lane1 == h, dshift[h], dgb)
        dc_s[...] = dC
        dn_s[...] = dn
        dml_ref[:, 2 * ML_W:] = dgates.astype(BF16)
        dgb_ref[...] += dgb

    def w(j):
        return pl.BlockSpec((ML_L, ML_W), lambda c, j=j: (nc - 1 - c, j))

    vec = pl.BlockSpec((1, 128), lambda c: (0, 0))
    vecw = pl.BlockSpec((1, ML_W), lambda c: (0, 0))
    gsp = pl.BlockSpec((ML_L, 128), lambda c: (nc - 1 - c, 22))
    st = pl.BlockSpec((1, ML_H, 1, 128), lambda c: (nc - 1 - c, 0, 0, 0))
    return pl.pallas_call(
        body, name=name, grid=(nc,),
        in_specs=[w(0), w(0), w(1), w(2), w(3), gsp, vec, vecw,
                  pl.BlockSpec((1, ML_H, ML_DH, ML_DH), lambda c: (nc - 1 - c, 0, 0, 0)), st, st],
        out_specs=[pl.BlockSpec((ML_L, 2 * ML_W), lambda c: (nc - 1 - c, 0)), pl.BlockSpec((ML_L, ML_GW), lambda c: (nc - 1 - c, 0)),
                   vec, vecw],
        out_shape=[_sds((S, 2 * ML_W), F32), _sds((S, ML_GW), BF16), _sds((1, 128), F32), _sds((1, ML_W), F32)],
        scratch_shapes=[pltpu.VMEM((ML_H, ML_DH, ML_DH), F32), pltpu.VMEM((ML_H, 1, 128), F32)],
        compiler_params=_params(("arbitrary",)))(dh, qk, qk, proj, proj, proj, gbias, ng, cs, ns, ms)


def _t5_buckets():
    r = np.arange(BLK)[:, None]
    c = np.arange(2 * BLK)[None, :]
    n = np.maximum(BLK + r - c, 0)
    max_exact = REL_B // 2
    nf = np.maximum(n, 1).astype(np.float32)
    large = max_exact + (np.log(nf / np.float32(max_exact)) / np.float32(math.log(REL_MAXD / max_exact))
                         * np.float32(REL_B - max_exact)).astype(np.int32)
    large = np.minimum(large, REL_B - 1)
    return np.where(n < max_exact, n, large).astype(np.int32)


def _bias_table(rel_bias, bucket, *, name):
    def body(rb_ref, bk_ref, o_ref):
        bk = bk_ref[...]
        for h in range(SW_H):
            acc = jnp.zeros((BLK, 2 * BLK), F32)
            for b in range(REL_B):
                acc = jnp.where(bk == b, rb_ref[b, h], acc)
            o_ref[h] = acc

    return pl.pallas_call(
        body, name=name, in_specs=[pl.BlockSpec(memory_space=pltpu.SMEM), pl.BlockSpec(memory_space=pltpu.VMEM)],
        out_specs=pl.BlockSpec(memory_space=pltpu.VMEM), out_shape=_sds((SW_H, BLK, 2 * BLK), F32),
        compiler_params=_params())(rel_bias, bucket)


def _bias_table_bwd(dbias_list, bucket, *, name):
    nl = len(dbias_list)

    def body(*refs):
        d_refs, bk_ref, o_ref = refs[:nl], refs[nl], refs[nl + 1]
        bk = bk_ref[...]
        rows = lax.broadcasted_iota(jnp.int32, (REL_B, 128), 0)
        lanes = lax.broadcasted_iota(jnp.int32, (REL_B, 128), 1)
        acc = jnp.zeros((REL_B, 128), F32)
        for h in range(SW_H):
            d = d_refs[0][h]
            for d_ref in d_refs[1:]:
                d = d + d_ref[h]
            for b in range(REL_B):
                t = jnp.sum(jnp.sum(jnp.where(bk == b, d, 0.0), axis=0, keepdims=True), axis=1, keepdims=True)
                acc = jnp.where((rows == b) & (lanes == h), t, acc)
        o_ref[...] = acc

    vm = pl.BlockSpec(memory_space=pltpu.VMEM)
    return pl.pallas_call(
        body, name=name, in_specs=[vm] * (nl + 1), out_specs=vm, out_shape=_sds((REL_B, 128), F32),
        compiler_params=_params())(*dbias_list, bucket)


def _swa_heads(q, kp, kc, vp, vc, bp, bc, sinks, has_prev):
    def rep(t):
        return jnp.concatenate([t[g:g + 1] for g in range(SW_H // SW_G) for _ in range(SW_G)], axis=0)

    r = lax.broadcasted_iota(jnp.int32, (SW_H, BLK, BLK), 1)
    c = lax.broadcasted_iota(jnp.int32, (SW_H, BLK, BLK), 2)
    hid = lax.broadcasted_iota(jnp.int32, (SW_H, 1, 128), 0)
    lane = lax.broadcasted_iota(jnp.int32, (SW_H, 1, 128), 2)
    sink = jnp.sum(jnp.where(lane == hid, jnp.broadcast_to(sinks[None], (SW_H, 1, 128)), 0.0), axis=2, keepdims=True)
    lp = jnp.where((c > r) & has_prev, dot_nt(q, rep(kp)) * (SW_DH ** -0.5) + bp, -jnp.inf)
    lc = jnp.where(c <= r, dot_nt(q, rep(kc)) * (SW_DH ** -0.5) + bc, -jnp.inf)
    mx = jnp.maximum(jnp.maximum(jnp.max(lp, axis=2, keepdims=True), jnp.max(lc, axis=2, keepdims=True)), sink)
    mx = lax.stop_gradient(mx)
    pp, pc = jnp.exp(lp - mx), jnp.exp(lc - mx)
    den = jnp.sum(pp, axis=2, keepdims=True) + jnp.sum(pc, axis=2, keepdims=True) + jnp.exp(sink - mx)
    return dot_nn(pp / den, rep(vp)) + dot_nn(pc / den, rep(vc))


def _qs(h, off=0):
    return slice(off + h * SW_DH, off + (h + 1) * SW_DH)


def _split(ref, n):
    return jnp.stack([ref[:, _qs(h)] for h in range(n)])


def _swa_fwd(proj, bias, sinks, *, name):
    S = proj.shape[0]
    nb = S // BLK
    nkv = SW_H // SW_G

    def body(q_ref, kp_ref, kc_ref, vp_ref, vc_ref, b_ref, s_ref, o_ref):
        out = _swa_heads(_split(q_ref, SW_H), _split(kp_ref, nkv), _split(kc_ref, nkv), _split(vp_ref, nkv), _split(vc_ref, nkv),
                         b_ref[:, :, :BLK], b_ref[:, :, BLK:], s_ref[...], pl.program_id(0) > 0)
        for h in range(SW_H):
            o_ref[:, _qs(h)] = out[h].astype(BF16)

    def cur(j):
        return pl.BlockSpec((BLK, 128), lambda n, j=j: (n, j))

    def prev(j):
        return pl.BlockSpec((BLK, 128), lambda n, j=j: (jnp.maximum(n - 1, 0), j))

    return pl.pallas_call(
        body, name=name, grid=(nb,),
        in_specs=[pl.BlockSpec((BLK, SW_W), lambda n: (n, 4)), prev(20), cur(20), prev(21), cur(21),
                  pl.BlockSpec((SW_H, BLK, 2 * BLK), lambda n: (0, 0, 0)), pl.BlockSpec((1, 128), lambda n: (0, 0))],
        out_specs=pl.BlockSpec((BLK, SW_W), lambda n: (n, 0)), out_shape=_sds((S, SW_W), BF16),
        compiler_params=_params(("parallel",)))(proj, proj, proj, proj, proj, bias, sinks)


def _swa_bwd(dh, proj, bias, sinks, *, name):
    S = proj.shape[0]
    nb = S // BLK

    nkv = SW_H // SW_G

    def body(dh_ref, q_ref, kp_ref, kc_ref, vp_ref, vc_ref, b_ref, s_ref, dsw_ref, db_ref, ds_ref, ck_s, cv_s):
        i = pl.program_id(0)

        @pl.when(i == 0)
        def _():
            ck_s[...] = jnp.zeros_like(ck_s)
            cv_s[...] = jnp.zeros_like(cv_s)
            db_ref[...] = jnp.zeros_like(db_ref)
            ds_ref[...] = jnp.zeros_like(ds_ref)

        f = functools.partial(_swa_heads, has_prev=i < nb - 1)
        _, vjp = jax.vjp(f, _split(q_ref, SW_H), _split(kp_ref, nkv), _split(kc_ref, nkv), _split(vp_ref, nkv),
                         _split(vc_ref, nkv), b_ref[:, :, :BLK], b_ref[:, :, BLK:], s_ref[...])
        dq, dkp, dkc, dvp, dvc, dbp, dbc, ds = vjp(_split(dh_ref, SW_H))
        for h in range(SW_H):
            dsw_ref[:, _qs(h)] = dq[h].astype(BF16)
        for g in range(nkv):
            dsw_ref[:, _qs(g, SW_W)] = (dkc[g] + ck_s[:, _qs(g)]).astype(BF16)
            dsw_ref[:, _qs(g, SW_W + SW_KVW)] = (dvc[g] + cv_s[:, _qs(g)]).astype(BF16)
            ck_s[:, _qs(g)] = dkp[g]
            cv_s[:, _qs(g)] = dvp[g]
        db_ref[:, :, :BLK] += dbp
        db_ref[:, :, BLK:] += dbc
        ds_ref[...] += ds

    def cur(j):
        return pl.BlockSpec((BLK, 128), lambda i, j=j: (nb - 1 - i, j))

    def prev(j):
        return pl.BlockSpec((BLK, 128), lambda i, j=j: (jnp.maximum(nb - 2 - i, 0), j))

    bsp = pl.BlockSpec((SW_H, BLK, 2 * BLK), lambda i: (0, 0, 0))
    vec = pl.BlockSpec((1, 128), lambda i: (0, 0))
    return pl.pallas_call(
        body, name=name, grid=(nb,),
        in_specs=[pl.BlockSpec((BLK, SW_W), lambda i: (nb - 1 - i, 1)), pl.BlockSpec((BLK, SW_W), lambda i: (nb - 1 - i, 4)),
                  prev(20), cur(20), prev(21), cur(21), bsp, vec],
        out_specs=[pl.BlockSpec((BLK, SW_GW), lambda i: (nb - 1 - i, 0)), bsp, vec],
        out_shape=[_sds((S, SW_GW), BF16), _sds((SW_H, BLK, 2 * BLK), F32), _sds((1, 128), F32)],
        scratch_shapes=[pltpu.VMEM((BLK, 128), F32)] * 2,
        compiler_params=_params(("arbitrary",)))(dh, proj, proj, proj, proj, proj, bias, sinks)


XA_TM = 512


def _xa_head(qh, kh, vh):
    logits = dot_nt(qh, kh) * (XA_DH ** -0.5)
    mx = lax.stop_gradient(jnp.max(logits, axis=1, keepdims=True))
    e = jnp.exp(logits - mx)
    return dot_nn(e / jnp.sum(e, axis=1, keepdims=True), vh)


def _xs(h, off=0):
    return slice(off + h * XA_DH, off + (h + 1) * XA_DH)


def _xattn_fwd(hb, wq, kv, *, name):
    S = hb.shape[0]
    M = kv.shape[0]

    def body(h_ref, wq_ref, kv_ref, o_ref):
        q = _dg(h_ref[...], wq_ref[...], NN)
        for h in range(XA_H):
            o_ref[:, _xs(h)] = _xa_head(q[:, _xs(h)], kv_ref[:, _xs(h)], kv_ref[:, _xs(h, D)]).astype(BF16)

    tm = min(XA_TM, S)
    row = pl.BlockSpec((tm, D), lambda i: (i, 0))
    return pl.pallas_call(
        body, name=name, grid=(S // tm,),
        in_specs=[row, pl.BlockSpec((D, D), lambda i: (0, 0)), pl.BlockSpec((M, 2 * D), lambda i: (0, 0))], out_specs=row,
        out_shape=_sds((S, D), BF16), compiler_params=_params(("parallel",)))(hb, wq, kv)


def _xattn_bwd(dzb, wo, hb, wq, kv, *, name):
    S = hb.shape[0]
    M = kv.shape[0]

    def body(dz_ref, wo_ref, h_ref, wq_ref, kv_ref, dq_ref, dkv_ref):
        @pl.when(pl.program_id(0) == 0)
        def _():
            dkv_ref[...] = jnp.zeros_like(dkv_ref)

        q = _dg(h_ref[...], wq_ref[...], NN)
        do = _dg(dz_ref[...], wo_ref[...], NT)
        for h in range(XA_H):
            _, vjp = jax.vjp(_xa_head, q[:, _xs(h)], kv_ref[:, _xs(h)], kv_ref[:, _xs(h, D)])
            dq, dk, dv = vjp(do[:, _xs(h)])
            dq_ref[:, _xs(h)] = dq.astype(BF16)
            dkv_ref[:, _xs(h)] += dk
            dkv_ref[:, _xs(h, D)] += dv

    tm = min(XA_TM, S)
    row = pl.BlockSpec((tm, D), lambda i: (i, 0))
    full = pl.BlockSpec((M, 2 * D), lambda i: (0, 0))
    sq = pl.BlockSpec((D, D), lambda i: (0, 0))
    return pl.pallas_call(
        body, name=name, grid=(S // tm,), in_specs=[row, sq, row, sq, full],
        out_specs=[row, full], out_shape=[_sds((S, D), BF16), _sds((M, 2 * D), F32)],
        compiler_params=_params(("arbitrary",)))(dzb, wo, hb, wq, kv)


ANY = pl.BlockSpec(memory_space=pl.ANY)


def _place():
    x, y, c = lax.axis_index("x"), lax.axis_index("y"), lax.axis_index("c")
    chips = [(1 - x, y), (x, 1 - y), (1 - x, 1 - y)]
    return x, y, c, chips


def _gather(arrs, *, name):
    n = len(arrs)

    def body(*refs):
        srcs, outs = refs[:n], refs[n:2 * n]
        send_sems, recv_sems, local_sems = refs[2 * n:]
        x, y, c, chips = _place()
        me, sib = (x, y, c), (x, y, 1 - c)

        def idx(p):
            return 4 * p[0] + 2 * p[1] + p[2]

        def copy(i, k, block, to, from_src=False):
            return pltpu.make_async_remote_copy(
                src_ref=srcs[i] if from_src else outs[i].at[idx(block)], dst_ref=outs[i].at[idx(block)],
                send_sem=send_sems.at[7 * i + k], recv_sem=recv_sems.at[7 * i + k], device_id=to, device_id_type=MESH)

        local = [pltpu.make_async_copy(srcs[i], outs[i].at[idx(me)], local_sems.at[i]) for i in range(n)]
        for cp in local:
            cp.start()
        first = []
        for i in range(n):
            first.append(copy(i, 0, me, sib, True))
            first += [copy(i, 1 + j, me, (*chip, c), True) for j, chip in enumerate(chips)]
        for cp in first:
            cp.start()
        passed = []
        for j, chip in enumerate(chips):
            for i in range(n):
                copy(i, 1 + j, (*chip, c), me).wait_recv()
                cp = copy(i, 4 + j, (*chip, c), sib)
                cp.start()
                passed.append(cp)
        for i in range(n):
            copy(i, 0, sib, me).wait_recv()
        for j, chip in enumerate(chips):
            for i in range(n):
                copy(i, 4 + j, (*chip, 1 - c), me).wait_recv()
        for cp in first + passed:
            cp.wait_send()
        for cp in local:
            cp.wait()

    return pl.pallas_call(
        body, name=name, in_specs=[ANY] * n, out_specs=[ANY] * n,
        out_shape=[_sds((N_DEV,) + a.shape, a.dtype) for a in arrs],
        scratch_shapes=[pltpu.SemaphoreType.DMA((7 * n,)), pltpu.SemaphoreType.DMA((7 * n,)), pltpu.SemaphoreType.DMA((n,))],
        compiler_params=pltpu.CompilerParams(has_side_effects=True))(*arrs)


def _swap_sibling(arrs, *, name):
    n = len(arrs)

    def body(*refs):
        srcs, outs = refs[:n], refs[n:2 * n]
        send_sems, recv_sems = refs[2 * n:]
        x, y, c, _ = _place()
        copies = [pltpu.make_async_remote_copy(
            src_ref=srcs[i].at[2 * j + (1 - c)], dst_ref=outs[i].at[j], send_sem=send_sems.at[N_CHIP * i + j],
            recv_sem=recv_sems.at[N_CHIP * i + j], device_id=(x, y, 1 - c), device_id_type=MESH)
            for i in range(n) for j in range(N_CHIP)]
        for cp in copies:
            cp.start()
        for cp in copies:
            cp.wait()

    return pl.pallas_call(
        body, name=name, in_specs=[ANY] * n, out_specs=[ANY] * n,
        out_shape=[_sds((N_CHIP,) + a.shape[1:], a.dtype) for a in arrs],
        scratch_shapes=[pltpu.SemaphoreType.DMA((N_CHIP * n,)), pltpu.SemaphoreType.DMA((N_CHIP * n,))],
        compiler_params=pltpu.CompilerParams(has_side_effects=True))(*arrs)


def _pair_sum(part, got, c_idx, *, name):
    _, R, C = part.shape
    tr = next(t for t in (R, 512, 256, 128, 64, 32, 16) if R % t == 0 and t * C <= 1024 * 1024)

    def body(c_ref, p_ref, g_ref, o_ref):
        o_ref[...] = (p_ref[0].astype(F32) + g_ref[...].astype(F32)).astype(o_ref.dtype)

    return pl.pallas_call(
        body, name=name,
        grid_spec=pltpu.PrefetchScalarGridSpec(
            num_scalar_prefetch=1, grid=(N_CHIP, R // tr),
            in_specs=[pl.BlockSpec((1, 1, tr, C), lambda j, r, c_ref: (j, c_ref[0], r, 0)),
                      pl.BlockSpec((1, tr, C), lambda j, r, c_ref: (j, r, 0))],
            out_specs=pl.BlockSpec((1, tr, C), lambda j, r, c_ref: (j, r, 0))),
        out_shape=_sds((N_CHIP, R, C), part.dtype),
        compiler_params=_params(("parallel", "parallel")))(c_idx, part.reshape(N_CHIP, 2, R, C), got)


def _swap_chips(arrs, *, name):
    n = len(arrs)

    def body(*refs):
        srcs, outs = refs[:n], refs[n:2 * n]
        send_sems, recv_sems, local_sems = refs[2 * n:]
        x, y, c, chips = _place()
        jme = 2 * x + y
        local = [pltpu.make_async_copy(srcs[i].at[jme], outs[i].at[jme], local_sems.at[i]) for i in range(n)]
        for cp in local:
            cp.start()
        copies = [pltpu.make_async_remote_copy(
            src_ref=srcs[i].at[2 * chip[0] + chip[1]], dst_ref=outs[i].at[jme], send_sem=send_sems.at[3 * i + j],
            recv_sem=recv_sems.at[3 * i + j], device_id=(*chip, c), device_id_type=MESH)
            for i in range(n) for j, chip in enumerate(chips)]
        for cp in copies:
            cp.start()
        for i in range(n):
            for j, chip in enumerate(chips):
                pltpu.make_async_remote_copy(
                    src_ref=srcs[i].at[jme], dst_ref=outs[i].at[2 * chip[0] + chip[1]], send_sem=send_sems.at[3 * i + j],
                    recv_sem=recv_sems.at[3 * i + j], device_id=(*chip, c), device_id_type=MESH).wait_recv()
        for cp in copies:
            cp.wait_send()
        for cp in local:
            cp.wait()

    return pl.pallas_call(
        body, name=name, in_specs=[ANY] * n, out_specs=[ANY] * n,
        out_shape=[_sds(a.shape, a.dtype) for a in arrs],
        scratch_shapes=[pltpu.SemaphoreType.DMA((3 * n,)), pltpu.SemaphoreType.DMA((3 * n,)), pltpu.SemaphoreType.DMA((n,))],
        compiler_params=pltpu.CompilerParams(has_side_effects=True))(*arrs)


HBM = pl.BlockSpec(memory_space=pltpu.HBM)
SEM = pl.BlockSpec(memory_space=pltpu.SEMAPHORE)
EFFECT = pltpu.SideEffectType.DATAFLOW_SIDE_EFFECTING


def _near_copies(srcs, lands, send_sems, recv_sems):
    x, y, c, chips = _place()
    me = 4 * x + 2 * y + c
    out = []
    for i in range(len(srcs)):
        for k, (px, py, pc) in enumerate([(x, y, 1 - c)] + [(*chip, c) for chip in chips]):
            out.append(tuple(pltpu.make_async_remote_copy(
                src_ref=srcs[i], dst_ref=lands[i].at[slot], send_sem=send_sems.at[4 * i + k], recv_sem=recv_sems.at[4 * i + k],
                device_id=(px, py, pc), device_id_type=MESH) for slot in (me, 4 * px + 2 * py + pc)))
    return out


def _forward_sibling(lands, *, name):
    n = len(lands)

    def body(*refs):
        bufs = refs[n:2 * n]
        send_sems, recv_sems = refs[2 * n:]
        x, y, c, chips = _place()
        copies = [tuple(pltpu.make_async_remote_copy(
            src_ref=bufs[i].at[4 * chip[0] + 2 * chip[1] + c], dst_ref=bufs[i].at[4 * chip[0] + 2 * chip[1] + cc],
            send_sem=send_sems.at[3 * i + j], recv_sem=recv_sems.at[3 * i + j], device_id=(x, y, 1 - c), device_id_type=MESH)
            for cc in (c, 1 - c)) for i in range(n) for j, chip in enumerate(chips)]
        for send, _ in copies:
            send.start()
        for send, recv in copies:
            send.wait_send()
            recv.wait_recv()

    return pl.pallas_call(
        body, name=name, in_specs=[ANY] * n, out_specs=[ANY] * n, out_shape=[_sds(a.shape, a.dtype) for a in lands],
        input_output_aliases={i: i for i in range(n)},
        scratch_shapes=[pltpu.SemaphoreType.DMA((3 * n,)), pltpu.SemaphoreType.DMA((3 * n,))],
        compiler_params=pltpu.CompilerParams(has_side_effects=True))(*lands)


def _scatter_copies(srcs, lands, send_sems, recv_sems):
    x, y, c, _ = _place()
    me = 4 * x + 2 * y + c
    out = []
    for i in range(len(srcs)):
        for j in range(1, N_DEV):
            px, py, pc = x ^ ((j >> 2) & 1), y ^ ((j >> 1) & 1), c ^ (j & 1)
            p = 4 * px + 2 * py + pc
            k = (N_DEV - 1) * i + j - 1
            out.append(tuple(pltpu.make_async_remote_copy(
                src_ref=srcs[i].at[s], dst_ref=lands[i].at[d], send_sem=send_sems.at[k], recv_sem=recv_sems.at[k],
                device_id=(px, py, pc), device_id_type=MESH) for s, d in ((p, me), (me, p))))
    return out


def _split_start(srcs, lands, pattern, ncopy, *, after=(), name):
    n = len(srcs)
    na = len(after)

    def body(*refs):
        sems = refs[2 * n + na:]
        for send, _ in pattern(refs[:n], refs[n:2 * n], sems[0], sems[1]):
            send.start()
        refs[-1][...] = jnp.zeros_like(refs[-1])

    arrs = list(srcs) + list(lands)
    return pl.pallas_call(
        body, name=name,
        out_shape=(pltpu.SemaphoreType.DMA((ncopy,)), pltpu.SemaphoreType.DMA((ncopy,)),
                   *[pltpu.HBM(a.shape, a.dtype) for a in arrs], _sds((8, 128), F32)),
        in_specs=[HBM] * (2 * n) + [ANY] * na, out_specs=(SEM, SEM, *[HBM] * (2 * n), pl.BlockSpec(memory_space=pltpu.VMEM)),
        input_output_aliases={i: 2 + i for i in range(2 * n)},
        compiler_params=pltpu.CompilerParams(has_side_effects=EFFECT))(
            *[pltpu.with_memory_space_constraint(a, pltpu.HBM) for a in arrs], *after)


def _split_wait(started, after, pattern, *, name):
    send_sems, recv_sems, *arrs = started[:-1]
    n = len(arrs) // 2

    def body(*refs):
        for send, recv in pattern(refs[:n], refs[n:2 * n], refs[2 * n], refs[2 * n + 1]):
            send.wait_send()
            recv.wait_recv()

    outs = pl.pallas_call(
        body, name=name, out_shape=tuple(pltpu.HBM(a.shape, a.dtype) for a in arrs),
        in_specs=[HBM] * (2 * n) + [SEM, SEM, ANY], out_specs=tuple([HBM] * (2 * n)),
        input_output_aliases={i: i for i in range(2 * n)},
        compiler_params=pltpu.CompilerParams(has_side_effects=EFFECT))(*arrs, send_sems, recv_sems, after)
    return list(outs[n:])


def _adamw(parts, w, m, v, *, layer=None, prev=None, name):
    P, R, C = parts.shape
    tr = next((t for t in (512, 256, 176, 128, 64, 32, 16, 8) if R % t == 0 and t * C <= 256 * 1024), R)
    c1 = 1.0 / (1.0 - ADAM_B1 ** ADAM_STEP)
    c2 = 1.0 / (1.0 - ADAM_B2 ** ADAM_STEP)
    nprev = 0 if prev is None else 4

    def body(p_ref, w_ref, m_ref, v_ref, *rest):
        g_ref, d_ref, nm_ref, nv_ref = rest[nprev:]
        g = p_ref[0].astype(F32)
        for j in range(1, P):
            g = g + p_ref[j].astype(F32)
        g = g.reshape(w_ref.shape)
        nm = ADAM_B1 * m_ref[...] + (1.0 - ADAM_B1) * g
        nv = ADAM_B2 * v_ref[...] + (1.0 - ADAM_B2) * (g * g)
        g_ref[...] = g
        nm_ref[...] = nm
        nv_ref[...] = nv
        d_ref[...] = -ADAM_LR * ((nm * c1) / (jnp.sqrt(nv * c2) + ADAM_EPS) + ADAM_WD * w_ref[...])

    if layer is None:
        row = pl.BlockSpec((tr, C), lambda i: (i, 0))
    else:
        row = pl.BlockSpec((1, tr, C), lambda i: (layer, i, 0))
    out = _sds(w.shape, F32)
    return pl.pallas_call(
        body, name=name, grid=(R // tr,),
        in_specs=[pl.BlockSpec((P, tr, C), lambda i: (0, i, 0)), row, row, row] + [ANY] * nprev,
        out_specs=[row, row, row, row], out_shape=[out, out, out, out],
        input_output_aliases={4 + k: k for k in range(nprev)},
        compiler_params=_params(("parallel",)))(parts, w, m, v, *(prev or ()))


BIG = ("w_in", "w_out", "xa_wq", "xa_wkv", "xa_wo", "ffn_w_up", "ffn_w_down")
COL_SHARDED = ("w_in", "xa_wkv", "ffn_w_up", "ml_conv_w", "ffn_conv_w")
SHARDED_SMALL = ("ml_conv_w", "ffn_conv_w")
REPLICATED = ("rel_bias", "ml_conv_b", "ml_i_bias", "ml_f_bias", "ml_norm_g", "swa_sinks", "ln1_g", "ln1_b", "ln2_g", "ln2_b",
              "ffn_conv_b", "ln3_g", "ln3_b")
NAMES = ("rel_bias", "w_in", "ml_conv_w", "ml_conv_b", "ml_i_bias", "ml_f_bias", "ml_norm_g", "swa_sinks", "w_out", "ln1_g", "ln1_b",
         "xa_wq", "xa_wkv", "xa_wo", "ln2_g", "ln2_b", "ffn_w_up", "ffn_conv_w", "ffn_conv_b", "ffn_w_down", "ln3_g", "ln3_b")


def _flat_rows(a, mult):
    f = a.reshape(-1)
    n = -(-f.shape[0] // (128 * mult)) * (128 * mult)
    if n != f.shape[0]:
        f = jnp.pad(f, (0, n - f.shape[0]))
    return f.reshape(-1, 128)


def _pack(arrs, mult):
    parts = [_flat_rows(a, mult) for a in arrs]
    return jnp.concatenate(parts, axis=0), [p.shape[0] for p in parts]


def _unpack(flat, rows, shapes):
    out, off = [], 0
    lead = flat.shape[:-2]
    for r, shp in zip(rows, shapes):
        n = int(np.prod(shp))
        piece = flat[..., off:off + r, :].reshape(lead + (r * 128,))[..., :n]
        out.append(piece.reshape(lead + tuple(shp)))
        off += r
    return out


def _full_from_shards(stacked, name):
    if name in COL_SHARDED:
        return jnp.moveaxis(stacked, 0, 2).reshape(stacked.shape[1], stacked.shape[2], N_DEV * stacked.shape[3])
    return jnp.moveaxis(stacked, 0, 1).reshape(stacked.shape[1], N_DEV * stacked.shape[2], stacked.shape[3])


def _shards_from_full(full, name):
    L, A, B = full.shape
    if name in COL_SHARDED:
        return jnp.moveaxis(full.reshape(L, A, N_DEV, B // N_DEV), 2, 0)
    return jnp.moveaxis(full.reshape(L, N_DEV, A // N_DEV, B), 1, 0)


def _pad_win(w):
    z = jnp.zeros(w.shape[:-1] + (NP_IN - N_IN,), w.dtype)
    return jnp.concatenate([w[..., :2048], w[..., 2056:], w[..., 2048:2056], z], axis=-1)


def _row128(v):
    return jnp.pad(v, (0, 128 - v.shape[0])).reshape(1, 128)


REST = BIG[1:]


def _layer_full(stacked, n):
    if n in COL_SHARDED:
        full = jnp.moveaxis(stacked, 0, 1).reshape(stacked.shape[1], N_DEV * stacked.shape[2])
    else:
        full = stacked.reshape(N_DEV * stacked.shape[1], stacked.shape[2])
    return _pad_win(full) if n == "w_in" else full


def _layer_shards(g, n):
    A, B = g.shape
    if n in COL_SHARDED:
        return jnp.moveaxis(g.reshape(A, N_DEV, B // N_DEV), 1, 0).astype(BF16)
    return g.reshape(N_DEV, A // N_DEV, B).astype(BF16)


def _with_own_block(a, idx, nblk):
    return lax.dynamic_update_slice(lax.empty((nblk,) + a.shape, a.dtype), a[None], (idx,) + (0,) * a.ndim)


def _gather_start(arrs, me, *, after=(), name):
    return _split_start(arrs, [_with_own_block(a, me, N_DEV) for a in arrs], _near_copies, 4 * len(arrs), after=after, name=name)


def _gather_finish(started, after, *, name):
    return _forward_sibling(_split_wait(started, after, _near_copies, name=name + "_wait"), name=name + "_forward")


def _reduce_now(tag, names, grads, c_idx):
    send = [_layer_shards(g, n) for n, g in zip(names, grads)]
    got = _swap_sibling(send, name=f"swap_sibling_{tag}")
    sums = [_pair_sum(s, g, c_idx, name=f"pair_sum_{tag}_{n}") for n, s, g in zip(names, send, got)]
    return _swap_chips(sums, name=f"swap_chips_{tag}")


def _reduce_start(tag, names, grads, me):
    send = [_layer_shards(g, n) for n, g in zip(names, grads)]
    lands = [_with_own_block(lax.dynamic_index_in_dim(s, me, 0, keepdims=False), me, N_DEV) for s in send]
    return _split_start(send, lands, _scatter_copies, (N_DEV - 1) * len(send), name=f"reduce_{tag}_start")


def _update_small(Gf, loss_part, W, Mo, Vo, me):
    res = {}
    small = REPLICATED + SHARDED_SMALL
    sp_flat, sp_rows = _pack([Gf[n] for n in small] + [loss_part], 8)
    sp_all = _gather([sp_flat], name="gather_small_grads")[0]

    def widen(n, t):
        if n not in SHARDED_SMALL:
            return t[n]
        return lax.dynamic_update_slice(jnp.zeros(Gf[n].shape, F32), t[n], (0, 0, me * t[n].shape[2]))

    zl = jnp.zeros((8, 128), F32)
    wsm, _ = _pack([widen(n, W) for n in small] + [zl], 8)
    msm, _ = _pack([widen(n, Mo) for n in small] + [zl], 8)
    vsm, _ = _pack([widen(n, Vo) for n in small] + [zl], 8)
    outs_small = [_unpack(o_, sp_rows, [Gf[n].shape for n in small] + [(8, 128)])
                  for o_ in _adamw(sp_all, wsm, msm, vsm, name="adamw_small")]
    for kind, os_ in zip(("g", "d", "m", "v"), outs_small):
        for n, a in zip(small, os_[:-1]):
            if n in SHARDED_SMALL:
                a = lax.dynamic_slice(a, (0, 0, me * W[n].shape[2]), W[n].shape)
            res[kind, n] = a
    return res, outs_small[0][-1][0, 0]


def kernel(x, mem, rel_bias, w_in, ml_conv_w, ml_conv_b, ml_i_bias, ml_f_bias, ml_norm_g, swa_sinks, w_out, ln1_g, ln1_b, xa_wq, xa_wkv, xa_wo, ln2_g, ln2_b, ffn_w_up, ffn_conv_w, ffn_conv_b, ffn_w_down, ln3_g, ln3_b, loss_target, m_rel_bias, m_w_in, m_ml_conv_w, m_ml_conv_b, m_ml_i_bias, m_ml_f_bias, m_ml_norm_g, m_swa_sinks, m_w_out, m_ln1_g, m_ln1_b, m_xa_wq, m_xa_wkv, m_xa_wo, m_ln2_g, m_ln2_b, m_ffn_w_up, m_ffn_conv_w, m_ffn_conv_b, m_ffn_w_down, m_ln3_g, m_ln3_b, v_rel_bias, v_w_in, v_ml_conv_w, v_ml_conv_b, v_ml_i_bias, v_ml_f_bias, v_ml_norm_g, v_swa_sinks, v_w_out, v_ln1_g, v_ln1_b, v_xa_wq, v_xa_wkv, v_xa_wo, v_ln2_g, v_ln2_b, v_ffn_w_up, v_ffn_conv_w, v_ffn_conv_b, v_ffn_w_down, v_ln3_g, v_ln3_b):
    W = dict(rel_bias=rel_bias, w_in=w_in, ml_conv_w=ml_conv_w, ml_conv_b=ml_conv_b, ml_i_bias=ml_i_bias, ml_f_bias=ml_f_bias,
             ml_norm_g=ml_norm_g, swa_sinks=swa_sinks, w_out=w_out, ln1_g=ln1_g, ln1_b=ln1_b, xa_wq=xa_wq, xa_wkv=xa_wkv,
             xa_wo=xa_wo, ln2_g=ln2_g, ln2_b=ln2_b, ffn_w_up=ffn_w_up, ffn_conv_w=ffn_conv_w, ffn_conv_b=ffn_conv_b,
             ffn_w_down=ffn_w_down, ln3_g=ln3_g, ln3_b=ln3_b)
    Mo = dict(rel_bias=m_rel_bias, w_in=m_w_in, ml_conv_w=m_ml_conv_w, ml_conv_b=m_ml_conv_b, ml_i_bias=m_ml_i_bias,
              ml_f_bias=m_ml_f_bias, ml_norm_g=m_ml_norm_g, swa_sinks=m_swa_sinks, w_out=m_w_out, ln1_g=m_ln1_g, ln1_b=m_ln1_b,
              xa_wq=m_xa_wq, xa_wkv=m_xa_wkv, xa_wo=m_xa_wo, ln2_g=m_ln2_g, ln2_b=m_ln2_b, ffn_w_up=m_ffn_w_up,
              ffn_conv_w=m_ffn_conv_w, ffn_conv_b=m_ffn_conv_b, ffn_w_down=m_ffn_w_down, ln3_g=m_ln3_g, ln3_b=m_ln3_b)
    Vo = dict(rel_bias=v_rel_bias, w_in=v_w_in, ml_conv_w=v_ml_conv_w, ml_conv_b=v_ml_conv_b, ml_i_bias=v_ml_i_bias,
              ml_f_bias=v_ml_f_bias, ml_norm_g=v_ml_norm_g, swa_sinks=v_swa_sinks, w_out=v_w_out, ln1_g=v_ln1_g, ln1_b=v_ln1_b,
              xa_wq=v_xa_wq, xa_wkv=v_xa_wkv, xa_wo=v_xa_wo, ln2_g=v_ln2_g, ln2_b=v_ln2_b, ffn_w_up=v_ffn_w_up,
              ffn_conv_w=v_ffn_conv_w, ffn_conv_b=v_ffn_conv_b, ffn_w_down=v_ffn_w_down, ln3_g=v_ln3_g, ln3_b=v_ln3_b)
    S = x.shape[1]
    c_me = lax.axis_index("c")
    me = 4 * lax.axis_index("x") + 2 * lax.axis_index("y") + c_me
    c_idx = jnp.reshape(c_me, (1,)).astype(jnp.int32)
    xs = x.reshape(S, D)
    mems = mem.reshape(mem.shape[1], D)
    tgt = loss_target.reshape(S, D)

    jme = 2 * lax.axis_index("x") + lax.axis_index("y")

    sm_flat, sm_rows = _pack([W[n] for n in SHARDED_SMALL], 8)
    first = _gather([w_in[0].astype(BF16), sm_flat], name="gather_first")
    rest0 = _gather_start([W[n][0].astype(BF16) for n in REST], me, name="gather_rest0_start")
    full = [{"w_in": _layer_full(first[0], "w_in")}, None]
    conv_w = {n: _full_from_shards(s, n) for n, s in zip(SHARDED_SMALL, _unpack(first[1], sm_rows, [W[n].shape for n in SHARDED_SMALL]))}

    bucket = jnp.asarray(_t5_buckets())
    bias = _bias_table(rel_bias, bucket, name="bias_table")

    saved = []
    h0, h0b = xs, xs.astype(BF16)
    h0t = h0b.T
    for l in range(DEPTH):
        gbias = _row128(jnp.concatenate([ml_i_bias[l], ml_f_bias[l]]))
        sinks = _row128(swa_sinks[l])
        ng = ml_norm_g[l].reshape(1, ML_W)
        proj = _mm(h0b, full[l]["w_in"], dep=(rest0[-1],) if l == 0 else (), name=f"proj{l}")
        qk = _silu_conv_fwd(proj, conv_w["ml_conv_w"][l], ml_conv_b[l].reshape(1, -1), name=f"mlconv{l}")
        h_ml, cs, ns, ms = _mlstm_fwd(qk, proj, gbias, ng, name=f"mlstm{l}")
        h_sw = _swa_fwd(proj, bias, sinks, name=f"swa{l}")
        dep = ()
        if l == 0:
            landed = _gather_finish(rest0, h_sw, name="gather_rest0")
            full[0].update({n: _layer_full(s, n) for n, s in zip(REST, landed)})
            layer1 = _gather_start([W[n][1].astype(BF16) for n in BIG], me, after=(landed[0],), name="gather_layer1_start")
            dep = (layer1[-1],)
        fw = full[l]
        h1, h1b, h1t, z1 = _mm_res_ln([h_ml, h_sw], fw["w_out"], h0, ln1_g[l].reshape(1, D), ln1_b[l].reshape(1, D),
                                      name=f"mix_out{l}")
        kv = _mm(mems, fw["xa_wkv"], tm=256, dep=dep, name=f"xa_kv{l}")
        o = _xattn_fwd(h1b, fw["xa_wq"], kv, name=f"xattn{l}")
        h2, h2b, h2t, z2 = _mm_res_ln([o], fw["xa_wo"], h1, ln2_g[l].reshape(1, D), ln2_b[l].reshape(1, D), name=f"xa_out{l}")
        act, act_t = _ffn_gate_fwd(h2b, fw["ffn_w_up"], conv_w["ffn_conv_w"][l], ffn_conv_b[l].reshape(1, -1), name=f"ffn_gate{l}")
        h3, h3b, h3t, z3 = _mm_res_ln([act], fw["ffn_w_down"], h2, ln3_g[l].reshape(1, D), ln3_b[l].reshape(1, D),
                                      name=f"ffn_out{l}")
        saved.append(dict(h0t=h0t, proj=proj, qk=qk, cs=cs, ns=ns, ms=ms, h_ml=h_ml, h_sw=h_sw, z1=z1, h1b=h1b, h1t=h1t, kv=kv, o=o,
                          z2=z2, h2b=h2b, h2t=h2t, act_t=act_t, z3=z3, gbias=gbias, sinks=sinks, ng=ng))
        h0, h0b, h0t = h3, h3b, h3t
        if l == 0:
            landed = _gather_finish(layer1, h3b, name="gather_layer1")
            full[1] = {n: _layer_full(s, n) for n, s in zip(BIG, landed)}

    G = {n: [None] * DEPTH for n in NAMES if n != "rel_bias"}
    dbias = [None] * DEPTH
    pending = []
    dz3, dz3b, G["ln3_g"][DEPTH - 1], G["ln3_b"][DEPTH - 1], loss_part = _grad_in(
        [], None, ln=(saved[-1]["z3"], ln3_g[DEPTH - 1].reshape(1, D)), loss=(h0, tgt), name="loss_head")
    for l in reversed(range(DEPTH)):
        sv, fw = saved[l], full[l]
        win = fw["w_in"]
        G["ffn_w_down"][l] = _wgrad(sv["act_t"], dz3b, name=f"d_w_down{l}")
        dupg, dupv, G["ffn_conv_w"][l], G["ffn_conv_b"][l] = _ffn_gate_bwd(
            dz3b, fw["ffn_w_down"], sv["h2b"], fw["ffn_w_up"], conv_w["ffn_conv_w"][l], ffn_conv_b[l].reshape(1, -1),
            name=f"ffn_gate_bwd{l}")
        G["ffn_w_up"][l] = jnp.concatenate([_wgrad(sv["h2t"], dupg, name=f"d_w_up_g{l}"),
                                            _wgrad(sv["h2t"], dupv, name=f"d_w_up_v{l}")], axis=1)
        dep = ()
        if l == 0:
            names = ("ffn_w_up", "ffn_w_down")
            pending.append((names, 0, "ffn0", _reduce_start("ffn0", names, [G[n][0] for n in names], me)))
            dep = (pending[-1][3][-1],)
        dz2, dz2b, G["ln2_g"][l], G["ln2_b"][l] = _grad_in(
            [(dupg, fw["ffn_w_up"], 0), (dupv, fw["ffn_w_up"], 1)], dz3, ln=(sv["z2"], ln2_g[l].reshape(1, D)), dep=dep,
            name=f"d_h2_{l}")
        G["xa_wo"][l] = _mm_tn(sv["o"], dz2b, name=f"d_xa_wo{l}")
        dq, dkv = _xattn_bwd(dz2b, fw["xa_wo"], sv["h1b"], fw["xa_wq"], sv["kv"], name=f"xattn_bwd{l}")
        G["xa_wkv"][l] = _mm_tn(mems, dkv, name=f"d_xa_wkv{l}")
        G["xa_wq"][l] = _wgrad(sv["h1t"], dq, name=f"d_xa_wq{l}")
        dep = ()
        if l == 0:
            names = ("xa_wq", "xa_wkv", "xa_wo")
            pending.append((names, 0, "xa0", _reduce_start("xa0", names, [G[n][0] for n in names], me)))
            dep = (pending[-1][3][-1],)
        dz1, dz1b, G["ln1_g"][l], G["ln1_b"][l] = _grad_in(
            [(dq, fw["xa_wq"], 0)], dz2, ln=(sv["z1"], ln1_g[l].reshape(1, D)), dep=dep, name=f"d_h1_{l}")
        G["w_out"][l] = jnp.concatenate([_mm_tn(sv["h_ml"], dz1b, name=f"d_w_out_ml{l}"),
                                         _mm_tn(sv["h_sw"], dz1b, name=f"d_w_out_sw{l}")], axis=0)
        dep = ()
        if l == 0:
            pending.append((("w_out",), 0, "out0", _reduce_start("out0", ("w_out",), [G["w_out"][0]], me)))
            dep = (pending[-1][3][-1],)
        dhcat = _mm(dz1b, fw["w_out"], trans_b=True, dep=dep, name=f"d_hcat{l}")
        dsw, dbias[l], dsinks = _swa_bwd(dhcat, sv["proj"], bias, sv["sinks"], name=f"swa_bwd{l}")
        dqk, dml, dgb, dng = _mlstm_bwd(dhcat, sv["qk"], sv["proj"], sv["gbias"], sv["ng"], sv["cs"], sv["ns"], sv["ms"],
                                        name=f"mlstm_bwd{l}")
        dqk_pre, G["ml_conv_w"][l], G["ml_conv_b"][l] = _silu_conv_bwd(
            dqk, sv["proj"], conv_w["ml_conv_w"][l], ml_conv_b[l].reshape(1, -1), name=f"mlconv_bwd{l}")
        dw_qk = _wgrad(sv["h0t"], dqk_pre, name=f"d_w_in_qk{l}")
        dw_ml = _wgrad(sv["h0t"], dml, name=f"d_w_in_ml{l}")
        dw_sw = _wgrad(sv["h0t"], dsw, name=f"d_w_in_sw{l}")
        G["w_in"][l] = jnp.concatenate([dw_qk, dw_ml[:, :2 * ML_W + 2 * ML_H], dw_sw], axis=1)
        win_ml = jnp.concatenate([win[:, 1024:2048], win[:, 2816:2944]], axis=1)
        pairs = [(dqk_pre, win, 0), (dml, win_ml, 0), (dsw, win[:, 2048:2816], 0)]
        G["ml_i_bias"][l] = dgb[0, :ML_H]
        G["ml_f_bias"][l] = dgb[0, ML_H:2 * ML_H]
        G["ml_norm_g"][l] = dng
        G["swa_sinks"][l] = dsinks[0, :SW_H]
        if l > 0:
            pending.append((BIG, l, f"l{l}", _reduce_start(f"l{l}", BIG, [G[n][l] for n in BIG], me)))
            dz3, dz3b, G["ln3_g"][l - 1], G["ln3_b"][l - 1] = _grad_in(
                pairs, dz1, ln=(saved[l - 1]["z3"], ln3_g[l - 1].reshape(1, D)), dep=(pending[-1][3][-1],), name=f"d_h0_{l}")
        else:
            grad_x = _grad_in(pairs, dz1, name="d_h0_0").reshape(x.shape)

    names = ("w_in",)
    parts = {(n, 0): p for n, p in zip(names, _reduce_now("in0", names, [G[n][0] for n in names], c_idx))}
    for names, l, tag, started in pending:
        landed = _split_wait(started, parts["w_in", 0], _scatter_copies, name=f"reduce_{tag}_wait")
        parts.update({(n, l): p for n, p in zip(names, landed)})
    res = {}
    for n in BIG:
        outs = None
        for l in reversed(range(DEPTH)):
            outs = _adamw(parts[n, l], W[n], Mo[n], Vo[n], layer=l, prev=outs, name=f"adamw_{n}{l}")
        for kind, a in zip(("g", "d", "m", "v"), outs):
            res[kind, n] = a

    Gf = {n: jnp.stack([g.reshape(W[n].shape[1:]) if n in REPLICATED else g for g in G[n]]) for n in G if n not in BIG}
    Gf["rel_bias"] = _bias_table_bwd(dbias, bucket, name="bias_table_bwd")[:, :SW_H]
    res_small, loss = _update_small(Gf, loss_part, W, Mo, Vo, me)
    res.update(res_small)
    return (loss, grad_x, *[res["g", n] for n in NAMES], *[res["d", n] for n in NAMES], *[res["m", n] for n in NAMES],
            *[res["v", n] for n in NAMES])
```

```python
import functools
import math

import jax
import jax.numpy as jnp
import numpy as np
from jax import lax
from jax.experimental import pallas as pl
from jax.experimental.pallas import tpu as pltpu

F32 = jnp.float32
BF16 = jnp.bfloat16

N_DEV = 8
D = 1024
DEPTH = 2
ML_H = 4
ML_W = 512
ML_DH = 128
ML_L = 64
ML_CONV = 4
SW_DH = 64
SW_W = 512
SW_H = 8
SW_G = 4
SW_KVW = 128
BLK = 128
REL_B = 32
REL_MAXD = 128
XA_H = 4
XA_DH = 256
DFF = 2816
NB_FF = DFF // 128
FFN_CONV = 3
ALPHA = (2.0 * DEPTH) ** 0.25
EPS = 1e-5
N_IN = 2824
NP_IN = 3072
ML_GW = 2 * ML_W + 128
SW_GW = SW_W + 2 * SW_KVW
ADAM_LR = 0.001
ADAM_B1 = 0.9
ADAM_B2 = 0.999
ADAM_EPS = 1e-08
ADAM_WD = 0.01
ADAM_STEP = 10
VMEM_LIMIT = 56 * 1024 * 1024
MESH = pl.DeviceIdType.MESH

NN = ((1,), (0,))
NT = ((1,), (1,))
TN = ((0,), (0,))


def _dg(a, b, dn):
    if a.ndim == 3:
        dims = (((dn[0][0] + 1,), (dn[1][0] + 1,)), ((0,), (0,)))
    else:
        dims = (dn, ((), ()))
    return lax.dot_general(a.astype(BF16), b.astype(BF16), dims, preferred_element_type=F32)


@jax.custom_vjp
def dot_nn(a, b):
    return _dg(a, b, NN)


dot_nn.defvjp(lambda a, b: (_dg(a, b, NN), (a, b)), lambda r, g: (_dg(g, r[1], NT), _dg(r[0], g, TN)))


@jax.custom_vjp
def dot_nt(a, b):
    return _dg(a, b, NT)


dot_nt.defvjp(lambda a, b: (_dg(a, b, NT), (a, b)), lambda r, g: (_dg(g, r[1], NN), _dg(g, r[0], TN)))


@jax.custom_vjp
def dot_tn(a, b):
    return _dg(a, b, TN)


dot_tn.defvjp(lambda a, b: (_dg(a, b, TN), (a, b)), lambda r, g: (_dg(r[1], g, NT), _dg(r[0], g, NN)))


def _params(sem=None):
    return pltpu.CompilerParams(dimension_semantics=sem, vmem_limit_bytes=VMEM_LIMIT)


def _sds(shape, dtype):
    return jax.ShapeDtypeStruct(tuple(shape), dtype)


TOKEN = pl.BlockSpec((8, 128), lambda *_: (0, 0))


def _mm(a, b, *, trans_b=False, out_dtype=F32, add=None, add_scale=1.0, tm=1024, tn=512, dep=(), name):
    a_list = list(a) if isinstance(a, (list, tuple)) else [a]
    M = a_list[0].shape[0]
    N = b.shape[0] if trans_b else b.shape[1]
    tm = min(tm, M)
    tn = next(t for t in (tn, 384, 256, 128) if N % t == 0)
    assert M % tm == 0
    Ka = a_list[0].shape[1]
    assert all(t.shape[1] == Ka for t in a_list)
    tk = next(t for t in (Ka, 1408, 1024) if Ka % t == 0 and t <= 1408)
    na, npa = len(a_list), Ka // tk
    nk = na * npa
    has_add = add is not None

    def body(*refs):
        a_refs, b_ref = refs[:na], refs[na]
        add_ref = refs[na + 1] if has_add else None
        o_ref, acc_ref = refs[-2], refs[-1]
        k = pl.program_id(2)

        def finish(r):
            if has_add:
                r = r + add_scale * add_ref[...].astype(F32)
            o_ref[...] = r.astype(out_dtype)

        for t, a_ref in enumerate(a_refs):
            def step(a_ref=a_ref):
                p = _dg(a_ref[...], b_ref[...], NT if trans_b else NN)
                if nk == 1:
                    finish(p)
                    return

                @pl.when(k == 0)
                def _():
                    acc_ref[...] = p

                @pl.when((k > 0) & (k < nk - 1))
                def _():
                    acc_ref[...] += p

                @pl.when(k == nk - 1)
                def _():
                    finish(acc_ref[...] + p)

            if na == 1:
                step()
            else:
                pl.when((k >= t * npa) & (k < (t + 1) * npa))(step)

    in_specs = [pl.BlockSpec((tm, tk), lambda i, j, k, t=t: (i, jnp.clip(k - t * npa, 0, npa - 1))) for t in range(na)]
    in_specs.append(pl.BlockSpec((tn, tk), lambda i, j, k: (j, k)) if trans_b else pl.BlockSpec((tk, tn), lambda i, j, k: (k, j)))
    args = a_list + [b]
    if has_add:
        in_specs.append(pl.BlockSpec((tm, tn), lambda i, j, k: (i, j)))
        args.append(add)
    in_specs += [TOKEN] * len(dep)
    args += list(dep)
    return pl.pallas_call(
        body, name=name, grid=(M // tm, N // tn, nk), in_specs=in_specs,
        out_specs=pl.BlockSpec((tm, tn), lambda i, j, k: (i, j)), out_shape=_sds((M, N), out_dtype),
        scratch_shapes=[pltpu.VMEM((tm, tn) if nk > 1 else (8, 128), F32)],
        compiler_params=_params(("parallel", "parallel", "arbitrary")))(*args)


def _wgrad(a_t, g, *, name):
    K, N = a_t.shape[0], g.shape[1]
    return _mm(a_t, g, tm=K if K <= 1024 else K // 2, tn=next(t for t in range(1536, 0, -128) if N % t == 0), name=name)


def _mm_tn(a, g, *, name):
    S, K = a.shape
    N = g.shape[1]
    tk = K if K <= 1024 else K // 2
    tn = next(t for t in range(1536, 0, -128) if N % t == 0)
    ts = min(512, S)
    ns = S // ts
    assert K % tk == 0 and S % ts == 0

    def body(a_ref, g_ref, o_ref):
        s = pl.program_id(2)
        p = _dg(a_ref[...], g_ref[...], TN)

        @pl.when(s == 0)
        def _():
            o_ref[...] = p

        @pl.when(s > 0)
        def _():
            o_ref[...] += p

    return pl.pallas_call(
        body, name=name, grid=(K // tk, N // tn, ns),
        in_specs=[pl.BlockSpec((ts, tk), lambda i, j, s: (s, i)), pl.BlockSpec((ts, tn), lambda i, j, s: (s, j))],
        out_specs=pl.BlockSpec((tk, tn), lambda i, j, s: (i, j)), out_shape=_sds((K, N), F32),
        compiler_params=_params(("parallel", "parallel", "arbitrary")))(a, g)


def _mm_res_ln(a_list, w, resid, gam, bet, *, name):
    M, Ka = a_list[0].shape
    na = len(a_list)
    assert w.shape[0] == na * Ka
    tm = min(256, M)

    def body(*refs):
        a_refs, w_refs = refs[:na], refs[na:2 * na]
        r_ref, g_ref, b_ref, y_ref, yb_ref, yt_ref, z_ref = refs[2 * na:]
        z = ALPHA * r_ref[...]
        for a_ref, w_ref in zip(a_refs, w_refs):
            z = z + _dg(a_ref[...], w_ref[...], NN)
        mu = jnp.mean(z, axis=1, keepdims=True)
        zc = z - mu
        var = jnp.mean(zc * zc, axis=1, keepdims=True)
        y = zc * lax.rsqrt(var + EPS) * g_ref[...] + b_ref[...]
        y_ref[...] = y
        yb_ref[...] = y.astype(BF16)
        yt_ref[...] = y.T.astype(BF16)
        z_ref[...] = z

    row = pl.BlockSpec((tm, D), lambda i: (i, 0))
    vec = pl.BlockSpec((1, D), lambda i: (0, 0))
    a_specs = [pl.BlockSpec((tm, Ka), lambda i: (i, 0)) for _ in a_list]
    w_specs = [pl.BlockSpec((Ka, D), lambda i, t=t: (t, 0)) for t in range(na)]
    return pl.pallas_call(
        body, name=name, grid=(M // tm,), in_specs=a_specs + w_specs + [row, vec, vec],
        out_specs=[row, row, pl.BlockSpec((D, tm), lambda i: (0, i)), row],
        out_shape=[_sds((M, D), F32), _sds((M, D), BF16), _sds((D, M), BF16), _sds((M, D), F32)],
        compiler_params=_params(("parallel",)))(*a_list, *([w] * na), resid, gam, bet)


def _grad_in(pairs, add, *, ln=None, loss=None, dep=(), name):
    M = (add if add is not None else loss[0]).shape[0]
    tm = min(256, M)
    npair, nd = len(pairs), len(dep)
    has_ln, has_loss = ln is not None, loss is not None

    def body(*refs):
        n_in = 2 * npair + (2 if has_loss else 1) + 2 * has_ln + nd
        ins, outs = refs[:n_in], refs[n_in:]
        i = pl.program_id(0)
        pos = 2 * npair
        if has_loss:
            e = ins[pos][...] - ins[pos + 1][...]
            pos += 2
            dy = e * (1.0 / D)
            part = 0.5 * jnp.sum(jnp.sum(e * e, axis=1, keepdims=True) * (1.0 / D), axis=0, keepdims=True)
        else:
            dy = ALPHA * ins[pos][...]
            pos += 1
            for t in range(npair):
                dy = dy + _dg(ins[2 * t][...], ins[2 * t + 1][...], NT)
        if not has_ln:
            outs[0][...] = dy
            return
        z, g_ref = ins[pos][...], ins[pos + 1]
        mu = jnp.mean(z, axis=1, keepdims=True)
        zc = z - mu
        var = jnp.mean(zc * zc, axis=1, keepdims=True)
        rstd = lax.rsqrt(var + EPS)
        xh = zc * rstd
        dxh = dy * g_ref[...]
        m1 = jnp.mean(dxh, axis=1, keepdims=True)
        m2 = jnp.mean(dxh * xh, axis=1, keepdims=True)
        dz = rstd * (dxh - m1 - xh * m2)
        outs[0][...] = dz
        outs[1][...] = dz.astype(BF16)
        acc = [(outs[2], jnp.sum(dy * xh, axis=0, keepdims=True)), (outs[3], jnp.sum(dy, axis=0, keepdims=True))]
        if has_loss:
            acc.append((outs[4], jnp.broadcast_to(part, (8, 128))))

        @pl.when(i == 0)
        def _():
            for ref, val in acc:
                ref[...] = val

        @pl.when(i > 0)
        def _():
            for ref, val in acc:
                ref[...] += val

    row = pl.BlockSpec((tm, D), lambda i: (i, 0))
    vec = pl.BlockSpec((1, D), lambda i: (0, 0))
    in_specs, args = [], []
    for a, b, blk in pairs:
        in_specs += [pl.BlockSpec((tm, a.shape[1]), lambda i: (i, 0)), pl.BlockSpec((D, a.shape[1]), lambda i, blk=blk: (0, blk))]
        args += [a, b]
    if has_loss:
        in_specs += [row, row]
        args += list(loss)
    else:
        in_specs.append(row)
        args.append(add)
    if has_ln:
        in_specs += [row, vec]
        args += list(ln)
    in_specs += [TOKEN] * nd
    args += list(dep)
    if has_ln:
        out_specs = [row, row, vec, vec] + ([pl.BlockSpec((8, 128), lambda i: (0, 0))] if has_loss else [])
        out_shape = [_sds((M, D), F32), _sds((M, D), BF16), _sds((1, D), F32), _sds((1, D), F32)] + ([_sds((8, 128), F32)] if has_loss else [])
    else:
        out_specs, out_shape = row, _sds((M, D), F32)
    return pl.pallas_call(
        body, name=name, grid=(M // tm,), in_specs=in_specs, out_specs=out_specs, out_shape=out_shape,
        compiler_params=_params(("arbitrary",) if has_ln else ("parallel",)))(*args)


def _shift_down(x, d):
    if d == 0:
        return x
    rows = lax.broadcasted_iota(jnp.int32, x.shape, 0)
    return jnp.where(rows >= d, pltpu.roll(x, d, 0), 0.0)


def _shift_up(x, d):
    if d == 0:
        return x
    S = x.shape[0]
    rows = lax.broadcasted_iota(jnp.int32, x.shape, 0)
    return jnp.where(rows < S - d, pltpu.roll(x, S - d, 0), 0.0)


def _conv(x, w_ref, b_ref, cs, K):
    y = b_ref[:, cs]
    for j in range(K):
        y = y + _shift_down(x, K - 1 - j) * w_ref[j:j + 1, cs]
    return y


def _conv_bwd(dy, x, w_ref, dw_ref, db_ref, cs, K):
    dx = jnp.zeros_like(x)
    for j in range(K):
        sdy = _shift_up(dy, K - 1 - j)
        dx = dx + sdy * w_ref[j:j + 1, cs]
        dw_ref[j:j + 1, cs] = jnp.sum(sdy * x, axis=0, keepdims=True)
    db_ref[:, cs] = jnp.sum(dy, axis=0, keepdims=True)
    return dx


ALL = slice(None)


def _silu_conv_fwd(proj, cw, cb, *, name):
    S = proj.shape[0]

    def body(x_ref, w_ref, b_ref, o_ref):
        o_ref[...] = jax.nn.silu(_conv(x_ref[...], w_ref, b_ref, ALL, ML_CONV))

    col = pl.BlockSpec((S, 128), lambda j: (0, j))
    return pl.pallas_call(
        body, name=name, grid=(8,),
        in_specs=[col, pl.BlockSpec((ML_CONV, 128), lambda j: (0, j)), pl.BlockSpec((1, 128), lambda j: (0, j))],
        out_specs=col, out_shape=_sds((S, 2 * ML_W), F32), compiler_params=_params(("parallel",)))(proj, cw, cb)


def _silu_conv_bwd(dqk, proj, cw, cb, *, name):
    S = proj.shape[0]

    def body(d_ref, x_ref, w_ref, b_ref, dx_ref, dw_ref, db_ref):
        x = x_ref[...]
        y = _conv(x, w_ref, b_ref, ALL, ML_CONV)
        dy = jax.vjp(jax.nn.silu, y)[1](d_ref[...])[0]
        dx_ref[...] = _conv_bwd(dy, x, w_ref, dw_ref, db_ref, ALL, ML_CONV).astype(BF16)

    col = pl.BlockSpec((S, 128), lambda j: (0, j))
    wsp = pl.BlockSpec((ML_CONV, 128), lambda j: (0, j))
    bsp = pl.BlockSpec((1, 128), lambda j: (0, j))
    return pl.pallas_call(
        body, name=name, grid=(8,), in_specs=[col, col, wsp, bsp], out_specs=[col, wsp, bsp],
        out_shape=[_sds((S, 2 * ML_W), BF16), _sds((ML_CONV, 2 * ML_W), F32), _sds((1, 2 * ML_W), F32)],
        compiler_params=_params(("parallel",)))(dqk, proj, cw, cb)


GELU_C0 = math.sqrt(2.0 / math.pi)
GELU_C1 = 0.044715


def _gate_bwd(ug, uv, da):
    t = jnp.tanh(GELU_C0 * (ug + GELU_C1 * (ug * ug * ug)))
    half = 0.5 * (1.0 + t)
    dgelu = half + 0.5 * ug * (1.0 - t * t) * (GELU_C0 * (1.0 + 3.0 * GELU_C1 * (ug * ug)))
    return da * uv * dgelu, da * (ug * half)


def _up_pair(h_ref, ugw_ref, uvw_ref):
    return _dg(h_ref[...], jnp.concatenate([ugw_ref[...], uvw_ref[...]], axis=1), NN)


def _ffn_specs(S):
    return (pl.BlockSpec((D, 128), lambda j: (0, j)), pl.BlockSpec((D, 128), lambda j: (0, j + NB_FF)),
            pl.BlockSpec((FFN_CONV, 128), lambda j: (0, j)), pl.BlockSpec((FFN_CONV, 128), lambda j: (0, j + NB_FF)),
            pl.BlockSpec((1, 128), lambda j: (0, j)), pl.BlockSpec((1, 128), lambda j: (0, j + NB_FF)))


def _ffn_gate_fwd(hb, w_up, cw, cb, *, name):
    S = hb.shape[0]
    ug_, uv_, wg, wv, bg, bv = _ffn_specs(S)

    def body(h_ref, ugw_ref, uvw_ref, wg_ref, wv_ref, bg_ref, bv_ref, o_ref, ot_ref, h_s):
        @pl.when(pl.program_id(0) == 0)
        def _():
            pltpu.sync_copy(h_ref, h_s)

        x2 = _up_pair(h_s, ugw_ref, uvw_ref)
        ug = _conv(x2[:, :128], wg_ref, bg_ref, ALL, FFN_CONV)
        uv = _conv(x2[:, 128:], wv_ref, bv_ref, ALL, FFN_CONV)
        act = jax.nn.gelu(ug) * uv
        o_ref[...] = act.astype(BF16)
        ot_ref[...] = act.T.astype(BF16)

    return pl.pallas_call(
        body, name=name, grid=(NB_FF,), in_specs=[ANY, ug_, uv_, wg, wv, bg, bv],
        out_specs=[pl.BlockSpec((S, 128), lambda j: (0, j)), pl.BlockSpec((128, S), lambda j: (j, 0))],
        out_shape=[_sds((S, DFF), BF16), _sds((DFF, S), BF16)],
        scratch_shapes=[pltpu.VMEM((S, D), BF16)],
        compiler_params=_params(("arbitrary",)))(hb, w_up, w_up, cw, cw, cb, cb)


def _ffn_gate_bwd(dzb, w_down, hb, w_up, cw, cb, *, name):
    S = hb.shape[0]
    ug_, uv_, wg, wv, bg, bv = _ffn_specs(S)

    def body(dz_ref, h_ref, wd_ref, ugw_ref, uvw_ref, wg_ref, wv_ref, bg_ref, bv_ref,
             dxg_ref, dxv_ref, dwg_ref, dwv_ref, dbg_ref, dbv_ref, dz_s, h_s):
        @pl.when(pl.program_id(0) == 0)
        def _():
            pltpu.sync_copy(dz_ref, dz_s)
            pltpu.sync_copy(h_ref, h_s)

        x2 = _up_pair(h_s, ugw_ref, uvw_ref)
        xg, xv = x2[:, :128], x2[:, 128:]
        ug = _conv(xg, wg_ref, bg_ref, ALL, FFN_CONV)
        uv = _conv(xv, wv_ref, bv_ref, ALL, FFN_CONV)
        dug, duv = _gate_bwd(ug, uv, _dg(dz_s[...], wd_ref[...], NT))
        dxg_ref[...] = _conv_bwd(dug, xg, wg_ref, dwg_ref, dbg_ref, ALL, FFN_CONV).astype(BF16)
        dxv_ref[...] = _conv_bwd(duv, xv, wv_ref, dwv_ref, dbv_ref, ALL, FFN_CONV).astype(BF16)

    col = pl.BlockSpec((S, 128), lambda j: (0, j))
    half = _sds((S, DFF), BF16)
    dxg, dxv, dwg, dwv, dbg, dbv = pl.pallas_call(
        body, name=name, grid=(NB_FF,),
        in_specs=[ANY, ANY, pl.BlockSpec((128, D), lambda j: (j, 0)), ug_, uv_, wg, wv, bg, bv],
        out_specs=[col, col, wg, wg, bg, bg],
        out_shape=[half, half, _sds((FFN_CONV, DFF), F32), _sds((FFN_CONV, DFF), F32), _sds((1, DFF), F32), _sds((1, DFF), F32)],
        scratch_shapes=[pltpu.VMEM((S, D), BF16), pltpu.VMEM((S, D), BF16)],
        compiler_params=_params(("arbitrary",)))(dzb, hb, w_down, w_up, w_up, cw, cw, cb, cb)
    return dxg, dxv, jnp.concatenate([dwg, dwv], axis=1), jnp.concatenate([dbg, dbv], axis=1)


def _log_sigmoid(x):
    return jnp.minimum(x, 0.0) - jnp.log1p(jnp.exp(-jnp.abs(x)))


@jax.custom_vjp
def _clamp_div(num, den, floor, shift):
    return num / jnp.maximum(jnp.abs(den), floor)


def _clamp_div_fwd(num, den, floor, shift):
    out = num / jnp.maximum(jnp.abs(den), floor)
    return out, (den, floor, out)


def _clamp_div_bwd(res, g):
    den, floor, out = res
    active = jnp.abs(den) < floor
    dinv = jnp.maximum(jnp.abs(den), floor)
    go = jnp.sum(g * out, axis=-1, keepdims=True)
    ddiv = -go / dinv
    return (g / dinv, jnp.where(active, 0.0, ddiv * jnp.sign(den)), jnp.where(active, ddiv, 0.0),
            jnp.sum(jnp.where(active, go, 0.0), axis=-2, keepdims=True))


_clamp_div.defvjp(_clamp_div_fwd, _clamp_div_bwd)


def _ml_heads(q, k, v, o_pre, gates, gbias, C, n, ng, m, shift):
    H, L, _ = q.shape
    lane1 = lax.broadcasted_iota(jnp.int32, (1, 128), 1)
    gz = gates + jnp.where(lane1 < ML_H, lax.stop_gradient(gbias), gbias)
    gz = jnp.broadcast_to(gz[None], (H, L, 128))
    hid = lax.broadcasted_iota(jnp.int32, (H, L, 128), 0)
    lane = lax.broadcasted_iota(jnp.int32, (H, L, 128), 2)
    ig = jnp.sum(jnp.where(lane == hid, gz, 0.0), axis=2, keepdims=True)
    lf = _log_sigmoid(jnp.sum(jnp.where(lane == ML_H + hid, gz, 0.0), axis=2, keepdims=True))
    r = lax.broadcasted_iota(jnp.int32, (H, L, L), 1)
    c = lax.broadcasted_iota(jnp.int32, (H, L, L), 2)
    eye, tril = r == c, r >= c

    def to_row(col):
        return jnp.sum(jnp.where(eye, col, 0.0), axis=1, keepdims=True)

    b_col = jnp.sum(jnp.where(tril, to_row(lf), 0.0), axis=2, keepdims=True)
    Dm = jnp.where(tril, b_col - to_row(b_col) + to_row(ig), -jnp.inf)
    inter = b_col + m
    m_t = lax.stop_gradient(jnp.maximum(inter, jnp.max(Dm, axis=2, keepdims=True)))
    w_inter = jnp.exp(inter - m_t)
    ks = k * (ML_DH ** -0.5)
    s = dot_nt(q, ks) * jnp.exp(Dm - m_t)
    num = w_inter * dot_nn(q, C) + dot_nn(s, v)
    den = w_inter * jnp.sum(q * n, axis=2, keepdims=True) + jnp.sum(s, axis=2, keepdims=True)
    h = _clamp_div(num, den, jnp.exp(-m_t), shift)
    g = jnp.sum(lf, axis=1, keepdims=True)
    a = g - b_col + ig
    m_new = lax.stop_gradient(jnp.maximum(g + m, jnp.max(a, axis=1, keepdims=True)))
    decay = jnp.exp(g + m - m_new)
    wk = jnp.exp(a - m_new)
    C_new = decay * C + dot_tn(ks * wk, v)
    n_new = decay * n + jnp.sum(wk * ks, axis=1, keepdims=True)
    mu = jnp.mean(h, axis=2, keepdims=True)
    hc = h - mu
    var = jnp.mean(hc * hc, axis=2, keepdims=True)
    out = jax.nn.sigmoid(o_pre) * (hc * lax.rsqrt(var + EPS) * ng)
    return out, C_new, n_new, m_new


def _hs(h, off=0):
    return slice(off + h * ML_DH, off + (h + 1) * ML_DH)


def _heads(ref, off=0):
    return jnp.stack([ref[:, _hs(h, off)] for h in range(ML_H)])


def _mlstm_fwd(qk, proj, gbias, ng, *, name):
    S = qk.shape[0]
    nc = S // ML_L

    def body(q_ref, k_ref, v_ref, o_ref, g_ref, gb_ref, ng_ref, h_ref, cs_ref, ns_ref, ms_ref, c_s, n_s, m_s):
        @pl.when(pl.program_id(0) == 0)
        def _():
            c_s[...] = jnp.zeros_like(c_s)
            n_s[...] = jnp.zeros_like(n_s)
            m_s[...] = jnp.zeros_like(m_s)

        C, n = c_s[...], n_s[...]
        cs_ref[0] = C
        ns_ref[0] = n
        ms_ref[0] = m_s[...]
        out, C2, n2, m2 = _ml_heads(_heads(q_ref), _heads(k_ref), _heads(v_ref), _heads(o_ref), g_ref[...], gb_ref[...], C, n,
                                    _heads(ng_ref), m_s[:, :, 0:1], jnp.zeros((ML_H, 1, 1), F32))
        for h in range(ML_H):
            h_ref[:, _hs(h)] = out[h].astype(BF16)
        c_s[...] = C2
        n_s[...] = n2
        m_s[...] = jnp.broadcast_to(m2, (ML_H, 1, 128))

    def w(j):
        return pl.BlockSpec((ML_L, ML_W), lambda c, j=j: (c, j))

    return pl.pallas_call(
        body, name=name, grid=(nc,),
        in_specs=[w(0), w(1), w(2), w(3), pl.BlockSpec((ML_L, 128), lambda c: (c, 22)),
                  pl.BlockSpec((1, 128), lambda c: (0, 0)), pl.BlockSpec((1, ML_W), lambda c: (0, 0))],
        out_specs=[w(0), pl.BlockSpec((1, ML_H, ML_DH, ML_DH), lambda c: (c, 0, 0, 0)),
                   pl.BlockSpec((1, ML_H, 1, 128), lambda c: (c, 0, 0, 0)), pl.BlockSpec((1, ML_H, 1, 128), lambda c: (c, 0, 0, 0))],
        out_shape=[_sds((S, ML_W), BF16), _sds((nc, ML_H, ML_DH, ML_DH), F32), _sds((nc, ML_H, 1, 128), F32),
                   _sds((nc, ML_H, 1, 128), F32)],
        scratch_shapes=[pltpu.VMEM((ML_H, ML_DH, ML_DH), F32), pltpu.VMEM((ML_H, 1, 128), F32), pltpu.VMEM((ML_H, 1, 128), F32)],
        compiler_params=_params(("arbitrary",)))(qk, qk, proj, proj, proj, gbias, ng)


def _mlstm_bwd(dh, qk, proj, gbias, ng, cs, ns, ms, *, name):
    S = qk.shape[0]
    nc = S // ML_L

    def body(dh_ref, q_ref, k_ref, v_ref, o_ref, g_ref, gb_ref, ng_ref, cs_ref, ns_ref, ms_ref,
             dqk_ref, dml_ref, dgb_ref, dng_ref, dc_s, dn_s):
        @pl.when(pl.program_id(0) == 0)
        def _():
            dc_s[...] = jnp.zeros_like(dc_s)
            dn_s[...] = jnp.zeros_like(dn_s)
            dgb_ref[...] = jnp.zeros_like(dgb_ref)
            dng_ref[...] = jnp.zeros_like(dng_ref)

        m = ms_ref[0][:, :, 0:1]

        def f(q, k, v, o_pre, gates, gb, C, n, ng_, shift):
            return _ml_heads(q, k, v, o_pre, gates, gb, C, n, ng_, m, shift)[:3]

        _, vjp = jax.vjp(f, _heads(q_ref), _heads(k_ref), _heads(v_ref), _heads(o_ref), g_ref[...], gb_ref[...],
                         cs_ref[0], ns_ref[0], _heads(ng_ref), jnp.zeros((ML_H, 1, 1), F32))
        dq, dk, dv, do, dgates, dgb, dC, dn, dng, dshift = vjp((_heads(dh_ref), dc_s[...], dn_s[...]))
        lane1 = lax.broadcasted_iota(jnp.int32, (1, 128), 1)
        for h in range(ML_H):
            dqk_ref[:, _hs(h)] = dq[h]
            dqk_ref[:, _hs(h, ML_W)] = dk[h]
            dml_ref[:, _hs(h)] = dv[h].astype(BF16)
            dml_ref[:, _hs(h, ML_W)] = do[h].astype(BF16)
            dng_ref[:, _hs(h)] += dng[h]
            dgb = jnp.where(lane1 == h, dshift[h], dgb)
        dc_s[...] = dC
        dn_s[...] = dn
        dml_ref[:, 2 * ML_W:] = dgates.astype(BF16)
        dgb_ref[...] += dgb

    def w(j):
        return pl.BlockSpec((ML_L, ML_W), lambda c, j=j: (nc - 1 - c, j))

    vec = pl.BlockSpec((1, 128), lambda c: (0, 0))
    vecw = pl.BlockSpec((1, ML_W), lambda c: (0, 0))
    gsp = pl.BlockSpec((ML_L, 128), lambda c: (nc - 1 - c, 22))
    st = pl.BlockSpec((1, ML_H, 1, 128), lambda c: (nc - 1 - c, 0, 0, 0))
    return pl.pallas_call(
        body, name=name, grid=(nc,),
        in_specs=[w(0), w(0), w(1), w(2), w(3), gsp, vec, vecw,
                  pl.BlockSpec((1, ML_H, ML_DH, ML_DH), lambda c: (nc - 1 - c, 0, 0, 0)), st, st],
        out_specs=[pl.BlockSpec((ML_L, 2 * ML_W), lambda c: (nc - 1 - c, 0)), pl.BlockSpec((ML_L, ML_GW), lambda c: (nc - 1 - c, 0)),
                   vec, vecw],
        out_shape=[_sds((S, 2 * ML_W), F32), _sds((S, ML_GW), BF16), _sds((1, 128), F32), _sds((1, ML_W), F32)],
        scratch_shapes=[pltpu.VMEM((ML_H, ML_DH, ML_DH), F32), pltpu.VMEM((ML_H, 1, 128), F32)],
        compiler_params=_params(("arbitrary",)))(dh, qk, qk, proj, proj, proj, gbias, ng, cs, ns, ms)


def _t5_buckets():
    r = np.arange(BLK)[:, None]
    c = np.arange(2 * BLK)[None, :]
    n = np.maximum(BLK + r - c, 0)
    max_exact = REL_B // 2
    nf = np.maximum(n, 1).astype(np.float32)
    large = max_exact + (np.log(nf / np.float32(max_exact)) / np.float32(math.log(REL_MAXD / max_exact))
                         * np.float32(REL_B - max_exact)).astype(np.int32)
    large = np.minimum(large, REL_B - 1)
    return np.where(n < max_exact, n, large).astype(np.int32)


def _bias_table(rel_bias, bucket, *, name):
    def body(rb_ref, bk_ref, o_ref):
        bk = bk_ref[...]
        for h in range(SW_H):
            acc = jnp.zeros((BLK, 2 * BLK), F32)
            for b in range(REL_B):
                acc = jnp.where(bk == b, rb_ref[b, h], acc)
            o_ref[h] = acc

    return pl.pallas_call(
        body, name=name, in_specs=[pl.BlockSpec(memory_space=pltpu.SMEM), pl.BlockSpec(memory_space=pltpu.VMEM)],
        out_specs=pl.BlockSpec(memory_space=pltpu.VMEM), out_shape=_sds((SW_H, BLK, 2 * BLK), F32),
        compiler_params=_params())(rel_bias, bucket)


def _bias_table_bwd(dbias_list, bucket, *, name):
    nl = len(dbias_list)

    def body(*refs):
        d_refs, bk_ref, o_ref = refs[:nl], refs[nl], refs[nl + 1]
        bk = bk_ref[...]
        rows = lax.broadcasted_iota(jnp.int32, (REL_B, 128), 0)
        lanes = lax.broadcasted_iota(jnp.int32, (REL_B, 128), 1)
        acc = jnp.zeros((REL_B, 128), F32)
        for h in range(SW_H):
            d = d_refs[0][h]
            for d_ref in d_refs[1:]:
                d = d + d_ref[h]
            for b in range(REL_B):
                t = jnp.sum(jnp.sum(jnp.where(bk == b, d, 0.0), axis=0, keepdims=True), axis=1, keepdims=True)
                acc = jnp.where((rows == b) & (lanes == h), t, acc)
        o_ref[...] = acc

    vm = pl.BlockSpec(memory_space=pltpu.VMEM)
    return pl.pallas_call(
        body, name=name, in_specs=[vm] * (nl + 1), out_specs=vm, out_shape=_sds((REL_B, 128), F32),
        compiler_params=_params())(*dbias_list, bucket)


def _swa_heads(q, kp, kc, vp, vc, bp, bc, sinks, has_prev):
    def rep(t):
        return jnp.concatenate([t[g:g + 1] for g in range(SW_H // SW_G) for _ in range(SW_G)], axis=0)

    r = lax.broadcasted_iota(jnp.int32, (SW_H, BLK, BLK), 1)
    c = lax.broadcasted_iota(jnp.int32, (SW_H, BLK, BLK), 2)
    hid = lax.broadcasted_iota(jnp.int32, (SW_H, 1, 128), 0)
    lane = lax.broadcasted_iota(jnp.int32, (SW_H, 1, 128), 2)
    sink = jnp.sum(jnp.where(lane == hid, jnp.broadcast_to(sinks[None], (SW_H, 1, 128)), 0.0), axis=2, keepdims=True)
    lp = jnp.where((c > r) & has_prev, dot_nt(q, rep(kp)) * (SW_DH ** -0.5) + bp, -jnp.inf)
    lc = jnp.where(c <= r, dot_nt(q, rep(kc)) * (SW_DH ** -0.5) + bc, -jnp.inf)
    mx = jnp.maximum(jnp.maximum(jnp.max(lp, axis=2, keepdims=True), jnp.max(lc, axis=2, keepdims=True)), sink)
    mx = lax.stop_gradient(mx)
    pp, pc = jnp.exp(lp - mx), jnp.exp(lc - mx)
    den = jnp.sum(pp, axis=2, keepdims=True) + jnp.sum(pc, axis=2, keepdims=True) + jnp.exp(sink - mx)
    return dot_nn(pp / den, rep(vp)) + dot_nn(pc / den, rep(vc))


def _qs(h, off=0):
    return slice(off + h * SW_DH, off + (h + 1) * SW_DH)


def _split(ref, n):
    return jnp.stack([ref[:, _qs(h)] for h in range(n)])


def _swa_fwd(proj, bias, sinks, *, name):
    S = proj.shape[0]
    nb = S // BLK
    nkv = SW_H // SW_G

    def body(q_ref, kp_ref, kc_ref, vp_ref, vc_ref, b_ref, s_ref, o_ref):
        out = _swa_heads(_split(q_ref, SW_H), _split(kp_ref, nkv), _split(kc_ref, nkv), _split(vp_ref, nkv), _split(vc_ref, nkv),
                         b_ref[:, :, :BLK], b_ref[:, :, BLK:], s_ref[...], pl.program_id(0) > 0)
        for h in range(SW_H):
            o_ref[:, _qs(h)] = out[h].astype(BF16)

    def cur(j):
        return pl.BlockSpec((BLK, 128), lambda n, j=j: (n, j))

    def prev(j):
        return pl.BlockSpec((BLK, 128), lambda n, j=j: (jnp.maximum(n - 1, 0), j))

    return pl.pallas_call(
        body, name=name, grid=(nb,),
        in_specs=[pl.BlockSpec((BLK, SW_W), lambda n: (n, 4)), prev(20), cur(20), prev(21), cur(21),
                  pl.BlockSpec((SW_H, BLK, 2 * BLK), lambda n: (0, 0, 0)), pl.BlockSpec((1, 128), lambda n: (0, 0))],
        out_specs=pl.BlockSpec((BLK, SW_W), lambda n: (n, 0)), out_shape=_sds((S, SW_W), BF16),
        compiler_params=_params(("parallel",)))(proj, proj, proj, proj, proj, bias, sinks)


def _swa_bwd(dh, proj, bias, sinks, *, name):
    S = proj.shape[0]
    nb = S // BLK

    nkv = SW_H // SW_G

    def body(dh_ref, q_ref, kp_ref, kc_ref, vp_ref, vc_ref, b_ref, s_ref, dsw_ref, db_ref, ds_ref, ck_s, cv_s):
        i = pl.program_id(0)

        @pl.when(i == 0)
        def _():
            ck_s[...] = jnp.zeros_like(ck_s)
            cv_s[...] = jnp.zeros_like(cv_s)
            db_ref[...] = jnp.zeros_like(db_ref)
            ds_ref[...] = jnp.zeros_like(ds_ref)

        f = functools.partial(_swa_heads, has_prev=i < nb - 1)
        _, vjp = jax.vjp(f, _split(q_ref, SW_H), _split(kp_ref, nkv), _split(kc_ref, nkv), _split(vp_ref, nkv),
                         _split(vc_ref, nkv), b_ref[:, :, :BLK], b_ref[:, :, BLK:], s_ref[...])
        dq, dkp, dkc, dvp, dvc, dbp, dbc, ds = vjp(_split(dh_ref, SW_H))
        for h in range(SW_H):
            dsw_ref[:, _qs(h)] = dq[h].astype(BF16)
        for g in range(nkv):
            dsw_ref[:, _qs(g, SW_W)] = (dkc[g] + ck_s[:, _qs(g)]).astype(BF16)
            dsw_ref[:, _qs(g, SW_W + SW_KVW)] = (dvc[g] + cv_s[:, _qs(g)]).astype(BF16)
            ck_s[:, _qs(g)] = dkp[g]
            cv_s[:, _qs(g)] = dvp[g]
        db_ref[:, :, :BLK] += dbp
        db_ref[:, :, BLK:] += dbc
        ds_ref[...] += ds

    def cur(j):
        return pl.BlockSpec((BLK, 128), lambda i, j=j: (nb - 1 - i, j))

    def prev(j):
        return pl.BlockSpec((BLK, 128), lambda i, j=j: (jnp.maximum(nb - 2 - i, 0), j))

    bsp = pl.BlockSpec((SW_H, BLK, 2 * BLK), lambda i: (0, 0, 0))
    vec = pl.BlockSpec((1, 128), lambda i: (0, 0))
    return pl.pallas_call(
        body, name=name, grid=(nb,),
        in_specs=[pl.BlockSpec((BLK, SW_W), lambda i: (nb - 1 - i, 1)), pl.BlockSpec((BLK, SW_W), lambda i: (nb - 1 - i, 4)),
                  prev(20), cur(20), prev(21), cur(21), bsp, vec],
        out_specs=[pl.BlockSpec((BLK, SW_GW), lambda i: (nb - 1 - i, 0)), bsp, vec],
        out_shape=[_sds((S, SW_GW), BF16), _sds((SW_H, BLK, 2 * BLK), F32), _sds((1, 128), F32)],
        scratch_shapes=[pltpu.VMEM((BLK, 128), F32)] * 2,
        compiler_params=_params(("arbitrary",)))(dh, proj, proj, proj, proj, proj, bias, sinks)


XA_TM = 512


def _xa_head(qh, kh, vh):
    logits = dot_nt(qh, kh) * (XA_DH ** -0.5)
    mx = lax.stop_gradient(jnp.max(logits, axis=1, keepdims=True))
    e = jnp.exp(logits - mx)
    return dot_nn(e / jnp.sum(e, axis=1, keepdims=True), vh)


def _xs(h, off=0):
    return slice(off + h * XA_DH, off + (h + 1) * XA_DH)


def _xattn_fwd(hb, wq, kv, *, name):
    S = hb.shape[0]
    M = kv.shape[0]

    def body(h_ref, wq_ref, kv_ref, o_ref):
        q = _dg(h_ref[...], wq_ref[...], NN)
        for h in range(XA_H):
            o_ref[:, _xs(h)] = _xa_head(q[:, _xs(h)], kv_ref[:, _xs(h)], kv_ref[:, _xs(h, D)]).astype(BF16)

    tm = min(XA_TM, S)
    row = pl.BlockSpec((tm, D), lambda i: (i, 0))
    return pl.pallas_call(
        body, name=name, grid=(S // tm,),
        in_specs=[row, pl.BlockSpec((D, D), lambda i: (0, 0)), pl.BlockSpec((M, 2 * D), lambda i: (0, 0))], out_specs=row,
        out_shape=_sds((S, D), BF16), compiler_params=_params(("parallel",)))(hb, wq, kv)


def _xattn_bwd(dzb, wo, hb, wq, kv, *, name):
    S = hb.shape[0]
    M = kv.shape[0]

    def body(dz_ref, wo_ref, h_ref, wq_ref, kv_ref, dq_ref, dkv_ref):
        @pl.when(pl.program_id(0) == 0)
        def _():
            dkv_ref[...] = jnp.zeros_like(dkv_ref)

        q = _dg(h_ref[...], wq_ref[...], NN)
        do = _dg(dz_ref[...], wo_ref[...], NT)
        for h in range(XA_H):
            _, vjp = jax.vjp(_xa_head, q[:, _xs(h)], kv_ref[:, _xs(h)], kv_ref[:, _xs(h, D)])
            dq, dk, dv = vjp(do[:, _xs(h)])
            dq_ref[:, _xs(h)] = dq.astype(BF16)
            dkv_ref[:, _xs(h)] += dk
            dkv_ref[:, _xs(h, D)] += dv

    tm = min(XA_TM, S)
    row = pl.BlockSpec((tm, D), lambda i: (i, 0))
    full = pl.BlockSpec((M, 2 * D), lambda i: (0, 0))
    sq = pl.BlockSpec((D, D), lambda i: (0, 0))
    return pl.pallas_call(
        body, name=name, grid=(S // tm,), in_specs=[row, sq, row, sq, full],
        out_specs=[row, full], out_shape=[_sds((S, D), BF16), _sds((M, 2 * D), F32)],
        compiler_params=_params(("arbitrary",)))(dzb, wo, hb, wq, kv)


ANY = pl.BlockSpec(memory_space=pl.ANY)


def _place():
    x, y, c = lax.axis_index("x"), lax.axis_index("y"), lax.axis_index("c")
    chips = [(1 - x, y), (x, 1 - y), (1 - x, 1 - y)]
    return x, y, c, chips


def _gather(arrs, *, name):
    n = len(arrs)

    def body(*refs):
        srcs, outs = refs[:n], refs[n:2 * n]
        send_sems, recv_sems, local_sems = refs[2 * n:]
        x, y, c, chips = _place()
        me, sib = (x, y, c), (x, y, 1 - c)

        def idx(p):
            return 4 * p[0] + 2 * p[1] + p[2]

        def copy(i, k, block, to, from_src=False):
            return pltpu.make_async_remote_copy(
                src_ref=srcs[i] if from_src else outs[i].at[idx(block)], dst_ref=outs[i].at[idx(block)],
                send_sem=send_sems.at[7 * i + k], recv_sem=recv_sems.at[7 * i + k], device_id=to, device_id_type=MESH)

        local = [pltpu.make_async_copy(srcs[i], outs[i].at[idx(me)], local_sems.at[i]) for i in range(n)]
        for cp in local:
            cp.start()
        first = []
        for i in range(n):
            first.append(copy(i, 0, me, sib, True))
            first += [copy(i, 1 + j, me, (*chip, c), True) for j, chip in enumerate(chips)]
        for cp in first:
            cp.start()
        passed = []
        for j, chip in enumerate(chips):
            for i in range(n):
                copy(i, 1 + j, (*chip, c), me).wait_recv()
                cp = copy(i, 4 + j, (*chip, c), sib)
                cp.start()
                passed.append(cp)
        for i in range(n):
            copy(i, 0, sib, me).wait_recv()
        for j, chip in enumerate(chips):
            for i in range(n):
                copy(i, 4 + j, (*chip, 1 - c), me).wait_recv()
        for cp in first + passed:
            cp.wait_send()
        for cp in local:
            cp.wait()

    return pl.pallas_call(
        body, name=name, in_specs=[ANY] * n, out_specs=[ANY] * n,
        out_shape=[_sds((N_DEV,) + a.shape, a.dtype) for a in arrs],
        scratch_shapes=[pltpu.SemaphoreType.DMA((7 * n,)), pltpu.SemaphoreType.DMA((7 * n,)), pltpu.SemaphoreType.DMA((n,))],
        compiler_params=pltpu.CompilerParams(has_side_effects=True))(*arrs)


HBM = pl.BlockSpec(memory_space=pltpu.HBM)
SEM = pl.BlockSpec(memory_space=pltpu.SEMAPHORE)
EFFECT = pltpu.SideEffectType.DATAFLOW_SIDE_EFFECTING


def _near_copies(srcs, lands, send_sems, recv_sems):
    x, y, c, chips = _place()
    me = 4 * x + 2 * y + c
    out = []
    for i in range(len(srcs)):
        for k, (px, py, pc) in enumerate([(x, y, 1 - c)] + [(*chip, c) for chip in chips]):
            out.append(tuple(pltpu.make_async_remote_copy(
                src_ref=srcs[i], dst_ref=lands[i].at[slot], send_sem=send_sems.at[4 * i + k], recv_sem=recv_sems.at[4 * i + k],
                device_id=(px, py, pc), device_id_type=MESH) for slot in (me, 4 * px + 2 * py + pc)))
    return out


def _forward_sibling(lands, *, name):
    n = len(lands)

    def body(*refs):
        bufs = refs[n:2 * n]
        send_sems, recv_sems = refs[2 * n:]
        x, y, c, chips = _place()
        copies = [tuple(pltpu.make_async_remote_copy(
            src_ref=bufs[i].at[4 * chip[0] + 2 * chip[1] + c], dst_ref=bufs[i].at[4 * chip[0] + 2 * chip[1] + cc],
            send_sem=send_sems.at[3 * i + j], recv_sem=recv_sems.at[3 * i + j], device_id=(x, y, 1 - c), device_id_type=MESH)
            for cc in (c, 1 - c)) for i in range(n) for j, chip in enumerate(chips)]
        for send, _ in copies:
            send.start()
        for send, recv in copies:
            send.wait_send()
            recv.wait_recv()

    return pl.pallas_call(
        body, name=name, in_specs=[ANY] * n, out_specs=[ANY] * n, out_shape=[_sds(a.shape, a.dtype) for a in lands],
        input_output_aliases={i: i for i in range(n)},
        scratch_shapes=[pltpu.SemaphoreType.DMA((3 * n,)), pltpu.SemaphoreType.DMA((3 * n,))],
        compiler_params=pltpu.CompilerParams(has_side_effects=True))(*lands)


def _scatter_copies(srcs, lands, send_sems, recv_sems):
    x, y, c, _ = _place()
    me = 4 * x + 2 * y + c
    out = []
    for i in range(len(srcs)):
        for j in range(1, N_DEV):
            px, py, pc = x ^ ((j >> 2) & 1), y ^ ((j >> 1) & 1), c ^ (j & 1)
            p = 4 * px + 2 * py + pc
            k = (N_DEV - 1) * i + j - 1
            out.append(tuple(pltpu.make_async_remote_copy(
                src_ref=srcs[i].at[s], dst_ref=lands[i].at[d], send_sem=send_sems.at[k], recv_sem=recv_sems.at[k],
                device_id=(px, py, pc), device_id_type=MESH) for s, d in ((p, me), (me, p))))
    return out


def _split_start(srcs, lands, pattern, ncopy, *, after=(), name):
    n = len(srcs)
    na = len(after)

    def body(*refs):
        sems = refs[2 * n + na:]
        for send, _ in pattern(refs[:n], refs[n:2 * n], sems[0], sems[1]):
            send.start()
        refs[-1][...] = jnp.zeros_like(refs[-1])

    arrs = list(srcs) + list(lands)
    return pl.pallas_call(
        body, name=name,
        out_shape=(pltpu.SemaphoreType.DMA((ncopy,)), pltpu.SemaphoreType.DMA((ncopy,)),
                   *[pltpu.HBM(a.shape, a.dtype) for a in arrs], _sds((8, 128), F32)),
        in_specs=[HBM] * (2 * n) + [ANY] * na, out_specs=(SEM, SEM, *[HBM] * (2 * n), pl.BlockSpec(memory_space=pltpu.VMEM)),
        input_output_aliases={i: 2 + i for i in range(2 * n)},
        compiler_params=pltpu.CompilerParams(has_side_effects=EFFECT))(
            *[pltpu.with_memory_space_constraint(a, pltpu.HBM) for a in arrs], *after)


def _split_wait(started, after, pattern, *, name):
    send_sems, recv_sems, *arrs = started[:-1]
    n = len(arrs) // 2

    def body(*refs):
        for send, recv in pattern(refs[:n], refs[n:2 * n], refs[2 * n], refs[2 * n + 1]):
            send.wait_send()
            recv.wait_recv()

    outs = pl.pallas_call(
        body, name=name, out_shape=tuple(pltpu.HBM(a.shape, a.dtype) for a in arrs),
        in_specs=[HBM] * (2 * n) + [SEM, SEM, ANY], out_specs=tuple([HBM] * (2 * n)),
        input_output_aliases={i: i for i in range(2 * n)},
        compiler_params=pltpu.CompilerParams(has_side_effects=EFFECT))(*arrs, send_sems, recv_sems, after)
    return list(outs[n:])


def _adamw(parts, w, m, v, *, layer=None, prev=None, name):
    P, R, C = parts.shape
    tr = next((t for t in (512, 256, 176, 128, 64, 32, 16, 8) if R % t == 0 and t * C <= 256 * 1024), R)
    c1 = 1.0 / (1.0 - ADAM_B1 ** ADAM_STEP)
    c2 = 1.0 / (1.0 - ADAM_B2 ** ADAM_STEP)
    nprev = 0 if prev is None else 4

    def body(p_ref, w_ref, m_ref, v_ref, *rest):
        g_ref, d_ref, nm_ref, nv_ref = rest[nprev:]
        g = p_ref[0].astype(F32)
        for j in range(1, P):
            g = g + p_ref[j].astype(F32)
        g = g.reshape(w_ref.shape)
        nm = ADAM_B1 * m_ref[...] + (1.0 - ADAM_B1) * g
        nv = ADAM_B2 * v_ref[...] + (1.0 - ADAM_B2) * (g * g)
        g_ref[...] = g
        nm_ref[...] = nm
        nv_ref[...] = nv
        d_ref[...] = -ADAM_LR * ((nm * c1) / (jnp.sqrt(nv * c2) + ADAM_EPS) + ADAM_WD * w_ref[...])

    if layer is None:
        row = pl.BlockSpec((tr, C), lambda i: (i, 0))
    else:
        row = pl.BlockSpec((1, tr, C), lambda i: (layer, i, 0))
    out = _sds(w.shape, F32)
    return pl.pallas_call(
        body, name=name, grid=(R // tr,),
        in_specs=[pl.BlockSpec((P, tr, C), lambda i: (0, i, 0)), row, row, row] + [ANY] * nprev,
        out_specs=[row, row, row, row], out_shape=[out, out, out, out],
        input_output_aliases={4 + k: k for k in range(nprev)},
        compiler_params=_params(("parallel",)))(parts, w, m, v, *(prev or ()))


BIG = ("w_in", "w_out", "xa_wq", "xa_wkv", "xa_wo", "ffn_w_up", "ffn_w_down")
COL_SHARDED = ("w_in", "xa_wkv", "ffn_w_up", "ml_conv_w", "ffn_conv_w")
SHARDED_SMALL = ("ml_conv_w", "ffn_conv_w")
REPLICATED = ("rel_bias", "ml_conv_b", "ml_i_bias", "ml_f_bias", "ml_norm_g", "swa_sinks", "ln1_g", "ln1_b", "ln2_g", "ln2_b",
              "ffn_conv_b", "ln3_g", "ln3_b")
NAMES = ("rel_bias", "w_in", "ml_conv_w", "ml_conv_b", "ml_i_bias", "ml_f_bias", "ml_norm_g", "swa_sinks", "w_out", "ln1_g", "ln1_b",
         "xa_wq", "xa_wkv", "xa_wo", "ln2_g", "ln2_b", "ffn_w_up", "ffn_conv_w", "ffn_conv_b", "ffn_w_down", "ln3_g", "ln3_b")


def _flat_rows(a, mult):
    f = a.reshape(-1)
    n = -(-f.shape[0] // (128 * mult)) * (128 * mult)
    if n != f.shape[0]:
        f = jnp.pad(f, (0, n - f.shape[0]))
    return f.reshape(-1, 128)


def _pack(arrs, mult):
    parts = [_flat_rows(a, mult) for a in arrs]
    return jnp.concatenate(parts, axis=0), [p.shape[0] for p in parts]


def _unpack(flat, rows, shapes):
    out, off = [], 0
    lead = flat.shape[:-2]
    for r, shp in zip(rows, shapes):
        n = int(np.prod(shp))
        piece = flat[..., off:off + r, :].reshape(lead + (r * 128,))[..., :n]
        out.append(piece.reshape(lead + tuple(shp)))
        off += r
    return out


def _full_from_shards(stacked, name):
    if name in COL_SHARDED:
        return jnp.moveaxis(stacked, 0, 2).reshape(stacked.shape[1], stacked.shape[2], N_DEV * stacked.shape[3])
    return jnp.moveaxis(stacked, 0, 1).reshape(stacked.shape[1], N_DEV * stacked.shape[2], stacked.shape[3])


def _shards_from_full(full, name):
    L, A, B = full.shape
    if name in COL_SHARDED:
        return jnp.moveaxis(full.reshape(L, A, N_DEV, B // N_DEV), 2, 0)
    return jnp.moveaxis(full.reshape(L, N_DEV, A // N_DEV, B), 1, 0)


def _pad_win(w):
    z = jnp.zeros(w.shape[:-1] + (NP_IN - N_IN,), w.dtype)
    return jnp.concatenate([w[..., :2048], w[..., 2056:], w[..., 2048:2056], z], axis=-1)


def _row128(v):
    return jnp.pad(v, (0, 128 - v.shape[0])).reshape(1, 128)


REST = BIG[1:]


def _layer_full(stacked, n):
    if n in COL_SHARDED:
        full = jnp.moveaxis(stacked, 0, 1).reshape(stacked.shape[1], N_DEV * stacked.shape[2])
    else:
        full = stacked.reshape(N_DEV * stacked.shape[1], stacked.shape[2])
    return _pad_win(full) if n == "w_in" else full


def _layer_shards(g, n):
    A, B = g.shape
    if n in COL_SHARDED:
        return jnp.moveaxis(g.reshape(A, N_DEV, B // N_DEV), 1, 0).astype(BF16)
    return g.reshape(N_DEV, A // N_DEV, B).astype(BF16)


def _with_own_block(a, idx, nblk):
    return lax.dynamic_update_slice(lax.empty((nblk,) + a.shape, a.dtype), a[None], (idx,) + (0,) * a.ndim)


def _gather_start(arrs, me, *, after=(), name):
    return _split_start(arrs, [_with_own_block(a, me, N_DEV) for a in arrs], _near_copies, 4 * len(arrs), after=after, name=name)


def _gather_finish(started, after, *, name):
    return _forward_sibling(_split_wait(started, after, _near_copies, name=name + "_wait"), name=name + "_forward")


def _reduce_start(tag, names, grads, me):
    send = [_layer_shards(g, n) for n, g in zip(names, grads)]
    lands = [_with_own_block(lax.dynamic_index_in_dim(s, me, 0, keepdims=False), me, N_DEV) for s in send]
    return _split_start(send, lands, _scatter_copies, (N_DEV - 1) * len(send), name=f"reduce_{tag}_start")


def _update_small(Gf, loss_part, W, Mo, Vo, me):
    res = {}
    small = REPLICATED + SHARDED_SMALL
    sp_flat, sp_rows = _pack([Gf[n] for n in small] + [loss_part], 8)
    sp_all = _gather([sp_flat], name="gather_small_grads")[0]

    def widen(n, t):
        if n not in SHARDED_SMALL:
            return t[n]
        return lax.dynamic_update_slice(jnp.zeros(Gf[n].shape, F32), t[n], (0, 0, me * t[n].shape[2]))

    zl = jnp.zeros((8, 128), F32)
    wsm, _ = _pack([widen(n, W) for n in small] + [zl], 8)
    msm, _ = _pack([widen(n, Mo) for n in small] + [zl], 8)
    vsm, _ = _pack([widen(n, Vo) for n in small] + [zl], 8)
    outs_small = [_unpack(o_, sp_rows, [Gf[n].shape for n in small] + [(8, 128)])
                  for o_ in _adamw(sp_all, wsm, msm, vsm, name="adamw_small")]
    for kind, os_ in zip(("g", "d", "m", "v"), outs_small):
        for n, a in zip(small, os_[:-1]):
            if n in SHARDED_SMALL:
                a = lax.dynamic_slice(a, (0, 0, me * W[n].shape[2]), W[n].shape)
            res[kind, n] = a
    return res, outs_small[0][-1][0, 0]


def kernel(x, mem, rel_bias, w_in, ml_conv_w, ml_conv_b, ml_i_bias, ml_f_bias, ml_norm_g, swa_sinks, w_out, ln1_g, ln1_b, xa_wq, xa_wkv, xa_wo, ln2_g, ln2_b, ffn_w_up, ffn_conv_w, ffn_conv_b, ffn_w_down, ln3_g, ln3_b, loss_target, m_rel_bias, m_w_in, m_ml_conv_w, m_ml_conv_b, m_ml_i_bias, m_ml_f_bias, m_ml_norm_g, m_swa_sinks, m_w_out, m_ln1_g, m_ln1_b, m_xa_wq, m_xa_wkv, m_xa_wo, m_ln2_g, m_ln2_b, m_ffn_w_up, m_ffn_conv_w, m_ffn_conv_b, m_ffn_w_down, m_ln3_g, m_ln3_b, v_rel_bias, v_w_in, v_ml_conv_w, v_ml_conv_b, v_ml_i_bias, v_ml_f_bias, v_ml_norm_g, v_swa_sinks, v_w_out, v_ln1_g, v_ln1_b, v_xa_wq, v_xa_wkv, v_xa_wo, v_ln2_g, v_ln2_b, v_ffn_w_up, v_ffn_conv_w, v_ffn_conv_b, v_ffn_w_down, v_ln3_g, v_ln3_b):
    W = dict(rel_bias=rel_bias, w_in=w_in, ml_conv_w=ml_conv_w, ml_conv_b=ml_conv_b, ml_i_bias=ml_i_bias, ml_f_bias=ml_f_bias,
             ml_norm_g=ml_norm_g, swa_sinks=swa_sinks, w_out=w_out, ln1_g=ln1_g, ln1_b=ln1_b, xa_wq=xa_wq, xa_wkv=xa_wkv,
             xa_wo=xa_wo, ln2_g=ln2_g, ln2_b=ln2_b, ffn_w_up=ffn_w_up, ffn_conv_w=ffn_conv_w, ffn_conv_b=ffn_conv_b,
             ffn_w_down=ffn_w_down, ln3_g=ln3_g, ln3_b=ln3_b)
    Mo = dict(rel_bias=m_rel_bias, w_in=m_w_in, ml_conv_w=m_ml_conv_w, ml_conv_b=m_ml_conv_b, ml_i_bias=m_ml_i_bias,
              ml_f_bias=m_ml_f_bias, ml_norm_g=m_ml_norm_g, swa_sinks=m_swa_sinks, w_out=m_w_out, ln1_g=m_ln1_g, ln1_b=m_ln1_b,
              xa_wq=m_xa_wq, xa_wkv=m_xa_wkv, xa_wo=m_xa_wo, ln2_g=m_ln2_g, ln2_b=m_ln2_b, ffn_w_up=m_ffn_w_up,
              ffn_conv_w=m_ffn_conv_w, ffn_conv_b=m_ffn_conv_b, ffn_w_down=m_ffn_w_down, ln3_g=m_ln3_g, ln3_b=m_ln3_b)
    Vo = dict(rel_bias=v_rel_bias, w_in=v_w_in, ml_conv_w=v_ml_conv_w, ml_conv_b=v_ml_conv_b, ml_i_bias=v_ml_i_bias,
              ml_f_bias=v_ml_f_bias, ml_norm_g=v_ml_norm_g, swa_sinks=v_swa_sinks, w_out=v_w_out, ln1_g=v_ln1_g, ln1_b=v_ln1_b,
              xa_wq=v_xa_wq, xa_wkv=v_xa_wkv, xa_wo=v_xa_wo, ln2_g=v_ln2_g, ln2_b=v_ln2_b, ffn_w_up=v_ffn_w_up,
              ffn_conv_w=v_ffn_conv_w, ffn_conv_b=v_ffn_conv_b, ffn_w_down=v_ffn_w_down, ln3_g=v_ln3_g, ln3_b=v_ln3_b)
    S = x.shape[1]
    c_me = lax.axis_index("c")
    me = 4 * lax.axis_index("x") + 2 * lax.axis_index("y") + c_me
    xs = x.reshape(S, D)
    mems = mem.reshape(mem.shape[1], D)
    tgt = loss_target.reshape(S, D)

    jme = 2 * lax.axis_index("x") + lax.axis_index("y")

    sm_flat, sm_rows = _pack([W[n] for n in SHARDED_SMALL], 8)
    first = _gather([w_in[0].astype(BF16), sm_flat], name="gather_first")
    rest0 = _gather_start([W[n][0].astype(BF16) for n in REST], me, name="gather_rest0_start")
    full = [{"w_in": _layer_full(first[0], "w_in")}, None]
    conv_w = {n: _full_from_shards(s, n) for n, s in zip(SHARDED_SMALL, _unpack(first[1], sm_rows, [W[n].shape for n in SHARDED_SMALL]))}

    bucket = jnp.asarray(_t5_buckets())
    bias = _bias_table(rel_bias, bucket, name="bias_table")

    saved = []
    h0, h0b = xs, xs.astype(BF16)
    h0t = h0b.T
    for l in range(DEPTH):
        gbias = _row128(jnp.concatenate([ml_i_bias[l], ml_f_bias[l]]))
        sinks = _row128(swa_sinks[l])
        ng = ml_norm_g[l].reshape(1, ML_W)
        proj = _mm(h0b, full[l]["w_in"], dep=(rest0[-1],) if l == 0 else (), name=f"proj{l}")
        qk = _silu_conv_fwd(proj, conv_w["ml_conv_w"][l], ml_conv_b[l].reshape(1, -1), name=f"mlconv{l}")
        h_ml, cs, ns, ms = _mlstm_fwd(qk, proj, gbias, ng, name=f"mlstm{l}")
        h_sw = _swa_fwd(proj, bias, sinks, name=f"swa{l}")
        dep = ()
        if l == 0:
            landed = _gather_finish(rest0, h_sw, name="gather_rest0")
            full[0].update({n: _layer_full(s, n) for n, s in zip(REST, landed)})
            layer1 = _gather_start([W[n][1].astype(BF16) for n in BIG], me, after=(landed[0],), name="gather_layer1_start")
            dep = (layer1[-1],)
        fw = full[l]
        h1, h1b, h1t, z1 = _mm_res_ln([h_ml, h_sw], fw["w_out"], h0, ln1_g[l].reshape(1, D), ln1_b[l].reshape(1, D),
                                      name=f"mix_out{l}")
        kv = _mm(mems, fw["xa_wkv"], tm=256, dep=dep, name=f"xa_kv{l}")
        o = _xattn_fwd(h1b, fw["xa_wq"], kv, name=f"xattn{l}")
        h2, h2b, h2t, z2 = _mm_res_ln([o], fw["xa_wo"], h1, ln2_g[l].reshape(1, D), ln2_b[l].reshape(1, D), name=f"xa_out{l}")
        act, act_t = _ffn_gate_fwd(h2b, fw["ffn_w_up"], conv_w["ffn_conv_w"][l], ffn_conv_b[l].reshape(1, -1), name=f"ffn_gate{l}")
        h3, h3b, h3t, z3 = _mm_res_ln([act], fw["ffn_w_down"], h2, ln3_g[l].reshape(1, D), ln3_b[l].reshape(1, D),
                                      name=f"ffn_out{l}")
        saved.append(dict(h0t=h0t, proj=proj, qk=qk, cs=cs, ns=ns, ms=ms, h_ml=h_ml, h_sw=h_sw, z1=z1, h1b=h1b, h1t=h1t, kv=kv, o=o,
                          z2=z2, h2b=h2b, h2t=h2t, act_t=act_t, z3=z3, gbias=gbias, sinks=sinks, ng=ng))
        h0, h0b, h0t = h3, h3b, h3t
        if l == 0:
            landed = _gather_finish(layer1, h3b, name="gather_layer1")
            full[1] = {n: _layer_full(s, n) for n, s in zip(BIG, landed)}

    G = {n: [None] * DEPTH for n in NAMES if n != "rel_bias"}
    dbias = [None] * DEPTH
    pending = []
    dz3, dz3b, G["ln3_g"][DEPTH - 1], G["ln3_b"][DEPTH - 1], loss_part = _grad_in(
        [], None, ln=(saved[-1]["z3"], ln3_g[DEPTH - 1].reshape(1, D)), loss=(h0, tgt), name="loss_head")
    for l in reversed(range(DEPTH)):
        sv, fw = saved[l], full[l]
        win = fw["w_in"]
        G["ffn_w_down"][l] = _wgrad(sv["act_t"], dz3b, name=f"d_w_down{l}")
        dupg, dupv, G["ffn_conv_w"][l], G["ffn_conv_b"][l] = _ffn_gate_bwd(
            dz3b, fw["ffn_w_down"], sv["h2b"], fw["ffn_w_up"], conv_w["ffn_conv_w"][l], ffn_conv_b[l].reshape(1, -1),
            name=f"ffn_gate_bwd{l}")
        G["ffn_w_up"][l] = jnp.concatenate([_wgrad(sv["h2t"], dupg, name=f"d_w_up_g{l}"),
                                            _wgrad(sv["h2t"], dupv, name=f"d_w_up_v{l}")], axis=1)
        dep = ()
        if l == 0:
            names = ("ffn_w_up", "ffn_w_down")
            pending.append((names, 0, "ffn0", _reduce_start("ffn0", names, [G[n][0] for n in names], me)))
            dep = (pending[-1][3][-1],)
        dz2, dz2b, G["ln2_g"][l], G["ln2_b"][l] = _grad_in(
            [(dupg, fw["ffn_w_up"], 0), (dupv, fw["ffn_w_up"], 1)], dz3, ln=(sv["z2"], ln2_g[l].reshape(1, D)), dep=dep,
            name=f"d_h2_{l}")
        G["xa_wo"][l] = _mm_tn(sv["o"], dz2b, name=f"d_xa_wo{l}")
        dq, dkv = _xattn_bwd(dz2b, fw["xa_wo"], sv["h1b"], fw["xa_wq"], sv["kv"], name=f"xattn_bwd{l}")
        G["xa_wkv"][l] = _mm_tn(mems, dkv, name=f"d_xa_wkv{l}")
        G["xa_wq"][l] = _wgrad(sv["h1t"], dq, name=f"d_xa_wq{l}")
        dep = ()
        if l == 0:
            names = ("xa_wq", "xa_wkv", "xa_wo")
            pending.append((names, 0, "xa0", _reduce_start("xa0", names, [G[n][0] for n in names], me)))
            dep = (pending[-1][3][-1],)
        dz1, dz1b, G["ln1_g"][l], G["ln1_b"][l] = _grad_in(
            [(dq, fw["xa_wq"], 0)], dz2, ln=(sv["z1"], ln1_g[l].reshape(1, D)), dep=dep, name=f"d_h1_{l}")
        G["w_out"][l] = jnp.concatenate([_mm_tn(sv["h_ml"], dz1b, name=f"d_w_out_ml{l}"),
                                         _mm_tn(sv["h_sw"], dz1b, name=f"d_w_out_sw{l}")], axis=0)
        dep = ()
        if l == 0:
            pending.append((("w_out",), 0, "out0", _reduce_start("out0", ("w_out",), [G["w_out"][0]], me)))
            dep = (pending[-1][3][-1],)
        dhcat = _mm(dz1b, fw["w_out"], trans_b=True, dep=dep, name=f"d_hcat{l}")
        dsw, dbias[l], dsinks = _swa_bwd(dhcat, sv["proj"], bias, sv["sinks"], name=f"swa_bwd{l}")
        dqk, dml, dgb, dng = _mlstm_bwd(dhcat, sv["qk"], sv["proj"], sv["gbias"], sv["ng"], sv["cs"], sv["ns"], sv["ms"],
                                        name=f"mlstm_bwd{l}")
        dqk_pre, G["ml_conv_w"][l], G["ml_conv_b"][l] = _silu_conv_bwd(
            dqk, sv["proj"], conv_w["ml_conv_w"][l], ml_conv_b[l].reshape(1, -1), name=f"mlconv_bwd{l}")
        dw_qk = _wgrad(sv["h0t"], dqk_pre, name=f"d_w_in_qk{l}")
        dw_ml = _wgrad(sv["h0t"], dml, name=f"d_w_in_ml{l}")
        dw_sw = _wgrad(sv["h0t"], dsw, name=f"d_w_in_sw{l}")
        G["w_in"][l] = jnp.concatenate([dw_qk, dw_ml[:, :2 * ML_W + 2 * ML_H], dw_sw], axis=1)
        win_ml = jnp.concatenate([win[:, 1024:2048], win[:, 2816:2944]], axis=1)
        pairs = [(dqk_pre, win, 0), (dml, win_ml, 0), (dsw, win[:, 2048:2816], 0)]
        G["ml_i_bias"][l] = dgb[0, :ML_H]
        G["ml_f_bias"][l] = dgb[0, ML_H:2 * ML_H]
        G["ml_norm_g"][l] = dng
        G["swa_sinks"][l] = dsinks[0, :SW_H]
        if l > 0:
            pending.append((BIG, l, f"l{l}", _reduce_start(f"l{l}", BIG, [G[n][l] for n in BIG], me)))
            dz3, dz3b, G["ln3_g"][l - 1], G["ln3_b"][l - 1] = _grad_in(
                pairs, dz1, ln=(saved[l - 1]["z3"], ln3_g[l - 1].reshape(1, D)), dep=(pending[-1][3][-1],), name=f"d_h0_{l}")
        else:
            pending.append((("w_in",), 0, "in0", _reduce_start("in0", ("w_in",), [G["w_in"][0]], me)))
            dh = _grad_in(pairs, dz1, dep=(pending[-1][3][-1],), name="d_h0_0")
            grad_x = dh.reshape(x.shape)

    parts = {}
    for names, l, tag, started in pending:
        landed = _split_wait(started, dh, _scatter_copies, name=f"reduce_{tag}_wait")
        parts.update({(n, l): p for n, p in zip(names, landed)})
    res = {}
    for n in BIG:
        outs = None
        for l in reversed(range(DEPTH)):
            outs = _adamw(parts[n, l], W[n], Mo[n], Vo[n], layer=l, prev=outs, name=f"adamw_{n}{l}")
        for kind, a in zip(("g", "d", "m", "v"), outs):
            res[kind, n] = a

    Gf = {n: jnp.stack([g.reshape(W[n].shape[1:]) if n in REPLICATED else g for g in G[n]]) for n in G if n not in BIG}
    Gf["rel_bias"] = _bias_table_bwd(dbias, bucket, name="bias_table_bwd")[:, :SW_H]
    res_small, loss = _update_small(Gf, loss_part, W, Mo, Vo, me)
    res.update(res_small)
    return (loss, grad_x, *[res["g", n] for n in NAMES], *[res["d", n] for n in NAMES], *[res["m", n] for n in NAMES],
            *[res["v", n] for n in NAMES])
```

```python
import functools
import math

import jax
import jax.numpy as jnp
import numpy as np
from jax import lax
from jax.experimental import pallas as pl
from jax.experimental.pallas import tpu as pltpu

F32 = jnp.float32
BF16 = jnp.bfloat16

N_DEV = 8
D = 1024
DEPTH = 2
ML_H = 4
ML_W = 512
ML_DH = 128
ML_L = 64
ML_CONV = 4
SW_DH = 64
SW_W = 512
SW_H = 8
SW_G = 4
SW_KVW = 128
BLK = 128
REL_B = 32
REL_MAXD = 128
XA_H = 4
XA_DH = 256
DFF = 2816
NB_FF = DFF // 128
FFN_CONV = 3
ALPHA = (2.0 * DEPTH) ** 0.25
EPS = 1e-5
N_IN = 2824
NP_IN = 3072
ML_GW = 2 * ML_W + 128
SW_GW = SW_W + 2 * SW_KVW
ADAM_LR = 0.001
ADAM_B1 = 0.9
ADAM_B2 = 0.999
ADAM_EPS = 1e-08
ADAM_WD = 0.01
ADAM_STEP = 10
VMEM_LIMIT = 56 * 1024 * 1024
MESH = pl.DeviceIdType.MESH

NN = ((1,), (0,))
NT = ((1,), (1,))
TN = ((0,), (0,))


def _dg(a, b, dn):
    if a.ndim == 3:
        dims = (((dn[0][0] + 1,), (dn[1][0] + 1,)), ((0,), (0,)))
    else:
        dims = (dn, ((), ()))
    return lax.dot_general(a.astype(BF16), b.astype(BF16), dims, preferred_element_type=F32)


@jax.custom_vjp
def dot_nn(a, b):
    return _dg(a, b, NN)


dot_nn.defvjp(lambda a, b: (_dg(a, b, NN), (a, b)), lambda r, g: (_dg(g, r[1], NT), _dg(r[0], g, TN)))


@jax.custom_vjp
def dot_nt(a, b):
    return _dg(a, b, NT)


dot_nt.defvjp(lambda a, b: (_dg(a, b, NT), (a, b)), lambda r, g: (_dg(g, r[1], NN), _dg(g, r[0], TN)))


@jax.custom_vjp
def dot_tn(a, b):
    return _dg(a, b, TN)


dot_tn.defvjp(lambda a, b: (_dg(a, b, TN), (a, b)), lambda r, g: (_dg(r[1], g, NT), _dg(r[0], g, NN)))


def _params(sem=None):
    return pltpu.CompilerParams(dimension_semantics=sem, vmem_limit_bytes=VMEM_LIMIT)


def _sds(shape, dtype):
    return jax.ShapeDtypeStruct(tuple(shape), dtype)


TOKEN = pl.BlockSpec((8, 128), lambda *_: (0, 0))


def _mm(a, b, *, trans_b=False, out_dtype=F32, add=None, add_scale=1.0, tm=1024, tn=512, dep=(), name):
    a_list = list(a) if isinstance(a, (list, tuple)) else [a]
    M = a_list[0].shape[0]
    N = b.shape[0] if trans_b else b.shape[1]
    tm = min(tm, M)
    tn = next(t for t in (tn, 384, 256, 128) if N % t == 0)
    assert M % tm == 0
    Ka = a_list[0].shape[1]
    assert all(t.shape[1] == Ka for t in a_list)
    tk = next(t for t in (Ka, 1408, 1024) if Ka % t == 0 and t <= 1408)
    na, npa = len(a_list), Ka // tk
    nk = na * npa
    has_add = add is not None

    def body(*refs):
        a_refs, b_ref = refs[:na], refs[na]
        add_ref = refs[na + 1] if has_add else None
        o_ref, acc_ref = refs[-2], refs[-1]
        k = pl.program_id(2)

        def finish(r):
            if has_add:
                r = r + add_scale * add_ref[...].astype(F32)
            o_ref[...] = r.astype(out_dtype)

        for t, a_ref in enumerate(a_refs):
            def step(a_ref=a_ref):
                p = _dg(a_ref[...], b_ref[...], NT if trans_b else NN)
                if nk == 1:
                    finish(p)
                    return

                @pl.when(k == 0)
                def _():
                    acc_ref[...] = p

                @pl.when((k > 0) & (k < nk - 1))
                def _():
                    acc_ref[...] += p

                @pl.when(k == nk - 1)
                def _():
                    finish(acc_ref[...] + p)

            if na == 1:
                step()
            else:
                pl.when((k >= t * npa) & (k < (t + 1) * npa))(step)

    in_specs = [pl.BlockSpec((tm, tk), lambda i, j, k, t=t: (i, jnp.clip(k - t * npa, 0, npa - 1))) for t in range(na)]
    in_specs.append(pl.BlockSpec((tn, tk), lambda i, j, k: (j, k)) if trans_b else pl.BlockSpec((tk, tn), lambda i, j, k: (k, j)))
    args = a_list + [b]
    if has_add:
        in_specs.append(pl.BlockSpec((tm, tn), lambda i, j, k: (i, j)))
        args.append(add)
    in_specs += [TOKEN] * len(dep)
    args += list(dep)
    return pl.pallas_call(
        body, name=name, grid=(M // tm, N // tn, nk), in_specs=in_specs,
        out_specs=pl.BlockSpec((tm, tn), lambda i, j, k: (i, j)), out_shape=_sds((M, N), out_dtype),
        scratch_shapes=[pltpu.VMEM((tm, tn) if nk > 1 else (8, 128), F32)],
        compiler_params=_params(("parallel", "parallel", "arbitrary")))(*args)


def _wgrad(a_t, g, *, name):
    K, N = a_t.shape[0], g.shape[1]
    return _mm(a_t, g, tm=K if K <= 1024 else K // 2, tn=next(t for t in range(1536, 0, -128) if N % t == 0), name=name)


def _mm_tn(a, g, *, name):
    S, K = a.shape
    N = g.shape[1]
    tk = K if K <= 1024 else K // 2
    tn = next(t for t in range(1536, 0, -128) if N % t == 0)
    ts = min(512, S)
    ns = S // ts
    assert K % tk == 0 and S % ts == 0

    def body(a_ref, g_ref, o_ref):
        s = pl.program_id(2)
        p = _dg(a_ref[...], g_ref[...], TN)

        @pl.when(s == 0)
        def _():
            o_ref[...] = p

        @pl.when(s > 0)
        def _():
            o_ref[...] += p

    return pl.pallas_call(
        body, name=name, grid=(K // tk, N // tn, ns),
        in_specs=[pl.BlockSpec((ts, tk), lambda i, j, s: (s, i)), pl.BlockSpec((ts, tn), lambda i, j, s: (s, j))],
        out_specs=pl.BlockSpec((tk, tn), lambda i, j, s: (i, j)), out_shape=_sds((K, N), F32),
        compiler_params=_params(("parallel", "parallel", "arbitrary")))(a, g)


def _mm_res_ln(a_list, w, resid, gam, bet, *, name):
    M, Ka = a_list[0].shape
    na = len(a_list)
    assert w.shape[0] == na * Ka
    tm = min(256, M)

    def body(*refs):
        a_refs, w_refs = refs[:na], refs[na:2 * na]
        r_ref, g_ref, b_ref, y_ref, yb_ref, yt_ref, z_ref = refs[2 * na:]
        z = ALPHA * r_ref[...]
        for a_ref, w_ref in zip(a_refs, w_refs):
            z = z + _dg(a_ref[...], w_ref[...], NN)
        mu = jnp.mean(z, axis=1, keepdims=True)
        zc = z - mu
        var = jnp.mean(zc * zc, axis=1, keepdims=True)
        y = zc * lax.rsqrt(var + EPS) * g_ref[...] + b_ref[...]
        y_ref[...] = y
        yb_ref[...] = y.astype(BF16)
        yt_ref[...] = y.T.astype(BF16)
        z_ref[...] = z

    row = pl.BlockSpec((tm, D), lambda i: (i, 0))
    vec = pl.BlockSpec((1, D), lambda i: (0, 0))
    a_specs = [pl.BlockSpec((tm, Ka), lambda i: (i, 0)) for _ in a_list]
    w_specs = [pl.BlockSpec((Ka, D), lambda i, t=t: (t, 0)) for t in range(na)]
    return pl.pallas_call(
        body, name=name, grid=(M // tm,), in_specs=a_specs + w_specs + [row, vec, vec],
        out_specs=[row, row, pl.BlockSpec((D, tm), lambda i: (0, i)), row],
        out_shape=[_sds((M, D), F32), _sds((M, D), BF16), _sds((D, M), BF16), _sds((M, D), F32)],
        compiler_params=_params(("parallel",)))(*a_list, *([w] * na), resid, gam, bet)


def _grad_in(pairs, add, *, ln=None, loss=None, dep=(), name):
    M = (add if add is not None else loss[0]).shape[0]
    tm = min(256, M)
    npair, nd = len(pairs), len(dep)
    has_ln, has_loss = ln is not None, loss is not None

    def body(*refs):
        n_in = 2 * npair + (2 if has_loss else 1) + 2 * has_ln + nd
        ins, outs = refs[:n_in], refs[n_in:]
        i = pl.program_id(0)
        pos = 2 * npair
        if has_loss:
            e = ins[pos][...] - ins[pos + 1][...]
            pos += 2
            dy = e * (1.0 / D)
            part = 0.5 * jnp.sum(jnp.sum(e * e, axis=1, keepdims=True) * (1.0 / D), axis=0, keepdims=True)
        else:
            dy = ALPHA * ins[pos][...]
            pos += 1
            for t in range(npair):
                dy = dy + _dg(ins[2 * t][...], ins[2 * t + 1][...], NT)
        if not has_ln:
            outs[0][...] = dy
            return
        z, g_ref = ins[pos][...], ins[pos + 1]
        mu = jnp.mean(z, axis=1, keepdims=True)
        zc = z - mu
        var = jnp.mean(zc * zc, axis=1, keepdims=True)
        rstd = lax.rsqrt(var + EPS)
        xh = zc * rstd
        dxh = dy * g_ref[...]
        m1 = jnp.mean(dxh, axis=1, keepdims=True)
        m2 = jnp.mean(dxh * xh, axis=1, keepdims=True)
        dz = rstd * (dxh - m1 - xh * m2)
        outs[0][...] = dz
        outs[1][...] = dz.astype(BF16)
        acc = [(outs[2], jnp.sum(dy * xh, axis=0, keepdims=True)), (outs[3], jnp.sum(dy, axis=0, keepdims=True))]
        if has_loss:
            acc.append((outs[4], jnp.broadcast_to(part, (8, 128))))

        @pl.when(i == 0)
        def _():
            for ref, val in acc:
                ref[...] = val

        @pl.when(i > 0)
        def _():
            for ref, val in acc:
                ref[...] += val

    row = pl.BlockSpec((tm, D), lambda i: (i, 0))
    vec = pl.BlockSpec((1, D), lambda i: (0, 0))
    in_specs, args = [], []
    for a, b, blk in pairs:
        in_specs += [pl.BlockSpec((tm, a.shape[1]), lambda i: (i, 0)), pl.BlockSpec((D, a.shape[1]), lambda i, blk=blk: (0, blk))]
        args += [a, b]
    if has_loss:
        in_specs += [row, row]
        args += list(loss)
    else:
        in_specs.append(row)
        args.append(add)
    if has_ln:
        in_specs += [row, vec]
        args += list(ln)
    in_specs += [TOKEN] * nd
    args += list(dep)
    if has_ln:
        out_specs = [row, row, vec, vec] + ([pl.BlockSpec((8, 128), lambda i: (0, 0))] if has_loss else [])
        out_shape = [_sds((M, D), F32), _sds((M, D), BF16), _sds((1, D), F32), _sds((1, D), F32)] + ([_sds((8, 128), F32)] if has_loss else [])
    else:
        out_specs, out_shape = row, _sds((M, D), F32)
    return pl.pallas_call(
        body, name=name, grid=(M // tm,), in_specs=in_specs, out_specs=out_specs, out_shape=out_shape,
        compiler_params=_params(("arbitrary",) if has_ln else ("parallel",)))(*args)


def _shift_down(x, d):
    if d == 0:
        return x
    rows = lax.broadcasted_iota(jnp.int32, x.shape, 0)
    return jnp.where(rows >= d, pltpu.roll(x, d, 0), 0.0)


def _shift_up(x, d):
    if d == 0:
        return x
    S = x.shape[0]
    rows = lax.broadcasted_iota(jnp.int32, x.shape, 0)
    return jnp.where(rows < S - d, pltpu.roll(x, S - d, 0), 0.0)


def _conv(x, w_ref, b_ref, cs, K):
    y = b_ref[:, cs]
    for j in range(K):
        y = y + _shift_down(x, K - 1 - j) * w_ref[j:j + 1, cs]
    return y


def _conv_bwd(dy, x, w_ref, dw_ref, db_ref, cs, K):
    dx = jnp.zeros_like(x)
    for j in range(K):
        sdy = _shift_up(dy, K - 1 - j)
        dx = dx + sdy * w_ref[j:j + 1, cs]
        dw_ref[j:j + 1, cs] = jnp.sum(sdy * x, axis=0, keepdims=True)
    db_ref[:, cs] = jnp.sum(dy, axis=0, keepdims=True)
    return dx


ALL = slice(None)


def _silu_conv_fwd(proj, cw, cb, *, name):
    S = proj.shape[0]

    def body(x_ref, w_ref, b_ref, o_ref):
        o_ref[...] = jax.nn.silu(_conv(x_ref[...], w_ref, b_ref, ALL, ML_CONV))

    col = pl.BlockSpec((S, 128), lambda j: (0, j))
    return pl.pallas_call(
        body, name=name, grid=(8,),
        in_specs=[col, pl.BlockSpec((ML_CONV, 128), lambda j: (0, j)), pl.BlockSpec((1, 128), lambda j: (0, j))],
        out_specs=col, out_shape=_sds((S, 2 * ML_W), F32), compiler_params=_params(("parallel",)))(proj, cw, cb)


def _silu_conv_bwd(dqk, proj, cw, cb, *, name):
    S = proj.shape[0]

    def body(d_ref, x_ref, w_ref, b_ref, dx_ref, dw_ref, db_ref):
        x = x_ref[...]
        y = _conv(x, w_ref, b_ref, ALL, ML_CONV)
        dy = jax.vjp(jax.nn.silu, y)[1](d_ref[...])[0]
        dx_ref[...] = _conv_bwd(dy, x, w_ref, dw_ref, db_ref, ALL, ML_CONV).astype(BF16)

    col = pl.BlockSpec((S, 128), lambda j: (0, j))
    wsp = pl.BlockSpec((ML_CONV, 128), lambda j: (0, j))
    bsp = pl.BlockSpec((1, 128), lambda j: (0, j))
    return pl.pallas_call(
        body, name=name, grid=(8,), in_specs=[col, col, wsp, bsp], out_specs=[col, wsp, bsp],
        out_shape=[_sds((S, 2 * ML_W), BF16), _sds((ML_CONV, 2 * ML_W), F32), _sds((1, 2 * ML_W), F32)],
        compiler_params=_params(("parallel",)))(dqk, proj, cw, cb)


GELU_C0 = math.sqrt(2.0 / math.pi)
GELU_C1 = 0.044715


def _gate_bwd(ug, uv, da):
    t = jnp.tanh(GELU_C0 * (ug + GELU_C1 * (ug * ug * ug)))
    half = 0.5 * (1.0 + t)
    dgelu = half + 0.5 * ug * (1.0 - t * t) * (GELU_C0 * (1.0 + 3.0 * GELU_C1 * (ug * ug)))
    return da * uv * dgelu, da * (ug * half)


def _up_pair(h_ref, ugw_ref, uvw_ref):
    return _dg(h_ref[...], jnp.concatenate([ugw_ref[...], uvw_ref[...]], axis=1), NN)


def _ffn_specs(S):
    return (pl.BlockSpec((D, 128), lambda j: (0, j)), pl.BlockSpec((D, 128), lambda j: (0, j + NB_FF)),
            pl.BlockSpec((FFN_CONV, 128), lambda j: (0, j)), pl.BlockSpec((FFN_CONV, 128), lambda j: (0, j + NB_FF)),
            pl.BlockSpec((1, 128), lambda j: (0, j)), pl.BlockSpec((1, 128), lambda j: (0, j + NB_FF)))


def _ffn_gate_fwd(hb, w_up, cw, cb, *, name):
    S = hb.shape[0]
    ug_, uv_, wg, wv, bg, bv = _ffn_specs(S)

    def body(h_ref, ugw_ref, uvw_ref, wg_ref, wv_ref, bg_ref, bv_ref, o_ref, ot_ref, h_s):
        @pl.when(pl.program_id(0) == 0)
        def _():
            pltpu.sync_copy(h_ref, h_s)

        x2 = _up_pair(h_s, ugw_ref, uvw_ref)
        ug = _conv(x2[:, :128], wg_ref, bg_ref, ALL, FFN_CONV)
        uv = _conv(x2[:, 128:], wv_ref, bv_ref, ALL, FFN_CONV)
        act = jax.nn.gelu(ug) * uv
        o_ref[...] = act.astype(BF16)
        ot_ref[...] = act.T.astype(BF16)

    return pl.pallas_call(
        body, name=name, grid=(NB_FF,), in_specs=[ANY, ug_, uv_, wg, wv, bg, bv],
        out_specs=[pl.BlockSpec((S, 128), lambda j: (0, j)), pl.BlockSpec((128, S), lambda j: (j, 0))],
        out_shape=[_sds((S, DFF), BF16), _sds((DFF, S), BF16)],
        scratch_shapes=[pltpu.VMEM((S, D), BF16)],
        compiler_params=_params(("arbitrary",)))(hb, w_up, w_up, cw, cw, cb, cb)


def _ffn_gate_bwd(dzb, w_down, hb, w_up, cw, cb, *, name):
    S = hb.shape[0]
    ug_, uv_, wg, wv, bg, bv = _ffn_specs(S)

    def body(dz_ref, h_ref, wd_ref, ugw_ref, uvw_ref, wg_ref, wv_ref, bg_ref, bv_ref,
             dxg_ref, dxv_ref, dwg_ref, dwv_ref, dbg_ref, dbv_ref, dz_s, h_s):
        @pl.when(pl.program_id(0) == 0)
        def _():
            pltpu.sync_copy(dz_ref, dz_s)
            pltpu.sync_copy(h_ref, h_s)

        x2 = _up_pair(h_s, ugw_ref, uvw_ref)
        xg, xv = x2[:, :128], x2[:, 128:]
        ug = _conv(xg, wg_ref, bg_ref, ALL, FFN_CONV)
        uv = _conv(xv, wv_ref, bv_ref, ALL, FFN_CONV)
        dug, duv = _gate_bwd(ug, uv, _dg(dz_s[...], wd_ref[...], NT))
        dxg_ref[...] = _conv_bwd(dug, xg, wg_ref, dwg_ref, dbg_ref, ALL, FFN_CONV).astype(BF16)
        dxv_ref[...] = _conv_bwd(duv, xv, wv_ref, dwv_ref, dbv_ref, ALL, FFN_CONV).astype(BF16)

    col = pl.BlockSpec((S, 128), lambda j: (0, j))
    half = _sds((S, DFF), BF16)
    dxg, dxv, dwg, dwv, dbg, dbv = pl.pallas_call(
        body, name=name, grid=(NB_FF,),
        in_specs=[ANY, ANY, pl.BlockSpec((128, D), lambda j: (j, 0)), ug_, uv_, wg, wv, bg, bv],
        out_specs=[col, col, wg, wg, bg, bg],
        out_shape=[half, half, _sds((FFN_CONV, DFF), F32), _sds((FFN_CONV, DFF), F32), _sds((1, DFF), F32), _sds((1, DFF), F32)],
        scratch_shapes=[pltpu.VMEM((S, D), BF16), pltpu.VMEM((S, D), BF16)],
        compiler_params=_params(("arbitrary",)))(dzb, hb, w_down, w_up, w_up, cw, cw, cb, cb)
    return dxg, dxv, jnp.concatenate([dwg, dwv], axis=1), jnp.concatenate([dbg, dbv], axis=1)


def _log_sigmoid(x):
    return jnp.minimum(x, 0.0) - jnp.log1p(jnp.exp(-jnp.abs(x)))


@jax.custom_vjp
def _clamp_div(num, den, floor, shift):
    return num / jnp.maximum(jnp.abs(den), floor)


def _clamp_div_fwd(num, den, floor, shift):
    out = num / jnp.maximum(jnp.abs(den), floor)
    return out, (den, floor, out)


def _clamp_div_bwd(res, g):
    den, floor, out = res
    active = jnp.abs(den) < floor
    dinv = jnp.maximum(jnp.abs(den), floor)
    go = jnp.sum(g * out, axis=-1, keepdims=True)
    ddiv = -go / dinv
    return (g / dinv, jnp.where(active, 0.0, ddiv * jnp.sign(den)), jnp.where(active, ddiv, 0.0),
            jnp.sum(jnp.where(active, go, 0.0), axis=-2, keepdims=True))


_clamp_div.defvjp(_clamp_div_fwd, _clamp_div_bwd)


def _ml_heads(q, k, v, o_pre, gates, gbias, C, n, ng, m, shift):
    H, L, _ = q.shape
    lane1 = lax.broadcasted_iota(jnp.int32, (1, 128), 1)
    gz = gates + jnp.where(lane1 < ML_H, lax.stop_gradient(gbias), gbias)
    gz = jnp.broadcast_to(gz[None], (H, L, 128))
    hid = lax.broadcasted_iota(jnp.int32, (H, L, 128), 0)
    lane = lax.broadcasted_iota(jnp.int32, (H, L, 128), 2)
    ig = jnp.sum(jnp.where(lane == hid, gz, 0.0), axis=2, keepdims=True)
    lf = _log_sigmoid(jnp.sum(jnp.where(lane == ML_H + hid, gz, 0.0), axis=2, keepdims=True))
    r = lax.broadcasted_iota(jnp.int32, (H, L, L), 1)
    c = lax.broadcasted_iota(jnp.int32, (H, L, L), 2)
    eye, tril = r == c, r >= c

    def to_row(col):
        return jnp.sum(jnp.where(eye, col, 0.0), axis=1, keepdims=True)

    b_col = jnp.sum(jnp.where(tril, to_row(lf), 0.0), axis=2, keepdims=True)
    Dm = jnp.where(tril, b_col - to_row(b_col) + to_row(ig), -jnp.inf)
    inter = b_col + m
    m_t = lax.stop_gradient(jnp.maximum(inter, jnp.max(Dm, axis=2, keepdims=True)))
    w_inter = jnp.exp(inter - m_t)
    ks = k * (ML_DH ** -0.5)
    s = dot_nt(q, ks) * jnp.exp(Dm - m_t)
    num = w_inter * dot_nn(q, C) + dot_nn(s, v)
    den = w_inter * jnp.sum(q * n, axis=2, keepdims=True) + jnp.sum(s, axis=2, keepdims=True)
    h = _clamp_div(num, den, jnp.exp(-m_t), shift)
    g = jnp.sum(lf, axis=1, keepdims=True)
    a = g - b_col + ig
    m_new = lax.stop_gradient(jnp.maximum(g + m, jnp.max(a, axis=1, keepdims=True)))
    decay = jnp.exp(g + m - m_new)
    wk = jnp.exp(a - m_new)
    C_new = decay * C + dot_tn(ks * wk, v)
    n_new = decay * n + jnp.sum(wk * ks, axis=1, keepdims=True)
    mu = jnp.mean(h, axis=2, keepdims=True)
    hc = h - mu
    var = jnp.mean(hc * hc, axis=2, keepdims=True)
    out = jax.nn.sigmoid(o_pre) * (hc * lax.rsqrt(var + EPS) * ng)
    return out, C_new, n_new, m_new


def _hs(h, off=0):
    return slice(off + h * ML_DH, off + (h + 1) * ML_DH)


def _heads(ref, off=0):
    return jnp.stack([ref[:, _hs(h, off)] for h in range(ML_H)])


def _mlstm_fwd(qk, proj, gbias, ng, *, name):
    S = qk.shape[0]
    nc = S // ML_L

    def body(q_ref, k_ref, v_ref, o_ref, g_ref, gb_ref, ng_ref, h_ref, cs_ref, ns_ref, ms_ref, c_s, n_s, m_s):
        @pl.when(pl.program_id(0) == 0)
        def _():
            c_s[...] = jnp.zeros_like(c_s)
            n_s[...] = jnp.zeros_like(n_s)
            m_s[...] = jnp.zeros_like(m_s)

        C, n = c_s[...], n_s[...]
        cs_ref[0] = C
        ns_ref[0] = n
        ms_ref[0] = m_s[...]
        out, C2, n2, m2 = _ml_heads(_heads(q_ref), _heads(k_ref), _heads(v_ref), _heads(o_ref), g_ref[...], gb_ref[...], C, n,
                                    _heads(ng_ref), m_s[:, :, 0:1], jnp.zeros((ML_H, 1, 1), F32))
        for h in range(ML_H):
            h_ref[:, _hs(h)] = out[h].astype(BF16)
        c_s[...] = C2
        n_s[...] = n2
        m_s[...] = jnp.broadcast_to(m2, (ML_H, 1, 128))

    def w(j):
        return pl.BlockSpec((ML_L, ML_W), lambda c, j=j: (c, j))

    return pl.pallas_call(
        body, name=name, grid=(nc,),
        in_specs=[w(0), w(1), w(2), w(3), pl.BlockSpec((ML_L, 128), lambda c: (c, 22)),
                  pl.BlockSpec((1, 128), lambda c: (0, 0)), pl.BlockSpec((1, ML_W), lambda c: (0, 0))],
        out_specs=[w(0), pl.BlockSpec((1, ML_H, ML_DH, ML_DH), lambda c: (c, 0, 0, 0)),
                   pl.BlockSpec((1, ML_H, 1, 128), lambda c: (c, 0, 0, 0)), pl.BlockSpec((1, ML_H, 1, 128), lambda c: (c, 0, 0, 0))],
        out_shape=[_sds((S, ML_W), BF16), _sds((nc, ML_H, ML_DH, ML_DH), F32), _sds((nc, ML_H, 1, 128), F32),
                   _sds((nc, ML_H, 1, 128), F32)],
        scratch_shapes=[pltpu.VMEM((ML_H, ML_DH, ML_DH), F32), pltpu.VMEM((ML_H, 1, 128), F32), pltpu.VMEM((ML_H, 1, 128), F32)],
        compiler_params=_params(("arbitrary",)))(qk, qk, proj, proj, proj, gbias, ng)


def _mlstm_bwd(dh, qk, proj, gbias, ng, cs, ns, ms, *, name):
    S = qk.shape[0]
    nc = S // ML_L

    def body(dh_ref, q_ref, k_ref, v_ref, o_ref, g_ref, gb_ref, ng_ref, cs_ref, ns_ref, ms_ref,
             dqk_ref, dml_ref, dgb_ref, dng_ref, dc_s, dn_s):
        @pl.when(pl.program_id(0) == 0)
        def _():
            dc_s[...] = jnp.zeros_like(dc_s)
            dn_s[...] = jnp.zeros_like(dn_s)
            dgb_ref[...] = jnp.zeros_like(dgb_ref)
            dng_ref[...] = jnp.zeros_like(dng_ref)

        m = ms_ref[0][:, :, 0:1]

        def f(q, k, v, o_pre, gates, gb, C, n, ng_, shift):
            return _ml_heads(q, k, v, o_pre, gates, gb, C, n, ng_, m, shift)[:3]

        _, vjp = jax.vjp(f, _heads(q_ref), _heads(k_ref), _heads(v_ref), _heads(o_ref), g_ref[...], gb_ref[...],
                         cs_ref[0], ns_ref[0], _heads(ng_ref), jnp.zeros((ML_H, 1, 1), F32))
        dq, dk, dv, do, dgates, dgb, dC, dn, dng, dshift = vjp((_heads(dh_ref), dc_s[...], dn_s[...]))
        lane1 = lax.broadcasted_iota(jnp.int32, (1, 128), 1)
        for h in range(ML_H):
            dqk_ref[:, _hs(h)] = dq[h]
            dqk_ref[:, _hs(h, ML_W)] = dk[h]
            dml_ref[:, _hs(h)] = dv[h].astype(BF16)
            dml_ref[:, _hs(h, ML_W)] = do[h].astype(BF16)
            dng_ref[:, _hs(h)] += dng[h]
            dgb = jnp.where(lane1 == h, dshift[h], dgb)
        dc_s[...] = dC
        dn_s[...] = dn
        dml_ref[:, 2 * ML_W:] = dgates.astype(BF16)
        dgb_ref[...] += dgb

    def w(j):
        return pl.BlockSpec((ML_L, ML_W), lambda c, j=j: (nc - 1 - c, j))

    vec = pl.BlockSpec((1, 128), lambda c: (0, 0))
    vecw = pl.BlockSpec((1, ML_W), lambda c: (0, 0))
    gsp = pl.BlockSpec((ML_L, 128), lambda c: (nc - 1 - c, 22))
    st = pl.BlockSpec((1, ML_H, 1, 128), lambda c: (nc - 1 - c, 0, 0, 0))
    return pl.pallas_call(
        body, name=name, grid=(nc,),
        in_specs=[w(0), w(0), w(1), w(2), w(3), gsp, vec, vecw,
                  pl.BlockSpec((1, ML_H, ML_DH, ML_DH), lambda c: (nc - 1 - c, 0, 0, 0)), st, st],
        out_specs=[pl.BlockSpec((ML_L, 2 * ML_W), lambda c: (nc - 1 - c, 0)), pl.BlockSpec((ML_L, ML_GW), lambda c: (nc - 1 - c, 0)),
                   vec, vecw],
        out_shape=[_sds((S, 2 * ML_W), F32), _sds((S, ML_GW), BF16), _sds((1, 128), F32), _sds((1, ML_W), F32)],
        scratch_shapes=[pltpu.VMEM((ML_H, ML_DH, ML_DH), F32), pltpu.VMEM((ML_H, 1, 128), F32)],
        compiler_params=_params(("arbitrary",)))(dh, qk, qk, proj, proj, proj, gbias, ng, cs, ns, ms)


def _t5_buckets():
    r = np.arange(BLK)[:, None]
    c = np.arange(2 * BLK)[None, :]
    n = np.maximum(BLK + r - c, 0)
    max_exact = REL_B // 2
    nf = np.maximum(n, 1).astype(np.float32)
    large = max_exact + (np.log(nf / np.float32(max_exact)) / np.float32(math.log(REL_MAXD / max_exact))
                         * np.float32(REL_B - max_exact)).astype(np.int32)
    large = np.minimum(large, REL_B - 1)
    return np.where(n < max_exact, n, large).astype(np.int32)


def _bias_table(rel_bias, bucket, *, name):
    def body(rb_ref, bk_ref, o_ref):
        bk = bk_ref[...]
        for h in range(SW_H):
            acc = jnp.zeros((BLK, 2 * BLK), F32)
            for b in range(REL_B):
                acc = jnp.where(bk == b, rb_ref[b, h], acc)
            o_ref[h] = acc

    return pl.pallas_call(
        body, name=name, in_specs=[pl.BlockSpec(memory_space=pltpu.SMEM), pl.BlockSpec(memory_space=pltpu.VMEM)],
        out_specs=pl.BlockSpec(memory_space=pltpu.VMEM), out_shape=_sds((SW_H, BLK, 2 * BLK), F32),
        compiler_params=_params())(rel_bias, bucket)


def _bias_table_bwd(dbias_list, bucket, *, name):
    nl = len(dbias_list)

    def body(*refs):
        d_refs, bk_ref, o_ref = refs[:nl], refs[nl], refs[nl + 1]
        bk = bk_ref[...]
        rows = lax.broadcasted_iota(jnp.int32, (REL_B, 128), 0)
        lanes = lax.broadcasted_iota(jnp.int32, (REL_B, 128), 1)
        acc = jnp.zeros((REL_B, 128), F32)
        for h in range(SW_H):
            d = d_refs[0][h]
            for d_ref in d_refs[1:]:
                d = d + d_ref[h]
            for b in range(REL_B):
                t = jnp.sum(jnp.sum(jnp.where(bk == b, d, 0.0), axis=0, keepdims=True), axis=1, keepdims=True)
                acc = jnp.where((rows == b) & (lanes == h), t, acc)
        o_ref[...] = acc

    vm = pl.BlockSpec(memory_space=pltpu.VMEM)
    return pl.pallas_call(
        body, name=name, in_specs=[vm] * (nl + 1), out_specs=vm, out_shape=_sds((REL_B, 128), F32),
        compiler_params=_params())(*dbias_list, bucket)


def _swa_heads(q, kp, kc, vp, vc, bp, bc, sinks, has_prev):
    def rep(t):
        return jnp.concatenate([t[g:g + 1] for g in range(SW_H // SW_G) for _ in range(SW_G)], axis=0)

    r = lax.broadcasted_iota(jnp.int32, (SW_H, BLK, BLK), 1)
    c = lax.broadcasted_iota(jnp.int32, (SW_H, BLK, BLK), 2)
    hid = lax.broadcasted_iota(jnp.int32, (SW_H, 1, 128), 0)
    lane = lax.broadcasted_iota(jnp.int32, (SW_H, 1, 128), 2)
    sink = jnp.sum(jnp.where(lane == hid, jnp.broadcast_to(sinks[None], (SW_H, 1, 128)), 0.0), axis=2, keepdims=True)
    lp = jnp.where((c > r) & has_prev, dot_nt(q, rep(kp)) * (SW_DH ** -0.5) + bp, -jnp.inf)
    lc = jnp.where(c <= r, dot_nt(q, rep(kc)) * (SW_DH ** -0.5) + bc, -jnp.inf)
    mx = jnp.maximum(jnp.maximum(jnp.max(lp, axis=2, keepdims=True), jnp.max(lc, axis=2, keepdims=True)), sink)
    mx = lax.stop_gradient(mx)
    pp, pc = jnp.exp(lp - mx), jnp.exp(lc - mx)
    den = jnp.sum(pp, axis=2, keepdims=True) + jnp.sum(pc, axis=2, keepdims=True) + jnp.exp(sink - mx)
    return dot_nn(pp / den, rep(vp)) + dot_nn(pc / den, rep(vc))


def _qs(h, off=0):
    return slice(off + h * SW_DH, off + (h + 1) * SW_DH)


def _split(ref, n):
    return jnp.stack([ref[:, _qs(h)] for h in range(n)])


def _swa_fwd(proj, bias, sinks, *, name):
    S = proj.shape[0]
    nb = S // BLK
    nkv = SW_H // SW_G

    def body(q_ref, kp_ref, kc_ref, vp_ref, vc_ref, b_ref, s_ref, o_ref):
        out = _swa_heads(_split(q_ref, SW_H), _split(kp_ref, nkv), _split(kc_ref, nkv), _split(vp_ref, nkv), _split(vc_ref, nkv),
                         b_ref[:, :, :BLK], b_ref[:, :, BLK:], s_ref[...], pl.program_id(0) > 0)
        for h in range(SW_H):
            o_ref[:, _qs(h)] = out[h].astype(BF16)

    def cur(j):
        return pl.BlockSpec((BLK, 128), lambda n, j=j: (n, j))

    def prev(j):
        return pl.BlockSpec((BLK, 128), lambda n, j=j: (jnp.maximum(n - 1, 0), j))

    return pl.pallas_call(
        body, name=name, grid=(nb,),
        in_specs=[pl.BlockSpec((BLK, SW_W), lambda n: (n, 4)), prev(20), cur(20), prev(21), cur(21),
                  pl.BlockSpec((SW_H, BLK, 2 * BLK), lambda n: (0, 0, 0)), pl.BlockSpec((1, 128), lambda n: (0, 0))],
        out_specs=pl.BlockSpec((BLK, SW_W), lambda n: (n, 0)), out_shape=_sds((S, SW_W), BF16),
        compiler_params=_params(("parallel",)))(proj, proj, proj, proj, proj, bias, sinks)


def _swa_bwd(dh, proj, bias, sinks, *, name):
    S = proj.shape[0]
    nb = S // BLK

    nkv = SW_H // SW_G

    def body(dh_ref, q_ref, kp_ref, kc_ref, vp_ref, vc_ref, b_ref, s_ref, dsw_ref, db_ref, ds_ref, ck_s, cv_s):
        i = pl.program_id(0)

        @pl.when(i == 0)
        def _():
            ck_s[...] = jnp.zeros_like(ck_s)
            cv_s[...] = jnp.zeros_like(cv_s)
            db_ref[...] = jnp.zeros_like(db_ref)
            ds_ref[...] = jnp.zeros_like(ds_ref)

        f = functools.partial(_swa_heads, has_prev=i < nb - 1)
        _, vjp = jax.vjp(f, _split(q_ref, SW_H), _split(kp_ref, nkv), _split(kc_ref, nkv), _split(vp_ref, nkv),
                         _split(vc_ref, nkv), b_ref[:, :, :BLK], b_ref[:, :, BLK:], s_ref[...])
        dq, dkp, dkc, dvp, dvc, dbp, dbc, ds = vjp(_split(dh_ref, SW_H))
        for h in range(SW_H):
            dsw_ref[:, _qs(h)] = dq[h].astype(BF16)
        for g in range(nkv):
            dsw_ref[:, _qs(g, SW_W)] = (dkc[g] + ck_s[:, _qs(g)]).astype(BF16)
            dsw_ref[:, _qs(g, SW_W + SW_KVW)] = (dvc[g] + cv_s[:, _qs(g)]).astype(BF16)
            ck_s[:, _qs(g)] = dkp[g]
            cv_s[:, _qs(g)] = dvp[g]
        db_ref[:, :, :BLK] += dbp
        db_ref[:, :, BLK:] += dbc
        ds_ref[...] += ds

    def cur(j):
        return pl.BlockSpec((BLK, 128), lambda i, j=j: (nb - 1 - i, j))

    def prev(j):
        return pl.BlockSpec((BLK, 128), lambda i, j=j: (jnp.maximum(nb - 2 - i, 0), j))

    bsp = pl.BlockSpec((SW_H, BLK, 2 * BLK), lambda i: (0, 0, 0))
    vec = pl.BlockSpec((1, 128), lambda i: (0, 0))
    return pl.pallas_call(
        body, name=name, grid=(nb,),
        in_specs=[pl.BlockSpec((BLK, SW_W), lambda i: (nb - 1 - i, 1)), pl.BlockSpec((BLK, SW_W), lambda i: (nb - 1 - i, 4)),
                  prev(20), cur(20), prev(21), cur(21), bsp, vec],
        out_specs=[pl.BlockSpec((BLK, SW_GW), lambda i: (nb - 1 - i, 0)), bsp, vec],
        out_shape=[_sds((S, SW_GW), BF16), _sds((SW_H, BLK, 2 * BLK), F32), _sds((1, 128), F32)],
        scratch_shapes=[pltpu.VMEM((BLK, 128), F32)] * 2,
        compiler_params=_params(("arbitrary",)))(dh, proj, proj, proj, proj, proj, bias, sinks)


XA_TM = 512


def _xa_head(qh, kh, vh):
    logits = dot_nt(qh, kh) * (XA_DH ** -0.5)
    mx = lax.stop_gradient(jnp.max(logits, axis=1, keepdims=True))
    e = jnp.exp(logits - mx)
    return dot_nn(e / jnp.sum(e, axis=1, keepdims=True), vh)


def _xs(h, off=0):
    return slice(off + h * XA_DH, off + (h + 1) * XA_DH)


def _xattn_fwd(hb, wq, kv, *, name):
    S = hb.shape[0]
    M = kv.shape[0]

    def body(h_ref, wq_ref, kv_ref, o_ref):
        q = _dg(h_ref[...], wq_ref[...], NN)
        for h in range(XA_H):
            o_ref[:, _xs(h)] = _xa_head(q[:, _xs(h)], kv_ref[:, _xs(h)], kv_ref[:, _xs(h, D)]).astype(BF16)

    tm = min(XA_TM, S)
    row = pl.BlockSpec((tm, D), lambda i: (i, 0))
    return pl.pallas_call(
        body, name=name, grid=(S // tm,),
        in_specs=[row, pl.BlockSpec((D, D), lambda i: (0, 0)), pl.BlockSpec((M, 2 * D), lambda i: (0, 0))], out_specs=row,
        out_shape=_sds((S, D), BF16), compiler_params=_params(("parallel",)))(hb, wq, kv)


def _xattn_bwd(dzb, wo, hb, wq, kv, *, name):
    S = hb.shape[0]
    M = kv.shape[0]

    def body(dz_ref, wo_ref, h_ref, wq_ref, kv_ref, dq_ref, dkv_ref):
        @pl.when(pl.program_id(0) == 0)
        def _():
            dkv_ref[...] = jnp.zeros_like(dkv_ref)

        q = _dg(h_ref[...], wq_ref[...], NN)
        do = _dg(dz_ref[...], wo_ref[...], NT)
        for h in range(XA_H):
            _, vjp = jax.vjp(_xa_head, q[:, _xs(h)], kv_ref[:, _xs(h)], kv_ref[:, _xs(h, D)])
            dq, dk, dv = vjp(do[:, _xs(h)])
            dq_ref[:, _xs(h)] = dq.astype(BF16)
            dkv_ref[:, _xs(h)] += dk
            dkv_ref[:, _xs(h, D)] += dv

    tm = min(XA_TM, S)
    row = pl.BlockSpec((tm, D), lambda i: (i, 0))
    full = pl.BlockSpec((M, 2 * D), lambda i: (0, 0))
    sq = pl.BlockSpec((D, D), lambda i: (0, 0))
    return pl.pallas_call(
        body, name=name, grid=(S // tm,), in_specs=[row, sq, row, sq, full],
        out_specs=[row, full], out_shape=[_sds((S, D), BF16), _sds((M, 2 * D), F32)],
        compiler_params=_params(("arbitrary",)))(dzb, wo, hb, wq, kv)


ANY = pl.BlockSpec(memory_space=pl.ANY)


def _place():
    x, y, c = lax.axis_index("x"), lax.axis_index("y"), lax.axis_index("c")
    chips = [(1 - x, y), (x, 1 - y), (1 - x, 1 - y)]
    return x, y, c, chips


def _gather(arrs, *, name):
    n = len(arrs)

    def body(*refs):
        srcs, outs = refs[:n], refs[n:2 * n]
        send_sems, recv_sems, local_sems = refs[2 * n:]
        x, y, c, chips = _place()
        me, sib = (x, y, c), (x, y, 1 - c)

        def idx(p):
            return 4 * p[0] + 2 * p[1] + p[2]

        def copy(i, k, block, to, from_src=False):
            return pltpu.make_async_remote_copy(
                src_ref=srcs[i] if from_src else outs[i].at[idx(block)], dst_ref=outs[i].at[idx(block)],
                send_sem=send_sems.at[7 * i + k], recv_sem=recv_sems.at[7 * i + k], device_id=to, device_id_type=MESH)

        local = [pltpu.make_async_copy(srcs[i], outs[i].at[idx(me)], local_sems.at[i]) for i in range(n)]
        for cp in local:
            cp.start()
        first = []
        for i in range(n):
            first.append(copy(i, 0, me, sib, True))
            first += [copy(i, 1 + j, me, (*chip, c), True) for j, chip in enumerate(chips)]
        for cp in first:
            cp.start()
        passed = []
        for j, chip in enumerate(chips):
            for i in range(n):
                copy(i, 1 + j, (*chip, c), me).wait_recv()
                cp = copy(i, 4 + j, (*chip, c), sib)
                cp.start()
                passed.append(cp)
        for i in range(n):
            copy(i, 0, sib, me).wait_recv()
        for j, chip in enumerate(chips):
            for i in range(n):
                copy(i, 4 + j, (*chip, 1 - c), me).wait_recv()
        for cp in first + passed:
            cp.wait_send()
        for cp in local:
            cp.wait()

    return pl.pallas_call(
        body, name=name, in_specs=[ANY] * n, out_specs=[ANY] * n,
        out_shape=[_sds((N_DEV,) + a.shape, a.dtype) for a in arrs],
        scratch_shapes=[pltpu.SemaphoreType.DMA((7 * n,)), pltpu.SemaphoreType.DMA((7 * n,)), pltpu.SemaphoreType.DMA((n,))],
        compiler_params=pltpu.CompilerParams(has_side_effects=True))(*arrs)


HBM = pl.BlockSpec(memory_space=pltpu.HBM)
SEM = pl.BlockSpec(memory_space=pltpu.SEMAPHORE)
EFFECT = pltpu.SideEffectType.DATAFLOW_SIDE_EFFECTING


def _near_copies(srcs, lands, send_sems, recv_sems):
    x, y, c, chips = _place()
    me = 4 * x + 2 * y + c
    out = []
    for i in range(len(srcs)):
        for k, (px, py, pc) in enumerate([(x, y, 1 - c)] + [(*chip, c) for chip in chips]):
            out.append(tuple(pltpu.make_async_remote_copy(
                src_ref=srcs[i], dst_ref=lands[i].at[slot], send_sem=send_sems.at[4 * i + k], recv_sem=recv_sems.at[4 * i + k],
                device_id=(px, py, pc), device_id_type=MESH) for slot in (me, 4 * px + 2 * py + pc)))
    return out


def _forward_sibling(lands, *, name):
    n = len(lands)

    def body(*refs):
        bufs = refs[n:2 * n]
        send_sems, recv_sems = refs[2 * n:]
        x, y, c, chips = _place()
        copies = [tuple(pltpu.make_async_remote_copy(
            src_ref=bufs[i].at[4 * chip[0] + 2 * chip[1] + c], dst_ref=bufs[i].at[4 * chip[0] + 2 * chip[1] + cc],
            send_sem=send_sems.at[3 * i + j], recv_sem=recv_sems.at[3 * i + j], device_id=(x, y, 1 - c), device_id_type=MESH)
            for cc in (c, 1 - c)) for i in range(n) for j, chip in enumerate(chips)]
        for send, _ in copies:
            send.start()
        for send, recv in copies:
            send.wait_send()
            recv.wait_recv()

    return pl.pallas_call(
        body, name=name, in_specs=[ANY] * n, out_specs=[ANY] * n, out_shape=[_sds(a.shape, a.dtype) for a in lands],
        input_output_aliases={i: i for i in range(n)},
        scratch_shapes=[pltpu.SemaphoreType.DMA((3 * n,)), pltpu.SemaphoreType.DMA((3 * n,))],
        compiler_params=pltpu.CompilerParams(has_side_effects=True))(*lands)


def _gather_copies(srcs, lands, send_sems, recv_sems):
    return _scatter_copies(srcs, lands, send_sems, recv_sems, whole=True)


def _scatter_copies(srcs, lands, send_sems, recv_sems, whole=False):
    x, y, c, _ = _place()
    me = 4 * x + 2 * y + c
    out = []
    for i in range(len(srcs)):
        for j in range(1, N_DEV):
            px, py, pc = x ^ ((j >> 2) & 1), y ^ ((j >> 1) & 1), c ^ (j & 1)
            p = 4 * px + 2 * py + pc
            k = (N_DEV - 1) * i + j - 1
            out.append(tuple(pltpu.make_async_remote_copy(
                src_ref=srcs[i] if whole else srcs[i].at[s], dst_ref=lands[i].at[d], send_sem=send_sems.at[k],
                recv_sem=recv_sems.at[k], device_id=(px, py, pc), device_id_type=MESH) for s, d in ((p, me), (me, p))))
    return out


def _split_start(srcs, lands, pattern, ncopy, *, after=(), name):
    n = len(srcs)
    na = len(after)

    def body(*refs):
        sems = refs[2 * n + na:]
        for send, _ in pattern(refs[:n], refs[n:2 * n], sems[0], sems[1]):
            send.start()
        refs[-1][...] = jnp.zeros_like(refs[-1])

    arrs = list(srcs) + list(lands)
    return pl.pallas_call(
        body, name=name,
        out_shape=(pltpu.SemaphoreType.DMA((ncopy,)), pltpu.SemaphoreType.DMA((ncopy,)),
                   *[pltpu.HBM(a.shape, a.dtype) for a in arrs], _sds((8, 128), F32)),
        in_specs=[HBM] * (2 * n) + [ANY] * na, out_specs=(SEM, SEM, *[HBM] * (2 * n), pl.BlockSpec(memory_space=pltpu.VMEM)),
        input_output_aliases={i: 2 + i for i in range(2 * n)},
        compiler_params=pltpu.CompilerParams(has_side_effects=EFFECT))(
            *[pltpu.with_memory_space_constraint(a, pltpu.HBM) for a in arrs], *after)


def _split_wait(started, after, pattern, *, name):
    send_sems, recv_sems, *arrs = started[:-1]
    n = len(arrs) // 2

    def body(*refs):
        for send, recv in pattern(refs[:n], refs[n:2 * n], refs[2 * n], refs[2 * n + 1]):
            send.wait_send()
            recv.wait_recv()

    outs = pl.pallas_call(
        body, name=name, out_shape=tuple(pltpu.HBM(a.shape, a.dtype) for a in arrs),
        in_specs=[HBM] * (2 * n) + [SEM, SEM, ANY], out_specs=tuple([HBM] * (2 * n)),
        input_output_aliases={i: i for i in range(2 * n)},
        compiler_params=pltpu.CompilerParams(has_side_effects=EFFECT))(*arrs, send_sems, recv_sems, after)
    return list(outs[n:])


def _adamw(parts, w, m, v, *, layer=None, prev=None, dep=(), name):
    P, R, C = parts.shape
    tr = next((t for t in (512, 256, 176, 128, 64, 32, 16, 8) if R % t == 0 and t * C <= 256 * 1024), R)
    c1 = 1.0 / (1.0 - ADAM_B1 ** ADAM_STEP)
    c2 = 1.0 / (1.0 - ADAM_B2 ** ADAM_STEP)
    nprev = 0 if prev is None else 4

    def body(p_ref, w_ref, m_ref, v_ref, *rest):
        g_ref, d_ref, nm_ref, nv_ref = rest[nprev + len(dep):]
        g = p_ref[0].astype(F32)
        for j in range(1, P):
            g = g + p_ref[j].astype(F32)
        g = g.reshape(w_ref.shape)
        nm = ADAM_B1 * m_ref[...] + (1.0 - ADAM_B1) * g
        nv = ADAM_B2 * v_ref[...] + (1.0 - ADAM_B2) * (g * g)
        g_ref[...] = g
        nm_ref[...] = nm
        nv_ref[...] = nv
        d_ref[...] = -ADAM_LR * ((nm * c1) / (jnp.sqrt(nv * c2) + ADAM_EPS) + ADAM_WD * w_ref[...])

    if layer is None:
        row = pl.BlockSpec((tr, C), lambda i: (i, 0))
    else:
        row = pl.BlockSpec((1, tr, C), lambda i: (layer, i, 0))
    out = _sds(w.shape, F32)
    return pl.pallas_call(
        body, name=name, grid=(R // tr,),
        in_specs=[pl.BlockSpec((P, tr, C), lambda i: (0, i, 0)), row, row, row] + [ANY] * (nprev + len(dep)),
        out_specs=[row, row, row, row], out_shape=[out, out, out, out],
        input_output_aliases={4 + k: k for k in range(nprev)},
        compiler_params=_params(("parallel",)))(parts, w, m, v, *(prev or ()), *dep)


BIG = ("w_in", "w_out", "xa_wq", "xa_wkv", "xa_wo", "ffn_w_up", "ffn_w_down")
COL_SHARDED = ("w_in", "xa_wkv", "ffn_w_up", "ml_conv_w", "ffn_conv_w")
SHARDED_SMALL = ("ml_conv_w", "ffn_conv_w")
REPLICATED = ("rel_bias", "ml_conv_b", "ml_i_bias", "ml_f_bias", "ml_norm_g", "swa_sinks", "ln1_g", "ln1_b", "ln2_g", "ln2_b",
              "ffn_conv_b", "ln3_g", "ln3_b")
NAMES = ("rel_bias", "w_in", "ml_conv_w", "ml_conv_b", "ml_i_bias", "ml_f_bias", "ml_norm_g", "swa_sinks", "w_out", "ln1_g", "ln1_b",
         "xa_wq", "xa_wkv", "xa_wo", "ln2_g", "ln2_b", "ffn_w_up", "ffn_conv_w", "ffn_conv_b", "ffn_w_down", "ln3_g", "ln3_b")


def _flat_rows(a, mult):
    f = a.reshape(-1)
    n = -(-f.shape[0] // (128 * mult)) * (128 * mult)
    if n != f.shape[0]:
        f = jnp.pad(f, (0, n - f.shape[0]))
    return f.reshape(-1, 128)


def _pack(arrs, mult):
    parts = [_flat_rows(a, mult) for a in arrs]
    return jnp.concatenate(parts, axis=0), [p.shape[0] for p in parts]


def _unpack(flat, rows, shapes):
    out, off = [], 0
    lead = flat.shape[:-2]
    for r, shp in zip(rows, shapes):
        n = int(np.prod(shp))
        piece = flat[..., off:off + r, :].reshape(lead + (r * 128,))[..., :n]
        out.append(piece.reshape(lead + tuple(shp)))
        off += r
    return out


def _full_from_shards(stacked, name):
    if name in COL_SHARDED:
        return jnp.moveaxis(stacked, 0, 2).reshape(stacked.shape[1], stacked.shape[2], N_DEV * stacked.shape[3])
    return jnp.moveaxis(stacked, 0, 1).reshape(stacked.shape[1], N_DEV * stacked.shape[2], stacked.shape[3])


def _shards_from_full(full, name):
    L, A, B = full.shape
    if name in COL_SHARDED:
        return jnp.moveaxis(full.reshape(L, A, N_DEV, B // N_DEV), 2, 0)
    return jnp.moveaxis(full.reshape(L, N_DEV, A // N_DEV, B), 1, 0)


def _pad_win(w):
    z = jnp.zeros(w.shape[:-1] + (NP_IN - N_IN,), w.dtype)
    return jnp.concatenate([w[..., :2048], w[..., 2056:], w[..., 2048:2056], z], axis=-1)


def _row128(v):
    return jnp.pad(v, (0, 128 - v.shape[0])).reshape(1, 128)


REST = BIG[1:]


def _layer_full(stacked, n):
    if n in COL_SHARDED:
        full = jnp.moveaxis(stacked, 0, 1).reshape(stacked.shape[1], N_DEV * stacked.shape[2])
    else:
        full = stacked.reshape(N_DEV * stacked.shape[1], stacked.shape[2])
    return _pad_win(full) if n == "w_in" else full


def _layer_shards(g, n):
    A, B = g.shape
    if n in COL_SHARDED:
        return jnp.moveaxis(g.reshape(A, N_DEV, B // N_DEV), 1, 0).astype(BF16)
    return g.reshape(N_DEV, A // N_DEV, B).astype(BF16)


def _with_own_block(a, idx, nblk):
    return lax.dynamic_update_slice(lax.empty((nblk,) + a.shape, a.dtype), a[None], (idx,) + (0,) * a.ndim)


def _gather_start(arrs, me, *, after=(), name):
    return _split_start(arrs, [_with_own_block(a, me, N_DEV) for a in arrs], _near_copies, 4 * len(arrs), after=after, name=name)


def _gather_finish(started, after, *, name):
    return _forward_sibling(_split_wait(started, after, _near_copies, name=name + "_wait"), name=name + "_forward")


def _reduce_start(tag, names, grads, me):
    send = [_layer_shards(g, n) for n, g in zip(names, grads)]
    lands = [_with_own_block(lax.dynamic_index_in_dim(s, me, 0, keepdims=False), me, N_DEV) for s in send]
    return _split_start(send, lands, _scatter_copies, (N_DEV - 1) * len(send), name=f"reduce_{tag}_start")


def _small_start(Gf, loss_part, me):
    sp_flat, sp_rows = _pack([Gf[n] for n in REPLICATED + SHARDED_SMALL] + [loss_part], 8)
    return _split_start([sp_flat], [_with_own_block(sp_flat, me, N_DEV)], _gather_copies, N_DEV - 1, name="gather_small_start"), sp_rows


def _small_finish(started, sp_rows, after, Gf, W, Mo, Vo, me):
    res = {}
    small = REPLICATED + SHARDED_SMALL
    sp_all = _split_wait(started, after, _gather_copies, name="gather_small_wait")[0]

    def widen(n, t):
        if n not in SHARDED_SMALL:
            return t[n]
        return lax.dynamic_update_slice(jnp.zeros(Gf[n].shape, F32), t[n], (0, 0, me * t[n].shape[2]))

    zl = jnp.zeros((8, 128), F32)
    wsm, _ = _pack([widen(n, W) for n in small] + [zl], 8)
    msm, _ = _pack([widen(n, Mo) for n in small] + [zl], 8)
    vsm, _ = _pack([widen(n, Vo) for n in small] + [zl], 8)
    outs_small = [_unpack(o_, sp_rows, [Gf[n].shape for n in small] + [(8, 128)])
                  for o_ in _adamw(sp_all, wsm, msm, vsm, name="adamw_small")]
    for kind, os_ in zip(("g", "d", "m", "v"), outs_small):
        for n, a in zip(small, os_[:-1]):
            if n in SHARDED_SMALL:
                a = lax.dynamic_slice(a, (0, 0, me * W[n].shape[2]), W[n].shape)
            res[kind, n] = a
    return res, outs_small[0][-1][0, 0]


def kernel(x, mem, rel_bias, w_in, ml_conv_w, ml_conv_b, ml_i_bias, ml_f_bias, ml_norm_g, swa_sinks, w_out, ln1_g, ln1_b, xa_wq, xa_wkv, xa_wo, ln2_g, ln2_b, ffn_w_up, ffn_conv_w, ffn_conv_b, ffn_w_down, ln3_g, ln3_b, loss_target, m_rel_bias, m_w_in, m_ml_conv_w, m_ml_conv_b, m_ml_i_bias, m_ml_f_bias, m_ml_norm_g, m_swa_sinks, m_w_out, m_ln1_g, m_ln1_b, m_xa_wq, m_xa_wkv, m_xa_wo, m_ln2_g, m_ln2_b, m_ffn_w_up, m_ffn_conv_w, m_ffn_conv_b, m_ffn_w_down, m_ln3_g, m_ln3_b, v_rel_bias, v_w_in, v_ml_conv_w, v_ml_conv_b, v_ml_i_bias, v_ml_f_bias, v_ml_norm_g, v_swa_sinks, v_w_out, v_ln1_g, v_ln1_b, v_xa_wq, v_xa_wkv, v_xa_wo, v_ln2_g, v_ln2_b, v_ffn_w_up, v_ffn_conv_w, v_ffn_conv_b, v_ffn_w_down, v_ln3_g, v_ln3_b):
    W = dict(rel_bias=rel_bias, w_in=w_in, ml_conv_w=ml_conv_w, ml_conv_b=ml_conv_b, ml_i_bias=ml_i_bias, ml_f_bias=ml_f_bias,
             ml_norm_g=ml_norm_g, swa_sinks=swa_sinks, w_out=w_out, ln1_g=ln1_g, ln1_b=ln1_b, xa_wq=xa_wq, xa_wkv=xa_wkv,
             xa_wo=xa_wo, ln2_g=ln2_g, ln2_b=ln2_b, ffn_w_up=ffn_w_up, ffn_conv_w=ffn_conv_w, ffn_conv_b=ffn_conv_b,
             ffn_w_down=ffn_w_down, ln3_g=ln3_g, ln3_b=ln3_b)
    Mo = dict(rel_bias=m_rel_bias, w_in=m_w_in, ml_conv_w=m_ml_conv_w, ml_conv_b=m_ml_conv_b, ml_i_bias=m_ml_i_bias,
              ml_f_bias=m_ml_f_bias, ml_norm_g=m_ml_norm_g, swa_sinks=m_swa_sinks, w_out=m_w_out, ln1_g=m_ln1_g, ln1_b=m_ln1_b,
              xa_wq=m_xa_wq, xa_wkv=m_xa_wkv, xa_wo=m_xa_wo, ln2_g=m_ln2_g, ln2_b=m_ln2_b, ffn_w_up=m_ffn_w_up,
              ffn_conv_w=m_ffn_conv_w, ffn_conv_b=m_ffn_conv_b, ffn_w_down=m_ffn_w_down, ln3_g=m_ln3_g, ln3_b=m_ln3_b)
    Vo = dict(rel_bias=v_rel_bias, w_in=v_w_in, ml_conv_w=v_ml_conv_w, ml_conv_b=v_ml_conv_b, ml_i_bias=v_ml_i_bias,
              ml_f_bias=v_ml_f_bias, ml_norm_g=v_ml_norm_g, swa_sinks=v_swa_sinks, w_out=v_w_out, ln1_g=v_ln1_g, ln1_b=v_ln1_b,
              xa_wq=v_xa_wq, xa_wkv=v_xa_wkv, xa_wo=v_xa_wo, ln2_g=v_ln2_g, ln2_b=v_ln2_b, ffn_w_up=v_ffn_w_up,
              ffn_conv_w=v_ffn_conv_w, ffn_conv_b=v_ffn_conv_b, ffn_w_down=v_ffn_w_down, ln3_g=v_ln3_g, ln3_b=v_ln3_b)
    S = x.shape[1]
    c_me = lax.axis_index("c")
    me = 4 * lax.axis_index("x") + 2 * lax.axis_index("y") + c_me
    xs = x.reshape(S, D)
    mems = mem.reshape(mem.shape[1], D)
    tgt = loss_target.reshape(S, D)


    sm_flat, sm_rows = _pack([W[n] for n in SHARDED_SMALL], 8)
    first = _gather([w_in[0].astype(BF16), sm_flat], name="gather_first")
    rest0 = _gather_start([W[n][0].astype(BF16) for n in REST], me, name="gather_rest0_start")
    full = [{"w_in": _layer_full(first[0], "w_in")}, None]
    conv_w = {n: _full_from_shards(s, n) for n, s in zip(SHARDED_SMALL, _unpack(first[1], sm_rows, [W[n].shape for n in SHARDED_SMALL]))}

    bucket = jnp.asarray(_t5_buckets())
    bias = _bias_table(rel_bias, bucket, name="bias_table")

    saved = []
    h0, h0b = xs, xs.astype(BF16)
    h0t = h0b.T
    for l in range(DEPTH):
        gbias = _row128(jnp.concatenate([ml_i_bias[l], ml_f_bias[l]]))
        sinks = _row128(swa_sinks[l])
        ng = ml_norm_g[l].reshape(1, ML_W)
        proj = _mm(h0b, full[l]["w_in"], dep=(rest0[-1],) if l == 0 else (), name=f"proj{l}")
        qk = _silu_conv_fwd(proj, conv_w["ml_conv_w"][l], ml_conv_b[l].reshape(1, -1), name=f"mlconv{l}")
        h_ml, cs, ns, ms = _mlstm_fwd(qk, proj, gbias, ng, name=f"mlstm{l}")
        h_sw = _swa_fwd(proj, bias, sinks, name=f"swa{l}")
        dep = ()
        if l == 0:
            landed = _gather_finish(rest0, h_sw, name="gather_rest0")
            full[0].update({n: _layer_full(s, n) for n, s in zip(REST, landed)})
            layer1 = _gather_start([W[n][1].astype(BF16) for n in BIG], me, after=(landed[0],), name="gather_layer1_start")
            dep = (layer1[-1],)
        fw = full[l]
        h1, h1b, h1t, z1 = _mm_res_ln([h_ml, h_sw], fw["w_out"], h0, ln1_g[l].reshape(1, D), ln1_b[l].reshape(1, D),
                                      name=f"mix_out{l}")
        kv = _mm(mems, fw["xa_wkv"], tm=256, dep=dep, name=f"xa_kv{l}")
        o = _xattn_fwd(h1b, fw["xa_wq"], kv, name=f"xattn{l}")
        h2, h2b, h2t, z2 = _mm_res_ln([o], fw["xa_wo"], h1, ln2_g[l].reshape(1, D), ln2_b[l].reshape(1, D), name=f"xa_out{l}")
        act, act_t = _ffn_gate_fwd(h2b, fw["ffn_w_up"], conv_w["ffn_conv_w"][l], ffn_conv_b[l].reshape(1, -1), name=f"ffn_gate{l}")
        h3, h3b, h3t, z3 = _mm_res_ln([act], fw["ffn_w_down"], h2, ln3_g[l].reshape(1, D), ln3_b[l].reshape(1, D),
                                      name=f"ffn_out{l}")
        saved.append(dict(h0t=h0t, proj=proj, qk=qk, cs=cs, ns=ns, ms=ms, h_ml=h_ml, h_sw=h_sw, z1=z1, h1b=h1b, h1t=h1t, kv=kv, o=o,
                          z2=z2, h2b=h2b, h2t=h2t, act_t=act_t, z3=z3, gbias=gbias, sinks=sinks, ng=ng))
        h0, h0b, h0t = h3, h3b, h3t
        if l == 0:
            landed = _gather_finish(layer1, h3b, name="gather_layer1")
            full[1] = {n: _layer_full(s, n) for n, s in zip(BIG, landed)}

    G = {n: [None] * DEPTH for n in NAMES if n != "rel_bias"}
    dbias = [None] * DEPTH
    pending = []
    dz3, dz3b, G["ln3_g"][DEPTH - 1], G["ln3_b"][DEPTH - 1], loss_part = _grad_in(
        [], None, ln=(saved[-1]["z3"], ln3_g[DEPTH - 1].reshape(1, D)), loss=(h0, tgt), name="loss_head")
    for l in reversed(range(DEPTH)):
        sv, fw = saved[l], full[l]
        win = fw["w_in"]
        G["ffn_w_down"][l] = _wgrad(sv["act_t"], dz3b, name=f"d_w_down{l}")
        dupg, dupv, G["ffn_conv_w"][l], G["ffn_conv_b"][l] = _ffn_gate_bwd(
            dz3b, fw["ffn_w_down"], sv["h2b"], fw["ffn_w_up"], conv_w["ffn_conv_w"][l], ffn_conv_b[l].reshape(1, -1),
            name=f"ffn_gate_bwd{l}")
        G["ffn_w_up"][l] = jnp.concatenate([_wgrad(sv["h2t"], dupg, name=f"d_w_up_g{l}"),
                                            _wgrad(sv["h2t"], dupv, name=f"d_w_up_v{l}")], axis=1)
        dep = ()
        if l == 0:
            names = ("ffn_w_up", "ffn_w_down")
            pending.append((names, 0, "ffn0", _reduce_start("ffn0", names, [G[n][0] for n in names], me)))
            dep = (pending[-1][3][-1],)
        dz2, dz2b, G["ln2_g"][l], G["ln2_b"][l] = _grad_in(
            [(dupg, fw["ffn_w_up"], 0), (dupv, fw["ffn_w_up"], 1)], dz3, ln=(sv["z2"], ln2_g[l].reshape(1, D)), dep=dep,
            name=f"d_h2_{l}")
        G["xa_wo"][l] = _mm_tn(sv["o"], dz2b, name=f"d_xa_wo{l}")
        dq, dkv = _xattn_bwd(dz2b, fw["xa_wo"], sv["h1b"], fw["xa_wq"], sv["kv"], name=f"xattn_bwd{l}")
        G["xa_wkv"][l] = _mm_tn(mems, dkv, name=f"d_xa_wkv{l}")
        G["xa_wq"][l] = _wgrad(sv["h1t"], dq, name=f"d_xa_wq{l}")
        dep = ()
        if l == 0:
            names = ("xa_wq", "xa_wkv", "xa_wo")
            pending.append((names, 0, "xa0", _reduce_start("xa0", names, [G[n][0] for n in names], me)))
            dep = (pending[-1][3][-1],)
        dz1, dz1b, G["ln1_g"][l], G["ln1_b"][l] = _grad_in(
            [(dq, fw["xa_wq"], 0)], dz2, ln=(sv["z1"], ln1_g[l].reshape(1, D)), dep=dep, name=f"d_h1_{l}")
        G["w_out"][l] = jnp.concatenate([_mm_tn(sv["h_ml"], dz1b, name=f"d_w_out_ml{l}"),
                                         _mm_tn(sv["h_sw"], dz1b, name=f"d_w_out_sw{l}")], axis=0)
        dep = ()
        if l == 0:
            pending.append((("w_out",), 0, "out0", _reduce_start("out0", ("w_out",), [G["w_out"][0]], me)))
            dep = (pending[-1][3][-1],)
        dhcat = _mm(dz1b, fw["w_out"], trans_b=True, dep=dep, name=f"d_hcat{l}")
        dsw, dbias[l], dsinks = _swa_bwd(dhcat, sv["proj"], bias, sv["sinks"], name=f"swa_bwd{l}")
        dqk, dml, dgb, dng = _mlstm_bwd(dhcat, sv["qk"], sv["proj"], sv["gbias"], sv["ng"], sv["cs"], sv["ns"], sv["ms"],
                                        name=f"mlstm_bwd{l}")
        dqk_pre, G["ml_conv_w"][l], G["ml_conv_b"][l] = _silu_conv_bwd(
            dqk, sv["proj"], conv_w["ml_conv_w"][l], ml_conv_b[l].reshape(1, -1), name=f"mlconv_bwd{l}")
        dw_qk = _wgrad(sv["h0t"], dqk_pre, name=f"d_w_in_qk{l}")
        dw_ml = _wgrad(sv["h0t"], dml, name=f"d_w_in_ml{l}")
        dw_sw = _wgrad(sv["h0t"], dsw, name=f"d_w_in_sw{l}")
        G["w_in"][l] = jnp.concatenate([dw_qk, dw_ml[:, :2 * ML_W + 2 * ML_H], dw_sw], axis=1)
        win_ml = jnp.concatenate([win[:, 1024:2048], win[:, 2816:2944]], axis=1)
        pairs = [(dqk_pre, win, 0), (dml, win_ml, 0), (dsw, win[:, 2048:2816], 0)]
        G["ml_i_bias"][l] = dgb[0, :ML_H]
        G["ml_f_bias"][l] = dgb[0, ML_H:2 * ML_H]
        G["ml_norm_g"][l] = dng
        G["swa_sinks"][l] = dsinks[0, :SW_H]
        if l > 0:
            pending.append((BIG, l, f"l{l}", _reduce_start(f"l{l}", BIG, [G[n][l] for n in BIG], me)))
            dz3, dz3b, G["ln3_g"][l - 1], G["ln3_b"][l - 1] = _grad_in(
                pairs, dz1, ln=(saved[l - 1]["z3"], ln3_g[l - 1].reshape(1, D)), dep=(pending[-1][3][-1],), name=f"d_h0_{l}")
        else:
            pending.append((("w_in",), 0, "in0", _reduce_start("in0", ("w_in",), [G["w_in"][0]], me)))
            dh = _grad_in(pairs, dz1, dep=(pending[-1][3][-1],), name="d_h0_0")
            grad_x = dh.reshape(x.shape)

    Gf = {n: jnp.stack([g.reshape(W[n].shape[1:]) if n in REPLICATED else g for g in G[n]]) for n in G if n not in BIG}
    Gf["rel_bias"] = _bias_table_bwd(dbias, bucket, name="bias_table_bwd")[:, :SW_H]
    small_started, sp_rows = _small_start(Gf, loss_part, me)

    parts = {}
    for names, l, tag, started in pending:
        landed = _split_wait(started, dh, _scatter_copies, name=f"reduce_{tag}_wait")
        parts.update({(n, l): p for n, p in zip(names, landed)})
    res = {}
    dep = (small_started[-1],)
    for n in BIG:
        outs = None
        for l in reversed(range(DEPTH)):
            outs = _adamw(parts[n, l], W[n], Mo[n], Vo[n], layer=l, prev=outs, dep=dep, name=f"adamw_{n}{l}")
            dep = ()
        for kind, a in zip(("g", "d", "m", "v"), outs):
            res[kind, n] = a
    res_small, loss = _small_finish(small_started, sp_rows, res["g", BIG[-1]], Gf, W, Mo, Vo, me)
    res.update(res_small)
    return (loss, grad_x, *[res["g", n] for n in NAMES], *[res["d", n] for n in NAMES], *[res["m", n] for n in NAMES],
            *[res["v", n] for n in NAMES])
```

```python
import functools
import math

import jax
import jax.numpy as jnp
import numpy as np
from jax import lax
from jax.experimental import pallas as pl
from jax.experimental.pallas import tpu as pltpu

F32 = jnp.float32
BF16 = jnp.bfloat16

N_DEV = 8
D = 1024
DEPTH = 2
ML_H = 4
ML_W = 512
ML_DH = 128
ML_L = 64
ML_CONV = 4
SW_DH = 64
SW_W = 512
SW_H = 8
SW_G = 4
SW_KVW = 128
BLK = 128
REL_B = 32
REL_MAXD = 128
XA_H = 4
XA_DH = 256
DFF = 2816
NB_FF = DFF // 128
FFN_CONV = 3
ALPHA = (2.0 * DEPTH) ** 0.25
EPS = 1e-5
N_IN = 2824
NP_IN = 3072
ML_GW = 2 * ML_W + 128
SW_GW = SW_W + 2 * SW_KVW
ADAM_LR = 0.001
ADAM_B1 = 0.9
ADAM_B2 = 0.999
ADAM_EPS = 1e-08
ADAM_WD = 0.01
ADAM_STEP = 10
VMEM_LIMIT = 56 * 1024 * 1024
MESH = pl.DeviceIdType.MESH

NN = ((1,), (0,))
NT = ((1,), (1,))
TN = ((0,), (0,))


def _dg(a, b, dn):
    if a.ndim == 3:
        dims = (((dn[0][0] + 1,), (dn[1][0] + 1,)), ((0,), (0,)))
    else:
        dims = (dn, ((), ()))
    return lax.dot_general(a.astype(BF16), b.astype(BF16), dims, preferred_element_type=F32)


@jax.custom_vjp
def dot_nn(a, b):
    return _dg(a, b, NN)


dot_nn.defvjp(lambda a, b: (_dg(a, b, NN), (a, b)), lambda r, g: (_dg(g, r[1], NT), _dg(r[0], g, TN)))


@jax.custom_vjp
def dot_nt(a, b):
    return _dg(a, b, NT)


dot_nt.defvjp(lambda a, b: (_dg(a, b, NT), (a, b)), lambda r, g: (_dg(g, r[1], NN), _dg(g, r[0], TN)))


@jax.custom_vjp
def dot_tn(a, b):
    return _dg(a, b, TN)


dot_tn.defvjp(lambda a, b: (_dg(a, b, TN), (a, b)), lambda r, g: (_dg(r[1], g, NT), _dg(r[0], g, NN)))


def _params(sem=None):
    return pltpu.CompilerParams(dimension_semantics=sem, vmem_limit_bytes=VMEM_LIMIT)


def _sds(shape, dtype):
    return jax.ShapeDtypeStruct(tuple(shape), dtype)


TOKEN = pl.BlockSpec((8, 128), lambda *_: (0, 0))


def _mm(a, b, *, trans_b=False, out_dtype=F32, add=None, add_scale=1.0, tm=1024, tn=512, dep=(), name):
    a_list = list(a) if isinstance(a, (list, tuple)) else [a]
    M = a_list[0].shape[0]
    N = b.shape[0] if trans_b else b.shape[1]
    tm = min(tm, M)
    tn = next(t for t in (tn, 384, 256, 128) if N % t == 0)
    assert M % tm == 0
    Ka = a_list[0].shape[1]
    assert all(t.shape[1] == Ka for t in a_list)
    tk = next(t for t in (Ka, 1408, 1024) if Ka % t == 0 and t <= 1408)
    na, npa = len(a_list), Ka // tk
    nk = na * npa
    has_add = add is not None

    def body(*refs):
        a_refs, b_ref = refs[:na], refs[na]
        add_ref = refs[na + 1] if has_add else None
        o_ref, acc_ref = refs[-2], refs[-1]
        k = pl.program_id(2)

        def finish(r):
            if has_add:
                r = r + add_scale * add_ref[...].astype(F32)
            o_ref[...] = r.astype(out_dtype)

        for t, a_ref in enumerate(a_refs):
            def step(a_ref=a_ref):
                p = _dg(a_ref[...], b_ref[...], NT if trans_b else NN)
                if nk == 1:
                    finish(p)
                    return

                @pl.when(k == 0)
                def _():
                    acc_ref[...] = p

                @pl.when((k > 0) & (k < nk - 1))
                def _():
                    acc_ref[...] += p

                @pl.when(k == nk - 1)
                def _():
                    finish(acc_ref[...] + p)

            if na == 1:
                step()
            else:
                pl.when((k >= t * npa) & (k < (t + 1) * npa))(step)

    in_specs = [pl.BlockSpec((tm, tk), lambda i, j, k, t=t: (i, jnp.clip(k - t * npa, 0, npa - 1))) for t in range(na)]
    in_specs.append(pl.BlockSpec((tn, tk), lambda i, j, k: (j, k)) if trans_b else pl.BlockSpec((tk, tn), lambda i, j, k: (k, j)))
    args = a_list + [b]
    if has_add:
        in_specs.append(pl.BlockSpec((tm, tn), lambda i, j, k: (i, j)))
        args.append(add)
    in_specs += [TOKEN] * len(dep)
    args += list(dep)
    return pl.pallas_call(
        body, name=name, grid=(M // tm, N // tn, nk), in_specs=in_specs,
        out_specs=pl.BlockSpec((tm, tn), lambda i, j, k: (i, j)), out_shape=_sds((M, N), out_dtype),
        scratch_shapes=[pltpu.VMEM((tm, tn) if nk > 1 else (8, 128), F32)],
        compiler_params=_params(("parallel", "parallel", "arbitrary")))(*args)


def _wgrad(a_t, g, *, name):
    K, N = a_t.shape[0], g.shape[1]
    return _mm(a_t, g, tm=K if K <= 1024 else K // 2, tn=next(t for t in range(1536, 0, -128) if N % t == 0), name=name)


def _mm_tn(a, g, *, name):
    S, K = a.shape
    N = g.shape[1]
    tk = K if K <= 1024 else K // 2
    tn = next(t for t in range(1536, 0, -128) if N % t == 0)
    ts = min(512, S)
    ns = S // ts
    assert K % tk == 0 and S % ts == 0

    def body(a_ref, g_ref, o_ref):
        s = pl.program_id(2)
        p = _dg(a_ref[...], g_ref[...], TN)

        @pl.when(s == 0)
        def _():
            o_ref[...] = p

        @pl.when(s > 0)
        def _():
            o_ref[...] += p

    return pl.pallas_call(
        body, name=name, grid=(K // tk, N // tn, ns),
        in_specs=[pl.BlockSpec((ts, tk), lambda i, j, s: (s, i)), pl.BlockSpec((ts, tn), lambda i, j, s: (s, j))],
        out_specs=pl.BlockSpec((tk, tn), lambda i, j, s: (i, j)), out_shape=_sds((K, N), F32),
        compiler_params=_params(("parallel", "parallel", "arbitrary")))(a, g)


def _mm_res_ln(a_list, w, resid, gam, bet, *, name):
    M, Ka = a_list[0].shape
    na = len(a_list)
    assert w.shape[0] == na * Ka
    tm = min(512, M)

    def body(*refs):
        a_refs, w_refs = refs[:na], refs[na:2 * na]
        r_ref, g_ref, b_ref, y_ref, yb_ref, yt_ref, z_ref = refs[2 * na:]
        z = ALPHA * r_ref[...]
        for a_ref, w_ref in zip(a_refs, w_refs):
            z = z + _dg(a_ref[...], w_ref[...], NN)
        mu = jnp.mean(z, axis=1, keepdims=True)
        zc = z - mu
        var = jnp.mean(zc * zc, axis=1, keepdims=True)
        y = zc * lax.rsqrt(var + EPS) * g_ref[...] + b_ref[...]
        y_ref[...] = y
        yb_ref[...] = y.astype(BF16)
        yt_ref[...] = y.T.astype(BF16)
        z_ref[...] = z

    row = pl.BlockSpec((tm, D), lambda i: (i, 0))
    vec = pl.BlockSpec((1, D), lambda i: (0, 0))
    a_specs = [pl.BlockSpec((tm, Ka), lambda i: (i, 0)) for _ in a_list]
    w_specs = [pl.BlockSpec((Ka, D), lambda i, t=t: (t, 0)) for t in range(na)]
    return pl.pallas_call(
        body, name=name, grid=(M // tm,), in_specs=a_specs + w_specs + [row, vec, vec],
        out_specs=[row, row, pl.BlockSpec((D, tm), lambda i: (0, i)), row],
        out_shape=[_sds((M, D), F32), _sds((M, D), BF16), _sds((D, M), BF16), _sds((M, D), F32)],
        compiler_params=_params(("parallel",)))(*a_list, *([w] * na), resid, gam, bet)


def _grad_in(pairs, add, *, ln=None, loss=None, dep=(), name):
    M = (add if add is not None else loss[0]).shape[0]
    tm = min(256, M)
    npair, nd = len(pairs), len(dep)
    has_ln, has_loss = ln is not None, loss is not None

    def body(*refs):
        n_in = 2 * npair + (2 if has_loss else 1) + 2 * has_ln + nd
        ins, outs = refs[:n_in], refs[n_in:]
        i = pl.program_id(0)
        pos = 2 * npair
        if has_loss:
            e = ins[pos][...] - ins[pos + 1][...]
            pos += 2
            dy = e * (1.0 / D)
            part = 0.5 * jnp.sum(jnp.sum(e * e, axis=1, keepdims=True) * (1.0 / D), axis=0, keepdims=True)
        else:
            dy = ALPHA * ins[pos][...]
            pos += 1
            for t in range(npair):
                dy = dy + _dg(ins[2 * t][...], ins[2 * t + 1][...], NT)
        if not has_ln:
            outs[0][...] = dy
            return
        z, g_ref = ins[pos][...], ins[pos + 1]
        mu = jnp.mean(z, axis=1, keepdims=True)
        zc = z - mu
        var = jnp.mean(zc * zc, axis=1, keepdims=True)
        rstd = lax.rsqrt(var + EPS)
        xh = zc * rstd
        dxh = dy * g_ref[...]
        m1 = jnp.mean(dxh, axis=1, keepdims=True)
        m2 = jnp.mean(dxh * xh, axis=1, keepdims=True)
        dz = rstd * (dxh - m1 - xh * m2)
        outs[0][...] = dz
        outs[1][...] = dz.astype(BF16)
        acc = [(outs[2], jnp.sum(dy * xh, axis=0, keepdims=True)), (outs[3], jnp.sum(dy, axis=0, keepdims=True))]
        if has_loss:
            acc.append((outs[4], jnp.broadcast_to(part, (8, 128))))

        @pl.when(i == 0)
        def _():
            for ref, val in acc:
                ref[...] = val

        @pl.when(i > 0)
        def _():
            for ref, val in acc:
                ref[...] += val

    row = pl.BlockSpec((tm, D), lambda i: (i, 0))
    vec = pl.BlockSpec((1, D), lambda i: (0, 0))
    in_specs, args = [], []
    for a, b, blk in pairs:
        in_specs += [pl.BlockSpec((tm, a.shape[1]), lambda i: (i, 0)), pl.BlockSpec((D, a.shape[1]), lambda i, blk=blk: (0, blk))]
        args += [a, b]
    if has_loss:
        in_specs += [row, row]
        args += list(loss)
    else:
        in_specs.append(row)
        args.append(add)
    if has_ln:
        in_specs += [row, vec]
        args += list(ln)
    in_specs += [TOKEN] * nd
    args += list(dep)
    if has_ln:
        out_specs = [row, row, vec, vec] + ([pl.BlockSpec((8, 128), lambda i: (0, 0))] if has_loss else [])
        out_shape = [_sds((M, D), F32), _sds((M, D), BF16), _sds((1, D), F32), _sds((1, D), F32)] + ([_sds((8, 128), F32)] if has_loss else [])
    else:
        out_specs, out_shape = row, _sds((M, D), F32)
    return pl.pallas_call(
        body, name=name, grid=(M // tm,), in_specs=in_specs, out_specs=out_specs, out_shape=out_shape,
        compiler_params=_params(("arbitrary",) if has_ln else ("parallel",)))(*args)


def _shift_down(x, d):
    if d == 0:
        return x
    rows = lax.broadcasted_iota(jnp.int32, x.shape, 0)
    return jnp.where(rows >= d, pltpu.roll(x, d, 0), 0.0)


def _shift_up(x, d):
    if d == 0:
        return x
    S = x.shape[0]
    rows = lax.broadcasted_iota(jnp.int32, x.shape, 0)
    return jnp.where(rows < S - d, pltpu.roll(x, S - d, 0), 0.0)


def _conv(x, w_ref, b_ref, cs, K):
    y = b_ref[:, cs]
    for j in range(K):
        y = y + _shift_down(x, K - 1 - j) * w_ref[j:j + 1, cs]
    return y


def _conv_bwd(dy, x, w_ref, dw_ref, db_ref, cs, K):
    dx = jnp.zeros_like(x)
    for j in range(K):
        sdy = _shift_up(dy, K - 1 - j)
        dx = dx + sdy * w_ref[j:j + 1, cs]
        dw_ref[j:j + 1, cs] = jnp.sum(sdy * x, axis=0, keepdims=True)
    db_ref[:, cs] = jnp.sum(dy, axis=0, keepdims=True)
    return dx


ALL = slice(None)


def _silu_conv_fwd(proj, cw, cb, *, name):
    S = proj.shape[0]

    def body(x_ref, w_ref, b_ref, o_ref):
        o_ref[...] = jax.nn.silu(_conv(x_ref[...], w_ref, b_ref, ALL, ML_CONV))

    col = pl.BlockSpec((S, 128), lambda j: (0, j))
    return pl.pallas_call(
        body, name=name, grid=(8,),
        in_specs=[col, pl.BlockSpec((ML_CONV, 128), lambda j: (0, j)), pl.BlockSpec((1, 128), lambda j: (0, j))],
        out_specs=col, out_shape=_sds((S, 2 * ML_W), F32), compiler_params=_params(("parallel",)))(proj, cw, cb)


def _silu_conv_bwd(dqk, proj, cw, cb, *, name):
    S = proj.shape[0]

    def body(d_ref, x_ref, w_ref, b_ref, dx_ref, dw_ref, db_ref):
        x = x_ref[...]
        y = _conv(x, w_ref, b_ref, ALL, ML_CONV)
        dy = jax.vjp(jax.nn.silu, y)[1](d_ref[...])[0]
        dx_ref[...] = _conv_bwd(dy, x, w_ref, dw_ref, db_ref, ALL, ML_CONV).astype(BF16)

    col = pl.BlockSpec((S, 128), lambda j: (0, j))
    wsp = pl.BlockSpec((ML_CONV, 128), lambda j: (0, j))
    bsp = pl.BlockSpec((1, 128), lambda j: (0, j))
    return pl.pallas_call(
        body, name=name, grid=(8,), in_specs=[col, col, wsp, bsp], out_specs=[col, wsp, bsp],
        out_shape=[_sds((S, 2 * ML_W), BF16), _sds((ML_CONV, 2 * ML_W), F32), _sds((1, 2 * ML_W), F32)],
        compiler_params=_params(("parallel",)))(dqk, proj, cw, cb)


GELU_C0 = math.sqrt(2.0 / math.pi)
GELU_C1 = 0.044715


def _gate_bwd(ug, uv, da):
    t = jnp.tanh(GELU_C0 * (ug + GELU_C1 * (ug * ug * ug)))
    half = 0.5 * (1.0 + t)
    dgelu = half + 0.5 * ug * (1.0 - t * t) * (GELU_C0 * (1.0 + 3.0 * GELU_C1 * (ug * ug)))
    return da * uv * dgelu, da * (ug * half)


def _up_pair(h_ref, ugw_ref, uvw_ref):
    return _dg(h_ref[...], jnp.concatenate([ugw_ref[...], uvw_ref[...]], axis=1), NN)


def _ffn_specs(S):
    return (pl.BlockSpec((D, 128), lambda j: (0, j)), pl.BlockSpec((D, 128), lambda j: (0, j + NB_FF)),
            pl.BlockSpec((FFN_CONV, 128), lambda j: (0, j)), pl.BlockSpec((FFN_CONV, 128), lambda j: (0, j + NB_FF)),
            pl.BlockSpec((1, 128), lambda j: (0, j)), pl.BlockSpec((1, 128), lambda j: (0, j + NB_FF)))


def _ffn_gate_fwd(hb, w_up, cw, cb, *, name):
    S = hb.shape[0]
    ug_, uv_, wg, wv, bg, bv = _ffn_specs(S)

    def body(h_ref, ugw_ref, uvw_ref, wg_ref, wv_ref, bg_ref, bv_ref, o_ref, ot_ref, h_s):
        @pl.when(pl.program_id(0) == 0)
        def _():
            pltpu.sync_copy(h_ref, h_s)

        x2 = _up_pair(h_s, ugw_ref, uvw_ref)
        ug = _conv(x2[:, :128], wg_ref, bg_ref, ALL, FFN_CONV)
        uv = _conv(x2[:, 128:], wv_ref, bv_ref, ALL, FFN_CONV)
        act = jax.nn.gelu(ug) * uv
        o_ref[...] = act.astype(BF16)
        ot_ref[...] = act.T.astype(BF16)

    return pl.pallas_call(
        body, name=name, grid=(NB_FF,), in_specs=[ANY, ug_, uv_, wg, wv, bg, bv],
        out_specs=[pl.BlockSpec((S, 128), lambda j: (0, j)), pl.BlockSpec((128, S), lambda j: (j, 0))],
        out_shape=[_sds((S, DFF), BF16), _sds((DFF, S), BF16)],
        scratch_shapes=[pltpu.VMEM((S, D), BF16)],
        compiler_params=_params(("arbitrary",)))(hb, w_up, w_up, cw, cw, cb, cb)


def _ffn_gate_bwd(dzb, w_down, hb, w_up, cw, cb, *, name):
    S = hb.shape[0]
    ug_, uv_, wg, wv, bg, bv = _ffn_specs(S)

    def body(dz_ref, h_ref, wd_ref, ugw_ref, uvw_ref, wg_ref, wv_ref, bg_ref, bv_ref,
             dxg_ref, dxv_ref, dwg_ref, dwv_ref, dbg_ref, dbv_ref, dz_s, h_s):
        @pl.when(pl.program_id(0) == 0)
        def _():
            pltpu.sync_copy(dz_ref, dz_s)
            pltpu.sync_copy(h_ref, h_s)

        x2 = _up_pair(h_s, ugw_ref, uvw_ref)
        xg, xv = x2[:, :128], x2[:, 128:]
        ug = _conv(xg, wg_ref, bg_ref, ALL, FFN_CONV)
        uv = _conv(xv, wv_ref, bv_ref, ALL, FFN_CONV)
        dug, duv = _gate_bwd(ug, uv, _dg(dz_s[...], wd_ref[...], NT))
        dxg_ref[...] = _conv_bwd(dug, xg, wg_ref, dwg_ref, dbg_ref, ALL, FFN_CONV).astype(BF16)
        dxv_ref[...] = _conv_bwd(duv, xv, wv_ref, dwv_ref, dbv_ref, ALL, FFN_CONV).astype(BF16)

    col = pl.BlockSpec((S, 128), lambda j: (0, j))
    half = _sds((S, DFF), BF16)
    dxg, dxv, dwg, dwv, dbg, dbv = pl.pallas_call(
        body, name=name, grid=(NB_FF,),
        in_specs=[ANY, ANY, pl.BlockSpec((128, D), lambda j: (j, 0)), ug_, uv_, wg, wv, bg, bv],
        out_specs=[col, col, wg, wg, bg, bg],
        out_shape=[half, half, _sds((FFN_CONV, DFF), F32), _sds((FFN_CONV, DFF), F32), _sds((1, DFF), F32), _sds((1, DFF), F32)],
        scratch_shapes=[pltpu.VMEM((S, D), BF16), pltpu.VMEM((S, D), BF16)],
        compiler_params=_params(("arbitrary",)))(dzb, hb, w_down, w_up, w_up, cw, cw, cb, cb)
    return dxg, dxv, jnp.concatenate([dwg, dwv], axis=1), jnp.concatenate([dbg, dbv], axis=1)


def _log_sigmoid(x):
    return jnp.minimum(x, 0.0) - jnp.log1p(jnp.exp(-jnp.abs(x)))


@jax.custom_vjp
def _clamp_div(num, den, floor, shift):
    return num / jnp.maximum(jnp.abs(den), floor)


def _clamp_div_fwd(num, den, floor, shift):
    out = num / jnp.maximum(jnp.abs(den), floor)
    return out, (den, floor, out)


def _clamp_div_bwd(res, g):
    den, floor, out = res
    active = jnp.abs(den) < floor
    dinv = jnp.maximum(jnp.abs(den), floor)
    go = jnp.sum(g * out, axis=-1, keepdims=True)
    ddiv = -go / dinv
    return (g / dinv, jnp.where(active, 0.0, ddiv * jnp.sign(den)), jnp.where(active, ddiv, 0.0),
            jnp.sum(jnp.where(active, go, 0.0), axis=-2, keepdims=True))


_clamp_div.defvjp(_clamp_div_fwd, _clamp_div_bwd)


def _ml_heads(q, k, v, o_pre, gates, gbias, C, n, ng, m, shift):
    H, L, _ = q.shape
    lane1 = lax.broadcasted_iota(jnp.int32, (1, 128), 1)
    gz = gates + jnp.where(lane1 < ML_H, lax.stop_gradient(gbias), gbias)
    gz = jnp.broadcast_to(gz[None], (H, L, 128))
    hid = lax.broadcasted_iota(jnp.int32, (H, L, 128), 0)
    lane = lax.broadcasted_iota(jnp.int32, (H, L, 128), 2)
    ig = jnp.sum(jnp.where(lane == hid, gz, 0.0), axis=2, keepdims=True)
    lf = _log_sigmoid(jnp.sum(jnp.where(lane == ML_H + hid, gz, 0.0), axis=2, keepdims=True))
    r = lax.broadcasted_iota(jnp.int32, (H, L, L), 1)
    c = lax.broadcasted_iota(jnp.int32, (H, L, L), 2)
    eye, tril = r == c, r >= c

    def to_row(col):
        return jnp.sum(jnp.where(eye, col, 0.0), axis=1, keepdims=True)

    b_col = jnp.sum(jnp.where(tril, to_row(lf), 0.0), axis=2, keepdims=True)
    Dm = jnp.where(tril, b_col - to_row(b_col) + to_row(ig), -jnp.inf)
    inter = b_col + m
    m_t = lax.stop_gradient(jnp.maximum(inter, jnp.max(Dm, axis=2, keepdims=True)))
    w_inter = jnp.exp(inter - m_t)
    ks = k * (ML_DH ** -0.5)
    s = dot_nt(q, ks) * jnp.exp(Dm - m_t)
    num = w_inter * dot_nn(q, C) + dot_nn(s, v)
    den = w_inter * jnp.sum(q * n, axis=2, keepdims=True) + jnp.sum(s, axis=2, keepdims=True)
    h = _clamp_div(num, den, jnp.exp(-m_t), shift)
    g = jnp.sum(lf, axis=1, keepdims=True)
    a = g - b_col + ig
    m_new = lax.stop_gradient(jnp.maximum(g + m, jnp.max(a, axis=1, keepdims=True)))
    decay = jnp.exp(g + m - m_new)
    wk = jnp.exp(a - m_new)
    C_new = decay * C + dot_tn(ks * wk, v)
    n_new = decay * n + jnp.sum(wk * ks, axis=1, keepdims=True)
    mu = jnp.mean(h, axis=2, keepdims=True)
    hc = h - mu
    var = jnp.mean(hc * hc, axis=2, keepdims=True)
    out = jax.nn.sigmoid(o_pre) * (hc * lax.rsqrt(var + EPS) * ng)
    return out, C_new, n_new, m_new


def _hs(h, off=0):
    return slice(off + h * ML_DH, off + (h + 1) * ML_DH)


def _heads(ref, off=0):
    return jnp.stack([ref[:, _hs(h, off)] for h in range(ML_H)])


def _mlstm_fwd(qk, proj, gbias, ng, *, name):
    S = qk.shape[0]
    nc = S // ML_L

    def body(q_ref, k_ref, v_ref, o_ref, g_ref, gb_ref, ng_ref, h_ref, cs_ref, ns_ref, ms_ref, c_s, n_s, m_s):
        @pl.when(pl.program_id(0) == 0)
        def _():
            c_s[...] = jnp.zeros_like(c_s)
            n_s[...] = jnp.zeros_like(n_s)
            m_s[...] = jnp.zeros_like(m_s)

        C, n = c_s[...], n_s[...]
        cs_ref[0] = C
        ns_ref[0] = n
        ms_ref[0] = m_s[...]
        out, C2, n2, m2 = _ml_heads(_heads(q_ref), _heads(k_ref), _heads(v_ref), _heads(o_ref), g_ref[...], gb_ref[...], C, n,
                                    _heads(ng_ref), m_s[:, :, 0:1], jnp.zeros((ML_H, 1, 1), F32))
        for h in range(ML_H):
            h_ref[:, _hs(h)] = out[h].astype(BF16)
        c_s[...] = C2
        n_s[...] = n2
        m_s[...] = jnp.broadcast_to(m2, (ML_H, 1, 128))

    def w(j):
        return pl.BlockSpec((ML_L, ML_W), lambda c, j=j: (c, j))

    return pl.pallas_call(
        body, name=name, grid=(nc,),
        in_specs=[w(0), w(1), w(2), w(3), pl.BlockSpec((ML_L, 128), lambda c: (c, 22)),
                  pl.BlockSpec((1, 128), lambda c: (0, 0)), pl.BlockSpec((1, ML_W), lambda c: (0, 0))],
        out_specs=[w(0), pl.BlockSpec((1, ML_H, ML_DH, ML_DH), lambda c: (c, 0, 0, 0)),
                   pl.BlockSpec((1, ML_H, 1, 128), lambda c: (c, 0, 0, 0)), pl.BlockSpec((1, ML_H, 1, 128), lambda c: (c, 0, 0, 0))],
        out_shape=[_sds((S, ML_W), BF16), _sds((nc, ML_H, ML_DH, ML_DH), F32), _sds((nc, ML_H, 1, 128), F32),
                   _sds((nc, ML_H, 1, 128), F32)],
        scratch_shapes=[pltpu.VMEM((ML_H, ML_DH, ML_DH), F32), pltpu.VMEM((ML_H, 1, 128), F32), pltpu.VMEM((ML_H, 1, 128), F32)],
        compiler_params=_params(("arbitrary",)))(qk, qk, proj, proj, proj, gbias, ng)


def _mlstm_bwd(dh, qk, proj, gbias, ng, cs, ns, ms, *, name):
    S = qk.shape[0]
    nc = S // ML_L

    def body(dh_ref, q_ref, k_ref, v_ref, o_ref, g_ref, gb_ref, ng_ref, cs_ref, ns_ref, ms_ref,
             dqk_ref, dml_ref, dgb_ref, dng_ref, dc_s, dn_s):
        @pl.when(pl.program_id(0) == 0)
        def _():
            dc_s[...] = jnp.zeros_like(dc_s)
            dn_s[...] = jnp.zeros_like(dn_s)
            dgb_ref[...] = jnp.zeros_like(dgb_ref)
            dng_ref[...] = jnp.zeros_like(dng_ref)

        m = ms_ref[0][:, :, 0:1]

        def f(q, k, v, o_pre, gates, gb, C, n, ng_, shift):
            return _ml_heads(q, k, v, o_pre, gates, gb, C, n, ng_, m, shift)[:3]

        _, vjp = jax.vjp(f, _heads(q_ref), _heads(k_ref), _heads(v_ref), _heads(o_ref), g_ref[...], gb_ref[...],
                         cs_ref[0], ns_ref[0], _heads(ng_ref), jnp.zeros((ML_H, 1, 1), F32))
        dq, dk, dv, do, dgates, dgb, dC, dn, dng, dshift = vjp((_heads(dh_ref), dc_s[...], dn_s[...]))
        lane1 = lax.broadcasted_iota(jnp.int32, (1, 128), 1)
        for h in range(ML_H):
            dqk_ref[:, _hs(h)] = dq[h]
            dqk_ref[:, _hs(h, ML_W)] = dk[h]
            dml_ref[:, _hs(h)] = dv[h].astype(BF16)
            dml_ref[:, _hs(h, ML_W)] = do[h].astype(BF16)
            dng_ref[:, _hs(h)] += dng[h]
            dgb = jnp.where(lane1 == h, dshift[h], dgb)
        dc_s[...] = dC
        dn_s[...] = dn
        dml_ref[:, 2 * ML_W:] = dgates.astype(BF16)
        dgb_ref[...] += dgb

    def w(j):
        return pl.BlockSpec((ML_L, ML_W), lambda c, j=j: (nc - 1 - c, j))

    vec = pl.BlockSpec((1, 128), lambda c: (0, 0))
    vecw = pl.BlockSpec((1, ML_W), lambda c: (0, 0))
    gsp = pl.BlockSpec((ML_L, 128), lambda c: (nc - 1 - c, 22))
    st = pl.BlockSpec((1, ML_H, 1, 128), lambda c: (nc - 1 - c, 0, 0, 0))
    return pl.pallas_call(
        body, name=name, grid=(nc,),
        in_specs=[w(0), w(0), w(1), w(2), w(3), gsp, vec, vecw,
                  pl.BlockSpec((1, ML_H, ML_DH, ML_DH), lambda c: (nc - 1 - c, 0, 0, 0)), st, st],
        out_specs=[pl.BlockSpec((ML_L, 2 * ML_W), lambda c: (nc - 1 - c, 0)), pl.BlockSpec((ML_L, ML_GW), lambda c: (nc - 1 - c, 0)),
                   vec, vecw],
        out_shape=[_sds((S, 2 * ML_W), F32), _sds((S, ML_GW), BF16), _sds((1, 128), F32), _sds((1, ML_W), F32)],
        scratch_shapes=[pltpu.VMEM((ML_H, ML_DH, ML_DH), F32), pltpu.VMEM((ML_H, 1, 128), F32)],
        compiler_params=_params(("arbitrary",)))(dh, qk, qk, proj, proj, proj, gbias, ng, cs, ns, ms)


def _t5_buckets():
    r = np.arange(BLK)[:, None]
    c = np.arange(2 * BLK)[None, :]
    n = np.maximum(BLK + r - c, 0)
    max_exact = REL_B // 2
    nf = np.maximum(n, 1).astype(np.float32)
    large = max_exact + (np.log(nf / np.float32(max_exact)) / np.float32(math.log(REL_MAXD / max_exact))
                         * np.float32(REL_B - max_exact)).astype(np.int32)
    large = np.minimum(large, REL_B - 1)
    return np.where(n < max_exact, n, large).astype(np.int32)


def _bias_table(rel_bias, bucket, *, name):
    def body(rb_ref, bk_ref, o_ref):
        bk = bk_ref[...]
        for h in range(SW_H):
            acc = jnp.zeros((BLK, 2 * BLK), F32)
            for b in range(REL_B):
                acc = jnp.where(bk == b, rb_ref[b, h], acc)
            o_ref[h] = acc

    return pl.pallas_call(
        body, name=name, in_specs=[pl.BlockSpec(memory_space=pltpu.SMEM), pl.BlockSpec(memory_space=pltpu.VMEM)],
        out_specs=pl.BlockSpec(memory_space=pltpu.VMEM), out_shape=_sds((SW_H, BLK, 2 * BLK), F32),
        compiler_params=_params())(rel_bias, bucket)


def _bias_table_bwd(dbias_list, bucket, *, name):
    nl = len(dbias_list)

    def body(*refs):
        d_refs, bk_ref, o_ref = refs[:nl], refs[nl], refs[nl + 1]
        bk = bk_ref[...]
        rows = lax.broadcasted_iota(jnp.int32, (REL_B, 128), 0)
        lanes = lax.broadcasted_iota(jnp.int32, (REL_B, 128), 1)
        acc = jnp.zeros((REL_B, 128), F32)
        for h in range(SW_H):
            d = d_refs[0][h]
            for d_ref in d_refs[1:]:
                d = d + d_ref[h]
            for b in range(REL_B):
                t = jnp.sum(jnp.sum(jnp.where(bk == b, d, 0.0), axis=0, keepdims=True), axis=1, keepdims=True)
                acc = jnp.where((rows == b) & (lanes == h), t, acc)
        o_ref[...] = acc

    vm = pl.BlockSpec(memory_space=pltpu.VMEM)
    return pl.pallas_call(
        body, name=name, in_specs=[vm] * (nl + 1), out_specs=vm, out_shape=_sds((REL_B, 128), F32),
        compiler_params=_params())(*dbias_list, bucket)


def _swa_heads(q, kp, kc, vp, vc, bp, bc, sinks, has_prev):
    def rep(t):
        return jnp.concatenate([t[g:g + 1] for g in range(SW_H // SW_G) for _ in range(SW_G)], axis=0)

    r = lax.broadcasted_iota(jnp.int32, (SW_H, BLK, BLK), 1)
    c = lax.broadcasted_iota(jnp.int32, (SW_H, BLK, BLK), 2)
    hid = lax.broadcasted_iota(jnp.int32, (SW_H, 1, 128), 0)
    lane = lax.broadcasted_iota(jnp.int32, (SW_H, 1, 128), 2)
    sink = jnp.sum(jnp.where(lane == hid, jnp.broadcast_to(sinks[None], (SW_H, 1, 128)), 0.0), axis=2, keepdims=True)
    lp = jnp.where((c > r) & has_prev, dot_nt(q, rep(kp)) * (SW_DH ** -0.5) + bp, -jnp.inf)
    lc = jnp.where(c <= r, dot_nt(q, rep(kc)) * (SW_DH ** -0.5) + bc, -jnp.inf)
    mx = jnp.maximum(jnp.maximum(jnp.max(lp, axis=2, keepdims=True), jnp.max(lc, axis=2, keepdims=True)), sink)
    mx = lax.stop_gradient(mx)
    pp, pc = jnp.exp(lp - mx), jnp.exp(lc - mx)
    den = jnp.sum(pp, axis=2, keepdims=True) + jnp.sum(pc, axis=2, keepdims=True) + jnp.exp(sink - mx)
    return dot_nn(pp / den, rep(vp)) + dot_nn(pc / den, rep(vc))


def _qs(h, off=0):
    return slice(off + h * SW_DH, off + (h + 1) * SW_DH)


def _split(ref, n):
    return jnp.stack([ref[:, _qs(h)] for h in range(n)])


def _swa_fwd(proj, bias, sinks, *, name):
    S = proj.shape[0]
    nb = S // BLK
    nkv = SW_H // SW_G

    def body(q_ref, kp_ref, kc_ref, vp_ref, vc_ref, b_ref, s_ref, o_ref):
        out = _swa_heads(_split(q_ref, SW_H), _split(kp_ref, nkv), _split(kc_ref, nkv), _split(vp_ref, nkv), _split(vc_ref, nkv),
                         b_ref[:, :, :BLK], b_ref[:, :, BLK:], s_ref[...], pl.program_id(0) > 0)
        for h in range(SW_H):
            o_ref[:, _qs(h)] = out[h].astype(BF16)

    def cur(j):
        return pl.BlockSpec((BLK, 128), lambda n, j=j: (n, j))

    def prev(j):
        return pl.BlockSpec((BLK, 128), lambda n, j=j: (jnp.maximum(n - 1, 0), j))

    return pl.pallas_call(
        body, name=name, grid=(nb,),
        in_specs=[pl.BlockSpec((BLK, SW_W), lambda n: (n, 4)), prev(20), cur(20), prev(21), cur(21),
                  pl.BlockSpec((SW_H, BLK, 2 * BLK), lambda n: (0, 0, 0)), pl.BlockSpec((1, 128), lambda n: (0, 0))],
        out_specs=pl.BlockSpec((BLK, SW_W), lambda n: (n, 0)), out_shape=_sds((S, SW_W), BF16),
        compiler_params=_params(("parallel",)))(proj, proj, proj, proj, proj, bias, sinks)


def _swa_bwd(dh, proj, bias, sinks, *, name):
    S = proj.shape[0]
    nb = S // BLK

    nkv = SW_H // SW_G

    def body(dh_ref, q_ref, kp_ref, kc_ref, vp_ref, vc_ref, b_ref, s_ref, dsw_ref, db_ref, ds_ref, ck_s, cv_s):
        i = pl.program_id(0)

        @pl.when(i == 0)
        def _():
            ck_s[...] = jnp.zeros_like(ck_s)
            cv_s[...] = jnp.zeros_like(cv_s)
            db_ref[...] = jnp.zeros_like(db_ref)
            ds_ref[...] = jnp.zeros_like(ds_ref)

        f = functools.partial(_swa_heads, has_prev=i < nb - 1)
        _, vjp = jax.vjp(f, _split(q_ref, SW_H), _split(kp_ref, nkv), _split(kc_ref, nkv), _split(vp_ref, nkv),
                         _split(vc_ref, nkv), b_ref[:, :, :BLK], b_ref[:, :, BLK:], s_ref[...])
        dq, dkp, dkc, dvp, dvc, dbp, dbc, ds = vjp(_split(dh_ref, SW_H))
        for h in range(SW_H):
            dsw_ref[:, _qs(h)] = dq[h].astype(BF16)
        for g in range(nkv):
            dsw_ref[:, _qs(g, SW_W)] = (dkc[g] + ck_s[:, _qs(g)]).astype(BF16)
            dsw_ref[:, _qs(g, SW_W + SW_KVW)] = (dvc[g] + cv_s[:, _qs(g)]).astype(BF16)
            ck_s[:, _qs(g)] = dkp[g]
            cv_s[:, _qs(g)] = dvp[g]
        db_ref[:, :, :BLK] += dbp
        db_ref[:, :, BLK:] += dbc
        ds_ref[...] += ds

    def cur(j):
        return pl.BlockSpec((BLK, 128), lambda i, j=j: (nb - 1 - i, j))

    def prev(j):
        return pl.BlockSpec((BLK, 128), lambda i, j=j: (jnp.maximum(nb - 2 - i, 0), j))

    bsp = pl.BlockSpec((SW_H, BLK, 2 * BLK), lambda i: (0, 0, 0))
    vec = pl.BlockSpec((1, 128), lambda i: (0, 0))
    return pl.pallas_call(
        body, name=name, grid=(nb,),
        in_specs=[pl.BlockSpec((BLK, SW_W), lambda i: (nb - 1 - i, 1)), pl.BlockSpec((BLK, SW_W), lambda i: (nb - 1 - i, 4)),
                  prev(20), cur(20), prev(21), cur(21), bsp, vec],
        out_specs=[pl.BlockSpec((BLK, SW_GW), lambda i: (nb - 1 - i, 0)), bsp, vec],
        out_shape=[_sds((S, SW_GW), BF16), _sds((SW_H, BLK, 2 * BLK), F32), _sds((1, 128), F32)],
        scratch_shapes=[pltpu.VMEM((BLK, 128), F32)] * 2,
        compiler_params=_params(("arbitrary",)))(dh, proj, proj, proj, proj, proj, bias, sinks)


XA_TM = 512


def _xa_head(qh, kh, vh):
    logits = dot_nt(qh, kh) * (XA_DH ** -0.5)
    mx = lax.stop_gradient(jnp.max(logits, axis=1, keepdims=True))
    e = jnp.exp(logits - mx)
    return dot_nn(e / jnp.sum(e, axis=1, keepdims=True), vh)


def _xs(h, off=0):
    return slice(off + h * XA_DH, off + (h + 1) * XA_DH)


def _xattn_fwd(hb, wq, kv, *, name):
    S = hb.shape[0]
    M = kv.shape[0]

    def body(h_ref, wq_ref, kv_ref, o_ref):
        q = _dg(h_ref[...], wq_ref[...], NN)
        for h in range(XA_H):
            o_ref[:, _xs(h)] = _xa_head(q[:, _xs(h)], kv_ref[:, _xs(h)], kv_ref[:, _xs(h, D)]).astype(BF16)

    tm = min(XA_TM, S)
    row = pl.BlockSpec((tm, D), lambda i: (i, 0))
    return pl.pallas_call(
        body, name=name, grid=(S // tm,),
        in_specs=[row, pl.BlockSpec((D, D), lambda i: (0, 0)), pl.BlockSpec((M, 2 * D), lambda i: (0, 0))], out_specs=row,
        out_shape=_sds((S, D), BF16), compiler_params=_params(("parallel",)))(hb, wq, kv)


def _xattn_bwd(dzb, wo, hb, wq, kv, *, name):
    S = hb.shape[0]
    M = kv.shape[0]

    def body(dz_ref, wo_ref, h_ref, wq_ref, kv_ref, dq_ref, dkv_ref):
        @pl.when(pl.program_id(0) == 0)
        def _():
            dkv_ref[...] = jnp.zeros_like(dkv_ref)

        q = _dg(h_ref[...], wq_ref[...], NN)
        do = _dg(dz_ref[...], wo_ref[...], NT)
        for h in range(XA_H):
            _, vjp = jax.vjp(_xa_head, q[:, _xs(h)], kv_ref[:, _xs(h)], kv_ref[:, _xs(h, D)])
            dq, dk, dv = vjp(do[:, _xs(h)])
            dq_ref[:, _xs(h)] = dq.astype(BF16)
            dkv_ref[:, _xs(h)] += dk
            dkv_ref[:, _xs(h, D)] += dv

    tm = min(XA_TM, S)
    row = pl.BlockSpec((tm, D), lambda i: (i, 0))
    full = pl.BlockSpec((M, 2 * D), lambda i: (0, 0))
    sq = pl.BlockSpec((D, D), lambda i: (0, 0))
    return pl.pallas_call(
        body, name=name, grid=(S // tm,), in_specs=[row, sq, row, sq, full],
        out_specs=[row, full], out_shape=[_sds((S, D), BF16), _sds((M, 2 * D), F32)],
        compiler_params=_params(("arbitrary",)))(dzb, wo, hb, wq, kv)


ANY = pl.BlockSpec(memory_space=pl.ANY)


def _place():
    x, y, c = lax.axis_index("x"), lax.axis_index("y"), lax.axis_index("c")
    chips = [(1 - x, y), (x, 1 - y), (1 - x, 1 - y)]
    return x, y, c, chips


def _gather(arrs, *, name):
    n = len(arrs)

    def body(*refs):
        srcs, outs = refs[:n], refs[n:2 * n]
        send_sems, recv_sems, local_sems = refs[2 * n:]
        x, y, c, chips = _place()
        me, sib = (x, y, c), (x, y, 1 - c)

        def idx(p):
            return 4 * p[0] + 2 * p[1] + p[2]

        def copy(i, k, block, to, from_src=False):
            return pltpu.make_async_remote_copy(
                src_ref=srcs[i] if from_src else outs[i].at[idx(block)], dst_ref=outs[i].at[idx(block)],
                send_sem=send_sems.at[7 * i + k], recv_sem=recv_sems.at[7 * i + k], device_id=to, device_id_type=MESH)

        local = [pltpu.make_async_copy(srcs[i], outs[i].at[idx(me)], local_sems.at[i]) for i in range(n)]
        for cp in local:
            cp.start()
        first = []
        for i in range(n):
            first.append(copy(i, 0, me, sib, True))
            first += [copy(i, 1 + j, me, (*chip, c), True) for j, chip in enumerate(chips)]
        for cp in first:
            cp.start()
        passed = []
        for j, chip in enumerate(chips):
            for i in range(n):
                copy(i, 1 + j, (*chip, c), me).wait_recv()
                cp = copy(i, 4 + j, (*chip, c), sib)
                cp.start()
                passed.append(cp)
        for i in range(n):
            copy(i, 0, sib, me).wait_recv()
        for j, chip in enumerate(chips):
            for i in range(n):
                copy(i, 4 + j, (*chip, 1 - c), me).wait_recv()
        for cp in first + passed:
            cp.wait_send()
        for cp in local:
            cp.wait()

    return pl.pallas_call(
        body, name=name, in_specs=[ANY] * n, out_specs=[ANY] * n,
        out_shape=[_sds((N_DEV,) + a.shape, a.dtype) for a in arrs],
        scratch_shapes=[pltpu.SemaphoreType.DMA((7 * n,)), pltpu.SemaphoreType.DMA((7 * n,)), pltpu.SemaphoreType.DMA((n,))],
        compiler_params=pltpu.CompilerParams(has_side_effects=True))(*arrs)


HBM = pl.BlockSpec(memory_space=pltpu.HBM)
SEM = pl.BlockSpec(memory_space=pltpu.SEMAPHORE)
EFFECT = pltpu.SideEffectType.DATAFLOW_SIDE_EFFECTING


def _near_copies(srcs, lands, send_sems, recv_sems):
    x, y, c, chips = _place()
    me = 4 * x + 2 * y + c
    out = []
    for i in range(len(srcs)):
        for k, (px, py, pc) in enumerate([(x, y, 1 - c)] + [(*chip, c) for chip in chips]):
            out.append(tuple(pltpu.make_async_remote_copy(
                src_ref=srcs[i], dst_ref=lands[i].at[slot], send_sem=send_sems.at[4 * i + k], recv_sem=recv_sems.at[4 * i + k],
                device_id=(px, py, pc), device_id_type=MESH) for slot in (me, 4 * px + 2 * py + pc)))
    return out


def _forward_sibling(lands, *, name):
    n = len(lands)

    def body(*refs):
        bufs = refs[n:2 * n]
        send_sems, recv_sems = refs[2 * n:]
        x, y, c, chips = _place()
        copies = [tuple(pltpu.make_async_remote_copy(
            src_ref=bufs[i].at[4 * chip[0] + 2 * chip[1] + c], dst_ref=bufs[i].at[4 * chip[0] + 2 * chip[1] + cc],
            send_sem=send_sems.at[3 * i + j], recv_sem=recv_sems.at[3 * i + j], device_id=(x, y, 1 - c), device_id_type=MESH)
            for cc in (c, 1 - c)) for i in range(n) for j, chip in enumerate(chips)]
        for send, _ in copies:
            send.start()
        for send, recv in copies:
            send.wait_send()
            recv.wait_recv()

    return pl.pallas_call(
        body, name=name, in_specs=[ANY] * n, out_specs=[ANY] * n, out_shape=[_sds(a.shape, a.dtype) for a in lands],
        input_output_aliases={i: i for i in range(n)},
        scratch_shapes=[pltpu.SemaphoreType.DMA((3 * n,)), pltpu.SemaphoreType.DMA((3 * n,))],
        compiler_params=pltpu.CompilerParams(has_side_effects=True))(*lands)


def _gather_copies(srcs, lands, send_sems, recv_sems):
    return _scatter_copies(srcs, lands, send_sems, recv_sems, whole=True)


def _scatter_copies(srcs, lands, send_sems, recv_sems, whole=False):
    x, y, c, _ = _place()
    me = 4 * x + 2 * y + c
    out = []
    for i in range(len(srcs)):
        for j in range(1, N_DEV):
            px, py, pc = x ^ ((j >> 2) & 1), y ^ ((j >> 1) & 1), c ^ (j & 1)
            p = 4 * px + 2 * py + pc
            k = (N_DEV - 1) * i + j - 1
            out.append(tuple(pltpu.make_async_remote_copy(
                src_ref=srcs[i] if whole else srcs[i].at[s], dst_ref=lands[i].at[d], send_sem=send_sems.at[k],
                recv_sem=recv_sems.at[k], device_id=(px, py, pc), device_id_type=MESH) for s, d in ((p, me), (me, p))))
    return out


def _split_start(srcs, lands, pattern, ncopy, *, after=(), name):
    n = len(srcs)
    na = len(after)

    def body(*refs):
        sems = refs[2 * n + na:]
        for send, _ in pattern(refs[:n], refs[n:2 * n], sems[0], sems[1]):
            send.start()
        refs[-1][...] = jnp.zeros_like(refs[-1])

    arrs = list(srcs) + list(lands)
    return pl.pallas_call(
        body, name=name,
        out_shape=(pltpu.SemaphoreType.DMA((ncopy,)), pltpu.SemaphoreType.DMA((ncopy,)),
                   *[pltpu.HBM(a.shape, a.dtype) for a in arrs], _sds((8, 128), F32)),
        in_specs=[HBM] * (2 * n) + [ANY] * na, out_specs=(SEM, SEM, *[HBM] * (2 * n), pl.BlockSpec(memory_space=pltpu.VMEM)),
        input_output_aliases={i: 2 + i for i in range(2 * n)},
        compiler_params=pltpu.CompilerParams(has_side_effects=EFFECT))(
            *[pltpu.with_memory_space_constraint(a, pltpu.HBM) for a in arrs], *after)


def _split_wait(started, after, pattern, *, name):
    send_sems, recv_sems, *arrs = started[:-1]
    n = len(arrs) // 2

    def body(*refs):
        for send, recv in pattern(refs[:n], refs[n:2 * n], refs[2 * n], refs[2 * n + 1]):
            send.wait_send()
            recv.wait_recv()

    outs = pl.pallas_call(
        body, name=name, out_shape=tuple(pltpu.HBM(a.shape, a.dtype) for a in arrs),
        in_specs=[HBM] * (2 * n) + [SEM, SEM, ANY], out_specs=tuple([HBM] * (2 * n)),
        input_output_aliases={i: i for i in range(2 * n)},
        compiler_params=pltpu.CompilerParams(has_side_effects=EFFECT))(*arrs, send_sems, recv_sems, after)
    return list(outs[n:])


def _adamw(parts, w, m, v, *, layer=None, prev=None, dep=(), name):
    P, R, C = parts.shape
    tr = next((t for t in (512, 256, 176, 128, 64, 32, 16, 8) if R % t == 0 and t * C <= 256 * 1024), R)
    c1 = 1.0 / (1.0 - ADAM_B1 ** ADAM_STEP)
    c2 = 1.0 / (1.0 - ADAM_B2 ** ADAM_STEP)
    nprev = 0 if prev is None else 4

    def body(p_ref, w_ref, m_ref, v_ref, *rest):
        g_ref, d_ref, nm_ref, nv_ref = rest[nprev + len(dep):]
        g = p_ref[0].astype(F32)
        for j in range(1, P):
            g = g + p_ref[j].astype(F32)
        g = g.reshape(w_ref.shape)
        nm = ADAM_B1 * m_ref[...] + (1.0 - ADAM_B1) * g
        nv = ADAM_B2 * v_ref[...] + (1.0 - ADAM_B2) * (g * g)
        g_ref[...] = g
        nm_ref[...] = nm
        nv_ref[...] = nv
        d_ref[...] = -ADAM_LR * ((nm * c1) / (jnp.sqrt(nv * c2) + ADAM_EPS) + ADAM_WD * w_ref[...])

    if layer is None:
        row = pl.BlockSpec((tr, C), lambda i: (i, 0))
    else:
        row = pl.BlockSpec((1, tr, C), lambda i: (layer, i, 0))
    out = _sds(w.shape, F32)
    return pl.pallas_call(
        body, name=name, grid=(R // tr,),
        in_specs=[pl.BlockSpec((P, tr, C), lambda i: (0, i, 0)), row, row, row] + [ANY] * (nprev + len(dep)),
        out_specs=[row, row, row, row], out_shape=[out, out, out, out],
        input_output_aliases={4 + k: k for k in range(nprev)},
        compiler_params=_params(("parallel",)))(parts, w, m, v, *(prev or ()), *dep)


BIG = ("w_in", "w_out", "xa_wq", "xa_wkv", "xa_wo", "ffn_w_up", "ffn_w_down")
COL_SHARDED = ("w_in", "xa_wkv", "ffn_w_up", "ml_conv_w", "ffn_conv_w")
SHARDED_SMALL = ("ml_conv_w", "ffn_conv_w")
REPLICATED = ("rel_bias", "ml_conv_b", "ml_i_bias", "ml_f_bias", "ml_norm_g", "swa_sinks", "ln1_g", "ln1_b", "ln2_g", "ln2_b",
              "ffn_conv_b", "ln3_g", "ln3_b")
NAMES = ("rel_bias", "w_in", "ml_conv_w", "ml_conv_b", "ml_i_bias", "ml_f_bias", "ml_norm_g", "swa_sinks", "w_out", "ln1_g", "ln1_b",
         "xa_wq", "xa_wkv", "xa_wo", "ln2_g", "ln2_b", "ffn_w_up", "ffn_conv_w", "ffn_conv_b", "ffn_w_down", "ln3_g", "ln3_b")


def _flat_rows(a, mult):
    f = a.reshape(-1)
    n = -(-f.shape[0] // (128 * mult)) * (128 * mult)
    if n != f.shape[0]:
        f = jnp.pad(f, (0, n - f.shape[0]))
    return f.reshape(-1, 128)


def _pack(arrs, mult):
    parts = [_flat_rows(a, mult) for a in arrs]
    return jnp.concatenate(parts, axis=0), [p.shape[0] for p in parts]


def _unpack(flat, rows, shapes):
    out, off = [], 0
    lead = flat.shape[:-2]
    for r, shp in zip(rows, shapes):
        n = int(np.prod(shp))
        piece = flat[..., off:off + r, :].reshape(lead + (r * 128,))[..., :n]
        out.append(piece.reshape(lead + tuple(shp)))
        off += r
    return out


def _full_from_shards(stacked, name):
    if name in COL_SHARDED:
        return jnp.moveaxis(stacked, 0, 2).reshape(stacked.shape[1], stacked.shape[2], N_DEV * stacked.shape[3])
    return jnp.moveaxis(stacked, 0, 1).reshape(stacked.shape[1], N_DEV * stacked.shape[2], stacked.shape[3])


def _shards_from_full(full, name):
    L, A, B = full.shape
    if name in COL_SHARDED:
        return jnp.moveaxis(full.reshape(L, A, N_DEV, B // N_DEV), 2, 0)
    return jnp.moveaxis(full.reshape(L, N_DEV, A // N_DEV, B), 1, 0)


def _pad_win(w):
    z = jnp.zeros(w.shape[:-1] + (NP_IN - N_IN,), w.dtype)
    return jnp.concatenate([w[..., :2048], w[..., 2056:], w[..., 2048:2056], z], axis=-1)


def _row128(v):
    return jnp.pad(v, (0, 128 - v.shape[0])).reshape(1, 128)


REST = BIG[1:]


def _layer_full(stacked, n):
    if n in COL_SHARDED:
        full = jnp.moveaxis(stacked, 0, 1).reshape(stacked.shape[1], N_DEV * stacked.shape[2])
    else:
        full = stacked.reshape(N_DEV * stacked.shape[1], stacked.shape[2])
    return _pad_win(full) if n == "w_in" else full


def _layer_shards(g, n):
    A, B = g.shape
    if n in COL_SHARDED:
        return jnp.moveaxis(g.reshape(A, N_DEV, B // N_DEV), 1, 0).astype(BF16)
    return g.reshape(N_DEV, A // N_DEV, B).astype(BF16)


def _with_own_block(a, idx, nblk):
    return lax.dynamic_update_slice(lax.empty((nblk,) + a.shape, a.dtype), a[None], (idx,) + (0,) * a.ndim)


def _gather_start(arrs, me, *, after=(), name):
    return _split_start(arrs, [_with_own_block(a, me, N_DEV) for a in arrs], _near_copies, 4 * len(arrs), after=after, name=name)


def _gather_finish(started, after, *, name):
    return _forward_sibling(_split_wait(started, after, _near_copies, name=name + "_wait"), name=name + "_forward")


def _reduce_start(tag, names, grads, me):
    send = [_layer_shards(g, n) for n, g in zip(names, grads)]
    lands = [_with_own_block(lax.dynamic_index_in_dim(s, me, 0, keepdims=False), me, N_DEV) for s in send]
    return _split_start(send, lands, _scatter_copies, (N_DEV - 1) * len(send), name=f"reduce_{tag}_start")


def _small_start(Gf, loss_part, me):
    sp_flat, sp_rows = _pack([Gf[n] for n in REPLICATED + SHARDED_SMALL] + [loss_part], 8)
    return _split_start([sp_flat], [_with_own_block(sp_flat, me, N_DEV)], _gather_copies, N_DEV - 1, name="gather_small_start"), sp_rows


def _small_finish(started, sp_rows, after, Gf, W, Mo, Vo, me):
    res = {}
    small = REPLICATED + SHARDED_SMALL
    sp_all = _split_wait(started, after, _gather_copies, name="gather_small_wait")[0]

    def widen(n, t):
        if n not in SHARDED_SMALL:
            return t[n]
        return lax.dynamic_update_slice(jnp.zeros(Gf[n].shape, F32), t[n], (0, 0, me * t[n].shape[2]))

    zl = jnp.zeros((8, 128), F32)
    wsm, _ = _pack([widen(n, W) for n in small] + [zl], 8)
    msm, _ = _pack([widen(n, Mo) for n in small] + [zl], 8)
    vsm, _ = _pack([widen(n, Vo) for n in small] + [zl], 8)
    outs_small = [_unpack(o_, sp_rows, [Gf[n].shape for n in small] + [(8, 128)])
                  for o_ in _adamw(sp_all, wsm, msm, vsm, name="adamw_small")]
    for kind, os_ in zip(("g", "d", "m", "v"), outs_small):
        for n, a in zip(small, os_[:-1]):
            if n in SHARDED_SMALL:
                a = lax.dynamic_slice(a, (0, 0, me * W[n].shape[2]), W[n].shape)
            res[kind, n] = a
    return res, outs_small[0][-1][0, 0]


def kernel(x, mem, rel_bias, w_in, ml_conv_w, ml_conv_b, ml_i_bias, ml_f_bias, ml_norm_g, swa_sinks, w_out, ln1_g, ln1_b, xa_wq, xa_wkv, xa_wo, ln2_g, ln2_b, ffn_w_up, ffn_conv_w, ffn_conv_b, ffn_w_down, ln3_g, ln3_b, loss_target, m_rel_bias, m_w_in, m_ml_conv_w, m_ml_conv_b, m_ml_i_bias, m_ml_f_bias, m_ml_norm_g, m_swa_sinks, m_w_out, m_ln1_g, m_ln1_b, m_xa_wq, m_xa_wkv, m_xa_wo, m_ln2_g, m_ln2_b, m_ffn_w_up, m_ffn_conv_w, m_ffn_conv_b, m_ffn_w_down, m_ln3_g, m_ln3_b, v_rel_bias, v_w_in, v_ml_conv_w, v_ml_conv_b, v_ml_i_bias, v_ml_f_bias, v_ml_norm_g, v_swa_sinks, v_w_out, v_ln1_g, v_ln1_b, v_xa_wq, v_xa_wkv, v_xa_wo, v_ln2_g, v_ln2_b, v_ffn_w_up, v_ffn_conv_w, v_ffn_conv_b, v_ffn_w_down, v_ln3_g, v_ln3_b):
    W = dict(rel_bias=rel_bias, w_in=w_in, ml_conv_w=ml_conv_w, ml_conv_b=ml_conv_b, ml_i_bias=ml_i_bias, ml_f_bias=ml_f_bias,
             ml_norm_g=ml_norm_g, swa_sinks=swa_sinks, w_out=w_out, ln1_g=ln1_g, ln1_b=ln1_b, xa_wq=xa_wq, xa_wkv=xa_wkv,
             xa_wo=xa_wo, ln2_g=ln2_g, ln2_b=ln2_b, ffn_w_up=ffn_w_up, ffn_conv_w=ffn_conv_w, ffn_conv_b=ffn_conv_b,
             ffn_w_down=ffn_w_down, ln3_g=ln3_g, ln3_b=ln3_b)
    Mo = dict(rel_bias=m_rel_bias, w_in=m_w_in, ml_conv_w=m_ml_conv_w, ml_conv_b=m_ml_conv_b, ml_i_bias=m_ml_i_bias,
              ml_f_bias=m_ml_f_bias, ml_norm_g=m_ml_norm_g, swa_sinks=m_swa_sinks, w_out=m_w_out, ln1_g=m_ln1_g, ln1_b=m_ln1_b,
              xa_wq=m_xa_wq, xa_wkv=m_xa_wkv, xa_wo=m_xa_wo, ln2_g=m_ln2_g, ln2_b=m_ln2_b, ffn_w_up=m_ffn_w_up,
              ffn_conv_w=m_ffn_conv_w, ffn_conv_b=m_ffn_conv_b, ffn_w_down=m_ffn_w_down, ln3_g=m_ln3_g, ln3_b=m_ln3_b)
    Vo = dict(rel_bias=v_rel_bias, w_in=v_w_in, ml_conv_w=v_ml_conv_w, ml_conv_b=v_ml_conv_b, ml_i_bias=v_ml_i_bias,
              ml_f_bias=v_ml_f_bias, ml_norm_g=v_ml_norm_g, swa_sinks=v_swa_sinks, w_out=v_w_out, ln1_g=v_ln1_g, ln1_b=v_ln1_b,
              xa_wq=v_xa_wq, xa_wkv=v_xa_wkv, xa_wo=v_xa_wo, ln2_g=v_ln2_g, ln2_b=v_ln2_b, ffn_w_up=v_ffn_w_up,
              ffn_conv_w=v_ffn_conv_w, ffn_conv_b=v_ffn_conv_b, ffn_w_down=v_ffn_w_down, ln3_g=v_ln3_g, ln3_b=v_ln3_b)
    S = x.shape[1]
    c_me = lax.axis_index("c")
    me = 4 * lax.axis_index("x") + 2 * lax.axis_index("y") + c_me
    xs = x.reshape(S, D)
    mems = mem.reshape(mem.shape[1], D)
    tgt = loss_target.reshape(S, D)


    sm_flat, sm_rows = _pack([W[n] for n in SHARDED_SMALL], 8)
    first = _gather([w_in[0].astype(BF16), sm_flat], name="gather_first")
    rest0 = _gather_start([W[n][0].astype(BF16) for n in REST], me, name="gather_rest0_start")
    full = [{"w_in": _layer_full(first[0], "w_in")}, None]
    conv_w = {n: _full_from_shards(s, n) for n, s in zip(SHARDED_SMALL, _unpack(first[1], sm_rows, [W[n].shape for n in SHARDED_SMALL]))}

    bucket = jnp.asarray(_t5_buckets())
    bias = _bias_table(rel_bias, bucket, name="bias_table")

    saved = []
    h0, h0b = xs, xs.astype(BF16)
    h0t = h0b.T
    for l in range(DEPTH):
        gbias = _row128(jnp.concatenate([ml_i_bias[l], ml_f_bias[l]]))
        sinks = _row128(swa_sinks[l])
        ng = ml_norm_g[l].reshape(1, ML_W)
        proj = _mm(h0b, full[l]["w_in"], tn=1024, dep=(rest0[-1],) if l == 0 else (), name=f"proj{l}")
        qk = _silu_conv_fwd(proj, conv_w["ml_conv_w"][l], ml_conv_b[l].reshape(1, -1), name=f"mlconv{l}")
        h_ml, cs, ns, ms = _mlstm_fwd(qk, proj, gbias, ng, name=f"mlstm{l}")
        h_sw = _swa_fwd(proj, bias, sinks, name=f"swa{l}")
        dep = ()
        if l == 0:
            landed = _gather_finish(rest0, h_sw, name="gather_rest0")
            full[0].update({n: _layer_full(s, n) for n, s in zip(REST, landed)})
            layer1 = _gather_start([W[n][1].astype(BF16) for n in BIG], me, after=(landed[0],), name="gather_layer1_start")
            dep = (layer1[-1],)
        fw = full[l]
        h1, h1b, h1t, z1 = _mm_res_ln([h_ml, h_sw], fw["w_out"], h0, ln1_g[l].reshape(1, D), ln1_b[l].reshape(1, D),
                                      name=f"mix_out{l}")
        kv = _mm(mems, fw["xa_wkv"], tm=256, dep=dep, name=f"xa_kv{l}")
        o = _xattn_fwd(h1b, fw["xa_wq"], kv, name=f"xattn{l}")
        h2, h2b, h2t, z2 = _mm_res_ln([o], fw["xa_wo"], h1, ln2_g[l].reshape(1, D), ln2_b[l].reshape(1, D), name=f"xa_out{l}")
        act, act_t = _ffn_gate_fwd(h2b, fw["ffn_w_up"], conv_w["ffn_conv_w"][l], ffn_conv_b[l].reshape(1, -1), name=f"ffn_gate{l}")
        h3, h3b, h3t, z3 = _mm_res_ln([act], fw["ffn_w_down"], h2, ln3_g[l].reshape(1, D), ln3_b[l].reshape(1, D),
                                      name=f"ffn_out{l}")
        saved.append(dict(h0t=h0t, proj=proj, qk=qk, cs=cs, ns=ns, ms=ms, h_ml=h_ml, h_sw=h_sw, z1=z1, h1b=h1b, h1t=h1t, kv=kv, o=o,
                          z2=z2, h2b=h2b, h2t=h2t, act_t=act_t, z3=z3, gbias=gbias, sinks=sinks, ng=ng))
        h0, h0b, h0t = h3, h3b, h3t
        if l == 0:
            landed = _gather_finish(layer1, h3b, name="gather_layer1")
            full[1] = {n: _layer_full(s, n) for n, s in zip(BIG, landed)}

    G = {n: [None] * DEPTH for n in NAMES if n != "rel_bias"}
    dbias = [None] * DEPTH
    pending = []
    dz3, dz3b, G["ln3_g"][DEPTH - 1], G["ln3_b"][DEPTH - 1], loss_part = _grad_in(
        [], None, ln=(saved[-1]["z3"], ln3_g[DEPTH - 1].reshape(1, D)), loss=(h0, tgt), name="loss_head")
    for l in reversed(range(DEPTH)):
        sv, fw = saved[l], full[l]
        win = fw["w_in"]
        G["ffn_w_down"][l] = _wgrad(sv["act_t"], dz3b, name=f"d_w_down{l}")
        dupg, dupv, G["ffn_conv_w"][l], G["ffn_conv_b"][l] = _ffn_gate_bwd(
            dz3b, fw["ffn_w_down"], sv["h2b"], fw["ffn_w_up"], conv_w["ffn_conv_w"][l], ffn_conv_b[l].reshape(1, -1),
            name=f"ffn_gate_bwd{l}")
        G["ffn_w_up"][l] = jnp.concatenate([_wgrad(sv["h2t"], dupg, name=f"d_w_up_g{l}"),
                                            _wgrad(sv["h2t"], dupv, name=f"d_w_up_v{l}")], axis=1)
        dep = ()
        if l == 0:
            names = ("ffn_w_up", "ffn_w_down")
            pending.append((names, 0, "ffn0", _reduce_start("ffn0", names, [G[n][0] for n in names], me)))
            dep = (pending[-1][3][-1],)
        dz2, dz2b, G["ln2_g"][l], G["ln2_b"][l] = _grad_in(
            [(dupg, fw["ffn_w_up"], 0), (dupv, fw["ffn_w_up"], 1)], dz3, ln=(sv["z2"], ln2_g[l].reshape(1, D)), dep=dep,
            name=f"d_h2_{l}")
        G["xa_wo"][l] = _mm_tn(sv["o"], dz2b, name=f"d_xa_wo{l}")
        dq, dkv = _xattn_bwd(dz2b, fw["xa_wo"], sv["h1b"], fw["xa_wq"], sv["kv"], name=f"xattn_bwd{l}")
        G["xa_wkv"][l] = _mm_tn(mems, dkv, name=f"d_xa_wkv{l}")
        G["xa_wq"][l] = _wgrad(sv["h1t"], dq, name=f"d_xa_wq{l}")
        dep = ()
        if l == 0:
            names = ("xa_wq", "xa_wkv", "xa_wo")
            pending.append((names, 0, "xa0", _reduce_start("xa0", names, [G[n][0] for n in names], me)))
            dep = (pending[-1][3][-1],)
        dz1, dz1b, G["ln1_g"][l], G["ln1_b"][l] = _grad_in(
            [(dq, fw["xa_wq"], 0)], dz2, ln=(sv["z1"], ln1_g[l].reshape(1, D)), dep=dep, name=f"d_h1_{l}")
        G["w_out"][l] = jnp.concatenate([_mm_tn(sv["h_ml"], dz1b, name=f"d_w_out_ml{l}"),
                                         _mm_tn(sv["h_sw"], dz1b, name=f"d_w_out_sw{l}")], axis=0)
        dep = ()
        if l == 0:
            pending.append((("w_out",), 0, "out0", _reduce_start("out0", ("w_out",), [G["w_out"][0]], me)))
            dep = (pending[-1][3][-1],)
        dhcat = _mm(dz1b, fw["w_out"], trans_b=True, dep=dep, name=f"d_hcat{l}")
        dsw, dbias[l], dsinks = _swa_bwd(dhcat, sv["proj"], bias, sv["sinks"], name=f"swa_bwd{l}")
        dqk, dml, dgb, dng = _mlstm_bwd(dhcat, sv["qk"], sv["proj"], sv["gbias"], sv["ng"], sv["cs"], sv["ns"], sv["ms"],
                                        name=f"mlstm_bwd{l}")
        dqk_pre, G["ml_conv_w"][l], G["ml_conv_b"][l] = _silu_conv_bwd(
            dqk, sv["proj"], conv_w["ml_conv_w"][l], ml_conv_b[l].reshape(1, -1), name=f"mlconv_bwd{l}")
        dw_qk = _wgrad(sv["h0t"], dqk_pre, name=f"d_w_in_qk{l}")
        dw_ml = _wgrad(sv["h0t"], dml, name=f"d_w_in_ml{l}")
        dw_sw = _wgrad(sv["h0t"], dsw, name=f"d_w_in_sw{l}")
        G["w_in"][l] = jnp.concatenate([dw_qk, dw_ml[:, :2 * ML_W + 2 * ML_H], dw_sw], axis=1)
        win_ml = jnp.concatenate([win[:, 1024:2048], win[:, 2816:2944]], axis=1)
        pairs = [(dqk_pre, win, 0), (dml, win_ml, 0), (dsw, win[:, 2048:2816], 0)]
        G["ml_i_bias"][l] = dgb[0, :ML_H]
        G["ml_f_bias"][l] = dgb[0, ML_H:2 * ML_H]
        G["ml_norm_g"][l] = dng
        G["swa_sinks"][l] = dsinks[0, :SW_H]
        if l > 0:
            pending.append((BIG, l, f"l{l}", _reduce_start(f"l{l}", BIG, [G[n][l] for n in BIG], me)))
            dz3, dz3b, G["ln3_g"][l - 1], G["ln3_b"][l - 1] = _grad_in(
                pairs, dz1, ln=(saved[l - 1]["z3"], ln3_g[l - 1].reshape(1, D)), dep=(pending[-1][3][-1],), name=f"d_h0_{l}")
        else:
            pending.append((("w_in",), 0, "in0", _reduce_start("in0", ("w_in",), [G["w_in"][0]], me)))
            dh = _grad_in(pairs, dz1, dep=(pending[-1][3][-1],), name="d_h0_0")
            grad_x = dh.reshape(x.shape)

    Gf = {n: jnp.stack([g.reshape(W[n].shape[1:]) if n in REPLICATED else g for g in G[n]]) for n in G if n not in BIG}
    Gf["rel_bias"] = _bias_table_bwd(dbias, bucket, name="bias_table_bwd")[:, :SW_H]
    small_started, sp_rows = _small_start(Gf, loss_part, me)

    parts = {}
    for names, l, tag, started in pending:
        landed = _split_wait(started, dh, _scatter_copies, name=f"reduce_{tag}_wait")
        parts.update({(n, l): p for n, p in zip(names, landed)})
    res = {}
    dep = (small_started[-1],)
    for n in BIG:
        outs = None
        for l in reversed(range(DEPTH)):
            outs = _adamw(parts[n, l], W[n], Mo[n], Vo[n], layer=l, prev=outs, dep=dep, name=f"adamw_{n}{l}")
            dep = ()
        for kind, a in zip(("g", "d", "m", "v"), outs):
            res[kind, n] = a
    res_small, loss = _small_finish(small_started, sp_rows, res["g", BIG[-1]], Gf, W, Mo, Vo, me)
    res.update(res_small)
    return (loss, grad_x, *[res["g", n] for n in NAMES], *[res["d", n] for n in NAMES], *[res["m", n] for n in NAMES],
            *[res["v", n] for n in NAMES])
```

```python
import functools
import math

import jax
import jax.numpy as jnp
import numpy as np
from jax import lax
from jax.experimental import pallas as pl
from jax.experimental.pallas import tpu as pltpu

F32 = jnp.float32
BF16 = jnp.bfloat16

N_DEV = 8
D = 1024
DEPTH = 2
ML_H = 4
ML_W = 512
ML_DH = 128
ML_L = 64
ML_CONV = 4
SW_DH = 64
SW_W = 512
SW_H = 8
SW_G = 4
SW_KVW = 128
BLK = 128
REL_B = 32
REL_MAXD = 128
XA_H = 4
XA_DH = 256
DFF = 2816
NB_FF = DFF // 128
FFN_CONV = 3
ALPHA = (2.0 * DEPTH) ** 0.25
EPS = 1e-5
N_IN = 2824
NP_IN = 3072
ML_GW = 2 * ML_W + 128
SW_GW = SW_W + 2 * SW_KVW
ADAM_LR = 0.001
ADAM_B1 = 0.9
ADAM_B2 = 0.999
ADAM_EPS = 1e-08
ADAM_WD = 0.01
ADAM_STEP = 10
VMEM_LIMIT = 56 * 1024 * 1024
MESH = pl.DeviceIdType.MESH

NN = ((1,), (0,))
NT = ((1,), (1,))
TN = ((0,), (0,))


def _dg(a, b, dn):
    if a.ndim == 3:
        dims = (((dn[0][0] + 1,), (dn[1][0] + 1,)), ((0,), (0,)))
    else:
        dims = (dn, ((), ()))
    return lax.dot_general(a.astype(BF16), b.astype(BF16), dims, preferred_element_type=F32)


@jax.custom_vjp
def dot_nn(a, b):
    return _dg(a, b, NN)


dot_nn.defvjp(lambda a, b: (_dg(a, b, NN), (a, b)), lambda r, g: (_dg(g, r[1], NT), _dg(r[0], g, TN)))


@jax.custom_vjp
def dot_nt(a, b):
    return _dg(a, b, NT)


dot_nt.defvjp(lambda a, b: (_dg(a, b, NT), (a, b)), lambda r, g: (_dg(g, r[1], NN), _dg(g, r[0], TN)))


@jax.custom_vjp
def dot_tn(a, b):
    return _dg(a, b, TN)


dot_tn.defvjp(lambda a, b: (_dg(a, b, TN), (a, b)), lambda r, g: (_dg(r[1], g, NT), _dg(r[0], g, NN)))


def _params(sem=None):
    return pltpu.CompilerParams(dimension_semantics=sem, vmem_limit_bytes=VMEM_LIMIT)


def _sds(shape, dtype):
    return jax.ShapeDtypeStruct(tuple(shape), dtype)


TOKEN = pl.BlockSpec((8, 128), lambda *_: (0, 0))


def _mm(a, b, *, trans_b=False, out_dtype=F32, add=None, add_scale=1.0, tm=1024, tn=512, dep=(), name):
    a_list = list(a) if isinstance(a, (list, tuple)) else [a]
    M = a_list[0].shape[0]
    N = b.shape[0] if trans_b else b.shape[1]
    tm = min(tm, M)
    tn = next(t for t in (tn, 384, 256, 128) if N % t == 0)
    assert M % tm == 0
    Ka = a_list[0].shape[1]
    assert all(t.shape[1] == Ka for t in a_list)
    tk = next(t for t in (Ka, 1408, 1024) if Ka % t == 0 and t <= 1408)
    na, npa = len(a_list), Ka // tk
    nk = na * npa
    has_add = add is not None

    def body(*refs):
        a_refs, b_ref = refs[:na], refs[na]
        add_ref = refs[na + 1] if has_add else None
        o_ref, acc_ref = refs[-2], refs[-1]
        k = pl.program_id(2)

        def finish(r):
            if has_add:
                r = r + add_scale * add_ref[...].astype(F32)
            o_ref[...] = r.astype(out_dtype)

        for t, a_ref in enumerate(a_refs):
            def step(a_ref=a_ref):
                p = _dg(a_ref[...], b_ref[...], NT if trans_b else NN)
                if nk == 1:
                    finish(p)
                    return

                @pl.when(k == 0)
                def _():
                    acc_ref[...] = p

                @pl.when((k > 0) & (k < nk - 1))
                def _():
                    acc_ref[...] += p

                @pl.when(k == nk - 1)
                def _():
                    finish(acc_ref[...] + p)

            if na == 1:
                step()
            else:
                pl.when((k >= t * npa) & (k < (t + 1) * npa))(step)

    in_specs = [pl.BlockSpec((tm, tk), lambda i, j, k, t=t: (i, jnp.clip(k - t * npa, 0, npa - 1))) for t in range(na)]
    in_specs.append(pl.BlockSpec((tn, tk), lambda i, j, k: (j, k)) if trans_b else pl.BlockSpec((tk, tn), lambda i, j, k: (k, j)))
    args = a_list + [b]
    if has_add:
        in_specs.append(pl.BlockSpec((tm, tn), lambda i, j, k: (i, j)))
        args.append(add)
    in_specs += [TOKEN] * len(dep)
    args += list(dep)
    return pl.pallas_call(
        body, name=name, grid=(M // tm, N // tn, nk), in_specs=in_specs,
        out_specs=pl.BlockSpec((tm, tn), lambda i, j, k: (i, j)), out_shape=_sds((M, N), out_dtype),
        scratch_shapes=[pltpu.VMEM((tm, tn) if nk > 1 else (8, 128), F32)],
        compiler_params=_params(("parallel", "parallel", "arbitrary")))(*args)


def _wgrad(a_t, g, *, name):
    K, N = a_t.shape[0], g.shape[1]
    return _mm(a_t, g, tm=K if K <= 1024 else K // 2, tn=next(t for t in range(1536, 0, -128) if N % t == 0), name=name)


def _mm_tn(a, g, *, name):
    S, K = a.shape
    N = g.shape[1]
    tk = K if K <= 1024 else K // 2
    tn = next(t for t in range(1536, 0, -128) if N % t == 0)
    ts = min(512, S)
    ns = S // ts
    assert K % tk == 0 and S % ts == 0

    def body(a_ref, g_ref, o_ref):
        s = pl.program_id(2)
        p = _dg(a_ref[...], g_ref[...], TN)

        @pl.when(s == 0)
        def _():
            o_ref[...] = p

        @pl.when(s > 0)
        def _():
            o_ref[...] += p

    return pl.pallas_call(
        body, name=name, grid=(K // tk, N // tn, ns),
        in_specs=[pl.BlockSpec((ts, tk), lambda i, j, s: (s, i)), pl.BlockSpec((ts, tn), lambda i, j, s: (s, j))],
        out_specs=pl.BlockSpec((tk, tn), lambda i, j, s: (i, j)), out_shape=_sds((K, N), F32),
        compiler_params=_params(("parallel", "parallel", "arbitrary")))(a, g)


def _mm_res_ln(a_list, w, resid, gam, bet, *, name):
    M, Ka = a_list[0].shape
    na = len(a_list)
    assert w.shape[0] == na * Ka
    tm = min(512, M)

    def body(*refs):
        a_refs, w_refs = refs[:na], refs[na:2 * na]
        r_ref, g_ref, b_ref, y_ref, yb_ref, yt_ref, z_ref = refs[2 * na:]
        z = ALPHA * r_ref[...]
        for a_ref, w_ref in zip(a_refs, w_refs):
            z = z + _dg(a_ref[...], w_ref[...], NN)
        mu = jnp.mean(z, axis=1, keepdims=True)
        zc = z - mu
        var = jnp.mean(zc * zc, axis=1, keepdims=True)
        y = zc * lax.rsqrt(var + EPS) * g_ref[...] + b_ref[...]
        y_ref[...] = y
        yb_ref[...] = y.astype(BF16)
        yt_ref[...] = y.T.astype(BF16)
        z_ref[...] = z

    row = pl.BlockSpec((tm, D), lambda i: (i, 0))
    vec = pl.BlockSpec((1, D), lambda i: (0, 0))
    a_specs = [pl.BlockSpec((tm, Ka), lambda i: (i, 0)) for _ in a_list]
    w_specs = [pl.BlockSpec((Ka, D), lambda i, t=t: (t, 0)) for t in range(na)]
    return pl.pallas_call(
        body, name=name, grid=(M // tm,), in_specs=a_specs + w_specs + [row, vec, vec],
        out_specs=[row, row, pl.BlockSpec((D, tm), lambda i: (0, i)), row],
        out_shape=[_sds((M, D), F32), _sds((M, D), BF16), _sds((D, M), BF16), _sds((M, D), F32)],
        compiler_params=_params(("parallel",)))(*a_list, *([w] * na), resid, gam, bet)


def _grad_in(pairs, add, *, ln=None, loss=None, dep=(), name):
    M = (add if add is not None else loss[0]).shape[0]
    tm = min(512, M)
    npair, nd = len(pairs), len(dep)
    has_ln, has_loss = ln is not None, loss is not None

    def body(*refs):
        n_in = 2 * npair + (2 if has_loss else 1) + 2 * has_ln + nd
        ins, outs = refs[:n_in], refs[n_in:]
        i = pl.program_id(0)
        pos = 2 * npair
        if has_loss:
            e = ins[pos][...] - ins[pos + 1][...]
            pos += 2
            dy = e * (1.0 / D)
            part = 0.5 * jnp.sum(jnp.sum(e * e, axis=1, keepdims=True) * (1.0 / D), axis=0, keepdims=True)
        else:
            dy = ALPHA * ins[pos][...]
            pos += 1
            for t in range(npair):
                dy = dy + _dg(ins[2 * t][...], ins[2 * t + 1][...], NT)
        if not has_ln:
            outs[0][...] = dy
            return
        z, g_ref = ins[pos][...], ins[pos + 1]
        mu = jnp.mean(z, axis=1, keepdims=True)
        zc = z - mu
        var = jnp.mean(zc * zc, axis=1, keepdims=True)
        rstd = lax.rsqrt(var + EPS)
        xh = zc * rstd
        dxh = dy * g_ref[...]
        m1 = jnp.mean(dxh, axis=1, keepdims=True)
        m2 = jnp.mean(dxh * xh, axis=1, keepdims=True)
        dz = rstd * (dxh - m1 - xh * m2)
        outs[0][...] = dz
        outs[1][...] = dz.astype(BF16)
        acc = [(outs[2], jnp.sum(dy * xh, axis=0, keepdims=True)), (outs[3], jnp.sum(dy, axis=0, keepdims=True))]
        if has_loss:
            acc.append((outs[4], jnp.broadcast_to(part, (8, 128))))

        @pl.when(i == 0)
        def _():
            for ref, val in acc:
                ref[...] = val

        @pl.when(i > 0)
        def _():
            for ref, val in acc:
                ref[...] += val

    row = pl.BlockSpec((tm, D), lambda i: (i, 0))
    vec = pl.BlockSpec((1, D), lambda i: (0, 0))
    in_specs, args = [], []
    for a, b, blk in pairs:
        in_specs += [pl.BlockSpec((tm, a.shape[1]), lambda i: (i, 0)), pl.BlockSpec((D, a.shape[1]), lambda i, blk=blk: (0, blk))]
        args += [a, b]
    if has_loss:
        in_specs += [row, row]
        args += list(loss)
    else:
        in_specs.append(row)
        args.append(add)
    if has_ln:
        in_specs += [row, vec]
        args += list(ln)
    in_specs += [TOKEN] * nd
    args += list(dep)
    if has_ln:
        out_specs = [row, row, vec, vec] + ([pl.BlockSpec((8, 128), lambda i: (0, 0))] if has_loss else [])
        out_shape = [_sds((M, D), F32), _sds((M, D), BF16), _sds((1, D), F32), _sds((1, D), F32)] + ([_sds((8, 128), F32)] if has_loss else [])
    else:
        out_specs, out_shape = row, _sds((M, D), F32)
    return pl.pallas_call(
        body, name=name, grid=(M // tm,), in_specs=in_specs, out_specs=out_specs, out_shape=out_shape,
        compiler_params=_params(("arbitrary",) if has_ln else ("parallel",)))(*args)


def _shift_down(x, d):
    if d == 0:
        return x
    rows = lax.broadcasted_iota(jnp.int32, x.shape, 0)
    return jnp.where(rows >= d, pltpu.roll(x, d, 0), 0.0)


def _shift_up(x, d):
    if d == 0:
        return x
    S = x.shape[0]
    rows = lax.broadcasted_iota(jnp.int32, x.shape, 0)
    return jnp.where(rows < S - d, pltpu.roll(x, S - d, 0), 0.0)


def _conv(x, w_ref, b_ref, cs, K):
    y = b_ref[:, cs]
    for j in range(K):
        y = y + _shift_down(x, K - 1 - j) * w_ref[j:j + 1, cs]
    return y


def _conv_bwd(dy, x, w_ref, dw_ref, db_ref, cs, K):
    dx = jnp.zeros_like(x)
    for j in range(K):
        sdy = _shift_up(dy, K - 1 - j)
        dx = dx + sdy * w_ref[j:j + 1, cs]
        dw_ref[j:j + 1, cs] = jnp.sum(sdy * x, axis=0, keepdims=True)
    db_ref[:, cs] = jnp.sum(dy, axis=0, keepdims=True)
    return dx


ALL = slice(None)


def _silu_conv_fwd(proj, cw, cb, *, name):
    S = proj.shape[0]

    def body(x_ref, w_ref, b_ref, o_ref):
        o_ref[...] = jax.nn.silu(_conv(x_ref[...], w_ref, b_ref, ALL, ML_CONV))

    col = pl.BlockSpec((S, 128), lambda j: (0, j))
    return pl.pallas_call(
        body, name=name, grid=(8,),
        in_specs=[col, pl.BlockSpec((ML_CONV, 128), lambda j: (0, j)), pl.BlockSpec((1, 128), lambda j: (0, j))],
        out_specs=col, out_shape=_sds((S, 2 * ML_W), F32), compiler_params=_params(("parallel",)))(proj, cw, cb)


def _silu_conv_bwd(dqk, proj, cw, cb, *, name):
    S = proj.shape[0]

    def body(d_ref, x_ref, w_ref, b_ref, dx_ref, dw_ref, db_ref):
        x = x_ref[...]
        y = _conv(x, w_ref, b_ref, ALL, ML_CONV)
        dy = jax.vjp(jax.nn.silu, y)[1](d_ref[...])[0]
        dx_ref[...] = _conv_bwd(dy, x, w_ref, dw_ref, db_ref, ALL, ML_CONV).astype(BF16)

    col = pl.BlockSpec((S, 128), lambda j: (0, j))
    wsp = pl.BlockSpec((ML_CONV, 128), lambda j: (0, j))
    bsp = pl.BlockSpec((1, 128), lambda j: (0, j))
    return pl.pallas_call(
        body, name=name, grid=(8,), in_specs=[col, col, wsp, bsp], out_specs=[col, wsp, bsp],
        out_shape=[_sds((S, 2 * ML_W), BF16), _sds((ML_CONV, 2 * ML_W), F32), _sds((1, 2 * ML_W), F32)],
        compiler_params=_params(("parallel",)))(dqk, proj, cw, cb)


GELU_C0 = math.sqrt(2.0 / math.pi)
GELU_C1 = 0.044715


def _gate_bwd(ug, uv, da):
    t = jnp.tanh(GELU_C0 * (ug + GELU_C1 * (ug * ug * ug)))
    half = 0.5 * (1.0 + t)
    dgelu = half + 0.5 * ug * (1.0 - t * t) * (GELU_C0 * (1.0 + 3.0 * GELU_C1 * (ug * ug)))
    return da * uv * dgelu, da * (ug * half)


def _up_pair(h_ref, ugw_ref, uvw_ref):
    return _dg(h_ref[...], jnp.concatenate([ugw_ref[...], uvw_ref[...]], axis=1), NN)


def _ffn_specs(S):
    return (pl.BlockSpec((D, 128), lambda j: (0, j)), pl.BlockSpec((D, 128), lambda j: (0, j + NB_FF)),
            pl.BlockSpec((FFN_CONV, 128), lambda j: (0, j)), pl.BlockSpec((FFN_CONV, 128), lambda j: (0, j + NB_FF)),
            pl.BlockSpec((1, 128), lambda j: (0, j)), pl.BlockSpec((1, 128), lambda j: (0, j + NB_FF)))


def _ffn_gate_fwd(hb, w_up, cw, cb, *, name):
    S = hb.shape[0]
    ug_, uv_, wg, wv, bg, bv = _ffn_specs(S)

    def body(h_ref, ugw_ref, uvw_ref, wg_ref, wv_ref, bg_ref, bv_ref, o_ref, ot_ref, h_s):
        @pl.when(pl.program_id(0) == 0)
        def _():
            pltpu.sync_copy(h_ref, h_s)

        x2 = _up_pair(h_s, ugw_ref, uvw_ref)
        ug = _conv(x2[:, :128], wg_ref, bg_ref, ALL, FFN_CONV)
        uv = _conv(x2[:, 128:], wv_ref, bv_ref, ALL, FFN_CONV)
        act = jax.nn.gelu(ug) * uv
        o_ref[...] = act.astype(BF16)
        ot_ref[...] = act.T.astype(BF16)

    return pl.pallas_call(
        body, name=name, grid=(NB_FF,), in_specs=[ANY, ug_, uv_, wg, wv, bg, bv],
        out_specs=[pl.BlockSpec((S, 128), lambda j: (0, j)), pl.BlockSpec((128, S), lambda j: (j, 0))],
        out_shape=[_sds((S, DFF), BF16), _sds((DFF, S), BF16)],
        scratch_shapes=[pltpu.VMEM((S, D), BF16)],
        compiler_params=_params(("arbitrary",)))(hb, w_up, w_up, cw, cw, cb, cb)


def _ffn_gate_bwd(dzb, w_down, hb, w_up, cw, cb, *, name):
    S = hb.shape[0]
    ug_, uv_, wg, wv, bg, bv = _ffn_specs(S)

    def body(dz_ref, h_ref, wd_ref, ugw_ref, uvw_ref, wg_ref, wv_ref, bg_ref, bv_ref,
             dxg_ref, dxv_ref, dwg_ref, dwv_ref, dbg_ref, dbv_ref, dz_s, h_s):
        @pl.when(pl.program_id(0) == 0)
        def _():
            pltpu.sync_copy(dz_ref, dz_s)
            pltpu.sync_copy(h_ref, h_s)

        x2 = _up_pair(h_s, ugw_ref, uvw_ref)
        xg, xv = x2[:, :128], x2[:, 128:]
        ug = _conv(xg, wg_ref, bg_ref, ALL, FFN_CONV)
        uv = _conv(xv, wv_ref, bv_ref, ALL, FFN_CONV)
        dug, duv = _gate_bwd(ug, uv, _dg(dz_s[...], wd_ref[...], NT))
        dxg_ref[...] = _conv_bwd(dug, xg, wg_ref, dwg_ref, dbg_ref, ALL, FFN_CONV).astype(BF16)
        dxv_ref[...] = _conv_bwd(duv, xv, wv_ref, dwv_ref, dbv_ref, ALL, FFN_CONV).astype(BF16)

    col = pl.BlockSpec((S, 128), lambda j: (0, j))
    half = _sds((S, DFF), BF16)
    dxg, dxv, dwg, dwv, dbg, dbv = pl.pallas_call(
        body, name=name, grid=(NB_FF,),
        in_specs=[ANY, ANY, pl.BlockSpec((128, D), lambda j: (j, 0)), ug_, uv_, wg, wv, bg, bv],
        out_specs=[col, col, wg, wg, bg, bg],
        out_shape=[half, half, _sds((FFN_CONV, DFF), F32), _sds((FFN_CONV, DFF), F32), _sds((1, DFF), F32), _sds((1, DFF), F32)],
        scratch_shapes=[pltpu.VMEM((S, D), BF16), pltpu.VMEM((S, D), BF16)],
        compiler_params=_params(("arbitrary",)))(dzb, hb, w_down, w_up, w_up, cw, cw, cb, cb)
    return dxg, dxv, jnp.concatenate([dwg, dwv], axis=1), jnp.concatenate([dbg, dbv], axis=1)


def _log_sigmoid(x):
    return jnp.minimum(x, 0.0) - jnp.log1p(jnp.exp(-jnp.abs(x)))


@jax.custom_vjp
def _clamp_div(num, den, floor, shift):
    return num / jnp.maximum(jnp.abs(den), floor)


def _clamp_div_fwd(num, den, floor, shift):
    out = num / jnp.maximum(jnp.abs(den), floor)
    return out, (den, floor, out)


def _clamp_div_bwd(res, g):
    den, floor, out = res
    active = jnp.abs(den) < floor
    dinv = jnp.maximum(jnp.abs(den), floor)
    go = jnp.sum(g * out, axis=-1, keepdims=True)
    ddiv = -go / dinv
    return (g / dinv, jnp.where(active, 0.0, ddiv * jnp.sign(den)), jnp.where(active, ddiv, 0.0),
            jnp.sum(jnp.where(active, go, 0.0), axis=-2, keepdims=True))


_clamp_div.defvjp(_clamp_div_fwd, _clamp_div_bwd)


def _ml_heads(q, k, v, o_pre, gates, gbias, C, n, ng, m, shift):
    H, L, _ = q.shape
    lane1 = lax.broadcasted_iota(jnp.int32, (1, 128), 1)
    gz = gates + jnp.where(lane1 < ML_H, lax.stop_gradient(gbias), gbias)
    gz = jnp.broadcast_to(gz[None], (H, L, 128))
    hid = lax.broadcasted_iota(jnp.int32, (H, L, 128), 0)
    lane = lax.broadcasted_iota(jnp.int32, (H, L, 128), 2)
    ig = jnp.sum(jnp.where(lane == hid, gz, 0.0), axis=2, keepdims=True)
    lf = _log_sigmoid(jnp.sum(jnp.where(lane == ML_H + hid, gz, 0.0), axis=2, keepdims=True))
    r = lax.broadcasted_iota(jnp.int32, (H, L, L), 1)
    c = lax.broadcasted_iota(jnp.int32, (H, L, L), 2)
    eye, tril = r == c, r >= c

    def to_row(col):
        return jnp.sum(jnp.where(eye, col, 0.0), axis=1, keepdims=True)

    b_col = jnp.sum(jnp.where(tril, to_row(lf), 0.0), axis=2, keepdims=True)
    Dm = jnp.where(tril, b_col - to_row(b_col) + to_row(ig), -jnp.inf)
    inter = b_col + m
    m_t = lax.stop_gradient(jnp.maximum(inter, jnp.max(Dm, axis=2, keepdims=True)))
    w_inter = jnp.exp(inter - m_t)
    ks = k * (ML_DH ** -0.5)
    s = dot_nt(q, ks) * jnp.exp(Dm - m_t)
    num = w_inter * dot_nn(q, C) + dot_nn(s, v)
    den = w_inter * jnp.sum(q * n, axis=2, keepdims=True) + jnp.sum(s, axis=2, keepdims=True)
    h = _clamp_div(num, den, jnp.exp(-m_t), shift)
    g = jnp.sum(lf, axis=1, keepdims=True)
    a = g - b_col + ig
    m_new = lax.stop_gradient(jnp.maximum(g + m, jnp.max(a, axis=1, keepdims=True)))
    decay = jnp.exp(g + m - m_new)
    wk = jnp.exp(a - m_new)
    C_new = decay * C + dot_tn(ks * wk, v)
    n_new = decay * n + jnp.sum(wk * ks, axis=1, keepdims=True)
    mu = jnp.mean(h, axis=2, keepdims=True)
    hc = h - mu
    var = jnp.mean(hc * hc, axis=2, keepdims=True)
    out = jax.nn.sigmoid(o_pre) * (hc * lax.rsqrt(var + EPS) * ng)
    return out, C_new, n_new, m_new


def _hs(h, off=0):
    return slice(off + h * ML_DH, off + (h + 1) * ML_DH)


def _heads(ref, off=0):
    return jnp.stack([ref[:, _hs(h, off)] for h in range(ML_H)])


def _mlstm_fwd(qk, proj, gbias, ng, *, name):
    S = qk.shape[0]
    nc = S // ML_L

    def body(q_ref, k_ref, v_ref, o_ref, g_ref, gb_ref, ng_ref, h_ref, cs_ref, ns_ref, ms_ref, c_s, n_s, m_s):
        @pl.when(pl.program_id(0) == 0)
        def _():
            c_s[...] = jnp.zeros_like(c_s)
            n_s[...] = jnp.zeros_like(n_s)
            m_s[...] = jnp.zeros_like(m_s)

        C, n = c_s[...], n_s[...]
        cs_ref[0] = C
        ns_ref[0] = n
        ms_ref[0] = m_s[...]
        out, C2, n2, m2 = _ml_heads(_heads(q_ref), _heads(k_ref), _heads(v_ref), _heads(o_ref), g_ref[...], gb_ref[...], C, n,
                                    _heads(ng_ref), m_s[:, :, 0:1], jnp.zeros((ML_H, 1, 1), F32))
        for h in range(ML_H):
            h_ref[:, _hs(h)] = out[h].astype(BF16)
        c_s[...] = C2
        n_s[...] = n2
        m_s[...] = jnp.broadcast_to(m2, (ML_H, 1, 128))

    def w(j):
        return pl.BlockSpec((ML_L, ML_W), lambda c, j=j: (c, j))

    return pl.pallas_call(
        body, name=name, grid=(nc,),
        in_specs=[w(0), w(1), w(2), w(3), pl.BlockSpec((ML_L, 128), lambda c: (c, 22)),
                  pl.BlockSpec((1, 128), lambda c: (0, 0)), pl.BlockSpec((1, ML_W), lambda c: (0, 0))],
        out_specs=[w(0), pl.BlockSpec((1, ML_H, ML_DH, ML_DH), lambda c: (c, 0, 0, 0)),
                   pl.BlockSpec((1, ML_H, 1, 128), lambda c: (c, 0, 0, 0)), pl.BlockSpec((1, ML_H, 1, 128), lambda c: (c, 0, 0, 0))],
        out_shape=[_sds((S, ML_W), BF16), _sds((nc, ML_H, ML_DH, ML_DH), F32), _sds((nc, ML_H, 1, 128), F32),
                   _sds((nc, ML_H, 1, 128), F32)],
        scratch_shapes=[pltpu.VMEM((ML_H, ML_DH, ML_DH), F32), pltpu.VMEM((ML_H, 1, 128), F32), pltpu.VMEM((ML_H, 1, 128), F32)],
        compiler_params=_params(("arbitrary",)))(qk, qk, proj, proj, proj, gbias, ng)


def _mlstm_bwd(dh, qk, proj, gbias, ng, cs, ns, ms, *, name):
    S = qk.shape[0]
    nc = S // ML_L

    def body(dh_ref, q_ref, k_ref, v_ref, o_ref, g_ref, gb_ref, ng_ref, cs_ref, ns_ref, ms_ref,
             dqk_ref, dml_ref, dgb_ref, dng_ref, dc_s, dn_s):
        @pl.when(pl.program_id(0) == 0)
        def _():
            dc_s[...] = jnp.zeros_like(dc_s)
            dn_s[...] = jnp.zeros_like(dn_s)
            dgb_ref[...] = jnp.zeros_like(dgb_ref)
            dng_ref[...] = jnp.zeros_like(dng_ref)

        m = ms_ref[0][:, :, 0:1]

        def f(q, k, v, o_pre, gates, gb, C, n, ng_, shift):
            return _ml_heads(q, k, v, o_pre, gates, gb, C, n, ng_, m, shift)[:3]

        _, vjp = jax.vjp(f, _heads(q_ref), _heads(k_ref), _heads(v_ref), _heads(o_ref), g_ref[...], gb_ref[...],
                         cs_ref[0], ns_ref[0], _heads(ng_ref), jnp.zeros((ML_H, 1, 1), F32))
        dq, dk, dv, do, dgates, dgb, dC, dn, dng, dshift = vjp((_heads(dh_ref), dc_s[...], dn_s[...]))
        lane1 = lax.broadcasted_iota(jnp.int32, (1, 128), 1)
        for h in range(ML_H):
            dqk_ref[:, _hs(h)] = dq[h]
            dqk_ref[:, _hs(h, ML_W)] = dk[h]
            dml_ref[:, _hs(h)] = dv[h].astype(BF16)
            dml_ref[:, _hs(h, ML_W)] = do[h].astype(BF16)
            dng_ref[:, _hs(h)] += dng[h]
            dgb = jnp.where(lane1 == h, dshift[h], dgb)
        dc_s[...] = dC
        dn_s[...] = dn
        dml_ref[:, 2 * ML_W:] = dgates.astype(BF16)
        dgb_ref[...] += dgb

    def w(j):
        return pl.BlockSpec((ML_L, ML_W), lambda c, j=j: (nc - 1 - c, j))

    vec = pl.BlockSpec((1, 128), lambda c: (0, 0))
    vecw = pl.BlockSpec((1, ML_W), lambda c: (0, 0))
    gsp = pl.BlockSpec((ML_L, 128), lambda c: (nc - 1 - c, 22))
    st = pl.BlockSpec((1, ML_H, 1, 128), lambda c: (nc - 1 - c, 0, 0, 0))
    return pl.pallas_call(
        body, name=name, grid=(nc,),
        in_specs=[w(0), w(0), w(1), w(2), w(3), gsp, vec, vecw,
                  pl.BlockSpec((1, ML_H, ML_DH, ML_DH), lambda c: (nc - 1 - c, 0, 0, 0)), st, st],
        out_specs=[pl.BlockSpec((ML_L, 2 * ML_W), lambda c: (nc - 1 - c, 0)), pl.BlockSpec((ML_L, ML_GW), lambda c: (nc - 1 - c, 0)),
                   vec, vecw],
        out_shape=[_sds((S, 2 * ML_W), F32), _sds((S, ML_GW), BF16), _sds((1, 128), F32), _sds((1, ML_W), F32)],
        scratch_shapes=[pltpu.VMEM((ML_H, ML_DH, ML_DH), F32), pltpu.VMEM((ML_H, 1, 128), F32)],
        compiler_params=_params(("arbitrary",)))(dh, qk, qk, proj, proj, proj, gbias, ng, cs, ns, ms)


def _t5_buckets():
    r = np.arange(BLK)[:, None]
    c = np.arange(2 * BLK)[None, :]
    n = np.maximum(BLK + r - c, 0)
    max_exact = REL_B // 2
    nf = np.maximum(n, 1).astype(np.float32)
    large = max_exact + (np.log(nf / np.float32(max_exact)) / np.float32(math.log(REL_MAXD / max_exact))
                         * np.float32(REL_B - max_exact)).astype(np.int32)
    large = np.minimum(large, REL_B - 1)
    return np.where(n < max_exact, n, large).astype(np.int32)


def _bias_table(rel_bias, bucket, *, name):
    def body(rb_ref, bk_ref, o_ref):
        bk = bk_ref[...]
        for h in range(SW_H):
            acc = jnp.zeros((BLK, 2 * BLK), F32)
            for b in range(REL_B):
                acc = jnp.where(bk == b, rb_ref[b, h], acc)
            o_ref[h] = acc

    return pl.pallas_call(
        body, name=name, in_specs=[pl.BlockSpec(memory_space=pltpu.SMEM), pl.BlockSpec(memory_space=pltpu.VMEM)],
        out_specs=pl.BlockSpec(memory_space=pltpu.VMEM), out_shape=_sds((SW_H, BLK, 2 * BLK), F32),
        compiler_params=_params())(rel_bias, bucket)


def _bias_table_bwd(dbias_list, bucket, *, name):
    nl = len(dbias_list)

    def body(*refs):
        d_refs, bk_ref, o_ref = refs[:nl], refs[nl], refs[nl + 1]
        bk = bk_ref[...]
        rows = lax.broadcasted_iota(jnp.int32, (REL_B, 128), 0)
        lanes = lax.broadcasted_iota(jnp.int32, (REL_B, 128), 1)
        acc = jnp.zeros((REL_B, 128), F32)
        for h in range(SW_H):
            d = d_refs[0][h]
            for d_ref in d_refs[1:]:
                d = d + d_ref[h]
            for b in range(REL_B):
                t = jnp.sum(jnp.sum(jnp.where(bk == b, d, 0.0), axis=0, keepdims=True), axis=1, keepdims=True)
                acc = jnp.where((rows == b) & (lanes == h), t, acc)
        o_ref[...] = acc

    vm = pl.BlockSpec(memory_space=pltpu.VMEM)
    return pl.pallas_call(
        body, name=name, in_specs=[vm] * (nl + 1), out_specs=vm, out_shape=_sds((REL_B, 128), F32),
        compiler_params=_params())(*dbias_list, bucket)


def _swa_heads(q, kp, kc, vp, vc, bp, bc, sinks, has_prev):
    def rep(t):
        return jnp.concatenate([t[g:g + 1] for g in range(SW_H // SW_G) for _ in range(SW_G)], axis=0)

    r = lax.broadcasted_iota(jnp.int32, (SW_H, BLK, BLK), 1)
    c = lax.broadcasted_iota(jnp.int32, (SW_H, BLK, BLK), 2)
    hid = lax.broadcasted_iota(jnp.int32, (SW_H, 1, 128), 0)
    lane = lax.broadcasted_iota(jnp.int32, (SW_H, 1, 128), 2)
    sink = jnp.sum(jnp.where(lane == hid, jnp.broadcast_to(sinks[None], (SW_H, 1, 128)), 0.0), axis=2, keepdims=True)
    lp = jnp.where((c > r) & has_prev, dot_nt(q, rep(kp)) * (SW_DH ** -0.5) + bp, -jnp.inf)
    lc = jnp.where(c <= r, dot_nt(q, rep(kc)) * (SW_DH ** -0.5) + bc, -jnp.inf)
    mx = jnp.maximum(jnp.maximum(jnp.max(lp, axis=2, keepdims=True), jnp.max(lc, axis=2, keepdims=True)), sink)
    mx = lax.stop_gradient(mx)
    pp, pc = jnp.exp(lp - mx), jnp.exp(lc - mx)
    den = jnp.sum(pp, axis=2, keepdims=True) + jnp.sum(pc, axis=2, keepdims=True) + jnp.exp(sink - mx)
    return dot_nn(pp / den, rep(vp)) + dot_nn(pc / den, rep(vc))


def _qs(h, off=0):
    return slice(off + h * SW_DH, off + (h + 1) * SW_DH)


def _split(ref, n):
    return jnp.stack([ref[:, _qs(h)] for h in range(n)])


def _swa_fwd(proj, bias, sinks, *, name):
    S = proj.shape[0]
    nb = S // BLK
    nkv = SW_H // SW_G

    def body(q_ref, kp_ref, kc_ref, vp_ref, vc_ref, b_ref, s_ref, o_ref):
        out = _swa_heads(_split(q_ref, SW_H), _split(kp_ref, nkv), _split(kc_ref, nkv), _split(vp_ref, nkv), _split(vc_ref, nkv),
                         b_ref[:, :, :BLK], b_ref[:, :, BLK:], s_ref[...], pl.program_id(0) > 0)
        for h in range(SW_H):
            o_ref[:, _qs(h)] = out[h].astype(BF16)

    def cur(j):
        return pl.BlockSpec((BLK, 128), lambda n, j=j: (n, j))

    def prev(j):
        return pl.BlockSpec((BLK, 128), lambda n, j=j: (jnp.maximum(n - 1, 0), j))

    return pl.pallas_call(
        body, name=name, grid=(nb,),
        in_specs=[pl.BlockSpec((BLK, SW_W), lambda n: (n, 4)), prev(20), cur(20), prev(21), cur(21),
                  pl.BlockSpec((SW_H, BLK, 2 * BLK), lambda n: (0, 0, 0)), pl.BlockSpec((1, 128), lambda n: (0, 0))],
        out_specs=pl.BlockSpec((BLK, SW_W), lambda n: (n, 0)), out_shape=_sds((S, SW_W), BF16),
        compiler_params=_params(("parallel",)))(proj, proj, proj, proj, proj, bias, sinks)


def _swa_bwd(dh, proj, bias, sinks, *, name):
    S = proj.shape[0]
    nb = S // BLK

    nkv = SW_H // SW_G

    def body(dh_ref, q_ref, kp_ref, kc_ref, vp_ref, vc_ref, b_ref, s_ref, dsw_ref, db_ref, ds_ref, ck_s, cv_s):
        i = pl.program_id(0)

        @pl.when(i == 0)
        def _():
            ck_s[...] = jnp.zeros_like(ck_s)
            cv_s[...] = jnp.zeros_like(cv_s)
            db_ref[...] = jnp.zeros_like(db_ref)
            ds_ref[...] = jnp.zeros_like(ds_ref)

        f = functools.partial(_swa_heads, has_prev=i < nb - 1)
        _, vjp = jax.vjp(f, _split(q_ref, SW_H), _split(kp_ref, nkv), _split(kc_ref, nkv), _split(vp_ref, nkv),
                         _split(vc_ref, nkv), b_ref[:, :, :BLK], b_ref[:, :, BLK:], s_ref[...])
        dq, dkp, dkc, dvp, dvc, dbp, dbc, ds = vjp(_split(dh_ref, SW_H))
        for h in range(SW_H):
            dsw_ref[:, _qs(h)] = dq[h].astype(BF16)
        for g in range(nkv):
            dsw_ref[:, _qs(g, SW_W)] = (dkc[g] + ck_s[:, _qs(g)]).astype(BF16)
            dsw_ref[:, _qs(g, SW_W + SW_KVW)] = (dvc[g] + cv_s[:, _qs(g)]).astype(BF16)
            ck_s[:, _qs(g)] = dkp[g]
            cv_s[:, _qs(g)] = dvp[g]
        db_ref[:, :, :BLK] += dbp
        db_ref[:, :, BLK:] += dbc
        ds_ref[...] += ds

    def cur(j):
        return pl.BlockSpec((BLK, 128), lambda i, j=j: (nb - 1 - i, j))

    def prev(j):
        return pl.BlockSpec((BLK, 128), lambda i, j=j: (jnp.maximum(nb - 2 - i, 0), j))

    bsp = pl.BlockSpec((SW_H, BLK, 2 * BLK), lambda i: (0, 0, 0))
    vec = pl.BlockSpec((1, 128), lambda i: (0, 0))
    return pl.pallas_call(
        body, name=name, grid=(nb,),
        in_specs=[pl.BlockSpec((BLK, SW_W), lambda i: (nb - 1 - i, 1)), pl.BlockSpec((BLK, SW_W), lambda i: (nb - 1 - i, 4)),
                  prev(20), cur(20), prev(21), cur(21), bsp, vec],
        out_specs=[pl.BlockSpec((BLK, SW_GW), lambda i: (nb - 1 - i, 0)), bsp, vec],
        out_shape=[_sds((S, SW_GW), BF16), _sds((SW_H, BLK, 2 * BLK), F32), _sds((1, 128), F32)],
        scratch_shapes=[pltpu.VMEM((BLK, 128), F32)] * 2,
        compiler_params=_params(("arbitrary",)))(dh, proj, proj, proj, proj, proj, bias, sinks)


XA_TM = 512


def _xa_head(qh, kh, vh):
    logits = dot_nt(qh, kh) * (XA_DH ** -0.5)
    mx = lax.stop_gradient(jnp.max(logits, axis=1, keepdims=True))
    e = jnp.exp(logits - mx)
    return dot_nn(e / jnp.sum(e, axis=1, keepdims=True), vh)


def _xs(h, off=0):
    return slice(off + h * XA_DH, off + (h + 1) * XA_DH)


def _xattn_fwd(hb, wq, kv, *, name):
    S = hb.shape[0]
    M = kv.shape[0]

    def body(h_ref, wq_ref, kv_ref, o_ref):
        q = _dg(h_ref[...], wq_ref[...], NN)
        for h in range(XA_H):
            o_ref[:, _xs(h)] = _xa_head(q[:, _xs(h)], kv_ref[:, _xs(h)], kv_ref[:, _xs(h, D)]).astype(BF16)

    tm = min(XA_TM, S)
    row = pl.BlockSpec((tm, D), lambda i: (i, 0))
    return pl.pallas_call(
        body, name=name, grid=(S // tm,),
        in_specs=[row, pl.BlockSpec((D, D), lambda i: (0, 0)), pl.BlockSpec((M, 2 * D), lambda i: (0, 0))], out_specs=row,
        out_shape=_sds((S, D), BF16), compiler_params=_params(("parallel",)))(hb, wq, kv)


def _xattn_bwd(dzb, wo, hb, wq, kv, *, name):
    S = hb.shape[0]
    M = kv.shape[0]

    def body(dz_ref, wo_ref, h_ref, wq_ref, kv_ref, dq_ref, dkv_ref):
        @pl.when(pl.program_id(0) == 0)
        def _():
            dkv_ref[...] = jnp.zeros_like(dkv_ref)

        q = _dg(h_ref[...], wq_ref[...], NN)
        do = _dg(dz_ref[...], wo_ref[...], NT)
        for h in range(XA_H):
            _, vjp = jax.vjp(_xa_head, q[:, _xs(h)], kv_ref[:, _xs(h)], kv_ref[:, _xs(h, D)])
            dq, dk, dv = vjp(do[:, _xs(h)])
            dq_ref[:, _xs(h)] = dq.astype(BF16)
            dkv_ref[:, _xs(h)] += dk
            dkv_ref[:, _xs(h, D)] += dv

    tm = min(XA_TM, S)
    row = pl.BlockSpec((tm, D), lambda i: (i, 0))
    full = pl.BlockSpec((M, 2 * D), lambda i: (0, 0))
    sq = pl.BlockSpec((D, D), lambda i: (0, 0))
    return pl.pallas_call(
        body, name=name, grid=(S // tm,), in_specs=[row, sq, row, sq, full],
        out_specs=[row, full], out_shape=[_sds((S, D), BF16), _sds((M, 2 * D), F32)],
        compiler_params=_params(("arbitrary",)))(dzb, wo, hb, wq, kv)


ANY = pl.BlockSpec(memory_space=pl.ANY)


def _place():
    x, y, c = lax.axis_index("x"), lax.axis_index("y"), lax.axis_index("c")
    chips = [(1 - x, y), (x, 1 - y), (1 - x, 1 - y)]
    return x, y, c, chips


def _gather(arrs, *, name):
    n = len(arrs)

    def body(*refs):
        srcs, outs = refs[:n], refs[n:2 * n]
        send_sems, recv_sems, local_sems = refs[2 * n:]
        x, y, c, chips = _place()
        me, sib = (x, y, c), (x, y, 1 - c)

        def idx(p):
            return 4 * p[0] + 2 * p[1] + p[2]

        def copy(i, k, block, to, from_src=False):
            return pltpu.make_async_remote_copy(
                src_ref=srcs[i] if from_src else outs[i].at[idx(block)], dst_ref=outs[i].at[idx(block)],
                send_sem=send_sems.at[7 * i + k], recv_sem=recv_sems.at[7 * i + k], device_id=to, device_id_type=MESH)

        local = [pltpu.make_async_copy(srcs[i], outs[i].at[idx(me)], local_sems.at[i]) for i in range(n)]
        for cp in local:
            cp.start()
        first = []
        for i in range(n):
            first.append(copy(i, 0, me, sib, True))
            first += [copy(i, 1 + j, me, (*chip, c), True) for j, chip in enumerate(chips)]
        for cp in first:
            cp.start()
        passed = []
        for j, chip in enumerate(chips):
            for i in range(n):
                copy(i, 1 + j, (*chip, c), me).wait_recv()
                cp = copy(i, 4 + j, (*chip, c), sib)
                cp.start()
                passed.append(cp)
        for i in range(n):
            copy(i, 0, sib, me).wait_recv()
        for j, chip in enumerate(chips):
            for i in range(n):
                copy(i, 4 + j, (*chip, 1 - c), me).wait_recv()
        for cp in first + passed:
            cp.wait_send()
        for cp in local:
            cp.wait()

    return pl.pallas_call(
        body, name=name, in_specs=[ANY] * n, out_specs=[ANY] * n,
        out_shape=[_sds((N_DEV,) + a.shape, a.dtype) for a in arrs],
        scratch_shapes=[pltpu.SemaphoreType.DMA((7 * n,)), pltpu.SemaphoreType.DMA((7 * n,)), pltpu.SemaphoreType.DMA((n,))],
        compiler_params=pltpu.CompilerParams(has_side_effects=True))(*arrs)


HBM = pl.BlockSpec(memory_space=pltpu.HBM)
SEM = pl.BlockSpec(memory_space=pltpu.SEMAPHORE)
EFFECT = pltpu.SideEffectType.DATAFLOW_SIDE_EFFECTING


def _near_copies(srcs, lands, send_sems, recv_sems):
    x, y, c, chips = _place()
    me = 4 * x + 2 * y + c
    out = []
    for i in range(len(srcs)):
        for k, (px, py, pc) in enumerate([(x, y, 1 - c)] + [(*chip, c) for chip in chips]):
            out.append(tuple(pltpu.make_async_remote_copy(
                src_ref=srcs[i], dst_ref=lands[i].at[slot], send_sem=send_sems.at[4 * i + k], recv_sem=recv_sems.at[4 * i + k],
                device_id=(px, py, pc), device_id_type=MESH) for slot in (me, 4 * px + 2 * py + pc)))
    return out


def _forward_sibling(lands, *, name):
    n = len(lands)

    def body(*refs):
        bufs = refs[n:2 * n]
        send_sems, recv_sems = refs[2 * n:]
        x, y, c, chips = _place()
        copies = [tuple(pltpu.make_async_remote_copy(
            src_ref=bufs[i].at[4 * chip[0] + 2 * chip[1] + c], dst_ref=bufs[i].at[4 * chip[0] + 2 * chip[1] + cc],
            send_sem=send_sems.at[3 * i + j], recv_sem=recv_sems.at[3 * i + j], device_id=(x, y, 1 - c), device_id_type=MESH)
            for cc in (c, 1 - c)) for i in range(n) for j, chip in enumerate(chips)]
        for send, _ in copies:
            send.start()
        for send, recv in copies:
            send.wait_send()
            recv.wait_recv()

    return pl.pallas_call(
        body, name=name, in_specs=[ANY] * n, out_specs=[ANY] * n, out_shape=[_sds(a.shape, a.dtype) for a in lands],
        input_output_aliases={i: i for i in range(n)},
        scratch_shapes=[pltpu.SemaphoreType.DMA((3 * n,)), pltpu.SemaphoreType.DMA((3 * n,))],
        compiler_params=pltpu.CompilerParams(has_side_effects=True))(*lands)


def _gather_copies(srcs, lands, send_sems, recv_sems):
    return _scatter_copies(srcs, lands, send_sems, recv_sems, whole=True)


def _scatter_copies(srcs, lands, send_sems, recv_sems, whole=False):
    x, y, c, _ = _place()
    me = 4 * x + 2 * y + c
    out = []
    for i in range(len(srcs)):
        for j in range(1, N_DEV):
            px, py, pc = x ^ ((j >> 2) & 1), y ^ ((j >> 1) & 1), c ^ (j & 1)
            p = 4 * px + 2 * py + pc
            k = (N_DEV - 1) * i + j - 1
            out.append(tuple(pltpu.make_async_remote_copy(
                src_ref=srcs[i] if whole else srcs[i].at[s], dst_ref=lands[i].at[d], send_sem=send_sems.at[k],
                recv_sem=recv_sems.at[k], device_id=(px, py, pc), device_id_type=MESH) for s, d in ((p, me), (me, p))))
    return out


def _split_start(srcs, lands, pattern, ncopy, *, after=(), name):
    n = len(srcs)
    na = len(after)

    def body(*refs):
        sems = refs[2 * n + na:]
        for send, _ in pattern(refs[:n], refs[n:2 * n], sems[0], sems[1]):
            send.start()
        refs[-1][...] = jnp.zeros_like(refs[-1])

    arrs = list(srcs) + list(lands)
    return pl.pallas_call(
        body, name=name,
        out_shape=(pltpu.SemaphoreType.DMA((ncopy,)), pltpu.SemaphoreType.DMA((ncopy,)),
                   *[pltpu.HBM(a.shape, a.dtype) for a in arrs], _sds((8, 128), F32)),
        in_specs=[HBM] * (2 * n) + [ANY] * na, out_specs=(SEM, SEM, *[HBM] * (2 * n), pl.BlockSpec(memory_space=pltpu.VMEM)),
        input_output_aliases={i: 2 + i for i in range(2 * n)},
        compiler_params=pltpu.CompilerParams(has_side_effects=EFFECT))(
            *[pltpu.with_memory_space_constraint(a, pltpu.HBM) for a in arrs], *after)


def _split_wait(started, after, pattern, *, name):
    send_sems, recv_sems, *arrs = started[:-1]
    n = len(arrs) // 2

    def body(*refs):
        for send, recv in pattern(refs[:n], refs[n:2 * n], refs[2 * n], refs[2 * n + 1]):
            send.wait_send()
            recv.wait_recv()

    outs = pl.pallas_call(
        body, name=name, out_shape=tuple(pltpu.HBM(a.shape, a.dtype) for a in arrs),
        in_specs=[HBM] * (2 * n) + [SEM, SEM, ANY], out_specs=tuple([HBM] * (2 * n)),
        input_output_aliases={i: i for i in range(2 * n)},
        compiler_params=pltpu.CompilerParams(has_side_effects=EFFECT))(*arrs, send_sems, recv_sems, after)
    return list(outs[n:])


def _adamw(parts, w, m, v, *, layer=None, prev=None, dep=(), name):
    P, R, C = parts.shape
    tr = next((t for t in (512, 256, 176, 128, 64, 32, 16, 8) if R % t == 0 and t * C <= 256 * 1024), R)
    c1 = 1.0 / (1.0 - ADAM_B1 ** ADAM_STEP)
    c2 = 1.0 / (1.0 - ADAM_B2 ** ADAM_STEP)
    nprev = 0 if prev is None else 4

    def body(p_ref, w_ref, m_ref, v_ref, *rest):
        g_ref, d_ref, nm_ref, nv_ref = rest[nprev + len(dep):]
        g = p_ref[0].astype(F32)
        for j in range(1, P):
            g = g + p_ref[j].astype(F32)
        g = g.reshape(w_ref.shape)
        nm = ADAM_B1 * m_ref[...] + (1.0 - ADAM_B1) * g
        nv = ADAM_B2 * v_ref[...] + (1.0 - ADAM_B2) * (g * g)
        g_ref[...] = g
        nm_ref[...] = nm
        nv_ref[...] = nv
        d_ref[...] = -ADAM_LR * ((nm * c1) / (jnp.sqrt(nv * c2) + ADAM_EPS) + ADAM_WD * w_ref[...])

    if layer is None:
        row = pl.BlockSpec((tr, C), lambda i: (i, 0))
    else:
        row = pl.BlockSpec((1, tr, C), lambda i: (layer, i, 0))
    out = _sds(w.shape, F32)
    return pl.pallas_call(
        body, name=name, grid=(R // tr,),
        in_specs=[pl.BlockSpec((P, tr, C), lambda i: (0, i, 0)), row, row, row] + [ANY] * (nprev + len(dep)),
        out_specs=[row, row, row, row], out_shape=[out, out, out, out],
        input_output_aliases={4 + k: k for k in range(nprev)},
        compiler_params=_params(("parallel",)))(parts, w, m, v, *(prev or ()), *dep)


BIG = ("w_in", "w_out", "xa_wq", "xa_wkv", "xa_wo", "ffn_w_up", "ffn_w_down")
COL_SHARDED = ("w_in", "xa_wkv", "ffn_w_up", "ml_conv_w", "ffn_conv_w")
SHARDED_SMALL = ("ml_conv_w", "ffn_conv_w")
REPLICATED = ("rel_bias", "ml_conv_b", "ml_i_bias", "ml_f_bias", "ml_norm_g", "swa_sinks", "ln1_g", "ln1_b", "ln2_g", "ln2_b",
              "ffn_conv_b", "ln3_g", "ln3_b")
NAMES = ("rel_bias", "w_in", "ml_conv_w", "ml_conv_b", "ml_i_bias", "ml_f_bias", "ml_norm_g", "swa_sinks", "w_out", "ln1_g", "ln1_b",
         "xa_wq", "xa_wkv", "xa_wo", "ln2_g", "ln2_b", "ffn_w_up", "ffn_conv_w", "ffn_conv_b", "ffn_w_down", "ln3_g", "ln3_b")


def _flat_rows(a, mult):
    f = a.reshape(-1)
    n = -(-f.shape[0] // (128 * mult)) * (128 * mult)
    if n != f.shape[0]:
        f = jnp.pad(f, (0, n - f.shape[0]))
    return f.reshape(-1, 128)


def _pack(arrs, mult):
    parts = [_flat_rows(a, mult) for a in arrs]
    return jnp.concatenate(parts, axis=0), [p.shape[0] for p in parts]


def _unpack(flat, rows, shapes):
    out, off = [], 0
    lead = flat.shape[:-2]
    for r, shp in zip(rows, shapes):
        n = int(np.prod(shp))
        piece = flat[..., off:off + r, :].reshape(lead + (r * 128,))[..., :n]
        out.append(piece.reshape(lead + tuple(shp)))
        off += r
    return out


def _full_from_shards(stacked, name):
    if name in COL_SHARDED:
        return jnp.moveaxis(stacked, 0, 2).reshape(stacked.shape[1], stacked.shape[2], N_DEV * stacked.shape[3])
    return jnp.moveaxis(stacked, 0, 1).reshape(stacked.shape[1], N_DEV * stacked.shape[2], stacked.shape[3])


def _shards_from_full(full, name):
    L, A, B = full.shape
    if name in COL_SHARDED:
        return jnp.moveaxis(full.reshape(L, A, N_DEV, B // N_DEV), 2, 0)
    return jnp.moveaxis(full.reshape(L, N_DEV, A // N_DEV, B), 1, 0)


def _pad_win(w):
    z = jnp.zeros(w.shape[:-1] + (NP_IN - N_IN,), w.dtype)
    return jnp.concatenate([w[..., :2048], w[..., 2056:], w[..., 2048:2056], z], axis=-1)


def _row128(v):
    return jnp.pad(v, (0, 128 - v.shape[0])).reshape(1, 128)


REST = BIG[1:]


def _layer_full(stacked, n):
    if n in COL_SHARDED:
        full = jnp.moveaxis(stacked, 0, 1).reshape(stacked.shape[1], N_DEV * stacked.shape[2])
    else:
        full = stacked.reshape(N_DEV * stacked.shape[1], stacked.shape[2])
    return _pad_win(full) if n == "w_in" else full


def _layer_shards(g, n):
    A, B = g.shape
    if n in COL_SHARDED:
        return jnp.moveaxis(g.reshape(A, N_DEV, B // N_DEV), 1, 0).astype(BF16)
    return g.reshape(N_DEV, A // N_DEV, B).astype(BF16)


def _with_own_block(a, idx, nblk):
    return lax.dynamic_update_slice(lax.empty((nblk,) + a.shape, a.dtype), a[None], (idx,) + (0,) * a.ndim)


def _gather_start(arrs, me, *, after=(), name):
    return _split_start(arrs, [_with_own_block(a, me, N_DEV) for a in arrs], _near_copies, 4 * len(arrs), after=after, name=name)


def _gather_finish(started, after, *, name):
    return _forward_sibling(_split_wait(started, after, _near_copies, name=name + "_wait"), name=name + "_forward")


def _reduce_start(tag, names, grads, me):
    send = [_layer_shards(g, n) for n, g in zip(names, grads)]
    lands = [_with_own_block(lax.dynamic_index_in_dim(s, me, 0, keepdims=False), me, N_DEV) for s in send]
    return _split_start(send, lands, _scatter_copies, (N_DEV - 1) * len(send), name=f"reduce_{tag}_start")


def _small_start(Gf, loss_part, me):
    sp_flat, sp_rows = _pack([Gf[n] for n in REPLICATED + SHARDED_SMALL] + [loss_part], 8)
    return _split_start([sp_flat], [_with_own_block(sp_flat, me, N_DEV)], _gather_copies, N_DEV - 1, name="gather_small_start"), sp_rows


def _small_finish(started, sp_rows, after, Gf, W, Mo, Vo, me):
    res = {}
    small = REPLICATED + SHARDED_SMALL
    sp_all = _split_wait(started, after, _gather_copies, name="gather_small_wait")[0]

    def widen(n, t):
        if n not in SHARDED_SMALL:
            return t[n]
        return lax.dynamic_update_slice(jnp.zeros(Gf[n].shape, F32), t[n], (0, 0, me * t[n].shape[2]))

    zl = jnp.zeros((8, 128), F32)
    wsm, _ = _pack([widen(n, W) for n in small] + [zl], 8)
    msm, _ = _pack([widen(n, Mo) for n in small] + [zl], 8)
    vsm, _ = _pack([widen(n, Vo) for n in small] + [zl], 8)
    outs_small = [_unpack(o_, sp_rows, [Gf[n].shape for n in small] + [(8, 128)])
                  for o_ in _adamw(sp_all, wsm, msm, vsm, name="adamw_small")]
    for kind, os_ in zip(("g", "d", "m", "v"), outs_small):
        for n, a in zip(small, os_[:-1]):
            if n in SHARDED_SMALL:
                a = lax.dynamic_slice(a, (0, 0, me * W[n].shape[2]), W[n].shape)
            res[kind, n] = a
    return res, outs_small[0][-1][0, 0]


def kernel(x, mem, rel_bias, w_in, ml_conv_w, ml_conv_b, ml_i_bias, ml_f_bias, ml_norm_g, swa_sinks, w_out, ln1_g, ln1_b, xa_wq, xa_wkv, xa_wo, ln2_g, ln2_b, ffn_w_up, ffn_conv_w, ffn_conv_b, ffn_w_down, ln3_g, ln3_b, loss_target, m_rel_bias, m_w_in, m_ml_conv_w, m_ml_conv_b, m_ml_i_bias, m_ml_f_bias, m_ml_norm_g, m_swa_sinks, m_w_out, m_ln1_g, m_ln1_b, m_xa_wq, m_xa_wkv, m_xa_wo, m_ln2_g, m_ln2_b, m_ffn_w_up, m_ffn_conv_w, m_ffn_conv_b, m_ffn_w_down, m_ln3_g, m_ln3_b, v_rel_bias, v_w_in, v_ml_conv_w, v_ml_conv_b, v_ml_i_bias, v_ml_f_bias, v_ml_norm_g, v_swa_sinks, v_w_out, v_ln1_g, v_ln1_b, v_xa_wq, v_xa_wkv, v_xa_wo, v_ln2_g, v_ln2_b, v_ffn_w_up, v_ffn_conv_w, v_ffn_conv_b, v_ffn_w_down, v_ln3_g, v_ln3_b):
    W = dict(rel_bias=rel_bias, w_in=w_in, ml_conv_w=ml_conv_w, ml_conv_b=ml_conv_b, ml_i_bias=ml_i_bias, ml_f_bias=ml_f_bias,
             ml_norm_g=ml_norm_g, swa_sinks=swa_sinks, w_out=w_out, ln1_g=ln1_g, ln1_b=ln1_b, xa_wq=xa_wq, xa_wkv=xa_wkv,
             xa_wo=xa_wo, ln2_g=ln2_g, ln2_b=ln2_b, ffn_w_up=ffn_w_up, ffn_conv_w=ffn_conv_w, ffn_conv_b=ffn_conv_b,
             ffn_w_down=ffn_w_down, ln3_g=ln3_g, ln3_b=ln3_b)
    Mo = dict(rel_bias=m_rel_bias, w_in=m_w_in, ml_conv_w=m_ml_conv_w, ml_conv_b=m_ml_conv_b, ml_i_bias=m_ml_i_bias,
              ml_f_bias=m_ml_f_bias, ml_norm_g=m_ml_norm_g, swa_sinks=m_swa_sinks, w_out=m_w_out, ln1_g=m_ln1_g, ln1_b=m_ln1_b,
              xa_wq=m_xa_wq, xa_wkv=m_xa_wkv, xa_wo=m_xa_wo, ln2_g=m_ln2_g, ln2_b=m_ln2_b, ffn_w_up=m_ffn_w_up,
              ffn_conv_w=m_ffn_conv_w, ffn_conv_b=m_ffn_conv_b, ffn_w_down=m_ffn_w_down, ln3_g=m_ln3_g, ln3_b=m_ln3_b)
    Vo = dict(rel_bias=v_rel_bias, w_in=v_w_in, ml_conv_w=v_ml_conv_w, ml_conv_b=v_ml_conv_b, ml_i_bias=v_ml_i_bias,
              ml_f_bias=v_ml_f_bias, ml_norm_g=v_ml_norm_g, swa_sinks=v_swa_sinks, w_out=v_w_out, ln1_g=v_ln1_g, ln1_b=v_ln1_b,
              xa_wq=v_xa_wq, xa_wkv=v_xa_wkv, xa_wo=v_xa_wo, ln2_g=v_ln2_g, ln2_b=v_ln2_b, ffn_w_up=v_ffn_w_up,
              ffn_conv_w=v_ffn_conv_w, ffn_conv_b=v_ffn_conv_b, ffn_w_down=v_ffn_w_down, ln3_g=v_ln3_g, ln3_b=v_ln3_b)
    S = x.shape[1]
    c_me = lax.axis_index("c")
    me = 4 * lax.axis_index("x") + 2 * lax.axis_index("y") + c_me
    xs = x.reshape(S, D)
    mems = mem.reshape(mem.shape[1], D)
    tgt = loss_target.reshape(S, D)


    sm_flat, sm_rows = _pack([W[n] for n in SHARDED_SMALL], 8)
    first = _gather([w_in[0].astype(BF16), sm_flat], name="gather_first")
    rest0 = _gather_start([W[n][0].astype(BF16) for n in REST], me, name="gather_rest0_start")
    full = [{"w_in": _layer_full(first[0], "w_in")}, None]
    conv_w = {n: _full_from_shards(s, n) for n, s in zip(SHARDED_SMALL, _unpack(first[1], sm_rows, [W[n].shape for n in SHARDED_SMALL]))}

    bucket = jnp.asarray(_t5_buckets())
    bias = _bias_table(rel_bias, bucket, name="bias_table")

    saved = []
    h0, h0b = xs, xs.astype(BF16)
    h0t = h0b.T
    for l in range(DEPTH):
        gbias = _row128(jnp.concatenate([ml_i_bias[l], ml_f_bias[l]]))
        sinks = _row128(swa_sinks[l])
        ng = ml_norm_g[l].reshape(1, ML_W)
        proj = _mm(h0b, full[l]["w_in"], tn=1024, dep=(rest0[-1],) if l == 0 else (), name=f"proj{l}")
        qk = _silu_conv_fwd(proj, conv_w["ml_conv_w"][l], ml_conv_b[l].reshape(1, -1), name=f"mlconv{l}")
        h_ml, cs, ns, ms = _mlstm_fwd(qk, proj, gbias, ng, name=f"mlstm{l}")
        h_sw = _swa_fwd(proj, bias, sinks, name=f"swa{l}")
        dep = ()
        if l == 0:
            landed = _gather_finish(rest0, h_sw, name="gather_rest0")
            full[0].update({n: _layer_full(s, n) for n, s in zip(REST, landed)})
            layer1 = _gather_start([W[n][1].astype(BF16) for n in BIG], me, after=(landed[0],), name="gather_layer1_start")
            dep = (layer1[-1],)
        fw = full[l]
        h1, h1b, h1t, z1 = _mm_res_ln([h_ml, h_sw], fw["w_out"], h0, ln1_g[l].reshape(1, D), ln1_b[l].reshape(1, D),
                                      name=f"mix_out{l}")
        kv = _mm(mems, fw["xa_wkv"], tm=256, dep=dep, name=f"xa_kv{l}")
        o = _xattn_fwd(h1b, fw["xa_wq"], kv, name=f"xattn{l}")
        h2, h2b, h2t, z2 = _mm_res_ln([o], fw["xa_wo"], h1, ln2_g[l].reshape(1, D), ln2_b[l].reshape(1, D), name=f"xa_out{l}")
        act, act_t = _ffn_gate_fwd(h2b, fw["ffn_w_up"], conv_w["ffn_conv_w"][l], ffn_conv_b[l].reshape(1, -1), name=f"ffn_gate{l}")
        h3, h3b, h3t, z3 = _mm_res_ln([act], fw["ffn_w_down"], h2, ln3_g[l].reshape(1, D), ln3_b[l].reshape(1, D),
                                      name=f"ffn_out{l}")
        saved.append(dict(h0t=h0t, proj=proj, qk=qk, cs=cs, ns=ns, ms=ms, h_ml=h_ml, h_sw=h_sw, z1=z1, h1b=h1b, h1t=h1t, kv=kv, o=o,
                          z2=z2, h2b=h2b, h2t=h2t, act_t=act_t, z3=z3, gbias=gbias, sinks=sinks, ng=ng))
        h0, h0b, h0t = h3, h3b, h3t
        if l == 0:
            landed = _gather_finish(layer1, h3b, name="gather_layer1")
            full[1] = {n: _layer_full(s, n) for n, s in zip(BIG, landed)}

    G = {n: [None] * DEPTH for n in NAMES if n != "rel_bias"}
    dbias = [None] * DEPTH
    pending = []
    dz3, dz3b, G["ln3_g"][DEPTH - 1], G["ln3_b"][DEPTH - 1], loss_part = _grad_in(
        [], None, ln=(saved[-1]["z3"], ln3_g[DEPTH - 1].reshape(1, D)), loss=(h0, tgt), name="loss_head")
    for l in reversed(range(DEPTH)):
        sv, fw = saved[l], full[l]
        win = fw["w_in"]
        G["ffn_w_down"][l] = _wgrad(sv["act_t"], dz3b, name=f"d_w_down{l}")
        dupg, dupv, G["ffn_conv_w"][l], G["ffn_conv_b"][l] = _ffn_gate_bwd(
            dz3b, fw["ffn_w_down"], sv["h2b"], fw["ffn_w_up"], conv_w["ffn_conv_w"][l], ffn_conv_b[l].reshape(1, -1),
            name=f"ffn_gate_bwd{l}")
        G["ffn_w_up"][l] = jnp.concatenate([_wgrad(sv["h2t"], dupg, name=f"d_w_up_g{l}"),
                                            _wgrad(sv["h2t"], dupv, name=f"d_w_up_v{l}")], axis=1)
        dep = ()
        if l == 0:
            names = ("ffn_w_up", "ffn_w_down")
            pending.append((names, 0, "ffn0", _reduce_start("ffn0", names, [G[n][0] for n in names], me)))
            dep = (pending[-1][3][-1],)
        dz2, dz2b, G["ln2_g"][l], G["ln2_b"][l] = _grad_in(
            [(dupg, fw["ffn_w_up"], 0), (dupv, fw["ffn_w_up"], 1)], dz3, ln=(sv["z2"], ln2_g[l].reshape(1, D)), dep=dep,
            name=f"d_h2_{l}")
        G["xa_wo"][l] = _mm_tn(sv["o"], dz2b, name=f"d_xa_wo{l}")
        dq, dkv = _xattn_bwd(dz2b, fw["xa_wo"], sv["h1b"], fw["xa_wq"], sv["kv"], name=f"xattn_bwd{l}")
        G["xa_wkv"][l] = _mm_tn(mems, dkv, name=f"d_xa_wkv{l}")
        G["xa_wq"][l] = _wgrad(sv["h1t"], dq, name=f"d_xa_wq{l}")
        dep = ()
        if l == 0:
            names = ("xa_wq", "xa_wkv", "xa_wo")
            pending.append((names, 0, "xa0", _reduce_start("xa0", names, [G[n][0] for n in names], me)))
            dep = (pending[-1][3][-1],)
        dz1, dz1b, G["ln1_g"][l], G["ln1_b"][l] = _grad_in(
            [(dq, fw["xa_wq"], 0)], dz2, ln=(sv["z1"], ln1_g[l].reshape(1, D)), dep=dep, name=f"d_h1_{l}")
        G["w_out"][l] = jnp.concatenate([_mm_tn(sv["h_ml"], dz1b, name=f"d_w_out_ml{l}"),
                                         _mm_tn(sv["h_sw"], dz1b, name=f"d_w_out_sw{l}")], axis=0)
        dep = ()
        if l == 0:
            pending.append((("w_out",), 0, "out0", _reduce_start("out0", ("w_out",), [G["w_out"][0]], me)))
            dep = (pending[-1][3][-1],)
        dhcat = _mm(dz1b, fw["w_out"], trans_b=True, dep=dep, name=f"d_hcat{l}")
        dsw, dbias[l], dsinks = _swa_bwd(dhcat, sv["proj"], bias, sv["sinks"], name=f"swa_bwd{l}")
        dqk, dml, dgb, dng = _mlstm_bwd(dhcat, sv["qk"], sv["proj"], sv["gbias"], sv["ng"], sv["cs"], sv["ns"], sv["ms"],
                                        name=f"mlstm_bwd{l}")
        dqk_pre, G["ml_conv_w"][l], G["ml_conv_b"][l] = _silu_conv_bwd(
            dqk, sv["proj"], conv_w["ml_conv_w"][l], ml_conv_b[l].reshape(1, -1), name=f"mlconv_bwd{l}")
        dw_qk = _wgrad(sv["h0t"], dqk_pre, name=f"d_w_in_qk{l}")
        dw_ml = _wgrad(sv["h0t"], dml, name=f"d_w_in_ml{l}")
        dw_sw = _wgrad(sv["h0t"], dsw, name=f"d_w_in_sw{l}")
        G["w_in"][l] = jnp.concatenate([dw_qk, dw_ml[:, :2 * ML_W + 2 * ML_H], dw_sw], axis=1)
        win_ml = jnp.concatenate([win[:, 1024:2048], win[:, 2816:2944]], axis=1)
        pairs = [(dqk_pre, win, 0), (dml, win_ml, 0), (dsw, win[:, 2048:2816], 0)]
        G["ml_i_bias"][l] = dgb[0, :ML_H]
        G["ml_f_bias"][l] = dgb[0, ML_H:2 * ML_H]
        G["ml_norm_g"][l] = dng
        G["swa_sinks"][l] = dsinks[0, :SW_H]
        if l > 0:
            pending.append((BIG, l, f"l{l}", _reduce_start(f"l{l}", BIG, [G[n][l] for n in BIG], me)))
            dz3, dz3b, G["ln3_g"][l - 1], G["ln3_b"][l - 1] = _grad_in(
                pairs, dz1, ln=(saved[l - 1]["z3"], ln3_g[l - 1].reshape(1, D)), dep=(pending[-1][3][-1],), name=f"d_h0_{l}")
        else:
            pending.append((("w_in",), 0, "in0", _reduce_start("in0", ("w_in",), [G["w_in"][0]], me)))
            dh = _grad_in(pairs, dz1, dep=(pending[-1][3][-1],), name="d_h0_0")
            grad_x = dh.reshape(x.shape)

    Gf = {n: jnp.stack([g.reshape(W[n].shape[1:]) if n in REPLICATED else g for g in G[n]]) for n in G if n not in BIG}
    Gf["rel_bias"] = _bias_table_bwd(dbias, bucket, name="bias_table_bwd")[:, :SW_H]
    small_started, sp_rows = _small_start(Gf, loss_part, me)

    parts = {}
    for names, l, tag, started in pending:
        landed = _split_wait(started, dh, _scatter_copies, name=f"reduce_{tag}_wait")
        parts.update({(n, l): p for n, p in zip(names, landed)})
    res = {}
    dep = (small_started[-1],)
    for n in BIG:
        outs = None
        for l in reversed(range(DEPTH)):
            outs = _adamw(parts[n, l], W[n], Mo[n], Vo[n], layer=l, prev=outs, dep=dep, name=f"adamw_{n}{l}")
            dep = ()
        for kind, a in zip(("g", "d", "m", "v"), outs):
            res[kind, n] = a
    res_small, loss = _small_finish(small_started, sp_rows, res["g", BIG[-1]], Gf, W, Mo, Vo, me)
    res.update(res_small)
    return (loss, grad_x, *[res["g", n] for n in NAMES], *[res["d", n] for n in NAMES], *[res["m", n] for n in NAMES],
            *[res["v", n] for n in NAMES])
```

```python
import functools
import math

import jax
import jax.numpy as jnp
import numpy as np
from jax import lax
from jax.experimental import pallas as pl
from jax.experimental.pallas import tpu as pltpu

F32 = jnp.float32
BF16 = jnp.bfloat16

N_DEV = 8
D = 1024
DEPTH = 2
ML_H = 4
ML_W = 512
ML_DH = 128
ML_L = 64
ML_CONV = 4
SW_DH = 64
SW_W = 512
SW_H = 8
SW_G = 4
SW_KVW = 128
BLK = 128
REL_B = 32
REL_MAXD = 128
XA_H = 4
XA_DH = 256
DFF = 2816
NB_FF = DFF // 128
FFN_CONV = 3
ALPHA = (2.0 * DEPTH) ** 0.25
EPS = 1e-5
N_IN = 2824
NP_IN = 3072
ML_GW = 2 * ML_W + 128
SW_GW = SW_W + 2 * SW_KVW
ADAM_LR = 0.001
ADAM_B1 = 0.9
ADAM_B2 = 0.999
ADAM_EPS = 1e-08
ADAM_WD = 0.01
ADAM_STEP = 10
VMEM_LIMIT = 56 * 1024 * 1024
MESH = pl.DeviceIdType.MESH

NN = ((1,), (0,))
NT = ((1,), (1,))
TN = ((0,), (0,))


def _dg(a, b, dn):
    if a.ndim == 3:
        dims = (((dn[0][0] + 1,), (dn[1][0] + 1,)), ((0,), (0,)))
    else:
        dims = (dn, ((), ()))
    return lax.dot_general(a.astype(BF16), b.astype(BF16), dims, preferred_element_type=F32)


@jax.custom_vjp
def dot_nn(a, b):
    return _dg(a, b, NN)


dot_nn.defvjp(lambda a, b: (_dg(a, b, NN), (a, b)), lambda r, g: (_dg(g, r[1], NT), _dg(r[0], g, TN)))


@jax.custom_vjp
def dot_nt(a, b):
    return _dg(a, b, NT)


dot_nt.defvjp(lambda a, b: (_dg(a, b, NT), (a, b)), lambda r, g: (_dg(g, r[1], NN), _dg(g, r[0], TN)))


@jax.custom_vjp
def dot_tn(a, b):
    return _dg(a, b, TN)


dot_tn.defvjp(lambda a, b: (_dg(a, b, TN), (a, b)), lambda r, g: (_dg(r[1], g, NT), _dg(r[0], g, NN)))


def _params(sem=None):
    return pltpu.CompilerParams(dimension_semantics=sem, vmem_limit_bytes=VMEM_LIMIT)


def _sds(shape, dtype):
    return jax.ShapeDtypeStruct(tuple(shape), dtype)


TOKEN = pl.BlockSpec((8, 128), lambda *_: (0, 0))


def _mm(a, b, *, trans_b=False, out_dtype=F32, add=None, add_scale=1.0, tm=1024, tn=512, dep=(), name):
    a_list = list(a) if isinstance(a, (list, tuple)) else [a]
    M = a_list[0].shape[0]
    N = b.shape[0] if trans_b else b.shape[1]
    tm = min(tm, M)
    tn = next(t for t in (tn, 384, 256, 128) if N % t == 0)
    assert M % tm == 0
    Ka = a_list[0].shape[1]
    assert all(t.shape[1] == Ka for t in a_list)
    tk = next(t for t in (Ka, 1408, 1024) if Ka % t == 0 and t <= 1408)
    na, npa = len(a_list), Ka // tk
    nk = na * npa
    has_add = add is not None

    def body(*refs):
        a_refs, b_ref = refs[:na], refs[na]
        add_ref = refs[na + 1] if has_add else None
        o_ref, acc_ref = refs[-2], refs[-1]
        k = pl.program_id(2)

        def finish(r):
            if has_add:
                r = r + add_scale * add_ref[...].astype(F32)
            o_ref[...] = r.astype(out_dtype)

        for t, a_ref in enumerate(a_refs):
            def step(a_ref=a_ref):
                p = _dg(a_ref[...], b_ref[...], NT if trans_b else NN)
                if nk == 1:
                    finish(p)
                    return

                @pl.when(k == 0)
                def _():
                    acc_ref[...] = p

                @pl.when((k > 0) & (k < nk - 1))
                def _():
                    acc_ref[...] += p

                @pl.when(k == nk - 1)
                def _():
                    finish(acc_ref[...] + p)

            if na == 1:
                step()
            else:
                pl.when((k >= t * npa) & (k < (t + 1) * npa))(step)

    in_specs = [pl.BlockSpec((tm, tk), lambda i, j, k, t=t: (i, jnp.clip(k - t * npa, 0, npa - 1))) for t in range(na)]
    in_specs.append(pl.BlockSpec((tn, tk), lambda i, j, k: (j, k)) if trans_b else pl.BlockSpec((tk, tn), lambda i, j, k: (k, j)))
    args = a_list + [b]
    if has_add:
        in_specs.append(pl.BlockSpec((tm, tn), lambda i, j, k: (i, j)))
        args.append(add)
    in_specs += [TOKEN] * len(dep)
    args += list(dep)
    return pl.pallas_call(
        body, name=name, grid=(M // tm, N // tn, nk), in_specs=in_specs,
        out_specs=pl.BlockSpec((tm, tn), lambda i, j, k: (i, j)), out_shape=_sds((M, N), out_dtype),
        scratch_shapes=[pltpu.VMEM((tm, tn) if nk > 1 else (8, 128), F32)],
        compiler_params=_params(("parallel", "parallel", "arbitrary")))(*args)


def _wgrad(a_t, g, *, name):
    K, N = a_t.shape[0], g.shape[1]
    return _mm(a_t, g, tm=K if K <= 1024 else K // 2, tn=next(t for t in range(1536, 0, -128) if N % t == 0), name=name)


def _mm_tn(a, g, *, name):
    S, K = a.shape
    N = g.shape[1]
    tk = K if K <= 1024 else K // 2
    tn = next(t for t in range(1536, 0, -128) if N % t == 0)
    ts = min(512, S)
    ns = S // ts
    assert K % tk == 0 and S % ts == 0

    def body(a_ref, g_ref, o_ref):
        s = pl.program_id(2)
        p = _dg(a_ref[...], g_ref[...], TN)

        @pl.when(s == 0)
        def _():
            o_ref[...] = p

        @pl.when(s > 0)
        def _():
            o_ref[...] += p

    return pl.pallas_call(
        body, name=name, grid=(K // tk, N // tn, ns),
        in_specs=[pl.BlockSpec((ts, tk), lambda i, j, s: (s, i)), pl.BlockSpec((ts, tn), lambda i, j, s: (s, j))],
        out_specs=pl.BlockSpec((tk, tn), lambda i, j, s: (i, j)), out_shape=_sds((K, N), F32),
        compiler_params=_params(("parallel", "parallel", "arbitrary")))(a, g)


def _mm_res_ln(a_list, w, resid, gam, bet, *, name):
    M, Ka = a_list[0].shape
    na = len(a_list)
    assert w.shape[0] == na * Ka
    tm = min(512, M)

    def body(*refs):
        a_refs, w_refs = refs[:na], refs[na:2 * na]
        r_ref, g_ref, b_ref, y_ref, yb_ref, yt_ref, z_ref = refs[2 * na:]
        z = ALPHA * r_ref[...]
        for a_ref, w_ref in zip(a_refs, w_refs):
            z = z + _dg(a_ref[...], w_ref[...], NN)
        mu = jnp.mean(z, axis=1, keepdims=True)
        zc = z - mu
        var = jnp.mean(zc * zc, axis=1, keepdims=True)
        y = zc * lax.rsqrt(var + EPS) * g_ref[...] + b_ref[...]
        y_ref[...] = y
        yb_ref[...] = y.astype(BF16)
        yt_ref[...] = y.T.astype(BF16)
        z_ref[...] = z

    row = pl.BlockSpec((tm, D), lambda i: (i, 0))
    vec = pl.BlockSpec((1, D), lambda i: (0, 0))
    a_specs = [pl.BlockSpec((tm, Ka), lambda i: (i, 0)) for _ in a_list]
    w_specs = [pl.BlockSpec((Ka, D), lambda i, t=t: (t, 0)) for t in range(na)]
    return pl.pallas_call(
        body, name=name, grid=(M // tm,), in_specs=a_specs + w_specs + [row, vec, vec],
        out_specs=[row, row, pl.BlockSpec((D, tm), lambda i: (0, i)), row],
        out_shape=[_sds((M, D), F32), _sds((M, D), BF16), _sds((D, M), BF16), _sds((M, D), F32)],
        compiler_params=_params(("parallel",)))(*a_list, *([w] * na), resid, gam, bet)


def _grad_in(pairs, add, *, ln=None, loss=None, dep=(), name):
    M = (add if add is not None else loss[0]).shape[0]
    tm = min(512, M)
    npair, nd = len(pairs), len(dep)
    has_ln, has_loss = ln is not None, loss is not None

    def body(*refs):
        n_in = 2 * npair + (2 if has_loss else 1) + 2 * has_ln + nd
        ins, outs = refs[:n_in], refs[n_in:]
        i = pl.program_id(0)
        pos = 2 * npair
        if has_loss:
            e = ins[pos][...] - ins[pos + 1][...]
            pos += 2
            dy = e * (1.0 / D)
            part = 0.5 * jnp.sum(jnp.sum(e * e, axis=1, keepdims=True) * (1.0 / D), axis=0, keepdims=True)
        else:
            dy = ALPHA * ins[pos][...]
            pos += 1
            for t in range(npair):
                dy = dy + _dg(ins[2 * t][...], ins[2 * t + 1][...], NT)
        if not has_ln:
            outs[0][...] = dy
            return
        z, g_ref = ins[pos][...], ins[pos + 1]
        mu = jnp.mean(z, axis=1, keepdims=True)
        zc = z - mu
        var = jnp.mean(zc * zc, axis=1, keepdims=True)
        rstd = lax.rsqrt(var + EPS)
        xh = zc * rstd
        dxh = dy * g_ref[...]
        m1 = jnp.mean(dxh, axis=1, keepdims=True)
        m2 = jnp.mean(dxh * xh, axis=1, keepdims=True)
        dz = rstd * (dxh - m1 - xh * m2)
        outs[0][...] = dz
        outs[1][...] = dz.astype(BF16)
        acc = [(outs[2], jnp.sum(dy * xh, axis=0, keepdims=True)), (outs[3], jnp.sum(dy, axis=0, keepdims=True))]
        if has_loss:
            acc.append((outs[4], jnp.broadcast_to(part, (8, 128))))

        @pl.when(i == 0)
        def _():
            for ref, val in acc:
                ref[...] = val

        @pl.when(i > 0)
        def _():
            for ref, val in acc:
                ref[...] += val

    row = pl.BlockSpec((tm, D), lambda i: (i, 0))
    vec = pl.BlockSpec((1, D), lambda i: (0, 0))
    in_specs, args = [], []
    for a, b, blk in pairs:
        in_specs += [pl.BlockSpec((tm, a.shape[1]), lambda i: (i, 0)), pl.BlockSpec((D, a.shape[1]), lambda i, blk=blk: (0, blk))]
        args += [a, b]
    if has_loss:
        in_specs += [row, row]
        args += list(loss)
    else:
        in_specs.append(row)
        args.append(add)
    if has_ln:
        in_specs += [row, vec]
        args += list(ln)
    in_specs += [TOKEN] * nd
    args += list(dep)
    if has_ln:
        out_specs = [row, row, vec, vec] + ([pl.BlockSpec((8, 128), lambda i: (0, 0))] if has_loss else [])
        out_shape = [_sds((M, D), F32), _sds((M, D), BF16), _sds((1, D), F32), _sds((1, D), F32)] + ([_sds((8, 128), F32)] if has_loss else [])
    else:
        out_specs, out_shape = row, _sds((M, D), F32)
    return pl.pallas_call(
        body, name=name, grid=(M // tm,), in_specs=in_specs, out_specs=out_specs, out_shape=out_shape,
        compiler_params=_params(("arbitrary",) if has_ln else ("parallel",)))(*args)


def _shift_down(x, d):
    if d == 0:
        return x
    rows = lax.broadcasted_iota(jnp.int32, x.shape, 0)
    return jnp.where(rows >= d, pltpu.roll(x, d, 0), 0.0)


def _shift_up(x, d):
    if d == 0:
        return x
    S = x.shape[0]
    rows = lax.broadcasted_iota(jnp.int32, x.shape, 0)
    return jnp.where(rows < S - d, pltpu.roll(x, S - d, 0), 0.0)


def _conv(x, w_ref, b_ref, cs, K):
    y = b_ref[:, cs]
    for j in range(K):
        y = y + _shift_down(x, K - 1 - j) * w_ref[j:j + 1, cs]
    return y


def _conv_bwd(dy, x, w_ref, dw_ref, db_ref, cs, K):
    dx = jnp.zeros_like(x)
    for j in range(K):
        sdy = _shift_up(dy, K - 1 - j)
        dx = dx + sdy * w_ref[j:j + 1, cs]
        dw_ref[j:j + 1, cs] = jnp.sum(sdy * x, axis=0, keepdims=True)
    db_ref[:, cs] = jnp.sum(dy, axis=0, keepdims=True)
    return dx


ALL = slice(None)


def _silu_conv_fwd(proj, cw, cb, *, name):
    S = proj.shape[0]

    def body(x_ref, w_ref, b_ref, o_ref):
        o_ref[...] = jax.nn.silu(_conv(x_ref[...], w_ref, b_ref, ALL, ML_CONV))

    col = pl.BlockSpec((S, 128), lambda j: (0, j))
    return pl.pallas_call(
        body, name=name, grid=(8,),
        in_specs=[col, pl.BlockSpec((ML_CONV, 128), lambda j: (0, j)), pl.BlockSpec((1, 128), lambda j: (0, j))],
        out_specs=col, out_shape=_sds((S, 2 * ML_W), F32), compiler_params=_params(("parallel",)))(proj, cw, cb)


def _silu_conv_bwd(dqk, proj, cw, cb, *, name):
    S = proj.shape[0]

    def body(d_ref, x_ref, w_ref, b_ref, dx_ref, dw_ref, db_ref):
        x = x_ref[...]
        y = _conv(x, w_ref, b_ref, ALL, ML_CONV)
        dy = jax.vjp(jax.nn.silu, y)[1](d_ref[...])[0]
        dx_ref[...] = _conv_bwd(dy, x, w_ref, dw_ref, db_ref, ALL, ML_CONV).astype(BF16)

    col = pl.BlockSpec((S, 128), lambda j: (0, j))
    wsp = pl.BlockSpec((ML_CONV, 128), lambda j: (0, j))
    bsp = pl.BlockSpec((1, 128), lambda j: (0, j))
    return pl.pallas_call(
        body, name=name, grid=(8,), in_specs=[col, col, wsp, bsp], out_specs=[col, wsp, bsp],
        out_shape=[_sds((S, 2 * ML_W), BF16), _sds((ML_CONV, 2 * ML_W), F32), _sds((1, 2 * ML_W), F32)],
        compiler_params=_params(("parallel",)))(dqk, proj, cw, cb)


GELU_C0 = math.sqrt(2.0 / math.pi)
GELU_C1 = 0.044715


def _gate_bwd(ug, uv, da):
    t = jnp.tanh(GELU_C0 * (ug + GELU_C1 * (ug * ug * ug)))
    half = 0.5 * (1.0 + t)
    dgelu = half + 0.5 * ug * (1.0 - t * t) * (GELU_C0 * (1.0 + 3.0 * GELU_C1 * (ug * ug)))
    return da * uv * dgelu, da * (ug * half)


def _up_pair(h_ref, ugw_ref, uvw_ref):
    return _dg(h_ref[...], jnp.concatenate([ugw_ref[...], uvw_ref[...]], axis=1), NN)


def _ffn_specs(S):
    return (pl.BlockSpec((D, 128), lambda j: (0, j)), pl.BlockSpec((D, 128), lambda j: (0, j + NB_FF)),
            pl.BlockSpec((FFN_CONV, 128), lambda j: (0, j)), pl.BlockSpec((FFN_CONV, 128), lambda j: (0, j + NB_FF)),
            pl.BlockSpec((1, 128), lambda j: (0, j)), pl.BlockSpec((1, 128), lambda j: (0, j + NB_FF)))


def _ffn_gate_fwd(hb, w_up, cw, cb, *, name):
    S = hb.shape[0]
    ug_, uv_, wg, wv, bg, bv = _ffn_specs(S)

    def body(h_ref, ugw_ref, uvw_ref, wg_ref, wv_ref, bg_ref, bv_ref, o_ref, ot_ref, h_s):
        @pl.when(pl.program_id(0) == 0)
        def _():
            pltpu.sync_copy(h_ref, h_s)

        x2 = _up_pair(h_s, ugw_ref, uvw_ref)
        ug = _conv(x2[:, :128], wg_ref, bg_ref, ALL, FFN_CONV)
        uv = _conv(x2[:, 128:], wv_ref, bv_ref, ALL, FFN_CONV)
        act = jax.nn.gelu(ug) * uv
        o_ref[...] = act.astype(BF16)
        ot_ref[...] = act.T.astype(BF16)

    return pl.pallas_call(
        body, name=name, grid=(NB_FF,), in_specs=[ANY, ug_, uv_, wg, wv, bg, bv],
        out_specs=[pl.BlockSpec((S, 128), lambda j: (0, j)), pl.BlockSpec((128, S), lambda j: (j, 0))],
        out_shape=[_sds((S, DFF), BF16), _sds((DFF, S), BF16)],
        scratch_shapes=[pltpu.VMEM((S, D), BF16)],
        compiler_params=_params(("arbitrary",)))(hb, w_up, w_up, cw, cw, cb, cb)


def _ffn_gate_bwd(dzb, w_down, hb, w_up, cw, cb, *, name):
    S = hb.shape[0]
    ug_, uv_, wg, wv, bg, bv = _ffn_specs(S)

    def body(dz_ref, h_ref, wd_ref, ugw_ref, uvw_ref, wg_ref, wv_ref, bg_ref, bv_ref,
             dxg_ref, dxv_ref, dwg_ref, dwv_ref, dbg_ref, dbv_ref, dz_s, h_s):
        @pl.when(pl.program_id(0) == 0)
        def _():
            pltpu.sync_copy(dz_ref, dz_s)
            pltpu.sync_copy(h_ref, h_s)

        x2 = _up_pair(h_s, ugw_ref, uvw_ref)
        xg, xv = x2[:, :128], x2[:, 128:]
        ug = _conv(xg, wg_ref, bg_ref, ALL, FFN_CONV)
        uv = _conv(xv, wv_ref, bv_ref, ALL, FFN_CONV)
        dug, duv = _gate_bwd(ug, uv, _dg(dz_s[...], wd_ref[...], NT))
        dxg_ref[...] = _conv_bwd(dug, xg, wg_ref, dwg_ref, dbg_ref, ALL, FFN_CONV).astype(BF16)
        dxv_ref[...] = _conv_bwd(duv, xv, wv_ref, dwv_ref, dbv_ref, ALL, FFN_CONV).astype(BF16)

    col = pl.BlockSpec((S, 128), lambda j: (0, j))
    half = _sds((S, DFF), BF16)
    dxg, dxv, dwg, dwv, dbg, dbv = pl.pallas_call(
        body, name=name, grid=(NB_FF,),
        in_specs=[ANY, ANY, pl.BlockSpec((128, D), lambda j: (j, 0)), ug_, uv_, wg, wv, bg, bv],
        out_specs=[col, col, wg, wg, bg, bg],
        out_shape=[half, half, _sds((FFN_CONV, DFF), F32), _sds((FFN_CONV, DFF), F32), _sds((1, DFF), F32), _sds((1, DFF), F32)],
        scratch_shapes=[pltpu.VMEM((S, D), BF16), pltpu.VMEM((S, D), BF16)],
        compiler_params=_params(("arbitrary",)))(dzb, hb, w_down, w_up, w_up, cw, cw, cb, cb)
    return dxg, dxv, jnp.concatenate([dwg, dwv], axis=1), jnp.concatenate([dbg, dbv], axis=1)


def _log_sigmoid(x):
    return jnp.minimum(x, 0.0) - jnp.log1p(jnp.exp(-jnp.abs(x)))


@jax.custom_vjp
def _clamp_div(num, den, floor, shift):
    return num / jnp.maximum(jnp.abs(den), floor)


def _clamp_div_fwd(num, den, floor, shift):
    out = num / jnp.maximum(jnp.abs(den), floor)
    return out, (den, floor, out)


def _clamp_div_bwd(res, g):
    den, floor, out = res
    active = jnp.abs(den) < floor
    dinv = jnp.maximum(jnp.abs(den), floor)
    go = jnp.sum(g * out, axis=-1, keepdims=True)
    ddiv = -go / dinv
    return (g / dinv, jnp.where(active, 0.0, ddiv * jnp.sign(den)), jnp.where(active, ddiv, 0.0),
            jnp.sum(jnp.where(active, go, 0.0), axis=-2, keepdims=True))


_clamp_div.defvjp(_clamp_div_fwd, _clamp_div_bwd)


def _ml_heads(q, k, v, o_pre, gates, gbias, C, n, ng, m, shift):
    H, L, _ = q.shape
    lane1 = lax.broadcasted_iota(jnp.int32, (1, 128), 1)
    gz = gates + jnp.where(lane1 < ML_H, lax.stop_gradient(gbias), gbias)
    gz = jnp.broadcast_to(gz[None], (H, L, 128))
    hid = lax.broadcasted_iota(jnp.int32, (H, L, 128), 0)
    lane = lax.broadcasted_iota(jnp.int32, (H, L, 128), 2)
    ig = jnp.sum(jnp.where(lane == hid, gz, 0.0), axis=2, keepdims=True)
    lf = _log_sigmoid(jnp.sum(jnp.where(lane == ML_H + hid, gz, 0.0), axis=2, keepdims=True))
    r = lax.broadcasted_iota(jnp.int32, (H, L, L), 1)
    c = lax.broadcasted_iota(jnp.int32, (H, L, L), 2)
    eye, tril = r == c, r >= c

    def to_row(col):
        return jnp.sum(jnp.where(eye, col, 0.0), axis=1, keepdims=True)

    b_col = jnp.sum(jnp.where(tril, to_row(lf), 0.0), axis=2, keepdims=True)
    Dm = jnp.where(tril, b_col - to_row(b_col) + to_row(ig), -jnp.inf)
    inter = b_col + m
    m_t = lax.stop_gradient(jnp.maximum(inter, jnp.max(Dm, axis=2, keepdims=True)))
    w_inter = jnp.exp(inter - m_t)
    ks = k * (ML_DH ** -0.5)
    s = dot_nt(q, ks) * jnp.exp(Dm - m_t)
    num = w_inter * dot_nn(q, C) + dot_nn(s, v)
    den = w_inter * jnp.sum(q * n, axis=2, keepdims=True) + jnp.sum(s, axis=2, keepdims=True)
    h = _clamp_div(num, den, jnp.exp(-m_t), shift)
    g = jnp.sum(lf, axis=1, keepdims=True)
    a = g - b_col + ig
    m_new = lax.stop_gradient(jnp.maximum(g + m, jnp.max(a, axis=1, keepdims=True)))
    decay = jnp.exp(g + m - m_new)
    wk = jnp.exp(a - m_new)
    C_new = decay * C + dot_tn(ks * wk, v)
    n_new = decay * n + jnp.sum(wk * ks, axis=1, keepdims=True)
    mu = jnp.mean(h, axis=2, keepdims=True)
    hc = h - mu
    var = jnp.mean(hc * hc, axis=2, keepdims=True)
    out = jax.nn.sigmoid(o_pre) * (hc * lax.rsqrt(var + EPS) * ng)
    return out, C_new, n_new, m_new


def _hs(h, off=0):
    return slice(off + h * ML_DH, off + (h + 1) * ML_DH)


def _heads(ref, off=0):
    return jnp.stack([ref[:, _hs(h, off)] for h in range(ML_H)])


def _mlstm_fwd(qk, proj, gbias, ng, *, name):
    S = qk.shape[0]
    nc = S // ML_L

    def body(q_ref, k_ref, v_ref, o_ref, g_ref, gb_ref, ng_ref, h_ref, cs_ref, ns_ref, ms_ref, c_s, n_s, m_s):
        @pl.when(pl.program_id(0) == 0)
        def _():
            c_s[...] = jnp.zeros_like(c_s)
            n_s[...] = jnp.zeros_like(n_s)
            m_s[...] = jnp.zeros_like(m_s)

        C, n = c_s[...], n_s[...]
        cs_ref[0] = C
        ns_ref[0] = n
        ms_ref[0] = m_s[...]
        out, C2, n2, m2 = _ml_heads(_heads(q_ref), _heads(k_ref), _heads(v_ref), _heads(o_ref), g_ref[...], gb_ref[...], C, n,
                                    _heads(ng_ref), m_s[:, :, 0:1], jnp.zeros((ML_H, 1, 1), F32))
        for h in range(ML_H):
            h_ref[:, _hs(h)] = out[h].astype(BF16)
        c_s[...] = C2
        n_s[...] = n2
        m_s[...] = jnp.broadcast_to(m2, (ML_H, 1, 128))

    def w(j):
        return pl.BlockSpec((ML_L, ML_W), lambda c, j=j: (c, j))

    return pl.pallas_call(
        body, name=name, grid=(nc,),
        in_specs=[w(0), w(1), w(2), w(3), pl.BlockSpec((ML_L, 128), lambda c: (c, 22)),
                  pl.BlockSpec((1, 128), lambda c: (0, 0)), pl.BlockSpec((1, ML_W), lambda c: (0, 0))],
        out_specs=[w(0), pl.BlockSpec((1, ML_H, ML_DH, ML_DH), lambda c: (c, 0, 0, 0)),
                   pl.BlockSpec((1, ML_H, 1, 128), lambda c: (c, 0, 0, 0)), pl.BlockSpec((1, ML_H, 1, 128), lambda c: (c, 0, 0, 0))],
        out_shape=[_sds((S, ML_W), BF16), _sds((nc, ML_H, ML_DH, ML_DH), F32), _sds((nc, ML_H, 1, 128), F32),
                   _sds((nc, ML_H, 1, 128), F32)],
        scratch_shapes=[pltpu.VMEM((ML_H, ML_DH, ML_DH), F32), pltpu.VMEM((ML_H, 1, 128), F32), pltpu.VMEM((ML_H, 1, 128), F32)],
        compiler_params=_params(("arbitrary",)))(qk, qk, proj, proj, proj, gbias, ng)


def _mlstm_bwd(dh, qk, proj, gbias, ng, cs, ns, ms, *, name):
    S = qk.shape[0]
    nc = S // ML_L

    def body(dh_ref, q_ref, k_ref, v_ref, o_ref, g_ref, gb_ref, ng_ref, cs_ref, ns_ref, ms_ref,
             dqk_ref, dml_ref, dgb_ref, dng_ref, dc_s, dn_s):
        @pl.when(pl.program_id(0) == 0)
        def _():
            dc_s[...] = jnp.zeros_like(dc_s)
            dn_s[...] = jnp.zeros_like(dn_s)
            dgb_ref[...] = jnp.zeros_like(dgb_ref)
            dng_ref[...] = jnp.zeros_like(dng_ref)

        m = ms_ref[0][:, :, 0:1]

        def f(q, k, v, o_pre, gates, gb, C, n, ng_, shift):
            return _ml_heads(q, k, v, o_pre, gates, gb, C, n, ng_, m, shift)[:3]

        _, vjp = jax.vjp(f, _heads(q_ref), _heads(k_ref), _heads(v_ref), _heads(o_ref), g_ref[...], gb_ref[...],
                         cs_ref[0], ns_ref[0], _heads(ng_ref), jnp.zeros((ML_H, 1, 1), F32))
        dq, dk, dv, do, dgates, dgb, dC, dn, dng, dshift = vjp((_heads(dh_ref), dc_s[...], dn_s[...]))
        lane1 = lax.broadcasted_iota(jnp.int32, (1, 128), 1)
        for h in range(ML_H):
            dqk_ref[:, _hs(h)] = dq[h]
            dqk_ref[:, _hs(h, ML_W)] = dk[h]
            dml_ref[:, _hs(h)] = dv[h].astype(BF16)
            dml_ref[:, _hs(h, ML_W)] = do[h].astype(BF16)
            dng_ref[:, _hs(h)] += dng[h]
            dgb = jnp.where(lane1 == h, dshift[h], dgb)
        dc_s[...] = dC
        dn_s[...] = dn
        dml_ref[:, 2 * ML_W:] = dgates.astype(BF16)
        dgb_ref[...] += dgb

    def w(j):
        return pl.BlockSpec((ML_L, ML_W), lambda c, j=j: (nc - 1 - c, j))

    vec = pl.BlockSpec((1, 128), lambda c: (0, 0))
    vecw = pl.BlockSpec((1, ML_W), lambda c: (0, 0))
    gsp = pl.BlockSpec((ML_L, 128), lambda c: (nc - 1 - c, 22))
    st = pl.BlockSpec((1, ML_H, 1, 128), lambda c: (nc - 1 - c, 0, 0, 0))
    return pl.pallas_call(
        body, name=name, grid=(nc,),
        in_specs=[w(0), w(0), w(1), w(2), w(3), gsp, vec, vecw,
                  pl.BlockSpec((1, ML_H, ML_DH, ML_DH), lambda c: (nc - 1 - c, 0, 0, 0)), st, st],
        out_specs=[pl.BlockSpec((ML_L, 2 * ML_W), lambda c: (nc - 1 - c, 0)), pl.BlockSpec((ML_L, ML_GW), lambda c: (nc - 1 - c, 0)),
                   vec, vecw],
        out_shape=[_sds((S, 2 * ML_W), F32), _sds((S, ML_GW), BF16), _sds((1, 128), F32), _sds((1, ML_W), F32)],
        scratch_shapes=[pltpu.VMEM((ML_H, ML_DH, ML_DH), F32), pltpu.VMEM((ML_H, 1, 128), F32)],
        compiler_params=_params(("arbitrary",)))(dh, qk, qk, proj, proj, proj, gbias, ng, cs, ns, ms)


def _t5_buckets():
    r = np.arange(BLK)[:, None]
    c = np.arange(2 * BLK)[None, :]
    n = np.maximum(BLK + r - c, 0)
    max_exact = REL_B // 2
    nf = np.maximum(n, 1).astype(np.float32)
    large = max_exact + (np.log(nf / np.float32(max_exact)) / np.float32(math.log(REL_MAXD / max_exact))
                         * np.float32(REL_B - max_exact)).astype(np.int32)
    large = np.minimum(large, REL_B - 1)
    return np.where(n < max_exact, n, large).astype(np.int32)


def _bias_table(rel_bias, bucket, *, name):
    def body(rb_ref, bk_ref, o_ref):
        bk = bk_ref[...]
        for h in range(SW_H):
            acc = jnp.zeros((BLK, 2 * BLK), F32)
            for b in range(REL_B):
                acc = jnp.where(bk == b, rb_ref[b, h], acc)
            o_ref[h] = acc

    return pl.pallas_call(
        body, name=name, in_specs=[pl.BlockSpec(memory_space=pltpu.SMEM), pl.BlockSpec(memory_space=pltpu.VMEM)],
        out_specs=pl.BlockSpec(memory_space=pltpu.VMEM), out_shape=_sds((SW_H, BLK, 2 * BLK), F32),
        compiler_params=_params())(rel_bias, bucket)


def _bias_table_bwd(dbias_list, bucket, *, name):
    nl = len(dbias_list)

    def body(*refs):
        d_refs, bk_ref, o_ref = refs[:nl], refs[nl], refs[nl + 1]
        bk = bk_ref[...]
        rows = lax.broadcasted_iota(jnp.int32, (REL_B, 128), 0)
        lanes = lax.broadcasted_iota(jnp.int32, (REL_B, 128), 1)
        acc = jnp.zeros((REL_B, 128), F32)
        for h in range(SW_H):
            d = d_refs[0][h]
            for d_ref in d_refs[1:]:
                d = d + d_ref[h]
            for b in range(REL_B):
                t = jnp.sum(jnp.sum(jnp.where(bk == b, d, 0.0), axis=0, keepdims=True), axis=1, keepdims=True)
                acc = jnp.where((rows == b) & (lanes == h), t, acc)
        o_ref[...] = acc

    vm = pl.BlockSpec(memory_space=pltpu.VMEM)
    return pl.pallas_call(
        body, name=name, in_specs=[vm] * (nl + 1), out_specs=vm, out_shape=_sds((REL_B, 128), F32),
        compiler_params=_params())(*dbias_list, bucket)


def _swa_heads(q, kp, kc, vp, vc, bp, bc, sinks, has_prev):
    def rep(t):
        return jnp.concatenate([t[g:g + 1] for g in range(SW_H // SW_G) for _ in range(SW_G)], axis=0)

    r = lax.broadcasted_iota(jnp.int32, (SW_H, BLK, BLK), 1)
    c = lax.broadcasted_iota(jnp.int32, (SW_H, BLK, BLK), 2)
    hid = lax.broadcasted_iota(jnp.int32, (SW_H, 1, 128), 0)
    lane = lax.broadcasted_iota(jnp.int32, (SW_H, 1, 128), 2)
    sink = jnp.sum(jnp.where(lane == hid, jnp.broadcast_to(sinks[None], (SW_H, 1, 128)), 0.0), axis=2, keepdims=True)
    lp = jnp.where((c > r) & has_prev, dot_nt(q, rep(kp)) * (SW_DH ** -0.5) + bp, -jnp.inf)
    lc = jnp.where(c <= r, dot_nt(q, rep(kc)) * (SW_DH ** -0.5) + bc, -jnp.inf)
    mx = jnp.maximum(jnp.maximum(jnp.max(lp, axis=2, keepdims=True), jnp.max(lc, axis=2, keepdims=True)), sink)
    mx = lax.stop_gradient(mx)
    pp, pc = jnp.exp(lp - mx), jnp.exp(lc - mx)
    den = jnp.sum(pp, axis=2, keepdims=True) + jnp.sum(pc, axis=2, keepdims=True) + jnp.exp(sink - mx)
    return dot_nn(pp / den, rep(vp)) + dot_nn(pc / den, rep(vc))


def _qs(h, off=0):
    return slice(off + h * SW_DH, off + (h + 1) * SW_DH)


def _split(ref, n):
    return jnp.stack([ref[:, _qs(h)] for h in range(n)])


def _swa_fwd(proj, bias, sinks, *, name):
    S = proj.shape[0]
    nb = S // BLK
    nkv = SW_H // SW_G

    def body(q_ref, kp_ref, kc_ref, vp_ref, vc_ref, b_ref, s_ref, o_ref):
        out = _swa_heads(_split(q_ref, SW_H), _split(kp_ref, nkv), _split(kc_ref, nkv), _split(vp_ref, nkv), _split(vc_ref, nkv),
                         b_ref[:, :, :BLK], b_ref[:, :, BLK:], s_ref[...], pl.program_id(0) > 0)
        for h in range(SW_H):
            o_ref[:, _qs(h)] = out[h].astype(BF16)

    def cur(j):
        return pl.BlockSpec((BLK, 128), lambda n, j=j: (n, j))

    def prev(j):
        return pl.BlockSpec((BLK, 128), lambda n, j=j: (jnp.maximum(n - 1, 0), j))

    return pl.pallas_call(
        body, name=name, grid=(nb,),
        in_specs=[pl.BlockSpec((BLK, SW_W), lambda n: (n, 4)), prev(20), cur(20), prev(21), cur(21),
                  pl.BlockSpec((SW_H, BLK, 2 * BLK), lambda n: (0, 0, 0)), pl.BlockSpec((1, 128), lambda n: (0, 0))],
        out_specs=pl.BlockSpec((BLK, SW_W), lambda n: (n, 0)), out_shape=_sds((S, SW_W), BF16),
        compiler_params=_params(("parallel",)))(proj, proj, proj, proj, proj, bias, sinks)


def _swa_bwd(dh, proj, bias, sinks, *, name):
    S = proj.shape[0]
    nb = S // BLK

    nkv = SW_H // SW_G

    def body(dh_ref, q_ref, kp_ref, kc_ref, vp_ref, vc_ref, b_ref, s_ref, dsw_ref, db_ref, ds_ref, ck_s, cv_s):
        i = pl.program_id(0)

        @pl.when(i == 0)
        def _():
            ck_s[...] = jnp.zeros_like(ck_s)
            cv_s[...] = jnp.zeros_like(cv_s)
            db_ref[...] = jnp.zeros_like(db_ref)
            ds_ref[...] = jnp.zeros_like(ds_ref)

        f = functools.partial(_swa_heads, has_prev=i < nb - 1)
        _, vjp = jax.vjp(f, _split(q_ref, SW_H), _split(kp_ref, nkv), _split(kc_ref, nkv), _split(vp_ref, nkv),
                         _split(vc_ref, nkv), b_ref[:, :, :BLK], b_ref[:, :, BLK:], s_ref[...])
        dq, dkp, dkc, dvp, dvc, dbp, dbc, ds = vjp(_split(dh_ref, SW_H))
        for h in range(SW_H):
            dsw_ref[:, _qs(h)] = dq[h].astype(BF16)
        for g in range(nkv):
            dsw_ref[:, _qs(g, SW_W)] = (dkc[g] + ck_s[:, _qs(g)]).astype(BF16)
            dsw_ref[:, _qs(g, SW_W + SW_KVW)] = (dvc[g] + cv_s[:, _qs(g)]).astype(BF16)
            ck_s[:, _qs(g)] = dkp[g]
            cv_s[:, _qs(g)] = dvp[g]
        db_ref[:, :, :BLK] += dbp
        db_ref[:, :, BLK:] += dbc
        ds_ref[...] += ds

    def cur(j):
        return pl.BlockSpec((BLK, 128), lambda i, j=j: (nb - 1 - i, j))

    def prev(j):
        return pl.BlockSpec((BLK, 128), lambda i, j=j: (jnp.maximum(nb - 2 - i, 0), j))

    bsp = pl.BlockSpec((SW_H, BLK, 2 * BLK), lambda i: (0, 0, 0))
    vec = pl.BlockSpec((1, 128), lambda i: (0, 0))
    return pl.pallas_call(
        body, name=name, grid=(nb,),
        in_specs=[pl.BlockSpec((BLK, SW_W), lambda i: (nb - 1 - i, 1)), pl.BlockSpec((BLK, SW_W), lambda i: (nb - 1 - i, 4)),
                  prev(20), cur(20), prev(21), cur(21), bsp, vec],
        out_specs=[pl.BlockSpec((BLK, SW_GW), lambda i: (nb - 1 - i, 0)), bsp, vec],
        out_shape=[_sds((S, SW_GW), BF16), _sds((SW_H, BLK, 2 * BLK), F32), _sds((1, 128), F32)],
        scratch_shapes=[pltpu.VMEM((BLK, 128), F32)] * 2,
        compiler_params=_params(("arbitrary",)))(dh, proj, proj, proj, proj, proj, bias, sinks)


XA_TM = 1024


def _xa_head(qh, kh, vh):
    logits = dot_nt(qh, kh) * (XA_DH ** -0.5)
    mx = lax.stop_gradient(jnp.max(logits, axis=1, keepdims=True))
    e = jnp.exp(logits - mx)
    return dot_nn(e / jnp.sum(e, axis=1, keepdims=True), vh)


def _xs(h, off=0):
    return slice(off + h * XA_DH, off + (h + 1) * XA_DH)


def _xattn_fwd(hb, wq, kv, *, name):
    S = hb.shape[0]
    M = kv.shape[0]

    def body(h_ref, wq_ref, kv_ref, o_ref):
        q = _dg(h_ref[...], wq_ref[...], NN)
        for h in range(XA_H):
            o_ref[:, _xs(h)] = _xa_head(q[:, _xs(h)], kv_ref[:, _xs(h)], kv_ref[:, _xs(h, D)]).astype(BF16)

    tm = min(XA_TM, S)
    row = pl.BlockSpec((tm, D), lambda i: (i, 0))
    return pl.pallas_call(
        body, name=name, grid=(S // tm,),
        in_specs=[row, pl.BlockSpec((D, D), lambda i: (0, 0)), pl.BlockSpec((M, 2 * D), lambda i: (0, 0))], out_specs=row,
        out_shape=_sds((S, D), BF16), compiler_params=_params(("parallel",)))(hb, wq, kv)


def _xattn_bwd(dzb, wo, hb, wq, kv, *, name):
    S = hb.shape[0]
    M = kv.shape[0]

    def body(dz_ref, wo_ref, h_ref, wq_ref, kv_ref, dq_ref, dkv_ref):
        @pl.when(pl.program_id(0) == 0)
        def _():
            dkv_ref[...] = jnp.zeros_like(dkv_ref)

        q = _dg(h_ref[...], wq_ref[...], NN)
        do = _dg(dz_ref[...], wo_ref[...], NT)
        for h in range(XA_H):
            _, vjp = jax.vjp(_xa_head, q[:, _xs(h)], kv_ref[:, _xs(h)], kv_ref[:, _xs(h, D)])
            dq, dk, dv = vjp(do[:, _xs(h)])
            dq_ref[:, _xs(h)] = dq.astype(BF16)
            dkv_ref[:, _xs(h)] += dk
            dkv_ref[:, _xs(h, D)] += dv

    tm = min(XA_TM, S)
    row = pl.BlockSpec((tm, D), lambda i: (i, 0))
    full = pl.BlockSpec((M, 2 * D), lambda i: (0, 0))
    sq = pl.BlockSpec((D, D), lambda i: (0, 0))
    return pl.pallas_call(
        body, name=name, grid=(S // tm,), in_specs=[row, sq, row, sq, full],
        out_specs=[row, full], out_shape=[_sds((S, D), BF16), _sds((M, 2 * D), F32)],
        compiler_params=_params(("arbitrary",)))(dzb, wo, hb, wq, kv)


ANY = pl.BlockSpec(memory_space=pl.ANY)


def _place():
    x, y, c = lax.axis_index("x"), lax.axis_index("y"), lax.axis_index("c")
    chips = [(1 - x, y), (x, 1 - y), (1 - x, 1 - y)]
    return x, y, c, chips


def _gather(arrs, *, name):
    n = len(arrs)

    def body(*refs):
        srcs, outs = refs[:n], refs[n:2 * n]
        send_sems, recv_sems, local_sems = refs[2 * n:]
        x, y, c, chips = _place()
        me, sib = (x, y, c), (x, y, 1 - c)

        def idx(p):
            return 4 * p[0] + 2 * p[1] + p[2]

        def copy(i, k, block, to, from_src=False):
            return pltpu.make_async_remote_copy(
                src_ref=srcs[i] if from_src else outs[i].at[idx(block)], dst_ref=outs[i].at[idx(block)],
                send_sem=send_sems.at[7 * i + k], recv_sem=recv_sems.at[7 * i + k], device_id=to, device_id_type=MESH)

        local = [pltpu.make_async_copy(srcs[i], outs[i].at[idx(me)], local_sems.at[i]) for i in range(n)]
        for cp in local:
            cp.start()
        first = []
        for i in range(n):
            first.append(copy(i, 0, me, sib, True))
            first += [copy(i, 1 + j, me, (*chip, c), True) for j, chip in enumerate(chips)]
        for cp in first:
            cp.start()
        passed = []
        for j, chip in enumerate(chips):
            for i in range(n):
                copy(i, 1 + j, (*chip, c), me).wait_recv()
                cp = copy(i, 4 + j, (*chip, c), sib)
                cp.start()
                passed.append(cp)
        for i in range(n):
            copy(i, 0, sib, me).wait_recv()
        for j, chip in enumerate(chips):
            for i in range(n):
                copy(i, 4 + j, (*chip, 1 - c), me).wait_recv()
        for cp in first + passed:
            cp.wait_send()
        for cp in local:
            cp.wait()

    return pl.pallas_call(
        body, name=name, in_specs=[ANY] * n, out_specs=[ANY] * n,
        out_shape=[_sds((N_DEV,) + a.shape, a.dtype) for a in arrs],
        scratch_shapes=[pltpu.SemaphoreType.DMA((7 * n,)), pltpu.SemaphoreType.DMA((7 * n,)), pltpu.SemaphoreType.DMA((n,))],
        compiler_params=pltpu.CompilerParams(has_side_effects=True))(*arrs)


HBM = pl.BlockSpec(memory_space=pltpu.HBM)
SEM = pl.BlockSpec(memory_space=pltpu.SEMAPHORE)
EFFECT = pltpu.SideEffectType.DATAFLOW_SIDE_EFFECTING


def _near_copies(srcs, lands, send_sems, recv_sems):
    x, y, c, chips = _place()
    me = 4 * x + 2 * y + c
    out = []
    for i in range(len(srcs)):
        for k, (px, py, pc) in enumerate([(x, y, 1 - c)] + [(*chip, c) for chip in chips]):
            out.append(tuple(pltpu.make_async_remote_copy(
                src_ref=srcs[i], dst_ref=lands[i].at[slot], send_sem=send_sems.at[4 * i + k], recv_sem=recv_sems.at[4 * i + k],
                device_id=(px, py, pc), device_id_type=MESH) for slot in (me, 4 * px + 2 * py + pc)))
    return out


def _forward_sibling(lands, *, name):
    n = len(lands)

    def body(*refs):
        bufs = refs[n:2 * n]
        send_sems, recv_sems = refs[2 * n:]
        x, y, c, chips = _place()
        copies = [tuple(pltpu.make_async_remote_copy(
            src_ref=bufs[i].at[4 * chip[0] + 2 * chip[1] + c], dst_ref=bufs[i].at[4 * chip[0] + 2 * chip[1] + cc],
            send_sem=send_sems.at[3 * i + j], recv_sem=recv_sems.at[3 * i + j], device_id=(x, y, 1 - c), device_id_type=MESH)
            for cc in (c, 1 - c)) for i in range(n) for j, chip in enumerate(chips)]
        for send, _ in copies:
            send.start()
        for send, recv in copies:
            send.wait_send()
            recv.wait_recv()

    return pl.pallas_call(
        body, name=name, in_specs=[ANY] * n, out_specs=[ANY] * n, out_shape=[_sds(a.shape, a.dtype) for a in lands],
        input_output_aliases={i: i for i in range(n)},
        scratch_shapes=[pltpu.SemaphoreType.DMA((3 * n,)), pltpu.SemaphoreType.DMA((3 * n,))],
        compiler_params=pltpu.CompilerParams(has_side_effects=True))(*lands)


def _gather_copies(srcs, lands, send_sems, recv_sems):
    return _scatter_copies(srcs, lands, send_sems, recv_sems, whole=True)


def _scatter_copies(srcs, lands, send_sems, recv_sems, whole=False):
    x, y, c, _ = _place()
    me = 4 * x + 2 * y + c
    out = []
    for i in range(len(srcs)):
        for j in range(1, N_DEV):
            px, py, pc = x ^ ((j >> 2) & 1), y ^ ((j >> 1) & 1), c ^ (j & 1)
            p = 4 * px + 2 * py + pc
            k = (N_DEV - 1) * i + j - 1
            out.append(tuple(pltpu.make_async_remote_copy(
                src_ref=srcs[i] if whole else srcs[i].at[s], dst_ref=lands[i].at[d], send_sem=send_sems.at[k],
                recv_sem=recv_sems.at[k], device_id=(px, py, pc), device_id_type=MESH) for s, d in ((p, me), (me, p))))
    return out


def _split_start(srcs, lands, pattern, ncopy, *, after=(), name):
    n = len(srcs)
    na = len(after)

    def body(*refs):
        sems = refs[2 * n + na:]
        for send, _ in pattern(refs[:n], refs[n:2 * n], sems[0], sems[1]):
            send.start()
        refs[-1][...] = jnp.zeros_like(refs[-1])

    arrs = list(srcs) + list(lands)
    return pl.pallas_call(
        body, name=name,
        out_shape=(pltpu.SemaphoreType.DMA((ncopy,)), pltpu.SemaphoreType.DMA((ncopy,)),
                   *[pltpu.HBM(a.shape, a.dtype) for a in arrs], _sds((8, 128), F32)),
        in_specs=[HBM] * (2 * n) + [ANY] * na, out_specs=(SEM, SEM, *[HBM] * (2 * n), pl.BlockSpec(memory_space=pltpu.VMEM)),
        input_output_aliases={i: 2 + i for i in range(2 * n)},
        compiler_params=pltpu.CompilerParams(has_side_effects=EFFECT))(
            *[pltpu.with_memory_space_constraint(a, pltpu.HBM) for a in arrs], *after)


def _split_wait(started, after, pattern, *, name):
    send_sems, recv_sems, *arrs = started[:-1]
    n = len(arrs) // 2

    def body(*refs):
        for send, recv in pattern(refs[:n], refs[n:2 * n], refs[2 * n], refs[2 * n + 1]):
            send.wait_send()
            recv.wait_recv()

    outs = pl.pallas_call(
        body, name=name, out_shape=tuple(pltpu.HBM(a.shape, a.dtype) for a in arrs),
        in_specs=[HBM] * (2 * n) + [SEM, SEM, ANY], out_specs=tuple([HBM] * (2 * n)),
        input_output_aliases={i: i for i in range(2 * n)},
        compiler_params=pltpu.CompilerParams(has_side_effects=EFFECT))(*arrs, send_sems, recv_sems, after)
    return list(outs[n:])


def _adamw(parts, w, m, v, *, layer=None, prev=None, dep=(), name):
    P, R, C = parts.shape
    tr = next((t for t in (512, 256, 176, 128, 64, 32, 16, 8) if R % t == 0 and t * C <= 256 * 1024), R)
    c1 = 1.0 / (1.0 - ADAM_B1 ** ADAM_STEP)
    c2 = 1.0 / (1.0 - ADAM_B2 ** ADAM_STEP)
    nprev = 0 if prev is None else 4

    def body(p_ref, w_ref, m_ref, v_ref, *rest):
        g_ref, d_ref, nm_ref, nv_ref = rest[nprev + len(dep):]
        g = p_ref[0].astype(F32)
        for j in range(1, P):
            g = g + p_ref[j].astype(F32)
        g = g.reshape(w_ref.shape)
        nm = ADAM_B1 * m_ref[...] + (1.0 - ADAM_B1) * g
        nv = ADAM_B2 * v_ref[...] + (1.0 - ADAM_B2) * (g * g)
        g_ref[...] = g
        nm_ref[...] = nm
        nv_ref[...] = nv
        d_ref[...] = -ADAM_LR * ((nm * c1) / (jnp.sqrt(nv * c2) + ADAM_EPS) + ADAM_WD * w_ref[...])

    if layer is None:
        row = pl.BlockSpec((tr, C), lambda i: (i, 0))
    else:
        row = pl.BlockSpec((1, tr, C), lambda i: (layer, i, 0))
    out = _sds(w.shape, F32)
    return pl.pallas_call(
        body, name=name, grid=(R // tr,),
        in_specs=[pl.BlockSpec((P, tr, C), lambda i: (0, i, 0)), row, row, row] + [ANY] * (nprev + len(dep)),
        out_specs=[row, row, row, row], out_shape=[out, out, out, out],
        input_output_aliases={4 + k: k for k in range(nprev)},
        compiler_params=_params(("parallel",)))(parts, w, m, v, *(prev or ()), *dep)


BIG = ("w_in", "w_out", "xa_wq", "xa_wkv", "xa_wo", "ffn_w_up", "ffn_w_down")
COL_SHARDED = ("w_in", "xa_wkv", "ffn_w_up", "ml_conv_w", "ffn_conv_w")
SHARDED_SMALL = ("ml_conv_w", "ffn_conv_w")
REPLICATED = ("rel_bias", "ml_conv_b", "ml_i_bias", "ml_f_bias", "ml_norm_g", "swa_sinks", "ln1_g", "ln1_b", "ln2_g", "ln2_b",
              "ffn_conv_b", "ln3_g", "ln3_b")
NAMES = ("rel_bias", "w_in", "ml_conv_w", "ml_conv_b", "ml_i_bias", "ml_f_bias", "ml_norm_g", "swa_sinks", "w_out", "ln1_g", "ln1_b",
         "xa_wq", "xa_wkv", "xa_wo", "ln2_g", "ln2_b", "ffn_w_up", "ffn_conv_w", "ffn_conv_b", "ffn_w_down", "ln3_g", "ln3_b")


def _flat_rows(a, mult):
    f = a.reshape(-1)
    n = -(-f.shape[0] // (128 * mult)) * (128 * mult)
    if n != f.shape[0]:
        f = jnp.pad(f, (0, n - f.shape[0]))
    return f.reshape(-1, 128)


def _pack(arrs, mult):
    parts = [_flat_rows(a, mult) for a in arrs]
    return jnp.concatenate(parts, axis=0), [p.shape[0] for p in parts]


def _unpack(flat, rows, shapes):
    out, off = [], 0
    lead = flat.shape[:-2]
    for r, shp in zip(rows, shapes):
        n = int(np.prod(shp))
        piece = flat[..., off:off + r, :].reshape(lead + (r * 128,))[..., :n]
        out.append(piece.reshape(lead + tuple(shp)))
        off += r
    return out


def _full_from_shards(stacked, name):
    if name in COL_SHARDED:
        return jnp.moveaxis(stacked, 0, 2).reshape(stacked.shape[1], stacked.shape[2], N_DEV * stacked.shape[3])
    return jnp.moveaxis(stacked, 0, 1).reshape(stacked.shape[1], N_DEV * stacked.shape[2], stacked.shape[3])


def _shards_from_full(full, name):
    L, A, B = full.shape
    if name in COL_SHARDED:
        return jnp.moveaxis(full.reshape(L, A, N_DEV, B // N_DEV), 2, 0)
    return jnp.moveaxis(full.reshape(L, N_DEV, A // N_DEV, B), 1, 0)


def _pad_win(w):
    z = jnp.zeros(w.shape[:-1] + (NP_IN - N_IN,), w.dtype)
    return jnp.concatenate([w[..., :2048], w[..., 2056:], w[..., 2048:2056], z], axis=-1)


def _row128(v):
    return jnp.pad(v, (0, 128 - v.shape[0])).reshape(1, 128)


REST = BIG[1:]


def _layer_full(stacked, n):
    if n in COL_SHARDED:
        full = jnp.moveaxis(stacked, 0, 1).reshape(stacked.shape[1], N_DEV * stacked.shape[2])
    else:
        full = stacked.reshape(N_DEV * stacked.shape[1], stacked.shape[2])
    return _pad_win(full) if n == "w_in" else full


def _layer_shards(g, n):
    A, B = g.shape
    if n in COL_SHARDED:
        return jnp.moveaxis(g.reshape(A, N_DEV, B // N_DEV), 1, 0).astype(BF16)
    return g.reshape(N_DEV, A // N_DEV, B).astype(BF16)


def _with_own_block(a, idx, nblk):
    return lax.dynamic_update_slice(lax.empty((nblk,) + a.shape, a.dtype), a[None], (idx,) + (0,) * a.ndim)


def _gather_start(arrs, me, *, after=(), name):
    return _split_start(arrs, [_with_own_block(a, me, N_DEV) for a in arrs], _near_copies, 4 * len(arrs), after=after, name=name)


def _gather_finish(started, after, *, name):
    return _forward_sibling(_split_wait(started, after, _near_copies, name=name + "_wait"), name=name + "_forward")


def _reduce_start(tag, names, grads, me):
    send = [_layer_shards(g, n) for n, g in zip(names, grads)]
    lands = [_with_own_block(lax.dynamic_index_in_dim(s, me, 0, keepdims=False), me, N_DEV) for s in send]
    return _split_start(send, lands, _scatter_copies, (N_DEV - 1) * len(send), name=f"reduce_{tag}_start")


def _small_start(Gf, loss_part, me):
    sp_flat, sp_rows = _pack([Gf[n] for n in REPLICATED + SHARDED_SMALL] + [loss_part], 8)
    return _split_start([sp_flat], [_with_own_block(sp_flat, me, N_DEV)], _gather_copies, N_DEV - 1, name="gather_small_start"), sp_rows


def _small_finish(started, sp_rows, after, Gf, W, Mo, Vo, me):
    res = {}
    small = REPLICATED + SHARDED_SMALL
    sp_all = _split_wait(started, after, _gather_copies, name="gather_small_wait")[0]

    def widen(n, t):
        if n not in SHARDED_SMALL:
            return t[n]
        return lax.dynamic_update_slice(jnp.zeros(Gf[n].shape, F32), t[n], (0, 0, me * t[n].shape[2]))

    zl = jnp.zeros((8, 128), F32)
    wsm, _ = _pack([widen(n, W) for n in small] + [zl], 8)
    msm, _ = _pack([widen(n, Mo) for n in small] + [zl], 8)
    vsm, _ = _pack([widen(n, Vo) for n in small] + [zl], 8)
    outs_small = [_unpack(o_, sp_rows, [Gf[n].shape for n in small] + [(8, 128)])
                  for o_ in _adamw(sp_all, wsm, msm, vsm, name="adamw_small")]
    for kind, os_ in zip(("g", "d", "m", "v"), outs_small):
        for n, a in zip(small, os_[:-1]):
            if n in SHARDED_SMALL:
                a = lax.dynamic_slice(a, (0, 0, me * W[n].shape[2]), W[n].shape)
            res[kind, n] = a
    return res, outs_small[0][-1][0, 0]


def kernel(x, mem, rel_bias, w_in, ml_conv_w, ml_conv_b, ml_i_bias, ml_f_bias, ml_norm_g, swa_sinks, w_out, ln1_g, ln1_b, xa_wq, xa_wkv, xa_wo, ln2_g, ln2_b, ffn_w_up, ffn_conv_w, ffn_conv_b, ffn_w_down, ln3_g, ln3_b, loss_target, m_rel_bias, m_w_in, m_ml_conv_w, m_ml_conv_b, m_ml_i_bias, m_ml_f_bias, m_ml_norm_g, m_swa_sinks, m_w_out, m_ln1_g, m_ln1_b, m_xa_wq, m_xa_wkv, m_xa_wo, m_ln2_g, m_ln2_b, m_ffn_w_up, m_ffn_conv_w, m_ffn_conv_b, m_ffn_w_down, m_ln3_g, m_ln3_b, v_rel_bias, v_w_in, v_ml_conv_w, v_ml_conv_b, v_ml_i_bias, v_ml_f_bias, v_ml_norm_g, v_swa_sinks, v_w_out, v_ln1_g, v_ln1_b, v_xa_wq, v_xa_wkv, v_xa_wo, v_ln2_g, v_ln2_b, v_ffn_w_up, v_ffn_conv_w, v_ffn_conv_b, v_ffn_w_down, v_ln3_g, v_ln3_b):
    W = dict(rel_bias=rel_bias, w_in=w_in, ml_conv_w=ml_conv_w, ml_conv_b=ml_conv_b, ml_i_bias=ml_i_bias, ml_f_bias=ml_f_bias,
             ml_norm_g=ml_norm_g, swa_sinks=swa_sinks, w_out=w_out, ln1_g=ln1_g, ln1_b=ln1_b, xa_wq=xa_wq, xa_wkv=xa_wkv,
             xa_wo=xa_wo, ln2_g=ln2_g, ln2_b=ln2_b, ffn_w_up=ffn_w_up, ffn_conv_w=ffn_conv_w, ffn_conv_b=ffn_conv_b,
             ffn_w_down=ffn_w_down, ln3_g=ln3_g, ln3_b=ln3_b)
    Mo = dict(rel_bias=m_rel_bias, w_in=m_w_in, ml_conv_w=m_ml_conv_w, ml_conv_b=m_ml_conv_b, ml_i_bias=m_ml_i_bias,
              ml_f_bias=m_ml_f_bias, ml_norm_g=m_ml_norm_g, swa_sinks=m_swa_sinks, w_out=m_w_out, ln1_g=m_ln1_g, ln1_b=m_ln1_b,
              xa_wq=m_xa_wq, xa_wkv=m_xa_wkv, xa_wo=m_xa_wo, ln2_g=m_ln2_g, ln2_b=m_ln2_b, ffn_w_up=m_ffn_w_up,
              ffn_conv_w=m_ffn_conv_w, ffn_conv_b=m_ffn_conv_b, ffn_w_down=m_ffn_w_down, ln3_g=m_ln3_g, ln3_b=m_ln3_b)
    Vo = dict(rel_bias=v_rel_bias, w_in=v_w_in, ml_conv_w=v_ml_conv_w, ml_conv_b=v_ml_conv_b, ml_i_bias=v_ml_i_bias,
              ml_f_bias=v_ml_f_bias, ml_norm_g=v_ml_norm_g, swa_sinks=v_swa_sinks, w_out=v_w_out, ln1_g=v_ln1_g, ln1_b=v_ln1_b,
              xa_wq=v_xa_wq, xa_wkv=v_xa_wkv, xa_wo=v_xa_wo, ln2_g=v_ln2_g, ln2_b=v_ln2_b, ffn_w_up=v_ffn_w_up,
              ffn_conv_w=v_ffn_conv_w, ffn_conv_b=v_ffn_conv_b, ffn_w_down=v_ffn_w_down, ln3_g=v_ln3_g, ln3_b=v_ln3_b)
    S = x.shape[1]
    c_me = lax.axis_index("c")
    me = 4 * lax.axis_index("x") + 2 * lax.axis_index("y") + c_me
    xs = x.reshape(S, D)
    mems = mem.reshape(mem.shape[1], D)
    tgt = loss_target.reshape(S, D)


    sm_flat, sm_rows = _pack([W[n] for n in SHARDED_SMALL], 8)
    first = _gather([w_in[0].astype(BF16), sm_flat], name="gather_first")
    rest0 = _gather_start([W[n][0].astype(BF16) for n in REST], me, name="gather_rest0_start")
    full = [{"w_in": _layer_full(first[0], "w_in")}, None]
    conv_w = {n: _full_from_shards(s, n) for n, s in zip(SHARDED_SMALL, _unpack(first[1], sm_rows, [W[n].shape for n in SHARDED_SMALL]))}

    bucket = jnp.asarray(_t5_buckets())
    bias = _bias_table(rel_bias, bucket, name="bias_table")

    saved = []
    h0, h0b = xs, xs.astype(BF16)
    h0t = h0b.T
    for l in range(DEPTH):
        gbias = _row128(jnp.concatenate([ml_i_bias[l], ml_f_bias[l]]))
        sinks = _row128(swa_sinks[l])
        ng = ml_norm_g[l].reshape(1, ML_W)
        proj = _mm(h0b, full[l]["w_in"], tn=1024, dep=(rest0[-1],) if l == 0 else (), name=f"proj{l}")
        qk = _silu_conv_fwd(proj, conv_w["ml_conv_w"][l], ml_conv_b[l].reshape(1, -1), name=f"mlconv{l}")
        h_ml, cs, ns, ms = _mlstm_fwd(qk, proj, gbias, ng, name=f"mlstm{l}")
        h_sw = _swa_fwd(proj, bias, sinks, name=f"swa{l}")
        dep = ()
        if l == 0:
            landed = _gather_finish(rest0, h_sw, name="gather_rest0")
            full[0].update({n: _layer_full(s, n) for n, s in zip(REST, landed)})
            layer1 = _gather_start([W[n][1].astype(BF16) for n in BIG], me, after=(landed[0],), name="gather_layer1_start")
            dep = (layer1[-1],)
        fw = full[l]
        h1, h1b, h1t, z1 = _mm_res_ln([h_ml, h_sw], fw["w_out"], h0, ln1_g[l].reshape(1, D), ln1_b[l].reshape(1, D),
                                      name=f"mix_out{l}")
        kv = _mm(mems, fw["xa_wkv"], tm=256, dep=dep, name=f"xa_kv{l}")
        o = _xattn_fwd(h1b, fw["xa_wq"], kv, name=f"xattn{l}")
        h2, h2b, h2t, z2 = _mm_res_ln([o], fw["xa_wo"], h1, ln2_g[l].reshape(1, D), ln2_b[l].reshape(1, D), name=f"xa_out{l}")
        act, act_t = _ffn_gate_fwd(h2b, fw["ffn_w_up"], conv_w["ffn_conv_w"][l], ffn_conv_b[l].reshape(1, -1), name=f"ffn_gate{l}")
        h3, h3b, h3t, z3 = _mm_res_ln([act], fw["ffn_w_down"], h2, ln3_g[l].reshape(1, D), ln3_b[l].reshape(1, D),
                                      name=f"ffn_out{l}")
        saved.append(dict(h0t=h0t, proj=proj, qk=qk, cs=cs, ns=ns, ms=ms, h_ml=h_ml, h_sw=h_sw, z1=z1, h1b=h1b, h1t=h1t, kv=kv, o=o,
                          z2=z2, h2b=h2b, h2t=h2t, act_t=act_t, z3=z3, gbias=gbias, sinks=sinks, ng=ng))
        h0, h0b, h0t = h3, h3b, h3t
        if l == 0:
            landed = _gather_finish(layer1, h3b, name="gather_layer1")
            full[1] = {n: _layer_full(s, n) for n, s in zip(BIG, landed)}

    G = {n: [None] * DEPTH for n in NAMES if n != "rel_bias"}
    dbias = [None] * DEPTH
    pending = []
    dz3, dz3b, G["ln3_g"][DEPTH - 1], G["ln3_b"][DEPTH - 1], loss_part = _grad_in(
        [], None, ln=(saved[-1]["z3"], ln3_g[DEPTH - 1].reshape(1, D)), loss=(h0, tgt), name="loss_head")
    for l in reversed(range(DEPTH)):
        sv, fw = saved[l], full[l]
        win = fw["w_in"]
        G["ffn_w_down"][l] = _wgrad(sv["act_t"], dz3b, name=f"d_w_down{l}")
        dupg, dupv, G["ffn_conv_w"][l], G["ffn_conv_b"][l] = _ffn_gate_bwd(
            dz3b, fw["ffn_w_down"], sv["h2b"], fw["ffn_w_up"], conv_w["ffn_conv_w"][l], ffn_conv_b[l].reshape(1, -1),
            name=f"ffn_gate_bwd{l}")
        G["ffn_w_up"][l] = jnp.concatenate([_wgrad(sv["h2t"], dupg, name=f"d_w_up_g{l}"),
                                            _wgrad(sv["h2t"], dupv, name=f"d_w_up_v{l}")], axis=1)
        dep = ()
        if l == 0:
            names = ("ffn_w_up", "ffn_w_down")
            pending.append((names, 0, "ffn0", _reduce_start("ffn0", names, [G[n][0] for n in names], me)))
            dep = (pending[-1][3][-1],)
        dz2, dz2b, G["ln2_g"][l], G["ln2_b"][l] = _grad_in(
            [(dupg, fw["ffn_w_up"], 0), (dupv, fw["ffn_w_up"], 1)], dz3, ln=(sv["z2"], ln2_g[l].reshape(1, D)), dep=dep,
            name=f"d_h2_{l}")
        G["xa_wo"][l] = _mm_tn(sv["o"], dz2b, name=f"d_xa_wo{l}")
        dq, dkv = _xattn_bwd(dz2b, fw["xa_wo"], sv["h1b"], fw["xa_wq"], sv["kv"], name=f"xattn_bwd{l}")
        G["xa_wkv"][l] = _mm_tn(mems, dkv, name=f"d_xa_wkv{l}")
        G["xa_wq"][l] = _wgrad(sv["h1t"], dq, name=f"d_xa_wq{l}")
        dep = ()
        if l == 0:
            names = ("xa_wq", "xa_wkv", "xa_wo")
            pending.append((names, 0, "xa0", _reduce_start("xa0", names, [G[n][0] for n in names], me)))
            dep = (pending[-1][3][-1],)
        dz1, dz1b, G["ln1_g"][l], G["ln1_b"][l] = _grad_in(
            [(dq, fw["xa_wq"], 0)], dz2, ln=(sv["z1"], ln1_g[l].reshape(1, D)), dep=dep, name=f"d_h1_{l}")
        G["w_out"][l] = jnp.concatenate([_mm_tn(sv["h_ml"], dz1b, name=f"d_w_out_ml{l}"),
                                         _mm_tn(sv["h_sw"], dz1b, name=f"d_w_out_sw{l}")], axis=0)
        dep = ()
        if l == 0:
            pending.append((("w_out",), 0, "out0", _reduce_start("out0", ("w_out",), [G["w_out"][0]], me)))
            dep = (pending[-1][3][-1],)
        dhcat = _mm(dz1b, fw["w_out"], trans_b=True, dep=dep, name=f"d_hcat{l}")
        dsw, dbias[l], dsinks = _swa_bwd(dhcat, sv["proj"], bias, sv["sinks"], name=f"swa_bwd{l}")
        dqk, dml, dgb, dng = _mlstm_bwd(dhcat, sv["qk"], sv["proj"], sv["gbias"], sv["ng"], sv["cs"], sv["ns"], sv["ms"],
                                        name=f"mlstm_bwd{l}")
        dqk_pre, G["ml_conv_w"][l], G["ml_conv_b"][l] = _silu_conv_bwd(
            dqk, sv["proj"], conv_w["ml_conv_w"][l], ml_conv_b[l].reshape(1, -1), name=f"mlconv_bwd{l}")
        dw_qk = _wgrad(sv["h0t"], dqk_pre, name=f"d_w_in_qk{l}")
        dw_ml = _wgrad(sv["h0t"], dml, name=f"d_w_in_ml{l}")
        dw_sw = _wgrad(sv["h0t"], dsw, name=f"d_w_in_sw{l}")
        G["w_in"][l] = jnp.concatenate([dw_qk, dw_ml[:, :2 * ML_W + 2 * ML_H], dw_sw], axis=1)
        win_ml = jnp.concatenate([win[:, 1024:2048], win[:, 2816:2944]], axis=1)
        pairs = [(dqk_pre, win, 0), (dml, win_ml, 0), (dsw, win[:, 2048:2816], 0)]
        G["ml_i_bias"][l] = dgb[0, :ML_H]
        G["ml_f_bias"][l] = dgb[0, ML_H:2 * ML_H]
        G["ml_norm_g"][l] = dng
        G["swa_sinks"][l] = dsinks[0, :SW_H]
        if l > 0:
            pending.append((BIG, l, f"l{l}", _reduce_start(f"l{l}", BIG, [G[n][l] for n in BIG], me)))
            dz3, dz3b, G["ln3_g"][l - 1], G["ln3_b"][l - 1] = _grad_in(
                pairs, dz1, ln=(saved[l - 1]["z3"], ln3_g[l - 1].reshape(1, D)), dep=(pending[-1][3][-1],), name=f"d_h0_{l}")
        else:
            pending.append((("w_in",), 0, "in0", _reduce_start("in0", ("w_in",), [G["w_in"][0]], me)))
            dh = _grad_in(pairs, dz1, dep=(pending[-1][3][-1],), name="d_h0_0")
            grad_x = dh.reshape(x.shape)

    Gf = {n: jnp.stack([g.reshape(W[n].shape[1:]) if n in REPLICATED else g for g in G[n]]) for n in G if n not in BIG}
    Gf["rel_bias"] = _bias_table_bwd(dbias, bucket, name="bias_table_bwd")[:, :SW_H]
    small_started, sp_rows = _small_start(Gf, loss_part, me)

    parts = {}
    for names, l, tag, started in pending:
        landed = _split_wait(started, dh, _scatter_copies, name=f"reduce_{tag}_wait")
        parts.update({(n, l): p for n, p in zip(names, landed)})
    res = {}
    dep = (small_started[-1],)
    for n in BIG:
        outs = None
        for l in reversed(range(DEPTH)):
            outs = _adamw(parts[n, l], W[n], Mo[n], Vo[n], layer=l, prev=outs, dep=dep, name=f"adamw_{n}{l}")
            dep = ()
        for kind, a in zip(("g", "d", "m", "v"), outs):
            res[kind, n] = a
    res_small, loss = _small_finish(small_started, sp_rows, res["g", BIG[-1]], Gf, W, Mo, Vo, me)
    res.update(res_small)
    return (loss, grad_x, *[res["g", n] for n in NAMES], *[res["d", n] for n in NAMES], *[res["m", n] for n in NAMES],
            *[res["v", n] for n in NAMES])
```
